```python
import jax, jax.numpy as jnp
from jax import lax
import numpy as np

D_MODEL = 2048
BATCH = 4
SEQ = 2048
DEPTH = 1

D_MIX = D_MODEL
D_LRU = D_MIX // 2
D_RWKV = D_MIX - D_LRU
LRU_HEADS = 4
LRU_HEAD_DIM = D_LRU // LRU_HEADS
CONV_WIDTH = 4
LRU_C = 8.0
RWKV_HEAD_DIM = 64
RWKV_HEADS = D_RWKV // RWKV_HEAD_DIM
W_LORA = 64
A_LORA = 64
G_LORA = 160
RWKV_PROJ_W = 3 * D_RWKV + W_LORA + A_LORA + G_LORA
IN_W = 2 * D_LRU + RWKV_PROJ_W
D_FF = -(-8 * D_MODEL // (3 * 256)) * 256
N_MOD = 6
RMS_EPS = 1e-6
GN_EPS = 64e-5
L2_EPS = 1e-12

kernel_name = "hybrid_rglru_rwkv7_adaln_layer"

_IN_SPLITS = [D_LRU, 2 * D_LRU]
_RWKV_SPLITS = [int(s) for s in np.cumsum([D_RWKV, D_RWKV, D_RWKV, W_LORA, A_LORA])]


def rms_norm(x, g):
    xf = x.astype(jnp.float32)
    y = xf * lax.rsqrt(jnp.mean(xf * xf, axis=-1, keepdims=True) + RMS_EPS)
    return (y * g.astype(jnp.float32)).astype(x.dtype)


def modulate(h, shift, scale):
    return h * (1.0 + scale[:, None, :]) + shift[:, None, :]


def shift_prev(p):
    return jnp.pad(p[:, :-1], ((0, 0), (1, 0), (0, 0)))


def causal_depthwise_conv(u, w, b):
    S = u.shape[1]
    up = jnp.pad(u, ((0, 0), (CONV_WIDTH - 1, 0), (0, 0)))
    y = b
    for k in range(CONV_WIDTH):
        y = y + up[:, k:k + S] * w[k]
    return y


def _linear_scan_combine(c1, c2):
    a1, b1 = c1
    a2, b2 = c2
    return a1 * a2, a2 * b1 + b2


def rg_lru(u, wa, ba, wx, bx, lam):
    B, S, _ = u.shape
    uf = u.astype(jnp.float32)
    uh = uf.reshape(B, S, LRU_HEADS, LRU_HEAD_DIM)
    r = jax.nn.sigmoid(jnp.einsum('bshi,hij->bshj', uh, wa.astype(jnp.float32)).reshape(B, S, D_LRU) + ba)
    i = jax.nn.sigmoid(jnp.einsum('bshi,hij->bshj', uh, wx.astype(jnp.float32)).reshape(B, S, D_LRU) + bx)
    log_a = -LRU_C * r * jax.nn.softplus(-lam.astype(jnp.float32))
    a = jnp.exp(log_a)
    mult = jnp.sqrt(1.0 - jnp.exp(2.0 * log_a))
    is_first = (jnp.arange(S) == 0)[None, :, None]
    mult = jnp.where(is_first, 1.0, mult)
    b = mult * (i * uf)
    _, h = lax.associative_scan(_linear_scan_combine, (a, b), axis=1)
    return h.astype(u.dtype)


def rwkv7_recurrence(r, log_decay, k, v, kk, a):
    B, S, H, N = r.shape
    decay = jnp.exp(log_decay)

    def step(state, inp):
        r_t, d_t, k_t, v_t, kk_t, a_t = inp
        sa = jnp.einsum('bhvk,bhk->bhv', state, -kk_t)
        state = (state * d_t[:, :, None, :]
                 + sa[..., None] * (kk_t * a_t)[:, :, None, :]
                 + v_t[..., None] * k_t[:, :, None, :])
        y_t = jnp.einsum('bhvk,bhk->bhv', state, r_t)
        return state, y_t

    xs = tuple(jnp.moveaxis(t, 1, 0) for t in (r, decay, k, v, kk, a))
    s0 = jnp.zeros((B, H, N, N), jnp.float32)
    _, y = lax.scan(step, s0, xs)
    return jnp.moveaxis(y, 0, 1)


def rwkv7_mix(p, mu, w0, w2, a0, a2, g2, k_k, k_a, r_k, ln_g, ln_b):
    B, S, _ = p.shape
    dt = p.dtype
    p = p + (shift_prev(p) - p) * mu
    r, k, v, wl, al, gl = jnp.split(p, _RWKV_SPLITS, axis=-1)
    w = -jax.nn.softplus(-(w0 + jnp.tanh(wl) @ w2)) - 0.5
    log_decay = -jnp.exp(w.astype(jnp.float32))
    a = jax.nn.sigmoid(a0 + al @ a2)
    g = jax.nn.sigmoid(gl) @ g2
    hs = (B, S, RWKV_HEADS, RWKV_HEAD_DIM)
    kk = (k * k_k).astype(jnp.float32).reshape(hs)
    kk = kk / jnp.maximum(jnp.linalg.norm(kk, axis=-1, keepdims=True), L2_EPS)
    k = k * (1.0 + (a - 1.0) * k_a)
    rf = r.astype(jnp.float32).reshape(hs)
    kf = k.astype(jnp.float32).reshape(hs)
    vf = v.astype(jnp.float32).reshape(hs)
    af = a.astype(jnp.float32).reshape(hs)
    y = rwkv7_recurrence(rf, log_decay.reshape(hs), kf, vf, kk, af)
    mean = jnp.mean(y, axis=-1, keepdims=True)
    var = jnp.mean(jnp.square(y - mean), axis=-1, keepdims=True)
    y = ((y - mean) * lax.rsqrt(var + GN_EPS)).reshape(B, S, D_RWKV)
    y = y * ln_g.astype(jnp.float32) + ln_b.astype(jnp.float32)
    bonus = jnp.sum(rf * kf * r_k.astype(jnp.float32), axis=-1, keepdims=True) * vf
    y = y + bonus.reshape(B, S, D_RWKV)
    return (y * g.astype(jnp.float32)).astype(dt)


def setup_inputs(seed: int = 0) -> dict:
    key = jax.random.key(seed)
    ks = iter(jax.random.split(key, 32))
    nrm = lambda shape, s: jax.random.normal(next(ks), shape, jnp.float32) * s
    L = DEPTH
    u = jax.random.uniform(next(ks), (L, D_LRU), jnp.float32, 0.9, 0.999)
    base = u ** (1.0 / LRU_C)
    lru_lambda = jnp.log(base) - jnp.log1p(-base)
    return {
        "x": nrm((BATCH, SEQ, D_MODEL), 1.0),
        "c": nrm((BATCH, D_MODEL), 1.0),
        "w_ada": nrm((L, D_MODEL, N_MOD * D_MODEL), 0.5 * D_MODEL ** -0.5),
        "b_ada": nrm((L, N_MOD * D_MODEL), 0.02),
        "norm_mix_g": 1.0 + nrm((L, D_MODEL), 0.02),
        "w_in": nrm((L, D_MODEL, IN_W), D_MODEL ** -0.5),
        "conv_w": nrm((L, CONV_WIDTH, D_LRU), CONV_WIDTH ** -0.5),
        "conv_b": nrm((L, D_LRU), 0.01),
        "lru_wa": nrm((L, LRU_HEADS, LRU_HEAD_DIM, LRU_HEAD_DIM), LRU_HEAD_DIM ** -0.5),
        "lru_ba": nrm((L, D_LRU), 0.01),
        "lru_wx": nrm((L, LRU_HEADS, LRU_HEAD_DIM, LRU_HEAD_DIM), LRU_HEAD_DIM ** -0.5),
        "lru_bx": nrm((L, D_LRU), 0.01),
        "lru_lambda": lru_lambda,
        "rwkv_mu": jax.random.uniform(next(ks), (L, RWKV_PROJ_W), jnp.float32),
        "rwkv_w0": jax.random.uniform(next(ks), (L, D_RWKV), jnp.float32, -6.5, -1.5),
        "rwkv_w2": nrm((L, W_LORA, D_RWKV), W_LORA ** -0.5),
        "rwkv_a0": nrm((L, D_RWKV), 0.5),
        "rwkv_a2": nrm((L, A_LORA, D_RWKV), A_LORA ** -0.5),
        "rwkv_g2": nrm((L, G_LORA, D_RWKV), G_LORA ** -0.5),
        "rwkv_k_k": 0.85 + nrm((L, D_RWKV), 0.05),
        "rwkv_k_a": 1.0 + nrm((L, D_RWKV), 0.05),
        "rwkv_r_k": nrm((L, RWKV_HEADS, RWKV_HEAD_DIM), 0.1),
        "rwkv_ln_g": 1.0 + nrm((L, D_RWKV), 0.02),
        "rwkv_ln_b": nrm((L, D_RWKV), 0.01),
        "w_out": nrm((L, D_MIX, D_MODEL), D_MIX ** -0.5),
        "norm_ffn_g": 1.0 + nrm((L, D_MODEL), 0.02),
        "w_gu": nrm((L, D_MODEL, 2 * D_FF), D_MODEL ** -0.5),
        "w_down": nrm((L, D_FF, D_MODEL), D_FF ** -0.5),
        "final_norm_g": 1.0 + nrm((D_MODEL,), 0.02),
    }


def reference(x, c, w_ada, b_ada, norm_mix_g, w_in, conv_w, conv_b, lru_wa, lru_ba,
              lru_wx, lru_bx, lru_lambda, rwkv_mu, rwkv_w0, rwkv_w2, rwkv_a0, rwkv_a2,
              rwkv_g2, rwkv_k_k, rwkv_k_a, rwkv_r_k, rwkv_ln_g, rwkv_ln_b, w_out,
              norm_ffn_g, w_gu, w_down, final_norm_g):
    c_act = jax.nn.silu(c)
    for l in range(DEPTH):
        mod = c_act @ w_ada[l] + b_ada[l]
        sh_m, sc_m, g_m, sh_f, sc_f, g_f = jnp.split(mod, N_MOD, axis=-1)

        h = modulate(rms_norm(x, norm_mix_g[l]), sh_m, sc_m)
        p = h @ w_in[l]
        p_lru, p_gate, p_rwkv = jnp.split(p, _IN_SPLITS, axis=-1)
        u = causal_depthwise_conv(p_lru, conv_w[l], conv_b[l])
        y_a = rg_lru(u, lru_wa[l], lru_ba[l], lru_wx[l], lru_bx[l], lru_lambda[l]) * jax.nn.gelu(p_gate)
        y_b = rwkv7_mix(p_rwkv, rwkv_mu[l], rwkv_w0[l], rwkv_w2[l], rwkv_a0[l], rwkv_a2[l],
                        rwkv_g2[l], rwkv_k_k[l], rwkv_k_a[l], rwkv_r_k[l], rwkv_ln_g[l], rwkv_ln_b[l])
        mix = jnp.concatenate([y_a, y_b], axis=-1) @ w_out[l]
        x = x + g_m[:, None, :] * mix

        h = modulate(rms_norm(x, norm_ffn_g[l]), sh_f, sc_f)
        gate, up = jnp.split(h @ w_gu[l], 2, axis=-1)
        x = x + g_f[:, None, :] * ((jax.nn.silu(gate) * up) @ w_down[l])
    return rms_norm(x, final_norm_g)
```

```python
import functools

import jax
import jax.numpy as jnp
from jax import lax
from jax.experimental import pallas as pl
from jax.experimental.pallas import tpu as pltpu

F32 = jnp.float32
BF16 = jnp.bfloat16

LRU_HEADS = 4
CONV_WIDTH = 4
LRU_C = 8.0
HEAD = 64
CHUNK = 64
HEADS_PER_STEP = 4
RMS_EPS = 1e-6
GN_EPS = 64e-5
L2_EPS = 1e-12
LANE = 128
SUBLANE = 8
VMEM_LIMIT = 56 * 1024 * 1024


def _params(*sem):
    return pltpu.CompilerParams(dimension_semantics=sem, vmem_limit_bytes=VMEM_LIMIT)


_NN = (((1,), (0,)), ((), ()))
_NT = (((1,), (1,)), ((), ()))
_TN = (((0,), (0,)), ((), ()))


def _dg(a, b, dims):
    return lax.dot_general(a, b, dims, preferred_element_type=F32)


def _split(x):
    hi = x.astype(BF16)
    lo = (x - hi.astype(F32)).astype(BF16)
    return hi, lo


def _mm3(a, b, dims=_NN):
    ah, al = _split(a)
    bh, bl = _split(b)
    return _dg(ah, bh, dims) + (_dg(ah, bl, dims) + _dg(al, bh, dims))


def _mm2_exact_rhs(a, b_bf16):
    ah, al = _split(a)
    return _dg(ah, b_bf16, _NN) + _dg(al, b_bf16, _NN)


def _mm2_exact_lhs(a_bf16, b):
    bh, bl = _split(b)
    return _dg(a_bf16, bh, _NN) + _dg(a_bf16, bl, _NN)


def _softplus(x):
    return jnp.maximum(x, 0.0) + jnp.log1p(jnp.exp(-jnp.abs(x)))


def _iota2(shape):
    return (lax.broadcasted_iota(jnp.int32, shape, 0),
            lax.broadcasted_iota(jnp.int32, shape, 1))


def _head_ones(n):
    r, c = _iota2((n, n))
    return jnp.where((r // HEAD) == (c // HEAD), 1.0, 0.0).astype(BF16)


def _mod_body(c_ref, w_ref, b_ref, o_ref):
    c = c_ref[...]
    ca = c * jax.nn.sigmoid(c)
    o_ref[...] = _mm3(ca, w_ref[...]) + b_ref[...]


def _mod(c, w, b, tn=1024):
    bsz, d = c.shape
    n = w.shape[1]
    return pl.pallas_call(
        _mod_body,
        grid=(n // tn,),
        in_specs=[pl.BlockSpec((bsz, d), lambda j: (0, 0)),
                  pl.BlockSpec((d, tn), lambda j: (0, j)),
                  pl.BlockSpec((1, tn), lambda j: (0, j))],
        out_specs=pl.BlockSpec((bsz, tn), lambda j: (0, j)),
        out_shape=jax.ShapeDtypeStruct((bsz, n), F32),
        compiler_params=_params("parallel"),
        name="mod",
    )(c, w, b)


def _norm_mod(x, g, sh, sc):
    y = x * lax.rsqrt(jnp.mean(x * x, axis=-1, keepdims=True) + RMS_EPS) * g
    return y * (1.0 + sc) + sh


def _mm_in_body(x_ref, g_ref, sh_ref, sc_ref, w_ref, o_ref, h_ref):
    @pl.when(pl.program_id(1) == 0)
    def _():
        h_ref[...] = _norm_mod(x_ref[...], g_ref[...], sh_ref[0], sc_ref[0]).astype(BF16)

    o_ref[...] = jnp.dot(h_ref[...], w_ref[...], preferred_element_type=F32)


def _mm_in(x2, g, sh, sc, w, seq, tm=512, tn=1408):
    m, d = x2.shape
    n = w.shape[1]
    per_b = seq // tm
    return pl.pallas_call(
        _mm_in_body,
        grid=(m // tm, n // tn),
        in_specs=[pl.BlockSpec((tm, d), lambda i, j: (i, 0)),
                  pl.BlockSpec((1, d), lambda i, j: (0, 0)),
                  pl.BlockSpec((1, 1, d), lambda i, j: (i // per_b, 0, 0)),
                  pl.BlockSpec((1, 1, d), lambda i, j: (i // per_b, 0, 0)),
                  pl.BlockSpec((d, tn), lambda i, j: (0, j))],
        out_specs=pl.BlockSpec((tm, tn), lambda i, j: (i, j)),
        out_shape=jax.ShapeDtypeStruct((m, n), F32),
        scratch_shapes=[pltpu.VMEM((tm, d), BF16)],
        compiler_params=_params("parallel", "arbitrary"),
        name="mm_in",
    )(x2, g, sh, sc, w)


def _shift_rows(x, s, fill, row):
    return jnp.where(row < s, fill, pltpu.roll(x, s, 0))


def _lru_body(u_ref, gate_ref, halo_ref, cw_ref, cb_ref, wa_ref, wx_ref, ba_ref, bx_ref,
              lam_ref, o_ref, carry_ref, *, tt):
    ti = pl.program_id(1)
    first = ti == 0

    @pl.when(first)
    def _():
        carry_ref[...] = jnp.zeros_like(carry_ref)

    p = u_ref[...]
    dl = p.shape[1]
    halo = jnp.where(first, 0.0, halo_ref[...])
    ext = jnp.concatenate([halo, p], axis=0)
    cw = cw_ref[...]
    u = cb_ref[...] + p * cw[CONV_WIDTH - 1:CONV_WIDTH, :]
    for j in range(1, CONV_WIDTH):
        shifted = pltpu.roll(ext, j, 0)[SUBLANE:, :]
        u = u + shifted * cw[CONV_WIDTH - 1 - j:CONV_WIDTH - j, :]

    hd = dl // LRU_HEADS
    ub = u.astype(BF16)
    ra, rx = [], []
    for h in range(LRU_HEADS):
        uh = ub[:, h * hd:(h + 1) * hd]
        ra.append(jnp.dot(uh, wa_ref[h], preferred_element_type=F32))
        rx.append(jnp.dot(uh, wx_ref[h], preferred_element_type=F32))
    r = jax.nn.sigmoid(jnp.concatenate(ra, axis=1) + ba_ref[...])
    ig = jax.nn.sigmoid(jnp.concatenate(rx, axis=1) + bx_ref[...])
    log_a = (-LRU_C) * r * _softplus(-lam_ref[...])
    a = jnp.exp(log_a)
    mult = jnp.sqrt(1.0 - jnp.exp(2.0 * log_a))
    row = lax.broadcasted_iota(jnp.int32, (tt, dl), 0)
    mult = jnp.where(jnp.logical_and(first, row == 0), 1.0, mult)
    b = mult * (ig * u)

    s = 1
    while s < tt:
        a_s = _shift_rows(a, s, 1.0, row)
        b_s = _shift_rows(b, s, 0.0, row)
        b = a * b_s + b
        a = a * a_s
        s *= 2
    h = b + a * carry_ref[...]
    carry_ref[...] = h[tt - 1:tt, :]
    o_ref[...] = h * jax.nn.gelu(gate_ref[...])


def _lru(p, conv_w, conv_b, wa, wx, ba, bx, lam, bsz, seq, tt=256):
    dl = conv_w.shape[1]
    nt = seq // tt
    rows8 = tt // SUBLANE
    row = lambda v: v.reshape(1, dl)
    return pl.pallas_call(
        functools.partial(_lru_body, tt=tt),
        grid=(bsz, nt),
        in_specs=[pl.BlockSpec((tt, dl), lambda b, i: (b * nt + i, 0)),
                  pl.BlockSpec((tt, dl), lambda b, i: (b * nt + i, 1)),
                  pl.BlockSpec((SUBLANE, dl),
                               lambda b, i: (jnp.maximum((b * nt + i) * rows8 - 1, 0), 0)),
                  pl.BlockSpec((CONV_WIDTH, dl), lambda b, i: (0, 0)),
                  pl.BlockSpec((1, dl), lambda b, i: (0, 0)),
                  pl.BlockSpec(wa.shape, lambda b, i: (0, 0, 0)),
                  pl.BlockSpec(wx.shape, lambda b, i: (0, 0, 0)),
                  pl.BlockSpec((1, dl), lambda b, i: (0, 0)),
                  pl.BlockSpec((1, dl), lambda b, i: (0, 0)),
                  pl.BlockSpec((1, dl), lambda b, i: (0, 0))],
        out_specs=pl.BlockSpec((tt, dl), lambda b, i: (b * nt + i, 0)),
        out_shape=jax.ShapeDtypeStruct((bsz * seq, dl), F32),
        scratch_shapes=[pltpu.VMEM((1, dl), F32)],
        compiler_params=_params("parallel", "arbitrary"),
        name="lru",
    )(p, p, p, conv_w, row(conv_b), wa, wx, row(ba), row(bx), row(lam))


def _token_shift(x, halo, mu, first, row):
    prev = jnp.where(first, 0.0, halo[SUBLANE - 1:SUBLANE, :])
    xs = jnp.where(row == 0, prev, pltpu.roll(x, 1, 0))
    return x + (xs - x) * mu


def _tri_inverse(a_low, eye, r, c):
    base = 8
    d = jnp.where((r // base) == (c // base), a_low, 0.0)
    d2 = _mm3(d, d)
    d4 = _mm3(d2, d2)
    x = eye + d
    x = x + _mm3(d2, x)
    x = x + _mm3(d4, x)
    size = base
    while size < CHUNK:
        off = jnp.logical_and((r // (2 * size)) == (c // (2 * size)),
                              (r // size) != (c // size))
        o = jnp.where(off, a_low, 0.0)
        x = x + _mm3(x, _mm3(o, x))
        size *= 2
    return x


def _rwkv_a_body(r_ref, k_ref, v_ref, l_ref, rh_ref, kh_ref, vh_ref, lh_ref,
                 mur_ref, muk_ref, muv_ref, mul_ref, w0_ref, a0_ref, kkw_ref, kaw_ref, rkw_ref,
                 w2_ref, a2_ref, g2_ref,
                 rp_ref, yp_ref, m_ref, n_ref, bonus_ref, g_ref):
    first = pl.program_id(1) == 0
    cl = CHUNK
    width = HEADS_PER_STEP * HEAD
    row_w = lax.broadcasted_iota(jnp.int32, (cl, width), 0)
    row_l = lax.broadcasted_iota(jnp.int32, (cl, l_ref.shape[1]), 0)

    r = _token_shift(r_ref[...], rh_ref[...], mur_ref[...], first, row_w)
    k = _token_shift(k_ref[...], kh_ref[...], muk_ref[...], first, row_w)
    v = _token_shift(v_ref[...], vh_ref[...], muv_ref[...], first, row_w)
    lo = _token_shift(l_ref[...], lh_ref[...], mul_ref[...], first, row_l)
    wl = lo[:, 0:LANE]
    al = lo[:, LANE:2 * LANE]
    gl = lo[:, 2 * LANE:]

    w = -_softplus(-(w0_ref[...] + _mm3(jnp.tanh(wl), w2_ref[...]))) - 0.5
    lw = -jnp.exp(w)
    a = jax.nn.sigmoid(a0_ref[...] + _mm3(al, a2_ref[...]))
    g_ref[...] = _mm3(jax.nn.sigmoid(gl), g2_ref[...])

    ones_h = _head_ones(width)
    kk = k * kkw_ref[...]
    kk = kk / jnp.maximum(jnp.sqrt(_mm2_exact_rhs(kk * kk, ones_h)), L2_EPS)
    kp = k * (1.0 + (a - 1.0) * kaw_ref[...])
    bonus_ref[...] = _mm2_exact_rhs(r * kp * rkw_ref[...], ones_h) * v

    rc, cc = _iota2((cl, cl))
    tri_incl = jnp.where(rc >= cc, 1.0, 0.0).astype(BF16)
    lc = _mm2_exact_lhs(tri_incl, lw)
    p_incl = jnp.exp(lc)
    p_excl = jnp.exp(lc - lw)
    p_inv = jnp.exp(-lc)
    p_end = p_incl[cl - 1:cl, :]

    abar = -(kk * p_excl)
    bbar = kk * a * p_inv
    kbar = kp * p_inv
    rbar = r * p_incl
    btil = bbar * p_end
    ktil = kbar * p_end

    strict = rc > cc
    incl = rc >= cc
    eye = jnp.where(rc == cc, 1.0, 0.0)
    rps, yps, ms, ns = [], [], [], []
    for h in range(HEADS_PER_STEP):
        sl = slice(h * HEAD, (h + 1) * HEAD)
        ab_, bb_, kb_, rb_, v_ = abar[:, sl], bbar[:, sl], kbar[:, sl], rbar[:, sl], v[:, sl]
        a_ab = jnp.where(strict, _mm3(ab_, bb_, _NT), 0.0)
        a_ak = jnp.where(strict, _mm3(ab_, kb_, _NT), 0.0)
        a_rb = jnp.where(incl, _mm3(rb_, bb_, _NT), 0.0)
        a_rk = jnp.where(incl, _mm3(rb_, kb_, _NT), 0.0)
        t = _tri_inverse(a_ab, eye, rc, cc)
        wm = _mm3(t, ab_)
        u0 = _mm3(t, _mm3(a_ak, v_))
        rps.append(rb_ + _mm3(a_rb, wm))
        yps.append(_mm3(a_rb, u0) + _mm3(a_rk, v_))
        bt_ = btil[:, sl]
        ms.append(jnp.where(rc == cc, p_end[:, sl], 0.0) + _mm3(bt_, wm, _TN))
        ns.append(_mm3(bt_, u0, _TN) + _mm3(ktil[:, sl], v_, _TN))
    rp_ref[...] = jnp.concatenate(rps, axis=1)
    yp_ref[...] = jnp.concatenate(yps, axis=1)
    m_ref[...] = jnp.concatenate(ms, axis=1)
    n_ref[...] = jnp.concatenate(ns, axis=1)


def _rwkv_a(p, mu_rkv, mu_lora, w0, a0, k_k, k_a, r_k, w2p, a2p, g2p, bsz, seq, rkv_col0):
    cl = CHUNK
    width = HEADS_PER_STEP * HEAD
    dr = w0.shape[1]
    ngroups = dr // width
    nc = seq // cl
    lw_ = mu_lora.shape[1]
    cb0 = rkv_col0 // width
    lora_cb = (rkv_col0 + 3 * dr) // lw_
    rows8 = cl // SUBLANE

    def tile(cb_off):
        return pl.BlockSpec((cl, width), lambda b, i, q: (b * nc + i, cb0 + cb_off + q))

    def halo(cb_off):
        return pl.BlockSpec(
            (SUBLANE, width),
            lambda b, i, q: (jnp.maximum((b * nc + i) * rows8 - 1, 0), cb0 + cb_off + q))

    def prow(off=0):
        return pl.BlockSpec((1, width), lambda b, i, q: (0, off + q))

    out_tile = pl.BlockSpec((cl, width), lambda b, i, q: (b * nc + i, q))
    out_mat = pl.BlockSpec((HEAD, width), lambda b, i, q: (b * nc + i, q))
    act = jax.ShapeDtypeStruct((bsz * seq, dr), F32)
    mat = jax.ShapeDtypeStruct((bsz * nc * HEAD, dr), F32)
    return pl.pallas_call(
        _rwkv_a_body,
        grid=(bsz, nc, ngroups),
        in_specs=[tile(0), tile(ngroups), tile(2 * ngroups),
                  pl.BlockSpec((cl, lw_), lambda b, i, q: (b * nc + i, lora_cb)),
                  halo(0), halo(ngroups), halo(2 * ngroups),
                  pl.BlockSpec((SUBLANE, lw_),
                               lambda b, i, q: (jnp.maximum((b * nc + i) * rows8 - 1, 0), lora_cb)),
                  prow(0), prow(ngroups), prow(2 * ngroups),
                  pl.BlockSpec((1, lw_), lambda b, i, q: (0, 0)),
                  prow(), prow(), prow(), prow(), prow(),
                  pl.BlockSpec((w2p.shape[0], width), lambda b, i, q: (0, q)),
                  pl.BlockSpec((a2p.shape[0], width), lambda b, i, q: (0, q)),
                  pl.BlockSpec((g2p.shape[0], width), lambda b, i, q: (0, q))],
        out_specs=[out_tile, out_tile, out_mat, out_mat, out_tile, out_tile],
        out_shape=[act, act, mat, mat, act, act],
        compiler_params=_params("parallel", "parallel", "parallel"),
        name="rwkv_a",
    )(p, p, p, p, p, p, p, p, mu_rkv, mu_rkv, mu_rkv, mu_lora, w0, a0, k_k, k_a, r_k,
      w2p, a2p, g2p)


def _rwkv_b_body(rp_ref, yp_ref, m_ref, n_ref, bonus_ref, g_ref, lng_ref, lnb_ref,
                 o_ref, state_ref):
    @pl.when(pl.program_id(1) == 0)
    def _():
        state_ref[...] = jnp.zeros_like(state_ref)

    nheads = state_ref.shape[0]
    width = HEADS_PER_STEP * HEAD
    ones_h = _head_ones(width)
    inv_n = 1.0 / HEAD
    for q in range(nheads // HEADS_PER_STEP):
        ys = []
        for hh in range(HEADS_PER_STEP):
            h = q * HEADS_PER_STEP + hh
            sl = slice(h * HEAD, (h + 1) * HEAD)
            g0 = state_ref[h]
            ys.append(_mm3(rp_ref[:, sl], g0) + yp_ref[:, sl])
            state_ref[h] = _mm3(m_ref[:, sl], g0) + n_ref[:, sl]
        y = jnp.concatenate(ys, axis=1)
        ql = slice(q * width, (q + 1) * width)
        mean = _mm2_exact_rhs(y, ones_h) * inv_n
        yc = y - mean
        var = _mm2_exact_rhs(yc * yc, ones_h) * inv_n
        yn = yc * lax.rsqrt(var + GN_EPS) * lng_ref[:, ql] + lnb_ref[:, ql]
        o_ref[:, ql] = (yn + bonus_ref[:, ql]) * g_ref[:, ql]


def _rwkv_b(rp, yp, mc, nm, bonus, g, ln_g, ln_b, bsz, seq):
    cl = CHUNK
    dr = rp.shape[1]
    nc = seq // cl
    tile = pl.BlockSpec((cl, dr), lambda b, i: (b * nc + i, 0))
    mat = pl.BlockSpec((HEAD, dr), lambda b, i: (b * nc + i, 0))
    prow = pl.BlockSpec((1, dr), lambda b, i: (0, 0))
    return pl.pallas_call(
        _rwkv_b_body,
        grid=(bsz, nc),
        in_specs=[tile, tile, mat, mat, tile, tile, prow, prow],
        out_specs=tile,
        out_shape=jax.ShapeDtypeStruct((bsz * seq, dr), F32),
        scratch_shapes=[pltpu.VMEM((dr // HEAD, HEAD, HEAD), F32)],
        compiler_params=_params("parallel", "arbitrary"),
        name="rwkv_b",
    )(rp, yp, mc, nm, bonus, g, ln_g, ln_b)


def _mm_out_body(ya_ref, yb_ref, x_ref, gm_ref, w_ref, o_ref):
    da = ya_ref.shape[1]
    mix = (jnp.dot(ya_ref[...].astype(BF16), w_ref[:da, :], preferred_element_type=F32)
           + jnp.dot(yb_ref[...].astype(BF16), w_ref[da:, :], preferred_element_type=F32))
    o_ref[...] = x_ref[...] + gm_ref[0] * mix


def _mm_out(ya, yb, x2, gm, w, seq, tm=512):
    m, d = x2.shape
    per_b = seq // tm
    return pl.pallas_call(
        _mm_out_body,
        grid=(m // tm,),
        in_specs=[pl.BlockSpec((tm, ya.shape[1]), lambda i: (i, 0)),
                  pl.BlockSpec((tm, yb.shape[1]), lambda i: (i, 0)),
                  pl.BlockSpec((tm, d), lambda i: (i, 0)),
                  pl.BlockSpec((1, 1, d), lambda i: (i // per_b, 0, 0)),
                  pl.BlockSpec(w.shape, lambda i: (0, 0))],
        out_specs=pl.BlockSpec((tm, d), lambda i: (i, 0)),
        out_shape=jax.ShapeDtypeStruct((m, d), F32),
        compiler_params=_params("parallel"),
        name="mm_out",
    )(ya, yb, x2, gm, w)


def _ffn_body(x_ref, g_ref, sh_ref, sc_ref, gf_ref, wg_ref, wu_ref, wd_ref, fg_ref,
              o_ref, h_ref, acc_ref):
    f = pl.program_id(1)

    @pl.when(f == 0)
    def _():
        h_ref[...] = _norm_mod(x_ref[...], g_ref[...], sh_ref[0], sc_ref[0]).astype(BF16)
        acc_ref[...] = jnp.zeros_like(acc_ref)

    h = h_ref[...]
    gate = jnp.dot(h, wg_ref[...], preferred_element_type=F32)
    up = jnp.dot(h, wu_ref[...], preferred_element_type=F32)
    act = (gate * jax.nn.sigmoid(gate) * up).astype(BF16)
    acc_ref[...] += jnp.dot(act, wd_ref[...], preferred_element_type=F32)

    @pl.when(f == pl.num_programs(1) - 1)
    def _():
        y = x_ref[...] + gf_ref[0] * acc_ref[...]
        o_ref[...] = (y * lax.rsqrt(jnp.mean(y * y, axis=-1, keepdims=True) + RMS_EPS)
                      * fg_ref[...])


def _ffn(x1, g, sh, sc, gf, w_gu, w_down, fg, seq, tm=512, tf=512):
    m, d = x1.shape
    dff = w_down.shape[0]
    nf = dff // tf
    per_b = seq // tm
    brow = pl.BlockSpec((1, 1, d), lambda i, f: (i // per_b, 0, 0))
    prow = pl.BlockSpec((1, d), lambda i, f: (0, 0))
    return pl.pallas_call(
        _ffn_body,
        grid=(m // tm, nf),
        in_specs=[pl.BlockSpec((tm, d), lambda i, f: (i, 0)),
                  prow, brow, brow, brow,
                  pl.BlockSpec((d, tf), lambda i, f: (0, f)),
                  pl.BlockSpec((d, tf), lambda i, f: (0, nf + f)),
                  pl.BlockSpec((tf, d), lambda i, f: (f, 0)),
                  prow],
        out_specs=pl.BlockSpec((tm, d), lambda i, f: (i, 0)),
        out_shape=jax.ShapeDtypeStruct((m, d), F32),
        scratch_shapes=[pltpu.VMEM((tm, d), BF16), pltpu.VMEM((tm, d), F32)],
        compiler_params=_params("parallel", "arbitrary"),
        name="ffn",
    )(x1, g, sh, sc, gf, w_gu, w_gu, w_down, fg)


def _pad_cols(w, n):
    return jnp.pad(w, ((0, 0), (0, n - w.shape[1])))


def _pad_rows(w, n):
    return jnp.pad(w, ((0, n - w.shape[0]), (0, 0)))


def kernel(x, c, w_ada, b_ada, norm_mix_g, w_in, conv_w, conv_b, lru_wa, lru_ba, lru_wx, lru_bx, lru_lambda, rwkv_mu, rwkv_w0, rwkv_w2, rwkv_a0, rwkv_a2, rwkv_g2, rwkv_k_k, rwkv_k_a, rwkv_r_k, rwkv_ln_g, rwkv_ln_b, w_out, norm_ffn_g, w_gu, w_down, final_norm_g):
    bsz, seq, d = x.shape
    depth = w_ada.shape[0]
    dl = conv_w.shape[2]
    dr = rwkv_w0.shape[1]
    w_lora, a_lora, g_lora = rwkv_w2.shape[1], rwkv_a2.shape[1], rwkv_g2.shape[1]
    wpad, apad = LANE, LANE
    gpad = -(-g_lora // LANE) * LANE
    rkv_col0 = 2 * dl
    lora0 = rkv_col0 + 3 * dr

    x2 = x.reshape(bsz * seq, d)
    for l in range(depth):
        mod = _mod(c, w_ada[l], b_ada[l].reshape(1, -1))
        sh_m, sc_m, g_m, sh_f, sc_f, g_f = [t.reshape(bsz, 1, d) for t in jnp.split(mod, 6, axis=-1)]

        wi = w_in[l]
        o1, o2 = lora0 + w_lora, lora0 + w_lora + a_lora
        w_in_p = jnp.concatenate(
            [wi[:, :lora0], _pad_cols(wi[:, lora0:o1], wpad), _pad_cols(wi[:, o1:o2], apad),
             _pad_cols(wi[:, o2:], gpad)], axis=1).astype(BF16)
        mu = rwkv_mu[l].reshape(1, -1)
        mu_rkv = mu[:, :3 * dr]
        mu_lora = jnp.concatenate(
            [_pad_cols(mu[:, 3 * dr:3 * dr + w_lora], wpad),
             _pad_cols(mu[:, 3 * dr + w_lora:3 * dr + w_lora + a_lora], apad),
             _pad_cols(mu[:, 3 * dr + w_lora + a_lora:], gpad)], axis=1)
        w2p = _pad_rows(rwkv_w2[l], wpad)
        a2p = _pad_rows(rwkv_a2[l], apad)
        g2p = _pad_rows(rwkv_g2[l], gpad)

        p = _mm_in(x2, norm_mix_g[l].reshape(1, d), sh_m, sc_m, w_in_p, seq)

        y_a = _lru(p, conv_w[l], conv_b[l], lru_wa[l].astype(BF16), lru_wx[l].astype(BF16),
                   lru_ba[l], lru_bx[l], lru_lambda[l], bsz, seq)

        rowv = lambda t: t.reshape(1, dr)
        rp, yp, mc, nm, bonus, gg = _rwkv_a(
            p, mu_rkv, mu_lora, rowv(rwkv_w0[l]), rowv(rwkv_a0[l]), rowv(rwkv_k_k[l]),
            rowv(rwkv_k_a[l]), rowv(rwkv_r_k[l]), w2p, a2p, g2p, bsz, seq, rkv_col0)
        y_b = _rwkv_b(rp, yp, mc, nm, bonus, gg, rowv(rwkv_ln_g[l]), rowv(rwkv_ln_b[l]), bsz, seq)

        x2 = _mm_out(y_a, y_b, x2, g_m, w_out[l].astype(BF16), seq)

        last = l == depth - 1
        fg = final_norm_g.reshape(1, d) if last else None
        assert last, "only the final layer carries the closing RMSNorm"
        x2 = _ffn(x2, norm_ffn_g[l].reshape(1, d), sh_f, sc_f, g_f, w_gu[l].astype(BF16),
                  w_down[l].astype(BF16), fg, seq)
    return x2.reshape(bsz, seq, d)
```

```python
import functools

import jax
import jax.numpy as jnp
from jax import lax
from jax.experimental import pallas as pl
from jax.experimental.pallas import tpu as pltpu

F32 = jnp.float32
BF16 = jnp.bfloat16

LRU_HEADS = 4
CONV_WIDTH = 4
LRU_C = 8.0
HEAD = 64
CHUNK = 64
HEADS_PER_STEP = 4
CHUNKS_PER_STEP = 2
RMS_EPS = 1e-6
GN_EPS = 64e-5
L2_EPS = 1e-12
LANE = 128
SUBLANE = 8
VMEM_LIMIT = 56 * 1024 * 1024


def _params(*sem):
    return pltpu.CompilerParams(dimension_semantics=sem, vmem_limit_bytes=VMEM_LIMIT)


_NN = (((1,), (0,)), ((), ()))
_NT = (((1,), (1,)), ((), ()))
_TN = (((0,), (0,)), ((), ()))


def _dg(a, b, dims):
    return lax.dot_general(a, b, dims, preferred_element_type=F32)


def _split(x):
    hi = x.astype(BF16)
    lo = (x - hi.astype(F32)).astype(BF16)
    return hi, lo


def _mm3(a, b, dims=_NN):
    ah, al = _split(a)
    bh, bl = _split(b)
    return _dg(ah, bh, dims) + (_dg(ah, bl, dims) + _dg(al, bh, dims))


def _mm2_exact_rhs(a, b_bf16):
    ah, al = _split(a)
    return _dg(ah, b_bf16, _NN) + _dg(al, b_bf16, _NN)


def _mm2_exact_lhs(a_bf16, b):
    bh, bl = _split(b)
    return _dg(a_bf16, bh, _NN) + _dg(a_bf16, bl, _NN)


def _softplus(x):
    return jnp.maximum(x, 0.0) + jnp.log1p(jnp.exp(-jnp.abs(x)))


def _iota2(shape):
    return (lax.broadcasted_iota(jnp.int32, shape, 0),
            lax.broadcasted_iota(jnp.int32, shape, 1))


def _head_ones(n):
    r, c = _iota2((n, n))
    return jnp.where((r // HEAD) == (c // HEAD), 1.0, 0.0).astype(BF16)


def _mod_body(c_ref, w_ref, b_ref, o_ref):
    c = c_ref[...]
    ca = c * jax.nn.sigmoid(c)
    o_ref[...] = _mm3(ca, w_ref[...]) + b_ref[...]


def _mod(c, w, b, tn=1024):
    bsz, d = c.shape
    n = w.shape[1]
    return pl.pallas_call(
        _mod_body,
        grid=(n // tn,),
        in_specs=[pl.BlockSpec((bsz, d), lambda j: (0, 0)),
                  pl.BlockSpec((d, tn), lambda j: (0, j)),
                  pl.BlockSpec((1, tn), lambda j: (0, j))],
        out_specs=pl.BlockSpec((bsz, tn), lambda j: (0, j)),
        out_shape=jax.ShapeDtypeStruct((bsz, n), F32),
        compiler_params=_params("parallel"),
        name="mod",
    )(c, w, b)


def _norm_mod(x, g, sh, sc):
    y = x * lax.rsqrt(jnp.mean(x * x, axis=-1, keepdims=True) + RMS_EPS) * g
    return y * (1.0 + sc) + sh


def _mm_in_body(x_ref, g_ref, sh_ref, sc_ref, w_ref, o_ref, h_ref):
    @pl.when(pl.program_id(1) == 0)
    def _():
        h_ref[...] = _norm_mod(x_ref[...], g_ref[...], sh_ref[0], sc_ref[0]).astype(BF16)

    o_ref[...] = jnp.dot(h_ref[...], w_ref[...], preferred_element_type=F32)


def _mm_in(x2, g, sh, sc, w, seq, tm=512, tn=1408):
    m, d = x2.shape
    n = w.shape[1]
    per_b = seq // tm
    return pl.pallas_call(
        _mm_in_body,
        grid=(m // tm, n // tn),
        in_specs=[pl.BlockSpec((tm, d), lambda i, j: (i, 0)),
                  pl.BlockSpec((1, d), lambda i, j: (0, 0)),
                  pl.BlockSpec((1, 1, d), lambda i, j: (i // per_b, 0, 0)),
                  pl.BlockSpec((1, 1, d), lambda i, j: (i // per_b, 0, 0)),
                  pl.BlockSpec((d, tn), lambda i, j: (0, j))],
        out_specs=pl.BlockSpec((tm, tn), lambda i, j: (i, j)),
        out_shape=jax.ShapeDtypeStruct((m, n), F32),
        scratch_shapes=[pltpu.VMEM((tm, d), BF16)],
        compiler_params=_params("parallel", "arbitrary"),
        name="mm_in",
    )(x2, g, sh, sc, w)


def _shift_rows(x, s, fill, row):
    return jnp.where(row < s, fill, pltpu.roll(x, s, 0))


def _lru_body(u_ref, gate_ref, halo_ref, cw_ref, cb_ref, wa_ref, wx_ref, ba_ref, bx_ref,
              lam_ref, o_ref, carry_ref, *, tt):
    ti = pl.program_id(1)
    first = ti == 0

    @pl.when(first)
    def _():
        carry_ref[...] = jnp.zeros_like(carry_ref)

    p = u_ref[...]
    dl = p.shape[1]
    halo = jnp.where(first, 0.0, halo_ref[...])
    ext = jnp.concatenate([halo, p], axis=0)
    cw = cw_ref[...]
    u = cb_ref[...] + p * cw[CONV_WIDTH - 1:CONV_WIDTH, :]
    for j in range(1, CONV_WIDTH):
        shifted = pltpu.roll(ext, j, 0)[SUBLANE:, :]
        u = u + shifted * cw[CONV_WIDTH - 1 - j:CONV_WIDTH - j, :]

    hd = dl // LRU_HEADS
    ub = u.astype(BF16)
    ra, rx = [], []
    for h in range(LRU_HEADS):
        uh = ub[:, h * hd:(h + 1) * hd]
        ra.append(jnp.dot(uh, wa_ref[h], preferred_element_type=F32))
        rx.append(jnp.dot(uh, wx_ref[h], preferred_element_type=F32))
    r = jax.nn.sigmoid(jnp.concatenate(ra, axis=1) + ba_ref[...])
    ig = jax.nn.sigmoid(jnp.concatenate(rx, axis=1) + bx_ref[...])
    log_a = (-LRU_C) * r * _softplus(-lam_ref[...])
    a = jnp.exp(log_a)
    mult = jnp.sqrt(1.0 - jnp.exp(2.0 * log_a))
    row = lax.broadcasted_iota(jnp.int32, (tt, dl), 0)
    mult = jnp.where(jnp.logical_and(first, row == 0), 1.0, mult)
    b = mult * (ig * u)

    s = 1
    while s < tt:
        a_s = _shift_rows(a, s, 1.0, row)
        b_s = _shift_rows(b, s, 0.0, row)
        b = a * b_s + b
        a = a * a_s
        s *= 2
    h = b + a * carry_ref[...]
    carry_ref[...] = h[tt - 1:tt, :]
    o_ref[...] = h * jax.nn.gelu(gate_ref[...])


def _lru(p, conv_w, conv_b, wa, wx, ba, bx, lam, bsz, seq, tt=256):
    dl = conv_w.shape[1]
    nt = seq // tt
    rows8 = tt // SUBLANE
    row = lambda v: v.reshape(1, dl)
    return pl.pallas_call(
        functools.partial(_lru_body, tt=tt),
        grid=(bsz, nt),
        in_specs=[pl.BlockSpec((tt, dl), lambda b, i: (b * nt + i, 0)),
                  pl.BlockSpec((tt, dl), lambda b, i: (b * nt + i, 1)),
                  pl.BlockSpec((SUBLANE, dl),
                               lambda b, i: (jnp.maximum((b * nt + i) * rows8 - 1, 0), 0)),
                  pl.BlockSpec((CONV_WIDTH, dl), lambda b, i: (0, 0)),
                  pl.BlockSpec((1, dl), lambda b, i: (0, 0)),
                  pl.BlockSpec(wa.shape, lambda b, i: (0, 0, 0)),
                  pl.BlockSpec(wx.shape, lambda b, i: (0, 0, 0)),
                  pl.BlockSpec((1, dl), lambda b, i: (0, 0)),
                  pl.BlockSpec((1, dl), lambda b, i: (0, 0)),
                  pl.BlockSpec((1, dl), lambda b, i: (0, 0))],
        out_specs=pl.BlockSpec((tt, dl), lambda b, i: (b * nt + i, 0)),
        out_shape=jax.ShapeDtypeStruct((bsz * seq, dl), F32),
        scratch_shapes=[pltpu.VMEM((1, dl), F32)],
        compiler_params=_params("parallel", "arbitrary"),
        name="lru",
    )(p, p, p, conv_w, row(conv_b), wa, wx, row(ba), row(bx), row(lam))


def _token_shift(x, halo, mu, first, row):
    prev = jnp.where(first, 0.0, halo[SUBLANE - 1:SUBLANE, :])
    xs = jnp.where(row == 0, prev, pltpu.roll(x, 1, 0))
    return x + (xs - x) * mu


def _mm1(a, b, dims=_NN):
    return _dg(a.astype(BF16), b.astype(BF16), dims)


def _tri_inverse(a_low, eye, r, c):
    base = 8
    same = (r // base) == (c // base)
    d = [jnp.where(same, a, 0.0).astype(BF16) for a in a_low]
    d2 = [_mm1(t, t) for t in d]
    x = [eye + t.astype(F32) for t in d]
    x = [xi + _mm1(t2, xi) for xi, t2 in zip(x, d2)]
    d4 = [_mm1(t2, t2) for t2 in d2]
    x = [xi + _mm1(t4, xi) for xi, t4 in zip(x, d4)]
    size = base
    while size < CHUNK:
        off = jnp.logical_and((r // (2 * size)) == (c // (2 * size)),
                              (r // size) != (c // size))
        o = [jnp.where(off, a, 0.0).astype(BF16) for a in a_low]
        ox = [_mm1(oi, xi) for oi, xi in zip(o, x)]
        x = [xi + _mm1(xi, oxi) for xi, oxi in zip(x, ox)]
        size *= 2
    return x


def _rwkv_a_body(r_ref, k_ref, v_ref, l_ref, rh_ref, kh_ref, vh_ref, lh_ref,
                 mur_ref, muk_ref, muv_ref, mul_ref, w0_ref, a0_ref, kkw_ref, kaw_ref, rkw_ref,
                 w2_ref, a2_ref, g2_ref,
                 rp_ref, yp_ref, m_ref, n_ref, bonus_ref, g_ref):
    first = pl.program_id(1) == 0
    cl = CHUNK
    rows = CHUNKS_PER_STEP * cl
    width = HEADS_PER_STEP * HEAD
    row_w = lax.broadcasted_iota(jnp.int32, (rows, width), 0)
    row_l = lax.broadcasted_iota(jnp.int32, (rows, l_ref.shape[1]), 0)

    r = _token_shift(r_ref[...], rh_ref[...], mur_ref[...], first, row_w)
    k = _token_shift(k_ref[...], kh_ref[...], muk_ref[...], first, row_w)
    v = _token_shift(v_ref[...], vh_ref[...], muv_ref[...], first, row_w)
    lo = _token_shift(l_ref[...], lh_ref[...], mul_ref[...], first, row_l)
    wl = lo[:, 0:LANE]
    al = lo[:, LANE:2 * LANE]
    gl = lo[:, 2 * LANE:]

    w = -_softplus(-(w0_ref[...] + _mm3(jnp.tanh(wl), w2_ref[...]))) - 0.5
    lw = -jnp.exp(w)
    a = jax.nn.sigmoid(a0_ref[...] + _mm3(al, a2_ref[...]))
    g_ref[...] = _mm3(jax.nn.sigmoid(gl), g2_ref[...])

    ones_h = _head_ones(width)
    kk = k * kkw_ref[...]
    kk = kk / jnp.maximum(jnp.sqrt(_mm2_exact_rhs(kk * kk, ones_h)), L2_EPS)
    kp = k * (1.0 + (a - 1.0) * kaw_ref[...])
    bonus_ref[...] = _mm2_exact_rhs(r * kp * rkw_ref[...], ones_h) * v

    rt, ct = _iota2((rows, rows))
    tri_incl = jnp.where(jnp.logical_and(rt >= ct, (rt // cl) == (ct // cl)), 1.0, 0.0)
    lc = _mm2_exact_lhs(tri_incl.astype(BF16), lw)
    p_incl = jnp.exp(lc)
    p_excl = jnp.exp(lc - lw)
    p_inv = jnp.exp(-lc)
    p_end = jnp.concatenate(
        [jnp.broadcast_to(p_incl[(j + 1) * cl - 1:(j + 1) * cl, :], (cl, width))
         for j in range(CHUNKS_PER_STEP)], axis=0)

    abar = -(kk * p_excl)
    bbar = kk * a * p_inv
    kbar = kp * p_inv
    rbar = r * p_incl
    btil = bbar * p_end
    ktil = kbar * p_end

    units = [(j, h) for j in range(CHUNKS_PER_STEP) for h in range(HEADS_PER_STEP)]

    def cut(x, dtype=BF16):
        return [x[j * cl:(j + 1) * cl, h * HEAD:(h + 1) * HEAD].astype(dtype) for j, h in units]

    ab_, bb_, kb_, rb_, v_ = cut(abar), cut(bbar), cut(kbar), cut(rbar), cut(v)
    bt_, kt_ = cut(btil), cut(ktil)
    rb32, pe32 = cut(rbar, F32), cut(p_end, F32)

    rc, cc = _iota2((cl, cl))
    strict = rc > cc
    incl = rc >= cc
    diag = rc == cc
    eye = jnp.where(diag, 1.0, 0.0)
    a_ab = [jnp.where(strict, _dg(x, y, _NT), 0.0) for x, y in zip(ab_, bb_)]
    a_ak = [jnp.where(strict, _dg(x, y, _NT), 0.0).astype(BF16) for x, y in zip(ab_, kb_)]
    a_rb = [jnp.where(incl, _dg(x, y, _NT), 0.0).astype(BF16) for x, y in zip(rb_, bb_)]
    a_rk = [jnp.where(incl, _dg(x, y, _NT), 0.0).astype(BF16) for x, y in zip(rb_, kb_)]
    akv = [_dg(x, y, _NN).astype(BF16) for x, y in zip(a_ak, v_)]
    t = [x.astype(BF16) for x in _tri_inverse(a_ab, eye, rc, cc)]
    wm = [_dg(x, y, _NN).astype(BF16) for x, y in zip(t, ab_)]
    u0 = [_dg(x, y, _NN).astype(BF16) for x, y in zip(t, akv)]
    rps = [x + _dg(y, z, _NN) for x, y, z in zip(rb32, a_rb, wm)]
    yps = [_dg(x, y, _NN) + _dg(z, q, _NN) for x, y, z, q in zip(a_rb, u0, a_rk, v_)]
    ms = [jnp.where(diag, x, 0.0) + _dg(y, z, _TN) for x, y, z in zip(pe32, bt_, wm)]
    ns = [_dg(x, y, _TN) + _dg(z, q, _TN) for x, y, z, q in zip(bt_, u0, kt_, v_)]
    for j in range(CHUNKS_PER_STEP):
        rs = slice(j * cl, (j + 1) * cl)
        us = slice(j * HEADS_PER_STEP, (j + 1) * HEADS_PER_STEP)
        rp_ref[rs, :] = jnp.concatenate(rps[us], axis=1)
        yp_ref[rs, :] = jnp.concatenate(yps[us], axis=1)
        m_ref[rs, :] = jnp.concatenate(ms[us], axis=1)
        n_ref[rs, :] = jnp.concatenate(ns[us], axis=1)


def _rwkv_a(p, mu_rkv, mu_lora, w0, a0, k_k, k_a, r_k, w2p, a2p, g2p, bsz, seq, rkv_col0):
    cl = CHUNKS_PER_STEP * CHUNK
    width = HEADS_PER_STEP * HEAD
    dr = w0.shape[1]
    ngroups = dr // width
    nc = seq // cl
    lw_ = mu_lora.shape[1]
    cb0 = rkv_col0 // width
    lora_cb = (rkv_col0 + 3 * dr) // lw_
    rows8 = cl // SUBLANE

    def tile(cb_off):
        return pl.BlockSpec((cl, width), lambda b, i, q: (b * nc + i, cb0 + cb_off + q))

    def halo(cb_off):
        return pl.BlockSpec(
            (SUBLANE, width),
            lambda b, i, q: (jnp.maximum((b * nc + i) * rows8 - 1, 0), cb0 + cb_off + q))

    def prow(off=0):
        return pl.BlockSpec((1, width), lambda b, i, q: (0, off + q))

    out_tile = pl.BlockSpec((cl, width), lambda b, i, q: (b * nc + i, q))
    out_mat = pl.BlockSpec((CHUNKS_PER_STEP * HEAD, width), lambda b, i, q: (b * nc + i, q))
    act = jax.ShapeDtypeStruct((bsz * seq, dr), F32)
    mat = jax.ShapeDtypeStruct((bsz * (seq // CHUNK) * HEAD, dr), F32)
    return pl.pallas_call(
        _rwkv_a_body,
        grid=(bsz, nc, ngroups),
        in_specs=[tile(0), tile(ngroups), tile(2 * ngroups),
                  pl.BlockSpec((cl, lw_), lambda b, i, q: (b * nc + i, lora_cb)),
                  halo(0), halo(ngroups), halo(2 * ngroups),
                  pl.BlockSpec((SUBLANE, lw_),
                               lambda b, i, q: (jnp.maximum((b * nc + i) * rows8 - 1, 0), lora_cb)),
                  prow(0), prow(ngroups), prow(2 * ngroups),
                  pl.BlockSpec((1, lw_), lambda b, i, q: (0, 0)),
                  prow(), prow(), prow(), prow(), prow(),
                  pl.BlockSpec((w2p.shape[0], width), lambda b, i, q: (0, q)),
                  pl.BlockSpec((a2p.shape[0], width), lambda b, i, q: (0, q)),
                  pl.BlockSpec((g2p.shape[0], width), lambda b, i, q: (0, q))],
        out_specs=[out_tile, out_tile, out_mat, out_mat, out_tile, out_tile],
        out_shape=[act, act, mat, mat, act, act],
        compiler_params=_params("parallel", "parallel", "parallel"),
        name="rwkv_a",
    )(p, p, p, p, p, p, p, p, mu_rkv, mu_rkv, mu_rkv, mu_lora, w0, a0, k_k, k_a, r_k,
      w2p, a2p, g2p)


def _rwkv_b_body(rp_ref, yp_ref, m_ref, n_ref, bonus_ref, g_ref, lng_ref, lnb_ref,
                 o_ref, state_ref):
    @pl.when(pl.program_id(1) == 0)
    def _():
        state_ref[...] = jnp.zeros_like(state_ref)

    nheads = state_ref.shape[0]
    width = HEADS_PER_STEP * HEAD
    ones_h = _head_ones(width)
    inv_n = 1.0 / HEAD
    for q in range(nheads // HEADS_PER_STEP):
        ys = []
        for hh in range(HEADS_PER_STEP):
            h = q * HEADS_PER_STEP + hh
            sl = slice(h * HEAD, (h + 1) * HEAD)
            g0 = state_ref[h]
            ys.append(_mm1(rp_ref[:, sl], g0) + yp_ref[:, sl])
            state_ref[h] = _mm3(m_ref[:, sl], g0) + n_ref[:, sl]
        y = jnp.concatenate(ys, axis=1)
        ql = slice(q * width, (q + 1) * width)
        mean = _mm2_exact_rhs(y, ones_h) * inv_n
        yc = y - mean
        var = _mm2_exact_rhs(yc * yc, ones_h) * inv_n
        yn = yc * lax.rsqrt(var + GN_EPS) * lng_ref[:, ql] + lnb_ref[:, ql]
        o_ref[:, ql] = (yn + bonus_ref[:, ql]) * g_ref[:, ql]


def _rwkv_b(rp, yp, mc, nm, bonus, g, ln_g, ln_b, bsz, seq):
    cl = CHUNK
    dr = rp.shape[1]
    nc = seq // cl
    tile = pl.BlockSpec((cl, dr), lambda b, i: (b * nc + i, 0))
    mat = pl.BlockSpec((HEAD, dr), lambda b, i: (b * nc + i, 0))
    prow = pl.BlockSpec((1, dr), lambda b, i: (0, 0))
    return pl.pallas_call(
        _rwkv_b_body,
        grid=(bsz, nc),
        in_specs=[tile, tile, mat, mat, tile, tile, prow, prow],
        out_specs=tile,
        out_shape=jax.ShapeDtypeStruct((bsz * seq, dr), F32),
        scratch_shapes=[pltpu.VMEM((dr // HEAD, HEAD, HEAD), F32)],
        compiler_params=_params("parallel", "arbitrary"),
        name="rwkv_b",
    )(rp, yp, mc, nm, bonus, g, ln_g, ln_b)


def _mm_out_body(ya_ref, yb_ref, x_ref, gm_ref, w_ref, o_ref):
    da = ya_ref.shape[1]
    mix = (jnp.dot(ya_ref[...].astype(BF16), w_ref[:da, :], preferred_element_type=F32)
           + jnp.dot(yb_ref[...].astype(BF16), w_ref[da:, :], preferred_element_type=F32))
    o_ref[...] = x_ref[...] + gm_ref[0] * mix


def _mm_out(ya, yb, x2, gm, w, seq, tm=512):
    m, d = x2.shape
    per_b = seq // tm
    return pl.pallas_call(
        _mm_out_body,
        grid=(m // tm,),
        in_specs=[pl.BlockSpec((tm, ya.shape[1]), lambda i: (i, 0)),
                  pl.BlockSpec((tm, yb.shape[1]), lambda i: (i, 0)),
                  pl.BlockSpec((tm, d), lambda i: (i, 0)),
                  pl.BlockSpec((1, 1, d), lambda i: (i // per_b, 0, 0)),
                  pl.BlockSpec(w.shape, lambda i: (0, 0))],
        out_specs=pl.BlockSpec((tm, d), lambda i: (i, 0)),
        out_shape=jax.ShapeDtypeStruct((m, d), F32),
        compiler_params=_params("parallel"),
        name="mm_out",
    )(ya, yb, x2, gm, w)


def _ffn_body(x_ref, g_ref, sh_ref, sc_ref, gf_ref, wg_ref, wu_ref, wd_ref, fg_ref,
              o_ref, h_ref, acc_ref):
    f = pl.program_id(1)

    @pl.when(f == 0)
    def _():
        h_ref[...] = _norm_mod(x_ref[...], g_ref[...], sh_ref[0], sc_ref[0]).astype(BF16)
        acc_ref[...] = jnp.zeros_like(acc_ref)

    h = h_ref[...]
    gate = jnp.dot(h, wg_ref[...], preferred_element_type=F32)
    up = jnp.dot(h, wu_ref[...], preferred_element_type=F32)
    act = (gate * jax.nn.sigmoid(gate) * up).astype(BF16)
    acc_ref[...] += jnp.dot(act, wd_ref[...], preferred_element_type=F32)

    @pl.when(f == pl.num_programs(1) - 1)
    def _():
        y = x_ref[...] + gf_ref[0] * acc_ref[...]
        o_ref[...] = (y * lax.rsqrt(jnp.mean(y * y, axis=-1, keepdims=True) + RMS_EPS)
                      * fg_ref[...])


def _ffn(x1, g, sh, sc, gf, w_gu, w_down, fg, seq, tm=512, tf=512):
    m, d = x1.shape
    dff = w_down.shape[0]
    nf = dff // tf
    per_b = seq // tm
    brow = pl.BlockSpec((1, 1, d), lambda i, f: (i // per_b, 0, 0))
    prow = pl.BlockSpec((1, d), lambda i, f: (0, 0))
    return pl.pallas_call(
        _ffn_body,
        grid=(m // tm, nf),
        in_specs=[pl.BlockSpec((tm, d), lambda i, f: (i, 0)),
                  prow, brow, brow, brow,
                  pl.BlockSpec((d, tf), lambda i, f: (0, f)),
                  pl.BlockSpec((d, tf), lambda i, f: (0, nf + f)),
                  pl.BlockSpec((tf, d), lambda i, f: (f, 0)),
                  prow],
        out_specs=pl.BlockSpec((tm, d), lambda i, f: (i, 0)),
        out_shape=jax.ShapeDtypeStruct((m, d), F32),
        scratch_shapes=[pltpu.VMEM((tm, d), BF16), pltpu.VMEM((tm, d), F32)],
        compiler_params=_params("parallel", "arbitrary"),
        name="ffn",
    )(x1, g, sh, sc, gf, w_gu, w_gu, w_down, fg)


def _pad_cols(w, n):
    return jnp.pad(w, ((0, 0), (0, n - w.shape[1])))


def _pad_rows(w, n):
    return jnp.pad(w, ((0, n - w.shape[0]), (0, 0)))


def kernel(x, c, w_ada, b_ada, norm_mix_g, w_in, conv_w, conv_b, lru_wa, lru_ba, lru_wx, lru_bx, lru_lambda, rwkv_mu, rwkv_w0, rwkv_w2, rwkv_a0, rwkv_a2, rwkv_g2, rwkv_k_k, rwkv_k_a, rwkv_r_k, rwkv_ln_g, rwkv_ln_b, w_out, norm_ffn_g, w_gu, w_down, final_norm_g):
    bsz, seq, d = x.shape
    depth = w_ada.shape[0]
    dl = conv_w.shape[2]
    dr = rwkv_w0.shape[1]
    w_lora, a_lora, g_lora = rwkv_w2.shape[1], rwkv_a2.shape[1], rwkv_g2.shape[1]
    wpad, apad = LANE, LANE
    gpad = -(-g_lora // LANE) * LANE
    rkv_col0 = 2 * dl
    lora0 = rkv_col0 + 3 * dr

    x2 = x.reshape(bsz * seq, d)
    for l in range(depth):
        mod = _mod(c, w_ada[l], b_ada[l].reshape(1, -1))
        sh_m, sc_m, g_m, sh_f, sc_f, g_f = [t.reshape(bsz, 1, d) for t in jnp.split(mod, 6, axis=-1)]

        wi = w_in[l]
        o1, o2 = lora0 + w_lora, lora0 + w_lora + a_lora
        w_in_p = jnp.concatenate(
            [wi[:, :lora0], _pad_cols(wi[:, lora0:o1], wpad), _pad_cols(wi[:, o1:o2], apad),
             _pad_cols(wi[:, o2:], gpad)], axis=1).astype(BF16)
        mu = rwkv_mu[l].reshape(1, -1)
        mu_rkv = mu[:, :3 * dr]
        mu_lora = jnp.concatenate(
            [_pad_cols(mu[:, 3 * dr:3 * dr + w_lora], wpad),
             _pad_cols(mu[:, 3 * dr + w_lora:3 * dr + w_lora + a_lora], apad),
             _pad_cols(mu[:, 3 * dr + w_lora + a_lora:], gpad)], axis=1)
        w2p = _pad_rows(rwkv_w2[l], wpad)
        a2p = _pad_rows(rwkv_a2[l], apad)
        g2p = _pad_rows(rwkv_g2[l], gpad)

        p = _mm_in(x2, norm_mix_g[l].reshape(1, d), sh_m, sc_m, w_in_p, seq)

        y_a = _lru(p, conv_w[l], conv_b[l], lru_wa[l].astype(BF16), lru_wx[l].astype(BF16),
                   lru_ba[l], lru_bx[l], lru_lambda[l], bsz, seq)

        rowv = lambda t: t.reshape(1, dr)
        rp, yp, mc, nm, bonus, gg = _rwkv_a(
            p, mu_rkv, mu_lora, rowv(rwkv_w0[l]), rowv(rwkv_a0[l]), rowv(rwkv_k_k[l]),
            rowv(rwkv_k_a[l]), rowv(rwkv_r_k[l]), w2p, a2p, g2p, bsz, seq, rkv_col0)
        y_b = _rwkv_b(rp, yp, mc, nm, bonus, gg, rowv(rwkv_ln_g[l]), rowv(rwkv_ln_b[l]), bsz, seq)

        x2 = _mm_out(y_a, y_b, x2, g_m, w_out[l].astype(BF16), seq)

        last = l == depth - 1
        fg = final_norm_g.reshape(1, d) if last else None
        assert last, "only the final layer carries the closing RMSNorm"
        x2 = _ffn(x2, norm_ffn_g[l].reshape(1, d), sh_f, sc_f, g_f, w_gu[l].astype(BF16),
                  w_down[l].astype(BF16), fg, seq)
    return x2.reshape(bsz, seq, d)
```

```python
import functools

import jax
import jax.numpy as jnp
from jax import lax
from jax.experimental import pallas as pl
from jax.experimental.pallas import tpu as pltpu

F32 = jnp.float32
BF16 = jnp.bfloat16

LRU_HEADS = 4
CONV_WIDTH = 4
LRU_C = 8.0
HEAD = 64
CHUNK = 64
HEADS_PER_STEP = 4
CHUNKS_PER_STEP = 2
RMS_EPS = 1e-6
GN_EPS = 64e-5
L2_EPS = 1e-12
LANE = 128
SUBLANE = 8
VMEM_LIMIT = 56 * 1024 * 1024


def _params(*sem):
    return pltpu.CompilerParams(dimension_semantics=sem, vmem_limit_bytes=VMEM_LIMIT)


_NN = (((1,), (0,)), ((), ()))
_NT = (((1,), (1,)), ((), ()))
_TN = (((0,), (0,)), ((), ()))


def _dg(a, b, dims):
    return lax.dot_general(a, b, dims, preferred_element_type=F32)


def _split(x):
    hi = x.astype(BF16)
    lo = (x - hi.astype(F32)).astype(BF16)
    return hi, lo


def _mm3(a, b, dims=_NN):
    ah, al = _split(a)
    bh, bl = _split(b)
    return _dg(ah, bh, dims) + (_dg(ah, bl, dims) + _dg(al, bh, dims))


def _mm3_presplit(a, bh, bl):
    ah, al = _split(a)
    return _dg(ah, bh, _NN) + (_dg(ah, bl, _NN) + _dg(al, bh, _NN))


def _mm2_exact_rhs(a, b_bf16):
    ah, al = _split(a)
    return _dg(ah, b_bf16, _NN) + _dg(al, b_bf16, _NN)


def _mm2_exact_lhs(a_bf16, b):
    bh, bl = _split(b)
    return _dg(a_bf16, bh, _NN) + _dg(a_bf16, bl, _NN)


def _softplus(x):
    return jnp.maximum(x, 0.0) + jnp.log1p(jnp.exp(-jnp.abs(x)))


def _iota2(shape):
    return (lax.broadcasted_iota(jnp.int32, shape, 0),
            lax.broadcasted_iota(jnp.int32, shape, 1))


def _head_ones(n):
    r, c = _iota2((n, n))
    return jnp.where((r // HEAD) == (c // HEAD), 1.0, 0.0).astype(BF16)


def _mod_body(c_ref, w_ref, b_ref, o_ref):
    c = c_ref[...]
    ca = c * jax.nn.sigmoid(c)
    o_ref[...] = _mm3(ca, w_ref[...]) + b_ref[...]


def _mod(c, w, b, tn=1024):
    bsz, d = c.shape
    n = w.shape[1]
    return pl.pallas_call(
        _mod_body,
        grid=(n // tn,),
        in_specs=[pl.BlockSpec((bsz, d), lambda j: (0, 0)),
                  pl.BlockSpec((d, tn), lambda j: (0, j)),
                  pl.BlockSpec((1, tn), lambda j: (0, j))],
        out_specs=pl.BlockSpec((bsz, tn), lambda j: (0, j)),
        out_shape=jax.ShapeDtypeStruct((bsz, n), F32),
        compiler_params=_params("parallel"),
        name="mod",
    )(c, w, b)


def _norm_mod(x, g, sh, sc):
    y = x * lax.rsqrt(jnp.mean(x * x, axis=-1, keepdims=True) + RMS_EPS) * g
    return y * (1.0 + sc) + sh


def _norm_body(x_ref, g_ref, sh_ref, sc_ref, o_ref):
    o_ref[...] = _norm_mod(x_ref[...], g_ref[...], sh_ref[0], sc_ref[0]).astype(BF16)


def _norm(x2, g, sh, sc, seq, tm=512):
    m, d = x2.shape
    per_b = seq // tm
    return pl.pallas_call(
        _norm_body,
        grid=(m // tm,),
        in_specs=[pl.BlockSpec((tm, d), lambda i: (i, 0)),
                  pl.BlockSpec((1, d), lambda i: (0, 0)),
                  pl.BlockSpec((1, 1, d), lambda i: (i // per_b, 0, 0)),
                  pl.BlockSpec((1, 1, d), lambda i: (i // per_b, 0, 0))],
        out_specs=pl.BlockSpec((tm, d), lambda i: (i, 0)),
        out_shape=jax.ShapeDtypeStruct((m, d), BF16),
        compiler_params=_params("parallel"),
        name="norm_mix",
    )(x2, g, sh, sc)


def _mm_in_body(h_ref, w_ref, o_ref, wb_ref):
    @pl.when(pl.program_id(1) == 0)
    def _():
        wb_ref[...] = w_ref[...].astype(BF16)

    o_ref[...] = jnp.dot(h_ref[...], wb_ref[...], preferred_element_type=F32)


def _mm_in(h, w, ncols, tm=1024, tn=1024, name="mm_in"):
    m, d = h.shape
    return pl.pallas_call(
        _mm_in_body,
        grid=(ncols // tn, m // tm),
        in_specs=[pl.BlockSpec((tm, d), lambda j, i: (i, 0)),
                  pl.BlockSpec((d, tn), lambda j, i: (0, j))],
        out_specs=pl.BlockSpec((tm, tn), lambda j, i: (i, j)),
        out_shape=jax.ShapeDtypeStruct((m, ncols), F32),
        scratch_shapes=[pltpu.VMEM((d, tn), BF16)],
        compiler_params=_params("parallel", "arbitrary"),
        name=name,
    )(h, w)


def _shift_rows(x, s, fill, row):
    return jnp.where(row < s, fill, pltpu.roll(x, s, 0))


def _lru_body(u_ref, gate_ref, halo_ref, cw_ref, cb_ref, wa_ref, wx_ref, ba_ref, bx_ref,
              lam_ref, o_ref, carry_ref, *, tt):
    ti = pl.program_id(1)
    first = ti == 0

    @pl.when(first)
    def _():
        carry_ref[...] = jnp.zeros_like(carry_ref)

    p = u_ref[...]
    dl = p.shape[1]
    halo = jnp.where(first, 0.0, halo_ref[...])
    ext = jnp.concatenate([halo, p], axis=0)
    cw = cw_ref[...]
    u = cb_ref[...] + p * cw[CONV_WIDTH - 1:CONV_WIDTH, :]
    for j in range(1, CONV_WIDTH):
        shifted = pltpu.roll(ext, j, 0)[SUBLANE:, :]
        u = u + shifted * cw[CONV_WIDTH - 1 - j:CONV_WIDTH - j, :]

    hd = dl // LRU_HEADS
    ub = u.astype(BF16)
    ra, rx = [], []
    for h in range(LRU_HEADS):
        uh = ub[:, h * hd:(h + 1) * hd]
        ra.append(jnp.dot(uh, wa_ref[h], preferred_element_type=F32))
        rx.append(jnp.dot(uh, wx_ref[h], preferred_element_type=F32))
    r = jax.nn.sigmoid(jnp.concatenate(ra, axis=1) + ba_ref[...])
    ig = jax.nn.sigmoid(jnp.concatenate(rx, axis=1) + bx_ref[...])
    log_a = (-LRU_C) * r * _softplus(-lam_ref[...])
    a = jnp.exp(log_a)
    mult = jnp.sqrt(1.0 - jnp.exp(2.0 * log_a))
    row = lax.broadcasted_iota(jnp.int32, (tt, dl), 0)
    mult = jnp.where(jnp.logical_and(first, row == 0), 1.0, mult)
    b = mult * (ig * u)

    s = 1
    while s < tt:
        a_s = _shift_rows(a, s, 1.0, row)
        b_s = _shift_rows(b, s, 0.0, row)
        b = a * b_s + b
        a = a * a_s
        s *= 2
    h = b + a * carry_ref[...]
    carry_ref[...] = h[tt - 1:tt, :]
    o_ref[...] = h * jax.nn.gelu(gate_ref[...])


def _lru(p, conv_w, conv_b, wa, wx, ba, bx, lam, bsz, seq, tt=256):
    dl = conv_w.shape[1]
    nt = seq // tt
    rows8 = tt // SUBLANE
    row = lambda v: v.reshape(1, dl)
    return pl.pallas_call(
        functools.partial(_lru_body, tt=tt),
        grid=(bsz, nt),
        in_specs=[pl.BlockSpec((tt, dl), lambda b, i: (b * nt + i, 0)),
                  pl.BlockSpec((tt, dl), lambda b, i: (b * nt + i, 1)),
                  pl.BlockSpec((SUBLANE, dl),
                               lambda b, i: (jnp.maximum((b * nt + i) * rows8 - 1, 0), 0)),
                  pl.BlockSpec((CONV_WIDTH, dl), lambda b, i: (0, 0)),
                  pl.BlockSpec((1, dl), lambda b, i: (0, 0)),
                  pl.BlockSpec(wa.shape, lambda b, i: (0, 0, 0)),
                  pl.BlockSpec(wx.shape, lambda b, i: (0, 0, 0)),
                  pl.BlockSpec((1, dl), lambda b, i: (0, 0)),
                  pl.BlockSpec((1, dl), lambda b, i: (0, 0)),
                  pl.BlockSpec((1, dl), lambda b, i: (0, 0))],
        out_specs=pl.BlockSpec((tt, dl), lambda b, i: (b * nt + i, 0)),
        out_shape=jax.ShapeDtypeStruct((bsz * seq, dl), F32),
        scratch_shapes=[pltpu.VMEM((1, dl), F32)],
        compiler_params=_params("parallel", "arbitrary"),
        name="lru",
    )(p, p, p, conv_w, row(conv_b), wa, wx, row(ba), row(bx), row(lam))


def _token_shift(x, halo, mu, first, row):
    prev = jnp.where(first, 0.0, halo[SUBLANE - 1:SUBLANE, :])
    xs = jnp.where(row == 0, prev, pltpu.roll(x, 1, 0))
    return x + (xs - x) * mu


def _mm1(a, b, dims=_NN):
    return _dg(a.astype(BF16), b.astype(BF16), dims)


def _tri_inverse(a_low, eye, r, c):
    base = 8
    same = (r // base) == (c // base)
    d = [jnp.where(same, a, 0.0).astype(BF16) for a in a_low]
    d2 = [_mm1(t, t) for t in d]
    x = [eye + t.astype(F32) for t in d]
    x = [xi + _mm1(t2, xi) for xi, t2 in zip(x, d2)]
    d4 = [_mm1(t2, t2) for t2 in d2]
    x = [xi + _mm1(t4, xi) for xi, t4 in zip(x, d4)]
    size = base
    while size < CHUNK:
        off = jnp.logical_and((r // (2 * size)) == (c // (2 * size)),
                              (r // size) != (c // size))
        o = [jnp.where(off, a, 0.0).astype(BF16) for a in a_low]
        ox = [_mm1(oi, xi) for oi, xi in zip(o, x)]
        x = [xi + _mm1(xi, oxi) for xi, oxi in zip(x, ox)]
        size *= 2
    return x


def _rwkv_a_body(r_ref, k_ref, v_ref, l_ref, rh_ref, kh_ref, vh_ref, lh_ref,
                 mur_ref, muk_ref, muv_ref, mul_ref, w0_ref, a0_ref, kkw_ref, kaw_ref, rkw_ref,
                 w2h_ref, w2l_ref, a2h_ref, a2l_ref, g2h_ref, g2l_ref, ones_ref, tri_ref,
                 rp_ref, yp_ref, m_ref, n_ref, bonus_ref, g_ref):
    first = pl.program_id(1) == 0
    cl = CHUNK
    rows = CHUNKS_PER_STEP * cl
    width = HEADS_PER_STEP * HEAD
    row_w = lax.broadcasted_iota(jnp.int32, (rows, width), 0)
    row_l = lax.broadcasted_iota(jnp.int32, (rows, l_ref.shape[1]), 0)

    r = _token_shift(r_ref[...], rh_ref[...], mur_ref[...], first, row_w)
    k = _token_shift(k_ref[...], kh_ref[...], muk_ref[...], first, row_w)
    v = _token_shift(v_ref[...], vh_ref[...], muv_ref[...], first, row_w)
    lo = _token_shift(l_ref[...], lh_ref[...], mul_ref[...], first, row_l)
    wl = lo[:, 0:LANE]
    al = lo[:, LANE:2 * LANE]
    gl = lo[:, 2 * LANE:]

    w = -_softplus(-(w0_ref[...] + _mm3_presplit(jnp.tanh(wl), w2h_ref[...], w2l_ref[...]))) - 0.5
    lw = -jnp.exp(w)
    a = jax.nn.sigmoid(a0_ref[...] + _mm3_presplit(al, a2h_ref[...], a2l_ref[...]))
    g_ref[...] = _mm3_presplit(jax.nn.sigmoid(gl), g2h_ref[...], g2l_ref[...])

    ones_h = ones_ref[...]
    kk = k * kkw_ref[...]
    kk = kk / jnp.maximum(jnp.sqrt(_mm2_exact_rhs(kk * kk, ones_h)), L2_EPS)
    kp = k * (1.0 + (a - 1.0) * kaw_ref[...])
    bonus_ref[...] = _mm2_exact_rhs(r * kp * rkw_ref[...], ones_h) * v

    lc = _mm2_exact_lhs(tri_ref[...], lw)
    p_incl = jnp.exp(lc)
    p_excl = jnp.exp(lc - lw)
    p_inv = jnp.exp(-lc)
    p_end = jnp.concatenate(
        [jnp.broadcast_to(p_incl[(j + 1) * cl - 1:(j + 1) * cl, :], (cl, width))
         for j in range(CHUNKS_PER_STEP)], axis=0)

    abar = -(kk * p_excl)
    bbar = kk * a * p_inv
    kbar = kp * p_inv
    rbar = r * p_incl
    btil = bbar * p_end
    ktil = kbar * p_end

    units = [(j, h) for j in range(CHUNKS_PER_STEP) for h in range(HEADS_PER_STEP)]

    def cut(x, dtype=BF16):
        return [x[j * cl:(j + 1) * cl, h * HEAD:(h + 1) * HEAD].astype(dtype) for j, h in units]

    ab_, bb_, kb_, rb_, v_ = cut(abar), cut(bbar), cut(kbar), cut(rbar), cut(v)
    bt_, kt_ = cut(btil), cut(ktil)
    rb32, pe32 = cut(rbar, F32), cut(p_end, F32)

    rc, cc = _iota2((cl, cl))
    strict = rc > cc
    incl = rc >= cc
    diag = rc == cc
    eye = jnp.where(diag, 1.0, 0.0)
    a_ab = [jnp.where(strict, _dg(x, y, _NT), 0.0) for x, y in zip(ab_, bb_)]
    a_ak = [jnp.where(strict, _dg(x, y, _NT), 0.0).astype(BF16) for x, y in zip(ab_, kb_)]
    a_rb = [jnp.where(incl, _dg(x, y, _NT), 0.0).astype(BF16) for x, y in zip(rb_, bb_)]
    a_rk = [jnp.where(incl, _dg(x, y, _NT), 0.0).astype(BF16) for x, y in zip(rb_, kb_)]
    akv = [_dg(x, y, _NN).astype(BF16) for x, y in zip(a_ak, v_)]
    t = [x.astype(BF16) for x in _tri_inverse(a_ab, eye, rc, cc)]
    wm = [_dg(x, y, _NN).astype(BF16) for x, y in zip(t, ab_)]
    u0 = [_dg(x, y, _NN).astype(BF16) for x, y in zip(t, akv)]
    rps = [x + _dg(y, z, _NN) for x, y, z in zip(rb32, a_rb, wm)]
    yps = [_dg(x, y, _NN) + _dg(z, q, _NN) for x, y, z, q in zip(a_rb, u0, a_rk, v_)]
    ms = [jnp.where(diag, x, 0.0) + _dg(y, z, _TN) for x, y, z in zip(pe32, bt_, wm)]
    ns = [_dg(x, y, _TN) + _dg(z, q, _TN) for x, y, z, q in zip(bt_, u0, kt_, v_)]
    for j in range(CHUNKS_PER_STEP):
        rs = slice(j * cl, (j + 1) * cl)
        us = slice(j * HEADS_PER_STEP, (j + 1) * HEADS_PER_STEP)
        rp_ref[rs, :] = jnp.concatenate(rps[us], axis=1)
        yp_ref[rs, :] = jnp.concatenate(yps[us], axis=1)
        m_ref[rs, :] = jnp.concatenate(ms[us], axis=1)
        n_ref[rs, :] = jnp.concatenate(ns[us], axis=1)


def _rwkv_a(p, p_lora, mu_rkv, mu_lora, w0, a0, k_k, k_a, r_k, w2p, a2p, g2p, bsz, seq, rkv_col0):
    cl = CHUNKS_PER_STEP * CHUNK
    width = HEADS_PER_STEP * HEAD
    dr = w0.shape[1]
    ngroups = dr // width
    nc = seq // cl
    lw_ = mu_lora.shape[1]
    cb0 = rkv_col0 // width
    rows8 = cl // SUBLANE
    rt, ct = _iota2((cl, cl))
    tri = jnp.where(jnp.logical_and(rt >= ct, (rt // CHUNK) == (ct // CHUNK)), 1.0, 0.0).astype(BF16)
    ones_h = _head_ones(width)
    const = lambda arr: pl.BlockSpec(arr.shape, lambda b, i, q: (0, 0))
    lora_w = [t for wgt in (w2p, a2p, g2p) for t in _split(wgt)]

    def tile(cb_off):
        return pl.BlockSpec((cl, width), lambda b, i, q: (b * nc + i, cb0 + cb_off + q))

    def halo(cb_off):
        return pl.BlockSpec(
            (SUBLANE, width),
            lambda b, i, q: (jnp.maximum((b * nc + i) * rows8 - 1, 0), cb0 + cb_off + q))

    def prow(off=0):
        return pl.BlockSpec((1, width), lambda b, i, q: (0, off + q))

    out_tile = pl.BlockSpec((cl, width), lambda b, i, q: (b * nc + i, q))
    out_mat = pl.BlockSpec((CHUNKS_PER_STEP * HEAD, width), lambda b, i, q: (b * nc + i, q))
    act = jax.ShapeDtypeStruct((bsz * seq, dr), F32)
    mat = jax.ShapeDtypeStruct((bsz * (seq // CHUNK) * HEAD, dr), F32)
    return pl.pallas_call(
        _rwkv_a_body,
        grid=(bsz, nc, ngroups),
        in_specs=[tile(0), tile(ngroups), tile(2 * ngroups),
                  pl.BlockSpec((cl, lw_), lambda b, i, q: (b * nc + i, 0)),
                  halo(0), halo(ngroups), halo(2 * ngroups),
                  pl.BlockSpec((SUBLANE, lw_),
                               lambda b, i, q: (jnp.maximum((b * nc + i) * rows8 - 1, 0), 0)),
                  prow(0), prow(ngroups), prow(2 * ngroups),
                  pl.BlockSpec((1, lw_), lambda b, i, q: (0, 0)),
                  prow(), prow(), prow(), prow(), prow()]
                 + [pl.BlockSpec((t.shape[0], width), lambda b, i, q: (0, q)) for t in lora_w]
                 + [const(ones_h), const(tri)],
        out_specs=[out_tile, out_tile, out_mat, out_mat, out_tile, out_tile],
        out_shape=[act, act, mat, mat, act, act],
        compiler_params=_params("parallel", "parallel", "parallel"),
        name="rwkv_a",
    )(p, p, p, p_lora, p, p, p, p_lora, mu_rkv, mu_rkv, mu_rkv, mu_lora, w0, a0, k_k, k_a, r_k,
      *lora_w, ones_h, tri)


def _rwkv_b_body(rp_ref, yp_ref, m_ref, n_ref, bonus_ref, g_ref, lng_ref, lnb_ref,
                 o_ref, state_ref):
    @pl.when(pl.program_id(1) == 0)
    def _():
        state_ref[...] = jnp.zeros_like(state_ref)

    nheads = state_ref.shape[0]
    width = HEADS_PER_STEP * HEAD
    ngroups = nheads // HEADS_PER_STEP
    ones_h = _head_ones(width)
    inv_n = 1.0 / HEAD
    heads = range(nheads)
    hs = [slice(h * HEAD, (h + 1) * HEAD) for h in heads]
    qs = [slice(q * width, (q + 1) * width) for q in range(ngroups)]
    g0 = [state_ref[h].astype(BF16) for h in heads]
    ys = [_dg(rp_ref[:, hs[h]].astype(BF16), g0[h], _NN) + yp_ref[:, hs[h]] for h in heads]
    for h in heads:
        state_ref[h] = _dg(m_ref[:, hs[h]].astype(BF16), g0[h], _NN) + n_ref[:, hs[h]]
    y = [jnp.concatenate(ys[q * HEADS_PER_STEP:(q + 1) * HEADS_PER_STEP], axis=1)
         for q in range(ngroups)]
    mean = [_mm2_exact_rhs(t, ones_h) * inv_n for t in y]
    yc = [t - m for t, m in zip(y, mean)]
    var = [_mm2_exact_rhs(t * t, ones_h) * inv_n for t in yc]
    for q in range(ngroups):
        yn = yc[q] * lax.rsqrt(var[q] + GN_EPS) * lng_ref[:, qs[q]] + lnb_ref[:, qs[q]]
        o_ref[:, qs[q]] = (yn + bonus_ref[:, qs[q]]) * g_ref[:, qs[q]]


def _rwkv_b(rp, yp, mc, nm, bonus, g, ln_g, ln_b, bsz, seq):
    cl = CHUNK
    dr = rp.shape[1]
    nc = seq // cl
    tile = pl.BlockSpec((cl, dr), lambda b, i: (b * nc + i, 0))
    mat = pl.BlockSpec((HEAD, dr), lambda b, i: (b * nc + i, 0))
    prow = pl.BlockSpec((1, dr), lambda b, i: (0, 0))
    return pl.pallas_call(
        _rwkv_b_body,
        grid=(bsz, nc),
        in_specs=[tile, tile, mat, mat, tile, tile, prow, prow],
        out_specs=tile,
        out_shape=jax.ShapeDtypeStruct((bsz * seq, dr), F32),
        scratch_shapes=[pltpu.VMEM((dr // HEAD, HEAD, HEAD), F32)],
        compiler_params=_params("parallel", "arbitrary"),
        name="rwkv_b",
    )(rp, yp, mc, nm, bonus, g, ln_g, ln_b)


def _mm_out_body(ya_ref, yb_ref, x_ref, gm_ref, w_ref, o_ref):
    da = ya_ref.shape[1]
    mix = (jnp.dot(ya_ref[...].astype(BF16), w_ref[:da, :], preferred_element_type=F32)
           + jnp.dot(yb_ref[...].astype(BF16), w_ref[da:, :], preferred_element_type=F32))
    o_ref[...] = x_ref[...] + gm_ref[0] * mix


def _mm_out(ya, yb, x2, gm, w, seq, tm=512):
    m, d = x2.shape
    per_b = seq // tm
    return pl.pallas_call(
        _mm_out_body,
        grid=(m // tm,),
        in_specs=[pl.BlockSpec((tm, ya.shape[1]), lambda i: (i, 0)),
                  pl.BlockSpec((tm, yb.shape[1]), lambda i: (i, 0)),
                  pl.BlockSpec((tm, d), lambda i: (i, 0)),
                  pl.BlockSpec((1, 1, d), lambda i: (i // per_b, 0, 0)),
                  pl.BlockSpec(w.shape, lambda i: (0, 0))],
        out_specs=pl.BlockSpec((tm, d), lambda i: (i, 0)),
        out_shape=jax.ShapeDtypeStruct((m, d), F32),
        compiler_params=_params("parallel"),
        name="mm_out",
    )(ya, yb, x2, gm, w)


def _ffn_body(x_ref, g_ref, sh_ref, sc_ref, gf_ref, wg_ref, wu_ref, wd_ref, fg_ref,
              o_ref, h_ref, acc_ref):
    f = pl.program_id(1)

    @pl.when(f == 0)
    def _():
        h_ref[...] = _norm_mod(x_ref[...], g_ref[...], sh_ref[0], sc_ref[0]).astype(BF16)
        acc_ref[...] = jnp.zeros_like(acc_ref)

    h = h_ref[...]
    gate = jnp.dot(h, wg_ref[...], preferred_element_type=F32)
    up = jnp.dot(h, wu_ref[...], preferred_element_type=F32)
    act = (gate * jax.nn.sigmoid(gate) * up).astype(BF16)
    acc_ref[...] += jnp.dot(act, wd_ref[...], preferred_element_type=F32)

    @pl.when(f == pl.num_programs(1) - 1)
    def _():
        y = x_ref[...] + gf_ref[0] * acc_ref[...]
        o_ref[...] = (y * lax.rsqrt(jnp.mean(y * y, axis=-1, keepdims=True) + RMS_EPS)
                      * fg_ref[...])


def _ffn(x1, g, sh, sc, gf, w_gu, w_down, fg, seq, tm=512, tf=512):
    m, d = x1.shape
    dff = w_down.shape[0]
    nf = dff // tf
    per_b = seq // tm
    brow = pl.BlockSpec((1, 1, d), lambda i, f: (i // per_b, 0, 0))
    prow = pl.BlockSpec((1, d), lambda i, f: (0, 0))
    return pl.pallas_call(
        _ffn_body,
        grid=(m // tm, nf),
        in_specs=[pl.BlockSpec((tm, d), lambda i, f: (i, 0)),
                  prow, brow, brow, brow,
                  pl.BlockSpec((d, tf), lambda i, f: (0, f)),
                  pl.BlockSpec((d, tf), lambda i, f: (0, nf + f)),
                  pl.BlockSpec((tf, d), lambda i, f: (f, 0)),
                  prow],
        out_specs=pl.BlockSpec((tm, d), lambda i, f: (i, 0)),
        out_shape=jax.ShapeDtypeStruct((m, d), F32),
        scratch_shapes=[pltpu.VMEM((tm, d), BF16), pltpu.VMEM((tm, d), F32)],
        compiler_params=_params("parallel", "arbitrary"),
        name="ffn",
    )(x1, g, sh, sc, gf, w_gu, w_gu, w_down, fg)


def _pad_cols(w, n):
    return jnp.pad(w, ((0, 0), (0, n - w.shape[1])))


def _pad_rows(w, n):
    return jnp.pad(w, ((0, n - w.shape[0]), (0, 0)))


def kernel(x, c, w_ada, b_ada, norm_mix_g, w_in, conv_w, conv_b, lru_wa, lru_ba, lru_wx, lru_bx, lru_lambda, rwkv_mu, rwkv_w0, rwkv_w2, rwkv_a0, rwkv_a2, rwkv_g2, rwkv_k_k, rwkv_k_a, rwkv_r_k, rwkv_ln_g, rwkv_ln_b, w_out, norm_ffn_g, w_gu, w_down, final_norm_g):
    bsz, seq, d = x.shape
    depth = w_ada.shape[0]
    dl = conv_w.shape[2]
    dr = rwkv_w0.shape[1]
    w_lora, a_lora, g_lora = rwkv_w2.shape[1], rwkv_a2.shape[1], rwkv_g2.shape[1]
    wpad, apad = LANE, LANE
    gpad = -(-g_lora // LANE) * LANE
    rkv_col0 = 2 * dl
    lora0 = rkv_col0 + 3 * dr

    x2 = x.reshape(bsz * seq, d)
    for l in range(depth):
        mod = _mod(c, w_ada[l], b_ada[l].reshape(1, -1))
        sh_m, sc_m, g_m, sh_f, sc_f, g_f = [t.reshape(bsz, 1, d) for t in jnp.split(mod, 6, axis=-1)]

        wi = w_in[l]
        o1, o2 = lora0 + w_lora, lora0 + w_lora + a_lora
        w_lora_p = jnp.concatenate(
            [_pad_cols(wi[:, lora0:o1], wpad), _pad_cols(wi[:, o1:o2], apad),
             _pad_cols(wi[:, o2:], gpad)], axis=1)
        mu = rwkv_mu[l].reshape(1, -1)
        mu_rkv = mu[:, :3 * dr]
        mu_lora = jnp.concatenate(
            [_pad_cols(mu[:, 3 * dr:3 * dr + w_lora], wpad),
             _pad_cols(mu[:, 3 * dr + w_lora:3 * dr + w_lora + a_lora], apad),
             _pad_cols(mu[:, 3 * dr + w_lora + a_lora:], gpad)], axis=1)
        w2p = _pad_rows(rwkv_w2[l], wpad)
        a2p = _pad_rows(rwkv_a2[l], apad)
        g2p = _pad_rows(rwkv_g2[l], gpad)

        h = _norm(x2, norm_mix_g[l].reshape(1, d), sh_m, sc_m, seq)
        p = _mm_in(h, wi, lora0)
        p_lora = _mm_in(h, w_lora_p, w_lora_p.shape[1], tn=w_lora_p.shape[1], name="mm_lora")

        y_a = _lru(p, conv_w[l], conv_b[l], lru_wa[l].astype(BF16), lru_wx[l].astype(BF16),
                   lru_ba[l], lru_bx[l], lru_lambda[l], bsz, seq)

        rowv = lambda t: t.reshape(1, dr)
        rp, yp, mc, nm, bonus, gg = _rwkv_a(
            p, p_lora, mu_rkv, mu_lora, rowv(rwkv_w0[l]), rowv(rwkv_a0[l]), rowv(rwkv_k_k[l]),
            rowv(rwkv_k_a[l]), rowv(rwkv_r_k[l]), w2p, a2p, g2p, bsz, seq, rkv_col0)
        y_b = _rwkv_b(rp, yp, mc, nm, bonus, gg, rowv(rwkv_ln_g[l]), rowv(rwkv_ln_b[l]), bsz, seq)

        x2 = _mm_out(y_a, y_b, x2, g_m, w_out[l].astype(BF16), seq)

        last = l == depth - 1
        fg = final_norm_g.reshape(1, d) if last else None
        assert last, "only the final layer carries the closing RMSNorm"
        x2 = _ffn(x2, norm_ffn_g[l].reshape(1, d), sh_f, sc_f, g_f, w_gu[l].astype(BF16),
                  w_down[l].astype(BF16), fg, seq)
    return x2.reshape(bsz, seq, d)
```

```python
import functools

import jax
import jax.numpy as jnp
from jax import lax
from jax.experimental import pallas as pl
from jax.experimental.pallas import tpu as pltpu

F32 = jnp.float32
BF16 = jnp.bfloat16

LRU_HEADS = 4
CONV_WIDTH = 4
LRU_C = 8.0
HEAD = 64
CHUNK = 64
HEADS_PER_STEP = 4
CHUNKS_PER_STEP = 2
RMS_EPS = 1e-6
GN_EPS = 64e-5
L2_EPS = 1e-12
LANE = 128
SUBLANE = 8
VMEM_LIMIT = 56 * 1024 * 1024


def _params(*sem):
    return pltpu.CompilerParams(dimension_semantics=sem, vmem_limit_bytes=VMEM_LIMIT)


_NN = (((1,), (0,)), ((), ()))
_NT = (((1,), (1,)), ((), ()))
_TN = (((0,), (0,)), ((), ()))


def _dg(a, b, dims):
    return lax.dot_general(a, b, dims, preferred_element_type=F32)


def _split(x):
    hi = x.astype(BF16)
    lo = (x - hi.astype(F32)).astype(BF16)
    return hi, lo


def _mm3(a, b, dims=_NN):
    ah, al = _split(a)
    bh, bl = _split(b)
    return _dg(ah, bh, dims) + (_dg(ah, bl, dims) + _dg(al, bh, dims))


def _mm3_presplit(a, bh, bl):
    ah, al = _split(a)
    return _dg(ah, bh, _NN) + (_dg(ah, bl, _NN) + _dg(al, bh, _NN))


def _mm2_exact_rhs(a, b_bf16):
    ah, al = _split(a)
    return _dg(ah, b_bf16, _NN) + _dg(al, b_bf16, _NN)


def _mm2_exact_lhs(a_bf16, b):
    bh, bl = _split(b)
    return _dg(a_bf16, bh, _NN) + _dg(a_bf16, bl, _NN)


def _softplus(x):
    return jnp.maximum(x, 0.0) + jnp.log1p(jnp.exp(-jnp.abs(x)))


def _iota2(shape):
    return (lax.broadcasted_iota(jnp.int32, shape, 0),
            lax.broadcasted_iota(jnp.int32, shape, 1))


def _head_ones(n):
    r, c = _iota2((n, n))
    return jnp.where((r // HEAD) == (c // HEAD), 1.0, 0.0).astype(BF16)


def _mod_body(c_ref, w_ref, b_ref, o_ref):
    c = c_ref[...]
    ca = c * jax.nn.sigmoid(c)
    o_ref[...] = _mm3(ca, w_ref[...]) + b_ref[...]


def _mod(c, w, b, tn=1024):
    bsz, d = c.shape
    n = w.shape[1]
    return pl.pallas_call(
        _mod_body,
        grid=(n // tn,),
        in_specs=[pl.BlockSpec((bsz, d), lambda j: (0, 0)),
                  pl.BlockSpec((d, tn), lambda j: (0, j)),
                  pl.BlockSpec((1, tn), lambda j: (0, j))],
        out_specs=pl.BlockSpec((bsz, tn), lambda j: (0, j)),
        out_shape=jax.ShapeDtypeStruct((bsz, n), F32),
        compiler_params=_params("parallel"),
        name="mod",
    )(c, w, b)


def _norm_mod(x, g, sh, sc):
    y = x * lax.rsqrt(jnp.mean(x * x, axis=-1, keepdims=True) + RMS_EPS) * g
    return y * (1.0 + sc) + sh


def _norm_body(x_ref, g_ref, sh_ref, sc_ref, o_ref):
    o_ref[...] = _norm_mod(x_ref[...], g_ref[...], sh_ref[0], sc_ref[0]).astype(BF16)


def _norm(x2, g, sh, sc, seq, tm=512):
    m, d = x2.shape
    per_b = seq // tm
    return pl.pallas_call(
        _norm_body,
        grid=(m // tm,),
        in_specs=[pl.BlockSpec((tm, d), lambda i: (i, 0)),
                  pl.BlockSpec((1, d), lambda i: (0, 0)),
                  pl.BlockSpec((1, 1, d), lambda i: (i // per_b, 0, 0)),
                  pl.BlockSpec((1, 1, d), lambda i: (i // per_b, 0, 0))],
        out_specs=pl.BlockSpec((tm, d), lambda i: (i, 0)),
        out_shape=jax.ShapeDtypeStruct((m, d), BF16),
        compiler_params=_params("parallel"),
        name="norm_mix",
    )(x2, g, sh, sc)


def _mm_in_body(h_ref, w_ref, o_ref, wb_ref):
    @pl.when(pl.program_id(1) == 0)
    def _():
        wb_ref[...] = w_ref[...].astype(BF16)

    o_ref[...] = _dg(h_ref[...], wb_ref[...], _NT)


def _mm_in(h, wt, ncols, tm=1024, tn=1024, name="mm_in"):
    m, d = h.shape
    return pl.pallas_call(
        _mm_in_body,
        grid=(ncols // tn, m // tm),
        in_specs=[pl.BlockSpec((tm, d), lambda j, i: (i, 0)),
                  pl.BlockSpec((tn, d), lambda j, i: (j, 0))],
        out_specs=pl.BlockSpec((tm, tn), lambda j, i: (i, j)),
        out_shape=jax.ShapeDtypeStruct((m, ncols), F32),
        scratch_shapes=[pltpu.VMEM((tn, d), BF16)],
        compiler_params=_params("parallel", "arbitrary"),
        name=name,
    )(h, wt)


def _shift_rows(x, s, fill, row):
    return jnp.where(row < s, fill, pltpu.roll(x, s, 0))


def _lru_body(u_ref, gate_ref, halo_ref, cw_ref, cb_ref, wa_ref, wx_ref, ba_ref, bx_ref,
              lam_ref, o_ref, carry_ref, *, tt):
    ti = pl.program_id(1)
    first = ti == 0

    @pl.when(first)
    def _():
        carry_ref[...] = jnp.zeros_like(carry_ref)

    p = u_ref[...]
    dl = p.shape[1]
    halo = jnp.where(first, 0.0, halo_ref[...])
    ext = jnp.concatenate([halo, p], axis=0)
    cw = cw_ref[...]
    u = cb_ref[...] + p * cw[CONV_WIDTH - 1:CONV_WIDTH, :]
    for j in range(1, CONV_WIDTH):
        shifted = pltpu.roll(ext, j, 0)[SUBLANE:, :]
        u = u + shifted * cw[CONV_WIDTH - 1 - j:CONV_WIDTH - j, :]

    hd = dl // LRU_HEADS
    ub = u.astype(BF16)
    ra, rx = [], []
    for h in range(LRU_HEADS):
        uh = ub[:, h * hd:(h + 1) * hd]
        ra.append(jnp.dot(uh, wa_ref[h], preferred_element_type=F32))
        rx.append(jnp.dot(uh, wx_ref[h], preferred_element_type=F32))
    r = jax.nn.sigmoid(jnp.concatenate(ra, axis=1) + ba_ref[...])
    ig = jax.nn.sigmoid(jnp.concatenate(rx, axis=1) + bx_ref[...])
    a = jnp.exp(r * ((-LRU_C) * _softplus(-lam_ref[...])))
    mult = jnp.sqrt(1.0 - a * a)
    row = lax.broadcasted_iota(jnp.int32, (tt, dl), 0)
    mult = jnp.where(jnp.logical_and(first, row == 0), 1.0, mult)
    b = mult * (ig * u)

    groups = tt // SUBLANE
    a3 = a.reshape(groups, SUBLANE, dl)
    b3 = b.reshape(groups, SUBLANE, dl)
    sub = lax.broadcasted_iota(jnp.int32, (groups, SUBLANE, dl), 1)
    s = 1
    while s < SUBLANE:
        keep = sub >= s
        a_s = jnp.where(keep, pltpu.roll(a3, s, 1), 1.0)
        b_s = jnp.where(keep, pltpu.roll(b3, s, 1), 0.0)
        b3 = a3 * b_s + b3
        a3 = a3 * a_s
        s *= 2
    gate = jax.nn.gelu(gate_ref[...])
    carry = carry_ref[...]
    for g in range(groups):
        h = b3[g] + a3[g] * carry
        carry = h[SUBLANE - 1:SUBLANE, :]
        o_ref[g * SUBLANE:(g + 1) * SUBLANE, :] = h * gate[g * SUBLANE:(g + 1) * SUBLANE, :]
    carry_ref[...] = carry


def _lru(p, conv_w, conv_b, wa, wx, ba, bx, lam, bsz, seq, tt=256):
    dl = conv_w.shape[1]
    nt = seq // tt
    rows8 = tt // SUBLANE
    row = lambda v: v.reshape(1, dl)
    return pl.pallas_call(
        functools.partial(_lru_body, tt=tt),
        grid=(bsz, nt),
        in_specs=[pl.BlockSpec((tt, dl), lambda b, i: (b * nt + i, 0)),
                  pl.BlockSpec((tt, dl), lambda b, i: (b * nt + i, 1)),
                  pl.BlockSpec((SUBLANE, dl),
                               lambda b, i: (jnp.maximum((b * nt + i) * rows8 - 1, 0), 0)),
                  pl.BlockSpec((CONV_WIDTH, dl), lambda b, i: (0, 0)),
                  pl.BlockSpec((1, dl), lambda b, i: (0, 0)),
                  pl.BlockSpec(wa.shape, lambda b, i: (0, 0, 0)),
                  pl.BlockSpec(wx.shape, lambda b, i: (0, 0, 0)),
                  pl.BlockSpec((1, dl), lambda b, i: (0, 0)),
                  pl.BlockSpec((1, dl), lambda b, i: (0, 0)),
                  pl.BlockSpec((1, dl), lambda b, i: (0, 0))],
        out_specs=pl.BlockSpec((tt, dl), lambda b, i: (b * nt + i, 0)),
        out_shape=jax.ShapeDtypeStruct((bsz * seq, dl), F32),
        scratch_shapes=[pltpu.VMEM((1, dl), F32)],
        compiler_params=_params("parallel", "arbitrary"),
        name="lru",
    )(p, p, p, conv_w, row(conv_b), wa, wx, row(ba), row(bx), row(lam))


def _token_shift(x, halo, mu, first, row):
    prev = jnp.where(first, 0.0, halo[SUBLANE - 1:SUBLANE, :])
    xs = jnp.where(row == 0, prev, pltpu.roll(x, 1, 0))
    return x + (xs - x) * mu


def _mm1(a, b, dims=_NN):
    return _dg(a.astype(BF16), b.astype(BF16), dims)


def _tri_inverse(a_low, eye, r, c):
    base = 8
    same = (r // base) == (c // base)
    d = [jnp.where(same, a, 0.0).astype(BF16) for a in a_low]
    d2 = [_mm1(t, t) for t in d]
    x = [eye + t.astype(F32) for t in d]
    x = [xi + _mm1(t2, xi) for xi, t2 in zip(x, d2)]
    d4 = [_mm1(t2, t2) for t2 in d2]
    x = [xi + _mm1(t4, xi) for xi, t4 in zip(x, d4)]
    size = base
    while size < CHUNK:
        off = jnp.logical_and((r // (2 * size)) == (c // (2 * size)),
                              (r // size) != (c // size))
        o = [jnp.where(off, a, 0.0).astype(BF16) for a in a_low]
        ox = [_mm1(oi, xi) for oi, xi in zip(o, x)]
        x = [xi + _mm1(xi, oxi) for xi, oxi in zip(x, ox)]
        size *= 2
    return x


def _rwkv_a_body(r_ref, k_ref, v_ref, l_ref, rh_ref, kh_ref, vh_ref, lh_ref,
                 mur_ref, muk_ref, muv_ref, mul_ref, w0_ref, a0_ref, kkw_ref, kaw_ref, rkw_ref,
                 w2h_ref, w2l_ref, a2h_ref, a2l_ref, g2h_ref, g2l_ref, ones_ref, tri_ref,
                 rp_ref, yp_ref, m_ref, n_ref, bonus_ref, g_ref):
    first = pl.program_id(1) == 0
    cl = CHUNK
    rows = CHUNKS_PER_STEP * cl
    width = HEADS_PER_STEP * HEAD
    row_w = lax.broadcasted_iota(jnp.int32, (rows, width), 0)
    row_l = lax.broadcasted_iota(jnp.int32, (rows, l_ref.shape[1]), 0)

    r = _token_shift(r_ref[...], rh_ref[...], mur_ref[...], first, row_w)
    k = _token_shift(k_ref[...], kh_ref[...], muk_ref[...], first, row_w)
    v = _token_shift(v_ref[...], vh_ref[...], muv_ref[...], first, row_w)
    lo = _token_shift(l_ref[...], lh_ref[...], mul_ref[...], first, row_l)
    wl = lo[:, 0:LANE]
    al = lo[:, LANE:2 * LANE]
    gl = lo[:, 2 * LANE:]

    w = -_softplus(-(w0_ref[...] + _mm3_presplit(jnp.tanh(wl), w2h_ref[...], w2l_ref[...]))) - 0.5
    lw = -jnp.exp(w)
    a = jax.nn.sigmoid(a0_ref[...] + _mm3_presplit(al, a2h_ref[...], a2l_ref[...]))
    g_ref[...] = _mm3_presplit(jax.nn.sigmoid(gl), g2h_ref[...], g2l_ref[...])

    ones_h = ones_ref[...]
    kk = k * kkw_ref[...]
    kk = kk / jnp.maximum(jnp.sqrt(_mm2_exact_rhs(kk * kk, ones_h)), L2_EPS)
    kp = k * (1.0 + (a - 1.0) * kaw_ref[...])
    bonus_ref[...] = _mm2_exact_rhs(r * kp * rkw_ref[...], ones_h) * v

    lc = _mm2_exact_lhs(tri_ref[...], lw)
    p_incl = jnp.exp(lc)
    p_excl = jnp.exp(lc - lw)
    p_inv = jnp.exp(-lc)
    p_end = jnp.concatenate(
        [jnp.broadcast_to(p_incl[(j + 1) * cl - 1:(j + 1) * cl, :], (cl, width))
         for j in range(CHUNKS_PER_STEP)], axis=0)

    abar = -(kk * p_excl)
    bbar = kk * a * p_inv
    kbar = kp * p_inv
    rbar = r * p_incl
    btil = bbar * p_end
    ktil = kbar * p_end

    units = [(j, h) for j in range(CHUNKS_PER_STEP) for h in range(HEADS_PER_STEP)]

    def cut(x, dtype=BF16):
        return [x[j * cl:(j + 1) * cl, h * HEAD:(h + 1) * HEAD].astype(dtype) for j, h in units]

    ab_, bb_, kb_, rb_, v_ = cut(abar), cut(bbar), cut(kbar), cut(rbar), cut(v)
    bt_, kt_ = cut(btil), cut(ktil)
    rb32, pe32 = cut(rbar, F32), cut(p_end, F32)

    rc, cc = _iota2((cl, cl))
    strict = rc > cc
    incl = rc >= cc
    diag = rc == cc
    eye = jnp.where(diag, 1.0, 0.0)
    a_ab = [jnp.where(strict, _dg(x, y, _NT), 0.0) for x, y in zip(ab_, bb_)]
    a_ak = [jnp.where(strict, _dg(x, y, _NT), 0.0).astype(BF16) for x, y in zip(ab_, kb_)]
    a_rb = [jnp.where(incl, _dg(x, y, _NT), 0.0).astype(BF16) for x, y in zip(rb_, bb_)]
    a_rk = [jnp.where(incl, _dg(x, y, _NT), 0.0).astype(BF16) for x, y in zip(rb_, kb_)]
    akv = [_dg(x, y, _NN).astype(BF16) for x, y in zip(a_ak, v_)]
    t = [x.astype(BF16) for x in _tri_inverse(a_ab, eye, rc, cc)]
    wm = [_dg(x, y, _NN).astype(BF16) for x, y in zip(t, ab_)]
    u0 = [_dg(x, y, _NN).astype(BF16) for x, y in zip(t, akv)]
    rps = [x + _dg(y, z, _NN) for x, y, z in zip(rb32, a_rb, wm)]
    yps = [_dg(x, y, _NN) + _dg(z, q, _NN) for x, y, z, q in zip(a_rb, u0, a_rk, v_)]
    ms = [jnp.where(diag, x, 0.0) + _dg(y, z, _TN) for x, y, z in zip(pe32, bt_, wm)]
    ns = [_dg(x, y, _TN) + _dg(z, q, _TN) for x, y, z, q in zip(bt_, u0, kt_, v_)]
    for j in range(CHUNKS_PER_STEP):
        rs = slice(j * cl, (j + 1) * cl)
        us = slice(j * HEADS_PER_STEP, (j + 1) * HEADS_PER_STEP)
        rp_ref[rs, :] = jnp.concatenate(rps[us], axis=1)
        yp_ref[rs, :] = jnp.concatenate(yps[us], axis=1)
        m_ref[rs, :] = jnp.concatenate(ms[us], axis=1)
        n_ref[rs, :] = jnp.concatenate(ns[us], axis=1)


def _rwkv_a(p, p_lora, mu_rkv, mu_lora, w0, a0, k_k, k_a, r_k, w2p, a2p, g2p, bsz, seq, rkv_col0):
    cl = CHUNKS_PER_STEP * CHUNK
    width = HEADS_PER_STEP * HEAD
    dr = w0.shape[1]
    ngroups = dr // width
    nc = seq // cl
    lw_ = mu_lora.shape[1]
    cb0 = rkv_col0 // width
    rows8 = cl // SUBLANE
    rt, ct = _iota2((cl, cl))
    tri = jnp.where(jnp.logical_and(rt >= ct, (rt // CHUNK) == (ct // CHUNK)), 1.0, 0.0).astype(BF16)
    ones_h = _head_ones(width)
    const = lambda arr: pl.BlockSpec(arr.shape, lambda b, i, q: (0, 0))
    lora_w = [t for wgt in (w2p, a2p, g2p) for t in _split(wgt)]

    def tile(cb_off):
        return pl.BlockSpec((cl, width), lambda b, i, q: (b * nc + i, cb0 + cb_off + q))

    def halo(cb_off):
        return pl.BlockSpec(
            (SUBLANE, width),
            lambda b, i, q: (jnp.maximum((b * nc + i) * rows8 - 1, 0), cb0 + cb_off + q))

    def prow(off=0):
        return pl.BlockSpec((1, width), lambda b, i, q: (0, off + q))

    out_tile = pl.BlockSpec((cl, width), lambda b, i, q: (b * nc + i, q))
    out_mat = pl.BlockSpec((CHUNKS_PER_STEP * HEAD, width), lambda b, i, q: (b * nc + i, q))
    act = jax.ShapeDtypeStruct((bsz * seq, dr), F32)
    mat = jax.ShapeDtypeStruct((bsz * (seq // CHUNK) * HEAD, dr), F32)
    return pl.pallas_call(
        _rwkv_a_body,
        grid=(bsz, nc, ngroups),
        in_specs=[tile(0), tile(ngroups), tile(2 * ngroups),
                  pl.BlockSpec((cl, lw_), lambda b, i, q: (b * nc + i, 0)),
                  halo(0), halo(ngroups), halo(2 * ngroups),
                  pl.BlockSpec((SUBLANE, lw_),
                               lambda b, i, q: (jnp.maximum((b * nc + i) * rows8 - 1, 0), 0)),
                  prow(0), prow(ngroups), prow(2 * ngroups),
                  pl.BlockSpec((1, lw_), lambda b, i, q: (0, 0)),
                  prow(), prow(), prow(), prow(), prow()]
                 + [pl.BlockSpec((t.shape[0], width), lambda b, i, q: (0, q)) for t in lora_w]
                 + [const(ones_h), const(tri)],
        out_specs=[out_tile, out_tile, out_mat, out_mat, out_tile, out_tile],
        out_shape=[act, act, mat, mat, act, act],
        compiler_params=_params("parallel", "parallel", "parallel"),
        name="rwkv_a",
    )(p, p, p, p_lora, p, p, p, p_lora, mu_rkv, mu_rkv, mu_rkv, mu_lora, w0, a0, k_k, k_a, r_k,
      *lora_w, ones_h, tri)


def _rwkv_b_body(rp_ref, yp_ref, m_ref, n_ref, bonus_ref, g_ref, lng_ref, lnb_ref,
                 o_ref, state_ref):
    @pl.when(pl.program_id(1) == 0)
    def _():
        state_ref[...] = jnp.zeros_like(state_ref)

    nheads = state_ref.shape[0]
    width = HEADS_PER_STEP * HEAD
    ngroups = nheads // HEADS_PER_STEP
    ones_h = _head_ones(width)
    inv_n = 1.0 / HEAD
    heads = range(nheads)
    hs = [slice(h * HEAD, (h + 1) * HEAD) for h in heads]
    qs = [slice(q * width, (q + 1) * width) for q in range(ngroups)]
    g0 = [state_ref[h].astype(BF16) for h in heads]
    ys = [_dg(rp_ref[:, hs[h]].astype(BF16), g0[h], _NN) + yp_ref[:, hs[h]] for h in heads]
    for h in heads:
        state_ref[h] = _dg(m_ref[:, hs[h]].astype(BF16), g0[h], _NN) + n_ref[:, hs[h]]
    y = [jnp.concatenate(ys[q * HEADS_PER_STEP:(q + 1) * HEADS_PER_STEP], axis=1)
         for q in range(ngroups)]
    mean = [_mm2_exact_rhs(t, ones_h) * inv_n for t in y]
    yc = [t - m for t, m in zip(y, mean)]
    var = [_mm2_exact_rhs(t * t, ones_h) * inv_n for t in yc]
    for q in range(ngroups):
        yn = yc[q] * lax.rsqrt(var[q] + GN_EPS) * lng_ref[:, qs[q]] + lnb_ref[:, qs[q]]
        o_ref[:, qs[q]] = (yn + bonus_ref[:, qs[q]]) * g_ref[:, qs[q]]


def _rwkv_b(rp, yp, mc, nm, bonus, g, ln_g, ln_b, bsz, seq):
    cl = CHUNK
    dr = rp.shape[1]
    nc = seq // cl
    tile = pl.BlockSpec((cl, dr), lambda b, i: (b * nc + i, 0))
    mat = pl.BlockSpec((HEAD, dr), lambda b, i: (b * nc + i, 0))
    prow = pl.BlockSpec((1, dr), lambda b, i: (0, 0))
    return pl.pallas_call(
        _rwkv_b_body,
        grid=(bsz, nc),
        in_specs=[tile, tile, mat, mat, tile, tile, prow, prow],
        out_specs=tile,
        out_shape=jax.ShapeDtypeStruct((bsz * seq, dr), F32),
        scratch_shapes=[pltpu.VMEM((dr // HEAD, HEAD, HEAD), F32)],
        compiler_params=_params("parallel", "arbitrary"),
        name="rwkv_b",
    )(rp, yp, mc, nm, bonus, g, ln_g, ln_b)


def _mm_out_body(ya_ref, yb_ref, x_ref, gm_ref, w_ref, o_ref):
    da = ya_ref.shape[1]
    mix = (jnp.dot(ya_ref[...].astype(BF16), w_ref[:da, :], preferred_element_type=F32)
           + jnp.dot(yb_ref[...].astype(BF16), w_ref[da:, :], preferred_element_type=F32))
    o_ref[...] = x_ref[...] + gm_ref[0] * mix


def _mm_out(ya, yb, x2, gm, w, seq, tm=512):
    m, d = x2.shape
    per_b = seq // tm
    return pl.pallas_call(
        _mm_out_body,
        grid=(m // tm,),
        in_specs=[pl.BlockSpec((tm, ya.shape[1]), lambda i: (i, 0)),
                  pl.BlockSpec((tm, yb.shape[1]), lambda i: (i, 0)),
                  pl.BlockSpec((tm, d), lambda i: (i, 0)),
                  pl.BlockSpec((1, 1, d), lambda i: (i // per_b, 0, 0)),
                  pl.BlockSpec(w.shape, lambda i: (0, 0))],
        out_specs=pl.BlockSpec((tm, d), lambda i: (i, 0)),
        out_shape=jax.ShapeDtypeStruct((m, d), F32),
        compiler_params=_params("parallel"),
        name="mm_out",
    )(ya, yb, x2, gm, w)


def _ffn_body(x_ref, g_ref, sh_ref, sc_ref, gf_ref, wg_ref, wu_ref, wd_ref, fg_ref,
              o_ref, h_ref, acc_ref):
    f = pl.program_id(1)

    @pl.when(f == 0)
    def _():
        h_ref[...] = _norm_mod(x_ref[...], g_ref[...], sh_ref[0], sc_ref[0]).astype(BF16)
        acc_ref[...] = jnp.zeros_like(acc_ref)

    h = h_ref[...]
    gate = jnp.dot(h, wg_ref[...], preferred_element_type=F32)
    up = jnp.dot(h, wu_ref[...], preferred_element_type=F32)
    act = (gate * jax.nn.sigmoid(gate) * up).astype(BF16)
    acc_ref[...] += jnp.dot(act, wd_ref[...], preferred_element_type=F32)

    @pl.when(f == pl.num_programs(1) - 1)
    def _():
        y = x_ref[...] + gf_ref[0] * acc_ref[...]
        o_ref[...] = (y * lax.rsqrt(jnp.mean(y * y, axis=-1, keepdims=True) + RMS_EPS)
                      * fg_ref[...])


def _ffn(x1, g, sh, sc, gf, w_gu, w_down, fg, seq, tm=512, tf=512):
    m, d = x1.shape
    dff = w_down.shape[0]
    nf = dff // tf
    per_b = seq // tm
    brow = pl.BlockSpec((1, 1, d), lambda i, f: (i // per_b, 0, 0))
    prow = pl.BlockSpec((1, d), lambda i, f: (0, 0))
    return pl.pallas_call(
        _ffn_body,
        grid=(m // tm, nf),
        in_specs=[pl.BlockSpec((tm, d), lambda i, f: (i, 0)),
                  prow, brow, brow, brow,
                  pl.BlockSpec((d, tf), lambda i, f: (0, f)),
                  pl.BlockSpec((d, tf), lambda i, f: (0, nf + f)),
                  pl.BlockSpec((tf, d), lambda i, f: (f, 0)),
                  prow],
        out_specs=pl.BlockSpec((tm, d), lambda i, f: (i, 0)),
        out_shape=jax.ShapeDtypeStruct((m, d), F32),
        scratch_shapes=[pltpu.VMEM((tm, d), BF16), pltpu.VMEM((tm, d), F32)],
        compiler_params=_params("parallel", "arbitrary"),
        name="ffn",
    )(x1, g, sh, sc, gf, w_gu, w_gu, w_down, fg)


def _pad_cols(w, n):
    return jnp.pad(w, ((0, 0), (0, n - w.shape[1])))


def _pad_rows(w, n):
    return jnp.pad(w, ((0, n - w.shape[0]), (0, 0)))


def kernel(x, c, w_ada, b_ada, norm_mix_g, w_in, conv_w, conv_b, lru_wa, lru_ba, lru_wx, lru_bx, lru_lambda, rwkv_mu, rwkv_w0, rwkv_w2, rwkv_a0, rwkv_a2, rwkv_g2, rwkv_k_k, rwkv_k_a, rwkv_r_k, rwkv_ln_g, rwkv_ln_b, w_out, norm_ffn_g, w_gu, w_down, final_norm_g):
    bsz, seq, d = x.shape
    depth = w_ada.shape[0]
    dl = conv_w.shape[2]
    dr = rwkv_w0.shape[1]
    w_lora, a_lora, g_lora = rwkv_w2.shape[1], rwkv_a2.shape[1], rwkv_g2.shape[1]
    wpad, apad = LANE, LANE
    gpad = -(-g_lora // LANE) * LANE
    rkv_col0 = 2 * dl
    lora0 = rkv_col0 + 3 * dr

    x2 = x.reshape(bsz * seq, d)
    for l in range(depth):
        mod = _mod(c, w_ada[l], b_ada[l].reshape(1, -1))
        sh_m, sc_m, g_m, sh_f, sc_f, g_f = [t.reshape(bsz, 1, d) for t in jnp.split(mod, 6, axis=-1)]

        wi = jnp.swapaxes(w_in[l], 0, 1)
        o1, o2 = lora0 + w_lora, lora0 + w_lora + a_lora
        w_lora_p = jnp.concatenate(
            [_pad_rows(wi[lora0:o1], wpad), _pad_rows(wi[o1:o2], apad),
             _pad_rows(wi[o2:], gpad)], axis=0)
        mu = rwkv_mu[l].reshape(1, -1)
        mu_rkv = mu[:, :3 * dr]
        mu_lora = jnp.concatenate(
            [_pad_cols(mu[:, 3 * dr:3 * dr + w_lora], wpad),
             _pad_cols(mu[:, 3 * dr + w_lora:3 * dr + w_lora + a_lora], apad),
             _pad_cols(mu[:, 3 * dr + w_lora + a_lora:], gpad)], axis=1)
        w2p = _pad_rows(rwkv_w2[l], wpad)
        a2p = _pad_rows(rwkv_a2[l], apad)
        g2p = _pad_rows(rwkv_g2[l], gpad)

        h = _norm(x2, norm_mix_g[l].reshape(1, d), sh_m, sc_m, seq)
        p = _mm_in(h, wi, lora0)
        p_lora = _mm_in(h, w_lora_p, w_lora_p.shape[0], tn=w_lora_p.shape[0], name="mm_lora")

        y_a = _lru(p, conv_w[l], conv_b[l], lru_wa[l].astype(BF16), lru_wx[l].astype(BF16),
                   lru_ba[l], lru_bx[l], lru_lambda[l], bsz, seq)

        rowv = lambda t: t.reshape(1, dr)
        rp, yp, mc, nm, bonus, gg = _rwkv_a(
            p, p_lora, mu_rkv, mu_lora, rowv(rwkv_w0[l]), rowv(rwkv_a0[l]), rowv(rwkv_k_k[l]),
            rowv(rwkv_k_a[l]), rowv(rwkv_r_k[l]), w2p, a2p, g2p, bsz, seq, rkv_col0)
        y_b = _rwkv_b(rp, yp, mc, nm, bonus, gg, rowv(rwkv_ln_g[l]), rowv(rwkv_ln_b[l]), bsz, seq)

        x2 = _mm_out(y_a, y_b, x2, g_m, w_out[l].astype(BF16), seq)

        last = l == depth - 1
        fg = final_norm_g.reshape(1, d) if last else None
        assert last, "only the final layer carries the closing RMSNorm"
        x2 = _ffn(x2, norm_ffn_g[l].reshape(1, d), sh_f, sc_f, g_f, w_gu[l].astype(BF16),
                  w_down[l].astype(BF16), fg, seq)
    return x2.reshape(bsz, seq, d)
```

```python
import functools

import jax
import jax.numpy as jnp
from jax import lax
from jax.experimental import pallas as pl
from jax.experimental.pallas import tpu as pltpu

F32 = jnp.float32
BF16 = jnp.bfloat16

LRU_HEADS = 4
CONV_WIDTH = 4
LRU_C = 8.0
HEAD = 64
CHUNK = 64
PAIR = 2 * HEAD
HEADS_PER_STEP = 8
ONES_WIDTH = 256
CHUNKS_PER_STEP = 2
RMS_EPS = 1e-6
GN_EPS = 64e-5
L2_EPS = 1e-12
LANE = 128
SUBLANE = 8
VMEM_LIMIT = 56 * 1024 * 1024


def _params(*sem):
    return pltpu.CompilerParams(dimension_semantics=sem, vmem_limit_bytes=VMEM_LIMIT)


_NN = (((1,), (0,)), ((), ()))
_NT = (((1,), (1,)), ((), ()))
_TN = (((0,), (0,)), ((), ()))


def _dg(a, b, dims):
    return lax.dot_general(a, b, dims, preferred_element_type=F32)


def _split(x):
    hi = x.astype(BF16)
    lo = (x - hi.astype(F32)).astype(BF16)
    return hi, lo


def _mm3(a, b, dims=_NN):
    ah, al = _split(a)
    bh, bl = _split(b)
    return _dg(ah, bh, dims) + (_dg(ah, bl, dims) + _dg(al, bh, dims))


def _mm3_presplit(a, bh, bl):
    ah, al = _split(a)
    return _dg(ah, bh, _NN) + (_dg(ah, bl, _NN) + _dg(al, bh, _NN))


def _mm2_exact_rhs(a, b_bf16):
    ah, al = _split(a)
    return _dg(ah, b_bf16, _NN) + _dg(al, b_bf16, _NN)


def _head_sums(x, ones_h):
    n = ones_h.shape[0]
    return jnp.concatenate([_mm2_exact_rhs(x[:, c:c + n], ones_h) for c in range(0, x.shape[1], n)],
                           axis=1)


def _mm2_exact_lhs(a_bf16, b):
    bh, bl = _split(b)
    return _dg(a_bf16, bh, _NN) + _dg(a_bf16, bl, _NN)


def _softplus(x):
    return jnp.maximum(x, 0.0) + jnp.log1p(jnp.exp(-jnp.abs(x)))


def _iota2(shape):
    return (lax.broadcasted_iota(jnp.int32, shape, 0),
            lax.broadcasted_iota(jnp.int32, shape, 1))


def _head_ones(n):
    r, c = _iota2((n, n))
    return jnp.where((r // HEAD) == (c // HEAD), 1.0, 0.0).astype(BF16)


def _mod_body(c_ref, w_ref, b_ref, o_ref):
    c = c_ref[...]
    ca = c * jax.nn.sigmoid(c)
    o_ref[...] = _mm3(ca, w_ref[...]) + b_ref[...]


def _mod(c, w, b, tn=1024):
    bsz, d = c.shape
    n = w.shape[1]
    return pl.pallas_call(
        _mod_body,
        grid=(n // tn,),
        in_specs=[pl.BlockSpec((bsz, d), lambda j: (0, 0)),
                  pl.BlockSpec((d, tn), lambda j: (0, j)),
                  pl.BlockSpec((1, tn), lambda j: (0, j))],
        out_specs=pl.BlockSpec((bsz, tn), lambda j: (0, j)),
        out_shape=jax.ShapeDtypeStruct((bsz, n), F32),
        compiler_params=_params("parallel"),
        name="mod",
    )(c, w, b)


def _norm_mod(x, g, sh, sc):
    y = x * lax.rsqrt(jnp.mean(x * x, axis=-1, keepdims=True) + RMS_EPS) * g
    return y * (1.0 + sc) + sh


def _norm_body(x_ref, g_ref, sh_ref, sc_ref, o_ref):
    o_ref[...] = _norm_mod(x_ref[...], g_ref[...], sh_ref[0], sc_ref[0]).astype(BF16)


def _norm(x2, g, sh, sc, seq, tm=512):
    m, d = x2.shape
    per_b = seq // tm
    return pl.pallas_call(
        _norm_body,
        grid=(m // tm,),
        in_specs=[pl.BlockSpec((tm, d), lambda i: (i, 0)),
                  pl.BlockSpec((1, d), lambda i: (0, 0)),
                  pl.BlockSpec((1, 1, d), lambda i: (i // per_b, 0, 0)),
                  pl.BlockSpec((1, 1, d), lambda i: (i // per_b, 0, 0))],
        out_specs=pl.BlockSpec((tm, d), lambda i: (i, 0)),
        out_shape=jax.ShapeDtypeStruct((m, d), BF16),
        compiler_params=_params("parallel"),
        name="norm_mix",
    )(x2, g, sh, sc)


def _mm_in_body(h_ref, w_ref, o_ref, wb_ref):
    @pl.when(pl.program_id(1) == 0)
    def _():
        wb_ref[...] = w_ref[...].astype(BF16)

    o_ref[...] = _dg(h_ref[...], wb_ref[...], _NT)


def _mm_in(h, wt, ncols, tm=1024, tn=1024, name="mm_in"):
    m, d = h.shape
    return pl.pallas_call(
        _mm_in_body,
        grid=(ncols // tn, m // tm),
        in_specs=[pl.BlockSpec((tm, d), lambda j, i: (i, 0)),
                  pl.BlockSpec((tn, d), lambda j, i: (j, 0))],
        out_specs=pl.BlockSpec((tm, tn), lambda j, i: (i, j)),
        out_shape=jax.ShapeDtypeStruct((m, ncols), F32),
        scratch_shapes=[pltpu.VMEM((tn, d), BF16)],
        compiler_params=_params("parallel", "arbitrary"),
        name=name,
    )(h, wt)


def _shift_rows(x, s, fill, row):
    return jnp.where(row < s, fill, pltpu.roll(x, s, 0))


def _lru_body(u_ref, gate_ref, halo_ref, cw_ref, cb_ref, wa_ref, wx_ref, ba_ref, bx_ref,
              lam_ref, o_ref, carry_ref, *, tt):
    ti = pl.program_id(1)
    first = ti == 0

    @pl.when(first)
    def _():
        carry_ref[...] = jnp.zeros_like(carry_ref)

    p = u_ref[...]
    dl = p.shape[1]
    halo = jnp.where(first, 0.0, halo_ref[...])
    ext = jnp.concatenate([halo, p], axis=0)
    cw = cw_ref[...]
    u = cb_ref[...] + p * cw[CONV_WIDTH - 1:CONV_WIDTH, :]
    for j in range(1, CONV_WIDTH):
        shifted = pltpu.roll(ext, j, 0)[SUBLANE:, :]
        u = u + shifted * cw[CONV_WIDTH - 1 - j:CONV_WIDTH - j, :]

    hd = dl // LRU_HEADS
    ub = u.astype(BF16)
    ra, rx = [], []
    for h in range(LRU_HEADS):
        uh = ub[:, h * hd:(h + 1) * hd]
        ra.append(jnp.dot(uh, wa_ref[h], preferred_element_type=F32))
        rx.append(jnp.dot(uh, wx_ref[h], preferred_element_type=F32))
    r = jax.nn.sigmoid(jnp.concatenate(ra, axis=1) + ba_ref[...])
    ig = jax.nn.sigmoid(jnp.concatenate(rx, axis=1) + bx_ref[...])
    a = jnp.exp(r * ((-LRU_C) * _softplus(-lam_ref[...])))
    mult = jnp.sqrt(1.0 - a * a)
    row = lax.broadcasted_iota(jnp.int32, (tt, dl), 0)
    mult = jnp.where(jnp.logical_and(first, row == 0), 1.0, mult)
    b = mult * (ig * u)

    groups = tt // SUBLANE
    a3 = a.reshape(groups, SUBLANE, dl)
    b3 = b.reshape(groups, SUBLANE, dl)
    sub = lax.broadcasted_iota(jnp.int32, (groups, SUBLANE, dl), 1)
    s = 1
    while s < SUBLANE:
        keep = sub >= s
        a_s = jnp.where(keep, pltpu.roll(a3, s, 1), 1.0)
        b_s = jnp.where(keep, pltpu.roll(b3, s, 1), 0.0)
        b3 = a3 * b_s + b3
        a3 = a3 * a_s
        s *= 2
    gate = jax.nn.gelu(gate_ref[...])
    carry = carry_ref[...]
    for g in range(groups):
        h = b3[g] + a3[g] * carry
        carry = h[SUBLANE - 1:SUBLANE, :]
        o_ref[g * SUBLANE:(g + 1) * SUBLANE, :] = h * gate[g * SUBLANE:(g + 1) * SUBLANE, :]
    carry_ref[...] = carry


def _lru(p, conv_w, conv_b, wa, wx, ba, bx, lam, bsz, seq, tt=256):
    dl = conv_w.shape[1]
    nt = seq // tt
    rows8 = tt // SUBLANE
    row = lambda v: v.reshape(1, dl)
    return pl.pallas_call(
        functools.partial(_lru_body, tt=tt),
        grid=(bsz, nt),
        in_specs=[pl.BlockSpec((tt, dl), lambda b, i: (b * nt + i, 0)),
                  pl.BlockSpec((tt, dl), lambda b, i: (b * nt + i, 1)),
                  pl.BlockSpec((SUBLANE, dl),
                               lambda b, i: (jnp.maximum((b * nt + i) * rows8 - 1, 0), 0)),
                  pl.BlockSpec((CONV_WIDTH, dl), lambda b, i: (0, 0)),
                  pl.BlockSpec((1, dl), lambda b, i: (0, 0)),
                  pl.BlockSpec(wa.shape, lambda b, i: (0, 0, 0)),
                  pl.BlockSpec(wx.shape, lambda b, i: (0, 0, 0)),
                  pl.BlockSpec((1, dl), lambda b, i: (0, 0)),
                  pl.BlockSpec((1, dl), lambda b, i: (0, 0)),
                  pl.BlockSpec((1, dl), lambda b, i: (0, 0))],
        out_specs=pl.BlockSpec((tt, dl), lambda b, i: (b * nt + i, 0)),
        out_shape=jax.ShapeDtypeStruct((bsz * seq, dl), F32),
        scratch_shapes=[pltpu.VMEM((1, dl), F32)],
        compiler_params=_params("parallel", "arbitrary"),
        name="lru",
    )(p, p, p, conv_w, row(conv_b), wa, wx, row(ba), row(bx), row(lam))


def _token_shift(x, halo, mu, first, row):
    prev = jnp.where(first, 0.0, halo[SUBLANE - 1:SUBLANE, :])
    xs = jnp.where(row == 0, prev, pltpu.roll(x, 1, 0))
    return x + (xs - x) * mu


def _mm1(a, b, dims=_NN):
    return _dg(a.astype(BF16), b.astype(BF16), dims)


def _pair_diag(y, left):
    return jnp.concatenate([jnp.where(left, y, 0.0), jnp.where(left, 0.0, y)], axis=0).astype(BF16)


def _pair_mm(x, y, left):
    return _dg(x.astype(BF16), _pair_diag(y, left), _NN)


def _tri_inverse(a_low, eye, r, c, left):
    base = 8
    same = (r // base) == (c // base)
    d = [jnp.where(same, a, 0.0) for a in a_low]
    d2 = [_pair_mm(t, t, left) for t in d]
    x = [eye + t for t in d]
    x = [xi + _pair_mm(t2, xi, left) for xi, t2 in zip(x, d2)]
    d4 = [_pair_mm(t2, t2, left) for t2 in d2]
    x = [xi + _pair_mm(t4, xi, left) for xi, t4 in zip(x, d4)]
    size = base
    while size < CHUNK:
        off = jnp.logical_and((r // (2 * size)) == (c // (2 * size)),
                              (r // size) != (c // size))
        o = [jnp.where(off, a, 0.0) for a in a_low]
        ox = [_pair_mm(oi, xi, left) for oi, xi in zip(o, x)]
        x = [xi + _pair_mm(xi, oxi, left) for xi, oxi in zip(x, ox)]
        size *= 2
    return x


def _rwkv_a_body(r_ref, k_ref, v_ref, l_ref, rh_ref, kh_ref, vh_ref, lh_ref,
                 mur_ref, muk_ref, muv_ref, mul_ref, w0_ref, a0_ref, kkw_ref, kaw_ref, rkw_ref,
                 w2h_ref, w2l_ref, a2h_ref, a2l_ref, g2h_ref, g2l_ref, ones_ref, tri_ref,
                 rp_ref, yp_ref, m_ref, n_ref, bonus_ref, g_ref):
    first = pl.program_id(1) == 0
    cl = CHUNK
    rows = CHUNKS_PER_STEP * cl
    width = HEADS_PER_STEP * HEAD
    row_w = lax.broadcasted_iota(jnp.int32, (rows, width), 0)
    row_l = lax.broadcasted_iota(jnp.int32, (rows, l_ref.shape[1]), 0)

    r = _token_shift(r_ref[...], rh_ref[...], mur_ref[...], first, row_w)
    k = _token_shift(k_ref[...], kh_ref[...], muk_ref[...], first, row_w)
    v = _token_shift(v_ref[...], vh_ref[...], muv_ref[...], first, row_w)
    lo = _token_shift(l_ref[...], lh_ref[...], mul_ref[...], first, row_l)
    wl = lo[:, 0:LANE]
    al = lo[:, LANE:2 * LANE]
    gl = lo[:, 2 * LANE:]

    w = -_softplus(-(w0_ref[...] + _mm3_presplit(jnp.tanh(wl), w2h_ref[...], w2l_ref[...]))) - 0.5
    lw = -jnp.exp(w)
    a = jax.nn.sigmoid(a0_ref[...] + _mm3_presplit(al, a2h_ref[...], a2l_ref[...]))
    g_ref[...] = _mm3_presplit(jax.nn.sigmoid(gl), g2h_ref[...], g2l_ref[...])

    ones_h = ones_ref[...]
    kk = k * kkw_ref[...]
    kk = kk / jnp.maximum(jnp.sqrt(_head_sums(kk * kk, ones_h)), L2_EPS)
    kp = k * (1.0 + (a - 1.0) * kaw_ref[...])
    bonus_ref[...] = _head_sums(r * kp * rkw_ref[...], ones_h) * v

    lc = _mm2_exact_lhs(tri_ref[...], lw)
    p_incl = jnp.exp(lc)
    p_excl = jnp.exp(lc - lw)
    p_inv = jnp.exp(-lc)
    p_end = jnp.concatenate(
        [jnp.broadcast_to(p_incl[(j + 1) * cl - 1:(j + 1) * cl, :], (cl, width))
         for j in range(CHUNKS_PER_STEP)], axis=0)

    abar = -(kk * p_excl)
    bbar = kk * a * p_inv
    kbar = kp * p_inv
    rbar = r * p_incl
    btil = bbar * p_end
    ktil = kbar * p_end

    units = [(j, q) for j in range(CHUNKS_PER_STEP) for q in range(width // PAIR)]

    def cut(x):
        return [x[j * cl:(j + 1) * cl, q * PAIR:(q + 1) * PAIR] for j, q in units]

    ab_, bb_, kb_, rb_, v_ = cut(abar), cut(bbar), cut(kbar), cut(rbar), cut(v)
    bt_, kt_, pe_ = cut(btil), cut(ktil), cut(p_end)

    rc, lane = _iota2((cl, PAIR))
    cc = lane % HEAD
    left = lane < HEAD
    strict = rc > cc
    incl = rc >= cc
    diag = rc == cc
    eye = jnp.where(diag, 1.0, 0.0)
    ab16 = [x.astype(BF16) for x in ab_]
    rb16 = [x.astype(BF16) for x in rb_]
    bd_b = [_pair_diag(x, left) for x in bb_]
    bd_k = [_pair_diag(x, left) for x in kb_]
    bd_v = [_pair_diag(x, left) for x in v_]
    a_ab = [jnp.where(strict, _dg(x, y, _NT), 0.0) for x, y in zip(ab16, bd_b)]
    a_ak = [jnp.where(strict, _dg(x, y, _NT), 0.0).astype(BF16) for x, y in zip(ab16, bd_k)]
    a_rb = [jnp.where(incl, _dg(x, y, _NT), 0.0).astype(BF16) for x, y in zip(rb16, bd_b)]
    a_rk = [jnp.where(incl, _dg(x, y, _NT), 0.0).astype(BF16) for x, y in zip(rb16, bd_k)]
    akv = [_dg(x, y, _NN) for x, y in zip(a_ak, bd_v)]
    t = [x.astype(BF16) for x in _tri_inverse(a_ab, eye, rc, cc, left)]
    wu = [_dg(x, jnp.concatenate([_pair_diag(y, left), _pair_diag(z, left)], axis=1), _NN)
          for x, y, z in zip(t, ab_, akv)]
    ry = [_dg(x, jnp.concatenate([_pair_diag(y[:, :PAIR], left), _pair_diag(y[:, PAIR:], left)], axis=1), _NN)
          for x, y in zip(a_rb, wu)]
    rkv = [_dg(x, y, _NN) for x, y in zip(a_rk, bd_v)]
    wu16 = [x.astype(BF16) for x in wu]
    mn = [_dg(x.astype(BF16), y, _TN) for x, y in zip(bt_, wu16)]
    kv = [_dg(x.astype(BF16), y.astype(BF16), _TN) for x, y in zip(kt_, v_)]

    def head_blocks(x):
        return jnp.where(left, x[:HEAD, :], x[HEAD:, :])

    for u, (j, q) in enumerate(units):
        rs = slice(j * cl, (j + 1) * cl)
        qs = slice(q * PAIR, (q + 1) * PAIR)
        rp_ref[rs, qs] = rb_[u] + ry[u][:, :PAIR]
        yp_ref[rs, qs] = ry[u][:, PAIR:] + rkv[u]
        m_ref[rs, qs] = jnp.where(diag, pe_[u], 0.0) + head_blocks(mn[u][:, :PAIR])
        n_ref[rs, qs] = head_blocks(mn[u][:, PAIR:]) + head_blocks(kv[u])


def _rwkv_a(p, p_lora, mu_rkv, mu_lora, w0, a0, k_k, k_a, r_k, w2p, a2p, g2p, bsz, seq, rkv_col0):
    cl = CHUNKS_PER_STEP * CHUNK
    width = HEADS_PER_STEP * HEAD
    dr = w0.shape[1]
    ngroups = dr // width
    nc = seq // cl
    lw_ = mu_lora.shape[1]
    cb0 = rkv_col0 // width
    rows8 = cl // SUBLANE
    rt, ct = _iota2((cl, cl))
    tri = jnp.where(jnp.logical_and(rt >= ct, (rt // CHUNK) == (ct // CHUNK)), 1.0, 0.0).astype(BF16)
    ones_h = _head_ones(ONES_WIDTH)
    const = lambda arr: pl.BlockSpec(arr.shape, lambda b, i, q: (0, 0))
    lora_w = [t for wgt in (w2p, a2p, g2p) for t in _split(wgt)]

    def tile(cb_off):
        return pl.BlockSpec((cl, width), lambda b, i, q: (b * nc + i, cb0 + cb_off + q))

    def halo(cb_off):
        return pl.BlockSpec(
            (SUBLANE, width),
            lambda b, i, q: (jnp.maximum((b * nc + i) * rows8 - 1, 0), cb0 + cb_off + q))

    def prow(off=0):
        return pl.BlockSpec((1, width), lambda b, i, q: (0, off + q))

    out_tile = pl.BlockSpec((cl, width), lambda b, i, q: (b * nc + i, q))
    out_mat = pl.BlockSpec((CHUNKS_PER_STEP * HEAD, width), lambda b, i, q: (b * nc + i, q))
    act = jax.ShapeDtypeStruct((bsz * seq, dr), F32)
    mat = jax.ShapeDtypeStruct((bsz * (seq // CHUNK) * HEAD, dr), F32)
    return pl.pallas_call(
        _rwkv_a_body,
        grid=(bsz, nc, ngroups),
        in_specs=[tile(0), tile(ngroups), tile(2 * ngroups),
                  pl.BlockSpec((cl, lw_), lambda b, i, q: (b * nc + i, 0)),
                  halo(0), halo(ngroups), halo(2 * ngroups),
                  pl.BlockSpec((SUBLANE, lw_),
                               lambda b, i, q: (jnp.maximum((b * nc + i) * rows8 - 1, 0), 0)),
                  prow(0), prow(ngroups), prow(2 * ngroups),
                  pl.BlockSpec((1, lw_), lambda b, i, q: (0, 0)),
                  prow(), prow(), prow(), prow(), prow()]
                 + [pl.BlockSpec((t.shape[0], width), lambda b, i, q: (0, q)) for t in lora_w]
                 + [const(ones_h), const(tri)],
        out_specs=[out_tile, out_tile, out_mat, out_mat, out_tile, out_tile],
        out_shape=[act, act, mat, mat, act, act],
        compiler_params=_params("parallel", "parallel", "parallel"),
        name="rwkv_a",
    )(p, p, p, p_lora, p, p, p, p_lora, mu_rkv, mu_rkv, mu_rkv, mu_lora, w0, a0, k_k, k_a, r_k,
      *lora_w, ones_h, tri)


def _rwkv_b_body(rp_ref, yp_ref, m_ref, n_ref, bonus_ref, g_ref, lng_ref, lnb_ref, ones_ref,
                 o_ref, state_ref):
    @pl.when(pl.program_id(1) == 0)
    def _():
        state_ref[...] = jnp.zeros_like(state_ref)

    npairs = state_ref.shape[0]
    pairs = range(npairs)
    ps = [slice(q * PAIR, (q + 1) * PAIR) for q in pairs]
    left = lax.broadcasted_iota(jnp.int32, (HEAD, PAIR), 1) < HEAD
    ones_h = ones_ref[...]
    inv_n = 1.0 / HEAD
    g0 = [_pair_diag(state_ref[q], left) for q in pairs]
    ys = [_dg(rp_ref[:, ps[q]].astype(BF16), g0[q], _NN) + yp_ref[:, ps[q]] for q in pairs]
    for q in pairs:
        state_ref[q] = _dg(m_ref[:, ps[q]].astype(BF16), g0[q], _NN) + n_ref[:, ps[q]]
    y = jnp.concatenate(ys, axis=1)
    yc = y - _head_sums(y, ones_h) * inv_n
    var = _head_sums(yc * yc, ones_h) * inv_n
    yn = yc * lax.rsqrt(var + GN_EPS) * lng_ref[...] + lnb_ref[...]
    o_ref[...] = (yn + bonus_ref[...]) * g_ref[...]


def _rwkv_b(rp, yp, mc, nm, bonus, g, ln_g, ln_b, bsz, seq):
    cl = CHUNK
    dr = rp.shape[1]
    nc = seq // cl
    tile = pl.BlockSpec((cl, dr), lambda b, i: (b * nc + i, 0))
    mat = pl.BlockSpec((HEAD, dr), lambda b, i: (b * nc + i, 0))
    prow = pl.BlockSpec((1, dr), lambda b, i: (0, 0))
    ones_h = _head_ones(ONES_WIDTH)
    return pl.pallas_call(
        _rwkv_b_body,
        grid=(bsz, nc),
        in_specs=[tile, tile, mat, mat, tile, tile, prow, prow,
                  pl.BlockSpec(ones_h.shape, lambda b, i: (0, 0))],
        out_specs=tile,
        out_shape=jax.ShapeDtypeStruct((bsz * seq, dr), F32),
        scratch_shapes=[pltpu.VMEM((dr // PAIR, HEAD, PAIR), F32)],
        compiler_params=_params("parallel", "arbitrary"),
        name="rwkv_b",
    )(rp, yp, mc, nm, bonus, g, ln_g, ln_b, ones_h)


def _mm_out_body(ya_ref, yb_ref, x_ref, gm_ref, w_ref, o_ref):
    da = ya_ref.shape[1]
    mix = (jnp.dot(ya_ref[...].astype(BF16), w_ref[:da, :], preferred_element_type=F32)
           + jnp.dot(yb_ref[...].astype(BF16), w_ref[da:, :], preferred_element_type=F32))
    o_ref[...] = x_ref[...] + gm_ref[0] * mix


def _mm_out(ya, yb, x2, gm, w, seq, tm=512):
    m, d = x2.shape
    per_b = seq // tm
    return pl.pallas_call(
        _mm_out_body,
        grid=(m // tm,),
        in_specs=[pl.BlockSpec((tm, ya.shape[1]), lambda i: (i, 0)),
                  pl.BlockSpec((tm, yb.shape[1]), lambda i: (i, 0)),
                  pl.BlockSpec((tm, d), lambda i: (i, 0)),
                  pl.BlockSpec((1, 1, d), lambda i: (i // per_b, 0, 0)),
                  pl.BlockSpec(w.shape, lambda i: (0, 0))],
        out_specs=pl.BlockSpec((tm, d), lambda i: (i, 0)),
        out_shape=jax.ShapeDtypeStruct((m, d), F32),
        compiler_params=_params("parallel"),
        name="mm_out",
    )(ya, yb, x2, gm, w)


def _ffn_body(x_ref, g_ref, sh_ref, sc_ref, gf_ref, wg_ref, wu_ref, wd_ref, fg_ref,
              o_ref, h_ref, acc_ref):
    f = pl.program_id(1)

    @pl.when(f == 0)
    def _():
        h_ref[...] = _norm_mod(x_ref[...], g_ref[...], sh_ref[0], sc_ref[0]).astype(BF16)
        acc_ref[...] = jnp.zeros_like(acc_ref)

    h = h_ref[...]
    gate = jnp.dot(h, wg_ref[...], preferred_element_type=F32)
    up = jnp.dot(h, wu_ref[...], preferred_element_type=F32)
    act = (gate * jax.nn.sigmoid(gate) * up).astype(BF16)
    acc_ref[...] += jnp.dot(act, wd_ref[...], preferred_element_type=F32)

    @pl.when(f == pl.num_programs(1) - 1)
    def _():
        y = x_ref[...] + gf_ref[0] * acc_ref[...]
        o_ref[...] = (y * lax.rsqrt(jnp.mean(y * y, axis=-1, keepdims=True) + RMS_EPS)
                      * fg_ref[...])


def _ffn(x1, g, sh, sc, gf, w_gu, w_down, fg, seq, tm=512, tf=512):
    m, d = x1.shape
    dff = w_down.shape[0]
    nf = dff // tf
    per_b = seq // tm
    brow = pl.BlockSpec((1, 1, d), lambda i, f: (i // per_b, 0, 0))
    prow = pl.BlockSpec((1, d), lambda i, f: (0, 0))
    return pl.pallas_call(
        _ffn_body,
        grid=(m // tm, nf),
        in_specs=[pl.BlockSpec((tm, d), lambda i, f: (i, 0)),
                  prow, brow, brow, brow,
                  pl.BlockSpec((d, tf), lambda i, f: (0, f)),
                  pl.BlockSpec((d, tf), lambda i, f: (0, nf + f)),
                  pl.BlockSpec((tf, d), lambda i, f: (f, 0)),
                  prow],
        out_specs=pl.BlockSpec((tm, d), lambda i, f: (i, 0)),
        out_shape=jax.ShapeDtypeStruct((m, d), F32),
        scratch_shapes=[pltpu.VMEM((tm, d), BF16), pltpu.VMEM((tm, d), F32)],
        compiler_params=_params("parallel", "arbitrary"),
        name="ffn",
    )(x1, g, sh, sc, gf, w_gu, w_gu, w_down, fg)


def _pad_cols(w, n):
    return jnp.pad(w, ((0, 0), (0, n - w.shape[1])))


def _pad_rows(w, n):
    return jnp.pad(w, ((0, n - w.shape[0]), (0, 0)))


def kernel(x, c, w_ada, b_ada, norm_mix_g, w_in, conv_w, conv_b, lru_wa, lru_ba, lru_wx, lru_bx, lru_lambda, rwkv_mu, rwkv_w0, rwkv_w2, rwkv_a0, rwkv_a2, rwkv_g2, rwkv_k_k, rwkv_k_a, rwkv_r_k, rwkv_ln_g, rwkv_ln_b, w_out, norm_ffn_g, w_gu, w_down, final_norm_g):
    bsz, seq, d = x.shape
    depth = w_ada.shape[0]
    dl = conv_w.shape[2]
    dr = rwkv_w0.shape[1]
    w_lora, a_lora, g_lora = rwkv_w2.shape[1], rwkv_a2.shape[1], rwkv_g2.shape[1]
    wpad, apad = LANE, LANE
    gpad = -(-g_lora // LANE) * LANE
    rkv_col0 = 2 * dl
    lora0 = rkv_col0 + 3 * dr

    x2 = x.reshape(bsz * seq, d)
    for l in range(depth):
        mod = _mod(c, w_ada[l], b_ada[l].reshape(1, -1))
        sh_m, sc_m, g_m, sh_f, sc_f, g_f = [t.reshape(bsz, 1, d) for t in jnp.split(mod, 6, axis=-1)]

        wi = jnp.swapaxes(w_in[l], 0, 1)
        o1, o2 = lora0 + w_lora, lora0 + w_lora + a_lora
        w_lora_p = jnp.concatenate(
            [_pad_rows(wi[lora0:o1], wpad), _pad_rows(wi[o1:o2], apad),
             _pad_rows(wi[o2:], gpad)], axis=0)
        mu = rwkv_mu[l].reshape(1, -1)
        mu_rkv = mu[:, :3 * dr]
        mu_lora = jnp.concatenate(
            [_pad_cols(mu[:, 3 * dr:3 * dr + w_lora], wpad),
             _pad_cols(mu[:, 3 * dr + w_lora:3 * dr + w_lora + a_lora], apad),
             _pad_cols(mu[:, 3 * dr + w_lora + a_lora:], gpad)], axis=1)
        w2p = _pad_rows(rwkv_w2[l], wpad)
        a2p = _pad_rows(rwkv_a2[l], apad)
        g2p = _pad_rows(rwkv_g2[l], gpad)

        h = _norm(x2, norm_mix_g[l].reshape(1, d), sh_m, sc_m, seq)
        p = _mm_in(h, wi, lora0)
        p_lora = _mm_in(h, w_lora_p, w_lora_p.shape[0], tn=w_lora_p.shape[0], name="mm_lora")

        y_a = _lru(p, conv_w[l], conv_b[l], lru_wa[l].astype(BF16), lru_wx[l].astype(BF16),
                   lru_ba[l], lru_bx[l], lru_lambda[l], bsz, seq)

        rowv = lambda t: t.reshape(1, dr)
        rp, yp, mc, nm, bonus, gg = _rwkv_a(
            p, p_lora, mu_rkv, mu_lora, rowv(rwkv_w0[l]), rowv(rwkv_a0[l]), rowv(rwkv_k_k[l]),
            rowv(rwkv_k_a[l]), rowv(rwkv_r_k[l]), w2p, a2p, g2p, bsz, seq, rkv_col0)
        y_b = _rwkv_b(rp, yp, mc, nm, bonus, gg, rowv(rwkv_ln_g[l]), rowv(rwkv_ln_b[l]), bsz, seq)

        x2 = _mm_out(y_a, y_b, x2, g_m, w_out[l].astype(BF16), seq)

        last = l == depth - 1
        fg = final_norm_g.reshape(1, d) if last else None
        assert last, "only the final layer carries the closing RMSNorm"
        x2 = _ffn(x2, norm_ffn_g[l].reshape(1, d), sh_f, sc_f, g_f, w_gu[l].astype(BF16),
                  w_down[l].astype(BF16), fg, seq)
    return x2.reshape(bsz, seq, d)
```

```python
import functools

import jax
import jax.numpy as jnp
from jax import lax
from jax.experimental import pallas as pl
from jax.experimental.pallas import tpu as pltpu

F32 = jnp.float32
BF16 = jnp.bfloat16

LRU_HEADS = 4
CONV_WIDTH = 4
LRU_C = 8.0
HEAD = 64
CHUNK = 64
PAIR = 2 * HEAD
HEADS_PER_STEP = 16
ONES_WIDTH = 256
CHUNKS_PER_STEP = 2
RWKV_B_CHUNKS = 4
RMS_EPS = 1e-6
GN_EPS = 64e-5
L2_EPS = 1e-12
LANE = 128
SUBLANE = 8
VMEM_LIMIT = 56 * 1024 * 1024


def _params(*sem):
    return pltpu.CompilerParams(dimension_semantics=sem, vmem_limit_bytes=VMEM_LIMIT)


_NN = (((1,), (0,)), ((), ()))
_NT = (((1,), (1,)), ((), ()))
_TN = (((0,), (0,)), ((), ()))


def _dg(a, b, dims):
    return lax.dot_general(a, b, dims, preferred_element_type=F32)


def _split(x):
    hi = x.astype(BF16)
    lo = (x - hi.astype(F32)).astype(BF16)
    return hi, lo


def _mm3(a, b, dims=_NN):
    ah, al = _split(a)
    bh, bl = _split(b)
    return _dg(ah, bh, dims) + (_dg(ah, bl, dims) + _dg(al, bh, dims))


def _mm3_presplit(a, bh, bl):
    ah, al = _split(a)
    return _dg(ah, bh, _NN) + (_dg(ah, bl, _NN) + _dg(al, bh, _NN))


def _mm2_exact_rhs(a, b_bf16):
    ah, al = _split(a)
    return _dg(ah, b_bf16, _NN) + _dg(al, b_bf16, _NN)


def _head_sums(x, ones_h):
    n = ones_h.shape[0]
    return jnp.concatenate([_mm2_exact_rhs(x[:, c:c + n], ones_h) for c in range(0, x.shape[1], n)],
                           axis=1)


def _mm2_exact_lhs(a_bf16, b):
    bh, bl = _split(b)
    return _dg(a_bf16, bh, _NN) + _dg(a_bf16, bl, _NN)


def _softplus(x):
    return jnp.maximum(x, 0.0) + jnp.log1p(jnp.exp(-jnp.abs(x)))


def _iota2(shape):
    return (lax.broadcasted_iota(jnp.int32, shape, 0),
            lax.broadcasted_iota(jnp.int32, shape, 1))


def _head_ones(n):
    r, c = _iota2((n, n))
    return jnp.where((r // HEAD) == (c // HEAD), 1.0, 0.0).astype(BF16)


def _mod_body(c_ref, w_ref, b_ref, o_ref):
    c = c_ref[...]
    ca = c * jax.nn.sigmoid(c)
    o_ref[...] = _mm3(ca, w_ref[...]) + b_ref[...]


def _mod(c, w, b, tn=1024):
    bsz, d = c.shape
    n = w.shape[1]
    return pl.pallas_call(
        _mod_body,
        grid=(n // tn,),
        in_specs=[pl.BlockSpec((bsz, d), lambda j: (0, 0)),
                  pl.BlockSpec((d, tn), lambda j: (0, j)),
                  pl.BlockSpec((1, tn), lambda j: (0, j))],
        out_specs=pl.BlockSpec((bsz, tn), lambda j: (0, j)),
        out_shape=jax.ShapeDtypeStruct((bsz, n), F32),
        compiler_params=_params("parallel"),
        name="mod",
    )(c, w, b)


def _norm_mod(x, g, sh, sc):
    y = x * lax.rsqrt(jnp.mean(x * x, axis=-1, keepdims=True) + RMS_EPS) * g
    return y * (1.0 + sc) + sh


def _norm_body(x_ref, g_ref, sh_ref, sc_ref, o_ref):
    o_ref[...] = _norm_mod(x_ref[...], g_ref[...], sh_ref[0], sc_ref[0]).astype(BF16)


def _norm(x2, g, sh, sc, seq, tm=512):
    m, d = x2.shape
    per_b = seq // tm
    return pl.pallas_call(
        _norm_body,
        grid=(m // tm,),
        in_specs=[pl.BlockSpec((tm, d), lambda i: (i, 0)),
                  pl.BlockSpec((1, d), lambda i: (0, 0)),
                  pl.BlockSpec((1, 1, d), lambda i: (i // per_b, 0, 0)),
                  pl.BlockSpec((1, 1, d), lambda i: (i // per_b, 0, 0))],
        out_specs=pl.BlockSpec((tm, d), lambda i: (i, 0)),
        out_shape=jax.ShapeDtypeStruct((m, d), BF16),
        compiler_params=_params("parallel"),
        name="norm_mix",
    )(x2, g, sh, sc)


def _mm_in_body(h_ref, w_ref, o_ref, wb_ref):
    @pl.when(pl.program_id(1) == 0)
    def _():
        wb_ref[...] = w_ref[...].astype(BF16)

    o_ref[...] = _dg(h_ref[...], wb_ref[...], _NT)


def _mm_in(h, wt, ncols, tm=1024, tn=1024, name="mm_in"):
    m, d = h.shape
    return pl.pallas_call(
        _mm_in_body,
        grid=(ncols // tn, m // tm),
        in_specs=[pl.BlockSpec((tm, d), lambda j, i: (i, 0)),
                  pl.BlockSpec((tn, d), lambda j, i: (j, 0))],
        out_specs=pl.BlockSpec((tm, tn), lambda j, i: (i, j)),
        out_shape=jax.ShapeDtypeStruct((m, ncols), F32),
        scratch_shapes=[pltpu.VMEM((tn, d), BF16)],
        compiler_params=_params("parallel", "arbitrary"),
        name=name,
    )(h, wt)


def _shift_rows(x, s, fill, row):
    return jnp.where(row < s, fill, pltpu.roll(x, s, 0))


def _lru_body(u_ref, gate_ref, halo_ref, cw_ref, cb_ref, wa_ref, wx_ref, ba_ref, bx_ref,
              lam_ref, o_ref, carry_ref, *, tt):
    ti = pl.program_id(1)
    first = ti == 0

    @pl.when(first)
    def _():
        carry_ref[...] = jnp.zeros_like(carry_ref)

    p = u_ref[...]
    dl = p.shape[1]
    halo = jnp.where(first, 0.0, halo_ref[...])
    ext = jnp.concatenate([halo, p], axis=0)
    cw = cw_ref[...]
    u = cb_ref[...] + p * cw[CONV_WIDTH - 1:CONV_WIDTH, :]
    for j in range(1, CONV_WIDTH):
        shifted = pltpu.roll(ext, j, 0)[SUBLANE:, :]
        u = u + shifted * cw[CONV_WIDTH - 1 - j:CONV_WIDTH - j, :]

    hd = dl // LRU_HEADS
    ub = u.astype(BF16)
    ra, rx = [], []
    for h in range(LRU_HEADS):
        uh = ub[:, h * hd:(h + 1) * hd]
        ra.append(jnp.dot(uh, wa_ref[h], preferred_element_type=F32))
        rx.append(jnp.dot(uh, wx_ref[h], preferred_element_type=F32))
    r = jax.nn.sigmoid(jnp.concatenate(ra, axis=1) + ba_ref[...])
    ig = jax.nn.sigmoid(jnp.concatenate(rx, axis=1) + bx_ref[...])
    a = jnp.exp(r * ((-LRU_C) * _softplus(-lam_ref[...])))
    mult = jnp.sqrt(1.0 - a * a)
    row = lax.broadcasted_iota(jnp.int32, (tt, dl), 0)
    mult = jnp.where(jnp.logical_and(first, row == 0), 1.0, mult)
    b = mult * (ig * u)

    groups = tt // SUBLANE
    a3 = a.reshape(groups, SUBLANE, dl)
    b3 = b.reshape(groups, SUBLANE, dl)
    sub = lax.broadcasted_iota(jnp.int32, (groups, SUBLANE, dl), 1)
    s = 1
    while s < SUBLANE:
        keep = sub >= s
        a_s = jnp.where(keep, pltpu.roll(a3, s, 1), 1.0)
        b_s = jnp.where(keep, pltpu.roll(b3, s, 1), 0.0)
        b3 = a3 * b_s + b3
        a3 = a3 * a_s
        s *= 2
    gate = jax.nn.gelu(gate_ref[...])
    carry = carry_ref[...]
    for g in range(groups):
        h = b3[g] + a3[g] * carry
        carry = h[SUBLANE - 1:SUBLANE, :]
        o_ref[g * SUBLANE:(g + 1) * SUBLANE, :] = h * gate[g * SUBLANE:(g + 1) * SUBLANE, :]
    carry_ref[...] = carry


def _lru(p, conv_w, conv_b, wa, wx, ba, bx, lam, bsz, seq, tt=256):
    dl = conv_w.shape[1]
    nt = seq // tt
    rows8 = tt // SUBLANE
    row = lambda v: v.reshape(1, dl)
    return pl.pallas_call(
        functools.partial(_lru_body, tt=tt),
        grid=(bsz, nt),
        in_specs=[pl.BlockSpec((tt, dl), lambda b, i: (b * nt + i, 0)),
                  pl.BlockSpec((tt, dl), lambda b, i: (b * nt + i, 1)),
                  pl.BlockSpec((SUBLANE, dl),
                               lambda b, i: (jnp.maximum((b * nt + i) * rows8 - 1, 0), 0)),
                  pl.BlockSpec((CONV_WIDTH, dl), lambda b, i: (0, 0)),
                  pl.BlockSpec((1, dl), lambda b, i: (0, 0)),
                  pl.BlockSpec(wa.shape, lambda b, i: (0, 0, 0)),
                  pl.BlockSpec(wx.shape, lambda b, i: (0, 0, 0)),
                  pl.BlockSpec((1, dl), lambda b, i: (0, 0)),
                  pl.BlockSpec((1, dl), lambda b, i: (0, 0)),
                  pl.BlockSpec((1, dl), lambda b, i: (0, 0))],
        out_specs=pl.BlockSpec((tt, dl), lambda b, i: (b * nt + i, 0)),
        out_shape=jax.ShapeDtypeStruct((bsz * seq, dl), F32),
        scratch_shapes=[pltpu.VMEM((1, dl), F32)],
        compiler_params=_params("parallel", "arbitrary"),
        name="lru",
    )(p, p, p, conv_w, row(conv_b), wa, wx, row(ba), row(bx), row(lam))


def _token_shift(x, halo, mu, first, row):
    prev = jnp.where(first, 0.0, halo[SUBLANE - 1:SUBLANE, :])
    xs = jnp.where(row == 0, prev, pltpu.roll(x, 1, 0))
    return x + (xs - x) * mu


def _mm1(a, b, dims=_NN):
    return _dg(a.astype(BF16), b.astype(BF16), dims)


def _pair_diag(y, left):
    return jnp.concatenate([jnp.where(left, y, 0.0), jnp.where(left, 0.0, y)], axis=0).astype(BF16)


def _pair_mm(x, y, left):
    return _dg(x.astype(BF16), _pair_diag(y, left), _NN)


def _chunk_chain(ops, store):
    ab_, bb_, kb_, rb_, v_, bt_, kt_, pe_ = ops
    rc, lane = _iota2((CHUNK, PAIR))
    cc = lane % HEAD
    left = lane < HEAD
    strict = rc > cc
    incl = rc >= cc
    diag = rc == cc
    ab16 = [x.astype(BF16) for x in ab_]
    rb16 = [x.astype(BF16) for x in rb_]
    bd_b = [_pair_diag(x, left) for x in bb_]
    bd_k = [_pair_diag(x, left) for x in kb_]
    bd_v = [_pair_diag(x, left) for x in v_]
    a_ab = [jnp.where(strict, _dg(x, y, _NT), 0.0) for x, y in zip(ab16, bd_b)]
    a_ak = [jnp.where(strict, _dg(x, y, _NT), 0.0).astype(BF16) for x, y in zip(ab16, bd_k)]
    a_rb = [jnp.where(incl, _dg(x, y, _NT), 0.0).astype(BF16) for x, y in zip(rb16, bd_b)]
    a_rk = [jnp.where(incl, _dg(x, y, _NT), 0.0).astype(BF16) for x, y in zip(rb16, bd_k)]
    yield
    base = 8
    d = [jnp.where((rc // base) == (cc // base), a, 0.0) for a in a_ab]
    d2 = [_pair_mm(t, t, left) for t in d]
    akv = [_dg(x, y, _NN) for x, y in zip(a_ak, bd_v)]
    x = [jnp.where(diag, 1.0, 0.0) + t for t in d]
    yield
    x = [xi + _pair_mm(t2, xi, left) for xi, t2 in zip(x, d2)]
    d4 = [_pair_mm(t2, t2, left) for t2 in d2]
    yield
    x = [xi + _pair_mm(t4, xi, left) for xi, t4 in zip(x, d4)]
    yield
    size = base
    while size < CHUNK:
        off = jnp.logical_and((rc // (2 * size)) == (cc // (2 * size)),
                              (rc // size) != (cc // size))
        o = [jnp.where(off, a, 0.0) for a in a_ab]
        ox = [_pair_mm(oi, xi, left) for oi, xi in zip(o, x)]
        yield
        x = [xi + _pair_mm(xi, oxi, left) for xi, oxi in zip(x, ox)]
        yield
        size *= 2
    t = [xi.astype(BF16) for xi in x]
    wu = [_dg(ti, jnp.concatenate([_pair_diag(y, left), _pair_diag(z, left)], axis=1), _NN)
          for ti, y, z in zip(t, ab_, akv)]
    rkv = [_dg(xi, y, _NN) for xi, y in zip(a_rk, bd_v)]
    kv = [_dg(xi.astype(BF16), y.astype(BF16), _TN) for xi, y in zip(kt_, v_)]
    yield
    ry = [_dg(xi, jnp.concatenate([_pair_diag(y[:, :PAIR], left), _pair_diag(y[:, PAIR:], left)], axis=1), _NN)
          for xi, y in zip(a_rb, wu)]
    mn = [_dg(xi.astype(BF16), y.astype(BF16), _TN) for xi, y in zip(bt_, wu)]
    yield

    def head_blocks(z):
        return jnp.where(left, z[:HEAD, :], z[HEAD:, :])

    for u in range(len(ab_)):
        store(u,
              rb_[u] + ry[u][:, :PAIR],
              ry[u][:, PAIR:] + rkv[u],
              jnp.where(diag, pe_[u], 0.0) + head_blocks(mn[u][:, :PAIR]),
              head_blocks(mn[u][:, PAIR:]) + head_blocks(kv[u]))


def _rwkv_a_body(r_ref, k_ref, v_ref, l_ref, rh_ref, kh_ref, vh_ref, lh_ref,
                 mur_ref, muk_ref, muv_ref, mul_ref, w0_ref, a0_ref, kkw_ref, kaw_ref, rkw_ref,
                 w2h_ref, w2l_ref, a2h_ref, a2l_ref, g2h_ref, g2l_ref, ones_ref, tri_ref,
                 rp_ref, yp_ref, m_ref, n_ref, bonus_ref, g_ref):
    first = pl.program_id(1) == 0
    cl = CHUNK
    rows = CHUNKS_PER_STEP * cl
    width = HEADS_PER_STEP * HEAD
    gw = ones_ref.shape[0]
    row_g = lax.broadcasted_iota(jnp.int32, (rows, gw), 0)
    row_l = lax.broadcasted_iota(jnp.int32, (rows, l_ref.shape[1]), 0)
    ones_h = ones_ref[...]

    lo = _token_shift(l_ref[...], lh_ref[...], mul_ref[...], first, row_l)
    act_w = _split(jnp.tanh(lo[:, 0:LANE]))
    act_a = _split(lo[:, LANE:2 * LANE])
    act_g = _split(jax.nn.sigmoid(lo[:, 2 * LANE:]))

    def lora(act, wh_ref, wl_ref, cs):
        (ah, al_), bh, bl = act, wh_ref[:, cs], wl_ref[:, cs]
        return _dg(ah, bh, _NN) + (_dg(ah, bl, _NN) + _dg(al_, bh, _NN))

    def prologue(c0, out):
        cs = slice(c0, c0 + gw)
        r = _token_shift(r_ref[:, cs], rh_ref[:, cs], mur_ref[:, cs], first, row_g)
        k = _token_shift(k_ref[:, cs], kh_ref[:, cs], muk_ref[:, cs], first, row_g)
        v = _token_shift(v_ref[:, cs], vh_ref[:, cs], muv_ref[:, cs], first, row_g)
        w_lin = w0_ref[:, cs] + lora(act_w, w2h_ref, w2l_ref, cs)
        a_lin = a0_ref[:, cs] + lora(act_a, a2h_ref, a2l_ref, cs)
        g_ref[:, cs] = lora(act_g, g2h_ref, g2l_ref, cs)
        kk = k * kkw_ref[:, cs]
        kk_ss = _mm2_exact_rhs(kk * kk, ones_h)
        yield
        w = -_softplus(-w_lin) - 0.5
        lw = -jnp.exp(w)
        a = jax.nn.sigmoid(a_lin)
        kk = kk / jnp.maximum(jnp.sqrt(kk_ss), L2_EPS)
        kp = k * (1.0 + (a - 1.0) * kaw_ref[:, cs])
        bonus_ref[:, cs] = _mm2_exact_rhs(r * kp * rkw_ref[:, cs], ones_h) * v
        lc = _mm2_exact_lhs(tri_ref[...], lw)
        yield
        p_incl = jnp.exp(lc)
        p_excl = jnp.exp(lc - lw)
        p_inv = jnp.exp(-lc)
        p_end = jnp.concatenate(
            [jnp.broadcast_to(p_incl[(j + 1) * cl - 1:(j + 1) * cl, :], (cl, gw))
             for j in range(CHUNKS_PER_STEP)], axis=0)
        abar = -(kk * p_excl)
        bbar = kk * a * p_inv
        kbar = kp * p_inv
        rbar = r * p_incl
        btil = bbar * p_end
        ktil = kbar * p_end
        units = [(j, q) for j in range(CHUNKS_PER_STEP) for q in range(gw // PAIR)]
        out.extend([x[j * cl:(j + 1) * cl, q * PAIR:(q + 1) * PAIR] for j, q in units]
                   for x in (abar, bbar, kbar, rbar, v, btil, ktil, p_end))
        yield

    def make_store(c0):
        units = [(j, q) for j in range(CHUNKS_PER_STEP) for q in range(gw // PAIR)]

        def store(u, rp, yp, mm, nn):
            j, q = units[u]
            rs = slice(j * cl, (j + 1) * cl)
            qs = slice(c0 + q * PAIR, c0 + (q + 1) * PAIR)
            rp_ref[rs, qs] = rp
            yp_ref[rs, qs] = yp
            m_ref[rs, qs] = mm
            n_ref[rs, qs] = nn
        return store

    chains = []
    for c0 in range(0, width, gw):
        ops = []
        for _ in prologue(c0, ops):
            for ch in chains:
                next(ch, None)
        chains.append(_chunk_chain(ops, make_store(c0)))
    live = list(chains)
    while live:
        live = [ch for ch in live if next(ch, StopIteration) is not StopIteration]


def _rwkv_a(p, p_lora, mu_rkv, mu_lora, w0, a0, k_k, k_a, r_k, w2p, a2p, g2p, bsz, seq, rkv_col0):
    cl = CHUNKS_PER_STEP * CHUNK
    width = HEADS_PER_STEP * HEAD
    dr = w0.shape[1]
    ngroups = dr // width
    nc = seq // cl
    lw_ = mu_lora.shape[1]
    cb0 = rkv_col0 // width
    rows8 = cl // SUBLANE
    rt, ct = _iota2((cl, cl))
    tri = jnp.where(jnp.logical_and(rt >= ct, (rt // CHUNK) == (ct // CHUNK)), 1.0, 0.0).astype(BF16)
    ones_h = _head_ones(ONES_WIDTH)
    const = lambda arr: pl.BlockSpec(arr.shape, lambda b, i, q: (0, 0))
    lora_w = [t for wgt in (w2p, a2p, g2p) for t in _split(wgt)]

    def tile(cb_off):
        return pl.BlockSpec((cl, width), lambda b, i, q: (b * nc + i, cb0 + cb_off + q))

    def halo(cb_off):
        return pl.BlockSpec(
            (SUBLANE, width),
            lambda b, i, q: (jnp.maximum((b * nc + i) * rows8 - 1, 0), cb0 + cb_off + q))

    def prow(off=0):
        return pl.BlockSpec((1, width), lambda b, i, q: (0, off + q))

    out_tile = pl.BlockSpec((cl, width), lambda b, i, q: (b * nc + i, q))
    out_mat = pl.BlockSpec((CHUNKS_PER_STEP * HEAD, width), lambda b, i, q: (b * nc + i, q))
    act = jax.ShapeDtypeStruct((bsz * seq, dr), F32)
    mat = jax.ShapeDtypeStruct((bsz * (seq // CHUNK) * HEAD, dr), F32)
    return pl.pallas_call(
        _rwkv_a_body,
        grid=(bsz, nc, ngroups),
        in_specs=[tile(0), tile(ngroups), tile(2 * ngroups),
                  pl.BlockSpec((cl, lw_), lambda b, i, q: (b * nc + i, 0)),
                  halo(0), halo(ngroups), halo(2 * ngroups),
                  pl.BlockSpec((SUBLANE, lw_),
                               lambda b, i, q: (jnp.maximum((b * nc + i) * rows8 - 1, 0), 0)),
                  prow(0), prow(ngroups), prow(2 * ngroups),
                  pl.BlockSpec((1, lw_), lambda b, i, q: (0, 0)),
                  prow(), prow(), prow(), prow(), prow()]
                 + [pl.BlockSpec((t.shape[0], width), lambda b, i, q: (0, q)) for t in lora_w]
                 + [const(ones_h), const(tri)],
        out_specs=[out_tile, out_tile, out_mat, out_mat, out_tile, out_tile],
        out_shape=[act, act, mat, mat, act, act],
        compiler_params=_params("parallel", "parallel", "parallel"),
        name="rwkv_a",
    )(p, p, p, p_lora, p, p, p, p_lora, mu_rkv, mu_rkv, mu_rkv, mu_lora, w0, a0, k_k, k_a, r_k,
      *lora_w, ones_h, tri)


def _rwkv_b_body(rp_ref, yp_ref, m_ref, n_ref, bonus_ref, g_ref, lng_ref, lnb_ref, ones_ref,
                 o_ref, state_ref):
    @pl.when(pl.program_id(1) == 0)
    def _():
        state_ref[...] = jnp.zeros_like(state_ref)

    npairs = state_ref.shape[0]
    pairs = range(npairs)
    ps = [slice(q * PAIR, (q + 1) * PAIR) for q in pairs]
    left = lax.broadcasted_iota(jnp.int32, (HEAD, PAIR), 1) < HEAD
    ones_h = ones_ref[...]
    inv_n = 1.0 / HEAD
    state = [state_ref[q] for q in pairs]
    for j in range(rp_ref.shape[0] // CHUNK):
        rs = slice(j * CHUNK, (j + 1) * CHUNK)
        ks = slice(j * HEAD, (j + 1) * HEAD)
        g0 = [_pair_diag(state[q], left) for q in pairs]
        ys = [_dg(rp_ref[rs, ps[q]].astype(BF16), g0[q], _NN) + yp_ref[rs, ps[q]] for q in pairs]
        state = [_dg(m_ref[ks, ps[q]].astype(BF16), g0[q], _NN) + n_ref[ks, ps[q]] for q in pairs]
        y = jnp.concatenate(ys, axis=1)
        yc = y - _head_sums(y, ones_h) * inv_n
        var = _head_sums(yc * yc, ones_h) * inv_n
        yn = yc * lax.rsqrt(var + GN_EPS) * lng_ref[...] + lnb_ref[...]
        o_ref[rs, :] = (yn + bonus_ref[rs, :]) * g_ref[rs, :]
    for q in pairs:
        state_ref[q] = state[q]


def _rwkv_b(rp, yp, mc, nm, bonus, g, ln_g, ln_b, bsz, seq):
    cl = RWKV_B_CHUNKS * CHUNK
    dr = rp.shape[1]
    nc = seq // cl
    tile = pl.BlockSpec((cl, dr), lambda b, i: (b * nc + i, 0))
    mat = pl.BlockSpec((RWKV_B_CHUNKS * HEAD, dr), lambda b, i: (b * nc + i, 0))
    prow = pl.BlockSpec((1, dr), lambda b, i: (0, 0))
    ones_h = _head_ones(ONES_WIDTH)
    return pl.pallas_call(
        _rwkv_b_body,
        grid=(bsz, nc),
        in_specs=[tile, tile, mat, mat, tile, tile, prow, prow,
                  pl.BlockSpec(ones_h.shape, lambda b, i: (0, 0))],
        out_specs=tile,
        out_shape=jax.ShapeDtypeStruct((bsz * seq, dr), F32),
        scratch_shapes=[pltpu.VMEM((dr // PAIR, HEAD, PAIR), F32)],
        compiler_params=_params("parallel", "arbitrary"),
        name="rwkv_b",
    )(rp, yp, mc, nm, bonus, g, ln_g, ln_b, ones_h)


def _mm_out_body(ya_ref, yb_ref, x_ref, gm_ref, w_ref, o_ref):
    da = ya_ref.shape[1]
    mix = (jnp.dot(ya_ref[...].astype(BF16), w_ref[:da, :], preferred_element_type=F32)
           + jnp.dot(yb_ref[...].astype(BF16), w_ref[da:, :], preferred_element_type=F32))
    o_ref[...] = x_ref[...] + gm_ref[0] * mix


def _mm_out(ya, yb, x2, gm, w, seq, tm=512):
    m, d = x2.shape
    per_b = seq // tm
    return pl.pallas_call(
        _mm_out_body,
        grid=(m // tm,),
        in_specs=[pl.BlockSpec((tm, ya.shape[1]), lambda i: (i, 0)),
                  pl.BlockSpec((tm, yb.shape[1]), lambda i: (i, 0)),
                  pl.BlockSpec((tm, d), lambda i: (i, 0)),
                  pl.BlockSpec((1, 1, d), lambda i: (i // per_b, 0, 0)),
                  pl.BlockSpec(w.shape, lambda i: (0, 0))],
        out_specs=pl.BlockSpec((tm, d), lambda i: (i, 0)),
        out_shape=jax.ShapeDtypeStruct((m, d), F32),
        compiler_params=_params("parallel"),
        name="mm_out",
    )(ya, yb, x2, gm, w)


def _ffn_body(x_ref, g_ref, sh_ref, sc_ref, gf_ref, wg_ref, wu_ref, wd_ref, fg_ref,
              o_ref, h_ref, acc_ref):
    f = pl.program_id(1)

    @pl.when(f == 0)
    def _():
        h_ref[...] = _norm_mod(x_ref[...], g_ref[...], sh_ref[0], sc_ref[0]).astype(BF16)
        acc_ref[...] = jnp.zeros_like(acc_ref)

    h = h_ref[...]
    gate = jnp.dot(h, wg_ref[...], preferred_element_type=F32)
    up = jnp.dot(h, wu_ref[...], preferred_element_type=F32)
    act = (gate * jax.nn.sigmoid(gate) * up).astype(BF16)
    acc_ref[...] += jnp.dot(act, wd_ref[...], preferred_element_type=F32)

    @pl.when(f == pl.num_programs(1) - 1)
    def _():
        y = x_ref[...] + gf_ref[0] * acc_ref[...]
        o_ref[...] = (y * lax.rsqrt(jnp.mean(y * y, axis=-1, keepdims=True) + RMS_EPS)
                      * fg_ref[...])


def _ffn(x1, g, sh, sc, gf, w_gu, w_down, fg, seq, tm=512, tf=512):
    m, d = x1.shape
    dff = w_down.shape[0]
    nf = dff // tf
    per_b = seq // tm
    brow = pl.BlockSpec((1, 1, d), lambda i, f: (i // per_b, 0, 0))
    prow = pl.BlockSpec((1, d), lambda i, f: (0, 0))
    return pl.pallas_call(
        _ffn_body,
        grid=(m // tm, nf),
        in_specs=[pl.BlockSpec((tm, d), lambda i, f: (i, 0)),
                  prow, brow, brow, brow,
                  pl.BlockSpec((d, tf), lambda i, f: (0, f)),
                  pl.BlockSpec((d, tf), lambda i, f: (0, nf + f)),
                  pl.BlockSpec((tf, d), lambda i, f: (f, 0)),
                  prow],
        out_specs=pl.BlockSpec((tm, d), lambda i, f: (i, 0)),
        out_shape=jax.ShapeDtypeStruct((m, d), F32),
        scratch_shapes=[pltpu.VMEM((tm, d), BF16), pltpu.VMEM((tm, d), F32)],
        compiler_params=_params("parallel", "arbitrary"),
        name="ffn",
    )(x1, g, sh, sc, gf, w_gu, w_gu, w_down, fg)


def _pad_cols(w, n):
    return jnp.pad(w, ((0, 0), (0, n - w.shape[1])))


def _pad_rows(w, n):
    return jnp.pad(w, ((0, n - w.shape[0]), (0, 0)))


def kernel(x, c, w_ada, b_ada, norm_mix_g, w_in, conv_w, conv_b, lru_wa, lru_ba, lru_wx, lru_bx, lru_lambda, rwkv_mu, rwkv_w0, rwkv_w2, rwkv_a0, rwkv_a2, rwkv_g2, rwkv_k_k, rwkv_k_a, rwkv_r_k, rwkv_ln_g, rwkv_ln_b, w_out, norm_ffn_g, w_gu, w_down, final_norm_g):
    bsz, seq, d = x.shape
    depth = w_ada.shape[0]
    dl = conv_w.shape[2]
    dr = rwkv_w0.shape[1]
    w_lora, a_lora, g_lora = rwkv_w2.shape[1], rwkv_a2.shape[1], rwkv_g2.shape[1]
    wpad, apad = LANE, LANE
    gpad = -(-g_lora // LANE) * LANE
    rkv_col0 = 2 * dl
    lora0 = rkv_col0 + 3 * dr

    x2 = x.reshape(bsz * seq, d)
    for l in range(depth):
        mod = _mod(c, w_ada[l], b_ada[l].reshape(1, -1))
        sh_m, sc_m, g_m, sh_f, sc_f, g_f = [t.reshape(bsz, 1, d) for t in jnp.split(mod, 6, axis=-1)]

        wi = jnp.swapaxes(w_in[l], 0, 1)
        o1, o2 = lora0 + w_lora, lora0 + w_lora + a_lora
        w_lora_p = jnp.concatenate(
            [_pad_rows(wi[lora0:o1], wpad), _pad_rows(wi[o1:o2], apad),
             _pad_rows(wi[o2:], gpad)], axis=0)
        mu = rwkv_mu[l].reshape(1, -1)
        mu_rkv = mu[:, :3 * dr]
        mu_lora = jnp.concatenate(
            [_pad_cols(mu[:, 3 * dr:3 * dr + w_lora], wpad),
             _pad_cols(mu[:, 3 * dr + w_lora:3 * dr + w_lora + a_lora], apad),
             _pad_cols(mu[:, 3 * dr + w_lora + a_lora:], gpad)], axis=1)
        w2p = _pad_rows(rwkv_w2[l], wpad)
        a2p = _pad_rows(rwkv_a2[l], apad)
        g2p = _pad_rows(rwkv_g2[l], gpad)

        h = _norm(x2, norm_mix_g[l].reshape(1, d), sh_m, sc_m, seq)
        p = _mm_in(h, wi, lora0)
        p_lora = _mm_in(h, w_lora_p, w_lora_p.shape[0], tn=w_lora_p.shape[0], name="mm_lora")

        y_a = _lru(p, conv_w[l], conv_b[l], lru_wa[l].astype(BF16), lru_wx[l].astype(BF16),
                   lru_ba[l], lru_bx[l], lru_lambda[l], bsz, seq)

        rowv = lambda t: t.reshape(1, dr)
        rp, yp, mc, nm, bonus, gg = _rwkv_a(
            p, p_lora, mu_rkv, mu_lora, rowv(rwkv_w0[l]), rowv(rwkv_a0[l]), rowv(rwkv_k_k[l]),
            rowv(rwkv_k_a[l]), rowv(rwkv_r_k[l]), w2p, a2p, g2p, bsz, seq, rkv_col0)
        y_b = _rwkv_b(rp, yp, mc, nm, bonus, gg, rowv(rwkv_ln_g[l]), rowv(rwkv_ln_b[l]), bsz, seq)

        x2 = _mm_out(y_a, y_b, x2, g_m, w_out[l].astype(BF16), seq)

        last = l == depth - 1
        fg = final_norm_g.reshape(1, d) if last else None
        assert last, "only the final layer carries the closing RMSNorm"
        x2 = _ffn(x2, norm_ffn_g[l].reshape(1, d), sh_f, sc_f, g_f, w_gu[l].astype(BF16),
                  w_down[l].astype(BF16), fg, seq)
    return x2.reshape(bsz, seq, d)
```

```python
import functools

import jax
import jax.numpy as jnp
from jax import lax
from jax.experimental import pallas as pl
from jax.experimental.pallas import tpu as pltpu

F32 = jnp.float32
BF16 = jnp.bfloat16

LRU_HEADS = 4
CONV_WIDTH = 4
LRU_C = 8.0
HEAD = 64
CHUNK = 64
PAIR = 2 * HEAD
HEADS_PER_STEP = 16
ONES_WIDTH = 256
CHUNKS_PER_STEP = 2
RWKV_B_CHUNKS = 4
RMS_EPS = 1e-6
GN_EPS = 64e-5
L2_EPS = 1e-12
LANE = 128
SUBLANE = 8
VMEM_LIMIT = 56 * 1024 * 1024


def _params(*sem):
    return pltpu.CompilerParams(dimension_semantics=sem, vmem_limit_bytes=VMEM_LIMIT)


_NN = (((1,), (0,)), ((), ()))
_NT = (((1,), (1,)), ((), ()))
_TN = (((0,), (0,)), ((), ()))


def _dg(a, b, dims):
    return lax.dot_general(a, b, dims, preferred_element_type=F32)


def _split(x):
    hi = x.astype(BF16)
    lo = (x - hi.astype(F32)).astype(BF16)
    return hi, lo


def _mm3(a, b, dims=_NN):
    ah, al = _split(a)
    bh, bl = _split(b)
    return _dg(ah, bh, dims) + (_dg(ah, bl, dims) + _dg(al, bh, dims))


def _mm3_presplit(a, bh, bl):
    ah, al = _split(a)
    return _dg(ah, bh, _NN) + (_dg(ah, bl, _NN) + _dg(al, bh, _NN))


def _mm2_exact_rhs(a, b_bf16):
    ah, al = _split(a)
    return _dg(ah, b_bf16, _NN) + _dg(al, b_bf16, _NN)


def _head_sums(x, ones_h):
    n = ones_h.shape[0]
    return jnp.concatenate([_mm2_exact_rhs(x[:, c:c + n], ones_h) for c in range(0, x.shape[1], n)],
                           axis=1)


def _mm2_exact_lhs(a_bf16, b):
    bh, bl = _split(b)
    return _dg(a_bf16, bh, _NN) + _dg(a_bf16, bl, _NN)


def _softplus(x):
    return jnp.maximum(x, 0.0) + jnp.log1p(jnp.exp(-jnp.abs(x)))


def _iota2(shape):
    return (lax.broadcasted_iota(jnp.int32, shape, 0),
            lax.broadcasted_iota(jnp.int32, shape, 1))


def _head_ones(n):
    r, c = _iota2((n, n))
    return jnp.where((r // HEAD) == (c // HEAD), 1.0, 0.0).astype(BF16)


def _mod_body(c_ref, w_ref, b_ref, o_ref):
    c = c_ref[...]
    ca = c * jax.nn.sigmoid(c)
    o_ref[...] = _mm3(ca, w_ref[...]) + b_ref[...]


def _mod(c, w, b, tn=1024):
    bsz, d = c.shape
    n = w.shape[1]
    return pl.pallas_call(
        _mod_body,
        grid=(n // tn,),
        in_specs=[pl.BlockSpec((bsz, d), lambda j: (0, 0)),
                  pl.BlockSpec((d, tn), lambda j: (0, j)),
                  pl.BlockSpec((1, tn), lambda j: (0, j))],
        out_specs=pl.BlockSpec((bsz, tn), lambda j: (0, j)),
        out_shape=jax.ShapeDtypeStruct((bsz, n), F32),
        compiler_params=_params("parallel"),
        name="mod",
    )(c, w, b)


def _norm_mod(x, g, sh, sc):
    y = x * lax.rsqrt(jnp.mean(x * x, axis=-1, keepdims=True) + RMS_EPS) * g
    return y * (1.0 + sc) + sh


def _norm_body(x_ref, g_ref, sh_ref, sc_ref, o_ref):
    o_ref[...] = _norm_mod(x_ref[...], g_ref[...], sh_ref[0], sc_ref[0]).astype(BF16)


def _norm(x2, g, sh, sc, seq, tm=512):
    m, d = x2.shape
    per_b = seq // tm
    return pl.pallas_call(
        _norm_body,
        grid=(m // tm,),
        in_specs=[pl.BlockSpec((tm, d), lambda i: (i, 0)),
                  pl.BlockSpec((1, d), lambda i: (0, 0)),
                  pl.BlockSpec((1, 1, d), lambda i: (i // per_b, 0, 0)),
                  pl.BlockSpec((1, 1, d), lambda i: (i // per_b, 0, 0))],
        out_specs=pl.BlockSpec((tm, d), lambda i: (i, 0)),
        out_shape=jax.ShapeDtypeStruct((m, d), BF16),
        compiler_params=_params("parallel"),
        name="norm_mix",
    )(x2, g, sh, sc)


def _mm_in_body(h_ref, w_ref, o_ref, wb_ref):
    @pl.when(pl.program_id(1) == 0)
    def _():
        wb_ref[...] = w_ref[...].astype(BF16)

    o_ref[...] = _dg(h_ref[...], wb_ref[...], _NT)


def _mm_in(h, wt, ncols, tm=1024, tn=1024, name="mm_in"):
    m, d = h.shape
    return pl.pallas_call(
        _mm_in_body,
        grid=(ncols // tn, m // tm),
        in_specs=[pl.BlockSpec((tm, d), lambda j, i: (i, 0)),
                  pl.BlockSpec((tn, d), lambda j, i: (j, 0))],
        out_specs=pl.BlockSpec((tm, tn), lambda j, i: (i, j)),
        out_shape=jax.ShapeDtypeStruct((m, ncols), F32),
        scratch_shapes=[pltpu.VMEM((tn, d), BF16)],
        compiler_params=_params("parallel", "arbitrary"),
        name=name,
    )(h, wt)


def _shift_rows(x, s, fill, row):
    return jnp.where(row < s, fill, pltpu.roll(x, s, 0))


def _lru_body(u_ref, gate_ref, halo_ref, cw_ref, cb_ref, wa_ref, wx_ref, ba_ref, bx_ref,
              lam_ref, o_ref, carry_ref, *, tt):
    ti = pl.program_id(1)
    first = ti == 0

    @pl.when(first)
    def _():
        carry_ref[...] = jnp.zeros_like(carry_ref)

    p = u_ref[...]
    dl = p.shape[1]
    halo = jnp.where(first, 0.0, halo_ref[...])
    ext = jnp.concatenate([halo, p], axis=0)
    cw = cw_ref[...]
    u = cb_ref[...] + p * cw[CONV_WIDTH - 1:CONV_WIDTH, :]
    for j in range(1, CONV_WIDTH):
        shifted = pltpu.roll(ext, j, 0)[SUBLANE:, :]
        u = u + shifted * cw[CONV_WIDTH - 1 - j:CONV_WIDTH - j, :]

    hd = dl // LRU_HEADS
    ub = u.astype(BF16)
    ra, rx = [], []
    for h in range(LRU_HEADS):
        uh = ub[:, h * hd:(h + 1) * hd]
        ra.append(jnp.dot(uh, wa_ref[h], preferred_element_type=F32))
        rx.append(jnp.dot(uh, wx_ref[h], preferred_element_type=F32))
    r = jax.nn.sigmoid(jnp.concatenate(ra, axis=1) + ba_ref[...])
    ig = jax.nn.sigmoid(jnp.concatenate(rx, axis=1) + bx_ref[...])
    a = jnp.exp(r * ((-LRU_C) * _softplus(-lam_ref[...])))
    mult = jnp.sqrt(1.0 - a * a)
    row = lax.broadcasted_iota(jnp.int32, (tt, dl), 0)
    mult = jnp.where(jnp.logical_and(first, row == 0), 1.0, mult)
    b = mult * (ig * u)

    groups = tt // SUBLANE
    a3 = a.reshape(groups, SUBLANE, dl)
    b3 = b.reshape(groups, SUBLANE, dl)
    sub = lax.broadcasted_iota(jnp.int32, (groups, SUBLANE, dl), 1)
    s = 1
    while s < SUBLANE:
        keep = sub >= s
        a_s = jnp.where(keep, pltpu.roll(a3, s, 1), 1.0)
        b_s = jnp.where(keep, pltpu.roll(b3, s, 1), 0.0)
        b3 = a3 * b_s + b3
        a3 = a3 * a_s
        s *= 2
    gate = jax.nn.gelu(gate_ref[...])
    carry = carry_ref[...]
    for g in range(groups):
        h = b3[g] + a3[g] * carry
        carry = h[SUBLANE - 1:SUBLANE, :]
        o_ref[g * SUBLANE:(g + 1) * SUBLANE, :] = h * gate[g * SUBLANE:(g + 1) * SUBLANE, :]
    carry_ref[...] = carry


def _lru(p, conv_w, conv_b, wa, wx, ba, bx, lam, bsz, seq, tt=256):
    dl = conv_w.shape[1]
    nt = seq // tt
    rows8 = tt // SUBLANE
    row = lambda v: v.reshape(1, dl)
    return pl.pallas_call(
        functools.partial(_lru_body, tt=tt),
        grid=(bsz, nt),
        in_specs=[pl.BlockSpec((tt, dl), lambda b, i: (b * nt + i, 0)),
                  pl.BlockSpec((tt, dl), lambda b, i: (b * nt + i, 1)),
                  pl.BlockSpec((SUBLANE, dl),
                               lambda b, i: (jnp.maximum((b * nt + i) * rows8 - 1, 0), 0)),
                  pl.BlockSpec((CONV_WIDTH, dl), lambda b, i: (0, 0)),
                  pl.BlockSpec((1, dl), lambda b, i: (0, 0)),
                  pl.BlockSpec(wa.shape, lambda b, i: (0, 0, 0)),
                  pl.BlockSpec(wx.shape, lambda b, i: (0, 0, 0)),
                  pl.BlockSpec((1, dl), lambda b, i: (0, 0)),
                  pl.BlockSpec((1, dl), lambda b, i: (0, 0)),
                  pl.BlockSpec((1, dl), lambda b, i: (0, 0))],
        out_specs=pl.BlockSpec((tt, dl), lambda b, i: (b * nt + i, 0)),
        out_shape=jax.ShapeDtypeStruct((bsz * seq, dl), F32),
        scratch_shapes=[pltpu.VMEM((1, dl), F32)],
        compiler_params=_params("parallel", "arbitrary"),
        name="lru",
    )(p, p, p, conv_w, row(conv_b), wa, wx, row(ba), row(bx), row(lam))


def _token_shift(x, halo, mu, first, row):
    prev = jnp.where(first, 0.0, halo[SUBLANE - 1:SUBLANE, :])
    xs = jnp.where(row == 0, prev, pltpu.roll(x, 1, 0))
    return x + (xs - x) * mu


def _mm1(a, b, dims=_NN):
    return _dg(a.astype(BF16), b.astype(BF16), dims)


def _pair_diag(y, left):
    return jnp.concatenate([jnp.where(left, y, 0.0), jnp.where(left, 0.0, y)], axis=0).astype(BF16)


def _pair_mm(x, y, left):
    return _dg(x.astype(BF16), _pair_diag(y, left), _NN)


def _chunk_chain(ops, store):
    ab_, bb_, kb_, rb_, v_, bt_, kt_, pe_ = ops
    rc, lane = _iota2((CHUNK, PAIR))
    cc = lane % HEAD
    left = lane < HEAD
    strict = rc > cc
    incl = rc >= cc
    diag = rc == cc
    ab16 = [x.astype(BF16) for x in ab_]
    rb16 = [x.astype(BF16) for x in rb_]
    bd_b = [_pair_diag(x, left) for x in bb_]
    bd_k = [_pair_diag(x, left) for x in kb_]
    bd_v = [_pair_diag(x, left) for x in v_]
    a_ab = [jnp.where(strict, _dg(x, y, _NT), 0.0) for x, y in zip(ab16, bd_b)]
    a_ak = [jnp.where(strict, _dg(x, y, _NT), 0.0).astype(BF16) for x, y in zip(ab16, bd_k)]
    a_rb = [jnp.where(incl, _dg(x, y, _NT), 0.0).astype(BF16) for x, y in zip(rb16, bd_b)]
    a_rk = [jnp.where(incl, _dg(x, y, _NT), 0.0).astype(BF16) for x, y in zip(rb16, bd_k)]
    yield
    base = 8
    d = [jnp.where((rc // base) == (cc // base), a, 0.0) for a in a_ab]
    d2 = [_pair_mm(t, t, left) for t in d]
    akv = [_dg(x, y, _NN) for x, y in zip(a_ak, bd_v)]
    x = [jnp.where(diag, 1.0, 0.0) + t for t in d]
    yield
    x = [xi + _pair_mm(t2, xi, left) for xi, t2 in zip(x, d2)]
    d4 = [_pair_mm(t2, t2, left) for t2 in d2]
    yield
    x = [xi + _pair_mm(t4, xi, left) for xi, t4 in zip(x, d4)]
    yield
    size = base
    while size < CHUNK:
        off = jnp.logical_and((rc // (2 * size)) == (cc // (2 * size)),
                              (rc // size) != (cc // size))
        o = [jnp.where(off, a, 0.0) for a in a_ab]
        ox = [_pair_mm(oi, xi, left) for oi, xi in zip(o, x)]
        yield
        x = [xi + _pair_mm(xi, oxi, left) for xi, oxi in zip(x, ox)]
        yield
        size *= 2
    t = [xi.astype(BF16) for xi in x]
    wu = [_dg(ti, jnp.concatenate([_pair_diag(y, left), _pair_diag(z, left)], axis=1), _NN)
          for ti, y, z in zip(t, ab_, akv)]
    rkv = [_dg(xi, y, _NN) for xi, y in zip(a_rk, bd_v)]
    kv = [_dg(xi.astype(BF16), y.astype(BF16), _TN) for xi, y in zip(kt_, v_)]
    yield
    ry = [_dg(xi, jnp.concatenate([_pair_diag(y[:, :PAIR], left), _pair_diag(y[:, PAIR:], left)], axis=1), _NN)
          for xi, y in zip(a_rb, wu)]
    mn = [_dg(xi.astype(BF16), y.astype(BF16), _TN) for xi, y in zip(bt_, wu)]
    yield

    def head_blocks(z):
        return jnp.where(left, z[:HEAD, :], z[HEAD:, :])

    for u in range(len(ab_)):
        store(u,
              rb_[u] + ry[u][:, :PAIR],
              ry[u][:, PAIR:] + rkv[u],
              jnp.where(diag, pe_[u], 0.0) + head_blocks(mn[u][:, :PAIR]),
              head_blocks(mn[u][:, PAIR:]) + head_blocks(kv[u]))


def _rwkv_a_body(r_ref, k_ref, v_ref, l_ref, rh_ref, kh_ref, vh_ref, lh_ref,
                 mur_ref, muk_ref, muv_ref, mul_ref, w0_ref, a0_ref, kkw_ref, kaw_ref, rkw_ref,
                 w2h_ref, w2l_ref, a2h_ref, a2l_ref, g2h_ref, g2l_ref, ones_ref, tri_ref,
                 rp_ref, yp_ref, m_ref, n_ref, bonus_ref, g_ref):
    first = pl.program_id(1) == 0
    cl = CHUNK
    rows = CHUNKS_PER_STEP * cl
    width = HEADS_PER_STEP * HEAD
    gw = ones_ref.shape[0]
    row_g = lax.broadcasted_iota(jnp.int32, (rows, gw), 0)
    row_l = lax.broadcasted_iota(jnp.int32, (rows, l_ref.shape[1]), 0)
    ones_h = ones_ref[...]

    lo = _token_shift(l_ref[...], lh_ref[...], mul_ref[...], first, row_l)
    act_w = _split(jnp.tanh(lo[:, 0:LANE]))
    act_a = _split(lo[:, LANE:2 * LANE])
    act_g = _split(jax.nn.sigmoid(lo[:, 2 * LANE:]))

    def lora(act, wh_ref, wl_ref, cs):
        (ah, al_), bh, bl = act, wh_ref[:, cs], wl_ref[:, cs]
        return _dg(ah, bh, _NN) + (_dg(ah, bl, _NN) + _dg(al_, bh, _NN))

    def prologue(c0, out):
        cs = slice(c0, c0 + gw)
        r = _token_shift(r_ref[:, cs], rh_ref[:, cs], mur_ref[:, cs], first, row_g)
        k = _token_shift(k_ref[:, cs], kh_ref[:, cs], muk_ref[:, cs], first, row_g)
        v = _token_shift(v_ref[:, cs], vh_ref[:, cs], muv_ref[:, cs], first, row_g)
        w_lin = w0_ref[:, cs] + lora(act_w, w2h_ref, w2l_ref, cs)
        a_lin = a0_ref[:, cs] + lora(act_a, a2h_ref, a2l_ref, cs)
        g_ref[:, cs] = lora(act_g, g2h_ref, g2l_ref, cs)
        kk = k * kkw_ref[:, cs]
        kk_ss = _mm2_exact_rhs(kk * kk, ones_h)
        yield
        w = -_softplus(-w_lin) - 0.5
        lw = -jnp.exp(w)
        a = jax.nn.sigmoid(a_lin)
        kk = kk / jnp.maximum(jnp.sqrt(kk_ss), L2_EPS)
        kp = k * (1.0 + (a - 1.0) * kaw_ref[:, cs])
        bonus_ref[:, cs] = _mm2_exact_rhs(r * kp * rkw_ref[:, cs], ones_h) * v
        lc = _mm2_exact_lhs(tri_ref[...], lw)
        yield
        p_incl = jnp.exp(lc)
        p_excl = jnp.exp(lc - lw)
        p_inv = jnp.exp(-lc)
        p_end = jnp.concatenate(
            [jnp.broadcast_to(p_incl[(j + 1) * cl - 1:(j + 1) * cl, :], (cl, gw))
             for j in range(CHUNKS_PER_STEP)], axis=0)
        abar = -(kk * p_excl)
        bbar = kk * a * p_inv
        kbar = kp * p_inv
        rbar = r * p_incl
        btil = bbar * p_end
        ktil = kbar * p_end
        units = [(j, q) for j in range(CHUNKS_PER_STEP) for q in range(gw // PAIR)]
        out.extend([x[j * cl:(j + 1) * cl, q * PAIR:(q + 1) * PAIR] for j, q in units]
                   for x in (abar, bbar, kbar, rbar, v, btil, ktil, p_end))
        yield

    def make_store(c0):
        units = [(j, q) for j in range(CHUNKS_PER_STEP) for q in range(gw // PAIR)]

        def store(u, rp, yp, mm, nn):
            j, q = units[u]
            rs = slice(j * cl, (j + 1) * cl)
            qs = slice(c0 + q * PAIR, c0 + (q + 1) * PAIR)
            rp_ref[rs, qs] = rp
            yp_ref[rs, qs] = yp
            m_ref[rs, qs] = mm
            n_ref[rs, qs] = nn
        return store

    chains = []
    for c0 in range(0, width, gw):
        ops = []
        for _ in prologue(c0, ops):
            for ch in chains:
                next(ch, None)
        chains.append(_chunk_chain(ops, make_store(c0)))
    live = list(chains)
    while live:
        live = [ch for ch in live if next(ch, StopIteration) is not StopIteration]


def _rwkv_a(p, p_lora, mu_rkv, mu_lora, w0, a0, k_k, k_a, r_k, w2p, a2p, g2p, bsz, seq, rkv_col0):
    cl = CHUNKS_PER_STEP * CHUNK
    width = HEADS_PER_STEP * HEAD
    dr = w0.shape[1]
    ngroups = dr // width
    nc = seq // cl
    lw_ = mu_lora.shape[1]
    cb0 = rkv_col0 // width
    rows8 = cl // SUBLANE
    rt, ct = _iota2((cl, cl))
    tri = jnp.where(jnp.logical_and(rt >= ct, (rt // CHUNK) == (ct // CHUNK)), 1.0, 0.0).astype(BF16)
    ones_h = _head_ones(ONES_WIDTH)
    const = lambda arr: pl.BlockSpec(arr.shape, lambda b, i, q: (0, 0))
    lora_w = [t for wgt in (w2p, a2p, g2p) for t in _split(wgt)]

    def tile(cb_off):
        return pl.BlockSpec((cl, width), lambda b, i, q: (b * nc + i, cb0 + cb_off + q))

    def halo(cb_off):
        return pl.BlockSpec(
            (SUBLANE, width),
            lambda b, i, q: (jnp.maximum((b * nc + i) * rows8 - 1, 0), cb0 + cb_off + q))

    def prow(off=0):
        return pl.BlockSpec((1, width), lambda b, i, q: (0, off + q))

    out_tile = pl.BlockSpec((cl, width), lambda b, i, q: (b * nc + i, q))
    out_mat = pl.BlockSpec((CHUNKS_PER_STEP * HEAD, width), lambda b, i, q: (b * nc + i, q))
    act = jax.ShapeDtypeStruct((bsz * seq, dr), F32)
    mat = jax.ShapeDtypeStruct((bsz * (seq // CHUNK) * HEAD, dr), F32)
    return pl.pallas_call(
        _rwkv_a_body,
        grid=(bsz, nc, ngroups),
        in_specs=[tile(0), tile(ngroups), tile(2 * ngroups),
                  pl.BlockSpec((cl, lw_), lambda b, i, q: (b * nc + i, 0)),
                  halo(0), halo(ngroups), halo(2 * ngroups),
                  pl.BlockSpec((SUBLANE, lw_),
                               lambda b, i, q: (jnp.maximum((b * nc + i) * rows8 - 1, 0), 0)),
                  prow(0), prow(ngroups), prow(2 * ngroups),
                  pl.BlockSpec((1, lw_), lambda b, i, q: (0, 0)),
                  prow(), prow(), prow(), prow(), prow()]
                 + [pl.BlockSpec((t.shape[0], width), lambda b, i, q: (0, q)) for t in lora_w]
                 + [const(ones_h), const(tri)],
        out_specs=[out_tile, out_tile, out_mat, out_mat, out_tile, out_tile],
        out_shape=[act, act, mat, mat, act, act],
        compiler_params=_params("parallel", "parallel", "parallel"),
        name="rwkv_a",
    )(p, p, p, p_lora, p, p, p, p_lora, mu_rkv, mu_rkv, mu_rkv, mu_lora, w0, a0, k_k, k_a, r_k,
      *lora_w, ones_h, tri)


def _rwkv_b_body(rp_ref, yp_ref, m_ref, n_ref, bonus_ref, g_ref, lng_ref, lnb_ref, ones_ref,
                 o_ref, state_ref):
    @pl.when(pl.program_id(1) == 0)
    def _():
        state_ref[...] = jnp.zeros_like(state_ref)

    npairs = state_ref.shape[0]
    pairs = range(npairs)
    ps = [slice(q * PAIR, (q + 1) * PAIR) for q in pairs]
    left = lax.broadcasted_iota(jnp.int32, (HEAD, PAIR), 1) < HEAD
    ones_h = ones_ref[...]
    inv_n = 1.0 / HEAD
    state = [state_ref[q] for q in pairs]
    for j in range(rp_ref.shape[0] // CHUNK):
        rs = slice(j * CHUNK, (j + 1) * CHUNK)
        ks = slice(j * HEAD, (j + 1) * HEAD)
        g0 = [_pair_diag(state[q], left) for q in pairs]
        ys = [_dg(rp_ref[rs, ps[q]].astype(BF16), g0[q], _NN) + yp_ref[rs, ps[q]] for q in pairs]
        state = [_dg(m_ref[ks, ps[q]].astype(BF16), g0[q], _NN) + n_ref[ks, ps[q]] for q in pairs]
        y = jnp.concatenate(ys, axis=1)
        yc = y - _head_sums(y, ones_h) * inv_n
        var = _head_sums(yc * yc, ones_h) * inv_n
        yn = yc * lax.rsqrt(var + GN_EPS) * lng_ref[...] + lnb_ref[...]
        o_ref[rs, :] = (yn + bonus_ref[rs, :]) * g_ref[rs, :]
    for q in pairs:
        state_ref[q] = state[q]


def _rwkv_b(rp, yp, mc, nm, bonus, g, ln_g, ln_b, bsz, seq):
    cl = RWKV_B_CHUNKS * CHUNK
    dr = rp.shape[1]
    nc = seq // cl
    tile = pl.BlockSpec((cl, dr), lambda b, i: (b * nc + i, 0))
    mat = pl.BlockSpec((RWKV_B_CHUNKS * HEAD, dr), lambda b, i: (b * nc + i, 0))
    prow = pl.BlockSpec((1, dr), lambda b, i: (0, 0))
    ones_h = _head_ones(ONES_WIDTH)
    return pl.pallas_call(
        _rwkv_b_body,
        grid=(bsz, nc),
        in_specs=[tile, tile, mat, mat, tile, tile, prow, prow,
                  pl.BlockSpec(ones_h.shape, lambda b, i: (0, 0))],
        out_specs=tile,
        out_shape=jax.ShapeDtypeStruct((bsz * seq, dr), F32),
        scratch_shapes=[pltpu.VMEM((dr // PAIR, HEAD, PAIR), F32)],
        compiler_params=_params("parallel", "arbitrary"),
        name="rwkv_b",
    )(rp, yp, mc, nm, bonus, g, ln_g, ln_b, ones_h)


def _mm_out_body(ya_ref, yb_ref, x_ref, gm_ref, w_ref, o_ref):
    da = ya_ref.shape[1]
    mix = (jnp.dot(ya_ref[...].astype(BF16), w_ref[:da, :], preferred_element_type=F32)
           + jnp.dot(yb_ref[...].astype(BF16), w_ref[da:, :], preferred_element_type=F32))
    o_ref[...] = x_ref[...] + gm_ref[0] * mix


def _mm_out(ya, yb, x2, gm, w, seq, tm=512):
    m, d = x2.shape
    per_b = seq // tm
    return pl.pallas_call(
        _mm_out_body,
        grid=(m // tm,),
        in_specs=[pl.BlockSpec((tm, ya.shape[1]), lambda i: (i, 0)),
                  pl.BlockSpec((tm, yb.shape[1]), lambda i: (i, 0)),
                  pl.BlockSpec((tm, d), lambda i: (i, 0)),
                  pl.BlockSpec((1, 1, d), lambda i: (i // per_b, 0, 0)),
                  pl.BlockSpec(w.shape, lambda i: (0, 0))],
        out_specs=pl.BlockSpec((tm, d), lambda i: (i, 0)),
        out_shape=jax.ShapeDtypeStruct((m, d), F32),
        compiler_params=_params("parallel"),
        name="mm_out",
    )(ya, yb, x2, gm, w)


def _ffn_body(x_ref, g_ref, sh_ref, sc_ref, gf_ref, wg_ref, wu_ref, wd_ref, fg_ref,
              o_ref, h_ref, a_ref):
    f = pl.program_id(1)

    @pl.when(f == 0)
    def _():
        h_ref[...] = _norm_mod(x_ref[...], g_ref[...], sh_ref[0], sc_ref[0]).astype(BF16)

    h = h_ref[...]
    gate = jnp.dot(h, wg_ref[...].astype(BF16), preferred_element_type=F32)
    up = jnp.dot(h, wu_ref[...].astype(BF16), preferred_element_type=F32)
    a_ref[...] = (gate * jax.nn.sigmoid(gate) * up).astype(BF16)

    @pl.when(f == 0)
    def _():
        o_ref[...] = jnp.dot(a_ref[...], wd_ref[...].astype(BF16), preferred_element_type=F32)

    @pl.when(f > 0)
    def _():
        o_ref[...] += jnp.dot(a_ref[...], wd_ref[...].astype(BF16), preferred_element_type=F32)

    @pl.when(f == pl.num_programs(1) - 1)
    def _():
        y = x_ref[...] + gf_ref[0] * o_ref[...]
        o_ref[...] = (y * lax.rsqrt(jnp.mean(y * y, axis=-1, keepdims=True) + RMS_EPS)
                      * fg_ref[...])


def _ffn(x1, g, sh, sc, gf, w_gu, w_down, fg, seq, tm=1024, tf=256):
    m, d = x1.shape
    dff = w_down.shape[0]
    nf = dff // tf
    assert seq % tm == 0 and dff % tf == 0, "row tiles must not straddle sequences"
    per_b = seq // tm
    brow = pl.BlockSpec((1, 1, d), lambda i, f: (i // per_b, 0, 0))
    prow = pl.BlockSpec((1, d), lambda i, f: (0, 0))
    return pl.pallas_call(
        _ffn_body,
        grid=(m // tm, nf),
        in_specs=[pl.BlockSpec((tm, d), lambda i, f: (i, 0), pipeline_mode=pl.Buffered(1)),
                  prow, brow, brow, brow,
                  pl.BlockSpec((d, tf), lambda i, f: (0, f)),
                  pl.BlockSpec((d, tf), lambda i, f: (0, nf + f)),
                  pl.BlockSpec((tf, d), lambda i, f: (f, 0)),
                  prow],
        out_specs=pl.BlockSpec((tm, d), lambda i, f: (i, 0)),
        out_shape=jax.ShapeDtypeStruct((m, d), F32),
        scratch_shapes=[pltpu.VMEM((tm, d), BF16), pltpu.VMEM((tm, tf), BF16)],
        compiler_params=_params("parallel", "arbitrary"),
        name="ffn",
    )(x1, g, sh, sc, gf, w_gu, w_gu, w_down, fg)


def _pad_cols(w, n):
    return jnp.pad(w, ((0, 0), (0, n - w.shape[1])))


def _pad_rows(w, n):
    return jnp.pad(w, ((0, n - w.shape[0]), (0, 0)))


def kernel(x, c, w_ada, b_ada, norm_mix_g, w_in, conv_w, conv_b, lru_wa, lru_ba, lru_wx, lru_bx, lru_lambda, rwkv_mu, rwkv_w0, rwkv_w2, rwkv_a0, rwkv_a2, rwkv_g2, rwkv_k_k, rwkv_k_a, rwkv_r_k, rwkv_ln_g, rwkv_ln_b, w_out, norm_ffn_g, w_gu, w_down, final_norm_g):
    bsz, seq, d = x.shape
    depth = w_ada.shape[0]
    dl = conv_w.shape[2]
    dr = rwkv_w0.shape[1]
    w_lora, a_lora, g_lora = rwkv_w2.shape[1], rwkv_a2.shape[1], rwkv_g2.shape[1]
    wpad, apad = LANE, LANE
    gpad = -(-g_lora // LANE) * LANE
    rkv_col0 = 2 * dl
    lora0 = rkv_col0 + 3 * dr

    x2 = x.reshape(bsz * seq, d)
    for l in range(depth):
        mod = _mod(c, w_ada[l], b_ada[l].reshape(1, -1))
        sh_m, sc_m, g_m, sh_f, sc_f, g_f = [t.reshape(bsz, 1, d) for t in jnp.split(mod, 6, axis=-1)]

        wi = jnp.swapaxes(w_in[l], 0, 1)
        o1, o2 = lora0 + w_lora, lora0 + w_lora + a_lora
        w_lora_p = jnp.concatenate(
            [_pad_rows(wi[lora0:o1], wpad), _pad_rows(wi[o1:o2], apad),
             _pad_rows(wi[o2:], gpad)], axis=0)
        mu = rwkv_mu[l].reshape(1, -1)
        mu_rkv = mu[:, :3 * dr]
        mu_lora = jnp.concatenate(
            [_pad_cols(mu[:, 3 * dr:3 * dr + w_lora], wpad),
             _pad_cols(mu[:, 3 * dr + w_lora:3 * dr + w_lora + a_lora], apad),
             _pad_cols(mu[:, 3 * dr + w_lora + a_lora:], gpad)], axis=1)
        w2p = _pad_rows(rwkv_w2[l], wpad)
        a2p = _pad_rows(rwkv_a2[l], apad)
        g2p = _pad_rows(rwkv_g2[l], gpad)

        h = _norm(x2, norm_mix_g[l].reshape(1, d), sh_m, sc_m, seq)
        p = _mm_in(h, wi, lora0)
        p_lora = _mm_in(h, w_lora_p, w_lora_p.shape[0], tn=w_lora_p.shape[0], name="mm_lora")

        y_a = _lru(p, conv_w[l], conv_b[l], lru_wa[l].astype(BF16), lru_wx[l].astype(BF16),
                   lru_ba[l], lru_bx[l], lru_lambda[l], bsz, seq)

        rowv = lambda t: t.reshape(1, dr)
        rp, yp, mc, nm, bonus, gg = _rwkv_a(
            p, p_lora, mu_rkv, mu_lora, rowv(rwkv_w0[l]), rowv(rwkv_a0[l]), rowv(rwkv_k_k[l]),
            rowv(rwkv_k_a[l]), rowv(rwkv_r_k[l]), w2p, a2p, g2p, bsz, seq, rkv_col0)
        y_b = _rwkv_b(rp, yp, mc, nm, bonus, gg, rowv(rwkv_ln_g[l]), rowv(rwkv_ln_b[l]), bsz, seq)

        x2 = _mm_out(y_a, y_b, x2, g_m, w_out[l].astype(BF16), seq)

        last = l == depth - 1
        fg = final_norm_g.reshape(1, d) if last else None
        assert last, "only the final layer carries the closing RMSNorm"
        x2 = _ffn(x2, norm_ffn_g[l].reshape(1, d), sh_f, sc_f, g_f, w_gu[l], w_down[l], fg, seq)
    return x2.reshape(bsz, seq, d)
```

```python
import functools

import jax
import jax.numpy as jnp
from jax import lax
from jax.experimental import pallas as pl
from jax.experimental.pallas import tpu as pltpu

F32 = jnp.float32
BF16 = jnp.bfloat16

LRU_HEADS = 4
CONV_WIDTH = 4
LRU_C = 8.0
HEAD = 64
CHUNK = 64
PAIR = 2 * HEAD
HEADS_PER_STEP = 16
ONES_WIDTH = 256
CHUNKS_PER_STEP = 2
RWKV_B_CHUNKS = 4
RMS_EPS = 1e-6
GN_EPS = 64e-5
L2_EPS = 1e-12
LANE = 128
SUBLANE = 8
BF16_SUBLANE = 16
VMEM_LIMIT = 56 * 1024 * 1024


def _params(*sem):
    return pltpu.CompilerParams(dimension_semantics=sem, vmem_limit_bytes=VMEM_LIMIT)


_NN = (((1,), (0,)), ((), ()))
_NT = (((1,), (1,)), ((), ()))
_TN = (((0,), (0,)), ((), ()))


def _dg(a, b, dims):
    return lax.dot_general(a, b, dims, preferred_element_type=F32)


def _split(x):
    hi = x.astype(BF16)
    lo = (x - hi.astype(F32)).astype(BF16)
    return hi, lo


def _mm3(a, b, dims=_NN):
    ah, al = _split(a)
    bh, bl = _split(b)
    return _dg(ah, bh, dims) + (_dg(ah, bl, dims) + _dg(al, bh, dims))


def _mm3_presplit(a, bh, bl):
    ah, al = _split(a)
    return _dg(ah, bh, _NN) + (_dg(ah, bl, _NN) + _dg(al, bh, _NN))


def _mm2_exact_rhs(a, b_bf16):
    ah, al = _split(a)
    return _dg(ah, b_bf16, _NN) + _dg(al, b_bf16, _NN)


def _head_sums(x, ones_h):
    n = ones_h.shape[0]
    return jnp.concatenate([_mm2_exact_rhs(x[:, c:c + n], ones_h) for c in range(0, x.shape[1], n)],
                           axis=1)


def _mm2_exact_lhs(a_bf16, b):
    bh, bl = _split(b)
    return _dg(a_bf16, bh, _NN) + _dg(a_bf16, bl, _NN)


def _softplus(x):
    return jnp.maximum(x, 0.0) + jnp.log1p(jnp.exp(-jnp.abs(x)))


def _iota2(shape):
    return (lax.broadcasted_iota(jnp.int32, shape, 0),
            lax.broadcasted_iota(jnp.int32, shape, 1))


def _head_ones(n):
    r, c = _iota2((n, n))
    return jnp.where((r // HEAD) == (c // HEAD), 1.0, 0.0).astype(BF16)


def _mod_body(c_ref, w_ref, b_ref, o_ref):
    c = c_ref[...]
    ca = c * jax.nn.sigmoid(c)
    o_ref[...] = _mm3(ca, w_ref[...]) + b_ref[...]


def _mod(c, w, b, tn=1024):
    bsz, d = c.shape
    n = w.shape[1]
    return pl.pallas_call(
        _mod_body,
        grid=(n // tn,),
        in_specs=[pl.BlockSpec((bsz, d), lambda j: (0, 0)),
                  pl.BlockSpec((d, tn), lambda j: (0, j)),
                  pl.BlockSpec((1, tn), lambda j: (0, j))],
        out_specs=pl.BlockSpec((bsz, tn), lambda j: (0, j)),
        out_shape=jax.ShapeDtypeStruct((bsz, n), F32),
        compiler_params=_params("parallel"),
        name="mod",
    )(c, w, b)


def _norm_mod(x, g, sh, sc):
    y = x * lax.rsqrt(jnp.mean(x * x, axis=-1, keepdims=True) + RMS_EPS) * g
    return y * (1.0 + sc) + sh


def _norm_body(x_ref, g_ref, sh_ref, sc_ref, o_ref):
    o_ref[...] = _norm_mod(x_ref[...], g_ref[...], sh_ref[0], sc_ref[0]).astype(BF16)


def _norm(x2, g, sh, sc, seq, tm=512):
    m, d = x2.shape
    per_b = seq // tm
    return pl.pallas_call(
        _norm_body,
        grid=(m // tm,),
        in_specs=[pl.BlockSpec((tm, d), lambda i: (i, 0)),
                  pl.BlockSpec((1, d), lambda i: (0, 0)),
                  pl.BlockSpec((1, 1, d), lambda i: (i // per_b, 0, 0)),
                  pl.BlockSpec((1, 1, d), lambda i: (i // per_b, 0, 0))],
        out_specs=pl.BlockSpec((tm, d), lambda i: (i, 0)),
        out_shape=jax.ShapeDtypeStruct((m, d), BF16),
        compiler_params=_params("parallel"),
        name="norm_mix",
    )(x2, g, sh, sc)


def _mm_in_body(h_ref, w_ref, o_ref, wb_ref):
    @pl.when(pl.program_id(1) == 0)
    def _():
        wb_ref[...] = w_ref[...].astype(BF16)

    o_ref[...] = _dg(h_ref[...], wb_ref[...], _NT)


def _mm_in(h, wt, ncols, tm=1024, tn=1024, name="mm_in"):
    m, d = h.shape
    return pl.pallas_call(
        _mm_in_body,
        grid=(ncols // tn, m // tm),
        in_specs=[pl.BlockSpec((tm, d), lambda j, i: (i, 0)),
                  pl.BlockSpec((tn, d), lambda j, i: (j, 0))],
        out_specs=pl.BlockSpec((tm, tn), lambda j, i: (i, j)),
        out_shape=jax.ShapeDtypeStruct((m, ncols), F32),
        scratch_shapes=[pltpu.VMEM((tn, d), BF16)],
        compiler_params=_params("parallel", "arbitrary"),
        name=name,
    )(h, wt)


def _shift_rows(x, s, fill, row):
    return jnp.where(row < s, fill, pltpu.roll(x, s, 0))


def _lru_body(u_ref, gate_ref, halo_ref, cw_ref, cb_ref, wa_ref, wx_ref, ba_ref, bx_ref,
              lam_ref, o_ref, carry_ref, *, tt):
    ti = pl.program_id(1)
    first = ti == 0

    @pl.when(first)
    def _():
        carry_ref[...] = jnp.zeros_like(carry_ref)

    p = u_ref[...]
    dl = p.shape[1]
    halo = jnp.where(first, 0.0, halo_ref[...])
    ext = jnp.concatenate([halo, p], axis=0)
    cw = cw_ref[...]
    u = cb_ref[...] + p * cw[CONV_WIDTH - 1:CONV_WIDTH, :]
    for j in range(1, CONV_WIDTH):
        shifted = pltpu.roll(ext, j, 0)[SUBLANE:, :]
        u = u + shifted * cw[CONV_WIDTH - 1 - j:CONV_WIDTH - j, :]

    hd = dl // LRU_HEADS
    ub = u.astype(BF16)
    ra, rx = [], []
    for h in range(LRU_HEADS):
        uh = ub[:, h * hd:(h + 1) * hd]
        ra.append(jnp.dot(uh, wa_ref[h], preferred_element_type=F32))
        rx.append(jnp.dot(uh, wx_ref[h], preferred_element_type=F32))
    r = jax.nn.sigmoid(jnp.concatenate(ra, axis=1) + ba_ref[...])
    ig = jax.nn.sigmoid(jnp.concatenate(rx, axis=1) + bx_ref[...])
    a = jnp.exp(r * ((-LRU_C) * _softplus(-lam_ref[...])))
    mult = jnp.sqrt(1.0 - a * a)
    row = lax.broadcasted_iota(jnp.int32, (tt, dl), 0)
    mult = jnp.where(jnp.logical_and(first, row == 0), 1.0, mult)
    b = mult * (ig * u)

    groups = tt // SUBLANE
    a3 = a.reshape(groups, SUBLANE, dl)
    b3 = b.reshape(groups, SUBLANE, dl)
    sub = lax.broadcasted_iota(jnp.int32, (groups, SUBLANE, dl), 1)
    s = 1
    while s < SUBLANE:
        keep = sub >= s
        a_s = jnp.where(keep, pltpu.roll(a3, s, 1), 1.0)
        b_s = jnp.where(keep, pltpu.roll(b3, s, 1), 0.0)
        b3 = a3 * b_s + b3
        a3 = a3 * a_s
        s *= 2
    gate = jax.nn.gelu(gate_ref[...])
    carry = carry_ref[...]
    for g in range(groups):
        h = b3[g] + a3[g] * carry
        carry = h[SUBLANE - 1:SUBLANE, :]
        o_ref[g * SUBLANE:(g + 1) * SUBLANE, :] = h * gate[g * SUBLANE:(g + 1) * SUBLANE, :]
    carry_ref[...] = carry


def _lru(p, conv_w, conv_b, wa, wx, ba, bx, lam, bsz, seq, tt=256):
    dl = conv_w.shape[1]
    nt = seq // tt
    rows8 = tt // SUBLANE
    row = lambda v: v.reshape(1, dl)
    return pl.pallas_call(
        functools.partial(_lru_body, tt=tt),
        grid=(bsz, nt),
        in_specs=[pl.BlockSpec((tt, dl), lambda b, i: (b * nt + i, 0)),
                  pl.BlockSpec((tt, dl), lambda b, i: (b * nt + i, 1)),
                  pl.BlockSpec((SUBLANE, dl),
                               lambda b, i: (jnp.maximum((b * nt + i) * rows8 - 1, 0), 0)),
                  pl.BlockSpec((CONV_WIDTH, dl), lambda b, i: (0, 0)),
                  pl.BlockSpec((1, dl), lambda b, i: (0, 0)),
                  pl.BlockSpec(wa.shape, lambda b, i: (0, 0, 0)),
                  pl.BlockSpec(wx.shape, lambda b, i: (0, 0, 0)),
                  pl.BlockSpec((1, dl), lambda b, i: (0, 0)),
                  pl.BlockSpec((1, dl), lambda b, i: (0, 0)),
                  pl.BlockSpec((1, dl), lambda b, i: (0, 0))],
        out_specs=pl.BlockSpec((tt, dl), lambda b, i: (b * nt + i, 0)),
        out_shape=jax.ShapeDtypeStruct((bsz * seq, dl), F32),
        scratch_shapes=[pltpu.VMEM((1, dl), F32)],
        compiler_params=_params("parallel", "arbitrary"),
        name="lru",
    )(p, p, p, conv_w, row(conv_b), wa, wx, row(ba), row(bx), row(lam))


def _token_shift(x, halo, mu, first, row):
    prev = jnp.where(first, 0.0, halo[SUBLANE - 1:SUBLANE, :])
    xs = jnp.where(row == 0, prev, pltpu.roll(x, 1, 0))
    return x + (xs - x) * mu


def _mm1(a, b, dims=_NN):
    return _dg(a.astype(BF16), b.astype(BF16), dims)


def _pair_diag(y, left):
    return jnp.concatenate([jnp.where(left, y, 0.0), jnp.where(left, 0.0, y)], axis=0).astype(BF16)


def _pair_mm(x, y, left):
    return _dg(x.astype(BF16), _pair_diag(y, left), _NN)


def _chunk_chain(ops, store):
    ab_, bb_, kb_, rb_, v_, bt_, kt_, pe_ = ops
    rc, lane = _iota2((CHUNK, PAIR))
    cc = lane % HEAD
    left = lane < HEAD
    strict = rc > cc
    incl = rc >= cc
    diag = rc == cc
    ab16 = [x.astype(BF16) for x in ab_]
    rb16 = [x.astype(BF16) for x in rb_]
    bd_b = [_pair_diag(x, left) for x in bb_]
    bd_k = [_pair_diag(x, left) for x in kb_]
    bd_v = [_pair_diag(x, left) for x in v_]
    a_ab = [jnp.where(strict, _dg(x, y, _NT), 0.0) for x, y in zip(ab16, bd_b)]
    a_ak = [jnp.where(strict, _dg(x, y, _NT), 0.0).astype(BF16) for x, y in zip(ab16, bd_k)]
    a_rb = [jnp.where(incl, _dg(x, y, _NT), 0.0).astype(BF16) for x, y in zip(rb16, bd_b)]
    a_rk = [jnp.where(incl, _dg(x, y, _NT), 0.0).astype(BF16) for x, y in zip(rb16, bd_k)]
    yield
    base = 8
    d = [jnp.where((rc // base) == (cc // base), a, 0.0) for a in a_ab]
    d2 = [_pair_mm(t, t, left) for t in d]
    akv = [_dg(x, y, _NN) for x, y in zip(a_ak, bd_v)]
    x = [jnp.where(diag, 1.0, 0.0) + t for t in d]
    yield
    x = [xi + _pair_mm(t2, xi, left) for xi, t2 in zip(x, d2)]
    d4 = [_pair_mm(t2, t2, left) for t2 in d2]
    yield
    x = [xi + _pair_mm(t4, xi, left) for xi, t4 in zip(x, d4)]
    yield
    size = base
    while size < CHUNK:
        off = jnp.logical_and((rc // (2 * size)) == (cc // (2 * size)),
                              (rc // size) != (cc // size))
        o = [jnp.where(off, a, 0.0) for a in a_ab]
        ox = [_pair_mm(oi, xi, left) for oi, xi in zip(o, x)]
        yield
        x = [xi + _pair_mm(xi, oxi, left) for xi, oxi in zip(x, ox)]
        yield
        size *= 2
    t = [xi.astype(BF16) for xi in x]
    wu = [_dg(ti, jnp.concatenate([_pair_diag(y, left), _pair_diag(z, left)], axis=1), _NN)
          for ti, y, z in zip(t, ab_, akv)]
    rkv = [_dg(xi, y, _NN) for xi, y in zip(a_rk, bd_v)]
    kv = [_dg(xi.astype(BF16), y.astype(BF16), _TN) for xi, y in zip(kt_, v_)]
    yield
    ry = [_dg(xi, jnp.concatenate([_pair_diag(y[:, :PAIR], left), _pair_diag(y[:, PAIR:], left)], axis=1), _NN)
          for xi, y in zip(a_rb, wu)]
    mn = [_dg(xi.astype(BF16), y.astype(BF16), _TN) for xi, y in zip(bt_, wu)]
    yield

    def head_blocks(z):
        return jnp.where(left, z[:HEAD, :], z[HEAD:, :])

    for u in range(len(ab_)):
        store(u,
              rb_[u] + ry[u][:, :PAIR],
              ry[u][:, PAIR:] + rkv[u],
              jnp.where(diag, pe_[u], 0.0) + head_blocks(mn[u][:, :PAIR]),
              head_blocks(mn[u][:, PAIR:]) + head_blocks(kv[u]))


def _rwkv_a_body(r_ref, k_ref, v_ref, l_ref, rh_ref, kh_ref, vh_ref, lh_ref,
                 mur_ref, muk_ref, muv_ref, mul_ref, w0_ref, a0_ref, kkw_ref, kaw_ref, rkw_ref,
                 w2h_ref, w2l_ref, a2h_ref, a2l_ref, g2h_ref, g2l_ref, ones_ref, tri_ref,
                 *rest):
    ncast = (len(rest) - 6) // 2
    cast_in = rest[:ncast]
    rp_ref, yp_ref, m_ref, n_ref, bonus_ref, g_ref = rest[ncast:ncast + 6]
    cast_out = rest[ncast + 6:]
    first = pl.program_id(1) == 0
    cl = CHUNK
    rows = CHUNKS_PER_STEP * cl
    width = HEADS_PER_STEP * HEAD
    gw = ones_ref.shape[0]
    row_g = lax.broadcasted_iota(jnp.int32, (rows, gw), 0)
    row_l = lax.broadcasted_iota(jnp.int32, (rows, l_ref.shape[1]), 0)
    ones_h = ones_ref[...]

    lo = _token_shift(l_ref[...], lh_ref[...], mul_ref[...], first, row_l)
    act_w = _split(jnp.tanh(lo[:, 0:LANE]))
    act_a = _split(lo[:, LANE:2 * LANE])
    act_g = _split(jax.nn.sigmoid(lo[:, 2 * LANE:]))

    def lora(act, wh_ref, wl_ref, cs):
        (ah, al_), bh, bl = act, wh_ref[:, cs], wl_ref[:, cs]
        return _dg(ah, bh, _NN) + (_dg(ah, bl, _NN) + _dg(al_, bh, _NN))

    def prologue(c0, out):
        cs = slice(c0, c0 + gw)
        r = _token_shift(r_ref[:, cs], rh_ref[:, cs], mur_ref[:, cs], first, row_g)
        k = _token_shift(k_ref[:, cs], kh_ref[:, cs], muk_ref[:, cs], first, row_g)
        v = _token_shift(v_ref[:, cs], vh_ref[:, cs], muv_ref[:, cs], first, row_g)
        w_lin = w0_ref[:, cs] + lora(act_w, w2h_ref, w2l_ref, cs)
        a_lin = a0_ref[:, cs] + lora(act_a, a2h_ref, a2l_ref, cs)
        g_ref[:, cs] = lora(act_g, g2h_ref, g2l_ref, cs)
        kk = k * kkw_ref[:, cs]
        kk_ss = _mm2_exact_rhs(kk * kk, ones_h)
        yield
        w = -_softplus(-w_lin) - 0.5
        lw = -jnp.exp(w)
        a = jax.nn.sigmoid(a_lin)
        kk = kk / jnp.maximum(jnp.sqrt(kk_ss), L2_EPS)
        kp = k * (1.0 + (a - 1.0) * kaw_ref[:, cs])
        bonus_ref[:, cs] = _mm2_exact_rhs(r * kp * rkw_ref[:, cs], ones_h) * v
        lc = _mm2_exact_lhs(tri_ref[...], lw)
        yield
        p_incl = jnp.exp(lc)
        p_excl = jnp.exp(lc - lw)
        p_inv = jnp.exp(-lc)
        p_end = jnp.concatenate(
            [jnp.broadcast_to(p_incl[(j + 1) * cl - 1:(j + 1) * cl, :], (cl, gw))
             for j in range(CHUNKS_PER_STEP)], axis=0)
        abar = -(kk * p_excl)
        bbar = kk * a * p_inv
        kbar = kp * p_inv
        rbar = r * p_incl
        btil = bbar * p_end
        ktil = kbar * p_end
        units = [(j, q) for j in range(CHUNKS_PER_STEP) for q in range(gw // PAIR)]
        out.extend([x[j * cl:(j + 1) * cl, q * PAIR:(q + 1) * PAIR] for j, q in units]
                   for x in (abar, bbar, kbar, rbar, v, btil, ktil, p_end))
        yield

    def make_store(c0):
        units = [(j, q) for j in range(CHUNKS_PER_STEP) for q in range(gw // PAIR)]

        def store(u, rp, yp, mm, nn):
            j, q = units[u]
            rs = slice(j * cl, (j + 1) * cl)
            qs = slice(c0 + q * PAIR, c0 + (q + 1) * PAIR)
            rp_ref[rs, qs] = rp
            yp_ref[rs, qs] = yp
            m_ref[rs, qs] = mm
            n_ref[rs, qs] = nn
        return store

    chains = []
    for c0 in range(0, width, gw):
        ops = []
        for _ in prologue(c0, ops):
            for ch in chains:
                next(ch, None)
        chains.append(_chunk_chain(ops, make_store(c0)))
    live = list(chains)
    while live:
        live = [ch for ch in live if next(ch, StopIteration) is not StopIteration]

    for src, dst in zip(cast_in, cast_out):
        dst[...] = src[...].astype(BF16)


def _rwkv_a(p, p_lora, mu_rkv, mu_lora, w0, a0, k_k, k_a, r_k, w2p, a2p, g2p, bsz, seq, rkv_col0,
            cast_ws=()):
    cl = CHUNKS_PER_STEP * CHUNK
    width = HEADS_PER_STEP * HEAD
    dr = w0.shape[1]
    ngroups = dr // width
    nc = seq // cl
    lw_ = mu_lora.shape[1]
    cb0 = rkv_col0 // width
    rows8 = cl // SUBLANE
    rt, ct = _iota2((cl, cl))
    tri = jnp.where(jnp.logical_and(rt >= ct, (rt // CHUNK) == (ct // CHUNK)), 1.0, 0.0).astype(BF16)
    ones_h = _head_ones(ONES_WIDTH)
    const = lambda arr: pl.BlockSpec(arr.shape, lambda b, i, q: (0, 0))
    lora_w = [t for wgt in (w2p, a2p, g2p) for t in _split(wgt)]

    def tile(cb_off):
        return pl.BlockSpec((cl, width), lambda b, i, q: (b * nc + i, cb0 + cb_off + q))

    def halo(cb_off):
        return pl.BlockSpec(
            (SUBLANE, width),
            lambda b, i, q: (jnp.maximum((b * nc + i) * rows8 - 1, 0), cb0 + cb_off + q))

    def prow(off=0):
        return pl.BlockSpec((1, width), lambda b, i, q: (0, off + q))

    out_tile = pl.BlockSpec((cl, width), lambda b, i, q: (b * nc + i, q))
    out_mat = pl.BlockSpec((CHUNKS_PER_STEP * HEAD, width), lambda b, i, q: (b * nc + i, q))
    act = jax.ShapeDtypeStruct((bsz * seq, dr), F32)
    mat = jax.ShapeDtypeStruct((bsz * (seq // CHUNK) * HEAD, dr), F32)

    nsteps = bsz * nc * ngroups
    cast_specs = []
    for wgt in cast_ws:
        hold = 1
        while (wgt.shape[0] * hold) % (nsteps * BF16_SUBLANE) != 0:
            hold *= 2
        blk = (wgt.shape[0] * hold // nsteps, wgt.shape[1])
        cast_specs.append(pl.BlockSpec(
            blk, lambda b, i, q, hold=hold: (((b * nc + i) * ngroups + q) // hold, 0)))
    cast_shapes = [jax.ShapeDtypeStruct(wgt.shape, BF16) for wgt in cast_ws]

    return pl.pallas_call(
        _rwkv_a_body,
        grid=(bsz, nc, ngroups),
        in_specs=[tile(0), tile(ngroups), tile(2 * ngroups),
                  pl.BlockSpec((cl, lw_), lambda b, i, q: (b * nc + i, 0)),
                  halo(0), halo(ngroups), halo(2 * ngroups),
                  pl.BlockSpec((SUBLANE, lw_),
                               lambda b, i, q: (jnp.maximum((b * nc + i) * rows8 - 1, 0), 0)),
                  prow(0), prow(ngroups), prow(2 * ngroups),
                  pl.BlockSpec((1, lw_), lambda b, i, q: (0, 0)),
                  prow(), prow(), prow(), prow(), prow()]
                 + [pl.BlockSpec((t.shape[0], width), lambda b, i, q: (0, q)) for t in lora_w]
                 + [const(ones_h), const(tri)] + cast_specs,
        out_specs=[out_tile, out_tile, out_mat, out_mat, out_tile, out_tile] + cast_specs,
        out_shape=[act, act, mat, mat, act, act] + cast_shapes,
        compiler_params=_params("arbitrary", "arbitrary", "arbitrary"),
        name="rwkv_a",
    )(p, p, p, p_lora, p, p, p, p_lora, mu_rkv, mu_rkv, mu_rkv, mu_lora, w0, a0, k_k, k_a, r_k,
      *lora_w, ones_h, tri, *cast_ws)


def _rwkv_b_body(rp_ref, yp_ref, m_ref, n_ref, bonus_ref, g_ref, lng_ref, lnb_ref, ones_ref,
                 o_ref, state_ref):
    @pl.when(pl.program_id(1) == 0)
    def _():
        state_ref[...] = jnp.zeros_like(state_ref)

    npairs = state_ref.shape[0]
    pairs = range(npairs)
    ps = [slice(q * PAIR, (q + 1) * PAIR) for q in pairs]
    left = lax.broadcasted_iota(jnp.int32, (HEAD, PAIR), 1) < HEAD
    ones_h = ones_ref[...]
    inv_n = 1.0 / HEAD
    state = [state_ref[q] for q in pairs]
    for j in range(rp_ref.shape[0] // CHUNK):
        rs = slice(j * CHUNK, (j + 1) * CHUNK)
        ks = slice(j * HEAD, (j + 1) * HEAD)
        g0 = [_pair_diag(state[q], left) for q in pairs]
        ys = [_dg(rp_ref[rs, ps[q]].astype(BF16), g0[q], _NN) + yp_ref[rs, ps[q]] for q in pairs]
        state = [_dg(m_ref[ks, ps[q]].astype(BF16), g0[q], _NN) + n_ref[ks, ps[q]] for q in pairs]
        y = jnp.concatenate(ys, axis=1)
        yc = y - _head_sums(y, ones_h) * inv_n
        var = _head_sums(yc * yc, ones_h) * inv_n
        yn = yc * lax.rsqrt(var + GN_EPS) * lng_ref[...] + lnb_ref[...]
        o_ref[rs, :] = (yn + bonus_ref[rs, :]) * g_ref[rs, :]
    for q in pairs:
        state_ref[q] = state[q]


def _rwkv_b(rp, yp, mc, nm, bonus, g, ln_g, ln_b, bsz, seq):
    cl = RWKV_B_CHUNKS * CHUNK
    dr = rp.shape[1]
    nc = seq // cl
    tile = pl.BlockSpec((cl, dr), lambda b, i: (b * nc + i, 0))
    mat = pl.BlockSpec((RWKV_B_CHUNKS * HEAD, dr), lambda b, i: (b * nc + i, 0))
    prow = pl.BlockSpec((1, dr), lambda b, i: (0, 0))
    ones_h = _head_ones(ONES_WIDTH)
    return pl.pallas_call(
        _rwkv_b_body,
        grid=(bsz, nc),
        in_specs=[tile, tile, mat, mat, tile, tile, prow, prow,
                  pl.BlockSpec(ones_h.shape, lambda b, i: (0, 0))],
        out_specs=tile,
        out_shape=jax.ShapeDtypeStruct((bsz * seq, dr), F32),
        scratch_shapes=[pltpu.VMEM((dr // PAIR, HEAD, PAIR), F32)],
        compiler_params=_params("parallel", "arbitrary"),
        name="rwkv_b",
    )(rp, yp, mc, nm, bonus, g, ln_g, ln_b, ones_h)


def _mm_out_body(ya_ref, yb_ref, x_ref, gm_ref, w_ref, o_ref):
    da = ya_ref.shape[1]
    mix = (jnp.dot(ya_ref[...].astype(BF16), w_ref[:da, :], preferred_element_type=F32)
           + jnp.dot(yb_ref[...].astype(BF16), w_ref[da:, :], preferred_element_type=F32))
    o_ref[...] = x_ref[...] + gm_ref[0] * mix


def _mm_out(ya, yb, x2, gm, w, seq, tm=512):
    m, d = x2.shape
    per_b = seq // tm
    return pl.pallas_call(
        _mm_out_body,
        grid=(m // tm,),
        in_specs=[pl.BlockSpec((tm, ya.shape[1]), lambda i: (i, 0)),
                  pl.BlockSpec((tm, yb.shape[1]), lambda i: (i, 0)),
                  pl.BlockSpec((tm, d), lambda i: (i, 0)),
                  pl.BlockSpec((1, 1, d), lambda i: (i // per_b, 0, 0)),
                  pl.BlockSpec(w.shape, lambda i: (0, 0))],
        out_specs=pl.BlockSpec((tm, d), lambda i: (i, 0)),
        out_shape=jax.ShapeDtypeStruct((m, d), F32),
        compiler_params=_params("parallel"),
        name="mm_out",
    )(ya, yb, x2, gm, w)


def _ffn_body(x_ref, g_ref, sh_ref, sc_ref, gf_ref, wg_ref, wu_ref, wd_ref, fg_ref,
              o_ref, h_ref, a_ref):
    f = pl.program_id(1)

    @pl.when(f == 0)
    def _():
        h_ref[...] = _norm_mod(x_ref[...], g_ref[...], sh_ref[0], sc_ref[0]).astype(BF16)

    h = h_ref[...]
    gate = jnp.dot(h, wg_ref[...], preferred_element_type=F32)
    up = jnp.dot(h, wu_ref[...], preferred_element_type=F32)
    a_ref[...] = (gate * jax.nn.sigmoid(gate) * up).astype(BF16)

    @pl.when(f == 0)
    def _():
        o_ref[...] = jnp.dot(a_ref[...], wd_ref[...], preferred_element_type=F32)

    @pl.when(f > 0)
    def _():
        o_ref[...] += jnp.dot(a_ref[...], wd_ref[...], preferred_element_type=F32)

    @pl.when(f == pl.num_programs(1) - 1)
    def _():
        y = x_ref[...] + gf_ref[0] * o_ref[...]
        o_ref[...] = (y * lax.rsqrt(jnp.mean(y * y, axis=-1, keepdims=True) + RMS_EPS)
                      * fg_ref[...])


def _ffn(x1, g, sh, sc, gf, w_gu, w_down, fg, seq, tm=1024, tf=512):
    m, d = x1.shape
    dff = w_down.shape[0]
    nf = dff // tf
    assert seq % tm == 0 and dff % tf == 0, "row tiles must not straddle sequences"
    per_b = seq // tm
    brow = pl.BlockSpec((1, 1, d), lambda i, f: (i // per_b, 0, 0))
    prow = pl.BlockSpec((1, d), lambda i, f: (0, 0))
    return pl.pallas_call(
        _ffn_body,
        grid=(m // tm, nf),
        in_specs=[pl.BlockSpec((tm, d), lambda i, f: (i, 0), pipeline_mode=pl.Buffered(1)),
                  prow, brow, brow, brow,
                  pl.BlockSpec((d, tf), lambda i, f: (0, f)),
                  pl.BlockSpec((d, tf), lambda i, f: (0, nf + f)),
                  pl.BlockSpec((tf, d), lambda i, f: (f, 0)),
                  prow],
        out_specs=pl.BlockSpec((tm, d), lambda i, f: (i, 0)),
        out_shape=jax.ShapeDtypeStruct((m, d), F32),
        scratch_shapes=[pltpu.VMEM((tm, d), BF16), pltpu.VMEM((tm, tf), BF16)],
        compiler_params=_params("parallel", "arbitrary"),
        name="ffn",
    )(x1, g, sh, sc, gf, w_gu, w_gu, w_down, fg)


def _pad_cols(w, n):
    return jnp.pad(w, ((0, 0), (0, n - w.shape[1])))


def _pad_rows(w, n):
    return jnp.pad(w, ((0, n - w.shape[0]), (0, 0)))


def kernel(x, c, w_ada, b_ada, norm_mix_g, w_in, conv_w, conv_b, lru_wa, lru_ba, lru_wx, lru_bx, lru_lambda, rwkv_mu, rwkv_w0, rwkv_w2, rwkv_a0, rwkv_a2, rwkv_g2, rwkv_k_k, rwkv_k_a, rwkv_r_k, rwkv_ln_g, rwkv_ln_b, w_out, norm_ffn_g, w_gu, w_down, final_norm_g):
    bsz, seq, d = x.shape
    depth = w_ada.shape[0]
    dl = conv_w.shape[2]
    dr = rwkv_w0.shape[1]
    w_lora, a_lora, g_lora = rwkv_w2.shape[1], rwkv_a2.shape[1], rwkv_g2.shape[1]
    wpad, apad = LANE, LANE
    gpad = -(-g_lora // LANE) * LANE
    rkv_col0 = 2 * dl
    lora0 = rkv_col0 + 3 * dr

    x2 = x.reshape(bsz * seq, d)
    for l in range(depth):
        mod = _mod(c, w_ada[l], b_ada[l].reshape(1, -1))
        sh_m, sc_m, g_m, sh_f, sc_f, g_f = [t.reshape(bsz, 1, d) for t in jnp.split(mod, 6, axis=-1)]

        wi = jnp.swapaxes(w_in[l], 0, 1)
        o1, o2 = lora0 + w_lora, lora0 + w_lora + a_lora
        w_lora_p = jnp.concatenate(
            [_pad_rows(wi[lora0:o1], wpad), _pad_rows(wi[o1:o2], apad),
             _pad_rows(wi[o2:], gpad)], axis=0)
        mu = rwkv_mu[l].reshape(1, -1)
        mu_rkv = mu[:, :3 * dr]
        mu_lora = jnp.concatenate(
            [_pad_cols(mu[:, 3 * dr:3 * dr + w_lora], wpad),
             _pad_cols(mu[:, 3 * dr + w_lora:3 * dr + w_lora + a_lora], apad),
             _pad_cols(mu[:, 3 * dr + w_lora + a_lora:], gpad)], axis=1)
        w2p = _pad_rows(rwkv_w2[l], wpad)
        a2p = _pad_rows(rwkv_a2[l], apad)
        g2p = _pad_rows(rwkv_g2[l], gpad)

        h = _norm(x2, norm_mix_g[l].reshape(1, d), sh_m, sc_m, seq)
        p = _mm_in(h, wi, lora0)
        p_lora = _mm_in(h, w_lora_p, w_lora_p.shape[0], tn=w_lora_p.shape[0], name="mm_lora")

        y_a = _lru(p, conv_w[l], conv_b[l], lru_wa[l].astype(BF16), lru_wx[l].astype(BF16),
                   lru_ba[l], lru_bx[l], lru_lambda[l], bsz, seq)

        rowv = lambda t: t.reshape(1, dr)
        rp, yp, mc, nm, bonus, gg, w_out_b, w_gu_b, w_down_b = _rwkv_a(
            p, p_lora, mu_rkv, mu_lora, rowv(rwkv_w0[l]), rowv(rwkv_a0[l]), rowv(rwkv_k_k[l]),
            rowv(rwkv_k_a[l]), rowv(rwkv_r_k[l]), w2p, a2p, g2p, bsz, seq, rkv_col0,
            cast_ws=(w_out[l], w_gu[l], w_down[l]))
        y_b = _rwkv_b(rp, yp, mc, nm, bonus, gg, rowv(rwkv_ln_g[l]), rowv(rwkv_ln_b[l]), bsz, seq)

        x2 = _mm_out(y_a, y_b, x2, g_m, w_out_b, seq)

        last = l == depth - 1
        fg = final_norm_g.reshape(1, d) if last else None
        assert last, "only the final layer carries the closing RMSNorm"
        x2 = _ffn(x2, norm_ffn_g[l].reshape(1, d), sh_f, sc_f, g_f, w_gu_b, w_down_b, fg, seq)
    return x2.reshape(bsz, seq, d)
```

```python
import functools

import jax
import jax.numpy as jnp
from jax import lax
from jax.experimental import pallas as pl
from jax.experimental.pallas import tpu as pltpu

F32 = jnp.float32
BF16 = jnp.bfloat16

LRU_HEADS = 4
CONV_WIDTH = 4
LRU_C = 8.0
HEAD = 64
CHUNK = 64
PAIR = 2 * HEAD
HEADS_PER_STEP = 16
ONES_WIDTH = 256
CHUNKS_PER_STEP = 2
RWKV_B_CHUNKS = 4
RMS_EPS = 1e-6
GN_EPS = 64e-5
L2_EPS = 1e-12
LANE = 128
SUBLANE = 8
BF16_SUBLANE = 16
VMEM_LIMIT = 56 * 1024 * 1024


def _params(*sem):
    return pltpu.CompilerParams(dimension_semantics=sem, vmem_limit_bytes=VMEM_LIMIT)


_NN = (((1,), (0,)), ((), ()))
_NT = (((1,), (1,)), ((), ()))
_TN = (((0,), (0,)), ((), ()))


def _dg(a, b, dims):
    return lax.dot_general(a, b, dims, preferred_element_type=F32)


def _split(x):
    hi = x.astype(BF16)
    lo = (x - hi.astype(F32)).astype(BF16)
    return hi, lo


def _mm3(a, b, dims=_NN):
    ah, al = _split(a)
    bh, bl = _split(b)
    return _dg(ah, bh, dims) + (_dg(ah, bl, dims) + _dg(al, bh, dims))


def _mm3_presplit(a, bh, bl):
    ah, al = _split(a)
    return _dg(ah, bh, _NN) + (_dg(ah, bl, _NN) + _dg(al, bh, _NN))


def _mm2_exact_rhs(a, b_bf16):
    ah, al = _split(a)
    return _dg(ah, b_bf16, _NN) + _dg(al, b_bf16, _NN)


def _head_sums(x, ones_h):
    n = ones_h.shape[0]
    return jnp.concatenate([_mm2_exact_rhs(x[:, c:c + n], ones_h) for c in range(0, x.shape[1], n)],
                           axis=1)


def _mm2_exact_lhs(a_bf16, b):
    bh, bl = _split(b)
    return _dg(a_bf16, bh, _NN) + _dg(a_bf16, bl, _NN)


def _softplus(x):
    return jnp.maximum(x, 0.0) + jnp.log1p(jnp.exp(-jnp.abs(x)))


def _iota2(shape):
    return (lax.broadcasted_iota(jnp.int32, shape, 0),
            lax.broadcasted_iota(jnp.int32, shape, 1))


def _head_ones(n):
    r, c = _iota2((n, n))
    return jnp.where((r // HEAD) == (c // HEAD), 1.0, 0.0).astype(BF16)


def _mod_body(c_ref, w_ref, b_ref, o_ref):
    c = c_ref[...]
    ca = c * jax.nn.sigmoid(c)
    o_ref[...] = _mm3(ca, w_ref[...]) + b_ref[...]


def _mod(c, w, b, tn=1024):
    bsz, d = c.shape
    n = w.shape[1]
    return pl.pallas_call(
        _mod_body,
        grid=(n // tn,),
        in_specs=[pl.BlockSpec((bsz, d), lambda j: (0, 0)),
                  pl.BlockSpec((d, tn), lambda j: (0, j)),
                  pl.BlockSpec((1, tn), lambda j: (0, j))],
        out_specs=pl.BlockSpec((bsz, tn), lambda j: (0, j)),
        out_shape=jax.ShapeDtypeStruct((bsz, n), F32),
        compiler_params=_params("parallel"),
        name="mod",
    )(c, w, b)


def _norm_mod(x, g, sh, sc):
    y = x * lax.rsqrt(jnp.mean(x * x, axis=-1, keepdims=True) + RMS_EPS) * g
    return y * (1.0 + sc) + sh


def _norm_body(x_ref, g_ref, sh_ref, sc_ref, o_ref):
    o_ref[...] = _norm_mod(x_ref[...], g_ref[...], sh_ref[0], sc_ref[0]).astype(BF16)


def _norm(x2, g, sh, sc, seq, tm=512):
    m, d = x2.shape
    per_b = seq // tm
    return pl.pallas_call(
        _norm_body,
        grid=(m // tm,),
        in_specs=[pl.BlockSpec((tm, d), lambda i: (i, 0)),
                  pl.BlockSpec((1, d), lambda i: (0, 0)),
                  pl.BlockSpec((1, 1, d), lambda i: (i // per_b, 0, 0)),
                  pl.BlockSpec((1, 1, d), lambda i: (i // per_b, 0, 0))],
        out_specs=pl.BlockSpec((tm, d), lambda i: (i, 0)),
        out_shape=jax.ShapeDtypeStruct((m, d), BF16),
        compiler_params=_params("parallel"),
        name="norm_mix",
    )(x2, g, sh, sc)


def _mm_in_body(h_ref, w_ref, o_ref, wb_ref):
    @pl.when(pl.program_id(1) == 0)
    def _():
        wb_ref[...] = w_ref[...].astype(BF16)

    o_ref[...] = _dg(h_ref[...], wb_ref[...], _NT)


def _mm_in(h, wt, ncols, tm=1024, tn=1024, name="mm_in"):
    m, d = h.shape
    return pl.pallas_call(
        _mm_in_body,
        grid=(ncols // tn, m // tm),
        in_specs=[pl.BlockSpec((tm, d), lambda j, i: (i, 0)),
                  pl.BlockSpec((tn, d), lambda j, i: (j, 0))],
        out_specs=pl.BlockSpec((tm, tn), lambda j, i: (i, j)),
        out_shape=jax.ShapeDtypeStruct((m, ncols), F32),
        scratch_shapes=[pltpu.VMEM((tn, d), BF16)],
        compiler_params=_params("parallel", "arbitrary"),
        name=name,
    )(h, wt)


def _shift_rows(x, s, fill, row):
    return jnp.where(row < s, fill, pltpu.roll(x, s, 0))


def _lru_body(u_ref, gate_ref, halo_ref, cw_ref, cb_ref, wa_ref, wx_ref, ba_ref, bx_ref,
              lam_ref, o_ref, carry_ref, *, tt):
    ti = pl.program_id(1)
    first = ti == 0

    @pl.when(first)
    def _():
        carry_ref[...] = jnp.zeros_like(carry_ref)

    p = u_ref[...]
    dl = p.shape[1]
    halo = jnp.where(first, 0.0, halo_ref[...])
    ext = jnp.concatenate([halo, p], axis=0)
    cw = cw_ref[...]
    u = cb_ref[...] + p * cw[CONV_WIDTH - 1:CONV_WIDTH, :]
    for j in range(1, CONV_WIDTH):
        shifted = pltpu.roll(ext, j, 0)[SUBLANE:, :]
        u = u + shifted * cw[CONV_WIDTH - 1 - j:CONV_WIDTH - j, :]

    hd = dl // LRU_HEADS
    ub = u.astype(BF16)
    ra, rx = [], []
    for h in range(LRU_HEADS):
        uh = ub[:, h * hd:(h + 1) * hd]
        ra.append(jnp.dot(uh, wa_ref[h], preferred_element_type=F32))
        rx.append(jnp.dot(uh, wx_ref[h], preferred_element_type=F32))
    r = jax.nn.sigmoid(jnp.concatenate(ra, axis=1) + ba_ref[...])
    ig = jax.nn.sigmoid(jnp.concatenate(rx, axis=1) + bx_ref[...])
    a = jnp.exp(r * ((-LRU_C) * _softplus(-lam_ref[...])))
    mult = jnp.sqrt(1.0 - a * a)
    row = lax.broadcasted_iota(jnp.int32, (tt, dl), 0)
    mult = jnp.where(jnp.logical_and(first, row == 0), 1.0, mult)
    b = mult * (ig * u)

    groups = tt // SUBLANE
    a3 = a.reshape(groups, SUBLANE, dl)
    b3 = b.reshape(groups, SUBLANE, dl)
    sub = lax.broadcasted_iota(jnp.int32, (groups, SUBLANE, dl), 1)
    s = 1
    while s < SUBLANE:
        keep = sub >= s
        a_s = jnp.where(keep, pltpu.roll(a3, s, 1), 1.0)
        b_s = jnp.where(keep, pltpu.roll(b3, s, 1), 0.0)
        b3 = a3 * b_s + b3
        a3 = a3 * a_s
        s *= 2
    gate = jax.nn.gelu(gate_ref[...])
    carry = carry_ref[...]
    for g in range(groups):
        h = b3[g] + a3[g] * carry
        carry = h[SUBLANE - 1:SUBLANE, :]
        o_ref[g * SUBLANE:(g + 1) * SUBLANE, :] = h * gate[g * SUBLANE:(g + 1) * SUBLANE, :]
    carry_ref[...] = carry


def _lru(p, conv_w, conv_b, wa, wx, ba, bx, lam, bsz, seq, tt=256):
    dl = conv_w.shape[1]
    nt = seq // tt
    rows8 = tt // SUBLANE
    row = lambda v: v.reshape(1, dl)
    return pl.pallas_call(
        functools.partial(_lru_body, tt=tt),
        grid=(bsz, nt),
        in_specs=[pl.BlockSpec((tt, dl), lambda b, i: (b * nt + i, 0)),
                  pl.BlockSpec((tt, dl), lambda b, i: (b * nt + i, 1)),
                  pl.BlockSpec((SUBLANE, dl),
                               lambda b, i: (jnp.maximum((b * nt + i) * rows8 - 1, 0), 0)),
                  pl.BlockSpec((CONV_WIDTH, dl), lambda b, i: (0, 0)),
                  pl.BlockSpec((1, dl), lambda b, i: (0, 0)),
                  pl.BlockSpec(wa.shape, lambda b, i: (0, 0, 0)),
                  pl.BlockSpec(wx.shape, lambda b, i: (0, 0, 0)),
                  pl.BlockSpec((1, dl), lambda b, i: (0, 0)),
                  pl.BlockSpec((1, dl), lambda b, i: (0, 0)),
                  pl.BlockSpec((1, dl), lambda b, i: (0, 0))],
        out_specs=pl.BlockSpec((tt, dl), lambda b, i: (b * nt + i, 0)),
        out_shape=jax.ShapeDtypeStruct((bsz * seq, dl), F32),
        scratch_shapes=[pltpu.VMEM((1, dl), F32)],
        compiler_params=_params("parallel", "arbitrary"),
        name="lru",
    )(p, p, p, conv_w, row(conv_b), wa, wx, row(ba), row(bx), row(lam))


def _token_shift(x, halo, mu, first, row):
    prev = jnp.where(first, 0.0, halo[SUBLANE - 1:SUBLANE, :])
    xs = jnp.where(row == 0, prev, pltpu.roll(x, 1, 0))
    return x + (xs - x) * mu


def _mm1(a, b, dims=_NN):
    return _dg(a.astype(BF16), b.astype(BF16), dims)


def _pair_diag(y, left):
    return jnp.concatenate([jnp.where(left, y, 0.0), jnp.where(left, 0.0, y)], axis=0).astype(BF16)


def _pair_mm(x, y, left):
    return _dg(x.astype(BF16), _pair_diag(y, left), _NN)


def _chunk_chain(ops, store):
    ab_, bb_, kb_, rb_, v_, bt_, kt_, pe_ = ops
    rc, lane = _iota2((CHUNK, PAIR))
    cc = lane % HEAD
    left = lane < HEAD
    strict = rc > cc
    incl = rc >= cc
    diag = rc == cc
    ar16 = [jnp.concatenate([x, y], axis=0).astype(BF16) for x, y in zip(ab_, rb_)]
    bd_b = [_pair_diag(x, left) for x in bb_]
    bd_k = [_pair_diag(x, left) for x in kb_]
    bd_v = [_pair_diag(x, left) for x in v_]
    arb = [_dg(x, y, _NT) for x, y in zip(ar16, bd_b)]
    ark = [_dg(x, y, _NT) for x, y in zip(ar16, bd_k)]
    a_ab = [jnp.where(strict, x[:CHUNK], 0.0) for x in arb]
    a_rb = [jnp.where(incl, x[CHUNK:], 0.0).astype(BF16) for x in arb]
    a_akrk = [jnp.concatenate([jnp.where(strict, x[:CHUNK], 0.0), jnp.where(incl, x[CHUNK:], 0.0)],
                              axis=0).astype(BF16) for x in ark]
    yield
    base = 8
    d = [jnp.where((rc // base) == (cc // base), a, 0.0) for a in a_ab]
    d2 = [_pair_mm(t, t, left) for t in d]
    akrkv = [_dg(x, y, _NN) for x, y in zip(a_akrk, bd_v)]
    akv = [x[:CHUNK] for x in akrkv]
    rkv = [x[CHUNK:] for x in akrkv]
    x = [jnp.where(diag, 1.0, 0.0) + t for t in d]
    yield
    x = [xi + _pair_mm(t2, xi, left) for xi, t2 in zip(x, d2)]
    d4 = [_pair_mm(t2, t2, left) for t2 in d2]
    yield
    x = [xi + _pair_mm(t4, xi, left) for xi, t4 in zip(x, d4)]
    yield
    size = base
    while size < CHUNK:
        off = jnp.logical_and((rc // (2 * size)) == (cc // (2 * size)),
                              (rc // size) != (cc // size))
        o = [jnp.where(off, a, 0.0) for a in a_ab]
        ox = [_pair_mm(oi, xi, left) for oi, xi in zip(o, x)]
        yield
        x = [xi + _pair_mm(xi, oxi, left) for xi, oxi in zip(x, ox)]
        yield
        size *= 2
    t = [xi.astype(BF16) for xi in x]
    wu = [_dg(ti, jnp.concatenate([_pair_diag(y, left), _pair_diag(z, left)], axis=1), _NN)
          for ti, y, z in zip(t, ab_, akv)]
    kv = [_dg(xi.astype(BF16), y.astype(BF16), _TN) for xi, y in zip(kt_, v_)]
    yield
    ry = [_dg(xi, jnp.concatenate([_pair_diag(y[:, :PAIR], left), _pair_diag(y[:, PAIR:], left)], axis=1), _NN)
          for xi, y in zip(a_rb, wu)]
    mn = [_dg(xi.astype(BF16), y.astype(BF16), _TN) for xi, y in zip(bt_, wu)]
    yield

    def head_blocks(z):
        return jnp.where(left, z[:HEAD, :], z[HEAD:, :])

    for u in range(len(ab_)):
        store(u,
              rb_[u] + ry[u][:, :PAIR],
              ry[u][:, PAIR:] + rkv[u],
              jnp.where(diag, pe_[u], 0.0) + head_blocks(mn[u][:, :PAIR]),
              head_blocks(mn[u][:, PAIR:]) + head_blocks(kv[u]))


def _rwkv_a_body(r_ref, k_ref, v_ref, l_ref, rh_ref, kh_ref, vh_ref, lh_ref,
                 mur_ref, muk_ref, muv_ref, mul_ref, w0_ref, a0_ref, kkw_ref, kaw_ref, rkw_ref,
                 w2h_ref, w2l_ref, a2h_ref, a2l_ref, g2h_ref, g2l_ref, ones_ref, tri_ref,
                 *rest):
    ncast = (len(rest) - 6) // 2
    cast_in = rest[:ncast]
    rp_ref, yp_ref, m_ref, n_ref, bonus_ref, g_ref = rest[ncast:ncast + 6]
    cast_out = rest[ncast + 6:]
    first = pl.program_id(1) == 0
    cl = CHUNK
    rows = CHUNKS_PER_STEP * cl
    width = HEADS_PER_STEP * HEAD
    gw = ones_ref.shape[0]
    row_g = lax.broadcasted_iota(jnp.int32, (rows, gw), 0)
    row_l = lax.broadcasted_iota(jnp.int32, (rows, l_ref.shape[1]), 0)
    ones_h = ones_ref[...]

    lo = _token_shift(l_ref[...], lh_ref[...], mul_ref[...], first, row_l)
    act_w = _split(jnp.tanh(lo[:, 0:LANE]))
    act_a = _split(lo[:, LANE:2 * LANE])
    act_g = _split(jax.nn.sigmoid(lo[:, 2 * LANE:]))

    def lora(act, wh_ref, wl_ref, cs):
        (ah, al_), bh, bl = act, wh_ref[:, cs], wl_ref[:, cs]
        return _dg(ah, bh, _NN) + (_dg(ah, bl, _NN) + _dg(al_, bh, _NN))

    def prologue(c0, out):
        cs = slice(c0, c0 + gw)
        r = _token_shift(r_ref[:, cs], rh_ref[:, cs], mur_ref[:, cs], first, row_g)
        k = _token_shift(k_ref[:, cs], kh_ref[:, cs], muk_ref[:, cs], first, row_g)
        v = _token_shift(v_ref[:, cs], vh_ref[:, cs], muv_ref[:, cs], first, row_g)
        w_lin = w0_ref[:, cs] + lora(act_w, w2h_ref, w2l_ref, cs)
        a_lin = a0_ref[:, cs] + lora(act_a, a2h_ref, a2l_ref, cs)
        g_ref[:, cs] = lora(act_g, g2h_ref, g2l_ref, cs)
        kk = k * kkw_ref[:, cs]
        kk_ss = _mm2_exact_rhs(kk * kk, ones_h)
        yield
        w = -_softplus(-w_lin) - 0.5
        lw = -jnp.exp(w)
        a = jax.nn.sigmoid(a_lin)
        kk = kk / jnp.maximum(jnp.sqrt(kk_ss), L2_EPS)
        kp = k * (1.0 + (a - 1.0) * kaw_ref[:, cs])
        bonus_ref[:, cs] = _mm2_exact_rhs(r * kp * rkw_ref[:, cs], ones_h) * v
        lc = _mm2_exact_lhs(tri_ref[...], lw)
        yield
        p_incl = jnp.exp(lc)
        p_excl = jnp.exp(lc - lw)
        p_inv = jnp.exp(-lc)
        p_end = jnp.concatenate(
            [jnp.broadcast_to(p_incl[(j + 1) * cl - 1:(j + 1) * cl, :], (cl, gw))
             for j in range(CHUNKS_PER_STEP)], axis=0)
        abar = -(kk * p_excl)
        bbar = kk * a * p_inv
        kbar = kp * p_inv
        rbar = r * p_incl
        btil = bbar * p_end
        ktil = kbar * p_end
        units = [(j, q) for j in range(CHUNKS_PER_STEP) for q in range(gw // PAIR)]
        out.extend([x[j * cl:(j + 1) * cl, q * PAIR:(q + 1) * PAIR] for j, q in units]
                   for x in (abar, bbar, kbar, rbar, v, btil, ktil, p_end))
        yield

    def make_store(c0):
        units = [(j, q) for j in range(CHUNKS_PER_STEP) for q in range(gw // PAIR)]

        def store(u, rp, yp, mm, nn):
            j, q = units[u]
            rs = slice(j * cl, (j + 1) * cl)
            qs = slice(c0 + q * PAIR, c0 + (q + 1) * PAIR)
            rp_ref[rs, qs] = rp
            yp_ref[rs, qs] = yp
            m_ref[rs, qs] = mm
            n_ref[rs, qs] = nn
        return store

    chains = []
    for c0 in range(0, width, gw):
        ops = []
        for _ in prologue(c0, ops):
            for ch in chains:
                next(ch, None)
        chains.append(_chunk_chain(ops, make_store(c0)))
    live = list(chains)
    while live:
        live = [ch for ch in live if next(ch, StopIteration) is not StopIteration]

    for src, dst in zip(cast_in, cast_out):
        dst[...] = src[...].astype(BF16)


def _rwkv_a(p, p_lora, mu_rkv, mu_lora, w0, a0, k_k, k_a, r_k, w2p, a2p, g2p, bsz, seq, rkv_col0,
            cast_ws=()):
    cl = CHUNKS_PER_STEP * CHUNK
    width = HEADS_PER_STEP * HEAD
    dr = w0.shape[1]
    ngroups = dr // width
    nc = seq // cl
    lw_ = mu_lora.shape[1]
    cb0 = rkv_col0 // width
    rows8 = cl // SUBLANE
    rt, ct = _iota2((cl, cl))
    tri = jnp.where(jnp.logical_and(rt >= ct, (rt // CHUNK) == (ct // CHUNK)), 1.0, 0.0).astype(BF16)
    ones_h = _head_ones(ONES_WIDTH)
    const = lambda arr: pl.BlockSpec(arr.shape, lambda b, i, q: (0, 0))
    lora_w = [t for wgt in (w2p, a2p, g2p) for t in _split(wgt)]

    def tile(cb_off):
        return pl.BlockSpec((cl, width), lambda b, i, q: (b * nc + i, cb0 + cb_off + q))

    def halo(cb_off):
        return pl.BlockSpec(
            (SUBLANE, width),
            lambda b, i, q: (jnp.maximum((b * nc + i) * rows8 - 1, 0), cb0 + cb_off + q))

    def prow(off=0):
        return pl.BlockSpec((1, width), lambda b, i, q: (0, off + q))

    out_tile = pl.BlockSpec((cl, width), lambda b, i, q: (b * nc + i, q))
    out_mat = pl.BlockSpec((CHUNKS_PER_STEP * HEAD, width), lambda b, i, q: (b * nc + i, q))
    act = jax.ShapeDtypeStruct((bsz * seq, dr), F32)
    mat = jax.ShapeDtypeStruct((bsz * (seq // CHUNK) * HEAD, dr), F32)

    nsteps = bsz * nc * ngroups
    cast_specs = []
    for wgt in cast_ws:
        hold = 1
        while (wgt.shape[0] * hold) % (nsteps * BF16_SUBLANE) != 0:
            hold *= 2
        blk = (wgt.shape[0] * hold // nsteps, wgt.shape[1])
        cast_specs.append(pl.BlockSpec(
            blk, lambda b, i, q, hold=hold: (((b * nc + i) * ngroups + q) // hold, 0)))
    cast_shapes = [jax.ShapeDtypeStruct(wgt.shape, BF16) for wgt in cast_ws]

    return pl.pallas_call(
        _rwkv_a_body,
        grid=(bsz, nc, ngroups),
        in_specs=[tile(0), tile(ngroups), tile(2 * ngroups),
                  pl.BlockSpec((cl, lw_), lambda b, i, q: (b * nc + i, 0)),
                  halo(0), halo(ngroups), halo(2 * ngroups),
                  pl.BlockSpec((SUBLANE, lw_),
                               lambda b, i, q: (jnp.maximum((b * nc + i) * rows8 - 1, 0), 0)),
                  prow(0), prow(ngroups), prow(2 * ngroups),
                  pl.BlockSpec((1, lw_), lambda b, i, q: (0, 0)),
                  prow(), prow(), prow(), prow(), prow()]
                 + [pl.BlockSpec((t.shape[0], width), lambda b, i, q: (0, q)) for t in lora_w]
                 + [const(ones_h), const(tri)] + cast_specs,
        out_specs=[out_tile, out_tile, out_mat, out_mat, out_tile, out_tile] + cast_specs,
        out_shape=[act, act, mat, mat, act, act] + cast_shapes,
        compiler_params=_params("arbitrary", "arbitrary", "arbitrary"),
        name="rwkv_a",
    )(p, p, p, p_lora, p, p, p, p_lora, mu_rkv, mu_rkv, mu_rkv, mu_lora, w0, a0, k_k, k_a, r_k,
      *lora_w, ones_h, tri, *cast_ws)


def _rwkv_b_body(rp_ref, yp_ref, m_ref, n_ref, bonus_ref, g_ref, lng_ref, lnb_ref, ones_ref,
                 o_ref, state_ref):
    @pl.when(pl.program_id(1) == 0)
    def _():
        state_ref[...] = jnp.zeros_like(state_ref)

    npairs = state_ref.shape[0]
    pairs = range(npairs)
    ps = [slice(q * PAIR, (q + 1) * PAIR) for q in pairs]
    left = lax.broadcasted_iota(jnp.int32, (HEAD, PAIR), 1) < HEAD
    ones_h = ones_ref[...]
    inv_n = 1.0 / HEAD
    state = [state_ref[q] for q in pairs]
    for j in range(rp_ref.shape[0] // CHUNK):
        rs = slice(j * CHUNK, (j + 1) * CHUNK)
        ks = slice(j * HEAD, (j + 1) * HEAD)
        g0 = [_pair_diag(state[q], left) for q in pairs]
        ys = [_dg(rp_ref[rs, ps[q]].astype(BF16), g0[q], _NN) + yp_ref[rs, ps[q]] for q in pairs]
        state = [_dg(m_ref[ks, ps[q]].astype(BF16), g0[q], _NN) + n_ref[ks, ps[q]] for q in pairs]
        y = jnp.concatenate(ys, axis=1)
        yc = y - _head_sums(y, ones_h) * inv_n
        var = _head_sums(yc * yc, ones_h) * inv_n
        yn = yc * lax.rsqrt(var + GN_EPS) * lng_ref[...] + lnb_ref[...]
        o_ref[rs, :] = (yn + bonus_ref[rs, :]) * g_ref[rs, :]
    for q in pairs:
        state_ref[q] = state[q]


def _rwkv_b(rp, yp, mc, nm, bonus, g, ln_g, ln_b, bsz, seq):
    cl = RWKV_B_CHUNKS * CHUNK
    dr = rp.shape[1]
    nc = seq // cl
    tile = pl.BlockSpec((cl, dr), lambda b, i: (b * nc + i, 0))
    mat = pl.BlockSpec((RWKV_B_CHUNKS * HEAD, dr), lambda b, i: (b * nc + i, 0))
    prow = pl.BlockSpec((1, dr), lambda b, i: (0, 0))
    ones_h = _head_ones(ONES_WIDTH)
    return pl.pallas_call(
        _rwkv_b_body,
        grid=(bsz, nc),
        in_specs=[tile, tile, mat, mat, tile, tile, prow, prow,
                  pl.BlockSpec(ones_h.shape, lambda b, i: (0, 0))],
        out_specs=tile,
        out_shape=jax.ShapeDtypeStruct((bsz * seq, dr), F32),
        scratch_shapes=[pltpu.VMEM((dr // PAIR, HEAD, PAIR), F32)],
        compiler_params=_params("parallel", "arbitrary"),
        name="rwkv_b",
    )(rp, yp, mc, nm, bonus, g, ln_g, ln_b, ones_h)


def _mm_out_body(ya_ref, yb_ref, x_ref, gm_ref, w_ref, o_ref):
    da = ya_ref.shape[1]
    mix = (jnp.dot(ya_ref[...].astype(BF16), w_ref[:da, :], preferred_element_type=F32)
           + jnp.dot(yb_ref[...].astype(BF16), w_ref[da:, :], preferred_element_type=F32))
    o_ref[...] = x_ref[...] + gm_ref[0] * mix


def _mm_out(ya, yb, x2, gm, w, seq, tm=512):
    m, d = x2.shape
    per_b = seq // tm
    return pl.pallas_call(
        _mm_out_body,
        grid=(m // tm,),
        in_specs=[pl.BlockSpec((tm, ya.shape[1]), lambda i: (i, 0)),
                  pl.BlockSpec((tm, yb.shape[1]), lambda i: (i, 0)),
                  pl.BlockSpec((tm, d), lambda i: (i, 0)),
                  pl.BlockSpec((1, 1, d), lambda i: (i // per_b, 0, 0)),
                  pl.BlockSpec(w.shape, lambda i: (0, 0))],
        out_specs=pl.BlockSpec((tm, d), lambda i: (i, 0)),
        out_shape=jax.ShapeDtypeStruct((m, d), F32),
        compiler_params=_params("parallel"),
        name="mm_out",
    )(ya, yb, x2, gm, w)


def _ffn_body(x_ref, g_ref, sh_ref, sc_ref, gf_ref, wg_ref, wu_ref, wd_ref, fg_ref,
              o_ref, h_ref, acc_ref):
    f = pl.program_id(1)

    @pl.when(f == 0)
    def _():
        h_ref[...] = _norm_mod(x_ref[...], g_ref[...], sh_ref[0], sc_ref[0]).astype(BF16)
        acc_ref[...] = jnp.zeros_like(acc_ref)

    h = h_ref[...]
    gate = jnp.dot(h, wg_ref[...], preferred_element_type=F32)
    up = jnp.dot(h, wu_ref[...], preferred_element_type=F32)
    act = (gate * jax.nn.sigmoid(gate) * up).astype(BF16)
    acc_ref[...] += jnp.dot(act, wd_ref[...], preferred_element_type=F32)

    @pl.when(f == pl.num_programs(1) - 1)
    def _():
        y = x_ref[...] + gf_ref[0] * acc_ref[...]
        o_ref[...] = (y * lax.rsqrt(jnp.mean(y * y, axis=-1, keepdims=True) + RMS_EPS)
                      * fg_ref[...])


def _ffn(x1, g, sh, sc, gf, w_gu, w_down, fg, seq, tm=512, tf=512):
    m, d = x1.shape
    dff = w_down.shape[0]
    nf = dff // tf
    assert seq % tm == 0 and dff % tf == 0, "row tiles must not straddle sequences"
    per_b = seq // tm
    brow = pl.BlockSpec((1, 1, d), lambda i, f: (i // per_b, 0, 0))
    prow = pl.BlockSpec((1, d), lambda i, f: (0, 0))
    return pl.pallas_call(
        _ffn_body,
        grid=(m // tm, nf),
        in_specs=[pl.BlockSpec((tm, d), lambda i, f: (i, 0)),
                  prow, brow, brow, brow,
                  pl.BlockSpec((d, tf), lambda i, f: (0, f)),
                  pl.BlockSpec((d, tf), lambda i, f: (0, nf + f)),
                  pl.BlockSpec((tf, d), lambda i, f: (f, 0)),
                  prow],
        out_specs=pl.BlockSpec((tm, d), lambda i, f: (i, 0)),
        out_shape=jax.ShapeDtypeStruct((m, d), F32),
        scratch_shapes=[pltpu.VMEM((tm, d), BF16), pltpu.VMEM((tm, d), F32)],
        compiler_params=_params("parallel", "arbitrary"),
        name="ffn",
    )(x1, g, sh, sc, gf, w_gu, w_gu, w_down, fg)


def _pad_cols(w, n):
    return jnp.pad(w, ((0, 0), (0, n - w.shape[1])))


def _pad_rows(w, n):
    return jnp.pad(w, ((0, n - w.shape[0]), (0, 0)))


def kernel(x, c, w_ada, b_ada, norm_mix_g, w_in, conv_w, conv_b, lru_wa, lru_ba, lru_wx, lru_bx, lru_lambda, rwkv_mu, rwkv_w0, rwkv_w2, rwkv_a0, rwkv_a2, rwkv_g2, rwkv_k_k, rwkv_k_a, rwkv_r_k, rwkv_ln_g, rwkv_ln_b, w_out, norm_ffn_g, w_gu, w_down, final_norm_g):
    bsz, seq, d = x.shape
    depth = w_ada.shape[0]
    dl = conv_w.shape[2]
    dr = rwkv_w0.shape[1]
    w_lora, a_lora, g_lora = rwkv_w2.shape[1], rwkv_a2.shape[1], rwkv_g2.shape[1]
    wpad, apad = LANE, LANE
    gpad = -(-g_lora // LANE) * LANE
    rkv_col0 = 2 * dl
    lora0 = rkv_col0 + 3 * dr

    x2 = x.reshape(bsz * seq, d)
    for l in range(depth):
        mod = _mod(c, w_ada[l], b_ada[l].reshape(1, -1))
        sh_m, sc_m, g_m, sh_f, sc_f, g_f = [t.reshape(bsz, 1, d) for t in jnp.split(mod, 6, axis=-1)]

        wi = jnp.swapaxes(w_in[l], 0, 1)
        o1, o2 = lora0 + w_lora, lora0 + w_lora + a_lora
        w_lora_p = jnp.concatenate(
            [_pad_rows(wi[lora0:o1], wpad), _pad_rows(wi[o1:o2], apad),
             _pad_rows(wi[o2:], gpad)], axis=0)
        mu = rwkv_mu[l].reshape(1, -1)
        mu_rkv = mu[:, :3 * dr]
        mu_lora = jnp.concatenate(
            [_pad_cols(mu[:, 3 * dr:3 * dr + w_lora], wpad),
             _pad_cols(mu[:, 3 * dr + w_lora:3 * dr + w_lora + a_lora], apad),
             _pad_cols(mu[:, 3 * dr + w_lora + a_lora:], gpad)], axis=1)
        w2p = _pad_rows(rwkv_w2[l], wpad)
        a2p = _pad_rows(rwkv_a2[l], apad)
        g2p = _pad_rows(rwkv_g2[l], gpad)

        h = _norm(x2, norm_mix_g[l].reshape(1, d), sh_m, sc_m, seq)
        p = _mm_in(h, wi, lora0)
        p_lora = _mm_in(h, w_lora_p, w_lora_p.shape[0], tn=w_lora_p.shape[0], name="mm_lora")

        y_a = _lru(p, conv_w[l], conv_b[l], lru_wa[l].astype(BF16), lru_wx[l].astype(BF16),
                   lru_ba[l], lru_bx[l], lru_lambda[l], bsz, seq)

        rowv = lambda t: t.reshape(1, dr)
        rp, yp, mc, nm, bonus, gg, w_out_b, w_gu_b, w_down_b = _rwkv_a(
            p, p_lora, mu_rkv, mu_lora, rowv(rwkv_w0[l]), rowv(rwkv_a0[l]), rowv(rwkv_k_k[l]),
            rowv(rwkv_k_a[l]), rowv(rwkv_r_k[l]), w2p, a2p, g2p, bsz, seq, rkv_col0,
            cast_ws=(w_out[l], w_gu[l], w_down[l]))
        y_b = _rwkv_b(rp, yp, mc, nm, bonus, gg, rowv(rwkv_ln_g[l]), rowv(rwkv_ln_b[l]), bsz, seq)

        x2 = _mm_out(y_a, y_b, x2, g_m, w_out_b, seq)

        last = l == depth - 1
        fg = final_norm_g.reshape(1, d) if last else None
        assert last, "only the final layer carries the closing RMSNorm"
        x2 = _ffn(x2, norm_ffn_g[l].reshape(1, d), sh_f, sc_f, g_f, w_gu_b, w_down_b, fg, seq)
    return x2.reshape(bsz, seq, d)
```

```python
import functools

import jax
import jax.numpy as jnp
from jax import lax
from jax.experimental import pallas as pl
from jax.experimental.pallas import tpu as pltpu

F32 = jnp.float32
BF16 = jnp.bfloat16

LRU_HEADS = 4
CONV_WIDTH = 4
LRU_C = 8.0
HEAD = 64
CHUNK = 64
PAIR = 2 * HEAD
HEADS_PER_STEP = 16
ONES_WIDTH = 256
CHUNKS_PER_STEP = 2
RWKV_B_CHUNKS = 4
RMS_EPS = 1e-6
GN_EPS = 64e-5
L2_EPS = 1e-12
LANE = 128
SUBLANE = 8
BF16_SUBLANE = 16
VMEM_LIMIT = 56 * 1024 * 1024


def _params(*sem):
    return pltpu.CompilerParams(dimension_semantics=sem, vmem_limit_bytes=VMEM_LIMIT)


_NN = (((1,), (0,)), ((), ()))
_NT = (((1,), (1,)), ((), ()))
_TN = (((0,), (0,)), ((), ()))


def _dg(a, b, dims):
    return lax.dot_general(a, b, dims, preferred_element_type=F32)


def _split(x):
    hi = x.astype(BF16)
    lo = (x - hi.astype(F32)).astype(BF16)
    return hi, lo


def _mm3(a, b, dims=_NN):
    ah, al = _split(a)
    bh, bl = _split(b)
    return _dg(ah, bh, dims) + (_dg(ah, bl, dims) + _dg(al, bh, dims))


def _mm3_presplit(a, bh, bl):
    ah, al = _split(a)
    return _dg(ah, bh, _NN) + (_dg(ah, bl, _NN) + _dg(al, bh, _NN))


def _mm2_exact_rhs(a, b_bf16):
    ah, al = _split(a)
    return _dg(ah, b_bf16, _NN) + _dg(al, b_bf16, _NN)


def _head_sums(x, ones_h):
    n = ones_h.shape[0]
    return jnp.concatenate([_mm2_exact_rhs(x[:, c:c + n], ones_h) for c in range(0, x.shape[1], n)],
                           axis=1)


def _mm2_exact_lhs(a_bf16, b):
    bh, bl = _split(b)
    return _dg(a_bf16, bh, _NN) + _dg(a_bf16, bl, _NN)


def _softplus(x):
    return jnp.maximum(x, 0.0) + jnp.log1p(jnp.exp(-jnp.abs(x)))


def _iota2(shape):
    return (lax.broadcasted_iota(jnp.int32, shape, 0),
            lax.broadcasted_iota(jnp.int32, shape, 1))


def _head_ones(n):
    r, c = _iota2((n, n))
    return jnp.where((r // HEAD) == (c // HEAD), 1.0, 0.0).astype(BF16)


def _mod_body(c_ref, w_ref, b_ref, o_ref):
    c = c_ref[...]
    ca = c * jax.nn.sigmoid(c)
    o_ref[...] = _mm3(ca, w_ref[...]) + b_ref[...]


def _mod(c, w, b, tn=1024):
    bsz, d = c.shape
    n = w.shape[1]
    return pl.pallas_call(
        _mod_body,
        grid=(n // tn,),
        in_specs=[pl.BlockSpec((bsz, d), lambda j: (0, 0)),
                  pl.BlockSpec((d, tn), lambda j: (0, j)),
                  pl.BlockSpec((1, tn), lambda j: (0, j))],
        out_specs=pl.BlockSpec((bsz, tn), lambda j: (0, j)),
        out_shape=jax.ShapeDtypeStruct((bsz, n), F32),
        compiler_params=_params("parallel"),
        name="mod",
    )(c, w, b)


def _norm_mod(x, g, sh, sc):
    y = x * lax.rsqrt(jnp.mean(x * x, axis=-1, keepdims=True) + RMS_EPS) * g
    return y * (1.0 + sc) + sh


def _norm_body(x_ref, g_ref, sh_ref, sc_ref, o_ref):
    o_ref[...] = _norm_mod(x_ref[...], g_ref[...], sh_ref[0], sc_ref[0]).astype(BF16)


def _norm(x2, g, sh, sc, seq, tm=512):
    m, d = x2.shape
    per_b = seq // tm
    return pl.pallas_call(
        _norm_body,
        grid=(m // tm,),
        in_specs=[pl.BlockSpec((tm, d), lambda i: (i, 0)),
                  pl.BlockSpec((1, d), lambda i: (0, 0)),
                  pl.BlockSpec((1, 1, d), lambda i: (i // per_b, 0, 0)),
                  pl.BlockSpec((1, 1, d), lambda i: (i // per_b, 0, 0))],
        out_specs=pl.BlockSpec((tm, d), lambda i: (i, 0)),
        out_shape=jax.ShapeDtypeStruct((m, d), BF16),
        compiler_params=_params("parallel"),
        name="norm_mix",
    )(x2, g, sh, sc)


def _mm_in_body(h_ref, w_ref, o_ref, wb_ref):
    @pl.when(pl.program_id(1) == 0)
    def _():
        wb_ref[...] = w_ref[...].astype(BF16)

    o_ref[...] = _dg(h_ref[...], wb_ref[...], _NT)


def _mm_in(h, wt, ncols, tm=1024, tn=1024, name="mm_in"):
    m, d = h.shape
    return pl.pallas_call(
        _mm_in_body,
        grid=(ncols // tn, m // tm),
        in_specs=[pl.BlockSpec((tm, d), lambda j, i: (i, 0)),
                  pl.BlockSpec((tn, d), lambda j, i: (j, 0))],
        out_specs=pl.BlockSpec((tm, tn), lambda j, i: (i, j)),
        out_shape=jax.ShapeDtypeStruct((m, ncols), F32),
        scratch_shapes=[pltpu.VMEM((tn, d), BF16)],
        compiler_params=_params("parallel", "arbitrary"),
        name=name,
    )(h, wt)


def _lru_head(h, first, u_ref, gate_ref, halo_ref, cw_ref, cb_ref, wa_ref, wx_ref, ba_ref, bx_ref,
              lam_ref, o_ref, carry_ref):
    tt = u_ref.shape[0]
    hd = u_ref.shape[1] // LRU_HEADS
    cs = slice(h * hd, (h + 1) * hd)
    p = u_ref[:, cs]
    halo = jnp.where(first, 0.0, halo_ref[:, cs])
    ext = jnp.concatenate([halo, p], axis=0)
    cw = cw_ref[:, cs]
    u = cb_ref[:, cs] + p * cw[CONV_WIDTH - 1:CONV_WIDTH, :]
    for j in range(1, CONV_WIDTH):
        shifted = pltpu.roll(ext, j, 0)[SUBLANE:, :]
        u = u + shifted * cw[CONV_WIDTH - 1 - j:CONV_WIDTH - j, :]
    ub = u.astype(BF16)
    ra = jnp.dot(ub, wa_ref[h], preferred_element_type=F32)
    rx = jnp.dot(ub, wx_ref[h], preferred_element_type=F32)
    yield
    r = jax.nn.sigmoid(ra + ba_ref[:, cs])
    ig = jax.nn.sigmoid(rx + bx_ref[:, cs])
    a = jnp.exp(r * ((-LRU_C) * _softplus(-lam_ref[:, cs])))
    mult = jnp.sqrt(1.0 - a * a)
    row = lax.broadcasted_iota(jnp.int32, (tt, hd), 0)
    mult = jnp.where(jnp.logical_and(first, row == 0), 1.0, mult)
    b = mult * (ig * u)

    groups = tt // SUBLANE
    a3 = a.reshape(groups, SUBLANE, hd)
    b3 = b.reshape(groups, SUBLANE, hd)
    sub = lax.broadcasted_iota(jnp.int32, (groups, SUBLANE, hd), 1)
    s = 1
    while s < SUBLANE:
        keep = sub >= s
        a_s = jnp.where(keep, pltpu.roll(a3, s, 1), 1.0)
        b_s = jnp.where(keep, pltpu.roll(b3, s, 1), 0.0)
        b3 = a3 * b_s + b3
        a3 = a3 * a_s
        s *= 2
    yield
    gate = jax.nn.gelu(gate_ref[:, cs])
    carry = carry_ref[:, cs]
    for g in range(groups):
        hh = b3[g] + a3[g] * carry
        carry = hh[SUBLANE - 1:SUBLANE, :]
        o_ref[g * SUBLANE:(g + 1) * SUBLANE, cs] = hh * gate[g * SUBLANE:(g + 1) * SUBLANE, :]
    carry_ref[:, cs] = carry
    yield


def _token_shift(x, halo, mu, first, row):
    prev = jnp.where(first, 0.0, halo[SUBLANE - 1:SUBLANE, :])
    xs = jnp.where(row == 0, prev, pltpu.roll(x, 1, 0))
    return x + (xs - x) * mu


def _mm1(a, b, dims=_NN):
    return _dg(a.astype(BF16), b.astype(BF16), dims)


def _pair_diag(y, left):
    return jnp.concatenate([jnp.where(left, y, 0.0), jnp.where(left, 0.0, y)], axis=0).astype(BF16)


def _pair_mm(x, y, left):
    return _dg(x.astype(BF16), _pair_diag(y, left), _NN)


def _chunk_chain(ops, store):
    ab_, bb_, kb_, rb_, v_, bt_, kt_, pe_ = ops
    rc, lane = _iota2((CHUNK, PAIR))
    cc = lane % HEAD
    left = lane < HEAD
    strict = rc > cc
    incl = rc >= cc
    diag = rc == cc
    ar16 = [jnp.concatenate([x, y], axis=0).astype(BF16) for x, y in zip(ab_, rb_)]
    bd_b = [_pair_diag(x, left) for x in bb_]
    bd_k = [_pair_diag(x, left) for x in kb_]
    bd_v = [_pair_diag(x, left) for x in v_]
    arb = [_dg(x, y, _NT) for x, y in zip(ar16, bd_b)]
    ark = [_dg(x, y, _NT) for x, y in zip(ar16, bd_k)]
    a_ab = [jnp.where(strict, x[:CHUNK], 0.0) for x in arb]
    a_rb = [jnp.where(incl, x[CHUNK:], 0.0).astype(BF16) for x in arb]
    a_akrk = [jnp.concatenate([jnp.where(strict, x[:CHUNK], 0.0), jnp.where(incl, x[CHUNK:], 0.0)],
                              axis=0).astype(BF16) for x in ark]
    yield
    base = 8
    d = [jnp.where((rc // base) == (cc // base), a, 0.0) for a in a_ab]
    d2 = [_pair_mm(t, t, left) for t in d]
    akrkv = [_dg(x, y, _NN) for x, y in zip(a_akrk, bd_v)]
    akv = [x[:CHUNK] for x in akrkv]
    rkv = [x[CHUNK:] for x in akrkv]
    x = [jnp.where(diag, 1.0, 0.0) + t for t in d]
    yield
    x = [xi + _pair_mm(t2, xi, left) for xi, t2 in zip(x, d2)]
    d4 = [_pair_mm(t2, t2, left) for t2 in d2]
    yield
    x = [xi + _pair_mm(t4, xi, left) for xi, t4 in zip(x, d4)]
    yield
    size = base
    while size < CHUNK:
        off = jnp.logical_and((rc // (2 * size)) == (cc // (2 * size)),
                              (rc // size) != (cc // size))
        o = [jnp.where(off, a, 0.0) for a in a_ab]
        ox = [_pair_mm(oi, xi, left) for oi, xi in zip(o, x)]
        yield
        x = [xi + _pair_mm(xi, oxi, left) for xi, oxi in zip(x, ox)]
        yield
        size *= 2
    t = [xi.astype(BF16) for xi in x]
    wu = [_dg(ti, jnp.concatenate([_pair_diag(y, left), _pair_diag(z, left)], axis=1), _NN)
          for ti, y, z in zip(t, ab_, akv)]
    kv = [_dg(xi.astype(BF16), y.astype(BF16), _TN) for xi, y in zip(kt_, v_)]
    yield
    ry = [_dg(xi, jnp.concatenate([_pair_diag(y[:, :PAIR], left), _pair_diag(y[:, PAIR:], left)], axis=1), _NN)
          for xi, y in zip(a_rb, wu)]
    mn = [_dg(xi.astype(BF16), y.astype(BF16), _TN) for xi, y in zip(bt_, wu)]
    yield

    def head_blocks(z):
        return jnp.where(left, z[:HEAD, :], z[HEAD:, :])

    for u in range(len(ab_)):
        store(u,
              rb_[u] + ry[u][:, :PAIR],
              ry[u][:, PAIR:] + rkv[u],
              jnp.where(diag, pe_[u], 0.0) + head_blocks(mn[u][:, :PAIR]),
              head_blocks(mn[u][:, PAIR:]) + head_blocks(kv[u]))


def _rwkv_a_body(r_ref, k_ref, v_ref, l_ref, rh_ref, kh_ref, vh_ref, lh_ref,
                 mur_ref, muk_ref, muv_ref, mul_ref, w0_ref, a0_ref, kkw_ref, kaw_ref, rkw_ref,
                 w2h_ref, w2l_ref, a2h_ref, a2l_ref, g2h_ref, g2l_ref, ones_ref, tri_ref,
                 u_ref, gate_ref, halo_ref, cw_ref, cb_ref, wa_ref, wx_ref, ba_ref, bx_ref, lam_ref,
                 *rest):
    ncast = (len(rest) - 8) // 2
    cast_in = rest[:ncast]
    rp_ref, yp_ref, m_ref, n_ref, bonus_ref, g_ref, ya_ref = rest[ncast:ncast + 7]
    cast_out = rest[ncast + 7:-1]
    carry_ref = rest[-1]
    first = pl.program_id(1) == 0

    @pl.when(first)
    def _():
        carry_ref[...] = jnp.zeros_like(carry_ref)
    cl = CHUNK
    rows = CHUNKS_PER_STEP * cl
    width = HEADS_PER_STEP * HEAD
    gw = ones_ref.shape[0]
    row_g = lax.broadcasted_iota(jnp.int32, (rows, gw), 0)
    row_l = lax.broadcasted_iota(jnp.int32, (rows, l_ref.shape[1]), 0)
    ones_h = ones_ref[...]

    lo = _token_shift(l_ref[...], lh_ref[...], mul_ref[...], first, row_l)
    act_w = _split(jnp.tanh(lo[:, 0:LANE]))
    act_a = _split(lo[:, LANE:2 * LANE])
    act_g = _split(jax.nn.sigmoid(lo[:, 2 * LANE:]))

    def lora(act, wh_ref, wl_ref, cs):
        (ah, al_), bh, bl = act, wh_ref[:, cs], wl_ref[:, cs]
        return _dg(ah, bh, _NN) + (_dg(ah, bl, _NN) + _dg(al_, bh, _NN))

    def prologue(c0, out):
        cs = slice(c0, c0 + gw)
        r = _token_shift(r_ref[:, cs], rh_ref[:, cs], mur_ref[:, cs], first, row_g)
        k = _token_shift(k_ref[:, cs], kh_ref[:, cs], muk_ref[:, cs], first, row_g)
        v = _token_shift(v_ref[:, cs], vh_ref[:, cs], muv_ref[:, cs], first, row_g)
        w_lin = w0_ref[:, cs] + lora(act_w, w2h_ref, w2l_ref, cs)
        a_lin = a0_ref[:, cs] + lora(act_a, a2h_ref, a2l_ref, cs)
        g_ref[:, cs] = lora(act_g, g2h_ref, g2l_ref, cs)
        kk = k * kkw_ref[:, cs]
        kk_ss = _mm2_exact_rhs(kk * kk, ones_h)
        yield
        w = -_softplus(-w_lin) - 0.5
        lw = -jnp.exp(w)
        a = jax.nn.sigmoid(a_lin)
        kk = kk / jnp.maximum(jnp.sqrt(kk_ss), L2_EPS)
        kp = k * (1.0 + (a - 1.0) * kaw_ref[:, cs])
        bonus_ref[:, cs] = _mm2_exact_rhs(r * kp * rkw_ref[:, cs], ones_h) * v
        lc = _mm2_exact_lhs(tri_ref[...], lw)
        yield
        p_incl = jnp.exp(lc)
        p_excl = jnp.exp(lc - lw)
        p_inv = jnp.exp(-lc)
        p_end = jnp.concatenate(
            [jnp.broadcast_to(p_incl[(j + 1) * cl - 1:(j + 1) * cl, :], (cl, gw))
             for j in range(CHUNKS_PER_STEP)], axis=0)
        abar = -(kk * p_excl)
        bbar = kk * a * p_inv
        kbar = kp * p_inv
        rbar = r * p_incl
        btil = bbar * p_end
        ktil = kbar * p_end
        units = [(j, q) for j in range(CHUNKS_PER_STEP) for q in range(gw // PAIR)]
        out.extend([x[j * cl:(j + 1) * cl, q * PAIR:(q + 1) * PAIR] for j, q in units]
                   for x in (abar, bbar, kbar, rbar, v, btil, ktil, p_end))
        yield

    def make_store(c0):
        units = [(j, q) for j in range(CHUNKS_PER_STEP) for q in range(gw // PAIR)]

        def store(u, rp, yp, mm, nn):
            j, q = units[u]
            rs = slice(j * cl, (j + 1) * cl)
            qs = slice(c0 + q * PAIR, c0 + (q + 1) * PAIR)
            rp_ref[rs, qs] = rp
            yp_ref[rs, qs] = yp
            m_ref[rs, qs] = mm
            n_ref[rs, qs] = nn
        return store

    chains = []
    for c0 in range(0, width, gw):
        ops = []
        for _ in prologue(c0, ops):
            for ch in chains:
                next(ch, None)
        chains.append(_chunk_chain(ops, make_store(c0)))
    def lru_heads(hs):
        gens = [_lru_head(h, first, u_ref, gate_ref, halo_ref, cw_ref, cb_ref, wa_ref, wx_ref,
                          ba_ref, bx_ref, lam_ref, ya_ref, carry_ref) for h in hs]
        while gens:
            gens = [g for g in gens if next(g, StopIteration) is not StopIteration]
            yield

    lru = (None for h in range(LRU_HEADS) for _ in lru_heads([h]))
    live = list(chains)
    while live:
        live = [ch for ch in live if next(ch, StopIteration) is not StopIteration]
        next(lru, None)
    for _ in lru:
        pass

    for src, dst in zip(cast_in, cast_out):
        dst[...] = src[...].astype(BF16)


def _rwkv_a(p, p_lora, mu_rkv, mu_lora, w0, a0, k_k, k_a, r_k, w2p, a2p, g2p,
            conv_w, conv_b, wa, wx, ba, bx, lam, bsz, seq, rkv_col0, cast_ws=()):
    dl = conv_w.shape[1]
    lvec = lambda t: t.reshape(1, dl)
    lrow = pl.BlockSpec((1, dl), lambda b, i, q: (0, 0))

    def lru_tile(cb):
        return pl.BlockSpec((CHUNKS_PER_STEP * CHUNK, dl), lambda b, i, q: (b * nc + i, cb))
    cl = CHUNKS_PER_STEP * CHUNK
    width = HEADS_PER_STEP * HEAD
    dr = w0.shape[1]
    ngroups = dr // width
    assert ngroups == 1, "the LRU ride-along expects one grid step per row tile"
    nc = seq // cl
    lw_ = mu_lora.shape[1]
    cb0 = rkv_col0 // width
    rows8 = cl // SUBLANE
    rt, ct = _iota2((cl, cl))
    tri = jnp.where(jnp.logical_and(rt >= ct, (rt // CHUNK) == (ct // CHUNK)), 1.0, 0.0).astype(BF16)
    ones_h = _head_ones(ONES_WIDTH)
    const = lambda arr: pl.BlockSpec(arr.shape, lambda b, i, q: (0, 0))
    lora_w = [t for wgt in (w2p, a2p, g2p) for t in _split(wgt)]

    def tile(cb_off):
        return pl.BlockSpec((cl, width), lambda b, i, q: (b * nc + i, cb0 + cb_off + q))

    def halo(cb_off):
        return pl.BlockSpec(
            (SUBLANE, width),
            lambda b, i, q: (jnp.maximum((b * nc + i) * rows8 - 1, 0), cb0 + cb_off + q))

    def prow(off=0):
        return pl.BlockSpec((1, width), lambda b, i, q: (0, off + q))

    out_tile = pl.BlockSpec((cl, width), lambda b, i, q: (b * nc + i, q))
    out_mat = pl.BlockSpec((CHUNKS_PER_STEP * HEAD, width), lambda b, i, q: (b * nc + i, q))
    act = jax.ShapeDtypeStruct((bsz * seq, dr), F32)
    mat = jax.ShapeDtypeStruct((bsz * (seq // CHUNK) * HEAD, dr), F32)

    nsteps = bsz * nc * ngroups
    cast_specs = []
    for wgt in cast_ws:
        hold = 1
        while (wgt.shape[0] * hold) % (nsteps * BF16_SUBLANE) != 0:
            hold *= 2
        blk = (wgt.shape[0] * hold // nsteps, wgt.shape[1])
        cast_specs.append(pl.BlockSpec(
            blk, lambda b, i, q, hold=hold: (((b * nc + i) * ngroups + q) // hold, 0)))
    cast_shapes = [jax.ShapeDtypeStruct(wgt.shape, BF16) for wgt in cast_ws]

    return pl.pallas_call(
        _rwkv_a_body,
        grid=(bsz, nc, ngroups),
        in_specs=[tile(0), tile(ngroups), tile(2 * ngroups),
                  pl.BlockSpec((cl, lw_), lambda b, i, q: (b * nc + i, 0)),
                  halo(0), halo(ngroups), halo(2 * ngroups),
                  pl.BlockSpec((SUBLANE, lw_),
                               lambda b, i, q: (jnp.maximum((b * nc + i) * rows8 - 1, 0), 0)),
                  prow(0), prow(ngroups), prow(2 * ngroups),
                  pl.BlockSpec((1, lw_), lambda b, i, q: (0, 0)),
                  prow(), prow(), prow(), prow(), prow()]
                 + [pl.BlockSpec((t.shape[0], width), lambda b, i, q: (0, q)) for t in lora_w]
                 + [const(ones_h), const(tri)]
                 + [lru_tile(0), lru_tile(1),
                    pl.BlockSpec((SUBLANE, dl),
                                 lambda b, i, q: (jnp.maximum((b * nc + i) * rows8 - 1, 0), 0)),
                    const(conv_w), lrow, pl.BlockSpec(wa.shape, lambda b, i, q: (0, 0, 0)),
                    pl.BlockSpec(wx.shape, lambda b, i, q: (0, 0, 0)), lrow, lrow, lrow]
                 + cast_specs,
        out_specs=[out_tile, out_tile, out_mat, out_mat, out_tile, out_tile, lru_tile(0)] + cast_specs,
        out_shape=[act, act, mat, mat, act, act, jax.ShapeDtypeStruct((bsz * seq, dl), F32)]
                  + cast_shapes,
        scratch_shapes=[pltpu.VMEM((1, dl), F32)],
        compiler_params=_params("arbitrary", "arbitrary", "arbitrary"),
        name="rwkv_a",
    )(p, p, p, p_lora, p, p, p, p_lora, mu_rkv, mu_rkv, mu_rkv, mu_lora, w0, a0, k_k, k_a, r_k,
      *lora_w, ones_h, tri,
      p, p, p, conv_w, lvec(conv_b), wa, wx, lvec(ba), lvec(bx), lvec(lam), *cast_ws)


def _rwkv_b_body(rp_ref, yp_ref, m_ref, n_ref, bonus_ref, g_ref, lng_ref, lnb_ref, ones_ref,
                 o_ref, state_ref):
    @pl.when(pl.program_id(1) == 0)
    def _():
        state_ref[...] = jnp.zeros_like(state_ref)

    npairs = state_ref.shape[0]
    pairs = range(npairs)
    ps = [slice(q * PAIR, (q + 1) * PAIR) for q in pairs]
    left = lax.broadcasted_iota(jnp.int32, (HEAD, PAIR), 1) < HEAD
    ones_h = ones_ref[...]
    inv_n = 1.0 / HEAD
    state = [state_ref[q] for q in pairs]
    for j in range(rp_ref.shape[0] // CHUNK):
        rs = slice(j * CHUNK, (j + 1) * CHUNK)
        ks = slice(j * HEAD, (j + 1) * HEAD)
        g0 = [_pair_diag(state[q], left) for q in pairs]
        ys = [_dg(rp_ref[rs, ps[q]].astype(BF16), g0[q], _NN) + yp_ref[rs, ps[q]] for q in pairs]
        state = [_dg(m_ref[ks, ps[q]].astype(BF16), g0[q], _NN) + n_ref[ks, ps[q]] for q in pairs]
        y = jnp.concatenate(ys, axis=1)
        yc = y - _head_sums(y, ones_h) * inv_n
        var = _head_sums(yc * yc, ones_h) * inv_n
        yn = yc * lax.rsqrt(var + GN_EPS) * lng_ref[...] + lnb_ref[...]
        o_ref[rs, :] = (yn + bonus_ref[rs, :]) * g_ref[rs, :]
    for q in pairs:
        state_ref[q] = state[q]


def _rwkv_b(rp, yp, mc, nm, bonus, g, ln_g, ln_b, bsz, seq):
    cl = RWKV_B_CHUNKS * CHUNK
    dr = rp.shape[1]
    nc = seq // cl
    tile = pl.BlockSpec((cl, dr), lambda b, i: (b * nc + i, 0))
    mat = pl.BlockSpec((RWKV_B_CHUNKS * HEAD, dr), lambda b, i: (b * nc + i, 0))
    prow = pl.BlockSpec((1, dr), lambda b, i: (0, 0))
    ones_h = _head_ones(ONES_WIDTH)
    return pl.pallas_call(
        _rwkv_b_body,
        grid=(bsz, nc),
        in_specs=[tile, tile, mat, mat, tile, tile, prow, prow,
                  pl.BlockSpec(ones_h.shape, lambda b, i: (0, 0))],
        out_specs=tile,
        out_shape=jax.ShapeDtypeStruct((bsz * seq, dr), F32),
        scratch_shapes=[pltpu.VMEM((dr // PAIR, HEAD, PAIR), F32)],
        compiler_params=_params("parallel", "arbitrary"),
        name="rwkv_b",
    )(rp, yp, mc, nm, bonus, g, ln_g, ln_b, ones_h)


def _mm_out_body(ya_ref, yb_ref, x_ref, gm_ref, w_ref, o_ref):
    da = ya_ref.shape[1]
    mix = (jnp.dot(ya_ref[...].astype(BF16), w_ref[:da, :], preferred_element_type=F32)
           + jnp.dot(yb_ref[...].astype(BF16), w_ref[da:, :], preferred_element_type=F32))
    o_ref[...] = x_ref[...] + gm_ref[0] * mix


def _mm_out(ya, yb, x2, gm, w, seq, tm=512):
    m, d = x2.shape
    per_b = seq // tm
    return pl.pallas_call(
        _mm_out_body,
        grid=(m // tm,),
        in_specs=[pl.BlockSpec((tm, ya.shape[1]), lambda i: (i, 0)),
                  pl.BlockSpec((tm, yb.shape[1]), lambda i: (i, 0)),
                  pl.BlockSpec((tm, d), lambda i: (i, 0)),
                  pl.BlockSpec((1, 1, d), lambda i: (i // per_b, 0, 0)),
                  pl.BlockSpec(w.shape, lambda i: (0, 0))],
        out_specs=pl.BlockSpec((tm, d), lambda i: (i, 0)),
        out_shape=jax.ShapeDtypeStruct((m, d), F32),
        compiler_params=_params("parallel"),
        name="mm_out",
    )(ya, yb, x2, gm, w)


def _ffn_body(x_ref, g_ref, sh_ref, sc_ref, gf_ref, wg_ref, wu_ref, wd_ref, fg_ref,
              o_ref, h_ref, acc_ref):
    f = pl.program_id(1)

    @pl.when(f == 0)
    def _():
        h_ref[...] = _norm_mod(x_ref[...], g_ref[...], sh_ref[0], sc_ref[0]).astype(BF16)
        acc_ref[...] = jnp.zeros_like(acc_ref)

    h = h_ref[...]
    gate = jnp.dot(h, wg_ref[...], preferred_element_type=F32)
    up = jnp.dot(h, wu_ref[...], preferred_element_type=F32)
    act = (gate * jax.nn.sigmoid(gate) * up).astype(BF16)
    acc_ref[...] += jnp.dot(act, wd_ref[...], preferred_element_type=F32)

    @pl.when(f == pl.num_programs(1) - 1)
    def _():
        y = x_ref[...] + gf_ref[0] * acc_ref[...]
        o_ref[...] = (y * lax.rsqrt(jnp.mean(y * y, axis=-1, keepdims=True) + RMS_EPS)
                      * fg_ref[...])


def _ffn(x1, g, sh, sc, gf, w_gu, w_down, fg, seq, tm=512, tf=512):
    m, d = x1.shape
    dff = w_down.shape[0]
    nf = dff // tf
    assert seq % tm == 0 and dff % tf == 0, "row tiles must not straddle sequences"
    per_b = seq // tm
    brow = pl.BlockSpec((1, 1, d), lambda i, f: (i // per_b, 0, 0))
    prow = pl.BlockSpec((1, d), lambda i, f: (0, 0))
    return pl.pallas_call(
        _ffn_body,
        grid=(m // tm, nf),
        in_specs=[pl.BlockSpec((tm, d), lambda i, f: (i, 0)),
                  prow, brow, brow, brow,
                  pl.BlockSpec((d, tf), lambda i, f: (0, f)),
                  pl.BlockSpec((d, tf), lambda i, f: (0, nf + f)),
                  pl.BlockSpec((tf, d), lambda i, f: (f, 0)),
                  prow],
        out_specs=pl.BlockSpec((tm, d), lambda i, f: (i, 0)),
        out_shape=jax.ShapeDtypeStruct((m, d), F32),
        scratch_shapes=[pltpu.VMEM((tm, d), BF16), pltpu.VMEM((tm, d), F32)],
        compiler_params=_params("parallel", "arbitrary"),
        name="ffn",
    )(x1, g, sh, sc, gf, w_gu, w_gu, w_down, fg)


def _pad_cols(w, n):
    return jnp.pad(w, ((0, 0), (0, n - w.shape[1])))


def _pad_rows(w, n):
    return jnp.pad(w, ((0, n - w.shape[0]), (0, 0)))


def kernel(x, c, w_ada, b_ada, norm_mix_g, w_in, conv_w, conv_b, lru_wa, lru_ba, lru_wx, lru_bx, lru_lambda, rwkv_mu, rwkv_w0, rwkv_w2, rwkv_a0, rwkv_a2, rwkv_g2, rwkv_k_k, rwkv_k_a, rwkv_r_k, rwkv_ln_g, rwkv_ln_b, w_out, norm_ffn_g, w_gu, w_down, final_norm_g):
    bsz, seq, d = x.shape
    depth = w_ada.shape[0]
    dl = conv_w.shape[2]
    dr = rwkv_w0.shape[1]
    w_lora, a_lora, g_lora = rwkv_w2.shape[1], rwkv_a2.shape[1], rwkv_g2.shape[1]
    wpad, apad = LANE, LANE
    gpad = -(-g_lora // LANE) * LANE
    rkv_col0 = 2 * dl
    lora0 = rkv_col0 + 3 * dr

    x2 = x.reshape(bsz * seq, d)
    for l in range(depth):
        mod = _mod(c, w_ada[l], b_ada[l].reshape(1, -1))
        sh_m, sc_m, g_m, sh_f, sc_f, g_f = [t.reshape(bsz, 1, d) for t in jnp.split(mod, 6, axis=-1)]

        wi = jnp.swapaxes(w_in[l], 0, 1)
        o1, o2 = lora0 + w_lora, lora0 + w_lora + a_lora
        w_lora_p = jnp.concatenate(
            [_pad_rows(wi[lora0:o1], wpad), _pad_rows(wi[o1:o2], apad),
             _pad_rows(wi[o2:], gpad)], axis=0)
        mu = rwkv_mu[l].reshape(1, -1)
        mu_rkv = mu[:, :3 * dr]
        mu_lora = jnp.concatenate(
            [_pad_cols(mu[:, 3 * dr:3 * dr + w_lora], wpad),
             _pad_cols(mu[:, 3 * dr + w_lora:3 * dr + w_lora + a_lora], apad),
             _pad_cols(mu[:, 3 * dr + w_lora + a_lora:], gpad)], axis=1)
        w2p = _pad_rows(rwkv_w2[l], wpad)
        a2p = _pad_rows(rwkv_a2[l], apad)
        g2p = _pad_rows(rwkv_g2[l], gpad)

        h = _norm(x2, norm_mix_g[l].reshape(1, d), sh_m, sc_m, seq)
        p = _mm_in(h, wi, lora0)
        p_lora = _mm_in(h, w_lora_p, w_lora_p.shape[0], tn=w_lora_p.shape[0], name="mm_lora")

        rowv = lambda t: t.reshape(1, dr)
        rp, yp, mc, nm, bonus, gg, y_a, w_out_b, w_gu_b, w_down_b = _rwkv_a(
            p, p_lora, mu_rkv, mu_lora, rowv(rwkv_w0[l]), rowv(rwkv_a0[l]), rowv(rwkv_k_k[l]),
            rowv(rwkv_k_a[l]), rowv(rwkv_r_k[l]), w2p, a2p, g2p,
            conv_w[l], conv_b[l], lru_wa[l].astype(BF16), lru_wx[l].astype(BF16),
            lru_ba[l], lru_bx[l], lru_lambda[l], bsz, seq, rkv_col0,
            cast_ws=(w_out[l], w_gu[l], w_down[l]))
        y_b = _rwkv_b(rp, yp, mc, nm, bonus, gg, rowv(rwkv_ln_g[l]), rowv(rwkv_ln_b[l]), bsz, seq)

        x2 = _mm_out(y_a, y_b, x2, g_m, w_out_b, seq)

        last = l == depth - 1
        fg = final_norm_g.reshape(1, d) if last else None
        assert last, "only the final layer carries the closing RMSNorm"
        x2 = _ffn(x2, norm_ffn_g[l].reshape(1, d), sh_f, sc_f, g_f, w_gu_b, w_down_b, fg, seq)
    return x2.reshape(bsz, seq, d)
```

```python
import functools

import jax
import jax.numpy as jnp
from jax import lax
from jax.experimental import pallas as pl
from jax.experimental.pallas import tpu as pltpu

F32 = jnp.float32
BF16 = jnp.bfloat16

LRU_HEADS = 4
CONV_WIDTH = 4
LRU_C = 8.0
HEAD = 64
CHUNK = 64
PAIR = 2 * HEAD
HEADS_PER_STEP = 16
ONES_WIDTH = 256
CHUNKS_PER_STEP = 2
RWKV_B_CHUNKS = 4
RMS_EPS = 1e-6
GN_EPS = 64e-5
L2_EPS = 1e-12
LANE = 128
SUBLANE = 8
BF16_SUBLANE = 16
VMEM_LIMIT = 56 * 1024 * 1024


def _params(*sem):
    return pltpu.CompilerParams(dimension_semantics=sem, vmem_limit_bytes=VMEM_LIMIT)


_NN = (((1,), (0,)), ((), ()))
_NT = (((1,), (1,)), ((), ()))
_TN = (((0,), (0,)), ((), ()))


def _dg(a, b, dims):
    return lax.dot_general(a, b, dims, preferred_element_type=F32)


def _split(x):
    hi = x.astype(BF16)
    lo = (x - hi.astype(F32)).astype(BF16)
    return hi, lo


def _mm3(a, b, dims=_NN):
    ah, al = _split(a)
    bh, bl = _split(b)
    return _dg(ah, bh, dims) + (_dg(ah, bl, dims) + _dg(al, bh, dims))


def _mm3_presplit(a, bh, bl):
    ah, al = _split(a)
    return _dg(ah, bh, _NN) + (_dg(ah, bl, _NN) + _dg(al, bh, _NN))


def _mm2_exact_rhs(a, b_bf16):
    ah, al = _split(a)
    return _dg(ah, b_bf16, _NN) + _dg(al, b_bf16, _NN)


def _head_sums(x, ones_h):
    n = ones_h.shape[0]
    return jnp.concatenate([_mm2_exact_rhs(x[:, c:c + n], ones_h) for c in range(0, x.shape[1], n)],
                           axis=1)


def _mm2_exact_lhs(a_bf16, b):
    bh, bl = _split(b)
    return _dg(a_bf16, bh, _NN) + _dg(a_bf16, bl, _NN)


def _softplus(x):
    return jnp.maximum(x, 0.0) + jnp.log1p(jnp.exp(-jnp.abs(x)))


def _iota2(shape):
    return (lax.broadcasted_iota(jnp.int32, shape, 0),
            lax.broadcasted_iota(jnp.int32, shape, 1))


def _head_ones(n):
    r, c = _iota2((n, n))
    return jnp.where((r // HEAD) == (c // HEAD), 1.0, 0.0).astype(BF16)


def _mod_body(c_ref, w_ref, b_ref, o_ref):
    c = c_ref[...]
    ca = c * jax.nn.sigmoid(c)
    o_ref[...] = _mm3(ca, w_ref[...]) + b_ref[...]


def _mod(c, w, b, tn=1024):
    bsz, d = c.shape
    n = w.shape[1]
    return pl.pallas_call(
        _mod_body,
        grid=(n // tn,),
        in_specs=[pl.BlockSpec((bsz, d), lambda j: (0, 0)),
                  pl.BlockSpec((d, tn), lambda j: (0, j)),
                  pl.BlockSpec((1, tn), lambda j: (0, j))],
        out_specs=pl.BlockSpec((bsz, tn), lambda j: (0, j)),
        out_shape=jax.ShapeDtypeStruct((bsz, n), F32),
        compiler_params=_params("parallel"),
        name="mod",
    )(c, w, b)


def _norm_mod(x, g, sh, sc):
    y = x * lax.rsqrt(jnp.mean(x * x, axis=-1, keepdims=True) + RMS_EPS) * g
    return y * (1.0 + sc) + sh


def _norm_body(x_ref, g_ref, sh_ref, sc_ref, o_ref):
    o_ref[...] = _norm_mod(x_ref[...], g_ref[...], sh_ref[0], sc_ref[0]).astype(BF16)


def _norm(x2, g, sh, sc, seq, tm=512):
    m, d = x2.shape
    per_b = seq // tm
    return pl.pallas_call(
        _norm_body,
        grid=(m // tm,),
        in_specs=[pl.BlockSpec((tm, d), lambda i: (i, 0)),
                  pl.BlockSpec((1, d), lambda i: (0, 0)),
                  pl.BlockSpec((1, 1, d), lambda i: (i // per_b, 0, 0)),
                  pl.BlockSpec((1, 1, d), lambda i: (i // per_b, 0, 0))],
        out_specs=pl.BlockSpec((tm, d), lambda i: (i, 0)),
        out_shape=jax.ShapeDtypeStruct((m, d), BF16),
        compiler_params=_params("parallel"),
        name="norm_mix",
    )(x2, g, sh, sc)


def _mm_in_body(h_ref, w_ref, o_ref, wb_ref):
    @pl.when(pl.program_id(1) == 0)
    def _():
        wb_ref[...] = w_ref[...].astype(BF16)

    o_ref[...] = _dg(h_ref[...], wb_ref[...], _NT)


def _mm_in(h, wt, ncols, tm=1024, tn=1024, name="mm_in"):
    m, d = h.shape
    return pl.pallas_call(
        _mm_in_body,
        grid=(ncols // tn, m // tm),
        in_specs=[pl.BlockSpec((tm, d), lambda j, i: (i, 0)),
                  pl.BlockSpec((tn, d), lambda j, i: (j, 0))],
        out_specs=pl.BlockSpec((tm, tn), lambda j, i: (i, j)),
        out_shape=jax.ShapeDtypeStruct((m, ncols), F32),
        scratch_shapes=[pltpu.VMEM((tn, d), BF16)],
        compiler_params=_params("parallel", "arbitrary"),
        name=name,
    )(h, wt)


def _lru_head(h, first, u_ref, gate_ref, halo_ref, cw_ref, cb_ref, wa_ref, wx_ref, ba_ref, bx_ref,
              lam_ref, o_ref, carry_ref):
    tt = u_ref.shape[0]
    hd = u_ref.shape[1] // LRU_HEADS
    cs = slice(h * hd, (h + 1) * hd)
    p = u_ref[:, cs]
    halo = jnp.where(first, 0.0, halo_ref[:, cs])
    ext = jnp.concatenate([halo, p], axis=0)
    cw = cw_ref[:, cs]
    u = cb_ref[:, cs] + p * cw[CONV_WIDTH - 1:CONV_WIDTH, :]
    for j in range(1, CONV_WIDTH):
        shifted = pltpu.roll(ext, j, 0)[SUBLANE:, :]
        u = u + shifted * cw[CONV_WIDTH - 1 - j:CONV_WIDTH - j, :]
    ub = u.astype(BF16)
    ra = jnp.dot(ub, wa_ref[h], preferred_element_type=F32)
    rx = jnp.dot(ub, wx_ref[h], preferred_element_type=F32)
    yield
    r = jax.nn.sigmoid(ra + ba_ref[:, cs])
    ig = jax.nn.sigmoid(rx + bx_ref[:, cs])
    a = jnp.exp(r * ((-LRU_C) * _softplus(-lam_ref[:, cs])))
    mult = jnp.sqrt(1.0 - a * a)
    row = lax.broadcasted_iota(jnp.int32, (tt, hd), 0)
    mult = jnp.where(jnp.logical_and(first, row == 0), 1.0, mult)
    b = mult * (ig * u)

    groups = tt // SUBLANE
    a3 = a.reshape(groups, SUBLANE, hd)
    b3 = b.reshape(groups, SUBLANE, hd)
    sub = lax.broadcasted_iota(jnp.int32, (groups, SUBLANE, hd), 1)
    s = 1
    while s < SUBLANE:
        keep = sub >= s
        a_s = jnp.where(keep, pltpu.roll(a3, s, 1), 1.0)
        b_s = jnp.where(keep, pltpu.roll(b3, s, 1), 0.0)
        b3 = a3 * b_s + b3
        a3 = a3 * a_s
        s *= 2
    yield
    gate = jax.nn.gelu(gate_ref[:, cs])
    carry = carry_ref[:, cs]
    for g in range(groups):
        hh = b3[g] + a3[g] * carry
        carry = hh[SUBLANE - 1:SUBLANE, :]
        o_ref[g * SUBLANE:(g + 1) * SUBLANE, cs] = hh * gate[g * SUBLANE:(g + 1) * SUBLANE, :]
    carry_ref[:, cs] = carry
    yield


def _token_shift(x, halo, mu, first, row):
    prev = jnp.where(first, 0.0, halo[SUBLANE - 1:SUBLANE, :])
    xs = jnp.where(row == 0, prev, pltpu.roll(x, 1, 0))
    return x + (xs - x) * mu


def _mm1(a, b, dims=_NN):
    return _dg(a.astype(BF16), b.astype(BF16), dims)


def _pair_diag(y, left):
    return jnp.concatenate([jnp.where(left, y, 0.0), jnp.where(left, 0.0, y)], axis=0).astype(BF16)


def _pair_mm(x, y, left):
    return _dg(x.astype(BF16), _pair_diag(y, left), _NN)


def _chunk_chain(ops, store):
    ab_, bb_, kb_, rb_, v_, bt_, kt_, pe_ = ops
    rc, lane = _iota2((CHUNK, PAIR))
    cc = lane % HEAD
    left = lane < HEAD
    strict = rc > cc
    incl = rc >= cc
    diag = rc == cc
    ar16 = [jnp.concatenate([x, y], axis=0).astype(BF16) for x, y in zip(ab_, rb_)]
    bd_b = [_pair_diag(x, left) for x in bb_]
    bd_k = [_pair_diag(x, left) for x in kb_]
    bd_v = [_pair_diag(x, left) for x in v_]
    arb = [_dg(x, y, _NT) for x, y in zip(ar16, bd_b)]
    ark = [_dg(x, y, _NT) for x, y in zip(ar16, bd_k)]
    a_ab = [jnp.where(strict, x[:CHUNK], 0.0) for x in arb]
    a_rb = [jnp.where(incl, x[CHUNK:], 0.0).astype(BF16) for x in arb]
    a_akrk = [jnp.concatenate([jnp.where(strict, x[:CHUNK], 0.0), jnp.where(incl, x[CHUNK:], 0.0)],
                              axis=0).astype(BF16) for x in ark]
    yield
    base = 8
    d = [jnp.where((rc // base) == (cc // base), a, 0.0) for a in a_ab]
    d2 = [_pair_mm(t, t, left) for t in d]
    akrkv = [_dg(x, y, _NN) for x, y in zip(a_akrk, bd_v)]
    akv = [x[:CHUNK] for x in akrkv]
    rkv = [x[CHUNK:] for x in akrkv]
    x = [jnp.where(diag, 1.0, 0.0) + t for t in d]
    yield
    x = [xi + _pair_mm(t2, xi, left) for xi, t2 in zip(x, d2)]
    d4 = [_pair_mm(t2, t2, left) for t2 in d2]
    yield
    x = [xi + _pair_mm(t4, xi, left) for xi, t4 in zip(x, d4)]
    yield
    size = base
    while size < CHUNK:
        off = jnp.logical_and((rc // (2 * size)) == (cc // (2 * size)),
                              (rc // size) != (cc // size))
        o = [jnp.where(off, a, 0.0) for a in a_ab]
        ox = [_pair_mm(oi, xi, left) for oi, xi in zip(o, x)]
        yield
        x = [xi + _pair_mm(xi, oxi, left) for xi, oxi in zip(x, ox)]
        yield
        size *= 2
    t = [xi.astype(BF16) for xi in x]
    wu = [_dg(ti, jnp.concatenate([_pair_diag(y, left), _pair_diag(z, left)], axis=1), _NN)
          for ti, y, z in zip(t, ab_, akv)]
    kv = [_dg(xi.astype(BF16), y.astype(BF16), _TN) for xi, y in zip(kt_, v_)]
    yield
    ry = [_dg(xi, jnp.concatenate([_pair_diag(y[:, :PAIR], left), _pair_diag(y[:, PAIR:], left)], axis=1), _NN)
          for xi, y in zip(a_rb, wu)]
    mn = [_dg(xi.astype(BF16), y.astype(BF16), _TN) for xi, y in zip(bt_, wu)]
    yield

    def head_blocks(z):
        return jnp.where(left, z[:HEAD, :], z[HEAD:, :])

    for u in range(len(ab_)):
        store(u,
              rb_[u] + ry[u][:, :PAIR],
              ry[u][:, PAIR:] + rkv[u],
              jnp.where(diag, pe_[u], 0.0) + head_blocks(mn[u][:, :PAIR]),
              head_blocks(mn[u][:, PAIR:]) + head_blocks(kv[u]))


def _rwkv_a_body(r_ref, k_ref, v_ref, l_ref, rh_ref, kh_ref, vh_ref, lh_ref,
                 mur_ref, muk_ref, muv_ref, mul_ref, w0_ref, a0_ref, kkw_ref, kaw_ref, rkw_ref,
                 w2h_ref, w2l_ref, a2h_ref, a2l_ref, g2h_ref, g2l_ref, ones_ref, tri_ref,
                 u_ref, gate_ref, halo_ref, cw_ref, cb_ref, wa_ref, wx_ref, ba_ref, bx_ref, lam_ref,
                 *rest):
    ncast = (len(rest) - 8) // 2
    cast_in = rest[:ncast]
    rp_ref, yp_ref, m_ref, n_ref, bonus_ref, g_ref, ya_ref = rest[ncast:ncast + 7]
    cast_out = rest[ncast + 7:-1]
    carry_ref = rest[-1]
    first = pl.program_id(1) == 0

    @pl.when(first)
    def _():
        carry_ref[...] = jnp.zeros_like(carry_ref)
    cl = CHUNK
    rows = CHUNKS_PER_STEP * cl
    width = HEADS_PER_STEP * HEAD
    gw = ones_ref.shape[0]
    row_g = lax.broadcasted_iota(jnp.int32, (rows, gw), 0)
    row_l = lax.broadcasted_iota(jnp.int32, (rows, l_ref.shape[1]), 0)
    ones_h = ones_ref[...]

    lo = _token_shift(l_ref[...], lh_ref[...], mul_ref[...], first, row_l)
    act_w = _split(jnp.tanh(lo[:, 0:LANE]))
    act_a = _split(lo[:, LANE:2 * LANE])
    act_g = _split(jax.nn.sigmoid(lo[:, 2 * LANE:]))

    def lora(act, wh_ref, wl_ref, cs):
        (ah, al_), bh, bl = act, wh_ref[:, cs], wl_ref[:, cs]
        return _dg(ah, bh, _NN) + (_dg(ah, bl, _NN) + _dg(al_, bh, _NN))

    def prologue(c0, out):
        cs = slice(c0, c0 + gw)
        r = _token_shift(r_ref[:, cs], rh_ref[:, cs], mur_ref[:, cs], first, row_g)
        k = _token_shift(k_ref[:, cs], kh_ref[:, cs], muk_ref[:, cs], first, row_g)
        v = _token_shift(v_ref[:, cs], vh_ref[:, cs], muv_ref[:, cs], first, row_g)
        w_lin = w0_ref[:, cs] + lora(act_w, w2h_ref, w2l_ref, cs)
        a_lin = a0_ref[:, cs] + lora(act_a, a2h_ref, a2l_ref, cs)
        g_ref[:, cs] = lora(act_g, g2h_ref, g2l_ref, cs)
        kk = k * kkw_ref[:, cs]
        kk_ss = _mm2_exact_rhs(kk * kk, ones_h)
        yield
        w = -_softplus(-w_lin) - 0.5
        lw = -jnp.exp(w)
        a = jax.nn.sigmoid(a_lin)
        kk = kk / jnp.maximum(jnp.sqrt(kk_ss), L2_EPS)
        kp = k * (1.0 + (a - 1.0) * kaw_ref[:, cs])
        bonus_ref[:, cs] = _mm2_exact_rhs(r * kp * rkw_ref[:, cs], ones_h) * v
        lc = _mm2_exact_lhs(tri_ref[...], lw)
        yield
        p_incl = jnp.exp(lc)
        p_excl = jnp.exp(lc - lw)
        p_inv = jnp.exp(-lc)
        p_end = jnp.concatenate(
            [jnp.broadcast_to(p_incl[(j + 1) * cl - 1:(j + 1) * cl, :], (cl, gw))
             for j in range(CHUNKS_PER_STEP)], axis=0)
        abar = -(kk * p_excl)
        bbar = kk * a * p_inv
        kbar = kp * p_inv
        rbar = r * p_incl
        btil = bbar * p_end
        ktil = kbar * p_end
        units = [(j, q) for j in range(CHUNKS_PER_STEP) for q in range(gw // PAIR)]
        out.extend([x[j * cl:(j + 1) * cl, q * PAIR:(q + 1) * PAIR] for j, q in units]
                   for x in (abar, bbar, kbar, rbar, v, btil, ktil, p_end))
        yield

    def make_store(c0):
        units = [(j, q) for j in range(CHUNKS_PER_STEP) for q in range(gw // PAIR)]

        def store(u, rp, yp, mm, nn):
            j, q = units[u]
            rs = slice(j * cl, (j + 1) * cl)
            qs = slice(c0 + q * PAIR, c0 + (q + 1) * PAIR)
            rp_ref[rs, qs] = rp.astype(BF16)
            yp_ref[rs, qs] = yp
            m_ref[rs, qs] = mm.astype(BF16)
            n_ref[rs, qs] = nn
        return store

    chains = []
    for c0 in range(0, width, gw):
        ops = []
        for _ in prologue(c0, ops):
            for ch in chains:
                next(ch, None)
        chains.append(_chunk_chain(ops, make_store(c0)))
    def lru_heads(hs):
        gens = [_lru_head(h, first, u_ref, gate_ref, halo_ref, cw_ref, cb_ref, wa_ref, wx_ref,
                          ba_ref, bx_ref, lam_ref, ya_ref, carry_ref) for h in hs]
        while gens:
            gens = [g for g in gens if next(g, StopIteration) is not StopIteration]
            yield

    lru = (None for h in range(LRU_HEADS) for _ in lru_heads([h]))
    live = list(chains)
    while live:
        live = [ch for ch in live if next(ch, StopIteration) is not StopIteration]
        next(lru, None)
    for _ in lru:
        pass

    for src, dst in zip(cast_in, cast_out):
        dst[...] = src[...].astype(BF16)


def _rwkv_a(p, p_lora, mu_rkv, mu_lora, w0, a0, k_k, k_a, r_k, w2p, a2p, g2p,
            conv_w, conv_b, wa, wx, ba, bx, lam, bsz, seq, rkv_col0, cast_ws=()):
    dl = conv_w.shape[1]
    lvec = lambda t: t.reshape(1, dl)
    lrow = pl.BlockSpec((1, dl), lambda b, i, q: (0, 0))

    def lru_tile(cb):
        return pl.BlockSpec((CHUNKS_PER_STEP * CHUNK, dl), lambda b, i, q: (b * nc + i, cb))
    cl = CHUNKS_PER_STEP * CHUNK
    width = HEADS_PER_STEP * HEAD
    dr = w0.shape[1]
    ngroups = dr // width
    assert ngroups == 1, "the LRU ride-along expects one grid step per row tile"
    nc = seq // cl
    lw_ = mu_lora.shape[1]
    cb0 = rkv_col0 // width
    rows8 = cl // SUBLANE
    rt, ct = _iota2((cl, cl))
    tri = jnp.where(jnp.logical_and(rt >= ct, (rt // CHUNK) == (ct // CHUNK)), 1.0, 0.0).astype(BF16)
    ones_h = _head_ones(ONES_WIDTH)
    const = lambda arr: pl.BlockSpec(arr.shape, lambda b, i, q: (0, 0))
    lora_w = [t for wgt in (w2p, a2p, g2p) for t in _split(wgt)]

    def tile(cb_off):
        return pl.BlockSpec((cl, width), lambda b, i, q: (b * nc + i, cb0 + cb_off + q))

    def halo(cb_off):
        return pl.BlockSpec(
            (SUBLANE, width),
            lambda b, i, q: (jnp.maximum((b * nc + i) * rows8 - 1, 0), cb0 + cb_off + q))

    def prow(off=0):
        return pl.BlockSpec((1, width), lambda b, i, q: (0, off + q))

    out_tile = pl.BlockSpec((cl, width), lambda b, i, q: (b * nc + i, q))
    out_mat = pl.BlockSpec((CHUNKS_PER_STEP * HEAD, width), lambda b, i, q: (b * nc + i, q))
    act = jax.ShapeDtypeStruct((bsz * seq, dr), F32)
    mat = jax.ShapeDtypeStruct((bsz * (seq // CHUNK) * HEAD, dr), F32)

    nsteps = bsz * nc * ngroups
    cast_specs = []
    for wgt in cast_ws:
        hold = 1
        while (wgt.shape[0] * hold) % (nsteps * BF16_SUBLANE) != 0:
            hold *= 2
        blk = (wgt.shape[0] * hold // nsteps, wgt.shape[1])
        cast_specs.append(pl.BlockSpec(
            blk, lambda b, i, q, hold=hold: (((b * nc + i) * ngroups + q) // hold, 0)))
    cast_shapes = [jax.ShapeDtypeStruct(wgt.shape, BF16) for wgt in cast_ws]

    return pl.pallas_call(
        _rwkv_a_body,
        grid=(bsz, nc, ngroups),
        in_specs=[tile(0), tile(ngroups), tile(2 * ngroups),
                  pl.BlockSpec((cl, lw_), lambda b, i, q: (b * nc + i, 0)),
                  halo(0), halo(ngroups), halo(2 * ngroups),
                  pl.BlockSpec((SUBLANE, lw_),
                               lambda b, i, q: (jnp.maximum((b * nc + i) * rows8 - 1, 0), 0)),
                  prow(0), prow(ngroups), prow(2 * ngroups),
                  pl.BlockSpec((1, lw_), lambda b, i, q: (0, 0)),
                  prow(), prow(), prow(), prow(), prow()]
                 + [pl.BlockSpec((t.shape[0], width), lambda b, i, q: (0, q)) for t in lora_w]
                 + [const(ones_h), const(tri)]
                 + [lru_tile(0), lru_tile(1),
                    pl.BlockSpec((SUBLANE, dl),
                                 lambda b, i, q: (jnp.maximum((b * nc + i) * rows8 - 1, 0), 0)),
                    const(conv_w), lrow, pl.BlockSpec(wa.shape, lambda b, i, q: (0, 0, 0)),
                    pl.BlockSpec(wx.shape, lambda b, i, q: (0, 0, 0)), lrow, lrow, lrow]
                 + cast_specs,
        out_specs=[out_tile, out_tile, out_mat, out_mat, out_tile, out_tile, lru_tile(0)] + cast_specs,
        out_shape=[jax.ShapeDtypeStruct(act.shape, BF16), act, jax.ShapeDtypeStruct(mat.shape, BF16),
                   mat, act, act, jax.ShapeDtypeStruct((bsz * seq, dl), F32)]
                  + cast_shapes,
        scratch_shapes=[pltpu.VMEM((1, dl), F32)],
        compiler_params=_params("arbitrary", "arbitrary", "arbitrary"),
        name="rwkv_a",
    )(p, p, p, p_lora, p, p, p, p_lora, mu_rkv, mu_rkv, mu_rkv, mu_lora, w0, a0, k_k, k_a, r_k,
      *lora_w, ones_h, tri,
      p, p, p, conv_w, lvec(conv_b), wa, wx, lvec(ba), lvec(bx), lvec(lam), *cast_ws)


def _rwkv_b_body(rp_ref, yp_ref, m_ref, n_ref, bonus_ref, g_ref, lng_ref, lnb_ref, ones_ref,
                 o_ref, state_ref):
    @pl.when(pl.program_id(1) == 0)
    def _():
        state_ref[...] = jnp.zeros_like(state_ref)

    npairs = state_ref.shape[0]
    pairs = range(npairs)
    ps = [slice(q * PAIR, (q + 1) * PAIR) for q in pairs]
    left = lax.broadcasted_iota(jnp.int32, (HEAD, PAIR), 1) < HEAD
    ones_h = ones_ref[...]
    inv_n = 1.0 / HEAD
    state = [state_ref[q] for q in pairs]
    for j in range(rp_ref.shape[0] // CHUNK):
        rs = slice(j * CHUNK, (j + 1) * CHUNK)
        ks = slice(j * HEAD, (j + 1) * HEAD)
        g0 = [_pair_diag(state[q], left) for q in pairs]
        ys = [_dg(rp_ref[rs, ps[q]].astype(BF16), g0[q], _NN) + yp_ref[rs, ps[q]] for q in pairs]
        state = [_dg(m_ref[ks, ps[q]].astype(BF16), g0[q], _NN) + n_ref[ks, ps[q]] for q in pairs]
        y = jnp.concatenate(ys, axis=1)
        yc = y - _head_sums(y, ones_h) * inv_n
        var = _head_sums(yc * yc, ones_h) * inv_n
        yn = yc * lax.rsqrt(var + GN_EPS) * lng_ref[...] + lnb_ref[...]
        o_ref[rs, :] = (yn + bonus_ref[rs, :]) * g_ref[rs, :]
    for q in pairs:
        state_ref[q] = state[q]


def _rwkv_b(rp, yp, mc, nm, bonus, g, ln_g, ln_b, bsz, seq):
    cl = RWKV_B_CHUNKS * CHUNK
    dr = rp.shape[1]
    nc = seq // cl
    tile = pl.BlockSpec((cl, dr), lambda b, i: (b * nc + i, 0))
    mat = pl.BlockSpec((RWKV_B_CHUNKS * HEAD, dr), lambda b, i: (b * nc + i, 0))
    prow = pl.BlockSpec((1, dr), lambda b, i: (0, 0))
    ones_h = _head_ones(ONES_WIDTH)
    return pl.pallas_call(
        _rwkv_b_body,
        grid=(bsz, nc),
        in_specs=[tile, tile, mat, mat, tile, tile, prow, prow,
                  pl.BlockSpec(ones_h.shape, lambda b, i: (0, 0))],
        out_specs=tile,
        out_shape=jax.ShapeDtypeStruct((bsz * seq, dr), F32),
        scratch_shapes=[pltpu.VMEM((dr // PAIR, HEAD, PAIR), F32)],
        compiler_params=_params("parallel", "arbitrary"),
        name="rwkv_b",
    )(rp, yp, mc, nm, bonus, g, ln_g, ln_b, ones_h)


def _mm_out_body(ya_ref, yb_ref, x_ref, gm_ref, w_ref, g_ref, sh_ref, sc_ref, o_ref, h_ref, *, sub):
    da = ya_ref.shape[1]
    for r0 in range(0, x_ref.shape[0], sub):
        rs = slice(r0, r0 + sub)
        mix = (jnp.dot(ya_ref[rs, :].astype(BF16), w_ref[:da, :], preferred_element_type=F32)
               + jnp.dot(yb_ref[rs, :].astype(BF16), w_ref[da:, :], preferred_element_type=F32))
        x1 = x_ref[rs, :] + gm_ref[0] * mix
        o_ref[rs, :] = x1
        h_ref[rs, :] = _norm_mod(x1, g_ref[...], sh_ref[0], sc_ref[0]).astype(BF16)


def _mm_out(ya, yb, x2, gm, w, g, sh, sc, seq, tm=512, sub=256):
    m, d = x2.shape
    per_b = seq // tm
    brow = pl.BlockSpec((1, 1, d), lambda i: (i // per_b, 0, 0))
    tile = pl.BlockSpec((tm, d), lambda i: (i, 0))
    return pl.pallas_call(
        functools.partial(_mm_out_body, sub=sub),
        grid=(m // tm,),
        in_specs=[pl.BlockSpec((tm, ya.shape[1]), lambda i: (i, 0)),
                  pl.BlockSpec((tm, yb.shape[1]), lambda i: (i, 0)),
                  tile, brow,
                  pl.BlockSpec(w.shape, lambda i: (0, 0)),
                  pl.BlockSpec((1, d), lambda i: (0, 0)), brow, brow],
        out_specs=[tile, tile],
        out_shape=[jax.ShapeDtypeStruct((m, d), F32), jax.ShapeDtypeStruct((m, d), BF16)],
        compiler_params=_params("parallel"),
        name="mm_out",
    )(ya, yb, x2, gm, w, g, sh, sc)


def _ffn_body(x_ref, h_ref, gf_ref, wg_ref, wu_ref, wd_ref, fg_ref, o_ref, acc_ref):
    f = pl.program_id(1)

    @pl.when(f == 0)
    def _():
        acc_ref[...] = jnp.zeros_like(acc_ref)

    h = h_ref[...]
    gate = jnp.dot(h, wg_ref[...], preferred_element_type=F32)
    up = jnp.dot(h, wu_ref[...], preferred_element_type=F32)
    act = (gate * jax.nn.sigmoid(gate) * up).astype(BF16)
    acc_ref[...] += jnp.dot(act, wd_ref[...], preferred_element_type=F32)

    @pl.when(f == pl.num_programs(1) - 1)
    def _():
        y = x_ref[...] + gf_ref[0] * acc_ref[...]
        o_ref[...] = (y * lax.rsqrt(jnp.mean(y * y, axis=-1, keepdims=True) + RMS_EPS)
                      * fg_ref[...])


def _ffn(x1, h2, gf, w_gu, w_down, fg, seq, tm=512, tf=512):
    m, d = x1.shape
    dff = w_down.shape[0]
    nf = dff // tf
    assert seq % tm == 0 and dff % tf == 0, "row tiles must not straddle sequences"
    per_b = seq // tm
    tile = pl.BlockSpec((tm, d), lambda i, f: (i, 0))
    prow = pl.BlockSpec((1, d), lambda i, f: (0, 0))
    return pl.pallas_call(
        _ffn_body,
        grid=(m // tm, nf),
        in_specs=[tile, tile,
                  pl.BlockSpec((1, 1, d), lambda i, f: (i // per_b, 0, 0)),
                  pl.BlockSpec((d, tf), lambda i, f: (0, f)),
                  pl.BlockSpec((d, tf), lambda i, f: (0, nf + f)),
                  pl.BlockSpec((tf, d), lambda i, f: (f, 0)),
                  prow],
        out_specs=tile,
        out_shape=jax.ShapeDtypeStruct((m, d), F32),
        scratch_shapes=[pltpu.VMEM((tm, d), F32)],
        compiler_params=_params("parallel", "arbitrary"),
        name="ffn",
    )(x1, h2, gf, w_gu, w_gu, w_down, fg)


def _pad_cols(w, n):
    return jnp.pad(w, ((0, 0), (0, n - w.shape[1])))


def _pad_rows(w, n):
    return jnp.pad(w, ((0, n - w.shape[0]), (0, 0)))


def kernel(x, c, w_ada, b_ada, norm_mix_g, w_in, conv_w, conv_b, lru_wa, lru_ba, lru_wx, lru_bx, lru_lambda, rwkv_mu, rwkv_w0, rwkv_w2, rwkv_a0, rwkv_a2, rwkv_g2, rwkv_k_k, rwkv_k_a, rwkv_r_k, rwkv_ln_g, rwkv_ln_b, w_out, norm_ffn_g, w_gu, w_down, final_norm_g):
    bsz, seq, d = x.shape
    depth = w_ada.shape[0]
    dl = conv_w.shape[2]
    dr = rwkv_w0.shape[1]
    w_lora, a_lora, g_lora = rwkv_w2.shape[1], rwkv_a2.shape[1], rwkv_g2.shape[1]
    wpad, apad = LANE, LANE
    gpad = -(-g_lora // LANE) * LANE
    rkv_col0 = 2 * dl
    lora0 = rkv_col0 + 3 * dr

    x2 = x.reshape(bsz * seq, d)
    for l in range(depth):
        mod = _mod(c, w_ada[l], b_ada[l].reshape(1, -1))
        sh_m, sc_m, g_m, sh_f, sc_f, g_f = [t.reshape(bsz, 1, d) for t in jnp.split(mod, 6, axis=-1)]

        wi = jnp.swapaxes(w_in[l], 0, 1)
        o1, o2 = lora0 + w_lora, lora0 + w_lora + a_lora
        w_lora_p = jnp.concatenate(
            [_pad_rows(wi[lora0:o1], wpad), _pad_rows(wi[o1:o2], apad),
             _pad_rows(wi[o2:], gpad)], axis=0)
        mu = rwkv_mu[l].reshape(1, -1)
        mu_rkv = mu[:, :3 * dr]
        mu_lora = jnp.concatenate(
            [_pad_cols(mu[:, 3 * dr:3 * dr + w_lora], wpad),
             _pad_cols(mu[:, 3 * dr + w_lora:3 * dr + w_lora + a_lora], apad),
             _pad_cols(mu[:, 3 * dr + w_lora + a_lora:], gpad)], axis=1)
        w2p = _pad_rows(rwkv_w2[l], wpad)
        a2p = _pad_rows(rwkv_a2[l], apad)
        g2p = _pad_rows(rwkv_g2[l], gpad)

        h = _norm(x2, norm_mix_g[l].reshape(1, d), sh_m, sc_m, seq)
        p = _mm_in(h, wi, lora0)
        p_lora = _mm_in(h, w_lora_p, w_lora_p.shape[0], tn=w_lora_p.shape[0], name="mm_lora")

        rowv = lambda t: t.reshape(1, dr)
        rp, yp, mc, nm, bonus, gg, y_a, w_out_b, w_gu_b, w_down_b = _rwkv_a(
            p, p_lora, mu_rkv, mu_lora, rowv(rwkv_w0[l]), rowv(rwkv_a0[l]), rowv(rwkv_k_k[l]),
            rowv(rwkv_k_a[l]), rowv(rwkv_r_k[l]), w2p, a2p, g2p,
            conv_w[l], conv_b[l], lru_wa[l].astype(BF16), lru_wx[l].astype(BF16),
            lru_ba[l], lru_bx[l], lru_lambda[l], bsz, seq, rkv_col0,
            cast_ws=(w_out[l], w_gu[l], w_down[l]))
        y_b = _rwkv_b(rp, yp, mc, nm, bonus, gg, rowv(rwkv_ln_g[l]), rowv(rwkv_ln_b[l]), bsz, seq)

        x2, h2 = _mm_out(y_a, y_b, x2, g_m, w_out_b, norm_ffn_g[l].reshape(1, d), sh_f, sc_f, seq)

        last = l == depth - 1
        fg = final_norm_g.reshape(1, d) if last else None
        assert last, "only the final layer carries the closing RMSNorm"
        x2 = _ffn(x2, h2, g_f, w_gu_b, w_down_b, fg, seq)
    return x2.reshape(bsz, seq, d)
```

```python
import functools
import math

import jax
import jax.numpy as jnp
from jax import lax
from jax.experimental import pallas as pl
from jax.experimental.pallas import tpu as pltpu

F32 = jnp.float32
BF16 = jnp.bfloat16

LRU_HEADS = 4
CONV_WIDTH = 4
LRU_C = 8.0
HEAD = 64
CHUNK = 64
PAIR = 2 * HEAD
HEADS_PER_STEP = 16
ONES_WIDTH = 256
CHUNKS_PER_STEP = 2
RWKV_B_CHUNKS = 4
RMS_EPS = 1e-6
GN_EPS = 64e-5
L2_EPS = 1e-12
DECAY_SCALE = -math.exp(-0.5)
LANE = 128
SUBLANE = 8
BF16_SUBLANE = 16
VMEM_LIMIT = 56 * 1024 * 1024


def _params(*sem):
    return pltpu.CompilerParams(dimension_semantics=sem, vmem_limit_bytes=VMEM_LIMIT)


_NN = (((1,), (0,)), ((), ()))
_NT = (((1,), (1,)), ((), ()))
_TN = (((0,), (0,)), ((), ()))


def _dg(a, b, dims):
    return lax.dot_general(a, b, dims, preferred_element_type=F32)


def _split(x):
    hi = x.astype(BF16)
    lo = (x - hi.astype(F32)).astype(BF16)
    return hi, lo


def _mm3(a, b, dims=_NN):
    ah, al = _split(a)
    bh, bl = _split(b)
    return _dg(ah, bh, dims) + (_dg(ah, bl, dims) + _dg(al, bh, dims))


def _mm3_presplit(a, bh, bl):
    ah, al = _split(a)
    return _dg(ah, bh, _NN) + (_dg(ah, bl, _NN) + _dg(al, bh, _NN))


def _mm2_exact_rhs(a, b_bf16):
    ah, al = _split(a)
    return _dg(ah, b_bf16, _NN) + _dg(al, b_bf16, _NN)


def _head_sums(x, ones_h):
    n = ones_h.shape[0]
    return jnp.concatenate([_mm2_exact_rhs(x[:, c:c + n], ones_h) for c in range(0, x.shape[1], n)],
                           axis=1)


def _mm2_exact_lhs(a_bf16, b):
    bh, bl = _split(b)
    return _dg(a_bf16, bh, _NN) + _dg(a_bf16, bl, _NN)


def _softplus(x):
    return jnp.maximum(x, 0.0) + jnp.log1p(jnp.exp(-jnp.abs(x)))


def _iota2(shape):
    return (lax.broadcasted_iota(jnp.int32, shape, 0),
            lax.broadcasted_iota(jnp.int32, shape, 1))


def _head_ones(n):
    r, c = _iota2((n, n))
    return jnp.where((r // HEAD) == (c // HEAD), 1.0, 0.0).astype(BF16)


def _mod_body(c_ref, w_ref, b_ref, o_ref):
    c = c_ref[...]
    ca = c * jax.nn.sigmoid(c)
    o_ref[...] = _mm3(ca, w_ref[...]) + b_ref[...]


def _mod(c, w, b, tn=1024):
    bsz, d = c.shape
    n = w.shape[1]
    return pl.pallas_call(
        _mod_body,
        grid=(n // tn,),
        in_specs=[pl.BlockSpec((bsz, d), lambda j: (0, 0)),
                  pl.BlockSpec((d, tn), lambda j: (0, j)),
                  pl.BlockSpec((1, tn), lambda j: (0, j))],
        out_specs=pl.BlockSpec((bsz, tn), lambda j: (0, j)),
        out_shape=jax.ShapeDtypeStruct((bsz, n), F32),
        compiler_params=_params("parallel"),
        name="mod",
    )(c, w, b)


def _norm_mod(x, g, sh, sc):
    y = x * lax.rsqrt(jnp.mean(x * x, axis=-1, keepdims=True) + RMS_EPS) * g
    return y * (1.0 + sc) + sh


def _norm_body(x_ref, g_ref, sh_ref, sc_ref, o_ref):
    o_ref[...] = _norm_mod(x_ref[...], g_ref[...], sh_ref[0], sc_ref[0]).astype(BF16)


def _norm(x2, g, sh, sc, seq, tm=512):
    m, d = x2.shape
    per_b = seq // tm
    return pl.pallas_call(
        _norm_body,
        grid=(m // tm,),
        in_specs=[pl.BlockSpec((tm, d), lambda i: (i, 0)),
                  pl.BlockSpec((1, d), lambda i: (0, 0)),
                  pl.BlockSpec((1, 1, d), lambda i: (i // per_b, 0, 0)),
                  pl.BlockSpec((1, 1, d), lambda i: (i // per_b, 0, 0))],
        out_specs=pl.BlockSpec((tm, d), lambda i: (i, 0)),
        out_shape=jax.ShapeDtypeStruct((m, d), BF16),
        compiler_params=_params("parallel"),
        name="norm_mix",
    )(x2, g, sh, sc)


def _mm_in_body(h_ref, w_ref, o_ref, wb_ref):
    @pl.when(pl.program_id(1) == 0)
    def _():
        wb_ref[...] = w_ref[...].astype(BF16)

    o_ref[...] = _dg(h_ref[...], wb_ref[...], _NT)


def _mm_in(h, wt, ncols, tm=1024, tn=1024, name="mm_in"):
    m, d = h.shape
    return pl.pallas_call(
        _mm_in_body,
        grid=(ncols // tn, m // tm),
        in_specs=[pl.BlockSpec((tm, d), lambda j, i: (i, 0)),
                  pl.BlockSpec((tn, d), lambda j, i: (j, 0))],
        out_specs=pl.BlockSpec((tm, tn), lambda j, i: (i, j)),
        out_shape=jax.ShapeDtypeStruct((m, ncols), F32),
        scratch_shapes=[pltpu.VMEM((tn, d), BF16)],
        compiler_params=_params("parallel", "arbitrary"),
        name=name,
    )(h, wt)


def _lru_head(h, first, u_ref, gate_ref, halo_ref, cw_ref, cb_ref, wa_ref, wx_ref, ba_ref, bx_ref,
              lam_ref, o_ref, carry_ref):
    tt = u_ref.shape[0]
    hd = u_ref.shape[1] // LRU_HEADS
    cs = slice(h * hd, (h + 1) * hd)
    p = u_ref[:, cs]
    halo = jnp.where(first, 0.0, halo_ref[:, cs])
    ext = jnp.concatenate([halo, p], axis=0)
    cw = cw_ref[:, cs]
    u = cb_ref[:, cs] + p * cw[CONV_WIDTH - 1:CONV_WIDTH, :]
    for j in range(1, CONV_WIDTH):
        shifted = pltpu.roll(ext, j, 0)[SUBLANE:, :]
        u = u + shifted * cw[CONV_WIDTH - 1 - j:CONV_WIDTH - j, :]
    ub = u.astype(BF16)
    ra = jnp.dot(ub, wa_ref[h], preferred_element_type=F32)
    rx = jnp.dot(ub, wx_ref[h], preferred_element_type=F32)
    yield
    r = jax.nn.sigmoid(ra + ba_ref[:, cs])
    ig = jax.nn.sigmoid(rx + bx_ref[:, cs])
    a = jnp.exp(r * ((-LRU_C) * _softplus(-lam_ref[:, cs])))
    mult = jnp.sqrt(1.0 - a * a)
    row = lax.broadcasted_iota(jnp.int32, (tt, hd), 0)
    mult = jnp.where(jnp.logical_and(first, row == 0), 1.0, mult)
    b = mult * (ig * u)

    groups = tt // SUBLANE
    a3 = a.reshape(groups, SUBLANE, hd)
    b3 = b.reshape(groups, SUBLANE, hd)
    sub = lax.broadcasted_iota(jnp.int32, (groups, SUBLANE, hd), 1)
    s = 1
    while s < SUBLANE:
        keep = sub >= s
        a_s = jnp.where(keep, pltpu.roll(a3, s, 1), 1.0)
        b_s = jnp.where(keep, pltpu.roll(b3, s, 1), 0.0)
        b3 = a3 * b_s + b3
        a3 = a3 * a_s
        s *= 2
    yield
    gate = jax.nn.gelu(gate_ref[:, cs])
    carry = carry_ref[:, cs]
    for g in range(groups):
        hh = b3[g] + a3[g] * carry
        carry = hh[SUBLANE - 1:SUBLANE, :]
        o_ref[g * SUBLANE:(g + 1) * SUBLANE, cs] = hh * gate[g * SUBLANE:(g + 1) * SUBLANE, :]
    carry_ref[:, cs] = carry
    yield


def _token_shift(x, halo, mu, first, row):
    prev = jnp.where(first, 0.0, halo[SUBLANE - 1:SUBLANE, :])
    xs = jnp.where(row == 0, prev, pltpu.roll(x, 1, 0))
    return x + (xs - x) * mu


def _mm1(a, b, dims=_NN):
    return _dg(a.astype(BF16), b.astype(BF16), dims)


def _pair_diag(y, left):
    return jnp.concatenate([jnp.where(left, y, 0.0), jnp.where(left, 0.0, y)], axis=0).astype(BF16)


def _pair_mm(x, y, left):
    return _dg(x.astype(BF16), _pair_diag(y, left), _NN)


def _chunk_chain(ops, store):
    ab_, bb_, kb_, rb_, v_, bt_, kt_, pe_ = ops
    rc, lane = _iota2((CHUNK, PAIR))
    cc = lane % HEAD
    left = lane < HEAD
    strict = rc > cc
    incl = rc >= cc
    diag = rc == cc
    ar16 = [jnp.concatenate([x, y], axis=0).astype(BF16) for x, y in zip(ab_, rb_)]
    bd_b = [_pair_diag(x, left) for x in bb_]
    bd_k = [_pair_diag(x, left) for x in kb_]
    bd_v = [_pair_diag(x, left) for x in v_]
    arb = [_dg(x, y, _NT) for x, y in zip(ar16, bd_b)]
    ark = [_dg(x, y, _NT) for x, y in zip(ar16, bd_k)]
    a_ab = [jnp.where(strict, x[:CHUNK], 0.0) for x in arb]
    a_rb = [jnp.where(incl, x[CHUNK:], 0.0).astype(BF16) for x in arb]
    a_akrk = [jnp.concatenate([jnp.where(strict, x[:CHUNK], 0.0), jnp.where(incl, x[CHUNK:], 0.0)],
                              axis=0).astype(BF16) for x in ark]
    yield
    base = 8
    d = [jnp.where((rc // base) == (cc // base), a, 0.0) for a in a_ab]
    d2 = [_pair_mm(t, t, left) for t in d]
    akrkv = [_dg(x, y, _NN) for x, y in zip(a_akrk, bd_v)]
    akv = [x[:CHUNK] for x in akrkv]
    rkv = [x[CHUNK:] for x in akrkv]
    x = [jnp.where(diag, 1.0, 0.0) + t for t in d]
    yield
    x = [xi + _pair_mm(t2, xi, left) for xi, t2 in zip(x, d2)]
    d4 = [_pair_mm(t2, t2, left) for t2 in d2]
    yield
    x = [xi + _pair_mm(t4, xi, left) for xi, t4 in zip(x, d4)]
    yield
    size = base
    while size < CHUNK:
        off = jnp.logical_and((rc // (2 * size)) == (cc // (2 * size)),
                              (rc // size) != (cc // size))
        o = [jnp.where(off, a, 0.0) for a in a_ab]
        ox = [_pair_mm(oi, xi, left) for oi, xi in zip(o, x)]
        yield
        x = [xi + _pair_mm(xi, oxi, left) for xi, oxi in zip(x, ox)]
        yield
        size *= 2
    t = [xi.astype(BF16) for xi in x]
    wu = [_dg(ti, jnp.concatenate([_pair_diag(y, left), _pair_diag(z, left)], axis=1), _NN)
          for ti, y, z in zip(t, ab_, akv)]
    kv = [_dg(xi.astype(BF16), y.astype(BF16), _TN) for xi, y in zip(kt_, v_)]
    yield
    ry = [_dg(xi, jnp.concatenate([_pair_diag(y[:, :PAIR], left), _pair_diag(y[:, PAIR:], left)], axis=1), _NN)
          for xi, y in zip(a_rb, wu)]
    mn = [_dg(xi.astype(BF16), y.astype(BF16), _TN) for xi, y in zip(bt_, wu)]
    yield

    def head_blocks(z):
        return jnp.where(left, z[:HEAD, :], z[HEAD:, :])

    for u in range(len(ab_)):
        store(u,
              rb_[u] + ry[u][:, :PAIR],
              ry[u][:, PAIR:] + rkv[u],
              jnp.where(diag, pe_[u], 0.0) + head_blocks(mn[u][:, :PAIR]),
              head_blocks(mn[u][:, PAIR:]) + head_blocks(kv[u]))


def _rwkv_a_body(r_ref, k_ref, v_ref, l_ref, rh_ref, kh_ref, vh_ref, lh_ref,
                 mur_ref, muk_ref, muv_ref, mul_ref, w0_ref, a0_ref, kkw_ref, kaw_ref, rkw_ref,
                 w2h_ref, w2l_ref, a2h_ref, a2l_ref, g2h_ref, g2l_ref, ones_ref, tri_ref,
                 u_ref, gate_ref, halo_ref, cw_ref, cb_ref, wa_ref, wx_ref, ba_ref, bx_ref, lam_ref,
                 *rest):
    ncast = (len(rest) - 8) // 2
    cast_in = rest[:ncast]
    rp_ref, yp_ref, m_ref, n_ref, bonus_ref, g_ref, ya_ref = rest[ncast:ncast + 7]
    cast_out = rest[ncast + 7:-1]
    carry_ref = rest[-1]
    first = pl.program_id(1) == 0

    @pl.when(first)
    def _():
        carry_ref[...] = jnp.zeros_like(carry_ref)
    cl = CHUNK
    rows = CHUNKS_PER_STEP * cl
    width = HEADS_PER_STEP * HEAD
    gw = ones_ref.shape[0]
    row_g = lax.broadcasted_iota(jnp.int32, (rows, gw), 0)
    row_l = lax.broadcasted_iota(jnp.int32, (rows, l_ref.shape[1]), 0)
    ones_h = ones_ref[...]

    lo = _token_shift(l_ref[...], lh_ref[...], mul_ref[...], first, row_l)
    act_w = _split(jnp.tanh(lo[:, 0:LANE]))
    act_a = _split(lo[:, LANE:2 * LANE])
    act_g = _split(jax.nn.sigmoid(lo[:, 2 * LANE:]))

    def lora(act, wh_ref, wl_ref, cs):
        (ah, al_), bh, bl = act, wh_ref[:, cs], wl_ref[:, cs]
        return _dg(ah, bh, _NN) + (_dg(ah, bl, _NN) + _dg(al_, bh, _NN))

    def prologue(c0, out):
        cs = slice(c0, c0 + gw)
        r = _token_shift(r_ref[:, cs], rh_ref[:, cs], mur_ref[:, cs], first, row_g)
        k = _token_shift(k_ref[:, cs], kh_ref[:, cs], muk_ref[:, cs], first, row_g)
        v = _token_shift(v_ref[:, cs], vh_ref[:, cs], muv_ref[:, cs], first, row_g)
        w_lin = w0_ref[:, cs] + lora(act_w, w2h_ref, w2l_ref, cs)
        a_lin = a0_ref[:, cs] + lora(act_a, a2h_ref, a2l_ref, cs)
        g_ref[:, cs] = lora(act_g, g2h_ref, g2l_ref, cs)
        kk = k * kkw_ref[:, cs]
        kk_ss = _mm2_exact_rhs(kk * kk, ones_h)
        yield
        lw = DECAY_SCALE * jax.nn.sigmoid(w_lin)
        a = jax.nn.sigmoid(a_lin)
        kk = kk * lax.rsqrt(jnp.maximum(kk_ss, L2_EPS * L2_EPS))
        kp = k * (1.0 + (a - 1.0) * kaw_ref[:, cs])
        bonus_ref[:, cs] = _mm2_exact_rhs(r * kp * rkw_ref[:, cs], ones_h) * v
        lc = _mm2_exact_lhs(tri_ref[...], lw)
        yield
        p_incl = jnp.exp(lc)
        p_excl = jnp.exp(lc - lw)
        p_inv = 1.0 / p_incl
        p_end = jnp.concatenate(
            [jnp.broadcast_to(p_incl[(j + 1) * cl - 1:(j + 1) * cl, :], (cl, gw))
             for j in range(CHUNKS_PER_STEP)], axis=0)
        abar = -(kk * p_excl)
        bbar = kk * a * p_inv
        kbar = kp * p_inv
        rbar = r * p_incl
        btil = bbar * p_end
        ktil = kbar * p_end
        units = [(j, q) for j in range(CHUNKS_PER_STEP) for q in range(gw // PAIR)]
        out.extend([x[j * cl:(j + 1) * cl, q * PAIR:(q + 1) * PAIR] for j, q in units]
                   for x in (abar, bbar, kbar, rbar, v, btil, ktil, p_end))
        yield

    def make_store(c0):
        units = [(j, q) for j in range(CHUNKS_PER_STEP) for q in range(gw // PAIR)]

        def store(u, rp, yp, mm, nn):
            j, q = units[u]
            rs = slice(j * cl, (j + 1) * cl)
            qs = slice(c0 + q * PAIR, c0 + (q + 1) * PAIR)
            rp_ref[rs, qs] = rp.astype(BF16)
            yp_ref[rs, qs] = yp
            m_ref[rs, qs] = mm.astype(BF16)
            n_ref[rs, qs] = nn
        return store

    chains = []
    for c0 in range(0, width, gw):
        ops = []
        for _ in prologue(c0, ops):
            for ch in chains:
                next(ch, None)
        chains.append(_chunk_chain(ops, make_store(c0)))
    def lru_heads(hs):
        gens = [_lru_head(h, first, u_ref, gate_ref, halo_ref, cw_ref, cb_ref, wa_ref, wx_ref,
                          ba_ref, bx_ref, lam_ref, ya_ref, carry_ref) for h in hs]
        while gens:
            gens = [g for g in gens if next(g, StopIteration) is not StopIteration]
            yield

    lru = (None for h in range(LRU_HEADS) for _ in lru_heads([h]))
    live = list(chains)
    while live:
        live = [ch for ch in live if next(ch, StopIteration) is not StopIteration]
        next(lru, None)
    for _ in lru:
        pass

    for src, dst in zip(cast_in, cast_out):
        dst[...] = src[...].astype(BF16)


def _rwkv_a(p, p_lora, mu_rkv, mu_lora, w0, a0, k_k, k_a, r_k, w2p, a2p, g2p,
            conv_w, conv_b, wa, wx, ba, bx, lam, bsz, seq, rkv_col0, cast_ws=()):
    dl = conv_w.shape[1]
    lvec = lambda t: t.reshape(1, dl)
    lrow = pl.BlockSpec((1, dl), lambda b, i, q: (0, 0))

    def lru_tile(cb):
        return pl.BlockSpec((CHUNKS_PER_STEP * CHUNK, dl), lambda b, i, q: (b * nc + i, cb))
    cl = CHUNKS_PER_STEP * CHUNK
    width = HEADS_PER_STEP * HEAD
    dr = w0.shape[1]
    ngroups = dr // width
    assert ngroups == 1, "the LRU ride-along expects one grid step per row tile"
    nc = seq // cl
    lw_ = mu_lora.shape[1]
    cb0 = rkv_col0 // width
    rows8 = cl // SUBLANE
    rt, ct = _iota2((cl, cl))
    tri = jnp.where(jnp.logical_and(rt >= ct, (rt // CHUNK) == (ct // CHUNK)), 1.0, 0.0).astype(BF16)
    ones_h = _head_ones(ONES_WIDTH)
    const = lambda arr: pl.BlockSpec(arr.shape, lambda b, i, q: (0, 0))
    lora_w = [t for wgt in (w2p, a2p, g2p) for t in _split(wgt)]

    def tile(cb_off):
        return pl.BlockSpec((cl, width), lambda b, i, q: (b * nc + i, cb0 + cb_off + q))

    def halo(cb_off):
        return pl.BlockSpec(
            (SUBLANE, width),
            lambda b, i, q: (jnp.maximum((b * nc + i) * rows8 - 1, 0), cb0 + cb_off + q))

    def prow(off=0):
        return pl.BlockSpec((1, width), lambda b, i, q: (0, off + q))

    out_tile = pl.BlockSpec((cl, width), lambda b, i, q: (b * nc + i, q))
    out_mat = pl.BlockSpec((CHUNKS_PER_STEP * HEAD, width), lambda b, i, q: (b * nc + i, q))
    act = jax.ShapeDtypeStruct((bsz * seq, dr), F32)
    mat = jax.ShapeDtypeStruct((bsz * (seq // CHUNK) * HEAD, dr), F32)

    nsteps = bsz * nc * ngroups
    cast_specs = []
    for wgt in cast_ws:
        hold = 1
        while (wgt.shape[0] * hold) % (nsteps * BF16_SUBLANE) != 0:
            hold *= 2
        blk = (wgt.shape[0] * hold // nsteps, wgt.shape[1])
        cast_specs.append(pl.BlockSpec(
            blk, lambda b, i, q, hold=hold: (((b * nc + i) * ngroups + q) // hold, 0)))
    cast_shapes = [jax.ShapeDtypeStruct(wgt.shape, BF16) for wgt in cast_ws]

    return pl.pallas_call(
        _rwkv_a_body,
        grid=(bsz, nc, ngroups),
        in_specs=[tile(0), tile(ngroups), tile(2 * ngroups),
                  pl.BlockSpec((cl, lw_), lambda b, i, q: (b * nc + i, 0)),
                  halo(0), halo(ngroups), halo(2 * ngroups),
                  pl.BlockSpec((SUBLANE, lw_),
                               lambda b, i, q: (jnp.maximum((b * nc + i) * rows8 - 1, 0), 0)),
                  prow(0), prow(ngroups), prow(2 * ngroups),
                  pl.BlockSpec((1, lw_), lambda b, i, q: (0, 0)),
                  prow(), prow(), prow(), prow(), prow()]
                 + [pl.BlockSpec((t.shape[0], width), lambda b, i, q: (0, q)) for t in lora_w]
                 + [const(ones_h), const(tri)]
                 + [lru_tile(0), lru_tile(1),
                    pl.BlockSpec((SUBLANE, dl),
                                 lambda b, i, q: (jnp.maximum((b * nc + i) * rows8 - 1, 0), 0)),
                    const(conv_w), lrow, pl.BlockSpec(wa.shape, lambda b, i, q: (0, 0, 0)),
                    pl.BlockSpec(wx.shape, lambda b, i, q: (0, 0, 0)), lrow, lrow, lrow]
                 + cast_specs,
        out_specs=[out_tile, out_tile, out_mat, out_mat, out_tile, out_tile, lru_tile(0)] + cast_specs,
        out_shape=[jax.ShapeDtypeStruct(act.shape, BF16), act, jax.ShapeDtypeStruct(mat.shape, BF16),
                   mat, act, act, jax.ShapeDtypeStruct((bsz * seq, dl), F32)]
                  + cast_shapes,
        scratch_shapes=[pltpu.VMEM((1, dl), F32)],
        compiler_params=_params("arbitrary", "arbitrary", "arbitrary"),
        name="rwkv_a",
    )(p, p, p, p_lora, p, p, p, p_lora, mu_rkv, mu_rkv, mu_rkv, mu_lora, w0, a0, k_k, k_a, r_k,
      *lora_w, ones_h, tri,
      p, p, p, conv_w, lvec(conv_b), wa, wx, lvec(ba), lvec(bx), lvec(lam), *cast_ws)


def _rwkv_b_body(rp_ref, yp_ref, m_ref, n_ref, bonus_ref, g_ref, lng_ref, lnb_ref, ones_ref,
                 o_ref, state_ref):
    @pl.when(pl.program_id(1) == 0)
    def _():
        state_ref[...] = jnp.zeros_like(state_ref)

    npairs = state_ref.shape[0]
    pairs = range(npairs)
    ps = [slice(q * PAIR, (q + 1) * PAIR) for q in pairs]
    left = lax.broadcasted_iota(jnp.int32, (HEAD, PAIR), 1) < HEAD
    ones_h = ones_ref[...]
    inv_n = 1.0 / HEAD
    state = [state_ref[q] for q in pairs]
    for j in range(rp_ref.shape[0] // CHUNK):
        rs = slice(j * CHUNK, (j + 1) * CHUNK)
        ks = slice(j * HEAD, (j + 1) * HEAD)
        g0 = [_pair_diag(state[q], left) for q in pairs]
        ys = [_dg(rp_ref[rs, ps[q]].astype(BF16), g0[q], _NN) + yp_ref[rs, ps[q]] for q in pairs]
        state = [_dg(m_ref[ks, ps[q]].astype(BF16), g0[q], _NN) + n_ref[ks, ps[q]] for q in pairs]
        y = jnp.concatenate(ys, axis=1)
        yc = y - _head_sums(y, ones_h) * inv_n
        var = _head_sums(yc * yc, ones_h) * inv_n
        yn = yc * lax.rsqrt(var + GN_EPS) * lng_ref[...] + lnb_ref[...]
        o_ref[rs, :] = (yn + bonus_ref[rs, :]) * g_ref[rs, :]
    for q in pairs:
        state_ref[q] = state[q]


def _rwkv_b(rp, yp, mc, nm, bonus, g, ln_g, ln_b, bsz, seq):
    cl = RWKV_B_CHUNKS * CHUNK
    dr = rp.shape[1]
    nc = seq // cl
    tile = pl.BlockSpec((cl, dr), lambda b, i: (b * nc + i, 0))
    mat = pl.BlockSpec((RWKV_B_CHUNKS * HEAD, dr), lambda b, i: (b * nc + i, 0))
    prow = pl.BlockSpec((1, dr), lambda b, i: (0, 0))
    ones_h = _head_ones(ONES_WIDTH)
    return pl.pallas_call(
        _rwkv_b_body,
        grid=(bsz, nc),
        in_specs=[tile, tile, mat, mat, tile, tile, prow, prow,
                  pl.BlockSpec(ones_h.shape, lambda b, i: (0, 0))],
        out_specs=tile,
        out_shape=jax.ShapeDtypeStruct((bsz * seq, dr), F32),
        scratch_shapes=[pltpu.VMEM((dr // PAIR, HEAD, PAIR), F32)],
        compiler_params=_params("parallel", "arbitrary"),
        name="rwkv_b",
    )(rp, yp, mc, nm, bonus, g, ln_g, ln_b, ones_h)


def _mm_out_body(ya_ref, yb_ref, x_ref, gm_ref, w_ref, g_ref, sh_ref, sc_ref, o_ref, h_ref, *, sub):
    da = ya_ref.shape[1]
    for r0 in range(0, x_ref.shape[0], sub):
        rs = slice(r0, r0 + sub)
        mix = (jnp.dot(ya_ref[rs, :].astype(BF16), w_ref[:da, :], preferred_element_type=F32)
               + jnp.dot(yb_ref[rs, :].astype(BF16), w_ref[da:, :], preferred_element_type=F32))
        x1 = x_ref[rs, :] + gm_ref[0] * mix
        o_ref[rs, :] = x1
        h_ref[rs, :] = _norm_mod(x1, g_ref[...], sh_ref[0], sc_ref[0]).astype(BF16)


def _mm_out(ya, yb, x2, gm, w, g, sh, sc, seq, tm=512, sub=256):
    m, d = x2.shape
    per_b = seq // tm
    brow = pl.BlockSpec((1, 1, d), lambda i: (i // per_b, 0, 0))
    tile = pl.BlockSpec((tm, d), lambda i: (i, 0))
    return pl.pallas_call(
        functools.partial(_mm_out_body, sub=sub),
        grid=(m // tm,),
        in_specs=[pl.BlockSpec((tm, ya.shape[1]), lambda i: (i, 0)),
                  pl.BlockSpec((tm, yb.shape[1]), lambda i: (i, 0)),
                  tile, brow,
                  pl.BlockSpec(w.shape, lambda i: (0, 0)),
                  pl.BlockSpec((1, d), lambda i: (0, 0)), brow, brow],
        out_specs=[tile, tile],
        out_shape=[jax.ShapeDtypeStruct((m, d), F32), jax.ShapeDtypeStruct((m, d), BF16)],
        compiler_params=_params("parallel"),
        name="mm_out",
    )(ya, yb, x2, gm, w, g, sh, sc)


def _ffn_body(x_ref, h_ref, gf_ref, wg_ref, wu_ref, wd_ref, fg_ref, o_ref, acc_ref):
    f = pl.program_id(1)

    @pl.when(f == 0)
    def _():
        acc_ref[...] = jnp.zeros_like(acc_ref)

    h = h_ref[...]
    gate = jnp.dot(h, wg_ref[...], preferred_element_type=F32)
    up = jnp.dot(h, wu_ref[...], preferred_element_type=F32)
    act = (gate * jax.nn.sigmoid(gate) * up).astype(BF16)
    acc_ref[...] += jnp.dot(act, wd_ref[...], preferred_element_type=F32)

    @pl.when(f == pl.num_programs(1) - 1)
    def _():
        y = x_ref[...] + gf_ref[0] * acc_ref[...]
        o_ref[...] = (y * lax.rsqrt(jnp.mean(y * y, axis=-1, keepdims=True) + RMS_EPS)
                      * fg_ref[...])


def _ffn(x1, h2, gf, w_gu, w_down, fg, seq, tm=512, tf=512):
    m, d = x1.shape
    dff = w_down.shape[0]
    nf = dff // tf
    assert seq % tm == 0 and dff % tf == 0, "row tiles must not straddle sequences"
    per_b = seq // tm
    tile = pl.BlockSpec((tm, d), lambda i, f: (i, 0))
    prow = pl.BlockSpec((1, d), lambda i, f: (0, 0))
    return pl.pallas_call(
        _ffn_body,
        grid=(m // tm, nf),
        in_specs=[tile, tile,
                  pl.BlockSpec((1, 1, d), lambda i, f: (i // per_b, 0, 0)),
                  pl.BlockSpec((d, tf), lambda i, f: (0, f)),
                  pl.BlockSpec((d, tf), lambda i, f: (0, nf + f)),
                  pl.BlockSpec((tf, d), lambda i, f: (f, 0)),
                  prow],
        out_specs=tile,
        out_shape=jax.ShapeDtypeStruct((m, d), F32),
        scratch_shapes=[pltpu.VMEM((tm, d), F32)],
        compiler_params=_params("parallel", "arbitrary"),
        name="ffn",
    )(x1, h2, gf, w_gu, w_gu, w_down, fg)


def _pad_cols(w, n):
    return jnp.pad(w, ((0, 0), (0, n - w.shape[1])))


def _pad_rows(w, n):
    return jnp.pad(w, ((0, n - w.shape[0]), (0, 0)))


def kernel(x, c, w_ada, b_ada, norm_mix_g, w_in, conv_w, conv_b, lru_wa, lru_ba, lru_wx, lru_bx, lru_lambda, rwkv_mu, rwkv_w0, rwkv_w2, rwkv_a0, rwkv_a2, rwkv_g2, rwkv_k_k, rwkv_k_a, rwkv_r_k, rwkv_ln_g, rwkv_ln_b, w_out, norm_ffn_g, w_gu, w_down, final_norm_g):
    bsz, seq, d = x.shape
    depth = w_ada.shape[0]
    dl = conv_w.shape[2]
    dr = rwkv_w0.shape[1]
    w_lora, a_lora, g_lora = rwkv_w2.shape[1], rwkv_a2.shape[1], rwkv_g2.shape[1]
    wpad, apad = LANE, LANE
    gpad = -(-g_lora // LANE) * LANE
    rkv_col0 = 2 * dl
    lora0 = rkv_col0 + 3 * dr

    x2 = x.reshape(bsz * seq, d)
    for l in range(depth):
        mod = _mod(c, w_ada[l], b_ada[l].reshape(1, -1))
        sh_m, sc_m, g_m, sh_f, sc_f, g_f = [t.reshape(bsz, 1, d) for t in jnp.split(mod, 6, axis=-1)]

        wi = jnp.swapaxes(w_in[l], 0, 1)
        o1, o2 = lora0 + w_lora, lora0 + w_lora + a_lora
        w_lora_p = jnp.concatenate(
            [_pad_rows(wi[lora0:o1], wpad), _pad_rows(wi[o1:o2], apad),
             _pad_rows(wi[o2:], gpad)], axis=0)
        mu = rwkv_mu[l].reshape(1, -1)
        mu_rkv = mu[:, :3 * dr]
        mu_lora = jnp.concatenate(
            [_pad_cols(mu[:, 3 * dr:3 * dr + w_lora], wpad),
             _pad_cols(mu[:, 3 * dr + w_lora:3 * dr + w_lora + a_lora], apad),
             _pad_cols(mu[:, 3 * dr + w_lora + a_lora:], gpad)], axis=1)
        w2p = _pad_rows(rwkv_w2[l], wpad)
        a2p = _pad_rows(rwkv_a2[l], apad)
        g2p = _pad_rows(rwkv_g2[l], gpad)

        h = _norm(x2, norm_mix_g[l].reshape(1, d), sh_m, sc_m, seq)
        p = _mm_in(h, wi, lora0)
        p_lora = _mm_in(h, w_lora_p, w_lora_p.shape[0], tn=w_lora_p.shape[0], name="mm_lora")

        rowv = lambda t: t.reshape(1, dr)
        rp, yp, mc, nm, bonus, gg, y_a, w_out_b, w_gu_b, w_down_b = _rwkv_a(
            p, p_lora, mu_rkv, mu_lora, rowv(rwkv_w0[l]), rowv(rwkv_a0[l]), rowv(rwkv_k_k[l]),
            rowv(rwkv_k_a[l]), rowv(rwkv_r_k[l]), w2p, a2p, g2p,
            conv_w[l], conv_b[l], lru_wa[l].astype(BF16), lru_wx[l].astype(BF16),
            lru_ba[l], lru_bx[l], lru_lambda[l], bsz, seq, rkv_col0,
            cast_ws=(w_out[l], w_gu[l], w_down[l]))
        y_b = _rwkv_b(rp, yp, mc, nm, bonus, gg, rowv(rwkv_ln_g[l]), rowv(rwkv_ln_b[l]), bsz, seq)

        x2, h2 = _mm_out(y_a, y_b, x2, g_m, w_out_b, norm_ffn_g[l].reshape(1, d), sh_f, sc_f, seq)

        last = l == depth - 1
        fg = final_norm_g.reshape(1, d) if last else None
        assert last, "only the final layer carries the closing RMSNorm"
        x2 = _ffn(x2, h2, g_f, w_gu_b, w_down_b, fg, seq)
    return x2.reshape(bsz, seq, d)
```

```python
import functools
import math

import jax
import jax.numpy as jnp
from jax import lax
from jax.experimental import pallas as pl
from jax.experimental.pallas import tpu as pltpu

F32 = jnp.float32
BF16 = jnp.bfloat16

LRU_HEADS = 4
CONV_WIDTH = 4
LRU_C = 8.0
HEAD = 64
CHUNK = 64
PAIR = 2 * HEAD
HEADS_PER_STEP = 16
ONES_WIDTH = 256
CHUNKS_PER_STEP = 2
RWKV_B_CHUNKS = 4
RMS_EPS = 1e-6
GN_EPS = 64e-5
L2_EPS = 1e-12
DECAY_SCALE = -math.exp(-0.5)
LANE = 128
SUBLANE = 8
BF16_SUBLANE = 16
VMEM_LIMIT = 56 * 1024 * 1024


def _params(*sem):
    return pltpu.CompilerParams(dimension_semantics=sem, vmem_limit_bytes=VMEM_LIMIT)


_NN = (((1,), (0,)), ((), ()))
_NT = (((1,), (1,)), ((), ()))
_TN = (((0,), (0,)), ((), ()))


def _dg(a, b, dims):
    return lax.dot_general(a, b, dims, preferred_element_type=F32)


def _split(x):
    hi = x.astype(BF16)
    lo = (x - hi.astype(F32)).astype(BF16)
    return hi, lo


def _mm3(a, b, dims=_NN):
    ah, al = _split(a)
    bh, bl = _split(b)
    return _dg(ah, bh, dims) + (_dg(ah, bl, dims) + _dg(al, bh, dims))


def _mm3_presplit(a, bh, bl):
    ah, al = _split(a)
    return _dg(ah, bh, _NN) + (_dg(ah, bl, _NN) + _dg(al, bh, _NN))


def _mm2_exact_rhs(a, b_bf16):
    ah, al = _split(a)
    return _dg(ah, b_bf16, _NN) + _dg(al, b_bf16, _NN)


def _head_sums(x, ones_h):
    n = ones_h.shape[0]
    return jnp.concatenate([_mm2_exact_rhs(x[:, c:c + n], ones_h) for c in range(0, x.shape[1], n)],
                           axis=1)


def _mm2_exact_lhs(a_bf16, b):
    bh, bl = _split(b)
    return _dg(a_bf16, bh, _NN) + _dg(a_bf16, bl, _NN)


def _softplus(x):
    return jnp.maximum(x, 0.0) + jnp.log1p(jnp.exp(-jnp.abs(x)))


def _iota2(shape):
    return (lax.broadcasted_iota(jnp.int32, shape, 0),
            lax.broadcasted_iota(jnp.int32, shape, 1))


def _head_ones(n):
    r, c = _iota2((n, n))
    return jnp.where((r // HEAD) == (c // HEAD), 1.0, 0.0).astype(BF16)


def _mod_body(c_ref, w_ref, b_ref, o_ref):
    c = c_ref[...]
    ca = c * jax.nn.sigmoid(c)
    o_ref[...] = _mm3(ca, w_ref[...]) + b_ref[...]


def _mod(c, w, b, tn=1024):
    bsz, d = c.shape
    n = w.shape[1]
    return pl.pallas_call(
        _mod_body,
        grid=(n // tn,),
        in_specs=[pl.BlockSpec((bsz, d), lambda j: (0, 0)),
                  pl.BlockSpec((d, tn), lambda j: (0, j)),
                  pl.BlockSpec((1, tn), lambda j: (0, j))],
        out_specs=pl.BlockSpec((bsz, tn), lambda j: (0, j)),
        out_shape=jax.ShapeDtypeStruct((bsz, n), F32),
        compiler_params=_params("parallel"),
        name="mod",
    )(c, w, b)


def _norm_mod(x, g, sh, sc):
    y = x * lax.rsqrt(jnp.mean(x * x, axis=-1, keepdims=True) + RMS_EPS) * g
    return y * (1.0 + sc) + sh


def _norm_body(x_ref, g_ref, sh_ref, sc_ref, o_ref):
    o_ref[...] = _norm_mod(x_ref[...], g_ref[...], sh_ref[0], sc_ref[0]).astype(BF16)


def _norm(x2, g, sh, sc, seq, tm=512):
    m, d = x2.shape
    per_b = seq // tm
    return pl.pallas_call(
        _norm_body,
        grid=(m // tm,),
        in_specs=[pl.BlockSpec((tm, d), lambda i: (i, 0)),
                  pl.BlockSpec((1, d), lambda i: (0, 0)),
                  pl.BlockSpec((1, 1, d), lambda i: (i // per_b, 0, 0)),
                  pl.BlockSpec((1, 1, d), lambda i: (i // per_b, 0, 0))],
        out_specs=pl.BlockSpec((tm, d), lambda i: (i, 0)),
        out_shape=jax.ShapeDtypeStruct((m, d), BF16),
        compiler_params=_params("parallel"),
        name="norm_mix",
    )(x2, g, sh, sc)


def _mm_in_body(h_ref, w_ref, *rest, gelu_tile):
    ncast = (len(rest) - 2) // 2
    cast_in, o_ref, cast_out, wb_ref = rest[:ncast], rest[ncast], rest[ncast + 1:-1], rest[-1]

    @pl.when(pl.program_id(1) == 0)
    def _():
        wb_ref[...] = w_ref[...].astype(BF16)

    if gelu_tile is None:
        o_ref[...] = _dg(h_ref[...], wb_ref[...], _NT)
    else:
        @pl.when(pl.program_id(0) == gelu_tile)
        def _():
            o_ref[...] = jax.nn.gelu(_dg(h_ref[...], wb_ref[...], _NT))

        @pl.when(pl.program_id(0) != gelu_tile)
        def _():
            o_ref[...] = _dg(h_ref[...], wb_ref[...], _NT)

    for src, dst in zip(cast_in, cast_out):
        dst[...] = src[...].astype(BF16)


def _mm_in(h, wt, ncols, tm=1024, tn=1024, name="mm_in", gelu_tile=None, cast_ws=()):
    m, d = h.shape
    nj, ni = ncols // tn, m // tm
    cast_specs = []
    for wgt in cast_ws:
        tiles = wgt.shape[0] // BF16_SUBLANE
        nb = max(k for k in range(1, nj * ni + 1) if tiles % k == 0)
        cast_specs.append(pl.BlockSpec(
            (wgt.shape[0] // nb, wgt.shape[1]),
            lambda j, i, nb=nb: (jnp.minimum(j * ni + i, nb - 1), 0)))
    outs = pl.pallas_call(
        functools.partial(_mm_in_body, gelu_tile=gelu_tile),
        grid=(nj, ni),
        in_specs=[pl.BlockSpec((tm, d), lambda j, i: (i, 0)),
                  pl.BlockSpec((tn, d), lambda j, i: (j, 0))] + cast_specs,
        out_specs=[pl.BlockSpec((tm, tn), lambda j, i: (i, j))] + cast_specs,
        out_shape=[jax.ShapeDtypeStruct((m, ncols), F32)]
                  + [jax.ShapeDtypeStruct(wgt.shape, BF16) for wgt in cast_ws],
        scratch_shapes=[pltpu.VMEM((tn, d), BF16)],
        compiler_params=_params("arbitrary", "arbitrary"),
        name=name,
    )(h, wt, *cast_ws)
    return outs if cast_ws else outs[0]


def _lru_head(h, first, u_ref, gate_ref, halo_ref, cw_ref, cb_ref, wa_ref, wx_ref, ba_ref, bx_ref,
              lam_ref, o_ref, carry_ref):
    tt = u_ref.shape[0]
    hd = u_ref.shape[1] // LRU_HEADS
    cs = slice(h * hd, (h + 1) * hd)
    p = u_ref[:, cs]
    halo = jnp.where(first, 0.0, halo_ref[:, cs])
    ext = jnp.concatenate([halo, p], axis=0)
    cw = cw_ref[:, cs]
    u = cb_ref[:, cs] + p * cw[CONV_WIDTH - 1:CONV_WIDTH, :]
    for j in range(1, CONV_WIDTH):
        shifted = pltpu.roll(ext, j, 0)[SUBLANE:, :]
        u = u + shifted * cw[CONV_WIDTH - 1 - j:CONV_WIDTH - j, :]
    ub = u.astype(BF16)
    ra = jnp.dot(ub, wa_ref[h], preferred_element_type=F32)
    rx = jnp.dot(ub, wx_ref[h], preferred_element_type=F32)
    yield
    r = jax.nn.sigmoid(ra + ba_ref[:, cs])
    ig = jax.nn.sigmoid(rx + bx_ref[:, cs])
    a = jnp.exp(r * ((-LRU_C) * _softplus(-lam_ref[:, cs])))
    mult = jnp.sqrt(1.0 - a * a)
    row = lax.broadcasted_iota(jnp.int32, (tt, hd), 0)
    mult = jnp.where(jnp.logical_and(first, row == 0), 1.0, mult)
    b = mult * (ig * u)

    groups = tt // SUBLANE
    a3 = a.reshape(groups, SUBLANE, hd)
    b3 = b.reshape(groups, SUBLANE, hd)
    sub = lax.broadcasted_iota(jnp.int32, (groups, SUBLANE, hd), 1)
    s = 1
    while s < SUBLANE:
        keep = sub >= s
        a_s = jnp.where(keep, pltpu.roll(a3, s, 1), 1.0)
        b_s = jnp.where(keep, pltpu.roll(b3, s, 1), 0.0)
        b3 = a3 * b_s + b3
        a3 = a3 * a_s
        s *= 2
    yield
    gate = gate_ref[:, cs]
    carry = carry_ref[:, cs]
    for g in range(groups):
        hh = b3[g] + a3[g] * carry
        carry = hh[SUBLANE - 1:SUBLANE, :]
        o_ref[g * SUBLANE:(g + 1) * SUBLANE, cs] = hh * gate[g * SUBLANE:(g + 1) * SUBLANE, :]
    carry_ref[:, cs] = carry
    yield


def _token_shift(x, halo, mu, first, row):
    prev = jnp.where(first, 0.0, halo[SUBLANE - 1:SUBLANE, :])
    xs = jnp.where(row == 0, prev, pltpu.roll(x, 1, 0))
    return x + (xs - x) * mu


def _mm1(a, b, dims=_NN):
    return _dg(a.astype(BF16), b.astype(BF16), dims)


def _pair_diag(y, left):
    return jnp.concatenate([jnp.where(left, y, 0.0), jnp.where(left, 0.0, y)], axis=0).astype(BF16)


def _pair_mm(x, y, left):
    return _dg(x.astype(BF16), _pair_diag(y, left), _NN)


def _chunk_chain(ops, store):
    ab_, bb_, kb_, rb_, v_, bt_, kt_, pe_ = ops
    rc, lane = _iota2((CHUNK, PAIR))
    cc = lane % HEAD
    left = lane < HEAD
    strict = rc > cc
    incl = rc >= cc
    diag = rc == cc
    ar16 = [jnp.concatenate([x, y], axis=0).astype(BF16) for x, y in zip(ab_, rb_)]
    bd_b = [_pair_diag(x, left) for x in bb_]
    bd_k = [_pair_diag(x, left) for x in kb_]
    bd_v = [_pair_diag(x, left) for x in v_]
    arb = [_dg(x, y, _NT) for x, y in zip(ar16, bd_b)]
    ark = [_dg(x, y, _NT) for x, y in zip(ar16, bd_k)]
    a_ab = [jnp.where(strict, x[:CHUNK], 0.0) for x in arb]
    a_rb = [jnp.where(incl, x[CHUNK:], 0.0).astype(BF16) for x in arb]
    a_akrk = [jnp.concatenate([jnp.where(strict, x[:CHUNK], 0.0), jnp.where(incl, x[CHUNK:], 0.0)],
                              axis=0).astype(BF16) for x in ark]
    yield
    base = 8
    d = [jnp.where((rc // base) == (cc // base), a, 0.0) for a in a_ab]
    d2 = [_pair_mm(t, t, left) for t in d]
    akrkv = [_dg(x, y, _NN) for x, y in zip(a_akrk, bd_v)]
    akv = [x[:CHUNK] for x in akrkv]
    rkv = [x[CHUNK:] for x in akrkv]
    x = [jnp.where(diag, 1.0, 0.0) + t for t in d]
    yield
    x = [xi + _pair_mm(t2, xi, left) for xi, t2 in zip(x, d2)]
    d4 = [_pair_mm(t2, t2, left) for t2 in d2]
    yield
    x = [xi + _pair_mm(t4, xi, left) for xi, t4 in zip(x, d4)]
    yield
    size = base
    while size < CHUNK:
        off = jnp.logical_and((rc // (2 * size)) == (cc // (2 * size)),
                              (rc // size) != (cc // size))
        o = [jnp.where(off, a, 0.0) for a in a_ab]
        ox = [_pair_mm(oi, xi, left) for oi, xi in zip(o, x)]
        yield
        x = [xi + _pair_mm(xi, oxi, left) for xi, oxi in zip(x, ox)]
        yield
        size *= 2
    t = [xi.astype(BF16) for xi in x]
    wu = [_dg(ti, jnp.concatenate([_pair_diag(y, left), _pair_diag(z, left)], axis=1), _NN)
          for ti, y, z in zip(t, ab_, akv)]
    kv = [_dg(xi.astype(BF16), y.astype(BF16), _TN) for xi, y in zip(kt_, v_)]
    yield
    ry = [_dg(xi, jnp.concatenate([_pair_diag(y[:, :PAIR], left), _pair_diag(y[:, PAIR:], left)], axis=1), _NN)
          for xi, y in zip(a_rb, wu)]
    mn = [_dg(xi.astype(BF16), y.astype(BF16), _TN) for xi, y in zip(bt_, wu)]
    yield

    def head_blocks(z):
        return jnp.where(left, z[:HEAD, :], z[HEAD:, :])

    for u in range(len(ab_)):
        store(u,
              rb_[u] + ry[u][:, :PAIR],
              ry[u][:, PAIR:] + rkv[u],
              jnp.where(diag, pe_[u], 0.0) + head_blocks(mn[u][:, :PAIR]),
              head_blocks(mn[u][:, PAIR:]) + head_blocks(kv[u]))


def _rwkv_a_body(r_ref, k_ref, v_ref, l_ref, rh_ref, kh_ref, vh_ref, lh_ref,
                 mur_ref, muk_ref, muv_ref, mul_ref, w0_ref, a0_ref, kkw_ref, kaw_ref, rkw_ref,
                 w2h_ref, w2l_ref, a2h_ref, a2l_ref, g2h_ref, g2l_ref, ones_ref, tri_ref,
                 u_ref, gate_ref, halo_ref, cw_ref, cb_ref, wa_ref, wx_ref, ba_ref, bx_ref, lam_ref,
                 rp_ref, yp_ref, m_ref, n_ref, bonus_ref, g_ref, ya_ref, carry_ref):
    first = pl.program_id(1) == 0

    @pl.when(first)
    def _():
        carry_ref[...] = jnp.zeros_like(carry_ref)
    cl = CHUNK
    rows = CHUNKS_PER_STEP * cl
    width = HEADS_PER_STEP * HEAD
    gw = ones_ref.shape[0]
    row_g = lax.broadcasted_iota(jnp.int32, (rows, gw), 0)
    row_l = lax.broadcasted_iota(jnp.int32, (rows, l_ref.shape[1]), 0)
    ones_h = ones_ref[...]

    lo = _token_shift(l_ref[...], lh_ref[...], mul_ref[...], first, row_l)
    act_w = _split(jnp.tanh(lo[:, 0:LANE]))
    act_a = _split(lo[:, LANE:2 * LANE])
    act_g = _split(jax.nn.sigmoid(lo[:, 2 * LANE:]))

    def lora(act, wh_ref, wl_ref, cs):
        (ah, al_), bh, bl = act, wh_ref[:, cs], wl_ref[:, cs]
        return _dg(ah, bh, _NN) + (_dg(ah, bl, _NN) + _dg(al_, bh, _NN))

    def prologue(c0, out):
        cs = slice(c0, c0 + gw)
        r = _token_shift(r_ref[:, cs], rh_ref[:, cs], mur_ref[:, cs], first, row_g)
        k = _token_shift(k_ref[:, cs], kh_ref[:, cs], muk_ref[:, cs], first, row_g)
        v = _token_shift(v_ref[:, cs], vh_ref[:, cs], muv_ref[:, cs], first, row_g)
        w_lin = w0_ref[:, cs] + lora(act_w, w2h_ref, w2l_ref, cs)
        a_lin = a0_ref[:, cs] + lora(act_a, a2h_ref, a2l_ref, cs)
        g_ref[:, cs] = lora(act_g, g2h_ref, g2l_ref, cs)
        kk = k * kkw_ref[:, cs]
        kk_ss = _mm2_exact_rhs(kk * kk, ones_h)
        yield
        lw = DECAY_SCALE * jax.nn.sigmoid(w_lin)
        a = jax.nn.sigmoid(a_lin)
        kk = kk * lax.rsqrt(jnp.maximum(kk_ss, L2_EPS * L2_EPS))
        kp = k * (1.0 + (a - 1.0) * kaw_ref[:, cs])
        bonus_ref[:, cs] = _mm2_exact_rhs(r * kp * rkw_ref[:, cs], ones_h) * v
        lc = _mm2_exact_lhs(tri_ref[...], lw)
        yield
        p_incl = jnp.exp(lc)
        p_excl = jnp.exp(lc - lw)
        p_inv = 1.0 / p_incl
        p_end = jnp.concatenate(
            [jnp.broadcast_to(p_incl[(j + 1) * cl - 1:(j + 1) * cl, :], (cl, gw))
             for j in range(CHUNKS_PER_STEP)], axis=0)
        abar = -(kk * p_excl)
        bbar = kk * a * p_inv
        kbar = kp * p_inv
        rbar = r * p_incl
        btil = bbar * p_end
        ktil = kbar * p_end
        units = [(j, q) for j in range(CHUNKS_PER_STEP) for q in range(gw // PAIR)]
        out.extend([x[j * cl:(j + 1) * cl, q * PAIR:(q + 1) * PAIR] for j, q in units]
                   for x in (abar, bbar, kbar, rbar, v, btil, ktil, p_end))
        yield

    def make_store(c0):
        units = [(j, q) for j in range(CHUNKS_PER_STEP) for q in range(gw // PAIR)]

        def store(u, rp, yp, mm, nn):
            j, q = units[u]
            rs = slice(j * cl, (j + 1) * cl)
            qs = slice(c0 + q * PAIR, c0 + (q + 1) * PAIR)
            rp_ref[rs, qs] = rp.astype(BF16)
            yp_ref[rs, qs] = yp
            m_ref[rs, qs] = mm.astype(BF16)
            n_ref[rs, qs] = nn
        return store

    lru = (None for h in range(LRU_HEADS)
           for _ in _lru_head(h, first, u_ref, gate_ref, halo_ref, cw_ref, cb_ref, wa_ref, wx_ref,
                              ba_ref, bx_ref, lam_ref, ya_ref, carry_ref))
    chains = []
    for c0 in range(0, width, gw):
        ops = []
        for _ in prologue(c0, ops):
            for ch in chains:
                next(ch, None)
        chains.append(_chunk_chain(ops, make_store(c0)))
    live = list(chains)
    while live:
        live = [ch for ch in live if next(ch, StopIteration) is not StopIteration]
        next(lru, None)
    for _ in lru:
        pass


def _rwkv_a(p, p_lora, mu_rkv, mu_lora, w0, a0, k_k, k_a, r_k, w2p, a2p, g2p,
            conv_w, conv_b, wa, wx, ba, bx, lam, bsz, seq, rkv_col0):
    dl = conv_w.shape[1]
    lvec = lambda t: t.reshape(1, dl)
    lrow = pl.BlockSpec((1, dl), lambda b, i, q: (0, 0))

    def lru_tile(cb):
        return pl.BlockSpec((CHUNKS_PER_STEP * CHUNK, dl), lambda b, i, q: (b * nc + i, cb))
    cl = CHUNKS_PER_STEP * CHUNK
    width = HEADS_PER_STEP * HEAD
    dr = w0.shape[1]
    ngroups = dr // width
    assert ngroups == 1, "the LRU ride-along expects one grid step per row tile"
    nc = seq // cl
    lw_ = mu_lora.shape[1]
    cb0 = rkv_col0 // width
    rows8 = cl // SUBLANE
    rt, ct = _iota2((cl, cl))
    tri = jnp.where(jnp.logical_and(rt >= ct, (rt // CHUNK) == (ct // CHUNK)), 1.0, 0.0).astype(BF16)
    ones_h = _head_ones(ONES_WIDTH)
    const = lambda arr: pl.BlockSpec(arr.shape, lambda b, i, q: (0, 0))
    lora_w = [t for wgt in (w2p, a2p, g2p) for t in _split(wgt)]

    def tile(cb_off):
        return pl.BlockSpec((cl, width), lambda b, i, q: (b * nc + i, cb0 + cb_off + q))

    def halo(cb_off):
        return pl.BlockSpec(
            (SUBLANE, width),
            lambda b, i, q: (jnp.maximum((b * nc + i) * rows8 - 1, 0), cb0 + cb_off + q))

    def prow(off=0):
        return pl.BlockSpec((1, width), lambda b, i, q: (0, off + q))

    out_tile = pl.BlockSpec((cl, width), lambda b, i, q: (b * nc + i, q))
    out_mat = pl.BlockSpec((CHUNKS_PER_STEP * HEAD, width), lambda b, i, q: (b * nc + i, q))
    act = jax.ShapeDtypeStruct((bsz * seq, dr), F32)
    mat = jax.ShapeDtypeStruct((bsz * (seq // CHUNK) * HEAD, dr), F32)

    return pl.pallas_call(
        _rwkv_a_body,
        grid=(bsz, nc, ngroups),
        in_specs=[tile(0), tile(ngroups), tile(2 * ngroups),
                  pl.BlockSpec((cl, lw_), lambda b, i, q: (b * nc + i, 0)),
                  halo(0), halo(ngroups), halo(2 * ngroups),
                  pl.BlockSpec((SUBLANE, lw_),
                               lambda b, i, q: (jnp.maximum((b * nc + i) * rows8 - 1, 0), 0)),
                  prow(0), prow(ngroups), prow(2 * ngroups),
                  pl.BlockSpec((1, lw_), lambda b, i, q: (0, 0)),
                  prow(), prow(), prow(), prow(), prow()]
                 + [pl.BlockSpec((t.shape[0], width), lambda b, i, q: (0, q)) for t in lora_w]
                 + [const(ones_h), const(tri)]
                 + [lru_tile(0), lru_tile(1),
                    pl.BlockSpec((SUBLANE, dl),
                                 lambda b, i, q: (jnp.maximum((b * nc + i) * rows8 - 1, 0), 0)),
                    const(conv_w), lrow, pl.BlockSpec(wa.shape, lambda b, i, q: (0, 0, 0)),
                    pl.BlockSpec(wx.shape, lambda b, i, q: (0, 0, 0)), lrow, lrow, lrow],
        out_specs=[out_tile, out_tile, out_mat, out_mat, out_tile, out_tile, lru_tile(0)],
        out_shape=[jax.ShapeDtypeStruct(act.shape, BF16), act, jax.ShapeDtypeStruct(mat.shape, BF16),
                   mat, act, act, jax.ShapeDtypeStruct((bsz * seq, dl), F32)],
        scratch_shapes=[pltpu.VMEM((1, dl), F32)],
        compiler_params=_params("parallel", "arbitrary", "arbitrary"),
        name="rwkv_a",
    )(p, p, p, p_lora, p, p, p, p_lora, mu_rkv, mu_rkv, mu_rkv, mu_lora, w0, a0, k_k, k_a, r_k,
      *lora_w, ones_h, tri,
      p, p, p, conv_w, lvec(conv_b), wa, wx, lvec(ba), lvec(bx), lvec(lam))


def _rwkv_b_body(rp_ref, yp_ref, m_ref, n_ref, bonus_ref, g_ref, lng_ref, lnb_ref, ones_ref,
                 o_ref, state_ref):
    @pl.when(pl.program_id(1) == 0)
    def _():
        state_ref[...] = jnp.zeros_like(state_ref)

    npairs = state_ref.shape[0]
    pairs = range(npairs)
    ps = [slice(q * PAIR, (q + 1) * PAIR) for q in pairs]
    left = lax.broadcasted_iota(jnp.int32, (HEAD, PAIR), 1) < HEAD
    ones_h = ones_ref[...]
    inv_n = 1.0 / HEAD
    state = [state_ref[q] for q in pairs]
    for j in range(rp_ref.shape[0] // CHUNK):
        rs = slice(j * CHUNK, (j + 1) * CHUNK)
        ks = slice(j * HEAD, (j + 1) * HEAD)
        g0 = [_pair_diag(state[q], left) for q in pairs]
        ys = [_dg(rp_ref[rs, ps[q]].astype(BF16), g0[q], _NN) + yp_ref[rs, ps[q]] for q in pairs]
        state = [_dg(m_ref[ks, ps[q]].astype(BF16), g0[q], _NN) + n_ref[ks, ps[q]] for q in pairs]
        y = jnp.concatenate(ys, axis=1)
        yc = y - _head_sums(y, ones_h) * inv_n
        var = _head_sums(yc * yc, ones_h) * inv_n
        yn = yc * lax.rsqrt(var + GN_EPS) * lng_ref[...] + lnb_ref[...]
        o_ref[rs, :] = (yn + bonus_ref[rs, :]) * g_ref[rs, :]
    for q in pairs:
        state_ref[q] = state[q]


def _rwkv_b(rp, yp, mc, nm, bonus, g, ln_g, ln_b, bsz, seq):
    cl = RWKV_B_CHUNKS * CHUNK
    dr = rp.shape[1]
    nc = seq // cl
    tile = pl.BlockSpec((cl, dr), lambda b, i: (b * nc + i, 0))
    mat = pl.BlockSpec((RWKV_B_CHUNKS * HEAD, dr), lambda b, i: (b * nc + i, 0))
    prow = pl.BlockSpec((1, dr), lambda b, i: (0, 0))
    ones_h = _head_ones(ONES_WIDTH)
    return pl.pallas_call(
        _rwkv_b_body,
        grid=(bsz, nc),
        in_specs=[tile, tile, mat, mat, tile, tile, prow, prow,
                  pl.BlockSpec(ones_h.shape, lambda b, i: (0, 0))],
        out_specs=tile,
        out_shape=jax.ShapeDtypeStruct((bsz * seq, dr), F32),
        scratch_shapes=[pltpu.VMEM((dr // PAIR, HEAD, PAIR), F32)],
        compiler_params=_params("parallel", "arbitrary"),
        name="rwkv_b",
    )(rp, yp, mc, nm, bonus, g, ln_g, ln_b, ones_h)


def _mm_out_body(ya_ref, yb_ref, x_ref, gm_ref, w_ref, g_ref, sh_ref, sc_ref, o_ref, h_ref, *, sub):
    da = ya_ref.shape[1]
    for r0 in range(0, x_ref.shape[0], sub):
        rs = slice(r0, r0 + sub)
        mix = (jnp.dot(ya_ref[rs, :].astype(BF16), w_ref[:da, :], preferred_element_type=F32)
               + jnp.dot(yb_ref[rs, :].astype(BF16), w_ref[da:, :], preferred_element_type=F32))
        x1 = x_ref[rs, :] + gm_ref[0] * mix
        o_ref[rs, :] = x1
        h_ref[rs, :] = _norm_mod(x1, g_ref[...], sh_ref[0], sc_ref[0]).astype(BF16)


def _mm_out(ya, yb, x2, gm, w, g, sh, sc, seq, tm=512, sub=256):
    m, d = x2.shape
    per_b = seq // tm
    brow = pl.BlockSpec((1, 1, d), lambda i: (i // per_b, 0, 0))
    tile = pl.BlockSpec((tm, d), lambda i: (i, 0))
    return pl.pallas_call(
        functools.partial(_mm_out_body, sub=sub),
        grid=(m // tm,),
        in_specs=[pl.BlockSpec((tm, ya.shape[1]), lambda i: (i, 0)),
                  pl.BlockSpec((tm, yb.shape[1]), lambda i: (i, 0)),
                  tile, brow,
                  pl.BlockSpec(w.shape, lambda i: (0, 0)),
                  pl.BlockSpec((1, d), lambda i: (0, 0)), brow, brow],
        out_specs=[tile, tile],
        out_shape=[jax.ShapeDtypeStruct((m, d), F32), jax.ShapeDtypeStruct((m, d), BF16)],
        compiler_params=_params("parallel"),
        name="mm_out",
    )(ya, yb, x2, gm, w, g, sh, sc)


def _ffn_body(x_ref, h_ref, gf_ref, wg_ref, wu_ref, wd_ref, fg_ref, o_ref, acc_ref):
    f = pl.program_id(1)

    @pl.when(f == 0)
    def _():
        acc_ref[...] = jnp.zeros_like(acc_ref)

    h = h_ref[...]
    gate = jnp.dot(h, wg_ref[...], preferred_element_type=F32)
    up = jnp.dot(h, wu_ref[...], preferred_element_type=F32)
    act = (gate * jax.nn.sigmoid(gate) * up).astype(BF16)
    acc_ref[...] += jnp.dot(act, wd_ref[...], preferred_element_type=F32)

    @pl.when(f == pl.num_programs(1) - 1)
    def _():
        y = x_ref[...] + gf_ref[0] * acc_ref[...]
        o_ref[...] = (y * lax.rsqrt(jnp.mean(y * y, axis=-1, keepdims=True) + RMS_EPS)
                      * fg_ref[...])


def _ffn(x1, h2, gf, w_gu, w_down, fg, seq, tm=512, tf=512):
    m, d = x1.shape
    dff = w_down.shape[0]
    nf = dff // tf
    assert seq % tm == 0 and dff % tf == 0, "row tiles must not straddle sequences"
    per_b = seq // tm
    tile = pl.BlockSpec((tm, d), lambda i, f: (i, 0))
    prow = pl.BlockSpec((1, d), lambda i, f: (0, 0))
    return pl.pallas_call(
        _ffn_body,
        grid=(m // tm, nf),
        in_specs=[tile, tile,
                  pl.BlockSpec((1, 1, d), lambda i, f: (i // per_b, 0, 0)),
                  pl.BlockSpec((d, tf), lambda i, f: (0, f)),
                  pl.BlockSpec((d, tf), lambda i, f: (0, nf + f)),
                  pl.BlockSpec((tf, d), lambda i, f: (f, 0)),
                  prow],
        out_specs=tile,
        out_shape=jax.ShapeDtypeStruct((m, d), F32),
        scratch_shapes=[pltpu.VMEM((tm, d), F32)],
        compiler_params=_params("parallel", "arbitrary"),
        name="ffn",
    )(x1, h2, gf, w_gu, w_gu, w_down, fg)


def _pad_cols(w, n):
    return jnp.pad(w, ((0, 0), (0, n - w.shape[1])))


def _pad_rows(w, n):
    return jnp.pad(w, ((0, n - w.shape[0]), (0, 0)))


def kernel(x, c, w_ada, b_ada, norm_mix_g, w_in, conv_w, conv_b, lru_wa, lru_ba, lru_wx, lru_bx, lru_lambda, rwkv_mu, rwkv_w0, rwkv_w2, rwkv_a0, rwkv_a2, rwkv_g2, rwkv_k_k, rwkv_k_a, rwkv_r_k, rwkv_ln_g, rwkv_ln_b, w_out, norm_ffn_g, w_gu, w_down, final_norm_g):
    bsz, seq, d = x.shape
    depth = w_ada.shape[0]
    dl = conv_w.shape[2]
    dr = rwkv_w0.shape[1]
    w_lora, a_lora, g_lora = rwkv_w2.shape[1], rwkv_a2.shape[1], rwkv_g2.shape[1]
    wpad, apad = LANE, LANE
    gpad = -(-g_lora // LANE) * LANE
    rkv_col0 = 2 * dl
    lora0 = rkv_col0 + 3 * dr

    x2 = x.reshape(bsz * seq, d)
    for l in range(depth):
        mod = _mod(c, w_ada[l], b_ada[l].reshape(1, -1))
        sh_m, sc_m, g_m, sh_f, sc_f, g_f = [t.reshape(bsz, 1, d) for t in jnp.split(mod, 6, axis=-1)]

        wi = jnp.swapaxes(w_in[l], 0, 1)
        o1, o2 = lora0 + w_lora, lora0 + w_lora + a_lora
        w_lora_p = jnp.concatenate(
            [_pad_rows(wi[lora0:o1], wpad), _pad_rows(wi[o1:o2], apad),
             _pad_rows(wi[o2:], gpad)], axis=0)
        mu = rwkv_mu[l].reshape(1, -1)
        mu_rkv = mu[:, :3 * dr]
        mu_lora = jnp.concatenate(
            [_pad_cols(mu[:, 3 * dr:3 * dr + w_lora], wpad),
             _pad_cols(mu[:, 3 * dr + w_lora:3 * dr + w_lora + a_lora], apad),
             _pad_cols(mu[:, 3 * dr + w_lora + a_lora:], gpad)], axis=1)
        w2p = _pad_rows(rwkv_w2[l], wpad)
        a2p = _pad_rows(rwkv_a2[l], apad)
        g2p = _pad_rows(rwkv_g2[l], gpad)

        h = _norm(x2, norm_mix_g[l].reshape(1, d), sh_m, sc_m, seq)
        tn_in = dl
        p, w_out_b, w_gu_b, w_down_b = _mm_in(
            h, wi, lora0, tn=tn_in, gelu_tile=dl // tn_in, cast_ws=(w_out[l], w_gu[l], w_down[l]))
        p_lora = _mm_in(h, w_lora_p, w_lora_p.shape[0], tn=w_lora_p.shape[0], name="mm_lora")

        rowv = lambda t: t.reshape(1, dr)
        rp, yp, mc, nm, bonus, gg, y_a = _rwkv_a(
            p, p_lora, mu_rkv, mu_lora, rowv(rwkv_w0[l]), rowv(rwkv_a0[l]), rowv(rwkv_k_k[l]),
            rowv(rwkv_k_a[l]), rowv(rwkv_r_k[l]), w2p, a2p, g2p,
            conv_w[l], conv_b[l], lru_wa[l].astype(BF16), lru_wx[l].astype(BF16),
            lru_ba[l], lru_bx[l], lru_lambda[l], bsz, seq, rkv_col0)
        y_b = _rwkv_b(rp, yp, mc, nm, bonus, gg, rowv(rwkv_ln_g[l]), rowv(rwkv_ln_b[l]), bsz, seq)

        x2, h2 = _mm_out(y_a, y_b, x2, g_m, w_out_b, norm_ffn_g[l].reshape(1, d), sh_f, sc_f, seq)

        last = l == depth - 1
        fg = final_norm_g.reshape(1, d) if last else None
        assert last, "only the final layer carries the closing RMSNorm"
        x2 = _ffn(x2, h2, g_f, w_gu_b, w_down_b, fg, seq)
    return x2.reshape(bsz, seq, d)
```

```python
import functools
import math

import jax
import jax.numpy as jnp
from jax import lax
from jax.experimental import pallas as pl
from jax.experimental.pallas import tpu as pltpu

F32 = jnp.float32
BF16 = jnp.bfloat16

LRU_HEADS = 4
CONV_WIDTH = 4
LRU_C = 8.0
HEAD = 64
CHUNK = 64
PAIR = 2 * HEAD
HEADS_PER_STEP = 16
ONES_WIDTH = 256
CHUNKS_PER_STEP = 2
RWKV_B_CHUNKS = 4
RMS_EPS = 1e-6
GN_EPS = 64e-5
L2_EPS = 1e-12
DECAY_SCALE = -math.exp(-0.5)
LANE = 128
SUBLANE = 8
BF16_SUBLANE = 16
VMEM_LIMIT = 56 * 1024 * 1024


def _params(*sem):
    return pltpu.CompilerParams(dimension_semantics=sem, vmem_limit_bytes=VMEM_LIMIT)


_NN = (((1,), (0,)), ((), ()))
_NT = (((1,), (1,)), ((), ()))
_TN = (((0,), (0,)), ((), ()))


def _dg(a, b, dims):
    return lax.dot_general(a, b, dims, preferred_element_type=F32)


def _split(x):
    hi = x.astype(BF16)
    lo = (x - hi.astype(F32)).astype(BF16)
    return hi, lo


def _mm3(a, b, dims=_NN):
    ah, al = _split(a)
    bh, bl = _split(b)
    return _dg(ah, bh, dims) + (_dg(ah, bl, dims) + _dg(al, bh, dims))


def _mm3_presplit(a, bh, bl):
    ah, al = _split(a)
    return _dg(ah, bh, _NN) + (_dg(ah, bl, _NN) + _dg(al, bh, _NN))


def _mm2_exact_rhs(a, b_bf16):
    ah, al = _split(a)
    return _dg(ah, b_bf16, _NN) + _dg(al, b_bf16, _NN)


def _head_sums(x, ones_h):
    n = ones_h.shape[0]
    return jnp.concatenate([_mm2_exact_rhs(x[:, c:c + n], ones_h) for c in range(0, x.shape[1], n)],
                           axis=1)


def _mm2_exact_lhs(a_bf16, b):
    bh, bl = _split(b)
    return _dg(a_bf16, bh, _NN) + _dg(a_bf16, bl, _NN)


def _softplus(x):
    return jnp.maximum(x, 0.0) + jnp.log1p(jnp.exp(-jnp.abs(x)))


def _iota2(shape):
    return (lax.broadcasted_iota(jnp.int32, shape, 0),
            lax.broadcasted_iota(jnp.int32, shape, 1))


def _head_ones(n):
    r, c = _iota2((n, n))
    return jnp.where((r // HEAD) == (c // HEAD), 1.0, 0.0).astype(BF16)


def _mod_body(c_ref, w_ref, b_ref, o_ref):
    c = c_ref[...]
    ca = c * jax.nn.sigmoid(c)
    o_ref[...] = _mm3(ca, w_ref[...]) + b_ref[...]


def _mod(c, w, b, tn=1024):
    bsz, d = c.shape
    n = w.shape[1]
    return pl.pallas_call(
        _mod_body,
        grid=(n // tn,),
        in_specs=[pl.BlockSpec((bsz, d), lambda j: (0, 0)),
                  pl.BlockSpec((d, tn), lambda j: (0, j)),
                  pl.BlockSpec((1, tn), lambda j: (0, j))],
        out_specs=pl.BlockSpec((bsz, tn), lambda j: (0, j)),
        out_shape=jax.ShapeDtypeStruct((bsz, n), F32),
        compiler_params=_params("parallel"),
        name="mod",
    )(c, w, b)


def _norm_mod(x, g, sh, sc):
    y = x * lax.rsqrt(jnp.mean(x * x, axis=-1, keepdims=True) + RMS_EPS) * g
    return y * (1.0 + sc) + sh


def _norm_body(x_ref, g_ref, sh_ref, sc_ref, o_ref):
    o_ref[...] = _norm_mod(x_ref[...], g_ref[...], sh_ref[0], sc_ref[0]).astype(BF16)


def _norm(x2, g, sh, sc, seq, tm=512):
    m, d = x2.shape
    per_b = seq // tm
    return pl.pallas_call(
        _norm_body,
        grid=(m // tm,),
        in_specs=[pl.BlockSpec((tm, d), lambda i: (i, 0)),
                  pl.BlockSpec((1, d), lambda i: (0, 0)),
                  pl.BlockSpec((1, 1, d), lambda i: (i // per_b, 0, 0)),
                  pl.BlockSpec((1, 1, d), lambda i: (i // per_b, 0, 0))],
        out_specs=pl.BlockSpec((tm, d), lambda i: (i, 0)),
        out_shape=jax.ShapeDtypeStruct((m, d), BF16),
        compiler_params=_params("parallel"),
        name="norm_mix",
    )(x2, g, sh, sc)


def _mm_in_body(h_ref, w_ref, o_ref, wb_ref, *, gelu_tile):
    @pl.when(pl.program_id(1) == 0)
    def _():
        wb_ref[...] = w_ref[...].astype(BF16)

    if gelu_tile is None:
        o_ref[...] = _dg(h_ref[...], wb_ref[...], _NT)
    else:
        @pl.when(pl.program_id(0) == gelu_tile)
        def _():
            o_ref[...] = jax.nn.gelu(_dg(h_ref[...], wb_ref[...], _NT))

        @pl.when(pl.program_id(0) != gelu_tile)
        def _():
            o_ref[...] = _dg(h_ref[...], wb_ref[...], _NT)


def _mm_in(h, wt, ncols, tm=1024, tn=1024, name="mm_in", gelu_tile=None):
    m, d = h.shape
    return pl.pallas_call(
        functools.partial(_mm_in_body, gelu_tile=gelu_tile),
        grid=(ncols // tn, m // tm),
        in_specs=[pl.BlockSpec((tm, d), lambda j, i: (i, 0)),
                  pl.BlockSpec((tn, d), lambda j, i: (j, 0))],
        out_specs=pl.BlockSpec((tm, tn), lambda j, i: (i, j)),
        out_shape=jax.ShapeDtypeStruct((m, ncols), F32),
        scratch_shapes=[pltpu.VMEM((tn, d), BF16)],
        compiler_params=_params("parallel", "arbitrary"),
        name=name,
    )(h, wt)


def _lru_head(h, first, u_ref, gate_ref, halo_ref, cw_ref, cb_ref, wa_ref, wx_ref, ba_ref, bx_ref,
              lam_ref, o_ref, carry_ref):
    tt = u_ref.shape[0]
    hd = u_ref.shape[1] // LRU_HEADS
    cs = slice(h * hd, (h + 1) * hd)
    p = u_ref[:, cs]
    halo = jnp.where(first, 0.0, halo_ref[:, cs])
    ext = jnp.concatenate([halo, p], axis=0)
    cw = cw_ref[:, cs]
    u = cb_ref[:, cs] + p * cw[CONV_WIDTH - 1:CONV_WIDTH, :]
    for j in range(1, CONV_WIDTH):
        shifted = pltpu.roll(ext, j, 0)[SUBLANE:, :]
        u = u + shifted * cw[CONV_WIDTH - 1 - j:CONV_WIDTH - j, :]
    ub = u.astype(BF16)
    ra = jnp.dot(ub, wa_ref[h], preferred_element_type=F32)
    rx = jnp.dot(ub, wx_ref[h], preferred_element_type=F32)
    yield
    r = jax.nn.sigmoid(ra + ba_ref[:, cs])
    ig = jax.nn.sigmoid(rx + bx_ref[:, cs])
    a = jnp.exp(r * ((-LRU_C) * _softplus(-lam_ref[:, cs])))
    mult = jnp.sqrt(1.0 - a * a)
    row = lax.broadcasted_iota(jnp.int32, (tt, hd), 0)
    mult = jnp.where(jnp.logical_and(first, row == 0), 1.0, mult)
    b = mult * (ig * u)

    groups = tt // SUBLANE
    a3 = a.reshape(groups, SUBLANE, hd)
    b3 = b.reshape(groups, SUBLANE, hd)
    sub = lax.broadcasted_iota(jnp.int32, (groups, SUBLANE, hd), 1)
    s = 1
    while s < SUBLANE:
        keep = sub >= s
        a_s = jnp.where(keep, pltpu.roll(a3, s, 1), 1.0)
        b_s = jnp.where(keep, pltpu.roll(b3, s, 1), 0.0)
        b3 = a3 * b_s + b3
        a3 = a3 * a_s
        s *= 2
    yield
    gate = gate_ref[:, cs]
    carry = carry_ref[:, cs]
    outs = []
    for g in range(groups):
        hh = b3[g] + a3[g] * carry
        carry = hh[SUBLANE - 1:SUBLANE, :]
        outs.append(hh * gate[g * SUBLANE:(g + 1) * SUBLANE, :])
    per = BF16_SUBLANE // SUBLANE
    for t0 in range(0, groups, per):
        o_ref[t0 * SUBLANE:(t0 + per) * SUBLANE, cs] = jnp.concatenate(
            outs[t0:t0 + per], axis=0).astype(BF16)
    carry_ref[:, cs] = carry
    yield


def _token_shift(x, halo, mu, first, row):
    prev = jnp.where(first, 0.0, halo[SUBLANE - 1:SUBLANE, :])
    xs = jnp.where(row == 0, prev, pltpu.roll(x, 1, 0))
    return x + (xs - x) * mu


def _mm1(a, b, dims=_NN):
    return _dg(a.astype(BF16), b.astype(BF16), dims)


def _pair_diag(y, left):
    return jnp.concatenate([jnp.where(left, y, 0.0), jnp.where(left, 0.0, y)], axis=0).astype(BF16)


def _pair_mm(x, y, left):
    return _dg(x.astype(BF16), _pair_diag(y, left), _NN)


def _chunk_chain(ops, store):
    ab_, bb_, kb_, rb_, v_, bt_, kt_, pe_ = ops
    rc, lane = _iota2((CHUNK, PAIR))
    cc = lane % HEAD
    left = lane < HEAD
    strict = rc > cc
    incl = rc >= cc
    diag = rc == cc
    ar16 = [jnp.concatenate([x, y], axis=0).astype(BF16) for x, y in zip(ab_, rb_)]
    bd_b = [_pair_diag(x, left) for x in bb_]
    bd_k = [_pair_diag(x, left) for x in kb_]
    bd_v = [_pair_diag(x, left) for x in v_]
    arb = [_dg(x, y, _NT) for x, y in zip(ar16, bd_b)]
    ark = [_dg(x, y, _NT) for x, y in zip(ar16, bd_k)]
    a_ab = [jnp.where(strict, x[:CHUNK], 0.0) for x in arb]
    a_rb = [jnp.where(incl, x[CHUNK:], 0.0).astype(BF16) for x in arb]
    a_akrk = [jnp.concatenate([jnp.where(strict, x[:CHUNK], 0.0), jnp.where(incl, x[CHUNK:], 0.0)],
                              axis=0).astype(BF16) for x in ark]
    yield
    base = 8
    d = [jnp.where((rc // base) == (cc // base), a, 0.0) for a in a_ab]
    d2 = [_pair_mm(t, t, left) for t in d]
    akrkv = [_dg(x, y, _NN) for x, y in zip(a_akrk, bd_v)]
    akv = [x[:CHUNK] for x in akrkv]
    rkv = [x[CHUNK:] for x in akrkv]
    x = [jnp.where(diag, 1.0, 0.0) + t for t in d]
    yield
    x = [xi + _pair_mm(t2, xi, left) for xi, t2 in zip(x, d2)]
    d4 = [_pair_mm(t2, t2, left) for t2 in d2]
    yield
    x = [xi + _pair_mm(t4, xi, left) for xi, t4 in zip(x, d4)]
    yield
    size = base
    while size < CHUNK:
        off = jnp.logical_and((rc // (2 * size)) == (cc // (2 * size)),
                              (rc // size) != (cc // size))
        o = [jnp.where(off, a, 0.0) for a in a_ab]
        ox = [_pair_mm(oi, xi, left) for oi, xi in zip(o, x)]
        yield
        x = [xi + _pair_mm(xi, oxi, left) for xi, oxi in zip(x, ox)]
        yield
        size *= 2
    t = [xi.astype(BF16) for xi in x]
    wu = [_dg(ti, jnp.concatenate([_pair_diag(y, left), _pair_diag(z, left)], axis=1), _NN)
          for ti, y, z in zip(t, ab_, akv)]
    kv = [_dg(xi.astype(BF16), y.astype(BF16), _TN) for xi, y in zip(kt_, v_)]
    yield
    ry = [_dg(xi, jnp.concatenate([_pair_diag(y[:, :PAIR], left), _pair_diag(y[:, PAIR:], left)], axis=1), _NN)
          for xi, y in zip(a_rb, wu)]
    mn = [_dg(xi.astype(BF16), y.astype(BF16), _TN) for xi, y in zip(bt_, wu)]
    yield

    def head_blocks(z):
        return jnp.where(left, z[:HEAD, :], z[HEAD:, :])

    for u in range(len(ab_)):
        store(u,
              rb_[u] + ry[u][:, :PAIR],
              ry[u][:, PAIR:] + rkv[u],
              jnp.where(diag, pe_[u], 0.0) + head_blocks(mn[u][:, :PAIR]),
              head_blocks(mn[u][:, PAIR:]) + head_blocks(kv[u]))


def _rwkv_a_body(r_ref, k_ref, v_ref, l_ref, rh_ref, kh_ref, vh_ref, lh_ref,
                 mur_ref, muk_ref, muv_ref, mul_ref, w0_ref, a0_ref, kkw_ref, kaw_ref, rkw_ref,
                 w2h_ref, w2l_ref, a2h_ref, a2l_ref, g2h_ref, g2l_ref, ones_ref, tri_ref,
                 u_ref, gate_ref, halo_ref, cw_ref, cb_ref, wa_ref, wx_ref, ba_ref, bx_ref, lam_ref,
                 *rest):
    ncast = (len(rest) - 8) // 2
    cast_in = rest[:ncast]
    rp_ref, yp_ref, m_ref, n_ref, bonus_ref, g_ref, ya_ref = rest[ncast:ncast + 7]
    cast_out = rest[ncast + 7:-1]
    carry_ref = rest[-1]
    first = pl.program_id(1) == 0

    @pl.when(first)
    def _():
        carry_ref[...] = jnp.zeros_like(carry_ref)
    cl = CHUNK
    rows = CHUNKS_PER_STEP * cl
    width = HEADS_PER_STEP * HEAD
    gw = ones_ref.shape[0]
    row_g = lax.broadcasted_iota(jnp.int32, (rows, gw), 0)
    row_l = lax.broadcasted_iota(jnp.int32, (rows, l_ref.shape[1]), 0)
    ones_h = ones_ref[...]

    lo = _token_shift(l_ref[...], lh_ref[...], mul_ref[...], first, row_l)
    act_w = _split(jnp.tanh(lo[:, 0:LANE]))
    act_a = _split(lo[:, LANE:2 * LANE])
    act_g = _split(jax.nn.sigmoid(lo[:, 2 * LANE:]))

    def lora(act, wh_ref, wl_ref, cs):
        (ah, al_), bh, bl = act, wh_ref[:, cs], wl_ref[:, cs]
        return _dg(ah, bh, _NN) + (_dg(ah, bl, _NN) + _dg(al_, bh, _NN))

    def prologue(c0, out):
        cs = slice(c0, c0 + gw)
        r = _token_shift(r_ref[:, cs], rh_ref[:, cs], mur_ref[:, cs], first, row_g)
        k = _token_shift(k_ref[:, cs], kh_ref[:, cs], muk_ref[:, cs], first, row_g)
        v = _token_shift(v_ref[:, cs], vh_ref[:, cs], muv_ref[:, cs], first, row_g)
        w_lin = w0_ref[:, cs] + lora(act_w, w2h_ref, w2l_ref, cs)
        a_lin = a0_ref[:, cs] + lora(act_a, a2h_ref, a2l_ref, cs)
        g_ref[:, cs] = lora(act_g, g2h_ref, g2l_ref, cs).astype(BF16)
        kk = k * kkw_ref[:, cs]
        kk_ss = _mm2_exact_rhs(kk * kk, ones_h)
        yield
        lw = DECAY_SCALE * jax.nn.sigmoid(w_lin)
        a = jax.nn.sigmoid(a_lin)
        kk = kk * lax.rsqrt(jnp.maximum(kk_ss, L2_EPS * L2_EPS))
        kp = k * (1.0 + (a - 1.0) * kaw_ref[:, cs])
        bonus_ref[:, cs] = (_mm2_exact_rhs(r * kp * rkw_ref[:, cs], ones_h) * v).astype(BF16)
        lc = _mm2_exact_lhs(tri_ref[...], lw)
        yield
        p_incl = jnp.exp(lc)
        p_excl = jnp.exp(lc - lw)
        p_inv = 1.0 / p_incl
        p_end = jnp.concatenate(
            [jnp.broadcast_to(p_incl[(j + 1) * cl - 1:(j + 1) * cl, :], (cl, gw))
             for j in range(CHUNKS_PER_STEP)], axis=0)
        abar = -(kk * p_excl)
        bbar = kk * a * p_inv
        kbar = kp * p_inv
        rbar = r * p_incl
        btil = bbar * p_end
        ktil = kbar * p_end
        units = [(j, q) for j in range(CHUNKS_PER_STEP) for q in range(gw // PAIR)]
        out.extend([x[j * cl:(j + 1) * cl, q * PAIR:(q + 1) * PAIR] for j, q in units]
                   for x in (abar, bbar, kbar, rbar, v, btil, ktil, p_end))
        yield

    def make_store(c0):
        units = [(j, q) for j in range(CHUNKS_PER_STEP) for q in range(gw // PAIR)]

        def store(u, rp, yp, mm, nn):
            j, q = units[u]
            rs = slice(j * cl, (j + 1) * cl)
            qs = slice(c0 + q * PAIR, c0 + (q + 1) * PAIR)
            rp_ref[rs, qs] = rp.astype(BF16)
            yp_ref[rs, qs] = yp.astype(BF16)
            m_ref[rs, qs] = mm.astype(BF16)
            n_ref[rs, qs] = nn.astype(BF16)
        return store

    lru = (None for h in range(LRU_HEADS)
           for _ in _lru_head(h, first, u_ref, gate_ref, halo_ref, cw_ref, cb_ref, wa_ref, wx_ref,
                              ba_ref, bx_ref, lam_ref, ya_ref, carry_ref))
    chains = []
    for c0 in range(0, width, gw):
        ops = []
        for _ in prologue(c0, ops):
            for ch in chains:
                next(ch, None)
        chains.append(_chunk_chain(ops, make_store(c0)))
    live = list(chains)
    while live:
        live = [ch for ch in live if next(ch, StopIteration) is not StopIteration]
        next(lru, None)
    for _ in lru:
        pass

    for src, dst in zip(cast_in, cast_out):
        dst[...] = src[...].astype(BF16)


def _rwkv_a(p, p_lora, mu_rkv, mu_lora, w0, a0, k_k, k_a, r_k, w2p, a2p, g2p,
            conv_w, conv_b, wa, wx, ba, bx, lam, bsz, seq, rkv_col0, cast_ws=()):
    dl = conv_w.shape[1]
    lvec = lambda t: t.reshape(1, dl)
    lrow = pl.BlockSpec((1, dl), lambda b, i, q: (0, 0))

    def lru_tile(cb):
        return pl.BlockSpec((CHUNKS_PER_STEP * CHUNK, dl), lambda b, i, q: (b * nc + i, cb))
    cl = CHUNKS_PER_STEP * CHUNK
    width = HEADS_PER_STEP * HEAD
    dr = w0.shape[1]
    ngroups = dr // width
    assert ngroups == 1, "the LRU ride-along expects one grid step per row tile"
    nc = seq // cl
    lw_ = mu_lora.shape[1]
    cb0 = rkv_col0 // width
    rows8 = cl // SUBLANE
    rt, ct = _iota2((cl, cl))
    tri = jnp.where(jnp.logical_and(rt >= ct, (rt // CHUNK) == (ct // CHUNK)), 1.0, 0.0).astype(BF16)
    ones_h = _head_ones(ONES_WIDTH)
    const = lambda arr: pl.BlockSpec(arr.shape, lambda b, i, q: (0, 0))
    lora_w = [t for wgt in (w2p, a2p, g2p) for t in _split(wgt)]

    def tile(cb_off):
        return pl.BlockSpec((cl, width), lambda b, i, q: (b * nc + i, cb0 + cb_off + q))

    def halo(cb_off):
        return pl.BlockSpec(
            (SUBLANE, width),
            lambda b, i, q: (jnp.maximum((b * nc + i) * rows8 - 1, 0), cb0 + cb_off + q))

    def prow(off=0):
        return pl.BlockSpec((1, width), lambda b, i, q: (0, off + q))

    out_tile = pl.BlockSpec((cl, width), lambda b, i, q: (b * nc + i, q))
    out_mat = pl.BlockSpec((CHUNKS_PER_STEP * HEAD, width), lambda b, i, q: (b * nc + i, q))
    act = jax.ShapeDtypeStruct((bsz * seq, dr), BF16)
    mat = jax.ShapeDtypeStruct((bsz * (seq // CHUNK) * HEAD, dr), BF16)

    nsteps = bsz * nc * ngroups
    cast_specs = []
    for wgt in cast_ws:
        hold = 1
        while (wgt.shape[0] * hold) % (nsteps * BF16_SUBLANE) != 0:
            hold *= 2
        blk = (wgt.shape[0] * hold // nsteps, wgt.shape[1])
        cast_specs.append(pl.BlockSpec(
            blk, lambda b, i, q, hold=hold: (((b * nc + i) * ngroups + q) // hold, 0)))
    cast_shapes = [jax.ShapeDtypeStruct(wgt.shape, BF16) for wgt in cast_ws]

    return pl.pallas_call(
        _rwkv_a_body,
        grid=(bsz, nc, ngroups),
        in_specs=[tile(0), tile(ngroups), tile(2 * ngroups),
                  pl.BlockSpec((cl, lw_), lambda b, i, q: (b * nc + i, 0)),
                  halo(0), halo(ngroups), halo(2 * ngroups),
                  pl.BlockSpec((SUBLANE, lw_),
                               lambda b, i, q: (jnp.maximum((b * nc + i) * rows8 - 1, 0), 0)),
                  prow(0), prow(ngroups), prow(2 * ngroups),
                  pl.BlockSpec((1, lw_), lambda b, i, q: (0, 0)),
                  prow(), prow(), prow(), prow(), prow()]
                 + [pl.BlockSpec((t.shape[0], width), lambda b, i, q: (0, q)) for t in lora_w]
                 + [const(ones_h), const(tri)]
                 + [lru_tile(0), lru_tile(1),
                    pl.BlockSpec((SUBLANE, dl),
                                 lambda b, i, q: (jnp.maximum((b * nc + i) * rows8 - 1, 0), 0)),
                    const(conv_w), lrow, pl.BlockSpec(wa.shape, lambda b, i, q: (0, 0, 0)),
                    pl.BlockSpec(wx.shape, lambda b, i, q: (0, 0, 0)), lrow, lrow, lrow]
                 + cast_specs,
        out_specs=[out_tile, out_tile, out_mat, out_mat, out_tile, out_tile, lru_tile(0)] + cast_specs,
        out_shape=[act, act, mat, mat, act, act, jax.ShapeDtypeStruct((bsz * seq, dl), BF16)]
                  + cast_shapes,
        scratch_shapes=[pltpu.VMEM((1, dl), F32)],
        compiler_params=_params("arbitrary", "arbitrary", "arbitrary"),
        name="rwkv_a",
    )(p, p, p, p_lora, p, p, p, p_lora, mu_rkv, mu_rkv, mu_rkv, mu_lora, w0, a0, k_k, k_a, r_k,
      *lora_w, ones_h, tri,
      p, p, p, conv_w, lvec(conv_b), wa, wx, lvec(ba), lvec(bx), lvec(lam), *cast_ws)


def _rwkv_b_body(rp_ref, yp_ref, m_ref, n_ref, bonus_ref, g_ref, lng_ref, lnb_ref, ones_ref,
                 o_ref, state_ref):
    @pl.when(pl.program_id(1) == 0)
    def _():
        state_ref[...] = jnp.zeros_like(state_ref)

    npairs = state_ref.shape[0]
    pairs = range(npairs)
    ps = [slice(q * PAIR, (q + 1) * PAIR) for q in pairs]
    left = lax.broadcasted_iota(jnp.int32, (HEAD, PAIR), 1) < HEAD
    ones_h = ones_ref[...]
    inv_n = 1.0 / HEAD
    state = [state_ref[q] for q in pairs]
    for j in range(rp_ref.shape[0] // CHUNK):
        rs = slice(j * CHUNK, (j + 1) * CHUNK)
        ks = slice(j * HEAD, (j + 1) * HEAD)
        g0 = [_pair_diag(state[q], left) for q in pairs]
        ys = [_dg(rp_ref[rs, ps[q]].astype(BF16), g0[q], _NN) + yp_ref[rs, ps[q]] for q in pairs]
        state = [_dg(m_ref[ks, ps[q]].astype(BF16), g0[q], _NN) + n_ref[ks, ps[q]] for q in pairs]
        y = jnp.concatenate(ys, axis=1)
        yc = y - _head_sums(y, ones_h) * inv_n
        var = _head_sums(yc * yc, ones_h) * inv_n
        yn = yc * lax.rsqrt(var + GN_EPS) * lng_ref[...] + lnb_ref[...]
        o_ref[rs, :] = ((yn + bonus_ref[rs, :]) * g_ref[rs, :]).astype(BF16)
    for q in pairs:
        state_ref[q] = state[q]


def _rwkv_b(rp, yp, mc, nm, bonus, g, ln_g, ln_b, bsz, seq):
    cl = RWKV_B_CHUNKS * CHUNK
    dr = rp.shape[1]
    nc = seq // cl
    tile = pl.BlockSpec((cl, dr), lambda b, i: (b * nc + i, 0))
    mat = pl.BlockSpec((RWKV_B_CHUNKS * HEAD, dr), lambda b, i: (b * nc + i, 0))
    prow = pl.BlockSpec((1, dr), lambda b, i: (0, 0))
    ones_h = _head_ones(ONES_WIDTH)
    return pl.pallas_call(
        _rwkv_b_body,
        grid=(bsz, nc),
        in_specs=[tile, tile, mat, mat, tile, tile, prow, prow,
                  pl.BlockSpec(ones_h.shape, lambda b, i: (0, 0))],
        out_specs=tile,
        out_shape=jax.ShapeDtypeStruct((bsz * seq, dr), BF16),
        scratch_shapes=[pltpu.VMEM((dr // PAIR, HEAD, PAIR), F32)],
        compiler_params=_params("parallel", "arbitrary"),
        name="rwkv_b",
    )(rp, yp, mc, nm, bonus, g, ln_g, ln_b, ones_h)


def _mm_out_body(ya_ref, yb_ref, x_ref, gm_ref, w_ref, g_ref, sh_ref, sc_ref, o_ref, h_ref, *, sub):
    da = ya_ref.shape[1]
    for r0 in range(0, x_ref.shape[0], sub):
        rs = slice(r0, r0 + sub)
        mix = (jnp.dot(ya_ref[rs, :].astype(BF16), w_ref[:da, :], preferred_element_type=F32)
               + jnp.dot(yb_ref[rs, :].astype(BF16), w_ref[da:, :], preferred_element_type=F32))
        x1 = x_ref[rs, :] + gm_ref[0] * mix
        o_ref[rs, :] = x1
        h_ref[rs, :] = _norm_mod(x1, g_ref[...], sh_ref[0], sc_ref[0]).astype(BF16)


def _mm_out(ya, yb, x2, gm, w, g, sh, sc, seq, tm=512, sub=256):
    m, d = x2.shape
    per_b = seq // tm
    brow = pl.BlockSpec((1, 1, d), lambda i: (i // per_b, 0, 0))
    tile = pl.BlockSpec((tm, d), lambda i: (i, 0))
    return pl.pallas_call(
        functools.partial(_mm_out_body, sub=sub),
        grid=(m // tm,),
        in_specs=[pl.BlockSpec((tm, ya.shape[1]), lambda i: (i, 0)),
                  pl.BlockSpec((tm, yb.shape[1]), lambda i: (i, 0)),
                  tile, brow,
                  pl.BlockSpec(w.shape, lambda i: (0, 0)),
                  pl.BlockSpec((1, d), lambda i: (0, 0)), brow, brow],
        out_specs=[tile, tile],
        out_shape=[jax.ShapeDtypeStruct((m, d), F32), jax.ShapeDtypeStruct((m, d), BF16)],
        compiler_params=_params("parallel"),
        name="mm_out",
    )(ya, yb, x2, gm, w, g, sh, sc)


def _ffn_body(x_ref, h_ref, gf_ref, wg_ref, wu_ref, wd_ref, fg_ref, o_ref, acc_ref):
    f = pl.program_id(1)

    @pl.when(f == 0)
    def _():
        acc_ref[...] = jnp.zeros_like(acc_ref)

    h = h_ref[...]
    gate = jnp.dot(h, wg_ref[...], preferred_element_type=F32)
    up = jnp.dot(h, wu_ref[...], preferred_element_type=F32)
    act = (gate * jax.nn.sigmoid(gate) * up).astype(BF16)
    acc_ref[...] += jnp.dot(act, wd_ref[...], preferred_element_type=F32)

    @pl.when(f == pl.num_programs(1) - 1)
    def _():
        y = x_ref[...] + gf_ref[0] * acc_ref[...]
        o_ref[...] = (y * lax.rsqrt(jnp.mean(y * y, axis=-1, keepdims=True) + RMS_EPS)
                      * fg_ref[...])


def _ffn(x1, h2, gf, w_gu, w_down, fg, seq, tm=512, tf=512):
    m, d = x1.shape
    dff = w_down.shape[0]
    nf = dff // tf
    assert seq % tm == 0 and dff % tf == 0, "row tiles must not straddle sequences"
    per_b = seq // tm
    tile = pl.BlockSpec((tm, d), lambda i, f: (i, 0))
    prow = pl.BlockSpec((1, d), lambda i, f: (0, 0))
    return pl.pallas_call(
        _ffn_body,
        grid=(m // tm, nf),
        in_specs=[tile, tile,
                  pl.BlockSpec((1, 1, d), lambda i, f: (i // per_b, 0, 0)),
                  pl.BlockSpec((d, tf), lambda i, f: (0, f)),
                  pl.BlockSpec((d, tf), lambda i, f: (0, nf + f)),
                  pl.BlockSpec((tf, d), lambda i, f: (f, 0)),
                  prow],
        out_specs=tile,
        out_shape=jax.ShapeDtypeStruct((m, d), F32),
        scratch_shapes=[pltpu.VMEM((tm, d), F32)],
        compiler_params=_params("parallel", "arbitrary"),
        name="ffn",
    )(x1, h2, gf, w_gu, w_gu, w_down, fg)


def _pad_cols(w, n):
    return jnp.pad(w, ((0, 0), (0, n - w.shape[1])))


def _pad_rows(w, n):
    return jnp.pad(w, ((0, n - w.shape[0]), (0, 0)))


def kernel(x, c, w_ada, b_ada, norm_mix_g, w_in, conv_w, conv_b, lru_wa, lru_ba, lru_wx, lru_bx, lru_lambda, rwkv_mu, rwkv_w0, rwkv_w2, rwkv_a0, rwkv_a2, rwkv_g2, rwkv_k_k, rwkv_k_a, rwkv_r_k, rwkv_ln_g, rwkv_ln_b, w_out, norm_ffn_g, w_gu, w_down, final_norm_g):
    bsz, seq, d = x.shape
    depth = w_ada.shape[0]
    dl = conv_w.shape[2]
    dr = rwkv_w0.shape[1]
    w_lora, a_lora, g_lora = rwkv_w2.shape[1], rwkv_a2.shape[1], rwkv_g2.shape[1]
    wpad, apad = LANE, LANE
    gpad = -(-g_lora // LANE) * LANE
    rkv_col0 = 2 * dl
    lora0 = rkv_col0 + 3 * dr

    x2 = x.reshape(bsz * seq, d)
    for l in range(depth):
        mod = _mod(c, w_ada[l], b_ada[l].reshape(1, -1))
        sh_m, sc_m, g_m, sh_f, sc_f, g_f = [t.reshape(bsz, 1, d) for t in jnp.split(mod, 6, axis=-1)]

        wi = jnp.swapaxes(w_in[l], 0, 1)
        o1, o2 = lora0 + w_lora, lora0 + w_lora + a_lora
        w_lora_p = jnp.concatenate(
            [_pad_rows(wi[lora0:o1], wpad), _pad_rows(wi[o1:o2], apad),
             _pad_rows(wi[o2:], gpad)], axis=0)
        mu = rwkv_mu[l].reshape(1, -1)
        mu_rkv = mu[:, :3 * dr]
        mu_lora = jnp.concatenate(
            [_pad_cols(mu[:, 3 * dr:3 * dr + w_lora], wpad),
             _pad_cols(mu[:, 3 * dr + w_lora:3 * dr + w_lora + a_lora], apad),
             _pad_cols(mu[:, 3 * dr + w_lora + a_lora:], gpad)], axis=1)
        w2p = _pad_rows(rwkv_w2[l], wpad)
        a2p = _pad_rows(rwkv_a2[l], apad)
        g2p = _pad_rows(rwkv_g2[l], gpad)

        h = _norm(x2, norm_mix_g[l].reshape(1, d), sh_m, sc_m, seq)
        tn_in = dl
        p = _mm_in(h, wi, lora0, tn=tn_in, gelu_tile=dl // tn_in)
        p_lora = _mm_in(h, w_lora_p, w_lora_p.shape[0], tn=w_lora_p.shape[0], name="mm_lora")

        rowv = lambda t: t.reshape(1, dr)
        rp, yp, mc, nm, bonus, gg, y_a, w_out_b, w_gu_b, w_down_b = _rwkv_a(
            p, p_lora, mu_rkv, mu_lora, rowv(rwkv_w0[l]), rowv(rwkv_a0[l]), rowv(rwkv_k_k[l]),
            rowv(rwkv_k_a[l]), rowv(rwkv_r_k[l]), w2p, a2p, g2p,
            conv_w[l], conv_b[l], lru_wa[l].astype(BF16), lru_wx[l].astype(BF16),
            lru_ba[l], lru_bx[l], lru_lambda[l], bsz, seq, rkv_col0,
            cast_ws=(w_out[l], w_gu[l], w_down[l]))
        y_b = _rwkv_b(rp, yp, mc, nm, bonus, gg, rowv(rwkv_ln_g[l]), rowv(rwkv_ln_b[l]), bsz, seq)

        x2, h2 = _mm_out(y_a, y_b, x2, g_m, w_out_b, norm_ffn_g[l].reshape(1, d), sh_f, sc_f, seq)

        last = l == depth - 1
        fg = final_norm_g.reshape(1, d) if last else None
        assert last, "only the final layer carries the closing RMSNorm"
        x2 = _ffn(x2, h2, g_f, w_gu_b, w_down_b, fg, seq)
    return x2.reshape(bsz, seq, d)
```

```python
import functools
import math

import jax
import jax.numpy as jnp
from jax import lax
from jax.experimental import pallas as pl
from jax.experimental.pallas import tpu as pltpu

F32 = jnp.float32
BF16 = jnp.bfloat16

LRU_HEADS = 4
CONV_WIDTH = 4
LRU_C = 8.0
HEAD = 64
CHUNK = 64
PAIR = 2 * HEAD
HEADS_PER_STEP = 16
ONES_WIDTH = 256
CHUNKS_PER_STEP = 4
RWKV_B_CHUNKS = 8
RMS_EPS = 1e-6
GN_EPS = 64e-5
L2_EPS = 1e-12
DECAY_SCALE = -math.exp(-0.5)
LANE = 128
SUBLANE = 8
BF16_SUBLANE = 16
VMEM_LIMIT = 56 * 1024 * 1024


def _params(*sem):
    return pltpu.CompilerParams(dimension_semantics=sem, vmem_limit_bytes=VMEM_LIMIT)


_NN = (((1,), (0,)), ((), ()))
_NT = (((1,), (1,)), ((), ()))
_TN = (((0,), (0,)), ((), ()))


def _dg(a, b, dims):
    return lax.dot_general(a, b, dims, preferred_element_type=F32)


def _split(x):
    hi = x.astype(BF16)
    lo = (x - hi.astype(F32)).astype(BF16)
    return hi, lo


def _mm3(a, b, dims=_NN):
    ah, al = _split(a)
    bh, bl = _split(b)
    return _dg(ah, bh, dims) + (_dg(ah, bl, dims) + _dg(al, bh, dims))


def _mm3_presplit(a, bh, bl):
    ah, al = _split(a)
    return _dg(ah, bh, _NN) + (_dg(ah, bl, _NN) + _dg(al, bh, _NN))


def _mm2_exact_rhs(a, b_bf16):
    ah, al = _split(a)
    return _dg(ah, b_bf16, _NN) + _dg(al, b_bf16, _NN)


def _head_sums(x, ones_h):
    n = ones_h.shape[0]
    return jnp.concatenate([_mm2_exact_rhs(x[:, c:c + n], ones_h) for c in range(0, x.shape[1], n)],
                           axis=1)


def _mm2_exact_lhs(a_bf16, b):
    bh, bl = _split(b)
    return _dg(a_bf16, bh, _NN) + _dg(a_bf16, bl, _NN)


def _softplus(x):
    return jnp.maximum(x, 0.0) + jnp.log1p(jnp.exp(-jnp.abs(x)))


def _iota2(shape):
    return (lax.broadcasted_iota(jnp.int32, shape, 0),
            lax.broadcasted_iota(jnp.int32, shape, 1))


def _head_ones(n):
    r, c = _iota2((n, n))
    return jnp.where((r // HEAD) == (c // HEAD), 1.0, 0.0).astype(BF16)


def _mod_body(c_ref, w_ref, b_ref, o_ref):
    c = c_ref[...]
    ca = c * jax.nn.sigmoid(c)
    o_ref[...] = _mm3(ca, w_ref[...]) + b_ref[...]


def _mod(c, w, b, tn=1024):
    bsz, d = c.shape
    n = w.shape[1]
    return pl.pallas_call(
        _mod_body,
        grid=(n // tn,),
        in_specs=[pl.BlockSpec((bsz, d), lambda j: (0, 0)),
                  pl.BlockSpec((d, tn), lambda j: (0, j)),
                  pl.BlockSpec((1, tn), lambda j: (0, j))],
        out_specs=pl.BlockSpec((bsz, tn), lambda j: (0, j)),
        out_shape=jax.ShapeDtypeStruct((bsz, n), F32),
        compiler_params=_params("parallel"),
        name="mod",
    )(c, w, b)


def _norm_mod(x, g, sh, sc):
    y = x * lax.rsqrt(jnp.mean(x * x, axis=-1, keepdims=True) + RMS_EPS) * g
    return y * (1.0 + sc) + sh


def _norm_body(x_ref, g_ref, sh_ref, sc_ref, o_ref):
    o_ref[...] = _norm_mod(x_ref[...], g_ref[...], sh_ref[0], sc_ref[0]).astype(BF16)


def _norm(x2, g, sh, sc, seq, tm=512):
    m, d = x2.shape
    per_b = seq // tm
    return pl.pallas_call(
        _norm_body,
        grid=(m // tm,),
        in_specs=[pl.BlockSpec((tm, d), lambda i: (i, 0)),
                  pl.BlockSpec((1, d), lambda i: (0, 0)),
                  pl.BlockSpec((1, 1, d), lambda i: (i // per_b, 0, 0)),
                  pl.BlockSpec((1, 1, d), lambda i: (i // per_b, 0, 0))],
        out_specs=pl.BlockSpec((tm, d), lambda i: (i, 0)),
        out_shape=jax.ShapeDtypeStruct((m, d), BF16),
        compiler_params=_params("parallel"),
        name="norm_mix",
    )(x2, g, sh, sc)


def _mm_in_body(h_ref, w_ref, o_ref, wb_ref, *, gelu_tile):
    @pl.when(pl.program_id(1) == 0)
    def _():
        wb_ref[...] = w_ref[...].astype(BF16)

    if gelu_tile is None:
        o_ref[...] = _dg(h_ref[...], wb_ref[...], _NT)
    else:
        @pl.when(pl.program_id(0) == gelu_tile)
        def _():
            o_ref[...] = jax.nn.gelu(_dg(h_ref[...], wb_ref[...], _NT))

        @pl.when(pl.program_id(0) != gelu_tile)
        def _():
            o_ref[...] = _dg(h_ref[...], wb_ref[...], _NT)


def _mm_in(h, wt, ncols, tm=1024, tn=1024, name="mm_in", gelu_tile=None):
    m, d = h.shape
    return pl.pallas_call(
        functools.partial(_mm_in_body, gelu_tile=gelu_tile),
        grid=(ncols // tn, m // tm),
        in_specs=[pl.BlockSpec((tm, d), lambda j, i: (i, 0)),
                  pl.BlockSpec((tn, d), lambda j, i: (j, 0))],
        out_specs=pl.BlockSpec((tm, tn), lambda j, i: (i, j)),
        out_shape=jax.ShapeDtypeStruct((m, ncols), F32),
        scratch_shapes=[pltpu.VMEM((tn, d), BF16)],
        compiler_params=_params("parallel", "arbitrary"),
        name=name,
    )(h, wt)


def _lru_head(h, first, u_ref, gate_ref, halo_ref, cw_ref, cb_ref, wa_ref, wx_ref, ba_ref, bx_ref,
              lam_ref, o_ref, carry_ref):
    tt = u_ref.shape[0]
    hd = u_ref.shape[1] // LRU_HEADS
    cs = slice(h * hd, (h + 1) * hd)
    p = u_ref[:, cs]
    halo = jnp.where(first, 0.0, halo_ref[:, cs])
    ext = jnp.concatenate([halo, p], axis=0)
    cw = cw_ref[:, cs]
    u = cb_ref[:, cs] + p * cw[CONV_WIDTH - 1:CONV_WIDTH, :]
    for j in range(1, CONV_WIDTH):
        shifted = pltpu.roll(ext, j, 0)[SUBLANE:, :]
        u = u + shifted * cw[CONV_WIDTH - 1 - j:CONV_WIDTH - j, :]
    ub = u.astype(BF16)
    ra = jnp.dot(ub, wa_ref[h], preferred_element_type=F32)
    rx = jnp.dot(ub, wx_ref[h], preferred_element_type=F32)
    yield
    r = jax.nn.sigmoid(ra + ba_ref[:, cs])
    ig = jax.nn.sigmoid(rx + bx_ref[:, cs])
    a = jnp.exp(r * ((-LRU_C) * _softplus(-lam_ref[:, cs])))
    mult = jnp.sqrt(1.0 - a * a)
    row = lax.broadcasted_iota(jnp.int32, (tt, hd), 0)
    mult = jnp.where(jnp.logical_and(first, row == 0), 1.0, mult)
    b = mult * (ig * u)

    groups = tt // SUBLANE
    a3 = a.reshape(groups, SUBLANE, hd)
    b3 = b.reshape(groups, SUBLANE, hd)
    sub = lax.broadcasted_iota(jnp.int32, (groups, SUBLANE, hd), 1)
    s = 1
    while s < SUBLANE:
        keep = sub >= s
        a_s = jnp.where(keep, pltpu.roll(a3, s, 1), 1.0)
        b_s = jnp.where(keep, pltpu.roll(b3, s, 1), 0.0)
        b3 = a3 * b_s + b3
        a3 = a3 * a_s
        s *= 2
    yield
    gate = gate_ref[:, cs]
    carry = carry_ref[:, cs]
    outs = []
    for g in range(groups):
        hh = b3[g] + a3[g] * carry
        carry = hh[SUBLANE - 1:SUBLANE, :]
        outs.append(hh * gate[g * SUBLANE:(g + 1) * SUBLANE, :])
    per = BF16_SUBLANE // SUBLANE
    for t0 in range(0, groups, per):
        o_ref[t0 * SUBLANE:(t0 + per) * SUBLANE, cs] = jnp.concatenate(
            outs[t0:t0 + per], axis=0).astype(BF16)
    carry_ref[:, cs] = carry
    yield


def _token_shift(x, halo, mu, first, row):
    prev = jnp.where(first, 0.0, halo[SUBLANE - 1:SUBLANE, :])
    xs = jnp.where(row == 0, prev, pltpu.roll(x, 1, 0))
    return x + (xs - x) * mu


def _mm1(a, b, dims=_NN):
    return _dg(a.astype(BF16), b.astype(BF16), dims)


def _pair_diag(y, left):
    return jnp.concatenate([jnp.where(left, y, 0.0), jnp.where(left, 0.0, y)], axis=0).astype(BF16)


def _pair_mm(x, y, left):
    return _dg(x.astype(BF16), _pair_diag(y, left), _NN)


def _chunk_chain(ops, store):
    ab_, bb_, kb_, rb_, v_, bt_, kt_, pe_ = ops
    rc, lane = _iota2((CHUNK, PAIR))
    cc = lane % HEAD
    left = lane < HEAD
    strict = rc > cc
    incl = rc >= cc
    diag = rc == cc
    ar16 = [jnp.concatenate([x, y], axis=0).astype(BF16) for x, y in zip(ab_, rb_)]
    bd_b = [_pair_diag(x, left) for x in bb_]
    bd_k = [_pair_diag(x, left) for x in kb_]
    bd_v = [_pair_diag(x, left) for x in v_]
    arb = [_dg(x, y, _NT) for x, y in zip(ar16, bd_b)]
    ark = [_dg(x, y, _NT) for x, y in zip(ar16, bd_k)]
    a_ab = [jnp.where(strict, x[:CHUNK], 0.0) for x in arb]
    a_rb = [jnp.where(incl, x[CHUNK:], 0.0).astype(BF16) for x in arb]
    a_akrk = [jnp.concatenate([jnp.where(strict, x[:CHUNK], 0.0), jnp.where(incl, x[CHUNK:], 0.0)],
                              axis=0).astype(BF16) for x in ark]
    yield
    base = 8
    d = [jnp.where((rc // base) == (cc // base), a, 0.0) for a in a_ab]
    d2 = [_pair_mm(t, t, left) for t in d]
    akrkv = [_dg(x, y, _NN) for x, y in zip(a_akrk, bd_v)]
    akv = [x[:CHUNK] for x in akrkv]
    rkv = [x[CHUNK:] for x in akrkv]
    x = [jnp.where(diag, 1.0, 0.0) + t for t in d]
    yield
    x = [xi + _pair_mm(t2, xi, left) for xi, t2 in zip(x, d2)]
    d4 = [_pair_mm(t2, t2, left) for t2 in d2]
    yield
    x = [xi + _pair_mm(t4, xi, left) for xi, t4 in zip(x, d4)]
    yield
    size = base
    while size < CHUNK:
        off = jnp.logical_and((rc // (2 * size)) == (cc // (2 * size)),
                              (rc // size) != (cc // size))
        o = [jnp.where(off, a, 0.0) for a in a_ab]
        ox = [_pair_mm(oi, xi, left) for oi, xi in zip(o, x)]
        yield
        x = [xi + _pair_mm(xi, oxi, left) for xi, oxi in zip(x, ox)]
        yield
        size *= 2
    t = [xi.astype(BF16) for xi in x]
    wu = [_dg(ti, jnp.concatenate([_pair_diag(y, left), _pair_diag(z, left)], axis=1), _NN)
          for ti, y, z in zip(t, ab_, akv)]
    kv = [_dg(xi.astype(BF16), y.astype(BF16), _TN) for xi, y in zip(kt_, v_)]
    yield
    ry = [_dg(xi, jnp.concatenate([_pair_diag(y[:, :PAIR], left), _pair_diag(y[:, PAIR:], left)], axis=1), _NN)
          for xi, y in zip(a_rb, wu)]
    mn = [_dg(xi.astype(BF16), y.astype(BF16), _TN) for xi, y in zip(bt_, wu)]
    yield

    def head_blocks(z):
        return jnp.where(left, z[:HEAD, :], z[HEAD:, :])

    for u in range(len(ab_)):
        store(u,
              rb_[u] + ry[u][:, :PAIR],
              ry[u][:, PAIR:] + rkv[u],
              jnp.where(diag, pe_[u], 0.0) + head_blocks(mn[u][:, :PAIR]),
              head_blocks(mn[u][:, PAIR:]) + head_blocks(kv[u]))


def _rwkv_a_body(r_ref, k_ref, v_ref, l_ref, rh_ref, kh_ref, vh_ref, lh_ref,
                 mur_ref, muk_ref, muv_ref, mul_ref, w0_ref, a0_ref, kkw_ref, kaw_ref, rkw_ref,
                 w2h_ref, w2l_ref, a2h_ref, a2l_ref, g2h_ref, g2l_ref, ones_ref, tri_ref,
                 u_ref, gate_ref, halo_ref, cw_ref, cb_ref, wa_ref, wx_ref, ba_ref, bx_ref, lam_ref,
                 *rest):
    ncast = (len(rest) - 8) // 2
    cast_in = rest[:ncast]
    rp_ref, yp_ref, m_ref, n_ref, bonus_ref, g_ref, ya_ref = rest[ncast:ncast + 7]
    cast_out = rest[ncast + 7:-1]
    carry_ref = rest[-1]
    first = pl.program_id(1) == 0

    @pl.when(first)
    def _():
        carry_ref[...] = jnp.zeros_like(carry_ref)
    cl = CHUNK
    rows = CHUNKS_PER_STEP * cl
    width = HEADS_PER_STEP * HEAD
    gw = ones_ref.shape[0]
    row_g = lax.broadcasted_iota(jnp.int32, (rows, gw), 0)
    row_l = lax.broadcasted_iota(jnp.int32, (rows, l_ref.shape[1]), 0)
    ones_h = ones_ref[...]

    lo = _token_shift(l_ref[...], lh_ref[...], mul_ref[...], first, row_l)
    act_w = _split(jnp.tanh(lo[:, 0:LANE]))
    act_a = _split(lo[:, LANE:2 * LANE])
    act_g = _split(jax.nn.sigmoid(lo[:, 2 * LANE:]))

    def lora(act, wh_ref, wl_ref, cs):
        (ah, al_), bh, bl = act, wh_ref[:, cs], wl_ref[:, cs]
        return _dg(ah, bh, _NN) + (_dg(ah, bl, _NN) + _dg(al_, bh, _NN))

    def prologue(c0, out):
        cs = slice(c0, c0 + gw)
        r = _token_shift(r_ref[:, cs], rh_ref[:, cs], mur_ref[:, cs], first, row_g)
        k = _token_shift(k_ref[:, cs], kh_ref[:, cs], muk_ref[:, cs], first, row_g)
        v = _token_shift(v_ref[:, cs], vh_ref[:, cs], muv_ref[:, cs], first, row_g)
        w_lin = w0_ref[:, cs] + lora(act_w, w2h_ref, w2l_ref, cs)
        a_lin = a0_ref[:, cs] + lora(act_a, a2h_ref, a2l_ref, cs)
        g_ref[:, cs] = lora(act_g, g2h_ref, g2l_ref, cs).astype(BF16)
        kk = k * kkw_ref[:, cs]
        kk_ss = _mm2_exact_rhs(kk * kk, ones_h)
        yield
        lw = DECAY_SCALE * jax.nn.sigmoid(w_lin)
        a = jax.nn.sigmoid(a_lin)
        kk = kk * lax.rsqrt(jnp.maximum(kk_ss, L2_EPS * L2_EPS))
        kp = k * (1.0 + (a - 1.0) * kaw_ref[:, cs])
        bonus_ref[:, cs] = (_mm2_exact_rhs(r * kp * rkw_ref[:, cs], ones_h) * v).astype(BF16)
        lc = _mm2_exact_lhs(tri_ref[...], lw)
        yield
        p_incl = jnp.exp(lc)
        p_excl = jnp.exp(lc - lw)
        p_inv = 1.0 / p_incl
        p_end = jnp.concatenate(
            [jnp.broadcast_to(p_incl[(j + 1) * cl - 1:(j + 1) * cl, :], (cl, gw))
             for j in range(CHUNKS_PER_STEP)], axis=0)
        abar = -(kk * p_excl)
        bbar = kk * a * p_inv
        kbar = kp * p_inv
        rbar = r * p_incl
        btil = bbar * p_end
        ktil = kbar * p_end
        units = [(j, q) for j in range(CHUNKS_PER_STEP) for q in range(gw // PAIR)]
        out.extend([x[j * cl:(j + 1) * cl, q * PAIR:(q + 1) * PAIR] for j, q in units]
                   for x in (abar, bbar, kbar, rbar, v, btil, ktil, p_end))
        yield

    def make_store(c0):
        units = [(j, q) for j in range(CHUNKS_PER_STEP) for q in range(gw // PAIR)]

        def store(u, rp, yp, mm, nn):
            j, q = units[u]
            rs = slice(j * cl, (j + 1) * cl)
            qs = slice(c0 + q * PAIR, c0 + (q + 1) * PAIR)
            rp_ref[rs, qs] = rp.astype(BF16)
            yp_ref[rs, qs] = yp.astype(BF16)
            m_ref[rs, qs] = mm.astype(BF16)
            n_ref[rs, qs] = nn.astype(BF16)
        return store

    lru = (None for h in range(LRU_HEADS)
           for _ in _lru_head(h, first, u_ref, gate_ref, halo_ref, cw_ref, cb_ref, wa_ref, wx_ref,
                              ba_ref, bx_ref, lam_ref, ya_ref, carry_ref))
    chains = []
    for c0 in range(0, width, gw):
        ops = []
        for _ in prologue(c0, ops):
            for ch in chains:
                next(ch, None)
        chains.append(_chunk_chain(ops, make_store(c0)))
    live = list(chains)
    while live:
        live = [ch for ch in live if next(ch, StopIteration) is not StopIteration]
        next(lru, None)
    for _ in lru:
        pass

    for src, dst in zip(cast_in, cast_out):
        dst[...] = src[...].astype(BF16)


def _rwkv_a(p, p_lora, mu_rkv, mu_lora, w0, a0, k_k, k_a, r_k, w2p, a2p, g2p,
            conv_w, conv_b, wa, wx, ba, bx, lam, bsz, seq, rkv_col0, cast_ws=()):
    dl = conv_w.shape[1]
    lvec = lambda t: t.reshape(1, dl)
    lrow = pl.BlockSpec((1, dl), lambda b, i, q: (0, 0))

    def lru_tile(cb):
        return pl.BlockSpec((CHUNKS_PER_STEP * CHUNK, dl), lambda b, i, q: (b * nc + i, cb))
    cl = CHUNKS_PER_STEP * CHUNK
    width = HEADS_PER_STEP * HEAD
    dr = w0.shape[1]
    ngroups = dr // width
    assert ngroups == 1, "the LRU ride-along expects one grid step per row tile"
    nc = seq // cl
    lw_ = mu_lora.shape[1]
    cb0 = rkv_col0 // width
    rows8 = cl // SUBLANE
    rt, ct = _iota2((cl, cl))
    tri = jnp.where(jnp.logical_and(rt >= ct, (rt // CHUNK) == (ct // CHUNK)), 1.0, 0.0).astype(BF16)
    ones_h = _head_ones(ONES_WIDTH)
    const = lambda arr: pl.BlockSpec(arr.shape, lambda b, i, q: (0, 0))
    lora_w = [t for wgt in (w2p, a2p, g2p) for t in _split(wgt)]

    def tile(cb_off):
        return pl.BlockSpec((cl, width), lambda b, i, q: (b * nc + i, cb0 + cb_off + q))

    def halo(cb_off):
        return pl.BlockSpec(
            (SUBLANE, width),
            lambda b, i, q: (jnp.maximum((b * nc + i) * rows8 - 1, 0), cb0 + cb_off + q))

    def prow(off=0):
        return pl.BlockSpec((1, width), lambda b, i, q: (0, off + q))

    out_tile = pl.BlockSpec((cl, width), lambda b, i, q: (b * nc + i, q))
    out_mat = pl.BlockSpec((CHUNKS_PER_STEP * HEAD, width), lambda b, i, q: (b * nc + i, q))
    act = jax.ShapeDtypeStruct((bsz * seq, dr), BF16)
    mat = jax.ShapeDtypeStruct((bsz * (seq // CHUNK) * HEAD, dr), BF16)

    nsteps = bsz * nc * ngroups
    cast_specs = []
    for wgt in cast_ws:
        hold = 1
        while (wgt.shape[0] * hold) % (nsteps * BF16_SUBLANE) != 0:
            hold *= 2
        blk = (wgt.shape[0] * hold // nsteps, wgt.shape[1])
        cast_specs.append(pl.BlockSpec(
            blk, lambda b, i, q, hold=hold: (((b * nc + i) * ngroups + q) // hold, 0)))
    cast_shapes = [jax.ShapeDtypeStruct(wgt.shape, BF16) for wgt in cast_ws]

    return pl.pallas_call(
        _rwkv_a_body,
        grid=(bsz, nc, ngroups),
        in_specs=[tile(0), tile(ngroups), tile(2 * ngroups),
                  pl.BlockSpec((cl, lw_), lambda b, i, q: (b * nc + i, 0)),
                  halo(0), halo(ngroups), halo(2 * ngroups),
                  pl.BlockSpec((SUBLANE, lw_),
                               lambda b, i, q: (jnp.maximum((b * nc + i) * rows8 - 1, 0), 0)),
                  prow(0), prow(ngroups), prow(2 * ngroups),
                  pl.BlockSpec((1, lw_), lambda b, i, q: (0, 0)),
                  prow(), prow(), prow(), prow(), prow()]
                 + [pl.BlockSpec((t.shape[0], width), lambda b, i, q: (0, q)) for t in lora_w]
                 + [const(ones_h), const(tri)]
                 + [lru_tile(0), lru_tile(1),
                    pl.BlockSpec((SUBLANE, dl),
                                 lambda b, i, q: (jnp.maximum((b * nc + i) * rows8 - 1, 0), 0)),
                    const(conv_w), lrow, pl.BlockSpec(wa.shape, lambda b, i, q: (0, 0, 0)),
                    pl.BlockSpec(wx.shape, lambda b, i, q: (0, 0, 0)), lrow, lrow, lrow]
                 + cast_specs,
        out_specs=[out_tile, out_tile, out_mat, out_mat, out_tile, out_tile, lru_tile(0)] + cast_specs,
        out_shape=[act, act, mat, mat, act, act, jax.ShapeDtypeStruct((bsz * seq, dl), BF16)]
                  + cast_shapes,
        scratch_shapes=[pltpu.VMEM((1, dl), F32)],
        compiler_params=_params("arbitrary", "arbitrary", "arbitrary"),
        name="rwkv_a",
    )(p, p, p, p_lora, p, p, p, p_lora, mu_rkv, mu_rkv, mu_rkv, mu_lora, w0, a0, k_k, k_a, r_k,
      *lora_w, ones_h, tri,
      p, p, p, conv_w, lvec(conv_b), wa, wx, lvec(ba), lvec(bx), lvec(lam), *cast_ws)


def _rwkv_b_body(rp_ref, yp_ref, m_ref, n_ref, bonus_ref, g_ref, lng_ref, lnb_ref, ones_ref,
                 o_ref, state_ref):
    @pl.when(pl.program_id(1) == 0)
    def _():
        state_ref[...] = jnp.zeros_like(state_ref)

    npairs = state_ref.shape[0]
    pairs = range(npairs)
    ps = [slice(q * PAIR, (q + 1) * PAIR) for q in pairs]
    left = lax.broadcasted_iota(jnp.int32, (HEAD, PAIR), 1) < HEAD
    ones_h = ones_ref[...]
    inv_n = 1.0 / HEAD
    state = [state_ref[q] for q in pairs]
    for j in range(rp_ref.shape[0] // CHUNK):
        rs = slice(j * CHUNK, (j + 1) * CHUNK)
        ks = slice(j * HEAD, (j + 1) * HEAD)
        g0 = [_pair_diag(state[q], left) for q in pairs]
        ys = [_dg(rp_ref[rs, ps[q]].astype(BF16), g0[q], _NN) + yp_ref[rs, ps[q]] for q in pairs]
        state = [_dg(m_ref[ks, ps[q]].astype(BF16), g0[q], _NN) + n_ref[ks, ps[q]] for q in pairs]
        y = jnp.concatenate(ys, axis=1)
        yc = y - _head_sums(y, ones_h) * inv_n
        var = _head_sums(yc * yc, ones_h) * inv_n
        yn = yc * lax.rsqrt(var + GN_EPS) * lng_ref[...] + lnb_ref[...]
        o_ref[rs, :] = ((yn + bonus_ref[rs, :]) * g_ref[rs, :]).astype(BF16)
    for q in pairs:
        state_ref[q] = state[q]


def _rwkv_b(rp, yp, mc, nm, bonus, g, ln_g, ln_b, bsz, seq):
    cl = RWKV_B_CHUNKS * CHUNK
    dr = rp.shape[1]
    nc = seq // cl
    tile = pl.BlockSpec((cl, dr), lambda b, i: (b * nc + i, 0))
    mat = pl.BlockSpec((RWKV_B_CHUNKS * HEAD, dr), lambda b, i: (b * nc + i, 0))
    prow = pl.BlockSpec((1, dr), lambda b, i: (0, 0))
    ones_h = _head_ones(ONES_WIDTH)
    return pl.pallas_call(
        _rwkv_b_body,
        grid=(bsz, nc),
        in_specs=[tile, tile, mat, mat, tile, tile, prow, prow,
                  pl.BlockSpec(ones_h.shape, lambda b, i: (0, 0))],
        out_specs=tile,
        out_shape=jax.ShapeDtypeStruct((bsz * seq, dr), BF16),
        scratch_shapes=[pltpu.VMEM((dr // PAIR, HEAD, PAIR), F32)],
        compiler_params=_params("parallel", "arbitrary"),
        name="rwkv_b",
    )(rp, yp, mc, nm, bonus, g, ln_g, ln_b, ones_h)


def _mm_out_body(ya_ref, yb_ref, x_ref, gm_ref, w_ref, g_ref, sh_ref, sc_ref, o_ref, h_ref, *, sub):
    da = ya_ref.shape[1]
    for r0 in range(0, x_ref.shape[0], sub):
        rs = slice(r0, r0 + sub)
        mix = (jnp.dot(ya_ref[rs, :].astype(BF16), w_ref[:da, :], preferred_element_type=F32)
               + jnp.dot(yb_ref[rs, :].astype(BF16), w_ref[da:, :], preferred_element_type=F32))
        x1 = x_ref[rs, :] + gm_ref[0] * mix
        o_ref[rs, :] = x1
        h_ref[rs, :] = _norm_mod(x1, g_ref[...], sh_ref[0], sc_ref[0]).astype(BF16)


def _mm_out(ya, yb, x2, gm, w, g, sh, sc, seq, tm=512, sub=256):
    m, d = x2.shape
    per_b = seq // tm
    brow = pl.BlockSpec((1, 1, d), lambda i: (i // per_b, 0, 0))
    tile = pl.BlockSpec((tm, d), lambda i: (i, 0))
    return pl.pallas_call(
        functools.partial(_mm_out_body, sub=sub),
        grid=(m // tm,),
        in_specs=[pl.BlockSpec((tm, ya.shape[1]), lambda i: (i, 0)),
                  pl.BlockSpec((tm, yb.shape[1]), lambda i: (i, 0)),
                  tile, brow,
                  pl.BlockSpec(w.shape, lambda i: (0, 0)),
                  pl.BlockSpec((1, d), lambda i: (0, 0)), brow, brow],
        out_specs=[tile, tile],
        out_shape=[jax.ShapeDtypeStruct((m, d), F32), jax.ShapeDtypeStruct((m, d), BF16)],
        compiler_params=_params("parallel"),
        name="mm_out",
    )(ya, yb, x2, gm, w, g, sh, sc)


def _ffn_body(x_ref, h_ref, gf_ref, wg_ref, wu_ref, wd_ref, fg_ref, o_ref, acc_ref):
    f = pl.program_id(1)

    @pl.when(f == 0)
    def _():
        acc_ref[...] = jnp.zeros_like(acc_ref)

    h = h_ref[...]
    gate = jnp.dot(h, wg_ref[...], preferred_element_type=F32)
    up = jnp.dot(h, wu_ref[...], preferred_element_type=F32)
    act = (gate * jax.nn.sigmoid(gate) * up).astype(BF16)
    acc_ref[...] += jnp.dot(act, wd_ref[...], preferred_element_type=F32)

    @pl.when(f == pl.num_programs(1) - 1)
    def _():
        y = x_ref[...] + gf_ref[0] * acc_ref[...]
        o_ref[...] = (y * lax.rsqrt(jnp.mean(y * y, axis=-1, keepdims=True) + RMS_EPS)
                      * fg_ref[...])


def _ffn(x1, h2, gf, w_gu, w_down, fg, seq, tm=512, tf=512):
    m, d = x1.shape
    dff = w_down.shape[0]
    nf = dff // tf
    assert seq % tm == 0 and dff % tf == 0, "row tiles must not straddle sequences"
    per_b = seq // tm
    tile = pl.BlockSpec((tm, d), lambda i, f: (i, 0))
    prow = pl.BlockSpec((1, d), lambda i, f: (0, 0))
    return pl.pallas_call(
        _ffn_body,
        grid=(m // tm, nf),
        in_specs=[tile, tile,
                  pl.BlockSpec((1, 1, d), lambda i, f: (i // per_b, 0, 0)),
                  pl.BlockSpec((d, tf), lambda i, f: (0, f)),
                  pl.BlockSpec((d, tf), lambda i, f: (0, nf + f)),
                  pl.BlockSpec((tf, d), lambda i, f: (f, 0)),
                  prow],
        out_specs=tile,
        out_shape=jax.ShapeDtypeStruct((m, d), F32),
        scratch_shapes=[pltpu.VMEM((tm, d), F32)],
        compiler_params=_params("parallel", "arbitrary"),
        name="ffn",
    )(x1, h2, gf, w_gu, w_gu, w_down, fg)


def _pad_cols(w, n):
    return jnp.pad(w, ((0, 0), (0, n - w.shape[1])))


def _pad_rows(w, n):
    return jnp.pad(w, ((0, n - w.shape[0]), (0, 0)))


def kernel(x, c, w_ada, b_ada, norm_mix_g, w_in, conv_w, conv_b, lru_wa, lru_ba, lru_wx, lru_bx, lru_lambda, rwkv_mu, rwkv_w0, rwkv_w2, rwkv_a0, rwkv_a2, rwkv_g2, rwkv_k_k, rwkv_k_a, rwkv_r_k, rwkv_ln_g, rwkv_ln_b, w_out, norm_ffn_g, w_gu, w_down, final_norm_g):
    bsz, seq, d = x.shape
    depth = w_ada.shape[0]
    dl = conv_w.shape[2]
    dr = rwkv_w0.shape[1]
    w_lora, a_lora, g_lora = rwkv_w2.shape[1], rwkv_a2.shape[1], rwkv_g2.shape[1]
    wpad, apad = LANE, LANE
    gpad = -(-g_lora // LANE) * LANE
    rkv_col0 = 2 * dl
    lora0 = rkv_col0 + 3 * dr

    x2 = x.reshape(bsz * seq, d)
    for l in range(depth):
        mod = _mod(c, w_ada[l], b_ada[l].reshape(1, -1))
        sh_m, sc_m, g_m, sh_f, sc_f, g_f = [t.reshape(bsz, 1, d) for t in jnp.split(mod, 6, axis=-1)]

        wi = jnp.swapaxes(w_in[l], 0, 1)
        o1, o2 = lora0 + w_lora, lora0 + w_lora + a_lora
        w_lora_p = jnp.concatenate(
            [_pad_rows(wi[lora0:o1], wpad), _pad_rows(wi[o1:o2], apad),
             _pad_rows(wi[o2:], gpad)], axis=0)
        mu = rwkv_mu[l].reshape(1, -1)
        mu_rkv = mu[:, :3 * dr]
        mu_lora = jnp.concatenate(
            [_pad_cols(mu[:, 3 * dr:3 * dr + w_lora], wpad),
             _pad_cols(mu[:, 3 * dr + w_lora:3 * dr + w_lora + a_lora], apad),
             _pad_cols(mu[:, 3 * dr + w_lora + a_lora:], gpad)], axis=1)
        w2p = _pad_rows(rwkv_w2[l], wpad)
        a2p = _pad_rows(rwkv_a2[l], apad)
        g2p = _pad_rows(rwkv_g2[l], gpad)

        h = _norm(x2, norm_mix_g[l].reshape(1, d), sh_m, sc_m, seq)
        tn_in = dl
        p = _mm_in(h, wi, lora0, tn=tn_in, gelu_tile=dl // tn_in)
        p_lora = _mm_in(h, w_lora_p, w_lora_p.shape[0], tn=w_lora_p.shape[0], name="mm_lora")

        rowv = lambda t: t.reshape(1, dr)
        rp, yp, mc, nm, bonus, gg, y_a, w_out_b, w_gu_b, w_down_b = _rwkv_a(
            p, p_lora, mu_rkv, mu_lora, rowv(rwkv_w0[l]), rowv(rwkv_a0[l]), rowv(rwkv_k_k[l]),
            rowv(rwkv_k_a[l]), rowv(rwkv_r_k[l]), w2p, a2p, g2p,
            conv_w[l], conv_b[l], lru_wa[l].astype(BF16), lru_wx[l].astype(BF16),
            lru_ba[l], lru_bx[l], lru_lambda[l], bsz, seq, rkv_col0,
            cast_ws=(w_out[l], w_gu[l], w_down[l]))
        y_b = _rwkv_b(rp, yp, mc, nm, bonus, gg, rowv(rwkv_ln_g[l]), rowv(rwkv_ln_b[l]), bsz, seq)

        x2, h2 = _mm_out(y_a, y_b, x2, g_m, w_out_b, norm_ffn_g[l].reshape(1, d), sh_f, sc_f, seq)

        last = l == depth - 1
        fg = final_norm_g.reshape(1, d) if last else None
        assert last, "only the final layer carries the closing RMSNorm"
        x2 = _ffn(x2, h2, g_f, w_gu_b, w_down_b, fg, seq)
    return x2.reshape(bsz, seq, d)
```

```python
import functools
import math

import jax
import jax.numpy as jnp
from jax import lax
from jax.experimental import pallas as pl
from jax.experimental.pallas import tpu as pltpu

F32 = jnp.float32
BF16 = jnp.bfloat16

LRU_HEADS = 4
CONV_WIDTH = 4
LRU_C = 8.0
HEAD = 64
CHUNK = 64
PAIR = 2 * HEAD
HEADS_PER_STEP = 16
ONES_WIDTH = 256
ONES_WIDTH_B = 128
CHUNKS_PER_STEP = 4
RWKV_B_CHUNKS = 8
RMS_EPS = 1e-6
GN_EPS = 64e-5
L2_EPS = 1e-12
DECAY_SCALE = -math.exp(-0.5)
LANE = 128
SUBLANE = 8
BF16_SUBLANE = 16
VMEM_LIMIT = 56 * 1024 * 1024


def _params(*sem):
    return pltpu.CompilerParams(dimension_semantics=sem, vmem_limit_bytes=VMEM_LIMIT)


_NN = (((1,), (0,)), ((), ()))
_NT = (((1,), (1,)), ((), ()))
_TN = (((0,), (0,)), ((), ()))


def _dg(a, b, dims):
    return lax.dot_general(a, b, dims, preferred_element_type=F32)


def _split(x):
    hi = x.astype(BF16)
    lo = (x - hi.astype(F32)).astype(BF16)
    return hi, lo


def _mm3(a, b, dims=_NN):
    ah, al = _split(a)
    bh, bl = _split(b)
    return _dg(ah, bh, dims) + (_dg(ah, bl, dims) + _dg(al, bh, dims))


def _head_sums(x, ones_h):
    n = ones_h.shape[0]
    xb = x.astype(BF16)
    return jnp.concatenate([_dg(xb[:, c:c + n], ones_h, _NN) for c in range(0, x.shape[1], n)],
                           axis=1)


def _mm2_exact_lhs(a_bf16, b):
    bh, bl = _split(b)
    return _dg(a_bf16, bh, _NN) + _dg(a_bf16, bl, _NN)


def _softplus(x):
    return jnp.maximum(x, 0.0) + jnp.log1p(jnp.exp(-jnp.abs(x)))


def _iota2(shape):
    return (lax.broadcasted_iota(jnp.int32, shape, 0),
            lax.broadcasted_iota(jnp.int32, shape, 1))


def _head_ones(n):
    r, c = _iota2((n, n))
    return jnp.where((r // HEAD) == (c // HEAD), 1.0, 0.0).astype(BF16)


def _mod_body(c_ref, w_ref, b_ref, o_ref):
    c = c_ref[...]
    ca = c * jax.nn.sigmoid(c)
    o_ref[...] = _mm3(ca, w_ref[...]) + b_ref[...]


def _mod(c, w, b, tn=1024):
    bsz, d = c.shape
    n = w.shape[1]
    return pl.pallas_call(
        _mod_body,
        grid=(n // tn,),
        in_specs=[pl.BlockSpec((bsz, d), lambda j: (0, 0)),
                  pl.BlockSpec((d, tn), lambda j: (0, j)),
                  pl.BlockSpec((1, tn), lambda j: (0, j))],
        out_specs=pl.BlockSpec((bsz, tn), lambda j: (0, j)),
        out_shape=jax.ShapeDtypeStruct((bsz, n), F32),
        compiler_params=_params("parallel"),
        name="mod",
    )(c, w, b)


def _norm_mod(x, g, sh, sc):
    y = x * lax.rsqrt(jnp.mean(x * x, axis=-1, keepdims=True) + RMS_EPS) * g
    return y * (1.0 + sc) + sh


def _norm_body(x_ref, g_ref, sh_ref, sc_ref, o_ref):
    o_ref[...] = _norm_mod(x_ref[...], g_ref[...], sh_ref[0], sc_ref[0]).astype(BF16)


def _norm(x2, g, sh, sc, seq, tm=512):
    m, d = x2.shape
    per_b = seq // tm
    return pl.pallas_call(
        _norm_body,
        grid=(m // tm,),
        in_specs=[pl.BlockSpec((tm, d), lambda i: (i, 0)),
                  pl.BlockSpec((1, d), lambda i: (0, 0)),
                  pl.BlockSpec((1, 1, d), lambda i: (i // per_b, 0, 0)),
                  pl.BlockSpec((1, 1, d), lambda i: (i // per_b, 0, 0))],
        out_specs=pl.BlockSpec((tm, d), lambda i: (i, 0)),
        out_shape=jax.ShapeDtypeStruct((m, d), BF16),
        compiler_params=_params("parallel"),
        name="norm_mix",
    )(x2, g, sh, sc)


def _mm_in_body(h_ref, w_ref, o_ref, wb_ref, *, gelu_tile):
    @pl.when(pl.program_id(1) == 0)
    def _():
        wb_ref[...] = w_ref[...].astype(BF16)

    if gelu_tile is None:
        o_ref[...] = _dg(h_ref[...], wb_ref[...], _NT)
    else:
        @pl.when(pl.program_id(0) == gelu_tile)
        def _():
            o_ref[...] = jax.nn.gelu(_dg(h_ref[...], wb_ref[...], _NT))

        @pl.when(pl.program_id(0) != gelu_tile)
        def _():
            o_ref[...] = _dg(h_ref[...], wb_ref[...], _NT)


def _mm_in(h, wt, ncols, tm=1024, tn=1024, name="mm_in", gelu_tile=None):
    m, d = h.shape
    return pl.pallas_call(
        functools.partial(_mm_in_body, gelu_tile=gelu_tile),
        grid=(ncols // tn, m // tm),
        in_specs=[pl.BlockSpec((tm, d), lambda j, i: (i, 0)),
                  pl.BlockSpec((tn, d), lambda j, i: (j, 0))],
        out_specs=pl.BlockSpec((tm, tn), lambda j, i: (i, j)),
        out_shape=jax.ShapeDtypeStruct((m, ncols), F32),
        scratch_shapes=[pltpu.VMEM((tn, d), BF16)],
        compiler_params=_params("parallel", "arbitrary"),
        name=name,
    )(h, wt)


def _lru_head(h, first, u_ref, gate_ref, halo_ref, cw_ref, cb_ref, wa_ref, wx_ref, ba_ref, bx_ref,
              lam_ref, o_ref, carry_ref):
    tt = u_ref.shape[0]
    hd = u_ref.shape[1] // LRU_HEADS
    cs = slice(h * hd, (h + 1) * hd)
    p = u_ref[:, cs]
    halo = jnp.where(first, 0.0, halo_ref[:, cs])
    ext = jnp.concatenate([halo, p], axis=0)
    cw = cw_ref[:, cs]
    u = cb_ref[:, cs] + p * cw[CONV_WIDTH - 1:CONV_WIDTH, :]
    for j in range(1, CONV_WIDTH):
        shifted = pltpu.roll(ext, j, 0)[SUBLANE:, :]
        u = u + shifted * cw[CONV_WIDTH - 1 - j:CONV_WIDTH - j, :]
    ub = u.astype(BF16)
    ra = jnp.dot(ub, wa_ref[h], preferred_element_type=F32)
    rx = jnp.dot(ub, wx_ref[h], preferred_element_type=F32)
    yield
    r = jax.nn.sigmoid(ra + ba_ref[:, cs])
    ig = jax.nn.sigmoid(rx + bx_ref[:, cs])
    a = jnp.exp(r * ((-LRU_C) * _softplus(-lam_ref[:, cs])))
    mult = jnp.sqrt(1.0 - a * a)
    row = lax.broadcasted_iota(jnp.int32, (tt, hd), 0)
    mult = jnp.where(jnp.logical_and(first, row == 0), 1.0, mult)
    b = mult * (ig * u)

    groups = tt // SUBLANE
    a3 = a.reshape(groups, SUBLANE, hd)
    b3 = b.reshape(groups, SUBLANE, hd)
    sub = lax.broadcasted_iota(jnp.int32, (groups, SUBLANE, hd), 1)
    s = 1
    while s < SUBLANE:
        keep = sub >= s
        a_s = jnp.where(keep, pltpu.roll(a3, s, 1), 1.0)
        b_s = jnp.where(keep, pltpu.roll(b3, s, 1), 0.0)
        b3 = a3 * b_s + b3
        a3 = a3 * a_s
        s *= 2
    yield
    gate = gate_ref[:, cs]
    carry = carry_ref[:, cs]
    outs = []
    for g in range(groups):
        hh = b3[g] + a3[g] * carry
        carry = hh[SUBLANE - 1:SUBLANE, :]
        outs.append(hh * gate[g * SUBLANE:(g + 1) * SUBLANE, :])
    per = BF16_SUBLANE // SUBLANE
    for t0 in range(0, groups, per):
        o_ref[t0 * SUBLANE:(t0 + per) * SUBLANE, cs] = jnp.concatenate(
            outs[t0:t0 + per], axis=0).astype(BF16)
    carry_ref[:, cs] = carry
    yield


def _token_shift(x, halo, mu, first, row):
    prev = jnp.where(first, 0.0, halo[SUBLANE - 1:SUBLANE, :])
    xs = jnp.where(row == 0, prev, pltpu.roll(x, 1, 0))
    return x + (xs - x) * mu


def _mm1(a, b, dims=_NN):
    return _dg(a.astype(BF16), b.astype(BF16), dims)


def _pair_diag(y, left):
    return jnp.concatenate([jnp.where(left, y, 0.0), jnp.where(left, 0.0, y)], axis=0).astype(BF16)


def _pair_mm(x, y, left):
    return _dg(x.astype(BF16), _pair_diag(y, left), _NN)


def _chunk_chain(ops, store):
    ab_, bb_, kb_, rb_, v_, bt_, kt_, pe_ = ops
    rc, lane = _iota2((CHUNK, PAIR))
    cc = lane % HEAD
    left = lane < HEAD
    strict = rc > cc
    incl = rc >= cc
    diag = rc == cc
    ar16 = [jnp.concatenate([x, y], axis=0).astype(BF16) for x, y in zip(ab_, rb_)]
    bd_b = [_pair_diag(x, left) for x in bb_]
    bd_k = [_pair_diag(x, left) for x in kb_]
    bd_v = [_pair_diag(x, left) for x in v_]
    arb = [_dg(x, y, _NT) for x, y in zip(ar16, bd_b)]
    ark = [_dg(x, y, _NT) for x, y in zip(ar16, bd_k)]
    a_ab = [jnp.where(strict, x[:CHUNK], 0.0) for x in arb]
    a_rb = [jnp.where(incl, x[CHUNK:], 0.0).astype(BF16) for x in arb]
    a_akrk = [jnp.concatenate([jnp.where(strict, x[:CHUNK], 0.0), jnp.where(incl, x[CHUNK:], 0.0)],
                              axis=0).astype(BF16) for x in ark]
    yield
    base = 8
    d = [jnp.where((rc // base) == (cc // base), a, 0.0) for a in a_ab]
    d2 = [_pair_mm(t, t, left) for t in d]
    akrkv = [_dg(x, y, _NN) for x, y in zip(a_akrk, bd_v)]
    akv = [x[:CHUNK] for x in akrkv]
    rkv = [x[CHUNK:] for x in akrkv]
    x = [jnp.where(diag, 1.0, 0.0) + t for t in d]
    yield
    x = [xi + _pair_mm(t2, xi, left) for xi, t2 in zip(x, d2)]
    d4 = [_pair_mm(t2, t2, left) for t2 in d2]
    yield
    x = [xi + _pair_mm(t4, xi, left) for xi, t4 in zip(x, d4)]
    yield
    size = base
    while size < CHUNK:
        off = jnp.logical_and((rc // (2 * size)) == (cc // (2 * size)),
                              (rc // size) != (cc // size))
        o = [jnp.where(off, a, 0.0) for a in a_ab]
        ox = [_pair_mm(oi, xi, left) for oi, xi in zip(o, x)]
        yield
        x = [xi + _pair_mm(xi, oxi, left) for xi, oxi in zip(x, ox)]
        yield
        size *= 2
    t = [xi.astype(BF16) for xi in x]
    wu = [_dg(ti, jnp.concatenate([_pair_diag(y, left), _pair_diag(z, left)], axis=1), _NN)
          for ti, y, z in zip(t, ab_, akv)]
    kv = [_dg(xi.astype(BF16), y.astype(BF16), _TN) for xi, y in zip(kt_, v_)]
    yield
    ry = [_dg(xi, jnp.concatenate([_pair_diag(y[:, :PAIR], left), _pair_diag(y[:, PAIR:], left)], axis=1), _NN)
          for xi, y in zip(a_rb, wu)]
    mn = [_dg(xi.astype(BF16), y.astype(BF16), _TN) for xi, y in zip(bt_, wu)]
    yield

    def head_blocks(z):
        return jnp.where(left, z[:HEAD, :], z[HEAD:, :])

    for u in range(len(ab_)):
        store(u,
              rb_[u] + ry[u][:, :PAIR],
              ry[u][:, PAIR:] + rkv[u],
              jnp.where(diag, pe_[u], 0.0) + head_blocks(mn[u][:, :PAIR]),
              head_blocks(mn[u][:, PAIR:]) + head_blocks(kv[u]))


def _rwkv_a_body(r_ref, k_ref, v_ref, l_ref, rh_ref, kh_ref, vh_ref, lh_ref,
                 mur_ref, muk_ref, muv_ref, mul_ref, w0_ref, a0_ref, kkw_ref, kaw_ref, rkw_ref,
                 w2_ref, a2_ref, g2_ref, ones_ref, tri_ref,
                 u_ref, gate_ref, halo_ref, cw_ref, cb_ref, wa_ref, wx_ref, ba_ref, bx_ref, lam_ref,
                 *rest):
    ncast = (len(rest) - 8) // 2
    cast_in = rest[:ncast]
    rp_ref, yp_ref, m_ref, n_ref, bonus_ref, g_ref, ya_ref = rest[ncast:ncast + 7]
    cast_out = rest[ncast + 7:-1]
    carry_ref = rest[-1]
    first = pl.program_id(1) == 0

    @pl.when(first)
    def _():
        carry_ref[...] = jnp.zeros_like(carry_ref)
    cl = CHUNK
    rows = CHUNKS_PER_STEP * cl
    width = HEADS_PER_STEP * HEAD
    gw = ones_ref.shape[0]
    row_g = lax.broadcasted_iota(jnp.int32, (rows, gw), 0)
    row_l = lax.broadcasted_iota(jnp.int32, (rows, l_ref.shape[1]), 0)
    ones_h = ones_ref[...]

    lo = _token_shift(l_ref[...], lh_ref[...], mul_ref[...], first, row_l)
    act_w = _split(jnp.tanh(lo[:, 0:LANE]))
    act_a = _split(lo[:, LANE:2 * LANE])
    act_g = _split(jax.nn.sigmoid(lo[:, 2 * LANE:]))

    def lora(act, w_ref, cs, keep_low):
        (ah, al_), wb = act, w_ref[:, cs]
        out = _dg(ah, wb, _NN)
        return out + _dg(al_, wb, _NN) if keep_low else out

    def prologue(c0, out):
        cs = slice(c0, c0 + gw)
        r = _token_shift(r_ref[:, cs], rh_ref[:, cs], mur_ref[:, cs], first, row_g)
        k = _token_shift(k_ref[:, cs], kh_ref[:, cs], muk_ref[:, cs], first, row_g)
        v = _token_shift(v_ref[:, cs], vh_ref[:, cs], muv_ref[:, cs], first, row_g)
        w_lin = w0_ref[:, cs] + lora(act_w, w2_ref, cs, True)
        a_lin = a0_ref[:, cs] + lora(act_a, a2_ref, cs, False)
        g_ref[:, cs] = lora(act_g, g2_ref, cs, False).astype(BF16)
        kk = k * kkw_ref[:, cs]
        kk_ss = _head_sums(kk * kk, ones_h)
        yield
        lw = DECAY_SCALE * jax.nn.sigmoid(w_lin)
        a = jax.nn.sigmoid(a_lin)
        kk = kk * lax.rsqrt(jnp.maximum(kk_ss, L2_EPS * L2_EPS))
        kp = k * (1.0 + (a - 1.0) * kaw_ref[:, cs])
        bonus_ref[:, cs] = (_head_sums(r * kp * rkw_ref[:, cs], ones_h) * v).astype(BF16)
        lc = _mm2_exact_lhs(tri_ref[...], lw)
        yield
        p_incl = jnp.exp(lc)
        p_excl = jnp.exp(lc - lw)
        p_inv = 1.0 / p_incl
        p_end = jnp.concatenate(
            [jnp.broadcast_to(p_incl[(j + 1) * cl - 1:(j + 1) * cl, :], (cl, gw))
             for j in range(CHUNKS_PER_STEP)], axis=0)
        abar = -(kk * p_excl)
        bbar = kk * a * p_inv
        kbar = kp * p_inv
        rbar = r * p_incl
        btil = bbar * p_end
        ktil = kbar * p_end
        units = [(j, q) for j in range(CHUNKS_PER_STEP) for q in range(gw // PAIR)]
        out.extend([x[j * cl:(j + 1) * cl, q * PAIR:(q + 1) * PAIR] for j, q in units]
                   for x in (abar, bbar, kbar, rbar, v, btil, ktil, p_end))
        yield

    def make_store(c0):
        units = [(j, q) for j in range(CHUNKS_PER_STEP) for q in range(gw // PAIR)]

        def store(u, rp, yp, mm, nn):
            j, q = units[u]
            rs = slice(j * cl, (j + 1) * cl)
            qs = slice(c0 + q * PAIR, c0 + (q + 1) * PAIR)
            rp_ref[rs, qs] = rp.astype(BF16)
            yp_ref[rs, qs] = yp.astype(BF16)
            m_ref[rs, qs] = mm.astype(BF16)
            n_ref[rs, qs] = nn.astype(BF16)
        return store

    lru = (None for h in range(LRU_HEADS)
           for _ in _lru_head(h, first, u_ref, gate_ref, halo_ref, cw_ref, cb_ref, wa_ref, wx_ref,
                              ba_ref, bx_ref, lam_ref, ya_ref, carry_ref))
    chains = []
    for c0 in range(0, width, gw):
        ops = []
        for _ in prologue(c0, ops):
            for ch in chains:
                next(ch, None)
        chains.append(_chunk_chain(ops, make_store(c0)))
    live = list(chains)
    while live:
        live = [ch for ch in live if next(ch, StopIteration) is not StopIteration]
        next(lru, None)
    for _ in lru:
        pass

    for src, dst in zip(cast_in, cast_out):
        dst[...] = src[...].astype(BF16)


def _rwkv_a(p, p_lora, mu_rkv, mu_lora, w0, a0, k_k, k_a, r_k, w2p, a2p, g2p,
            conv_w, conv_b, wa, wx, ba, bx, lam, bsz, seq, rkv_col0, cast_ws=()):
    dl = conv_w.shape[1]
    lvec = lambda t: t.reshape(1, dl)
    lrow = pl.BlockSpec((1, dl), lambda b, i, q: (0, 0))

    def lru_tile(cb):
        return pl.BlockSpec((CHUNKS_PER_STEP * CHUNK, dl), lambda b, i, q: (b * nc + i, cb))
    cl = CHUNKS_PER_STEP * CHUNK
    width = HEADS_PER_STEP * HEAD
    dr = w0.shape[1]
    ngroups = dr // width
    assert ngroups == 1, "the LRU ride-along expects one grid step per row tile"
    nc = seq // cl
    lw_ = mu_lora.shape[1]
    cb0 = rkv_col0 // width
    rows8 = cl // SUBLANE
    rt, ct = _iota2((cl, cl))
    tri = jnp.where(jnp.logical_and(rt >= ct, (rt // CHUNK) == (ct // CHUNK)), 1.0, 0.0).astype(BF16)
    ones_h = _head_ones(ONES_WIDTH)
    const = lambda arr: pl.BlockSpec(arr.shape, lambda b, i, q: (0, 0))
    lora_w = [wgt.astype(BF16) for wgt in (w2p, a2p, g2p)]

    def tile(cb_off):
        return pl.BlockSpec((cl, width), lambda b, i, q: (b * nc + i, cb0 + cb_off + q))

    def halo(cb_off):
        return pl.BlockSpec(
            (SUBLANE, width),
            lambda b, i, q: (jnp.maximum((b * nc + i) * rows8 - 1, 0), cb0 + cb_off + q))

    def prow(off=0):
        return pl.BlockSpec((1, width), lambda b, i, q: (0, off + q))

    out_tile = pl.BlockSpec((cl, width), lambda b, i, q: (b * nc + i, q))
    out_mat = pl.BlockSpec((CHUNKS_PER_STEP * HEAD, width), lambda b, i, q: (b * nc + i, q))
    act = jax.ShapeDtypeStruct((bsz * seq, dr), BF16)
    mat = jax.ShapeDtypeStruct((bsz * (seq // CHUNK) * HEAD, dr), BF16)

    nsteps = bsz * nc * ngroups
    cast_specs = []
    for wgt in cast_ws:
        hold = 1
        while (wgt.shape[0] * hold) % (nsteps * BF16_SUBLANE) != 0:
            hold *= 2
        blk = (wgt.shape[0] * hold // nsteps, wgt.shape[1])
        cast_specs.append(pl.BlockSpec(
            blk, lambda b, i, q, hold=hold: (((b * nc + i) * ngroups + q) // hold, 0)))
    cast_shapes = [jax.ShapeDtypeStruct(wgt.shape, BF16) for wgt in cast_ws]

    return pl.pallas_call(
        _rwkv_a_body,
        grid=(bsz, nc, ngroups),
        in_specs=[tile(0), tile(ngroups), tile(2 * ngroups),
                  pl.BlockSpec((cl, lw_), lambda b, i, q: (b * nc + i, 0)),
                  halo(0), halo(ngroups), halo(2 * ngroups),
                  pl.BlockSpec((SUBLANE, lw_),
                               lambda b, i, q: (jnp.maximum((b * nc + i) * rows8 - 1, 0), 0)),
                  prow(0), prow(ngroups), prow(2 * ngroups),
                  pl.BlockSpec((1, lw_), lambda b, i, q: (0, 0)),
                  prow(), prow(), prow(), prow(), prow()]
                 + [pl.BlockSpec((t.shape[0], width), lambda b, i, q: (0, q)) for t in lora_w]
                 + [const(ones_h), const(tri)]
                 + [lru_tile(0), lru_tile(1),
                    pl.BlockSpec((SUBLANE, dl),
                                 lambda b, i, q: (jnp.maximum((b * nc + i) * rows8 - 1, 0), 0)),
                    const(conv_w), lrow, pl.BlockSpec(wa.shape, lambda b, i, q: (0, 0, 0)),
                    pl.BlockSpec(wx.shape, lambda b, i, q: (0, 0, 0)), lrow, lrow, lrow]
                 + cast_specs,
        out_specs=[out_tile, out_tile, out_mat, out_mat, out_tile, out_tile, lru_tile(0)] + cast_specs,
        out_shape=[act, act, mat, mat, act, act, jax.ShapeDtypeStruct((bsz * seq, dl), BF16)]
                  + cast_shapes,
        scratch_shapes=[pltpu.VMEM((1, dl), F32)],
        compiler_params=_params("arbitrary", "arbitrary", "arbitrary"),
        name="rwkv_a",
    )(p, p, p, p_lora, p, p, p, p_lora, mu_rkv, mu_rkv, mu_rkv, mu_lora, w0, a0, k_k, k_a, r_k,
      *lora_w, ones_h, tri,
      p, p, p, conv_w, lvec(conv_b), wa, wx, lvec(ba), lvec(bx), lvec(lam), *cast_ws)


def _rwkv_b_body(rp_ref, yp_ref, m_ref, n_ref, bonus_ref, g_ref, lng_ref, lnb_ref, ones_ref,
                 o_ref, state_ref):
    @pl.when(pl.program_id(1) == 0)
    def _():
        state_ref[...] = jnp.zeros_like(state_ref)

    npairs = state_ref.shape[0]
    pairs = range(npairs)
    ps = [slice(q * PAIR, (q + 1) * PAIR) for q in pairs]
    left = lax.broadcasted_iota(jnp.int32, (HEAD, PAIR), 1) < HEAD
    ones_h = ones_ref[...]
    inv_n = 1.0 / HEAD
    state = [state_ref[q] for q in pairs]
    for j in range(rp_ref.shape[0] // CHUNK):
        rs = slice(j * CHUNK, (j + 1) * CHUNK)
        ks = slice(j * HEAD, (j + 1) * HEAD)
        g0 = [_pair_diag(state[q], left) for q in pairs]
        ys = [_dg(rp_ref[rs, ps[q]].astype(BF16), g0[q], _NN) + yp_ref[rs, ps[q]] for q in pairs]
        state = [_dg(m_ref[ks, ps[q]].astype(BF16), g0[q], _NN) + n_ref[ks, ps[q]] for q in pairs]
        y = jnp.concatenate(ys, axis=1)
        yc = y - _head_sums(y, ones_h) * inv_n
        var = _head_sums(yc * yc, ones_h) * inv_n
        yn = yc * lax.rsqrt(var + GN_EPS) * lng_ref[...] + lnb_ref[...]
        o_ref[rs, :] = ((yn + bonus_ref[rs, :]) * g_ref[rs, :]).astype(BF16)
    for q in pairs:
        state_ref[q] = state[q]


def _rwkv_b(rp, yp, mc, nm, bonus, g, ln_g, ln_b, bsz, seq):
    cl = RWKV_B_CHUNKS * CHUNK
    dr = rp.shape[1]
    nc = seq // cl
    tile = pl.BlockSpec((cl, dr), lambda b, i: (b * nc + i, 0))
    mat = pl.BlockSpec((RWKV_B_CHUNKS * HEAD, dr), lambda b, i: (b * nc + i, 0))
    prow = pl.BlockSpec((1, dr), lambda b, i: (0, 0))
    ones_h = _head_ones(ONES_WIDTH_B)
    return pl.pallas_call(
        _rwkv_b_body,
        grid=(bsz, nc),
        in_specs=[tile, tile, mat, mat, tile, tile, prow, prow,
                  pl.BlockSpec(ones_h.shape, lambda b, i: (0, 0))],
        out_specs=tile,
        out_shape=jax.ShapeDtypeStruct((bsz * seq, dr), BF16),
        scratch_shapes=[pltpu.VMEM((dr // PAIR, HEAD, PAIR), F32)],
        compiler_params=_params("parallel", "arbitrary"),
        name="rwkv_b",
    )(rp, yp, mc, nm, bonus, g, ln_g, ln_b, ones_h)


def _mm_out_body(ya_ref, yb_ref, x_ref, gm_ref, w_ref, g_ref, sh_ref, sc_ref, o_ref, h_ref, *, sub):
    da = ya_ref.shape[1]
    for r0 in range(0, x_ref.shape[0], sub):
        rs = slice(r0, r0 + sub)
        mix = (jnp.dot(ya_ref[rs, :].astype(BF16), w_ref[:da, :], preferred_element_type=F32)
               + jnp.dot(yb_ref[rs, :].astype(BF16), w_ref[da:, :], preferred_element_type=F32))
        x1 = x_ref[rs, :] + gm_ref[0] * mix
        o_ref[rs, :] = x1
        h_ref[rs, :] = _norm_mod(x1, g_ref[...], sh_ref[0], sc_ref[0]).astype(BF16)


def _mm_out(ya, yb, x2, gm, w, g, sh, sc, seq, tm=512, sub=256):
    m, d = x2.shape
    per_b = seq // tm
    brow = pl.BlockSpec((1, 1, d), lambda i: (i // per_b, 0, 0))
    tile = pl.BlockSpec((tm, d), lambda i: (i, 0))
    return pl.pallas_call(
        functools.partial(_mm_out_body, sub=sub),
        grid=(m // tm,),
        in_specs=[pl.BlockSpec((tm, ya.shape[1]), lambda i: (i, 0)),
                  pl.BlockSpec((tm, yb.shape[1]), lambda i: (i, 0)),
                  tile, brow,
                  pl.BlockSpec(w.shape, lambda i: (0, 0)),
                  pl.BlockSpec((1, d), lambda i: (0, 0)), brow, brow],
        out_specs=[tile, tile],
        out_shape=[jax.ShapeDtypeStruct((m, d), F32), jax.ShapeDtypeStruct((m, d), BF16)],
        compiler_params=_params("parallel"),
        name="mm_out",
    )(ya, yb, x2, gm, w, g, sh, sc)


def _ffn_body(x_ref, h_ref, gf_ref, wg_ref, wu_ref, wd_ref, fg_ref, o_ref, acc_ref):
    f = pl.program_id(1)

    @pl.when(f == 0)
    def _():
        acc_ref[...] = jnp.zeros_like(acc_ref)

    h = h_ref[...]
    gate = jnp.dot(h, wg_ref[...], preferred_element_type=F32)
    up = jnp.dot(h, wu_ref[...], preferred_element_type=F32)
    act = (gate * jax.nn.sigmoid(gate) * up).astype(BF16)
    acc_ref[...] += jnp.dot(act, wd_ref[...], preferred_element_type=F32)

    @pl.when(f == pl.num_programs(1) - 1)
    def _():
        y = x_ref[...] + gf_ref[0] * acc_ref[...]
        o_ref[...] = (y * lax.rsqrt(jnp.mean(y * y, axis=-1, keepdims=True) + RMS_EPS)
                      * fg_ref[...])


def _ffn(x1, h2, gf, w_gu, w_down, fg, seq, tm=512, tf=512):
    m, d = x1.shape
    dff = w_down.shape[0]
    nf = dff // tf
    assert seq % tm == 0 and dff % tf == 0, "row tiles must not straddle sequences"
    per_b = seq // tm
    tile = pl.BlockSpec((tm, d), lambda i, f: (i, 0))
    prow = pl.BlockSpec((1, d), lambda i, f: (0, 0))
    return pl.pallas_call(
        _ffn_body,
        grid=(m // tm, nf),
        in_specs=[tile, tile,
                  pl.BlockSpec((1, 1, d), lambda i, f: (i // per_b, 0, 0)),
                  pl.BlockSpec((d, tf), lambda i, f: (0, f)),
                  pl.BlockSpec((d, tf), lambda i, f: (0, nf + f)),
                  pl.BlockSpec((tf, d), lambda i, f: (f, 0)),
                  prow],
        out_specs=tile,
        out_shape=jax.ShapeDtypeStruct((m, d), F32),
        scratch_shapes=[pltpu.VMEM((tm, d), F32)],
        compiler_params=_params("parallel", "arbitrary"),
        name="ffn",
    )(x1, h2, gf, w_gu, w_gu, w_down, fg)


def _pad_cols(w, n):
    return jnp.pad(w, ((0, 0), (0, n - w.shape[1])))


def _pad_rows(w, n):
    return jnp.pad(w, ((0, n - w.shape[0]), (0, 0)))


def kernel(x, c, w_ada, b_ada, norm_mix_g, w_in, conv_w, conv_b, lru_wa, lru_ba, lru_wx, lru_bx, lru_lambda, rwkv_mu, rwkv_w0, rwkv_w2, rwkv_a0, rwkv_a2, rwkv_g2, rwkv_k_k, rwkv_k_a, rwkv_r_k, rwkv_ln_g, rwkv_ln_b, w_out, norm_ffn_g, w_gu, w_down, final_norm_g):
    bsz, seq, d = x.shape
    depth = w_ada.shape[0]
    dl = conv_w.shape[2]
    dr = rwkv_w0.shape[1]
    w_lora, a_lora, g_lora = rwkv_w2.shape[1], rwkv_a2.shape[1], rwkv_g2.shape[1]
    wpad, apad = LANE, LANE
    gpad = -(-g_lora // LANE) * LANE
    rkv_col0 = 2 * dl
    lora0 = rkv_col0 + 3 * dr

    x2 = x.reshape(bsz * seq, d)
    for l in range(depth):
        mod = _mod(c, w_ada[l], b_ada[l].reshape(1, -1))
        sh_m, sc_m, g_m, sh_f, sc_f, g_f = [t.reshape(bsz, 1, d) for t in jnp.split(mod, 6, axis=-1)]

        wi = jnp.swapaxes(w_in[l], 0, 1)
        o1, o2 = lora0 + w_lora, lora0 + w_lora + a_lora
        w_lora_p = jnp.concatenate(
            [_pad_rows(wi[lora0:o1], wpad), _pad_rows(wi[o1:o2], apad),
             _pad_rows(wi[o2:], gpad)], axis=0)
        mu = rwkv_mu[l].reshape(1, -1)
        mu_rkv = mu[:, :3 * dr]
        mu_lora = jnp.concatenate(
            [_pad_cols(mu[:, 3 * dr:3 * dr + w_lora], wpad),
             _pad_cols(mu[:, 3 * dr + w_lora:3 * dr + w_lora + a_lora], apad),
             _pad_cols(mu[:, 3 * dr + w_lora + a_lora:], gpad)], axis=1)
        w2p = _pad_rows(rwkv_w2[l], wpad)
        a2p = _pad_rows(rwkv_a2[l], apad)
        g2p = _pad_rows(rwkv_g2[l], gpad)

        h = _norm(x2, norm_mix_g[l].reshape(1, d), sh_m, sc_m, seq)
        tn_in = dl
        p = _mm_in(h, wi, lora0, tn=tn_in, gelu_tile=dl // tn_in)
        p_lora = _mm_in(h, w_lora_p, w_lora_p.shape[0], tn=w_lora_p.shape[0], name="mm_lora")

        rowv = lambda t: t.reshape(1, dr)
        rp, yp, mc, nm, bonus, gg, y_a, w_out_b, w_gu_b, w_down_b = _rwkv_a(
            p, p_lora, mu_rkv, mu_lora, rowv(rwkv_w0[l]), rowv(rwkv_a0[l]), rowv(rwkv_k_k[l]),
            rowv(rwkv_k_a[l]), rowv(rwkv_r_k[l]), w2p, a2p, g2p,
            conv_w[l], conv_b[l], lru_wa[l].astype(BF16), lru_wx[l].astype(BF16),
            lru_ba[l], lru_bx[l], lru_lambda[l], bsz, seq, rkv_col0,
            cast_ws=(w_out[l], w_gu[l], w_down[l]))
        y_b = _rwkv_b(rp, yp, mc, nm, bonus, gg, rowv(rwkv_ln_g[l]), rowv(rwkv_ln_b[l]), bsz, seq)

        x2, h2 = _mm_out(y_a, y_b, x2, g_m, w_out_b, norm_ffn_g[l].reshape(1, d), sh_f, sc_f, seq)

        last = l == depth - 1
        fg = final_norm_g.reshape(1, d) if last else None
        assert last, "only the final layer carries the closing RMSNorm"
        x2 = _ffn(x2, h2, g_f, w_gu_b, w_down_b, fg, seq)
    return x2.reshape(bsz, seq, d)
```

```python
import functools
import math

import jax
import jax.numpy as jnp
from jax import lax
from jax.experimental import pallas as pl
from jax.experimental.pallas import tpu as pltpu

F32 = jnp.float32
BF16 = jnp.bfloat16

LRU_HEADS = 4
CONV_WIDTH = 4
LRU_C = 8.0
HEAD = 64
CHUNK = 64
PAIR = 2 * HEAD
HEADS_PER_STEP = 16
ONES_WIDTH = 256
ONES_WIDTH_B = 128
CHUNKS_PER_STEP = 4
RWKV_B_CHUNKS = 8
RMS_EPS = 1e-6
GN_EPS = 64e-5
L2_EPS = 1e-12
DECAY_SCALE = -math.exp(-0.5)
LANE = 128
SUBLANE = 8
BF16_SUBLANE = 16
VMEM_LIMIT = 56 * 1024 * 1024


def _params(*sem):
    return pltpu.CompilerParams(dimension_semantics=sem, vmem_limit_bytes=VMEM_LIMIT)


_NN = (((1,), (0,)), ((), ()))
_NT = (((1,), (1,)), ((), ()))
_TN = (((0,), (0,)), ((), ()))


def _dg(a, b, dims):
    return lax.dot_general(a, b, dims, preferred_element_type=F32)


def _split(x):
    hi = x.astype(BF16)
    lo = (x - hi.astype(F32)).astype(BF16)
    return hi, lo


def _mm3(a, b, dims=_NN):
    ah, al = _split(a)
    bh, bl = _split(b)
    return _dg(ah, bh, dims) + (_dg(ah, bl, dims) + _dg(al, bh, dims))


def _head_sums(x, ones_h):
    n = ones_h.shape[0]
    xb = x.astype(BF16)
    return jnp.concatenate([_dg(xb[:, c:c + n], ones_h, _NN) for c in range(0, x.shape[1], n)],
                           axis=1)


def _mm2_exact_lhs(a_bf16, b):
    bh, bl = _split(b)
    return _dg(a_bf16, bh, _NN) + _dg(a_bf16, bl, _NN)


def _softplus(x):
    return jnp.maximum(x, 0.0) + jnp.log1p(jnp.exp(-jnp.abs(x)))


def _iota2(shape):
    return (lax.broadcasted_iota(jnp.int32, shape, 0),
            lax.broadcasted_iota(jnp.int32, shape, 1))


def _head_ones(n):
    r, c = _iota2((n, n))
    return jnp.where((r // HEAD) == (c // HEAD), 1.0, 0.0).astype(BF16)


def _mod_body(c_ref, w_ref, b_ref, o_ref):
    c = c_ref[...]
    ca = c * jax.nn.sigmoid(c)
    o_ref[...] = _mm3(ca, w_ref[...]) + b_ref[...]


def _mod(c, w, b, tn=1024):
    bsz, d = c.shape
    n = w.shape[1]
    return pl.pallas_call(
        _mod_body,
        grid=(n // tn,),
        in_specs=[pl.BlockSpec((bsz, d), lambda j: (0, 0)),
                  pl.BlockSpec((d, tn), lambda j: (0, j)),
                  pl.BlockSpec((1, tn), lambda j: (0, j))],
        out_specs=pl.BlockSpec((bsz, tn), lambda j: (0, j)),
        out_shape=jax.ShapeDtypeStruct((bsz, n), F32),
        compiler_params=_params("parallel"),
        name="mod",
    )(c, w, b)


def _norm_mod(x, g, sh, sc):
    y = x * lax.rsqrt(jnp.mean(x * x, axis=-1, keepdims=True) + RMS_EPS) * g
    return y * (1.0 + sc) + sh


def _norm_body(x_ref, g_ref, sh_ref, sc_ref, w_ref, o_ref, pl_ref, wb_ref):
    @pl.when(pl.program_id(0) == 0)
    def _():
        wb_ref[...] = w_ref[...].astype(BF16)

    h = _norm_mod(x_ref[...], g_ref[...], sh_ref[0], sc_ref[0]).astype(BF16)
    o_ref[...] = h
    pl_ref[...] = _dg(h, wb_ref[...], _NT)


def _norm(x2, g, sh, sc, w_lora_t, seq, tm=512):
    m, d = x2.shape
    nl = w_lora_t.shape[0]
    per_b = seq // tm
    return pl.pallas_call(
        _norm_body,
        grid=(m // tm,),
        in_specs=[pl.BlockSpec((tm, d), lambda i: (i, 0)),
                  pl.BlockSpec((1, d), lambda i: (0, 0)),
                  pl.BlockSpec((1, 1, d), lambda i: (i // per_b, 0, 0)),
                  pl.BlockSpec((1, 1, d), lambda i: (i // per_b, 0, 0)),
                  pl.BlockSpec((nl, d), lambda i: (0, 0))],
        out_specs=[pl.BlockSpec((tm, d), lambda i: (i, 0)), pl.BlockSpec((tm, nl), lambda i: (i, 0))],
        out_shape=[jax.ShapeDtypeStruct((m, d), BF16), jax.ShapeDtypeStruct((m, nl), F32)],
        scratch_shapes=[pltpu.VMEM((nl, d), BF16)],
        compiler_params=_params("arbitrary"),
        name="norm_mix",
    )(x2, g, sh, sc, w_lora_t)


def _mm_in_body(h_ref, w_ref, o_ref, wb_ref, *, gelu_tile):
    @pl.when(pl.program_id(1) == 0)
    def _():
        wb_ref[...] = w_ref[...].astype(BF16)

    if gelu_tile is None:
        o_ref[...] = _dg(h_ref[...], wb_ref[...], _NT)
    else:
        @pl.when(pl.program_id(0) == gelu_tile)
        def _():
            o_ref[...] = jax.nn.gelu(_dg(h_ref[...], wb_ref[...], _NT))

        @pl.when(pl.program_id(0) != gelu_tile)
        def _():
            o_ref[...] = _dg(h_ref[...], wb_ref[...], _NT)


def _mm_in(h, wt, ncols, tm=1024, tn=1024, name="mm_in", gelu_tile=None):
    m, d = h.shape
    return pl.pallas_call(
        functools.partial(_mm_in_body, gelu_tile=gelu_tile),
        grid=(ncols // tn, m // tm),
        in_specs=[pl.BlockSpec((tm, d), lambda j, i: (i, 0)),
                  pl.BlockSpec((tn, d), lambda j, i: (j, 0))],
        out_specs=pl.BlockSpec((tm, tn), lambda j, i: (i, j)),
        out_shape=jax.ShapeDtypeStruct((m, ncols), F32),
        scratch_shapes=[pltpu.VMEM((tn, d), BF16)],
        compiler_params=_params("parallel", "arbitrary"),
        name=name,
    )(h, wt)


def _lru_head(h, first, u_ref, gate_ref, halo_ref, cw_ref, cb_ref, wa_ref, wx_ref, ba_ref, bx_ref,
              lam_ref, o_ref, carry_ref):
    tt = u_ref.shape[0]
    hd = u_ref.shape[1] // LRU_HEADS
    cs = slice(h * hd, (h + 1) * hd)
    p = u_ref[:, cs]
    halo = jnp.where(first, 0.0, halo_ref[:, cs])
    ext = jnp.concatenate([halo, p], axis=0)
    cw = cw_ref[:, cs]
    u = cb_ref[:, cs] + p * cw[CONV_WIDTH - 1:CONV_WIDTH, :]
    for j in range(1, CONV_WIDTH):
        shifted = pltpu.roll(ext, j, 0)[SUBLANE:, :]
        u = u + shifted * cw[CONV_WIDTH - 1 - j:CONV_WIDTH - j, :]
    ub = u.astype(BF16)
    ra = jnp.dot(ub, wa_ref[h], preferred_element_type=F32)
    rx = jnp.dot(ub, wx_ref[h], preferred_element_type=F32)
    yield
    r = jax.nn.sigmoid(ra + ba_ref[:, cs])
    ig = jax.nn.sigmoid(rx + bx_ref[:, cs])
    a = jnp.exp(r * ((-LRU_C) * _softplus(-lam_ref[:, cs])))
    mult = jnp.sqrt(1.0 - a * a)
    row = lax.broadcasted_iota(jnp.int32, (tt, hd), 0)
    mult = jnp.where(jnp.logical_and(first, row == 0), 1.0, mult)
    b = mult * (ig * u)

    groups = tt // SUBLANE
    a3 = a.reshape(groups, SUBLANE, hd)
    b3 = b.reshape(groups, SUBLANE, hd)
    sub = lax.broadcasted_iota(jnp.int32, (groups, SUBLANE, hd), 1)
    s = 1
    while s < SUBLANE:
        keep = sub >= s
        a_s = jnp.where(keep, pltpu.roll(a3, s, 1), 1.0)
        b_s = jnp.where(keep, pltpu.roll(b3, s, 1), 0.0)
        b3 = a3 * b_s + b3
        a3 = a3 * a_s
        s *= 2
    yield
    gate = gate_ref[:, cs]
    carry = carry_ref[:, cs]
    outs = []
    for g in range(groups):
        hh = b3[g] + a3[g] * carry
        carry = hh[SUBLANE - 1:SUBLANE, :]
        outs.append(hh * gate[g * SUBLANE:(g + 1) * SUBLANE, :])
    per = BF16_SUBLANE // SUBLANE
    for t0 in range(0, groups, per):
        o_ref[t0 * SUBLANE:(t0 + per) * SUBLANE, cs] = jnp.concatenate(
            outs[t0:t0 + per], axis=0).astype(BF16)
    carry_ref[:, cs] = carry
    yield


def _token_shift(x, halo, mu, first, row):
    prev = jnp.where(first, 0.0, halo[SUBLANE - 1:SUBLANE, :])
    xs = jnp.where(row == 0, prev, pltpu.roll(x, 1, 0))
    return x + (xs - x) * mu


def _mm1(a, b, dims=_NN):
    return _dg(a.astype(BF16), b.astype(BF16), dims)


def _pair_diag(y, left):
    return jnp.concatenate([jnp.where(left, y, 0.0), jnp.where(left, 0.0, y)], axis=0).astype(BF16)


def _pair_mm(x, y, left):
    return _dg(x.astype(BF16), _pair_diag(y, left), _NN)


def _chunk_chain(ops, store):
    ab_, bb_, kb_, rb_, v_, bt_, kt_, pe_ = ops
    rc, lane = _iota2((CHUNK, PAIR))
    cc = lane % HEAD
    left = lane < HEAD
    strict = rc > cc
    incl = rc >= cc
    diag = rc == cc
    ar16 = [jnp.concatenate([x, y], axis=0).astype(BF16) for x, y in zip(ab_, rb_)]
    bd_b = [_pair_diag(x, left) for x in bb_]
    bd_k = [_pair_diag(x, left) for x in kb_]
    bd_v = [_pair_diag(x, left) for x in v_]
    arb = [_dg(x, y, _NT) for x, y in zip(ar16, bd_b)]
    ark = [_dg(x, y, _NT) for x, y in zip(ar16, bd_k)]
    a_ab = [jnp.where(strict, x[:CHUNK], 0.0) for x in arb]
    a_rb = [jnp.where(incl, x[CHUNK:], 0.0).astype(BF16) for x in arb]
    a_akrk = [jnp.concatenate([jnp.where(strict, x[:CHUNK], 0.0), jnp.where(incl, x[CHUNK:], 0.0)],
                              axis=0).astype(BF16) for x in ark]
    yield
    base = 8
    d = [jnp.where((rc // base) == (cc // base), a, 0.0) for a in a_ab]
    d2 = [_pair_mm(t, t, left) for t in d]
    akrkv = [_dg(x, y, _NN) for x, y in zip(a_akrk, bd_v)]
    akv = [x[:CHUNK] for x in akrkv]
    rkv = [x[CHUNK:] for x in akrkv]
    x = [jnp.where(diag, 1.0, 0.0) + t for t in d]
    yield
    x = [xi + _pair_mm(t2, xi, left) for xi, t2 in zip(x, d2)]
    d4 = [_pair_mm(t2, t2, left) for t2 in d2]
    yield
    x = [xi + _pair_mm(t4, xi, left) for xi, t4 in zip(x, d4)]
    yield
    size = base
    while size < CHUNK:
        off = jnp.logical_and((rc // (2 * size)) == (cc // (2 * size)),
                              (rc // size) != (cc // size))
        o = [jnp.where(off, a, 0.0) for a in a_ab]
        ox = [_pair_mm(oi, xi, left) for oi, xi in zip(o, x)]
        yield
        x = [xi + _pair_mm(xi, oxi, left) for xi, oxi in zip(x, ox)]
        yield
        size *= 2
    t = [xi.astype(BF16) for xi in x]
    wu = [_dg(ti, jnp.concatenate([_pair_diag(y, left), _pair_diag(z, left)], axis=1), _NN)
          for ti, y, z in zip(t, ab_, akv)]
    kv = [_dg(xi.astype(BF16), y.astype(BF16), _TN) for xi, y in zip(kt_, v_)]
    yield
    ry = [_dg(xi, jnp.concatenate([_pair_diag(y[:, :PAIR], left), _pair_diag(y[:, PAIR:], left)], axis=1), _NN)
          for xi, y in zip(a_rb, wu)]
    mn = [_dg(xi.astype(BF16), y.astype(BF16), _TN) for xi, y in zip(bt_, wu)]
    yield

    def head_blocks(z):
        return jnp.where(left, z[:HEAD, :], z[HEAD:, :])

    for u in range(len(ab_)):
        store(u,
              rb_[u] + ry[u][:, :PAIR],
              ry[u][:, PAIR:] + rkv[u],
              jnp.where(diag, pe_[u], 0.0) + head_blocks(mn[u][:, :PAIR]),
              head_blocks(mn[u][:, PAIR:]) + head_blocks(kv[u]))


def _rwkv_a_body(r_ref, k_ref, v_ref, l_ref, rh_ref, kh_ref, vh_ref, lh_ref,
                 mur_ref, muk_ref, muv_ref, mul_ref, w0_ref, a0_ref, kkw_ref, kaw_ref, rkw_ref,
                 w2_ref, a2_ref, g2_ref, ones_ref, tri_ref,
                 u_ref, gate_ref, halo_ref, cw_ref, cb_ref, wa_ref, wx_ref, ba_ref, bx_ref, lam_ref,
                 *rest):
    ncast = (len(rest) - 8) // 2
    cast_in = rest[:ncast]
    rp_ref, yp_ref, m_ref, n_ref, bonus_ref, g_ref, ya_ref = rest[ncast:ncast + 7]
    cast_out = rest[ncast + 7:-1]
    carry_ref = rest[-1]
    first = pl.program_id(1) == 0

    @pl.when(first)
    def _():
        carry_ref[...] = jnp.zeros_like(carry_ref)
    cl = CHUNK
    rows = CHUNKS_PER_STEP * cl
    width = HEADS_PER_STEP * HEAD
    gw = ones_ref.shape[0]
    row_g = lax.broadcasted_iota(jnp.int32, (rows, gw), 0)
    row_l = lax.broadcasted_iota(jnp.int32, (rows, l_ref.shape[1]), 0)
    ones_h = ones_ref[...]

    lo = _token_shift(l_ref[...], lh_ref[...], mul_ref[...], first, row_l)
    act_w = _split(jnp.tanh(lo[:, 0:LANE]))
    act_a = _split(lo[:, LANE:2 * LANE])
    act_g = _split(jax.nn.sigmoid(lo[:, 2 * LANE:]))

    def lora(act, w_ref, cs, keep_low):
        (ah, al_), wb = act, w_ref[:, cs]
        out = _dg(ah, wb, _NN)
        return out + _dg(al_, wb, _NN) if keep_low else out

    def prologue(c0, out):
        cs = slice(c0, c0 + gw)
        r = _token_shift(r_ref[:, cs], rh_ref[:, cs], mur_ref[:, cs], first, row_g)
        k = _token_shift(k_ref[:, cs], kh_ref[:, cs], muk_ref[:, cs], first, row_g)
        v = _token_shift(v_ref[:, cs], vh_ref[:, cs], muv_ref[:, cs], first, row_g)
        w_lin = w0_ref[:, cs] + lora(act_w, w2_ref, cs, True)
        a_lin = a0_ref[:, cs] + lora(act_a, a2_ref, cs, False)
        g_ref[:, cs] = lora(act_g, g2_ref, cs, False).astype(BF16)
        kk = k * kkw_ref[:, cs]
        kk_ss = _head_sums(kk * kk, ones_h)
        yield
        lw = DECAY_SCALE * jax.nn.sigmoid(w_lin)
        a = jax.nn.sigmoid(a_lin)
        kk = kk * lax.rsqrt(jnp.maximum(kk_ss, L2_EPS * L2_EPS))
        kp = k * (1.0 + (a - 1.0) * kaw_ref[:, cs])
        bonus_ref[:, cs] = (_head_sums(r * kp * rkw_ref[:, cs], ones_h) * v).astype(BF16)
        lc = _mm2_exact_lhs(tri_ref[...], lw)
        yield
        p_incl = jnp.exp(lc)
        p_excl = jnp.exp(lc - lw)
        p_inv = 1.0 / p_incl
        p_end = jnp.concatenate(
            [jnp.broadcast_to(p_incl[(j + 1) * cl - 1:(j + 1) * cl, :], (cl, gw))
             for j in range(CHUNKS_PER_STEP)], axis=0)
        abar = -(kk * p_excl)
        bbar = kk * a * p_inv
        kbar = kp * p_inv
        rbar = r * p_incl
        btil = bbar * p_end
        ktil = kbar * p_end
        units = [(j, q) for j in range(CHUNKS_PER_STEP) for q in range(gw // PAIR)]
        out.extend([x[j * cl:(j + 1) * cl, q * PAIR:(q + 1) * PAIR] for j, q in units]
                   for x in (abar, bbar, kbar, rbar, v, btil, ktil, p_end))
        yield

    def make_store(c0):
        units = [(j, q) for j in range(CHUNKS_PER_STEP) for q in range(gw // PAIR)]

        def store(u, rp, yp, mm, nn):
            j, q = units[u]
            rs = slice(j * cl, (j + 1) * cl)
            qs = slice(c0 + q * PAIR, c0 + (q + 1) * PAIR)
            rp_ref[rs, qs] = rp.astype(BF16)
            yp_ref[rs, qs] = yp.astype(BF16)
            m_ref[rs, qs] = mm.astype(BF16)
            n_ref[rs, qs] = nn.astype(BF16)
        return store

    lru = (None for h in range(LRU_HEADS)
           for _ in _lru_head(h, first, u_ref, gate_ref, halo_ref, cw_ref, cb_ref, wa_ref, wx_ref,
                              ba_ref, bx_ref, lam_ref, ya_ref, carry_ref))
    chains = []
    for c0 in range(0, width, gw):
        ops = []
        for _ in prologue(c0, ops):
            for ch in chains:
                next(ch, None)
        chains.append(_chunk_chain(ops, make_store(c0)))
    live = list(chains)
    while live:
        live = [ch for ch in live if next(ch, StopIteration) is not StopIteration]
        next(lru, None)
    for _ in lru:
        pass

    for src, dst in zip(cast_in, cast_out):
        dst[...] = src[...].astype(BF16)


def _rwkv_a(p, p_lora, mu_rkv, mu_lora, w0, a0, k_k, k_a, r_k, w2p, a2p, g2p,
            conv_w, conv_b, wa, wx, ba, bx, lam, bsz, seq, rkv_col0, cast_ws=()):
    dl = conv_w.shape[1]
    lvec = lambda t: t.reshape(1, dl)
    lrow = pl.BlockSpec((1, dl), lambda b, i, q: (0, 0))

    def lru_tile(cb):
        return pl.BlockSpec((CHUNKS_PER_STEP * CHUNK, dl), lambda b, i, q: (b * nc + i, cb))
    cl = CHUNKS_PER_STEP * CHUNK
    width = HEADS_PER_STEP * HEAD
    dr = w0.shape[1]
    ngroups = dr // width
    assert ngroups == 1, "the LRU ride-along expects one grid step per row tile"
    nc = seq // cl
    lw_ = mu_lora.shape[1]
    cb0 = rkv_col0 // width
    rows8 = cl // SUBLANE
    rt, ct = _iota2((cl, cl))
    tri = jnp.where(jnp.logical_and(rt >= ct, (rt // CHUNK) == (ct // CHUNK)), 1.0, 0.0).astype(BF16)
    ones_h = _head_ones(ONES_WIDTH)
    const = lambda arr: pl.BlockSpec(arr.shape, lambda b, i, q: (0, 0))
    lora_w = [wgt.astype(BF16) for wgt in (w2p, a2p, g2p)]

    def tile(cb_off):
        return pl.BlockSpec((cl, width), lambda b, i, q: (b * nc + i, cb0 + cb_off + q))

    def halo(cb_off):
        return pl.BlockSpec(
            (SUBLANE, width),
            lambda b, i, q: (jnp.maximum((b * nc + i) * rows8 - 1, 0), cb0 + cb_off + q))

    def prow(off=0):
        return pl.BlockSpec((1, width), lambda b, i, q: (0, off + q))

    out_tile = pl.BlockSpec((cl, width), lambda b, i, q: (b * nc + i, q))
    out_mat = pl.BlockSpec((CHUNKS_PER_STEP * HEAD, width), lambda b, i, q: (b * nc + i, q))
    act = jax.ShapeDtypeStruct((bsz * seq, dr), BF16)
    mat = jax.ShapeDtypeStruct((bsz * (seq // CHUNK) * HEAD, dr), BF16)

    nsteps = bsz * nc * ngroups
    cast_specs = []
    for wgt in cast_ws:
        hold = 1
        while (wgt.shape[0] * hold) % (nsteps * BF16_SUBLANE) != 0:
            hold *= 2
        blk = (wgt.shape[0] * hold // nsteps, wgt.shape[1])
        cast_specs.append(pl.BlockSpec(
            blk, lambda b, i, q, hold=hold: (((b * nc + i) * ngroups + q) // hold, 0)))
    cast_shapes = [jax.ShapeDtypeStruct(wgt.shape, BF16) for wgt in cast_ws]

    return pl.pallas_call(
        _rwkv_a_body,
        grid=(bsz, nc, ngroups),
        in_specs=[tile(0), tile(ngroups), tile(2 * ngroups),
                  pl.BlockSpec((cl, lw_), lambda b, i, q: (b * nc + i, 0)),
                  halo(0), halo(ngroups), halo(2 * ngroups),
                  pl.BlockSpec((SUBLANE, lw_),
                               lambda b, i, q: (jnp.maximum((b * nc + i) * rows8 - 1, 0), 0)),
                  prow(0), prow(ngroups), prow(2 * ngroups),
                  pl.BlockSpec((1, lw_), lambda b, i, q: (0, 0)),
                  prow(), prow(), prow(), prow(), prow()]
                 + [pl.BlockSpec((t.shape[0], width), lambda b, i, q: (0, q)) for t in lora_w]
                 + [const(ones_h), const(tri)]
                 + [lru_tile(0), lru_tile(1),
                    pl.BlockSpec((SUBLANE, dl),
                                 lambda b, i, q: (jnp.maximum((b * nc + i) * rows8 - 1, 0), 0)),
                    const(conv_w), lrow, pl.BlockSpec(wa.shape, lambda b, i, q: (0, 0, 0)),
                    pl.BlockSpec(wx.shape, lambda b, i, q: (0, 0, 0)), lrow, lrow, lrow]
                 + cast_specs,
        out_specs=[out_tile, out_tile, out_mat, out_mat, out_tile, out_tile, lru_tile(0)] + cast_specs,
        out_shape=[act, act, mat, mat, act, act, jax.ShapeDtypeStruct((bsz * seq, dl), BF16)]
                  + cast_shapes,
        scratch_shapes=[pltpu.VMEM((1, dl), F32)],
        compiler_params=_params("arbitrary", "arbitrary", "arbitrary"),
        name="rwkv_a",
    )(p, p, p, p_lora, p, p, p, p_lora, mu_rkv, mu_rkv, mu_rkv, mu_lora, w0, a0, k_k, k_a, r_k,
      *lora_w, ones_h, tri,
      p, p, p, conv_w, lvec(conv_b), wa, wx, lvec(ba), lvec(bx), lvec(lam), *cast_ws)


def _rwkv_b_body(rp_ref, yp_ref, m_ref, n_ref, bonus_ref, g_ref, lng_ref, lnb_ref, ones_ref,
                 o_ref, state_ref):
    @pl.when(pl.program_id(1) == 0)
    def _():
        state_ref[...] = jnp.zeros_like(state_ref)

    npairs = state_ref.shape[0]
    pairs = range(npairs)
    ps = [slice(q * PAIR, (q + 1) * PAIR) for q in pairs]
    left = lax.broadcasted_iota(jnp.int32, (HEAD, PAIR), 1) < HEAD
    ones_h = ones_ref[...]
    inv_n = 1.0 / HEAD
    state = [state_ref[q] for q in pairs]
    for j in range(rp_ref.shape[0] // CHUNK):
        rs = slice(j * CHUNK, (j + 1) * CHUNK)
        ks = slice(j * HEAD, (j + 1) * HEAD)
        g0 = [_pair_diag(state[q], left) for q in pairs]
        ys = [_dg(rp_ref[rs, ps[q]].astype(BF16), g0[q], _NN) + yp_ref[rs, ps[q]] for q in pairs]
        state = [_dg(m_ref[ks, ps[q]].astype(BF16), g0[q], _NN) + n_ref[ks, ps[q]] for q in pairs]
        y = jnp.concatenate(ys, axis=1)
        yc = y - _head_sums(y, ones_h) * inv_n
        var = _head_sums(yc * yc, ones_h) * inv_n
        yn = yc * lax.rsqrt(var + GN_EPS) * lng_ref[...] + lnb_ref[...]
        o_ref[rs, :] = ((yn + bonus_ref[rs, :]) * g_ref[rs, :]).astype(BF16)
    for q in pairs:
        state_ref[q] = state[q]


def _rwkv_b(rp, yp, mc, nm, bonus, g, ln_g, ln_b, bsz, seq):
    cl = RWKV_B_CHUNKS * CHUNK
    dr = rp.shape[1]
    nc = seq // cl
    tile = pl.BlockSpec((cl, dr), lambda b, i: (b * nc + i, 0))
    mat = pl.BlockSpec((RWKV_B_CHUNKS * HEAD, dr), lambda b, i: (b * nc + i, 0))
    prow = pl.BlockSpec((1, dr), lambda b, i: (0, 0))
    ones_h = _head_ones(ONES_WIDTH_B)
    return pl.pallas_call(
        _rwkv_b_body,
        grid=(bsz, nc),
        in_specs=[tile, tile, mat, mat, tile, tile, prow, prow,
                  pl.BlockSpec(ones_h.shape, lambda b, i: (0, 0))],
        out_specs=tile,
        out_shape=jax.ShapeDtypeStruct((bsz * seq, dr), BF16),
        scratch_shapes=[pltpu.VMEM((dr // PAIR, HEAD, PAIR), F32)],
        compiler_params=_params("parallel", "arbitrary"),
        name="rwkv_b",
    )(rp, yp, mc, nm, bonus, g, ln_g, ln_b, ones_h)


def _mm_out_body(ya_ref, yb_ref, x_ref, gm_ref, w_ref, g_ref, sh_ref, sc_ref, o_ref, h_ref, *, sub):
    da = ya_ref.shape[1]
    for r0 in range(0, x_ref.shape[0], sub):
        rs = slice(r0, r0 + sub)
        mix = (jnp.dot(ya_ref[rs, :].astype(BF16), w_ref[:da, :], preferred_element_type=F32)
               + jnp.dot(yb_ref[rs, :].astype(BF16), w_ref[da:, :], preferred_element_type=F32))
        x1 = x_ref[rs, :] + gm_ref[0] * mix
        o_ref[rs, :] = x1
        h_ref[rs, :] = _norm_mod(x1, g_ref[...], sh_ref[0], sc_ref[0]).astype(BF16)


def _mm_out(ya, yb, x2, gm, w, g, sh, sc, seq, tm=512, sub=256):
    m, d = x2.shape
    per_b = seq // tm
    brow = pl.BlockSpec((1, 1, d), lambda i: (i // per_b, 0, 0))
    tile = pl.BlockSpec((tm, d), lambda i: (i, 0))
    return pl.pallas_call(
        functools.partial(_mm_out_body, sub=sub),
        grid=(m // tm,),
        in_specs=[pl.BlockSpec((tm, ya.shape[1]), lambda i: (i, 0)),
                  pl.BlockSpec((tm, yb.shape[1]), lambda i: (i, 0)),
                  tile, brow,
                  pl.BlockSpec(w.shape, lambda i: (0, 0)),
                  pl.BlockSpec((1, d), lambda i: (0, 0)), brow, brow],
        out_specs=[tile, tile],
        out_shape=[jax.ShapeDtypeStruct((m, d), F32), jax.ShapeDtypeStruct((m, d), BF16)],
        compiler_params=_params("parallel"),
        name="mm_out",
    )(ya, yb, x2, gm, w, g, sh, sc)


def _ffn_body(x_ref, h_ref, gf_ref, wg_ref, wu_ref, wd_ref, fg_ref, o_ref, acc_ref):
    f = pl.program_id(1)

    @pl.when(f == 0)
    def _():
        acc_ref[...] = jnp.zeros_like(acc_ref)

    h = h_ref[...]
    gate = jnp.dot(h, wg_ref[...], preferred_element_type=F32)
    up = jnp.dot(h, wu_ref[...], preferred_element_type=F32)
    act = (gate * jax.nn.sigmoid(gate) * up).astype(BF16)
    acc_ref[...] += jnp.dot(act, wd_ref[...], preferred_element_type=F32)

    @pl.when(f == pl.num_programs(1) - 1)
    def _():
        y = x_ref[...] + gf_ref[0] * acc_ref[...]
        o_ref[...] = (y * lax.rsqrt(jnp.mean(y * y, axis=-1, keepdims=True) + RMS_EPS)
                      * fg_ref[...])


def _ffn(x1, h2, gf, w_gu, w_down, fg, seq, tm=512, tf=512):
    m, d = x1.shape
    dff = w_down.shape[0]
    nf = dff // tf
    assert seq % tm == 0 and dff % tf == 0, "row tiles must not straddle sequences"
    per_b = seq // tm
    tile = pl.BlockSpec((tm, d), lambda i, f: (i, 0))
    prow = pl.BlockSpec((1, d), lambda i, f: (0, 0))
    return pl.pallas_call(
        _ffn_body,
        grid=(m // tm, nf),
        in_specs=[tile, tile,
                  pl.BlockSpec((1, 1, d), lambda i, f: (i // per_b, 0, 0)),
                  pl.BlockSpec((d, tf), lambda i, f: (0, f)),
                  pl.BlockSpec((d, tf), lambda i, f: (0, nf + f)),
                  pl.BlockSpec((tf, d), lambda i, f: (f, 0)),
                  prow],
        out_specs=tile,
        out_shape=jax.ShapeDtypeStruct((m, d), F32),
        scratch_shapes=[pltpu.VMEM((tm, d), F32)],
        compiler_params=_params("parallel", "arbitrary"),
        name="ffn",
    )(x1, h2, gf, w_gu, w_gu, w_down, fg)


def _pad_cols(w, n):
    return jnp.pad(w, ((0, 0), (0, n - w.shape[1])))


def _pad_rows(w, n):
    return jnp.pad(w, ((0, n - w.shape[0]), (0, 0)))


def kernel(x, c, w_ada, b_ada, norm_mix_g, w_in, conv_w, conv_b, lru_wa, lru_ba, lru_wx, lru_bx, lru_lambda, rwkv_mu, rwkv_w0, rwkv_w2, rwkv_a0, rwkv_a2, rwkv_g2, rwkv_k_k, rwkv_k_a, rwkv_r_k, rwkv_ln_g, rwkv_ln_b, w_out, norm_ffn_g, w_gu, w_down, final_norm_g):
    bsz, seq, d = x.shape
    depth = w_ada.shape[0]
    dl = conv_w.shape[2]
    dr = rwkv_w0.shape[1]
    w_lora, a_lora, g_lora = rwkv_w2.shape[1], rwkv_a2.shape[1], rwkv_g2.shape[1]
    wpad, apad = LANE, LANE
    gpad = -(-g_lora // LANE) * LANE
    rkv_col0 = 2 * dl
    lora0 = rkv_col0 + 3 * dr

    x2 = x.reshape(bsz * seq, d)
    for l in range(depth):
        mod = _mod(c, w_ada[l], b_ada[l].reshape(1, -1))
        sh_m, sc_m, g_m, sh_f, sc_f, g_f = [t.reshape(bsz, 1, d) for t in jnp.split(mod, 6, axis=-1)]

        wi = jnp.swapaxes(w_in[l], 0, 1)
        o1, o2 = lora0 + w_lora, lora0 + w_lora + a_lora
        w_lora_p = jnp.concatenate(
            [_pad_rows(wi[lora0:o1], wpad), _pad_rows(wi[o1:o2], apad),
             _pad_rows(wi[o2:], gpad)], axis=0)
        mu = rwkv_mu[l].reshape(1, -1)
        mu_rkv = mu[:, :3 * dr]
        mu_lora = jnp.concatenate(
            [_pad_cols(mu[:, 3 * dr:3 * dr + w_lora], wpad),
             _pad_cols(mu[:, 3 * dr + w_lora:3 * dr + w_lora + a_lora], apad),
             _pad_cols(mu[:, 3 * dr + w_lora + a_lora:], gpad)], axis=1)
        w2p = _pad_rows(rwkv_w2[l], wpad)
        a2p = _pad_rows(rwkv_a2[l], apad)
        g2p = _pad_rows(rwkv_g2[l], gpad)

        h, p_lora = _norm(x2, norm_mix_g[l].reshape(1, d), sh_m, sc_m, w_lora_p, seq)
        tn_in = dl
        p = _mm_in(h, wi, lora0, tn=tn_in, gelu_tile=dl // tn_in)

        rowv = lambda t: t.reshape(1, dr)
        rp, yp, mc, nm, bonus, gg, y_a, w_out_b, w_gu_b, w_down_b = _rwkv_a(
            p, p_lora, mu_rkv, mu_lora, rowv(rwkv_w0[l]), rowv(rwkv_a0[l]), rowv(rwkv_k_k[l]),
            rowv(rwkv_k_a[l]), rowv(rwkv_r_k[l]), w2p, a2p, g2p,
            conv_w[l], conv_b[l], lru_wa[l].astype(BF16), lru_wx[l].astype(BF16),
            lru_ba[l], lru_bx[l], lru_lambda[l], bsz, seq, rkv_col0,
            cast_ws=(w_out[l], w_gu[l], w_down[l]))
        y_b = _rwkv_b(rp, yp, mc, nm, bonus, gg, rowv(rwkv_ln_g[l]), rowv(rwkv_ln_b[l]), bsz, seq)

        x2, h2 = _mm_out(y_a, y_b, x2, g_m, w_out_b, norm_ffn_g[l].reshape(1, d), sh_f, sc_f, seq)

        last = l == depth - 1
        fg = final_norm_g.reshape(1, d) if last else None
        assert last, "only the final layer carries the closing RMSNorm"
        x2 = _ffn(x2, h2, g_f, w_gu_b, w_down_b, fg, seq)
    return x2.reshape(bsz, seq, d)
```

```python
import functools
import math

import jax
import jax.numpy as jnp
from jax import lax
from jax.experimental import pallas as pl
from jax.experimental.pallas import tpu as pltpu

F32 = jnp.float32
BF16 = jnp.bfloat16

LRU_HEADS = 4
CONV_WIDTH = 4
LRU_C = 8.0
HEAD = 64
CHUNK = 64
PAIR = 2 * HEAD
HEADS_PER_STEP = 16
ONES_WIDTH = 256
ONES_WIDTH_B = 128
CHUNKS_PER_STEP = 4
RWKV_B_CHUNKS = 8
MOD_DMA_BANDS = 4
RMS_EPS = 1e-6
GN_EPS = 64e-5
L2_EPS = 1e-12
DECAY_SCALE = -math.exp(-0.5)
LANE = 128
SUBLANE = 8
BF16_SUBLANE = 16
VMEM_LIMIT = 56 * 1024 * 1024


def _params(*sem):
    return pltpu.CompilerParams(dimension_semantics=sem, vmem_limit_bytes=VMEM_LIMIT)


_NN = (((1,), (0,)), ((), ()))
_NT = (((1,), (1,)), ((), ()))
_TN = (((0,), (0,)), ((), ()))


def _dg(a, b, dims):
    return lax.dot_general(a, b, dims, preferred_element_type=F32)


def _split(x):
    hi = x.astype(BF16)
    lo = (x - hi.astype(F32)).astype(BF16)
    return hi, lo


def _mm3(a, b, dims=_NN):
    ah, al = _split(a)
    bh, bl = _split(b)
    return _dg(ah, bh, dims) + (_dg(ah, bl, dims) + _dg(al, bh, dims))


def _head_sums(x, ones_h):
    n = ones_h.shape[0]
    xb = x.astype(BF16)
    return jnp.concatenate([_dg(xb[:, c:c + n], ones_h, _NN) for c in range(0, x.shape[1], n)],
                           axis=1)


def _mm2_exact_lhs(a_bf16, b):
    bh, bl = _split(b)
    return _dg(a_bf16, bh, _NN) + _dg(a_bf16, bl, _NN)


def _softplus(x):
    return jnp.maximum(x, 0.0) + jnp.log1p(jnp.exp(-jnp.abs(x)))


def _iota2(shape):
    return (lax.broadcasted_iota(jnp.int32, shape, 0),
            lax.broadcasted_iota(jnp.int32, shape, 1))


def _head_ones(n):
    r, c = _iota2((n, n))
    return jnp.where((r // HEAD) == (c // HEAD), 1.0, 0.0).astype(BF16)


def _mod_body(c_ref, *refs):
    w_refs, b_ref, o_ref = refs[:-2], refs[-2], refs[-1]
    c = c_ref[...]
    ca = c * jax.nn.sigmoid(c)
    kb = w_refs[0].shape[0]
    acc = b_ref[...]
    for s, w_ref in enumerate(w_refs):
        acc = acc + _mm3(ca[:, s * kb:(s + 1) * kb], w_ref[...])
    o_ref[...] = acc


def _mod(c, w, b, tn=1024, bands=MOD_DMA_BANDS):
    bsz, d = c.shape
    n = w.shape[1]
    kb = d // bands
    return pl.pallas_call(
        _mod_body,
        grid=(n // tn,),
        in_specs=[pl.BlockSpec((bsz, d), lambda j: (0, 0))]
                 + [pl.BlockSpec((kb, tn), lambda j, s=s: (s, j)) for s in range(bands)]
                 + [pl.BlockSpec((1, tn), lambda j: (0, j))],
        out_specs=pl.BlockSpec((bsz, tn), lambda j: (0, j)),
        out_shape=jax.ShapeDtypeStruct((bsz, n), F32),
        compiler_params=_params("parallel"),
        name="mod",
    )(c, *([w] * bands), b)


def _norm_mod(x, g, sh, sc):
    y = x * lax.rsqrt(jnp.mean(x * x, axis=-1, keepdims=True) + RMS_EPS) * g
    return y * (1.0 + sc) + sh


def _norm_body(*refs):
    x_refs = refs[:-7]
    g_ref, sh_ref, sc_ref, w_ref, o_ref, pl_ref, wb_ref = refs[-7:]

    @pl.when(pl.program_id(0) == 0)
    def _():
        wb_ref[...] = w_ref[...].astype(BF16)

    x = jnp.concatenate([r[...] for r in x_refs], axis=1)
    h = _norm_mod(x, g_ref[...], sh_ref[0], sc_ref[0]).astype(BF16)
    o_ref[...] = h
    pl_ref[...] = _dg(h, wb_ref[...], _NT)


def _norm(x2, g, sh, sc, w_lora_t, seq, tm=512, bands=MOD_DMA_BANDS):
    m, d = x2.shape
    nl = w_lora_t.shape[0]
    per_b = seq // tm
    return pl.pallas_call(
        _norm_body,
        grid=(m // tm,),
        in_specs=[pl.BlockSpec((tm, d // bands), lambda i, s=s: (i, s)) for s in range(bands)]
                 + [pl.BlockSpec((1, d), lambda i: (0, 0)),
                  pl.BlockSpec((1, 1, d), lambda i: (i // per_b, 0, 0)),
                  pl.BlockSpec((1, 1, d), lambda i: (i // per_b, 0, 0)),
                  pl.BlockSpec((nl, d), lambda i: (0, 0))],
        out_specs=[pl.BlockSpec((tm, d), lambda i: (i, 0)), pl.BlockSpec((tm, nl), lambda i: (i, 0))],
        out_shape=[jax.ShapeDtypeStruct((m, d), BF16), jax.ShapeDtypeStruct((m, nl), F32)],
        scratch_shapes=[pltpu.VMEM((nl, d), BF16)],
        compiler_params=_params("arbitrary"),
        name="norm_mix",
    )(*([x2] * bands), g, sh, sc, w_lora_t)


def _mm_in_body(h_ref, w_ref, o_ref, wb_ref, *, gelu_tile):
    @pl.when(pl.program_id(1) == 0)
    def _():
        wb_ref[...] = w_ref[...].astype(BF16)

    if gelu_tile is None:
        o_ref[...] = _dg(h_ref[...], wb_ref[...], _NT)
    else:
        @pl.when(pl.program_id(0) == gelu_tile)
        def _():
            o_ref[...] = jax.nn.gelu(_dg(h_ref[...], wb_ref[...], _NT))

        @pl.when(pl.program_id(0) != gelu_tile)
        def _():
            o_ref[...] = _dg(h_ref[...], wb_ref[...], _NT)


def _mm_in(h, wt, ncols, tm=1024, tn=1024, name="mm_in", gelu_tile=None):
    m, d = h.shape
    return pl.pallas_call(
        functools.partial(_mm_in_body, gelu_tile=gelu_tile),
        grid=(ncols // tn, m // tm),
        in_specs=[pl.BlockSpec((tm, d), lambda j, i: (i, 0)),
                  pl.BlockSpec((tn, d), lambda j, i: (j, 0))],
        out_specs=pl.BlockSpec((tm, tn), lambda j, i: (i, j)),
        out_shape=jax.ShapeDtypeStruct((m, ncols), F32),
        scratch_shapes=[pltpu.VMEM((tn, d), BF16)],
        compiler_params=_params("parallel", "arbitrary"),
        name=name,
    )(h, wt)


def _lru_head(h, first, u_ref, gate_ref, halo_ref, cw_ref, cb_ref, wa_ref, wx_ref, ba_ref, bx_ref,
              lam_ref, o_ref, carry_ref):
    tt = u_ref.shape[0]
    hd = u_ref.shape[1] // LRU_HEADS
    cs = slice(h * hd, (h + 1) * hd)
    p = u_ref[:, cs]
    halo = jnp.where(first, 0.0, halo_ref[:, cs])
    ext = jnp.concatenate([halo, p], axis=0)
    cw = cw_ref[:, cs]
    u = cb_ref[:, cs] + p * cw[CONV_WIDTH - 1:CONV_WIDTH, :]
    for j in range(1, CONV_WIDTH):
        shifted = pltpu.roll(ext, j, 0)[SUBLANE:, :]
        u = u + shifted * cw[CONV_WIDTH - 1 - j:CONV_WIDTH - j, :]
    ub = u.astype(BF16)
    ra = jnp.dot(ub, wa_ref[h], preferred_element_type=F32)
    rx = jnp.dot(ub, wx_ref[h], preferred_element_type=F32)
    yield
    r = jax.nn.sigmoid(ra + ba_ref[:, cs])
    ig = jax.nn.sigmoid(rx + bx_ref[:, cs])
    a = jnp.exp(r * ((-LRU_C) * _softplus(-lam_ref[:, cs])))
    mult = jnp.sqrt(1.0 - a * a)
    row = lax.broadcasted_iota(jnp.int32, (tt, hd), 0)
    mult = jnp.where(jnp.logical_and(first, row == 0), 1.0, mult)
    b = mult * (ig * u)

    groups = tt // SUBLANE
    a3 = a.reshape(groups, SUBLANE, hd)
    b3 = b.reshape(groups, SUBLANE, hd)
    sub = lax.broadcasted_iota(jnp.int32, (groups, SUBLANE, hd), 1)
    s = 1
    while s < SUBLANE:
        keep = sub >= s
        a_s = jnp.where(keep, pltpu.roll(a3, s, 1), 1.0)
        b_s = jnp.where(keep, pltpu.roll(b3, s, 1), 0.0)
        b3 = a3 * b_s + b3
        a3 = a3 * a_s
        s *= 2
    yield
    gate = gate_ref[:, cs]
    carry = carry_ref[:, cs]
    outs = []
    for g in range(groups):
        hh = b3[g] + a3[g] * carry
        carry = hh[SUBLANE - 1:SUBLANE, :]
        outs.append(hh * gate[g * SUBLANE:(g + 1) * SUBLANE, :])
    per = BF16_SUBLANE // SUBLANE
    for t0 in range(0, groups, per):
        o_ref[t0 * SUBLANE:(t0 + per) * SUBLANE, cs] = jnp.concatenate(
            outs[t0:t0 + per], axis=0).astype(BF16)
    carry_ref[:, cs] = carry
    yield


def _token_shift(x, halo, mu, first, row):
    prev = jnp.where(first, 0.0, halo[SUBLANE - 1:SUBLANE, :])
    xs = jnp.where(row == 0, prev, pltpu.roll(x, 1, 0))
    return x + (xs - x) * mu


def _mm1(a, b, dims=_NN):
    return _dg(a.astype(BF16), b.astype(BF16), dims)


def _pair_diag(y, left):
    return jnp.concatenate([jnp.where(left, y, 0.0), jnp.where(left, 0.0, y)], axis=0).astype(BF16)


def _pair_mm(x, y, left):
    return _dg(x.astype(BF16), _pair_diag(y, left), _NN)


def _chunk_chain(ops, store):
    ab_, bb_, kb_, rb_, v_, bt_, kt_, pe_ = ops
    rc, lane = _iota2((CHUNK, PAIR))
    cc = lane % HEAD
    left = lane < HEAD
    strict = rc > cc
    incl = rc >= cc
    diag = rc == cc
    ar16 = [jnp.concatenate([x, y], axis=0).astype(BF16) for x, y in zip(ab_, rb_)]
    bd_b = [_pair_diag(x, left) for x in bb_]
    bd_k = [_pair_diag(x, left) for x in kb_]
    bd_v = [_pair_diag(x, left) for x in v_]
    arb = [_dg(x, y, _NT) for x, y in zip(ar16, bd_b)]
    ark = [_dg(x, y, _NT) for x, y in zip(ar16, bd_k)]
    a_ab = [jnp.where(strict, x[:CHUNK], 0.0) for x in arb]
    a_rb = [jnp.where(incl, x[CHUNK:], 0.0).astype(BF16) for x in arb]
    a_akrk = [jnp.concatenate([jnp.where(strict, x[:CHUNK], 0.0), jnp.where(incl, x[CHUNK:], 0.0)],
                              axis=0).astype(BF16) for x in ark]
    yield
    base = 8
    d = [jnp.where((rc // base) == (cc // base), a, 0.0) for a in a_ab]
    d2 = [_pair_mm(t, t, left) for t in d]
    akrkv = [_dg(x, y, _NN) for x, y in zip(a_akrk, bd_v)]
    akv = [x[:CHUNK] for x in akrkv]
    rkv = [x[CHUNK:] for x in akrkv]
    x = [jnp.where(diag, 1.0, 0.0) + t for t in d]
    yield
    x = [xi + _pair_mm(t2, xi, left) for xi, t2 in zip(x, d2)]
    d4 = [_pair_mm(t2, t2, left) for t2 in d2]
    yield
    x = [xi + _pair_mm(t4, xi, left) for xi, t4 in zip(x, d4)]
    yield
    size = base
    while size < CHUNK:
        off = jnp.logical_and((rc // (2 * size)) == (cc // (2 * size)),
                              (rc // size) != (cc // size))
        o = [jnp.where(off, a, 0.0) for a in a_ab]
        ox = [_pair_mm(oi, xi, left) for oi, xi in zip(o, x)]
        yield
        x = [xi + _pair_mm(xi, oxi, left) for xi, oxi in zip(x, ox)]
        yield
        size *= 2
    t = [xi.astype(BF16) for xi in x]
    wu = [_dg(ti, jnp.concatenate([_pair_diag(y, left), _pair_diag(z, left)], axis=1), _NN)
          for ti, y, z in zip(t, ab_, akv)]
    kv = [_dg(xi.astype(BF16), y.astype(BF16), _TN) for xi, y in zip(kt_, v_)]
    yield
    ry = [_dg(xi, jnp.concatenate([_pair_diag(y[:, :PAIR], left), _pair_diag(y[:, PAIR:], left)], axis=1), _NN)
          for xi, y in zip(a_rb, wu)]
    mn = [_dg(xi.astype(BF16), y.astype(BF16), _TN) for xi, y in zip(bt_, wu)]
    yield

    def head_blocks(z):
        return jnp.where(left, z[:HEAD, :], z[HEAD:, :])

    for u in range(len(ab_)):
        store(u,
              rb_[u] + ry[u][:, :PAIR],
              ry[u][:, PAIR:] + rkv[u],
              jnp.where(diag, pe_[u], 0.0) + head_blocks(mn[u][:, :PAIR]),
              head_blocks(mn[u][:, PAIR:]) + head_blocks(kv[u]))


def _rwkv_a_body(r_ref, k_ref, v_ref, l_ref, rh_ref, kh_ref, vh_ref, lh_ref,
                 mur_ref, muk_ref, muv_ref, mul_ref, w0_ref, a0_ref, kkw_ref, kaw_ref, rkw_ref,
                 w2_ref, a2_ref, g2_ref, ones_ref, tri_ref,
                 u_ref, gate_ref, halo_ref, cw_ref, cb_ref, wa_ref, wx_ref, ba_ref, bx_ref, lam_ref,
                 *rest):
    ncast = (len(rest) - 8) // 2
    cast_in = rest[:ncast]
    rp_ref, yp_ref, m_ref, n_ref, bonus_ref, g_ref, ya_ref = rest[ncast:ncast + 7]
    cast_out = rest[ncast + 7:-1]
    carry_ref = rest[-1]
    first = pl.program_id(1) == 0

    @pl.when(first)
    def _():
        carry_ref[...] = jnp.zeros_like(carry_ref)
    cl = CHUNK
    rows = CHUNKS_PER_STEP * cl
    width = HEADS_PER_STEP * HEAD
    gw = ones_ref.shape[0]
    row_g = lax.broadcasted_iota(jnp.int32, (rows, gw), 0)
    row_l = lax.broadcasted_iota(jnp.int32, (rows, l_ref.shape[1]), 0)
    ones_h = ones_ref[...]

    lo = _token_shift(l_ref[...], lh_ref[...], mul_ref[...], first, row_l)
    act_w = _split(jnp.tanh(lo[:, 0:LANE]))
    act_a = _split(lo[:, LANE:2 * LANE])
    act_g = _split(jax.nn.sigmoid(lo[:, 2 * LANE:]))

    def lora(act, w_ref, cs, keep_low):
        (ah, al_), wb = act, w_ref[:, cs]
        out = _dg(ah, wb, _NN)
        return out + _dg(al_, wb, _NN) if keep_low else out

    def prologue(c0, out):
        cs = slice(c0, c0 + gw)
        r = _token_shift(r_ref[:, cs], rh_ref[:, cs], mur_ref[:, cs], first, row_g)
        k = _token_shift(k_ref[:, cs], kh_ref[:, cs], muk_ref[:, cs], first, row_g)
        v = _token_shift(v_ref[:, cs], vh_ref[:, cs], muv_ref[:, cs], first, row_g)
        w_lin = w0_ref[:, cs] + lora(act_w, w2_ref, cs, True)
        a_lin = a0_ref[:, cs] + lora(act_a, a2_ref, cs, False)
        g_ref[:, cs] = lora(act_g, g2_ref, cs, False).astype(BF16)
        kk = k * kkw_ref[:, cs]
        kk_ss = _head_sums(kk * kk, ones_h)
        yield
        lw = DECAY_SCALE * jax.nn.sigmoid(w_lin)
        a = jax.nn.sigmoid(a_lin)
        kk = kk * lax.rsqrt(jnp.maximum(kk_ss, L2_EPS * L2_EPS))
        kp = k * (1.0 + (a - 1.0) * kaw_ref[:, cs])
        bonus_ref[:, cs] = (_head_sums(r * kp * rkw_ref[:, cs], ones_h) * v).astype(BF16)
        lc = _mm2_exact_lhs(tri_ref[...], lw)
        yield
        p_incl = jnp.exp(lc)
        p_excl = jnp.exp(lc - lw)
        p_inv = 1.0 / p_incl
        p_end = jnp.concatenate(
            [jnp.broadcast_to(p_incl[(j + 1) * cl - 1:(j + 1) * cl, :], (cl, gw))
             for j in range(CHUNKS_PER_STEP)], axis=0)
        abar = -(kk * p_excl)
        bbar = kk * a * p_inv
        kbar = kp * p_inv
        rbar = r * p_incl
        btil = bbar * p_end
        ktil = kbar * p_end
        units = [(j, q) for j in range(CHUNKS_PER_STEP) for q in range(gw // PAIR)]
        out.extend([x[j * cl:(j + 1) * cl, q * PAIR:(q + 1) * PAIR] for j, q in units]
                   for x in (abar, bbar, kbar, rbar, v, btil, ktil, p_end))
        yield

    def make_store(c0):
        units = [(j, q) for j in range(CHUNKS_PER_STEP) for q in range(gw // PAIR)]

        def store(u, rp, yp, mm, nn):
            j, q = units[u]
            rs = slice(j * cl, (j + 1) * cl)
            qs = slice(c0 + q * PAIR, c0 + (q + 1) * PAIR)
            rp_ref[rs, qs] = rp.astype(BF16)
            yp_ref[rs, qs] = yp.astype(BF16)
            m_ref[rs, qs] = mm.astype(BF16)
            n_ref[rs, qs] = nn.astype(BF16)
        return store

    lru = (None for h in range(LRU_HEADS)
           for _ in _lru_head(h, first, u_ref, gate_ref, halo_ref, cw_ref, cb_ref, wa_ref, wx_ref,
                              ba_ref, bx_ref, lam_ref, ya_ref, carry_ref))
    chains = []
    for c0 in range(0, width, gw):
        ops = []
        for _ in prologue(c0, ops):
            for ch in chains:
                next(ch, None)
        chains.append(_chunk_chain(ops, make_store(c0)))
    live = list(chains)
    while live:
        live = [ch for ch in live if next(ch, StopIteration) is not StopIteration]
        next(lru, None)
    for _ in lru:
        pass

    for src, dst in zip(cast_in, cast_out):
        dst[...] = src[...].astype(BF16)


def _rwkv_a(p, p_lora, mu_rkv, mu_lora, w0, a0, k_k, k_a, r_k, w2p, a2p, g2p,
            conv_w, conv_b, wa, wx, ba, bx, lam, bsz, seq, rkv_col0, cast_ws=()):
    dl = conv_w.shape[1]
    lvec = lambda t: t.reshape(1, dl)
    lrow = pl.BlockSpec((1, dl), lambda b, i, q: (0, 0))

    def lru_tile(cb):
        return pl.BlockSpec((CHUNKS_PER_STEP * CHUNK, dl), lambda b, i, q: (b * nc + i, cb))
    cl = CHUNKS_PER_STEP * CHUNK
    width = HEADS_PER_STEP * HEAD
    dr = w0.shape[1]
    ngroups = dr // width
    assert ngroups == 1, "the LRU ride-along expects one grid step per row tile"
    nc = seq // cl
    lw_ = mu_lora.shape[1]
    cb0 = rkv_col0 // width
    rows8 = cl // SUBLANE
    rt, ct = _iota2((cl, cl))
    tri = jnp.where(jnp.logical_and(rt >= ct, (rt // CHUNK) == (ct // CHUNK)), 1.0, 0.0).astype(BF16)
    ones_h = _head_ones(ONES_WIDTH)
    const = lambda arr: pl.BlockSpec(arr.shape, lambda b, i, q: (0, 0))
    lora_w = [wgt.astype(BF16) for wgt in (w2p, a2p, g2p)]

    def tile(cb_off):
        return pl.BlockSpec((cl, width), lambda b, i, q: (b * nc + i, cb0 + cb_off + q))

    def halo(cb_off):
        return pl.BlockSpec(
            (SUBLANE, width),
            lambda b, i, q: (jnp.maximum((b * nc + i) * rows8 - 1, 0), cb0 + cb_off + q))

    def prow(off=0):
        return pl.BlockSpec((1, width), lambda b, i, q: (0, off + q))

    out_tile = pl.BlockSpec((cl, width), lambda b, i, q: (b * nc + i, q))
    out_mat = pl.BlockSpec((CHUNKS_PER_STEP * HEAD, width), lambda b, i, q: (b * nc + i, q))
    act = jax.ShapeDtypeStruct((bsz * seq, dr), BF16)
    mat = jax.ShapeDtypeStruct((bsz * (seq // CHUNK) * HEAD, dr), BF16)

    nsteps = bsz * nc * ngroups
    cast_specs = []
    for wgt in cast_ws:
        hold = 1
        while (wgt.shape[0] * hold) % (nsteps * BF16_SUBLANE) != 0:
            hold *= 2
        blk = (wgt.shape[0] * hold // nsteps, wgt.shape[1])
        cast_specs.append(pl.BlockSpec(
            blk, lambda b, i, q, hold=hold: (((b * nc + i) * ngroups + q) // hold, 0)))
    cast_shapes = [jax.ShapeDtypeStruct(wgt.shape, BF16) for wgt in cast_ws]

    return pl.pallas_call(
        _rwkv_a_body,
        grid=(bsz, nc, ngroups),
        in_specs=[tile(0), tile(ngroups), tile(2 * ngroups),
                  pl.BlockSpec((cl, lw_), lambda b, i, q: (b * nc + i, 0)),
                  halo(0), halo(ngroups), halo(2 * ngroups),
                  pl.BlockSpec((SUBLANE, lw_),
                               lambda b, i, q: (jnp.maximum((b * nc + i) * rows8 - 1, 0), 0)),
                  prow(0), prow(ngroups), prow(2 * ngroups),
                  pl.BlockSpec((1, lw_), lambda b, i, q: (0, 0)),
                  prow(), prow(), prow(), prow(), prow()]
                 + [pl.BlockSpec((t.shape[0], width), lambda b, i, q: (0, q)) for t in lora_w]
                 + [const(ones_h), const(tri)]
                 + [lru_tile(0), lru_tile(1),
                    pl.BlockSpec((SUBLANE, dl),
                                 lambda b, i, q: (jnp.maximum((b * nc + i) * rows8 - 1, 0), 0)),
                    const(conv_w), lrow, pl.BlockSpec(wa.shape, lambda b, i, q: (0, 0, 0)),
                    pl.BlockSpec(wx.shape, lambda b, i, q: (0, 0, 0)), lrow, lrow, lrow]
                 + cast_specs,
        out_specs=[out_tile, out_tile, out_mat, out_mat, out_tile, out_tile, lru_tile(0)] + cast_specs,
        out_shape=[act, act, mat, mat, act, act, jax.ShapeDtypeStruct((bsz * seq, dl), BF16)]
                  + cast_shapes,
        scratch_shapes=[pltpu.VMEM((1, dl), F32)],
        compiler_params=_params("arbitrary", "arbitrary", "arbitrary"),
        name="rwkv_a",
    )(p, p, p, p_lora, p, p, p, p_lora, mu_rkv, mu_rkv, mu_rkv, mu_lora, w0, a0, k_k, k_a, r_k,
      *lora_w, ones_h, tri,
      p, p, p, conv_w, lvec(conv_b), wa, wx, lvec(ba), lvec(bx), lvec(lam), *cast_ws)


def _rwkv_b_body(rp_ref, yp_ref, m_ref, n_ref, bonus_ref, g_ref, lng_ref, lnb_ref, ones_ref,
                 o_ref, state_ref):
    @pl.when(pl.program_id(1) == 0)
    def _():
        state_ref[...] = jnp.zeros_like(state_ref)

    npairs = state_ref.shape[0]
    pairs = range(npairs)
    ps = [slice(q * PAIR, (q + 1) * PAIR) for q in pairs]
    left = lax.broadcasted_iota(jnp.int32, (HEAD, PAIR), 1) < HEAD
    ones_h = ones_ref[...]
    inv_n = 1.0 / HEAD
    state = [state_ref[q] for q in pairs]
    for j in range(rp_ref.shape[0] // CHUNK):
        rs = slice(j * CHUNK, (j + 1) * CHUNK)
        ks = slice(j * HEAD, (j + 1) * HEAD)
        g0 = [_pair_diag(state[q], left) for q in pairs]
        ys = [_dg(rp_ref[rs, ps[q]].astype(BF16), g0[q], _NN) + yp_ref[rs, ps[q]] for q in pairs]
        state = [_dg(m_ref[ks, ps[q]].astype(BF16), g0[q], _NN) + n_ref[ks, ps[q]] for q in pairs]
        y = jnp.concatenate(ys, axis=1)
        yc = y - _head_sums(y, ones_h) * inv_n
        var = _head_sums(yc * yc, ones_h) * inv_n
        yn = yc * lax.rsqrt(var + GN_EPS) * lng_ref[...] + lnb_ref[...]
        o_ref[rs, :] = ((yn + bonus_ref[rs, :]) * g_ref[rs, :]).astype(BF16)
    for q in pairs:
        state_ref[q] = state[q]


def _rwkv_b(rp, yp, mc, nm, bonus, g, ln_g, ln_b, bsz, seq):
    cl = RWKV_B_CHUNKS * CHUNK
    dr = rp.shape[1]
    nc = seq // cl
    tile = pl.BlockSpec((cl, dr), lambda b, i: (b * nc + i, 0))
    mat = pl.BlockSpec((RWKV_B_CHUNKS * HEAD, dr), lambda b, i: (b * nc + i, 0))
    prow = pl.BlockSpec((1, dr), lambda b, i: (0, 0))
    ones_h = _head_ones(ONES_WIDTH_B)
    return pl.pallas_call(
        _rwkv_b_body,
        grid=(bsz, nc),
        in_specs=[tile, tile, mat, mat, tile, tile, prow, prow,
                  pl.BlockSpec(ones_h.shape, lambda b, i: (0, 0))],
        out_specs=tile,
        out_shape=jax.ShapeDtypeStruct((bsz * seq, dr), BF16),
        scratch_shapes=[pltpu.VMEM((dr // PAIR, HEAD, PAIR), F32)],
        compiler_params=_params("parallel", "arbitrary"),
        name="rwkv_b",
    )(rp, yp, mc, nm, bonus, g, ln_g, ln_b, ones_h)


def _mm_out_body(ya_ref, yb_ref, x_ref, gm_ref, w_ref, g_ref, sh_ref, sc_ref, o_ref, h_ref, *, sub):
    da = ya_ref.shape[1]
    for r0 in range(0, x_ref.shape[0], sub):
        rs = slice(r0, r0 + sub)
        mix = (jnp.dot(ya_ref[rs, :].astype(BF16), w_ref[:da, :], preferred_element_type=F32)
               + jnp.dot(yb_ref[rs, :].astype(BF16), w_ref[da:, :], preferred_element_type=F32))
        x1 = x_ref[rs, :] + gm_ref[0] * mix
        o_ref[rs, :] = x1
        h_ref[rs, :] = _norm_mod(x1, g_ref[...], sh_ref[0], sc_ref[0]).astype(BF16)


def _mm_out(ya, yb, x2, gm, w, g, sh, sc, seq, tm=512, sub=256):
    m, d = x2.shape
    per_b = seq // tm
    brow = pl.BlockSpec((1, 1, d), lambda i: (i // per_b, 0, 0))
    tile = pl.BlockSpec((tm, d), lambda i: (i, 0))
    return pl.pallas_call(
        functools.partial(_mm_out_body, sub=sub),
        grid=(m // tm,),
        in_specs=[pl.BlockSpec((tm, ya.shape[1]), lambda i: (i, 0)),
                  pl.BlockSpec((tm, yb.shape[1]), lambda i: (i, 0)),
                  tile, brow,
                  pl.BlockSpec(w.shape, lambda i: (0, 0)),
                  pl.BlockSpec((1, d), lambda i: (0, 0)), brow, brow],
        out_specs=[tile, tile],
        out_shape=[jax.ShapeDtypeStruct((m, d), F32), jax.ShapeDtypeStruct((m, d), BF16)],
        compiler_params=_params("parallel"),
        name="mm_out",
    )(ya, yb, x2, gm, w, g, sh, sc)


def _ffn_body(x_ref, h_ref, gf_ref, wg_ref, wu_ref, wd_ref, fg_ref, o_ref, acc_ref):
    f = pl.program_id(1)

    @pl.when(f == 0)
    def _():
        acc_ref[...] = jnp.zeros_like(acc_ref)

    h = h_ref[...]
    gate = jnp.dot(h, wg_ref[...], preferred_element_type=F32)
    up = jnp.dot(h, wu_ref[...], preferred_element_type=F32)
    act = (gate * jax.nn.sigmoid(gate) * up).astype(BF16)
    acc_ref[...] += jnp.dot(act, wd_ref[...], preferred_element_type=F32)

    @pl.when(f == pl.num_programs(1) - 1)
    def _():
        y = x_ref[...] + gf_ref[0] * acc_ref[...]
        o_ref[...] = (y * lax.rsqrt(jnp.mean(y * y, axis=-1, keepdims=True) + RMS_EPS)
                      * fg_ref[...])


def _ffn(x1, h2, gf, w_gu, w_down, fg, seq, tm=512, tf=512):
    m, d = x1.shape
    dff = w_down.shape[0]
    nf = dff // tf
    assert seq % tm == 0 and dff % tf == 0, "row tiles must not straddle sequences"
    per_b = seq // tm
    tile = pl.BlockSpec((tm, d), lambda i, f: (i, 0))
    prow = pl.BlockSpec((1, d), lambda i, f: (0, 0))
    return pl.pallas_call(
        _ffn_body,
        grid=(m // tm, nf),
        in_specs=[tile, tile,
                  pl.BlockSpec((1, 1, d), lambda i, f: (i // per_b, 0, 0)),
                  pl.BlockSpec((d, tf), lambda i, f: (0, f)),
                  pl.BlockSpec((d, tf), lambda i, f: (0, nf + f)),
                  pl.BlockSpec((tf, d), lambda i, f: (f, 0)),
                  prow],
        out_specs=tile,
        out_shape=jax.ShapeDtypeStruct((m, d), F32),
        scratch_shapes=[pltpu.VMEM((tm, d), F32)],
        compiler_params=_params("parallel", "arbitrary"),
        name="ffn",
    )(x1, h2, gf, w_gu, w_gu, w_down, fg)


def _pad_cols(w, n):
    return jnp.pad(w, ((0, 0), (0, n - w.shape[1])))


def _pad_rows(w, n):
    return jnp.pad(w, ((0, n - w.shape[0]), (0, 0)))


def kernel(x, c, w_ada, b_ada, norm_mix_g, w_in, conv_w, conv_b, lru_wa, lru_ba, lru_wx, lru_bx, lru_lambda, rwkv_mu, rwkv_w0, rwkv_w2, rwkv_a0, rwkv_a2, rwkv_g2, rwkv_k_k, rwkv_k_a, rwkv_r_k, rwkv_ln_g, rwkv_ln_b, w_out, norm_ffn_g, w_gu, w_down, final_norm_g):
    bsz, seq, d = x.shape
    depth = w_ada.shape[0]
    dl = conv_w.shape[2]
    dr = rwkv_w0.shape[1]
    w_lora, a_lora, g_lora = rwkv_w2.shape[1], rwkv_a2.shape[1], rwkv_g2.shape[1]
    wpad, apad = LANE, LANE
    gpad = -(-g_lora // LANE) * LANE
    rkv_col0 = 2 * dl
    lora0 = rkv_col0 + 3 * dr

    x2 = x.reshape(bsz * seq, d)
    for l in range(depth):
        mod = _mod(c, w_ada[l], b_ada[l].reshape(1, -1))
        sh_m, sc_m, g_m, sh_f, sc_f, g_f = [t.reshape(bsz, 1, d) for t in jnp.split(mod, 6, axis=-1)]

        wi = jnp.swapaxes(w_in[l], 0, 1)
        o1, o2 = lora0 + w_lora, lora0 + w_lora + a_lora
        w_lora_p = jnp.concatenate(
            [_pad_rows(wi[lora0:o1], wpad), _pad_rows(wi[o1:o2], apad),
             _pad_rows(wi[o2:], gpad)], axis=0)
        mu = rwkv_mu[l].reshape(1, -1)
        mu_rkv = mu[:, :3 * dr]
        mu_lora = jnp.concatenate(
            [_pad_cols(mu[:, 3 * dr:3 * dr + w_lora], wpad),
             _pad_cols(mu[:, 3 * dr + w_lora:3 * dr + w_lora + a_lora], apad),
             _pad_cols(mu[:, 3 * dr + w_lora + a_lora:], gpad)], axis=1)
        w2p = _pad_rows(rwkv_w2[l], wpad)
        a2p = _pad_rows(rwkv_a2[l], apad)
        g2p = _pad_rows(rwkv_g2[l], gpad)

        h, p_lora = _norm(x2, norm_mix_g[l].reshape(1, d), sh_m, sc_m, w_lora_p, seq)
        tn_in = dl
        p = _mm_in(h, wi, lora0, tn=tn_in, gelu_tile=dl // tn_in)

        rowv = lambda t: t.reshape(1, dr)
        rp, yp, mc, nm, bonus, gg, y_a, w_out_b, w_gu_b, w_down_b = _rwkv_a(
            p, p_lora, mu_rkv, mu_lora, rowv(rwkv_w0[l]), rowv(rwkv_a0[l]), rowv(rwkv_k_k[l]),
            rowv(rwkv_k_a[l]), rowv(rwkv_r_k[l]), w2p, a2p, g2p,
            conv_w[l], conv_b[l], lru_wa[l].astype(BF16), lru_wx[l].astype(BF16),
            lru_ba[l], lru_bx[l], lru_lambda[l], bsz, seq, rkv_col0,
            cast_ws=(w_out[l], w_gu[l], w_down[l]))
        y_b = _rwkv_b(rp, yp, mc, nm, bonus, gg, rowv(rwkv_ln_g[l]), rowv(rwkv_ln_b[l]), bsz, seq)

        x2, h2 = _mm_out(y_a, y_b, x2, g_m, w_out_b, norm_ffn_g[l].reshape(1, d), sh_f, sc_f, seq)

        last = l == depth - 1
        fg = final_norm_g.reshape(1, d) if last else None
        assert last, "only the final layer carries the closing RMSNorm"
        x2 = _ffn(x2, h2, g_f, w_gu_b, w_down_b, fg, seq)
    return x2.reshape(bsz, seq, d)
```

```python
import functools
import math

import jax
import jax.numpy as jnp
from jax import lax
from jax.experimental import pallas as pl
from jax.experimental.pallas import tpu as pltpu

F32 = jnp.float32
BF16 = jnp.bfloat16

LRU_HEADS = 4
CONV_WIDTH = 4
LRU_C = 8.0
HEAD = 64
CHUNK = 64
PAIR = 2 * HEAD
HEADS_PER_STEP = 16
ONES_WIDTH = 256
ONES_WIDTH_B = 128
CHUNKS_PER_STEP = 4
RWKV_B_CHUNKS = 8
MOD_DMA_BANDS = 4
RMS_EPS = 1e-6
GN_EPS = 64e-5
L2_EPS = 1e-12
DECAY_SCALE = -math.exp(-0.5)
LANE = 128
SUBLANE = 8
BF16_SUBLANE = 16
VMEM_LIMIT = 56 * 1024 * 1024


def _params(*sem):
    return pltpu.CompilerParams(dimension_semantics=sem, vmem_limit_bytes=VMEM_LIMIT)


_NN = (((1,), (0,)), ((), ()))
_NT = (((1,), (1,)), ((), ()))
_TN = (((0,), (0,)), ((), ()))


def _dg(a, b, dims):
    return lax.dot_general(a, b, dims, preferred_element_type=F32)


def _split(x):
    hi = x.astype(BF16)
    lo = (x - hi.astype(F32)).astype(BF16)
    return hi, lo


def _mm3(a, b, dims=_NN):
    ah, al = _split(a)
    bh, bl = _split(b)
    return _dg(ah, bh, dims) + (_dg(ah, bl, dims) + _dg(al, bh, dims))


def _head_sums(x, ones_h):
    n = ones_h.shape[0]
    xb = x.astype(BF16)
    return jnp.concatenate([_dg(xb[:, c:c + n], ones_h, _NN) for c in range(0, x.shape[1], n)],
                           axis=1)


def _mm2_exact_lhs(a_bf16, b):
    bh, bl = _split(b)
    return _dg(a_bf16, bh, _NN) + _dg(a_bf16, bl, _NN)


def _softplus(x):
    return jnp.maximum(x, 0.0) + jnp.log1p(jnp.exp(-jnp.abs(x)))


def _iota2(shape):
    return (lax.broadcasted_iota(jnp.int32, shape, 0),
            lax.broadcasted_iota(jnp.int32, shape, 1))


def _head_ones(n):
    r, c = _iota2((n, n))
    return jnp.where((r // HEAD) == (c // HEAD), 1.0, 0.0).astype(BF16)


def _mod_body(c_ref, *refs):
    w_refs, b_ref, o_ref = refs[:-2], refs[-2], refs[-1]
    c = c_ref[...]
    ca = c * jax.nn.sigmoid(c)
    kb = w_refs[0].shape[0]
    acc = b_ref[...]
    for s, w_ref in enumerate(w_refs):
        acc = acc + _mm3(ca[:, s * kb:(s + 1) * kb], w_ref[...])
    o_ref[...] = acc


def _mod(c, w, b, tn=1024, bands=MOD_DMA_BANDS):
    bsz, d = c.shape
    n = w.shape[1]
    kb = d // bands
    return pl.pallas_call(
        _mod_body,
        grid=(n // tn,),
        in_specs=[pl.BlockSpec((bsz, d), lambda j: (0, 0))]
                 + [pl.BlockSpec((kb, tn), lambda j, s=s: (s, j)) for s in range(bands)]
                 + [pl.BlockSpec((1, tn), lambda j: (0, j))],
        out_specs=pl.BlockSpec((bsz, tn), lambda j: (0, j)),
        out_shape=jax.ShapeDtypeStruct((bsz, n), F32),
        compiler_params=_params("parallel"),
        name="mod",
    )(c, *([w] * bands), b)


def _norm_mod(x, g, sh, sc):
    y = x * lax.rsqrt(jnp.mean(x * x, axis=-1, keepdims=True) + RMS_EPS) * g
    return y * (1.0 + sc) + sh


def _norm_body(x_ref, g_ref, sh_ref, sc_ref, w_ref, o_ref, pl_ref, wb_ref):
    @pl.when(pl.program_id(0) == 0)
    def _():
        wb_ref[...] = w_ref[...].astype(BF16)

    h = _norm_mod(x_ref[...], g_ref[...], sh_ref[0], sc_ref[0]).astype(BF16)
    o_ref[...] = h
    pl_ref[...] = _dg(h, wb_ref[...], _NT)


def _norm(x2, g, sh, sc, w_lora_t, seq, tm=512):
    m, d = x2.shape
    nl = w_lora_t.shape[0]
    per_b = seq // tm
    return pl.pallas_call(
        _norm_body,
        grid=(m // tm,),
        in_specs=[pl.BlockSpec((tm, d), lambda i: (i, 0)),
                  pl.BlockSpec((1, d), lambda i: (0, 0)),
                  pl.BlockSpec((1, 1, d), lambda i: (i // per_b, 0, 0)),
                  pl.BlockSpec((1, 1, d), lambda i: (i // per_b, 0, 0)),
                  pl.BlockSpec((nl, d), lambda i: (0, 0))],
        out_specs=[pl.BlockSpec((tm, d), lambda i: (i, 0)), pl.BlockSpec((tm, nl), lambda i: (i, 0))],
        out_shape=[jax.ShapeDtypeStruct((m, d), BF16), jax.ShapeDtypeStruct((m, nl), F32)],
        scratch_shapes=[pltpu.VMEM((nl, d), BF16)],
        compiler_params=_params("arbitrary"),
        name="norm_mix",
    )(x2, g, sh, sc, w_lora_t)


def _mm_in_body(h_ref, w_ref, o_ref, wb_ref, *, gelu_tile):
    @pl.when(pl.program_id(1) == 0)
    def _():
        wb_ref[...] = w_ref[...].astype(BF16)

    if gelu_tile is None:
        o_ref[...] = _dg(h_ref[...], wb_ref[...], _NT)
    else:
        @pl.when(pl.program_id(0) == gelu_tile)
        def _():
            o_ref[...] = jax.nn.gelu(_dg(h_ref[...], wb_ref[...], _NT))

        @pl.when(pl.program_id(0) != gelu_tile)
        def _():
            o_ref[...] = _dg(h_ref[...], wb_ref[...], _NT)


def _mm_in(h, wt, ncols, tm=1024, tn=1024, name="mm_in", gelu_tile=None):
    m, d = h.shape
    return pl.pallas_call(
        functools.partial(_mm_in_body, gelu_tile=gelu_tile),
        grid=(ncols // tn, m // tm),
        in_specs=[pl.BlockSpec((tm, d), lambda j, i: (i, 0)),
                  pl.BlockSpec((tn, d), lambda j, i: (j, 0))],
        out_specs=pl.BlockSpec((tm, tn), lambda j, i: (i, j)),
        out_shape=jax.ShapeDtypeStruct((m, ncols), F32),
        scratch_shapes=[pltpu.VMEM((tn, d), BF16)],
        compiler_params=_params("parallel", "arbitrary"),
        name=name,
    )(h, wt)


def _lru_head(h, first, u_ref, gate_ref, halo_ref, cw_ref, cb_ref, wa_ref, wx_ref, ba_ref, bx_ref,
              lam_ref, o_ref, carry_ref):
    tt = u_ref.shape[0]
    hd = u_ref.shape[1] // LRU_HEADS
    cs = slice(h * hd, (h + 1) * hd)
    p = u_ref[:, cs]
    halo = jnp.where(first, 0.0, halo_ref[:, cs])
    ext = jnp.concatenate([halo, p], axis=0)
    cw = cw_ref[:, cs]
    u = cb_ref[:, cs] + p * cw[CONV_WIDTH - 1:CONV_WIDTH, :]
    for j in range(1, CONV_WIDTH):
        shifted = pltpu.roll(ext, j, 0)[SUBLANE:, :]
        u = u + shifted * cw[CONV_WIDTH - 1 - j:CONV_WIDTH - j, :]
    ub = u.astype(BF16)
    ra = jnp.dot(ub, wa_ref[h], preferred_element_type=F32)
    rx = jnp.dot(ub, wx_ref[h], preferred_element_type=F32)
    yield
    r = jax.nn.sigmoid(ra + ba_ref[:, cs])
    ig = jax.nn.sigmoid(rx + bx_ref[:, cs])
    a = jnp.exp(r * ((-LRU_C) * _softplus(-lam_ref[:, cs])))
    mult = jnp.sqrt(1.0 - a * a)
    row = lax.broadcasted_iota(jnp.int32, (tt, hd), 0)
    mult = jnp.where(jnp.logical_and(first, row == 0), 1.0, mult)
    b = mult * (ig * u)

    groups = tt // SUBLANE
    a3 = a.reshape(groups, SUBLANE, hd)
    b3 = b.reshape(groups, SUBLANE, hd)
    sub = lax.broadcasted_iota(jnp.int32, (groups, SUBLANE, hd), 1)
    s = 1
    while s < SUBLANE:
        keep = sub >= s
        a_s = jnp.where(keep, pltpu.roll(a3, s, 1), 1.0)
        b_s = jnp.where(keep, pltpu.roll(b3, s, 1), 0.0)
        b3 = a3 * b_s + b3
        a3 = a3 * a_s
        s *= 2
    yield
    gate = gate_ref[:, cs]
    carry = carry_ref[:, cs]
    outs = []
    for g in range(groups):
        hh = b3[g] + a3[g] * carry
        carry = hh[SUBLANE - 1:SUBLANE, :]
        outs.append(hh * gate[g * SUBLANE:(g + 1) * SUBLANE, :])
    per = BF16_SUBLANE // SUBLANE
    for t0 in range(0, groups, per):
        o_ref[t0 * SUBLANE:(t0 + per) * SUBLANE, cs] = jnp.concatenate(
            outs[t0:t0 + per], axis=0).astype(BF16)
    carry_ref[:, cs] = carry
    yield


def _token_shift(x, halo, mu, first, row):
    prev = jnp.where(first, 0.0, halo[SUBLANE - 1:SUBLANE, :])
    xs = jnp.where(row == 0, prev, pltpu.roll(x, 1, 0))
    return x + (xs - x) * mu


def _mm1(a, b, dims=_NN):
    return _dg(a.astype(BF16), b.astype(BF16), dims)


def _pair_diag(y, left):
    return jnp.concatenate([jnp.where(left, y, 0.0), jnp.where(left, 0.0, y)], axis=0).astype(BF16)


def _pair_mm(x, y, left):
    return _dg(x.astype(BF16), _pair_diag(y, left), _NN)


def _chunk_chain(ops, store):
    ab_, bb_, kb_, rb_, v_, bt_, kt_, pe_ = ops
    rc, lane = _iota2((CHUNK, PAIR))
    cc = lane % HEAD
    left = lane < HEAD
    strict = rc > cc
    incl = rc >= cc
    diag = rc == cc
    ar16 = [jnp.concatenate([x, y], axis=0).astype(BF16) for x, y in zip(ab_, rb_)]
    bd_b = [_pair_diag(x, left) for x in bb_]
    bd_k = [_pair_diag(x, left) for x in kb_]
    bd_v = [_pair_diag(x, left) for x in v_]
    arb = [_dg(x, y, _NT) for x, y in zip(ar16, bd_b)]
    ark = [_dg(x, y, _NT) for x, y in zip(ar16, bd_k)]
    a_ab = [jnp.where(strict, x[:CHUNK], 0.0) for x in arb]
    a_rb = [jnp.where(incl, x[CHUNK:], 0.0).astype(BF16) for x in arb]
    a_akrk = [jnp.concatenate([jnp.where(strict, x[:CHUNK], 0.0), jnp.where(incl, x[CHUNK:], 0.0)],
                              axis=0).astype(BF16) for x in ark]
    yield
    base = 8
    d = [jnp.where((rc // base) == (cc // base), a, 0.0) for a in a_ab]
    d2 = [_pair_mm(t, t, left) for t in d]
    akrkv = [_dg(x, y, _NN) for x, y in zip(a_akrk, bd_v)]
    akv = [x[:CHUNK] for x in akrkv]
    rkv = [x[CHUNK:] for x in akrkv]
    x = [jnp.where(diag, 1.0, 0.0) + t for t in d]
    yield
    x = [xi + _pair_mm(t2, xi, left) for xi, t2 in zip(x, d2)]
    d4 = [_pair_mm(t2, t2, left) for t2 in d2]
    yield
    x = [xi + _pair_mm(t4, xi, left) for xi, t4 in zip(x, d4)]
    yield
    size = base
    while size < CHUNK:
        off = jnp.logical_and((rc // (2 * size)) == (cc // (2 * size)),
                              (rc // size) != (cc // size))
        o = [jnp.where(off, a, 0.0) for a in a_ab]
        ox = [_pair_mm(oi, xi, left) for oi, xi in zip(o, x)]
        yield
        x = [xi + _pair_mm(xi, oxi, left) for xi, oxi in zip(x, ox)]
        yield
        size *= 2
    t = [xi.astype(BF16) for xi in x]
    wu = [_dg(ti, jnp.concatenate([_pair_diag(y, left), _pair_diag(z, left)], axis=1), _NN)
          for ti, y, z in zip(t, ab_, akv)]
    kv = [_dg(xi.astype(BF16), y.astype(BF16), _TN) for xi, y in zip(kt_, v_)]
    yield
    ry = [_dg(xi, jnp.concatenate([_pair_diag(y[:, :PAIR], left), _pair_diag(y[:, PAIR:], left)], axis=1), _NN)
          for xi, y in zip(a_rb, wu)]
    mn = [_dg(xi.astype(BF16), y.astype(BF16), _TN) for xi, y in zip(bt_, wu)]
    yield

    def head_blocks(z):
        return jnp.where(left, z[:HEAD, :], z[HEAD:, :])

    for u in range(len(ab_)):
        store(u,
              rb_[u] + ry[u][:, :PAIR],
              ry[u][:, PAIR:] + rkv[u],
              jnp.where(diag, pe_[u], 0.0) + head_blocks(mn[u][:, :PAIR]),
              head_blocks(mn[u][:, PAIR:]) + head_blocks(kv[u]))


def _rwkv_a_body(r_ref, k_ref, v_ref, l_ref, rh_ref, kh_ref, vh_ref, lh_ref,
                 mur_ref, muk_ref, muv_ref, mul_ref, w0_ref, a0_ref, kkw_ref, kaw_ref, rkw_ref,
                 w2_ref, a2_ref, g2_ref, ones_ref, tri_ref,
                 u_ref, gate_ref, halo_ref, cw_ref, cb_ref, wa_ref, wx_ref, ba_ref, bx_ref, lam_ref,
                 *rest):
    ncast = (len(rest) - 8) // 2
    cast_in = rest[:ncast]
    rp_ref, yp_ref, m_ref, n_ref, bonus_ref, g_ref, ya_ref = rest[ncast:ncast + 7]
    cast_out = rest[ncast + 7:-1]
    carry_ref = rest[-1]
    first = pl.program_id(1) == 0

    @pl.when(first)
    def _():
        carry_ref[...] = jnp.zeros_like(carry_ref)
    cl = CHUNK
    rows = CHUNKS_PER_STEP * cl
    width = HEADS_PER_STEP * HEAD
    gw = ones_ref.shape[0]
    row_g = lax.broadcasted_iota(jnp.int32, (rows, gw), 0)
    row_l = lax.broadcasted_iota(jnp.int32, (rows, l_ref.shape[1]), 0)
    ones_h = ones_ref[...]

    lo = _token_shift(l_ref[...], lh_ref[...], mul_ref[...], first, row_l)
    act_w = _split(jnp.tanh(lo[:, 0:LANE]))
    act_a = _split(lo[:, LANE:2 * LANE])
    act_g = _split(jax.nn.sigmoid(lo[:, 2 * LANE:]))

    def lora(act, w_ref, cs, keep_low):
        (ah, al_), wb = act, w_ref[:, cs]
        out = _dg(ah, wb, _NN)
        return out + _dg(al_, wb, _NN) if keep_low else out

    def prologue(c0, out):
        cs = slice(c0, c0 + gw)
        r = _token_shift(r_ref[:, cs], rh_ref[:, cs], mur_ref[:, cs], first, row_g)
        k = _token_shift(k_ref[:, cs], kh_ref[:, cs], muk_ref[:, cs], first, row_g)
        v = _token_shift(v_ref[:, cs], vh_ref[:, cs], muv_ref[:, cs], first, row_g)
        w_lin = w0_ref[:, cs] + lora(act_w, w2_ref, cs, True)
        a_lin = a0_ref[:, cs] + lora(act_a, a2_ref, cs, False)
        g_ref[:, cs] = lora(act_g, g2_ref, cs, False).astype(BF16)
        kk = k * kkw_ref[:, cs]
        kk_ss = _head_sums(kk * kk, ones_h)
        yield
        lw = DECAY_SCALE * jax.nn.sigmoid(w_lin)
        a = jax.nn.sigmoid(a_lin)
        kk = kk * lax.rsqrt(jnp.maximum(kk_ss, L2_EPS * L2_EPS))
        kp = k * (1.0 + (a - 1.0) * kaw_ref[:, cs])
        bonus_ref[:, cs] = (_head_sums(r * kp * rkw_ref[:, cs], ones_h) * v).astype(BF16)
        lc = _mm2_exact_lhs(tri_ref[...], lw)
        yield
        p_incl = jnp.exp(lc)
        p_excl = jnp.exp(lc - lw)
        p_inv = 1.0 / p_incl
        p_end = jnp.concatenate(
            [jnp.broadcast_to(p_incl[(j + 1) * cl - 1:(j + 1) * cl, :], (cl, gw))
             for j in range(CHUNKS_PER_STEP)], axis=0)
        abar = -(kk * p_excl)
        bbar = kk * a * p_inv
        kbar = kp * p_inv
        rbar = r * p_incl
        btil = bbar * p_end
        ktil = kbar * p_end
        units = [(j, q) for j in range(CHUNKS_PER_STEP) for q in range(gw // PAIR)]
        out.extend([x[j * cl:(j + 1) * cl, q * PAIR:(q + 1) * PAIR] for j, q in units]
                   for x in (abar, bbar, kbar, rbar, v, btil, ktil, p_end))
        yield

    def make_store(c0):
        units = [(j, q) for j in range(CHUNKS_PER_STEP) for q in range(gw // PAIR)]

        def store(u, rp, yp, mm, nn):
            j, q = units[u]
            rs = slice(j * cl, (j + 1) * cl)
            qs = slice(c0 + q * PAIR, c0 + (q + 1) * PAIR)
            rp_ref[rs, qs] = rp.astype(BF16)
            yp_ref[rs, qs] = yp.astype(BF16)
            m_ref[rs, qs] = mm.astype(BF16)
            n_ref[rs, qs] = nn.astype(BF16)
        return store

    lru = (None for h in range(LRU_HEADS)
           for _ in _lru_head(h, first, u_ref, gate_ref, halo_ref, cw_ref, cb_ref, wa_ref, wx_ref,
                              ba_ref, bx_ref, lam_ref, ya_ref, carry_ref))
    chains = []
    for c0 in range(0, width, gw):
        ops = []
        for _ in prologue(c0, ops):
            for ch in chains:
                next(ch, None)
        chains.append(_chunk_chain(ops, make_store(c0)))
    live = list(chains)
    while live:
        live = [ch for ch in live if next(ch, StopIteration) is not StopIteration]
        next(lru, None)
    for _ in lru:
        pass

    for src, dst in zip(cast_in, cast_out):
        dst[...] = src[...].astype(BF16)


def _rwkv_a(p, p_lora, mu_rkv, mu_lora, w0, a0, k_k, k_a, r_k, w2p, a2p, g2p,
            conv_w, conv_b, wa, wx, ba, bx, lam, bsz, seq, rkv_col0, cast_ws=()):
    dl = conv_w.shape[1]
    lvec = lambda t: t.reshape(1, dl)
    lrow = pl.BlockSpec((1, dl), lambda b, i, q: (0, 0))

    def lru_tile(cb):
        return pl.BlockSpec((CHUNKS_PER_STEP * CHUNK, dl), lambda b, i, q: (b * nc + i, cb))
    cl = CHUNKS_PER_STEP * CHUNK
    width = HEADS_PER_STEP * HEAD
    dr = w0.shape[1]
    ngroups = dr // width
    assert ngroups == 1, "the LRU ride-along expects one grid step per row tile"
    nc = seq // cl
    lw_ = mu_lora.shape[1]
    cb0 = rkv_col0 // width
    rows8 = cl // SUBLANE
    rt, ct = _iota2((cl, cl))
    tri = jnp.where(jnp.logical_and(rt >= ct, (rt // CHUNK) == (ct // CHUNK)), 1.0, 0.0).astype(BF16)
    ones_h = _head_ones(ONES_WIDTH)
    const = lambda arr: pl.BlockSpec(arr.shape, lambda b, i, q: (0, 0))
    lora_w = [wgt.astype(BF16) for wgt in (w2p, a2p, g2p)]

    def tile(cb_off):
        return pl.BlockSpec((cl, width), lambda b, i, q: (b * nc + i, cb0 + cb_off + q))

    def halo(cb_off):
        return pl.BlockSpec(
            (SUBLANE, width),
            lambda b, i, q: (jnp.maximum((b * nc + i) * rows8 - 1, 0), cb0 + cb_off + q))

    def prow(off=0):
        return pl.BlockSpec((1, width), lambda b, i, q: (0, off + q))

    out_tile = pl.BlockSpec((cl, width), lambda b, i, q: (b * nc + i, q))
    out_mat = pl.BlockSpec((CHUNKS_PER_STEP * HEAD, width), lambda b, i, q: (b * nc + i, q))
    act = jax.ShapeDtypeStruct((bsz * seq, dr), BF16)
    mat = jax.ShapeDtypeStruct((bsz * (seq // CHUNK) * HEAD, dr), BF16)

    nsteps = bsz * nc * ngroups
    cast_specs = []
    for wgt in cast_ws:
        hold = 1
        while (wgt.shape[0] * hold) % (nsteps * BF16_SUBLANE) != 0:
            hold *= 2
        blk = (wgt.shape[0] * hold // nsteps, wgt.shape[1])
        cast_specs.append(pl.BlockSpec(
            blk, lambda b, i, q, hold=hold: (((b * nc + i) * ngroups + q) // hold, 0)))
    cast_shapes = [jax.ShapeDtypeStruct(wgt.shape, BF16) for wgt in cast_ws]

    return pl.pallas_call(
        _rwkv_a_body,
        grid=(bsz, nc, ngroups),
        in_specs=[tile(0), tile(ngroups), tile(2 * ngroups),
                  pl.BlockSpec((cl, lw_), lambda b, i, q: (b * nc + i, 0)),
                  halo(0), halo(ngroups), halo(2 * ngroups),
                  pl.BlockSpec((SUBLANE, lw_),
                               lambda b, i, q: (jnp.maximum((b * nc + i) * rows8 - 1, 0), 0)),
                  prow(0), prow(ngroups), prow(2 * ngroups),
                  pl.BlockSpec((1, lw_), lambda b, i, q: (0, 0)),
                  prow(), prow(), prow(), prow(), prow()]
                 + [pl.BlockSpec((t.shape[0], width), lambda b, i, q: (0, q)) for t in lora_w]
                 + [const(ones_h), const(tri)]
                 + [lru_tile(0), lru_tile(1),
                    pl.BlockSpec((SUBLANE, dl),
                                 lambda b, i, q: (jnp.maximum((b * nc + i) * rows8 - 1, 0), 0)),
                    const(conv_w), lrow, pl.BlockSpec(wa.shape, lambda b, i, q: (0, 0, 0)),
                    pl.BlockSpec(wx.shape, lambda b, i, q: (0, 0, 0)), lrow, lrow, lrow]
                 + cast_specs,
        out_specs=[out_tile, out_tile, out_mat, out_mat, out_tile, out_tile, lru_tile(0)] + cast_specs,
        out_shape=[act, act, mat, mat, act, act, jax.ShapeDtypeStruct((bsz * seq, dl), BF16)]
                  + cast_shapes,
        scratch_shapes=[pltpu.VMEM((1, dl), F32)],
        compiler_params=_params("arbitrary", "arbitrary", "arbitrary"),
        name="rwkv_a",
    )(p, p, p, p_lora, p, p, p, p_lora, mu_rkv, mu_rkv, mu_rkv, mu_lora, w0, a0, k_k, k_a, r_k,
      *lora_w, ones_h, tri,
      p, p, p, conv_w, lvec(conv_b), wa, wx, lvec(ba), lvec(bx), lvec(lam), *cast_ws)


def _rwkv_b_body(rp_ref, yp_ref, m_ref, n_ref, bonus_ref, g_ref, lng_ref, lnb_ref, ones_ref,
                 o_ref, state_ref):
    @pl.when(pl.program_id(1) == 0)
    def _():
        state_ref[...] = jnp.zeros_like(state_ref)

    npairs = state_ref.shape[0]
    pairs = range(npairs)
    ps = [slice(q * PAIR, (q + 1) * PAIR) for q in pairs]
    left = lax.broadcasted_iota(jnp.int32, (HEAD, PAIR), 1) < HEAD
    ones_h = ones_ref[...]
    inv_n = 1.0 / HEAD
    state = [state_ref[q] for q in pairs]
    for j in range(rp_ref.shape[0] // CHUNK):
        rs = slice(j * CHUNK, (j + 1) * CHUNK)
        ks = slice(j * HEAD, (j + 1) * HEAD)
        g0 = [_pair_diag(state[q], left) for q in pairs]
        ys = [_dg(rp_ref[rs, ps[q]].astype(BF16), g0[q], _NN) + yp_ref[rs, ps[q]] for q in pairs]
        state = [_dg(m_ref[ks, ps[q]].astype(BF16), g0[q], _NN) + n_ref[ks, ps[q]] for q in pairs]
        y = jnp.concatenate(ys, axis=1)
        yc = y - _head_sums(y, ones_h) * inv_n
        var = _head_sums(yc * yc, ones_h) * inv_n
        yn = yc * lax.rsqrt(var + GN_EPS) * lng_ref[...] + lnb_ref[...]
        o_ref[rs, :] = ((yn + bonus_ref[rs, :]) * g_ref[rs, :]).astype(BF16)
    for q in pairs:
        state_ref[q] = state[q]


def _rwkv_b(rp, yp, mc, nm, bonus, g, ln_g, ln_b, bsz, seq):
    cl = RWKV_B_CHUNKS * CHUNK
    dr = rp.shape[1]
    nc = seq // cl
    tile = pl.BlockSpec((cl, dr), lambda b, i: (b * nc + i, 0))
    mat = pl.BlockSpec((RWKV_B_CHUNKS * HEAD, dr), lambda b, i: (b * nc + i, 0))
    prow = pl.BlockSpec((1, dr), lambda b, i: (0, 0))
    ones_h = _head_ones(ONES_WIDTH_B)
    return pl.pallas_call(
        _rwkv_b_body,
        grid=(bsz, nc),
        in_specs=[tile, tile, mat, mat, tile, tile, prow, prow,
                  pl.BlockSpec(ones_h.shape, lambda b, i: (0, 0))],
        out_specs=tile,
        out_shape=jax.ShapeDtypeStruct((bsz * seq, dr), BF16),
        scratch_shapes=[pltpu.VMEM((dr // PAIR, HEAD, PAIR), F32)],
        compiler_params=_params("parallel", "arbitrary"),
        name="rwkv_b",
    )(rp, yp, mc, nm, bonus, g, ln_g, ln_b, ones_h)


def _mm_out_body(ya_ref, yb_ref, x_ref, gm_ref, w_ref, g_ref, sh_ref, sc_ref, o_ref, h_ref, *, sub):
    da = ya_ref.shape[1]
    for r0 in range(0, x_ref.shape[0], sub):
        rs = slice(r0, r0 + sub)
        mix = (jnp.dot(ya_ref[rs, :].astype(BF16), w_ref[:da, :], preferred_element_type=F32)
               + jnp.dot(yb_ref[rs, :].astype(BF16), w_ref[da:, :], preferred_element_type=F32))
        x1 = x_ref[rs, :] + gm_ref[0] * mix
        o_ref[rs, :] = x1
        h_ref[rs, :] = _norm_mod(x1, g_ref[...], sh_ref[0], sc_ref[0]).astype(BF16)


def _mm_out(ya, yb, x2, gm, w, g, sh, sc, seq, tm=512, sub=256):
    m, d = x2.shape
    per_b = seq // tm
    brow = pl.BlockSpec((1, 1, d), lambda i: (i // per_b, 0, 0))
    tile = pl.BlockSpec((tm, d), lambda i: (i, 0))
    return pl.pallas_call(
        functools.partial(_mm_out_body, sub=sub),
        grid=(m // tm,),
        in_specs=[pl.BlockSpec((tm, ya.shape[1]), lambda i: (i, 0)),
                  pl.BlockSpec((tm, yb.shape[1]), lambda i: (i, 0)),
                  tile, brow,
                  pl.BlockSpec(w.shape, lambda i: (0, 0)),
                  pl.BlockSpec((1, d), lambda i: (0, 0)), brow, brow],
        out_specs=[tile, tile],
        out_shape=[jax.ShapeDtypeStruct((m, d), F32), jax.ShapeDtypeStruct((m, d), BF16)],
        compiler_params=_params("parallel"),
        name="mm_out",
    )(ya, yb, x2, gm, w, g, sh, sc)


def _ffn_body(x_ref, h_ref, gf_ref, wg_ref, wu_ref, wd_ref, fg_ref, o_ref, acc_ref):
    f = pl.program_id(1)

    @pl.when(f == 0)
    def _():
        acc_ref[...] = jnp.zeros_like(acc_ref)

    h = h_ref[...]
    gate = jnp.dot(h, wg_ref[...], preferred_element_type=F32)
    up = jnp.dot(h, wu_ref[...], preferred_element_type=F32)
    act = (gate * jax.nn.sigmoid(gate) * up).astype(BF16)
    acc_ref[...] += jnp.dot(act, wd_ref[...], preferred_element_type=F32)

    @pl.when(f == pl.num_programs(1) - 1)
    def _():
        y = x_ref[...] + gf_ref[0] * acc_ref[...]
        o_ref[...] = (y * lax.rsqrt(jnp.mean(y * y, axis=-1, keepdims=True) + RMS_EPS)
                      * fg_ref[...])


def _ffn(x1, h2, gf, w_gu, w_down, fg, seq, tm=512, tf=512):
    m, d = x1.shape
    dff = w_down.shape[0]
    nf = dff // tf
    assert seq % tm == 0 and dff % tf == 0, "row tiles must not straddle sequences"
    per_b = seq // tm
    tile = pl.BlockSpec((tm, d), lambda i, f: (i, 0))
    prow = pl.BlockSpec((1, d), lambda i, f: (0, 0))
    return pl.pallas_call(
        _ffn_body,
        grid=(m // tm, nf),
        in_specs=[tile, tile,
                  pl.BlockSpec((1, 1, d), lambda i, f: (i // per_b, 0, 0)),
                  pl.BlockSpec((d, tf), lambda i, f: (0, f)),
                  pl.BlockSpec((d, tf), lambda i, f: (0, nf + f)),
                  pl.BlockSpec((tf, d), lambda i, f: (f, 0)),
                  prow],
        out_specs=tile,
        out_shape=jax.ShapeDtypeStruct((m, d), F32),
        scratch_shapes=[pltpu.VMEM((tm, d), F32)],
        compiler_params=_params("parallel", "arbitrary"),
        name="ffn",
    )(x1, h2, gf, w_gu, w_gu, w_down, fg)


def _pad_cols(w, n):
    return jnp.pad(w, ((0, 0), (0, n - w.shape[1])))


def _pad_rows(w, n):
    return jnp.pad(w, ((0, n - w.shape[0]), (0, 0)))


def kernel(x, c, w_ada, b_ada, norm_mix_g, w_in, conv_w, conv_b, lru_wa, lru_ba, lru_wx, lru_bx, lru_lambda, rwkv_mu, rwkv_w0, rwkv_w2, rwkv_a0, rwkv_a2, rwkv_g2, rwkv_k_k, rwkv_k_a, rwkv_r_k, rwkv_ln_g, rwkv_ln_b, w_out, norm_ffn_g, w_gu, w_down, final_norm_g):
    bsz, seq, d = x.shape
    depth = w_ada.shape[0]
    dl = conv_w.shape[2]
    dr = rwkv_w0.shape[1]
    w_lora, a_lora, g_lora = rwkv_w2.shape[1], rwkv_a2.shape[1], rwkv_g2.shape[1]
    wpad, apad = LANE, LANE
    gpad = -(-g_lora // LANE) * LANE
    rkv_col0 = 2 * dl
    lora0 = rkv_col0 + 3 * dr

    x2 = x.reshape(bsz * seq, d)
    for l in range(depth):
        mod = _mod(c, w_ada[l], b_ada[l].reshape(1, -1))
        sh_m, sc_m, g_m, sh_f, sc_f, g_f = [t.reshape(bsz, 1, d) for t in jnp.split(mod, 6, axis=-1)]

        wi = jnp.swapaxes(w_in[l], 0, 1)
        o1, o2 = lora0 + w_lora, lora0 + w_lora + a_lora
        w_lora_p = jnp.concatenate(
            [_pad_rows(wi[lora0:o1], wpad), _pad_rows(wi[o1:o2], apad),
             _pad_rows(wi[o2:], gpad)], axis=0)
        mu = rwkv_mu[l].reshape(1, -1)
        mu_rkv = mu[:, :3 * dr]
        mu_lora = jnp.concatenate(
            [_pad_cols(mu[:, 3 * dr:3 * dr + w_lora], wpad),
             _pad_cols(mu[:, 3 * dr + w_lora:3 * dr + w_lora + a_lora], apad),
             _pad_cols(mu[:, 3 * dr + w_lora + a_lora:], gpad)], axis=1)
        w2p = _pad_rows(rwkv_w2[l], wpad)
        a2p = _pad_rows(rwkv_a2[l], apad)
        g2p = _pad_rows(rwkv_g2[l], gpad)

        h, p_lora = _norm(x2, norm_mix_g[l].reshape(1, d), sh_m, sc_m, w_lora_p, seq)
        tn_in = dl
        p = _mm_in(h, wi, lora0, tn=tn_in, gelu_tile=dl // tn_in)

        rowv = lambda t: t.reshape(1, dr)
        rp, yp, mc, nm, bonus, gg, y_a, w_out_b, w_gu_b, w_down_b = _rwkv_a(
            p, p_lora, mu_rkv, mu_lora, rowv(rwkv_w0[l]), rowv(rwkv_a0[l]), rowv(rwkv_k_k[l]),
            rowv(rwkv_k_a[l]), rowv(rwkv_r_k[l]), w2p, a2p, g2p,
            conv_w[l], conv_b[l], lru_wa[l].astype(BF16), lru_wx[l].astype(BF16),
            lru_ba[l], lru_bx[l], lru_lambda[l], bsz, seq, rkv_col0,
            cast_ws=(w_out[l], w_gu[l], w_down[l]))
        y_b = _rwkv_b(rp, yp, mc, nm, bonus, gg, rowv(rwkv_ln_g[l]), rowv(rwkv_ln_b[l]), bsz, seq)

        x2, h2 = _mm_out(y_a, y_b, x2, g_m, w_out_b, norm_ffn_g[l].reshape(1, d), sh_f, sc_f, seq)

        last = l == depth - 1
        fg = final_norm_g.reshape(1, d) if last else None
        assert last, "only the final layer carries the closing RMSNorm"
        x2 = _ffn(x2, h2, g_f, w_gu_b, w_down_b, fg, seq)
    return x2.reshape(bsz, seq, d)
```

```python
import functools
import math

import jax
import jax.numpy as jnp
from jax import lax
from jax.experimental import pallas as pl
from jax.experimental.pallas import tpu as pltpu

F32 = jnp.float32
BF16 = jnp.bfloat16

LRU_HEADS = 4
CONV_WIDTH = 4
LRU_C = 8.0
HEAD = 64
CHUNK = 64
PAIR = 2 * HEAD
HEADS_PER_STEP = 16
ONES_WIDTH = 256
ONES_WIDTH_B = 128
CHUNKS_PER_STEP = 4
RWKV_B_CHUNKS = 8
MOD_DMA_BANDS = 4
RMS_EPS = 1e-6
GN_EPS = 64e-5
L2_EPS = 1e-12
DECAY_SCALE = -math.exp(-0.5)
LANE = 128
SUBLANE = 8
BF16_SUBLANE = 16
VMEM_LIMIT = 56 * 1024 * 1024


def _params(*sem):
    return pltpu.CompilerParams(dimension_semantics=sem, vmem_limit_bytes=VMEM_LIMIT)


_NN = (((1,), (0,)), ((), ()))
_NT = (((1,), (1,)), ((), ()))
_TN = (((0,), (0,)), ((), ()))


def _dg(a, b, dims):
    return lax.dot_general(a, b, dims, preferred_element_type=F32)


def _split(x):
    hi = x.astype(BF16)
    lo = (x - hi.astype(F32)).astype(BF16)
    return hi, lo


def _mm3(a, b, dims=_NN):
    ah, al = _split(a)
    bh, bl = _split(b)
    return _dg(ah, bh, dims) + (_dg(ah, bl, dims) + _dg(al, bh, dims))


def _head_sums(x, ones_h):
    n = ones_h.shape[0]
    xb = x.astype(BF16)
    return jnp.concatenate([_dg(xb[:, c:c + n], ones_h, _NN) for c in range(0, x.shape[1], n)],
                           axis=1)


def _mm2_exact_lhs(a_bf16, b):
    bh, bl = _split(b)
    return _dg(a_bf16, bh, _NN) + _dg(a_bf16, bl, _NN)


def _softplus(x):
    return jnp.maximum(x, 0.0) + jnp.log1p(jnp.exp(-jnp.abs(x)))


def _iota2(shape):
    return (lax.broadcasted_iota(jnp.int32, shape, 0),
            lax.broadcasted_iota(jnp.int32, shape, 1))


def _head_ones(n):
    r, c = _iota2((n, n))
    return jnp.where((r // HEAD) == (c // HEAD), 1.0, 0.0).astype(BF16)


def _mod_body(c_ref, *refs):
    w_refs, b_ref, o_ref = refs[:-2], refs[-2], refs[-1]
    c = c_ref[...]
    ca = c * jax.nn.sigmoid(c)
    kb = w_refs[0].shape[0]
    acc = b_ref[...]
    for s, w_ref in enumerate(w_refs):
        acc = acc + _mm3(ca[:, s * kb:(s + 1) * kb], w_ref[...])
    o_ref[...] = acc


def _mod(c, w, b, tn=1024, bands=MOD_DMA_BANDS):
    bsz, d = c.shape
    n = w.shape[1]
    kb = d // bands
    return pl.pallas_call(
        _mod_body,
        grid=(n // tn,),
        in_specs=[pl.BlockSpec((bsz, d), lambda j: (0, 0))]
                 + [pl.BlockSpec((kb, tn), lambda j, s=s: (s, j)) for s in range(bands)]
                 + [pl.BlockSpec((1, tn), lambda j: (0, j))],
        out_specs=pl.BlockSpec((bsz, tn), lambda j: (0, j)),
        out_shape=jax.ShapeDtypeStruct((bsz, n), F32),
        compiler_params=_params("parallel"),
        name="mod",
    )(c, *([w] * bands), b)


def _norm_mod(x, g, sh, sc):
    y = x * lax.rsqrt(jnp.mean(x * x, axis=-1, keepdims=True) + RMS_EPS) * g
    return y * (1.0 + sc) + sh


def _norm_body(x_ref, g_ref, sh_ref, sc_ref, w_ref, o_ref, pl_ref, wb_ref):
    @pl.when(pl.program_id(0) == 0)
    def _():
        wb_ref[...] = w_ref[...].astype(BF16)

    h = _norm_mod(x_ref[...], g_ref[...], sh_ref[0], sc_ref[0]).astype(BF16)
    o_ref[...] = h
    pl_ref[...] = _dg(h, wb_ref[...], _NT)


def _norm(x2, g, sh, sc, w_lora_t, seq, tm=512):
    m, d = x2.shape
    nl = w_lora_t.shape[0]
    per_b = seq // tm
    return pl.pallas_call(
        _norm_body,
        grid=(m // tm,),
        in_specs=[pl.BlockSpec((tm, d), lambda i: (i, 0)),
                  pl.BlockSpec((1, d), lambda i: (0, 0)),
                  pl.BlockSpec((1, 1, d), lambda i: (i // per_b, 0, 0)),
                  pl.BlockSpec((1, 1, d), lambda i: (i // per_b, 0, 0)),
                  pl.BlockSpec((nl, d), lambda i: (0, 0))],
        out_specs=[pl.BlockSpec((tm, d), lambda i: (i, 0)), pl.BlockSpec((tm, nl), lambda i: (i, 0))],
        out_shape=[jax.ShapeDtypeStruct((m, d), BF16), jax.ShapeDtypeStruct((m, nl), F32)],
        scratch_shapes=[pltpu.VMEM((nl, d), BF16)],
        compiler_params=_params("arbitrary"),
        name="norm_mix",
    )(x2, g, sh, sc, w_lora_t)


def _mm_in_body(h_ref, w_ref, mu_ref, cw_ref, cb_ref, o_ref, wb_ref, tail_ref, *, per_b, sub):
    j, i = pl.program_id(0), pl.program_id(1)

    @pl.when(i == 0)
    def _():
        wb_ref[...] = w_ref[...].astype(BF16)

    tm = h_ref.shape[0]

    def run(epilogue):
        prev = jnp.where(i % per_b == 0, 0.0, tail_ref[...])
        for r0 in range(0, tm, sub):
            raw = _dg(h_ref[r0:r0 + sub, :], wb_ref[...], _NT)
            o_ref[r0:r0 + sub, :] = epilogue(raw, prev)
            prev = raw[sub - SUBLANE:, :]
        tail_ref[...] = prev

    def back(ext, k):
        return pltpu.roll(ext, k, 0)[SUBLANE:, :]

    def conv(raw, prev):
        ext = jnp.concatenate([prev, raw], axis=0)
        cw = cw_ref[...]
        u = cb_ref[...] + raw * cw[CONV_WIDTH - 1:CONV_WIDTH, :]
        for k in range(1, CONV_WIDTH):
            u = u + back(ext, k) * cw[CONV_WIDTH - 1 - k:CONV_WIDTH - k, :]
        return u

    def shift(raw, prev):
        ext = jnp.concatenate([prev, raw], axis=0)
        return raw + (back(ext, 1) - raw) * mu_ref[...]

    @pl.when(j == 0)
    def _():
        run(conv)

    @pl.when(j == 1)
    def _():
        run(lambda raw, prev: jax.nn.gelu(raw))

    @pl.when(j > 1)
    def _():
        run(shift)


def _mm_in(h, wt, mu, conv_w, conv_b, ncols, seq, tm=1024, sub=256):
    m, d = h.shape
    tn = conv_w.shape[1]
    assert seq % tm == 0 and ncols % tn == 0
    return pl.pallas_call(
        functools.partial(_mm_in_body, per_b=seq // tm, sub=sub),
        grid=(ncols // tn, m // tm),
        in_specs=[pl.BlockSpec((tm, d), lambda j, i: (i, 0)),
                  pl.BlockSpec((tn, d), lambda j, i: (j, 0)),
                  pl.BlockSpec((1, tn), lambda j, i: (0, j)),
                  pl.BlockSpec(conv_w.shape, lambda j, i: (0, 0)),
                  pl.BlockSpec((1, tn), lambda j, i: (0, 0))],
        out_specs=pl.BlockSpec((tm, tn), lambda j, i: (i, j)),
        out_shape=jax.ShapeDtypeStruct((m, ncols), F32),
        scratch_shapes=[pltpu.VMEM((tn, d), BF16), pltpu.VMEM((SUBLANE, tn), F32)],
        compiler_params=_params("arbitrary", "arbitrary"),
        name="mm_in",
    )(h, wt, mu, conv_w, conv_b.reshape(1, tn))


def _lru_head(h, first, u_ref, gate_ref, wa_ref, wx_ref, ba_ref, bx_ref, lam_ref, o_ref, carry_ref):
    tt = u_ref.shape[0]
    hd = u_ref.shape[1] // LRU_HEADS
    cs = slice(h * hd, (h + 1) * hd)
    u = u_ref[:, cs]
    ub = u.astype(BF16)
    ra = jnp.dot(ub, wa_ref[h], preferred_element_type=F32)
    rx = jnp.dot(ub, wx_ref[h], preferred_element_type=F32)
    yield
    r = jax.nn.sigmoid(ra + ba_ref[:, cs])
    ig = jax.nn.sigmoid(rx + bx_ref[:, cs])
    a = jnp.exp(r * ((-LRU_C) * _softplus(-lam_ref[:, cs])))
    mult = jnp.sqrt(1.0 - a * a)
    row = lax.broadcasted_iota(jnp.int32, (tt, hd), 0)
    mult = jnp.where(jnp.logical_and(first, row == 0), 1.0, mult)
    b = mult * (ig * u)

    groups = tt // SUBLANE
    a3 = a.reshape(groups, SUBLANE, hd)
    b3 = b.reshape(groups, SUBLANE, hd)
    sub = lax.broadcasted_iota(jnp.int32, (groups, SUBLANE, hd), 1)
    s = 1
    while s < SUBLANE:
        keep = sub >= s
        a_s = jnp.where(keep, pltpu.roll(a3, s, 1), 1.0)
        b_s = jnp.where(keep, pltpu.roll(b3, s, 1), 0.0)
        b3 = a3 * b_s + b3
        a3 = a3 * a_s
        s *= 2
    yield
    gate = gate_ref[:, cs]
    carry = carry_ref[:, cs]
    outs = []
    for g in range(groups):
        hh = b3[g] + a3[g] * carry
        carry = hh[SUBLANE - 1:SUBLANE, :]
        outs.append(hh * gate[g * SUBLANE:(g + 1) * SUBLANE, :])
    per = BF16_SUBLANE // SUBLANE
    for t0 in range(0, groups, per):
        o_ref[t0 * SUBLANE:(t0 + per) * SUBLANE, cs] = jnp.concatenate(
            outs[t0:t0 + per], axis=0).astype(BF16)
    carry_ref[:, cs] = carry
    yield


def _token_shift(x, halo, mu, first, row):
    prev = jnp.where(first, 0.0, halo[SUBLANE - 1:SUBLANE, :])
    xs = jnp.where(row == 0, prev, pltpu.roll(x, 1, 0))
    return x + (xs - x) * mu


def _mm1(a, b, dims=_NN):
    return _dg(a.astype(BF16), b.astype(BF16), dims)


def _pair_diag(y, left):
    return jnp.concatenate([jnp.where(left, y, 0.0), jnp.where(left, 0.0, y)], axis=0).astype(BF16)


def _pair_mm(x, y, left):
    return _dg(x.astype(BF16), _pair_diag(y, left), _NN)


def _chunk_chain(ops, store):
    ab_, bb_, kb_, rb_, v_, bt_, kt_, pe_ = ops
    rc, lane = _iota2((CHUNK, PAIR))
    cc = lane % HEAD
    left = lane < HEAD
    strict = rc > cc
    incl = rc >= cc
    diag = rc == cc
    ar16 = [jnp.concatenate([x, y], axis=0).astype(BF16) for x, y in zip(ab_, rb_)]
    bd_b = [_pair_diag(x, left) for x in bb_]
    bd_k = [_pair_diag(x, left) for x in kb_]
    bd_v = [_pair_diag(x, left) for x in v_]
    arb = [_dg(x, y, _NT) for x, y in zip(ar16, bd_b)]
    ark = [_dg(x, y, _NT) for x, y in zip(ar16, bd_k)]
    a_ab = [jnp.where(strict, x[:CHUNK], 0.0) for x in arb]
    a_rb = [jnp.where(incl, x[CHUNK:], 0.0).astype(BF16) for x in arb]
    a_akrk = [jnp.concatenate([jnp.where(strict, x[:CHUNK], 0.0), jnp.where(incl, x[CHUNK:], 0.0)],
                              axis=0).astype(BF16) for x in ark]
    yield
    base = 8
    d = [jnp.where((rc // base) == (cc // base), a, 0.0) for a in a_ab]
    d2 = [_pair_mm(t, t, left) for t in d]
    akrkv = [_dg(x, y, _NN) for x, y in zip(a_akrk, bd_v)]
    akv = [x[:CHUNK] for x in akrkv]
    rkv = [x[CHUNK:] for x in akrkv]
    x = [jnp.where(diag, 1.0, 0.0) + t for t in d]
    yield
    x = [xi + _pair_mm(t2, xi, left) for xi, t2 in zip(x, d2)]
    d4 = [_pair_mm(t2, t2, left) for t2 in d2]
    yield
    x = [xi + _pair_mm(t4, xi, left) for xi, t4 in zip(x, d4)]
    yield
    size = base
    while size < CHUNK:
        off = jnp.logical_and((rc // (2 * size)) == (cc // (2 * size)),
                              (rc // size) != (cc // size))
        o = [jnp.where(off, a, 0.0) for a in a_ab]
        ox = [_pair_mm(oi, xi, left) for oi, xi in zip(o, x)]
        yield
        x = [xi + _pair_mm(xi, oxi, left) for xi, oxi in zip(x, ox)]
        yield
        size *= 2
    t = [xi.astype(BF16) for xi in x]
    wu = [_dg(ti, jnp.concatenate([_pair_diag(y, left), _pair_diag(z, left)], axis=1), _NN)
          for ti, y, z in zip(t, ab_, akv)]
    kv = [_dg(xi.astype(BF16), y.astype(BF16), _TN) for xi, y in zip(kt_, v_)]
    yield
    ry = [_dg(xi, jnp.concatenate([_pair_diag(y[:, :PAIR], left), _pair_diag(y[:, PAIR:], left)], axis=1), _NN)
          for xi, y in zip(a_rb, wu)]
    mn = [_dg(xi.astype(BF16), y.astype(BF16), _TN) for xi, y in zip(bt_, wu)]
    yield

    def head_blocks(z):
        return jnp.where(left, z[:HEAD, :], z[HEAD:, :])

    for u in range(len(ab_)):
        store(u,
              rb_[u] + ry[u][:, :PAIR],
              ry[u][:, PAIR:] + rkv[u],
              jnp.where(diag, pe_[u], 0.0) + head_blocks(mn[u][:, :PAIR]),
              head_blocks(mn[u][:, PAIR:]) + head_blocks(kv[u]))


def _rwkv_a_body(r_ref, k_ref, v_ref, l_ref, lh_ref,
                 mul_ref, w0_ref, a0_ref, kkw_ref, kaw_ref, rkw_ref,
                 w2_ref, a2_ref, g2_ref, ones_ref, tri_ref,
                 u_ref, gate_ref, wa_ref, wx_ref, ba_ref, bx_ref, lam_ref,
                 *rest):
    ncast = (len(rest) - 8) // 2
    cast_in = rest[:ncast]
    rp_ref, yp_ref, m_ref, n_ref, bonus_ref, g_ref, ya_ref = rest[ncast:ncast + 7]
    cast_out = rest[ncast + 7:-1]
    carry_ref = rest[-1]
    first = pl.program_id(1) == 0

    @pl.when(first)
    def _():
        carry_ref[...] = jnp.zeros_like(carry_ref)
    cl = CHUNK
    rows = CHUNKS_PER_STEP * cl
    width = HEADS_PER_STEP * HEAD
    gw = ones_ref.shape[0]
    row_l = lax.broadcasted_iota(jnp.int32, (rows, l_ref.shape[1]), 0)
    ones_h = ones_ref[...]

    lo = _token_shift(l_ref[...], lh_ref[...], mul_ref[...], first, row_l)
    act_w = _split(jnp.tanh(lo[:, 0:LANE]))
    act_a = _split(lo[:, LANE:2 * LANE])
    act_g = _split(jax.nn.sigmoid(lo[:, 2 * LANE:]))

    def lora(act, w_ref, cs, keep_low):
        (ah, al_), wb = act, w_ref[:, cs]
        out = _dg(ah, wb, _NN)
        return out + _dg(al_, wb, _NN) if keep_low else out

    def prologue(c0, out):
        cs = slice(c0, c0 + gw)
        r, k, v = r_ref[:, cs], k_ref[:, cs], v_ref[:, cs]
        w_lin = w0_ref[:, cs] + lora(act_w, w2_ref, cs, True)
        a_lin = a0_ref[:, cs] + lora(act_a, a2_ref, cs, False)
        g_ref[:, cs] = lora(act_g, g2_ref, cs, False).astype(BF16)
        kk = k * kkw_ref[:, cs]
        kk_ss = _head_sums(kk * kk, ones_h)
        yield
        lw = DECAY_SCALE * jax.nn.sigmoid(w_lin)
        a = jax.nn.sigmoid(a_lin)
        kk = kk * lax.rsqrt(jnp.maximum(kk_ss, L2_EPS * L2_EPS))
        kp = k * (1.0 + (a - 1.0) * kaw_ref[:, cs])
        bonus_ref[:, cs] = (_head_sums(r * kp * rkw_ref[:, cs], ones_h) * v).astype(BF16)
        lc = _mm2_exact_lhs(tri_ref[...], lw)
        yield
        p_incl = jnp.exp(lc)
        p_excl = jnp.exp(lc - lw)
        p_inv = 1.0 / p_incl
        p_end = jnp.concatenate(
            [jnp.broadcast_to(p_incl[(j + 1) * cl - 1:(j + 1) * cl, :], (cl, gw))
             for j in range(CHUNKS_PER_STEP)], axis=0)
        abar = -(kk * p_excl)
        bbar = kk * a * p_inv
        kbar = kp * p_inv
        rbar = r * p_incl
        btil = bbar * p_end
        ktil = kbar * p_end
        units = [(j, q) for j in range(CHUNKS_PER_STEP) for q in range(gw // PAIR)]
        out.extend([x[j * cl:(j + 1) * cl, q * PAIR:(q + 1) * PAIR] for j, q in units]
                   for x in (abar, bbar, kbar, rbar, v, btil, ktil, p_end))
        yield

    def make_store(c0):
        units = [(j, q) for j in range(CHUNKS_PER_STEP) for q in range(gw // PAIR)]

        def store(u, rp, yp, mm, nn):
            j, q = units[u]
            rs = slice(j * cl, (j + 1) * cl)
            qs = slice(c0 + q * PAIR, c0 + (q + 1) * PAIR)
            rp_ref[rs, qs] = rp.astype(BF16)
            yp_ref[rs, qs] = yp.astype(BF16)
            m_ref[rs, qs] = mm.astype(BF16)
            n_ref[rs, qs] = nn.astype(BF16)
        return store

    lru = (None for h in range(LRU_HEADS)
           for _ in _lru_head(h, first, u_ref, gate_ref, wa_ref, wx_ref,
                              ba_ref, bx_ref, lam_ref, ya_ref, carry_ref))
    chains = []
    for c0 in range(0, width, gw):
        ops = []
        for _ in prologue(c0, ops):
            for ch in chains:
                next(ch, None)
        chains.append(_chunk_chain(ops, make_store(c0)))
    live = list(chains)
    while live:
        live = [ch for ch in live if next(ch, StopIteration) is not StopIteration]
        next(lru, None)
    for _ in lru:
        pass

    for src, dst in zip(cast_in, cast_out):
        dst[...] = src[...].astype(BF16)


def _rwkv_a(p, p_lora, mu_lora, w0, a0, k_k, k_a, r_k, w2p, a2p, g2p,
            wa, wx, ba, bx, lam, bsz, seq, rkv_col0, cast_ws=()):
    dl = ba.shape[0]
    lvec = lambda t: t.reshape(1, dl)
    lrow = pl.BlockSpec((1, dl), lambda b, i, q: (0, 0))

    def lru_tile(cb):
        return pl.BlockSpec((CHUNKS_PER_STEP * CHUNK, dl), lambda b, i, q: (b * nc + i, cb))
    cl = CHUNKS_PER_STEP * CHUNK
    width = HEADS_PER_STEP * HEAD
    dr = w0.shape[1]
    ngroups = dr // width
    assert ngroups == 1, "the LRU ride-along expects one grid step per row tile"
    nc = seq // cl
    lw_ = mu_lora.shape[1]
    cb0 = rkv_col0 // width
    rows8 = cl // SUBLANE
    rt, ct = _iota2((cl, cl))
    tri = jnp.where(jnp.logical_and(rt >= ct, (rt // CHUNK) == (ct // CHUNK)), 1.0, 0.0).astype(BF16)
    ones_h = _head_ones(ONES_WIDTH)
    const = lambda arr: pl.BlockSpec(arr.shape, lambda b, i, q: (0, 0))
    lora_w = [wgt.astype(BF16) for wgt in (w2p, a2p, g2p)]

    def tile(cb_off):
        return pl.BlockSpec((cl, width), lambda b, i, q: (b * nc + i, cb0 + cb_off + q))

    def prow(off=0):
        return pl.BlockSpec((1, width), lambda b, i, q: (0, off + q))

    out_tile = pl.BlockSpec((cl, width), lambda b, i, q: (b * nc + i, q))
    out_mat = pl.BlockSpec((CHUNKS_PER_STEP * HEAD, width), lambda b, i, q: (b * nc + i, q))
    act = jax.ShapeDtypeStruct((bsz * seq, dr), BF16)
    mat = jax.ShapeDtypeStruct((bsz * (seq // CHUNK) * HEAD, dr), BF16)

    nsteps = bsz * nc * ngroups
    cast_specs = []
    for wgt in cast_ws:
        hold = 1
        while (wgt.shape[0] * hold) % (nsteps * BF16_SUBLANE) != 0:
            hold *= 2
        blk = (wgt.shape[0] * hold // nsteps, wgt.shape[1])
        cast_specs.append(pl.BlockSpec(
            blk, lambda b, i, q, hold=hold: (((b * nc + i) * ngroups + q) // hold, 0)))
    cast_shapes = [jax.ShapeDtypeStruct(wgt.shape, BF16) for wgt in cast_ws]

    return pl.pallas_call(
        _rwkv_a_body,
        grid=(bsz, nc, ngroups),
        in_specs=[tile(0), tile(ngroups), tile(2 * ngroups),
                  pl.BlockSpec((cl, lw_), lambda b, i, q: (b * nc + i, 0)),
                  pl.BlockSpec((SUBLANE, lw_),
                               lambda b, i, q: (jnp.maximum((b * nc + i) * rows8 - 1, 0), 0)),
                  pl.BlockSpec((1, lw_), lambda b, i, q: (0, 0)),
                  prow(), prow(), prow(), prow(), prow()]
                 + [pl.BlockSpec((t.shape[0], width), lambda b, i, q: (0, q)) for t in lora_w]
                 + [const(ones_h), const(tri)]
                 + [lru_tile(0), lru_tile(1), pl.BlockSpec(wa.shape, lambda b, i, q: (0, 0, 0)),
                    pl.BlockSpec(wx.shape, lambda b, i, q: (0, 0, 0)), lrow, lrow, lrow]
                 + cast_specs,
        out_specs=[out_tile, out_tile, out_mat, out_mat, out_tile, out_tile, lru_tile(0)] + cast_specs,
        out_shape=[act, act, mat, mat, act, act, jax.ShapeDtypeStruct((bsz * seq, dl), BF16)]
                  + cast_shapes,
        scratch_shapes=[pltpu.VMEM((1, dl), F32)],
        compiler_params=_params("arbitrary", "arbitrary", "arbitrary"),
        name="rwkv_a",
    )(p, p, p, p_lora, p_lora, mu_lora, w0, a0, k_k, k_a, r_k,
      *lora_w, ones_h, tri,
      p, p, wa, wx, lvec(ba), lvec(bx), lvec(lam), *cast_ws)


def _rwkv_b_body(rp_ref, yp_ref, m_ref, n_ref, bonus_ref, g_ref, lng_ref, lnb_ref, ones_ref,
                 o_ref, state_ref):
    @pl.when(pl.program_id(1) == 0)
    def _():
        state_ref[...] = jnp.zeros_like(state_ref)

    npairs = state_ref.shape[0]
    pairs = range(npairs)
    ps = [slice(q * PAIR, (q + 1) * PAIR) for q in pairs]
    left = lax.broadcasted_iota(jnp.int32, (HEAD, PAIR), 1) < HEAD
    ones_h = ones_ref[...]
    inv_n = 1.0 / HEAD
    state = [state_ref[q] for q in pairs]
    for j in range(rp_ref.shape[0] // CHUNK):
        rs = slice(j * CHUNK, (j + 1) * CHUNK)
        ks = slice(j * HEAD, (j + 1) * HEAD)
        g0 = [_pair_diag(state[q], left) for q in pairs]
        ys = [_dg(rp_ref[rs, ps[q]].astype(BF16), g0[q], _NN) + yp_ref[rs, ps[q]] for q in pairs]
        state = [_dg(m_ref[ks, ps[q]].astype(BF16), g0[q], _NN) + n_ref[ks, ps[q]] for q in pairs]
        y = jnp.concatenate(ys, axis=1)
        yc = y - _head_sums(y, ones_h) * inv_n
        var = _head_sums(yc * yc, ones_h) * inv_n
        yn = yc * lax.rsqrt(var + GN_EPS) * lng_ref[...] + lnb_ref[...]
        o_ref[rs, :] = ((yn + bonus_ref[rs, :]) * g_ref[rs, :]).astype(BF16)
    for q in pairs:
        state_ref[q] = state[q]


def _rwkv_b(rp, yp, mc, nm, bonus, g, ln_g, ln_b, bsz, seq):
    cl = RWKV_B_CHUNKS * CHUNK
    dr = rp.shape[1]
    nc = seq // cl
    tile = pl.BlockSpec((cl, dr), lambda b, i: (b * nc + i, 0))
    mat = pl.BlockSpec((RWKV_B_CHUNKS * HEAD, dr), lambda b, i: (b * nc + i, 0))
    prow = pl.BlockSpec((1, dr), lambda b, i: (0, 0))
    ones_h = _head_ones(ONES_WIDTH_B)
    return pl.pallas_call(
        _rwkv_b_body,
        grid=(bsz, nc),
        in_specs=[tile, tile, mat, mat, tile, tile, prow, prow,
                  pl.BlockSpec(ones_h.shape, lambda b, i: (0, 0))],
        out_specs=tile,
        out_shape=jax.ShapeDtypeStruct((bsz * seq, dr), BF16),
        scratch_shapes=[pltpu.VMEM((dr // PAIR, HEAD, PAIR), F32)],
        compiler_params=_params("parallel", "arbitrary"),
        name="rwkv_b",
    )(rp, yp, mc, nm, bonus, g, ln_g, ln_b, ones_h)


def _mm_out_body(ya_ref, yb_ref, x_ref, gm_ref, w_ref, g_ref, sh_ref, sc_ref, o_ref, h_ref, *, sub):
    da = ya_ref.shape[1]
    for r0 in range(0, x_ref.shape[0], sub):
        rs = slice(r0, r0 + sub)
        mix = (jnp.dot(ya_ref[rs, :].astype(BF16), w_ref[:da, :], preferred_element_type=F32)
               + jnp.dot(yb_ref[rs, :].astype(BF16), w_ref[da:, :], preferred_element_type=F32))
        x1 = x_ref[rs, :] + gm_ref[0] * mix
        o_ref[rs, :] = x1
        h_ref[rs, :] = _norm_mod(x1, g_ref[...], sh_ref[0], sc_ref[0]).astype(BF16)


def _mm_out(ya, yb, x2, gm, w, g, sh, sc, seq, tm=512, sub=256):
    m, d = x2.shape
    per_b = seq // tm
    brow = pl.BlockSpec((1, 1, d), lambda i: (i // per_b, 0, 0))
    tile = pl.BlockSpec((tm, d), lambda i: (i, 0))
    return pl.pallas_call(
        functools.partial(_mm_out_body, sub=sub),
        grid=(m // tm,),
        in_specs=[pl.BlockSpec((tm, ya.shape[1]), lambda i: (i, 0)),
                  pl.BlockSpec((tm, yb.shape[1]), lambda i: (i, 0)),
                  tile, brow,
                  pl.BlockSpec(w.shape, lambda i: (0, 0)),
                  pl.BlockSpec((1, d), lambda i: (0, 0)), brow, brow],
        out_specs=[tile, tile],
        out_shape=[jax.ShapeDtypeStruct((m, d), F32), jax.ShapeDtypeStruct((m, d), BF16)],
        compiler_params=_params("parallel"),
        name="mm_out",
    )(ya, yb, x2, gm, w, g, sh, sc)


def _ffn_body(x_ref, h_ref, gf_ref, wg_ref, wu_ref, wd_ref, fg_ref, o_ref, acc_ref):
    f = pl.program_id(1)

    @pl.when(f == 0)
    def _():
        acc_ref[...] = jnp.zeros_like(acc_ref)

    h = h_ref[...]
    gate = jnp.dot(h, wg_ref[...], preferred_element_type=F32)
    up = jnp.dot(h, wu_ref[...], preferred_element_type=F32)
    act = (gate * jax.nn.sigmoid(gate) * up).astype(BF16)
    acc_ref[...] += jnp.dot(act, wd_ref[...], preferred_element_type=F32)

    @pl.when(f == pl.num_programs(1) - 1)
    def _():
        y = x_ref[...] + gf_ref[0] * acc_ref[...]
        o_ref[...] = (y * lax.rsqrt(jnp.mean(y * y, axis=-1, keepdims=True) + RMS_EPS)
                      * fg_ref[...])


def _ffn(x1, h2, gf, w_gu, w_down, fg, seq, tm=512, tf=512):
    m, d = x1.shape
    dff = w_down.shape[0]
    nf = dff // tf
    assert seq % tm == 0 and dff % tf == 0, "row tiles must not straddle sequences"
    per_b = seq // tm
    tile = pl.BlockSpec((tm, d), lambda i, f: (i, 0))
    prow = pl.BlockSpec((1, d), lambda i, f: (0, 0))
    return pl.pallas_call(
        _ffn_body,
        grid=(m // tm, nf),
        in_specs=[tile, tile,
                  pl.BlockSpec((1, 1, d), lambda i, f: (i // per_b, 0, 0)),
                  pl.BlockSpec((d, tf), lambda i, f: (0, f)),
                  pl.BlockSpec((d, tf), lambda i, f: (0, nf + f)),
                  pl.BlockSpec((tf, d), lambda i, f: (f, 0)),
                  prow],
        out_specs=tile,
        out_shape=jax.ShapeDtypeStruct((m, d), F32),
        scratch_shapes=[pltpu.VMEM((tm, d), F32)],
        compiler_params=_params("parallel", "arbitrary"),
        name="ffn",
    )(x1, h2, gf, w_gu, w_gu, w_down, fg)


def _pad_cols(w, n):
    return jnp.pad(w, ((0, 0), (0, n - w.shape[1])))


def _pad_rows(w, n):
    return jnp.pad(w, ((0, n - w.shape[0]), (0, 0)))


def kernel(x, c, w_ada, b_ada, norm_mix_g, w_in, conv_w, conv_b, lru_wa, lru_ba, lru_wx, lru_bx, lru_lambda, rwkv_mu, rwkv_w0, rwkv_w2, rwkv_a0, rwkv_a2, rwkv_g2, rwkv_k_k, rwkv_k_a, rwkv_r_k, rwkv_ln_g, rwkv_ln_b, w_out, norm_ffn_g, w_gu, w_down, final_norm_g):
    bsz, seq, d = x.shape
    depth = w_ada.shape[0]
    dl = conv_w.shape[2]
    dr = rwkv_w0.shape[1]
    w_lora, a_lora, g_lora = rwkv_w2.shape[1], rwkv_a2.shape[1], rwkv_g2.shape[1]
    wpad, apad = LANE, LANE
    gpad = -(-g_lora // LANE) * LANE
    rkv_col0 = 2 * dl
    lora0 = rkv_col0 + 3 * dr

    x2 = x.reshape(bsz * seq, d)
    for l in range(depth):
        mod = _mod(c, w_ada[l], b_ada[l].reshape(1, -1))
        sh_m, sc_m, g_m, sh_f, sc_f, g_f = [t.reshape(bsz, 1, d) for t in jnp.split(mod, 6, axis=-1)]

        wi = jnp.swapaxes(w_in[l], 0, 1)
        o1, o2 = lora0 + w_lora, lora0 + w_lora + a_lora
        w_lora_p = jnp.concatenate(
            [_pad_rows(wi[lora0:o1], wpad), _pad_rows(wi[o1:o2], apad),
             _pad_rows(wi[o2:], gpad)], axis=0)
        mu = rwkv_mu[l].reshape(1, -1)
        mu_main = jnp.pad(mu[:, :3 * dr], ((0, 0), (rkv_col0, 0)))
        mu_lora = jnp.concatenate(
            [_pad_cols(mu[:, 3 * dr:3 * dr + w_lora], wpad),
             _pad_cols(mu[:, 3 * dr + w_lora:3 * dr + w_lora + a_lora], apad),
             _pad_cols(mu[:, 3 * dr + w_lora + a_lora:], gpad)], axis=1)
        w2p = _pad_rows(rwkv_w2[l], wpad)
        a2p = _pad_rows(rwkv_a2[l], apad)
        g2p = _pad_rows(rwkv_g2[l], gpad)

        h, p_lora = _norm(x2, norm_mix_g[l].reshape(1, d), sh_m, sc_m, w_lora_p, seq)
        p = _mm_in(h, wi, mu_main, conv_w[l], conv_b[l], lora0, seq)

        rowv = lambda t: t.reshape(1, dr)
        rp, yp, mc, nm, bonus, gg, y_a, w_out_b, w_gu_b, w_down_b = _rwkv_a(
            p, p_lora, mu_lora, rowv(rwkv_w0[l]), rowv(rwkv_a0[l]), rowv(rwkv_k_k[l]),
            rowv(rwkv_k_a[l]), rowv(rwkv_r_k[l]), w2p, a2p, g2p,
            lru_wa[l].astype(BF16), lru_wx[l].astype(BF16),
            lru_ba[l], lru_bx[l], lru_lambda[l], bsz, seq, rkv_col0,
            cast_ws=(w_out[l], w_gu[l], w_down[l]))
        y_b = _rwkv_b(rp, yp, mc, nm, bonus, gg, rowv(rwkv_ln_g[l]), rowv(rwkv_ln_b[l]), bsz, seq)

        x2, h2 = _mm_out(y_a, y_b, x2, g_m, w_out_b, norm_ffn_g[l].reshape(1, d), sh_f, sc_f, seq)

        last = l == depth - 1
        fg = final_norm_g.reshape(1, d) if last else None
        assert last, "only the final layer carries the closing RMSNorm"
        x2 = _ffn(x2, h2, g_f, w_gu_b, w_down_b, fg, seq)
    return x2.reshape(bsz, seq, d)
```

```python
import functools
import math

import jax
import jax.numpy as jnp
from jax import lax
from jax.experimental import pallas as pl
from jax.experimental.pallas import tpu as pltpu

F32 = jnp.float32
BF16 = jnp.bfloat16

LRU_HEADS = 4
CONV_WIDTH = 4
LRU_C = 8.0
HEAD = 64
CHUNK = 64
PAIR = 2 * HEAD
HEADS_PER_STEP = 16
ONES_WIDTH = 256
ONES_WIDTH_B = 128
CHUNKS_PER_STEP = 4
RWKV_B_CHUNKS = 8
MOD_DMA_BANDS = 4
RMS_EPS = 1e-6
GN_EPS = 64e-5
L2_EPS = 1e-12
DECAY_SCALE = -math.exp(-0.5)
LANE = 128
SUBLANE = 8
BF16_SUBLANE = 16
VMEM_LIMIT = 56 * 1024 * 1024


def _params(*sem):
    return pltpu.CompilerParams(dimension_semantics=sem, vmem_limit_bytes=VMEM_LIMIT)


_NN = (((1,), (0,)), ((), ()))
_NT = (((1,), (1,)), ((), ()))
_TN = (((0,), (0,)), ((), ()))


def _dg(a, b, dims):
    return lax.dot_general(a, b, dims, preferred_element_type=F32)


def _split(x):
    hi = x.astype(BF16)
    lo = (x - hi.astype(F32)).astype(BF16)
    return hi, lo


def _mm3(a, b, dims=_NN):
    ah, al = _split(a)
    bh, bl = _split(b)
    return _dg(ah, bh, dims) + (_dg(ah, bl, dims) + _dg(al, bh, dims))


def _head_sums(x, ones_h):
    n = ones_h.shape[0]
    xb = x.astype(BF16)
    return jnp.concatenate([_dg(xb[:, c:c + n], ones_h, _NN) for c in range(0, x.shape[1], n)],
                           axis=1)


def _mm2_exact_lhs(a_bf16, b):
    bh, bl = _split(b)
    return _dg(a_bf16, bh, _NN) + _dg(a_bf16, bl, _NN)


def _softplus(x):
    return jnp.maximum(x, 0.0) + jnp.log1p(jnp.exp(-jnp.abs(x)))


def _iota2(shape):
    return (lax.broadcasted_iota(jnp.int32, shape, 0),
            lax.broadcasted_iota(jnp.int32, shape, 1))


def _head_ones(n):
    r, c = _iota2((n, n))
    return jnp.where((r // HEAD) == (c // HEAD), 1.0, 0.0).astype(BF16)


def _mod_body(c_ref, *refs):
    w_refs, b_ref, o_ref = refs[:-2], refs[-2], refs[-1]
    c = c_ref[...]
    ca = c * jax.nn.sigmoid(c)
    kb = w_refs[0].shape[0]
    acc = b_ref[...]
    for s, w_ref in enumerate(w_refs):
        acc = acc + _mm3(ca[:, s * kb:(s + 1) * kb], w_ref[...])
    o_ref[...] = acc


def _mod(c, w, b, tn=1024, bands=MOD_DMA_BANDS):
    bsz, d = c.shape
    n = w.shape[1]
    kb = d // bands
    return pl.pallas_call(
        _mod_body,
        grid=(n // tn,),
        in_specs=[pl.BlockSpec((bsz, d), lambda j: (0, 0))]
                 + [pl.BlockSpec((kb, tn), lambda j, s=s: (s, j)) for s in range(bands)]
                 + [pl.BlockSpec((1, tn), lambda j: (0, j))],
        out_specs=pl.BlockSpec((bsz, tn), lambda j: (0, j)),
        out_shape=jax.ShapeDtypeStruct((bsz, n), F32),
        compiler_params=_params("parallel"),
        name="mod",
    )(c, *([w] * bands), b)


def _norm_mod(x, g, sh, sc):
    y = x * lax.rsqrt(jnp.mean(x * x, axis=-1, keepdims=True) + RMS_EPS) * g
    return y * (1.0 + sc) + sh


def _norm_body(x_ref, g_ref, sh_ref, sc_ref, w_ref, o_ref, pl_ref, wb_ref):
    @pl.when(pl.program_id(0) == 0)
    def _():
        wb_ref[...] = w_ref[...].astype(BF16)

    h = _norm_mod(x_ref[...], g_ref[...], sh_ref[0], sc_ref[0]).astype(BF16)
    o_ref[...] = h
    pl_ref[...] = _dg(h, wb_ref[...], _NT)


def _norm(x2, g, sh, sc, w_lora_t, seq, tm=512):
    m, d = x2.shape
    nl = w_lora_t.shape[0]
    per_b = seq // tm
    return pl.pallas_call(
        _norm_body,
        grid=(m // tm,),
        in_specs=[pl.BlockSpec((tm, d), lambda i: (i, 0)),
                  pl.BlockSpec((1, d), lambda i: (0, 0)),
                  pl.BlockSpec((1, 1, d), lambda i: (i // per_b, 0, 0)),
                  pl.BlockSpec((1, 1, d), lambda i: (i // per_b, 0, 0)),
                  pl.BlockSpec((nl, d), lambda i: (0, 0))],
        out_specs=[pl.BlockSpec((tm, d), lambda i: (i, 0)), pl.BlockSpec((tm, nl), lambda i: (i, 0))],
        out_shape=[jax.ShapeDtypeStruct((m, d), BF16), jax.ShapeDtypeStruct((m, nl), F32)],
        scratch_shapes=[pltpu.VMEM((nl, d), BF16)],
        compiler_params=_params("arbitrary"),
        name="norm_mix",
    )(x2, g, sh, sc, w_lora_t)


def _mm_in_body(h_ref, w_ref, o_ref, wb_ref, *, gelu_tile):
    @pl.when(pl.program_id(1) == 0)
    def _():
        wb_ref[...] = w_ref[...].astype(BF16)

    @pl.when(pl.program_id(0) == gelu_tile)
    def _():
        o_ref[...] = jax.nn.gelu(_dg(h_ref[...], wb_ref[...], _NT))

    @pl.when(pl.program_id(0) != gelu_tile)
    def _():
        o_ref[...] = _dg(h_ref[...], wb_ref[...], _NT)


def _mm_in(h, wt, ncols, gelu_tile, tm=1024, tn=1024):
    m, d = h.shape
    return pl.pallas_call(
        functools.partial(_mm_in_body, gelu_tile=gelu_tile),
        grid=(ncols // tn, m // tm),
        in_specs=[pl.BlockSpec((tm, d), lambda j, i: (i, 0)),
                  pl.BlockSpec((tn, d), lambda j, i: (j, 0))],
        out_specs=pl.BlockSpec((tm, tn), lambda j, i: (i, j)),
        out_shape=jax.ShapeDtypeStruct((m, ncols), F32),
        scratch_shapes=[pltpu.VMEM((tn, d), BF16)],
        compiler_params=_params("parallel", "arbitrary"),
        name="mm_in",
    )(h, wt)


def _lru_head(h, first, u_ref, gate_ref, halo_ref, cw_ref, cb_ref, wa_ref, wx_ref, ba_ref, bx_ref,
              lam_ref, o_ref, carry_ref):
    tt = u_ref.shape[0]
    hd = u_ref.shape[1] // LRU_HEADS
    cs = slice(h * hd, (h + 1) * hd)
    p = u_ref[:, cs]
    halo = jnp.where(first, 0.0, halo_ref[:, cs])
    ext = jnp.concatenate([halo, p], axis=0)
    cw = cw_ref[:, cs]
    u = cb_ref[:, cs] + p * cw[CONV_WIDTH - 1:CONV_WIDTH, :]
    for j in range(1, CONV_WIDTH):
        shifted = pltpu.roll(ext, j, 0)[SUBLANE:, :]
        u = u + shifted * cw[CONV_WIDTH - 1 - j:CONV_WIDTH - j, :]
    ub = u.astype(BF16)
    ra = jnp.dot(ub, wa_ref[h], preferred_element_type=F32)
    rx = jnp.dot(ub, wx_ref[h], preferred_element_type=F32)
    yield
    r = jax.nn.sigmoid(ra + ba_ref[:, cs])
    ig = jax.nn.sigmoid(rx + bx_ref[:, cs])
    a = jnp.exp(r * ((-LRU_C) * _softplus(-lam_ref[:, cs])))
    mult = jnp.sqrt(1.0 - a * a)
    row = lax.broadcasted_iota(jnp.int32, (tt, hd), 0)
    mult = jnp.where(jnp.logical_and(first, row == 0), 1.0, mult)
    b = mult * (ig * u)

    groups = tt // SUBLANE
    a3 = a.reshape(groups, SUBLANE, hd)
    b3 = b.reshape(groups, SUBLANE, hd)
    sub = lax.broadcasted_iota(jnp.int32, (groups, SUBLANE, hd), 1)
    s = 1
    while s < SUBLANE:
        keep = sub >= s
        a_s = jnp.where(keep, pltpu.roll(a3, s, 1), 1.0)
        b_s = jnp.where(keep, pltpu.roll(b3, s, 1), 0.0)
        b3 = a3 * b_s + b3
        a3 = a3 * a_s
        s *= 2
    yield
    gate = gate_ref[:, cs]
    carry = carry_ref[:, cs]
    outs = []
    for g in range(groups):
        hh = b3[g] + a3[g] * carry
        carry = hh[SUBLANE - 1:SUBLANE, :]
        outs.append(hh * gate[g * SUBLANE:(g + 1) * SUBLANE, :])
    per = BF16_SUBLANE // SUBLANE
    for t0 in range(0, groups, per):
        o_ref[t0 * SUBLANE:(t0 + per) * SUBLANE, cs] = jnp.concatenate(
            outs[t0:t0 + per], axis=0).astype(BF16)
    carry_ref[:, cs] = carry
    yield


def _token_shift(x, halo, mu, first, row):
    prev = jnp.where(first, 0.0, halo[SUBLANE - 1:SUBLANE, :])
    xs = jnp.where(row == 0, prev, pltpu.roll(x, 1, 0))
    return x + (xs - x) * mu


def _pair_diag(y, left):
    return jnp.concatenate([jnp.where(left, y, 0.0), jnp.where(left, 0.0, y)], axis=0).astype(BF16)


def _pair_mm(x, y, left):
    return _dg(x.astype(BF16), _pair_diag(y, left), _NN)


def _chunk_chain(ops, store):
    ab_, bb_, kb_, rb_, v_, bt_, kt_, pe_ = ops
    rc, lane = _iota2((CHUNK, PAIR))
    cc = lane % HEAD
    left = lane < HEAD
    strict = rc > cc
    incl = rc >= cc
    diag = rc == cc
    ar16 = [jnp.concatenate([x, y], axis=0).astype(BF16) for x, y in zip(ab_, rb_)]
    bd_b = [_pair_diag(x, left) for x in bb_]
    bd_k = [_pair_diag(x, left) for x in kb_]
    bd_v = [_pair_diag(x, left) for x in v_]
    arb = [_dg(x, y, _NT) for x, y in zip(ar16, bd_b)]
    ark = [_dg(x, y, _NT) for x, y in zip(ar16, bd_k)]
    a_ab = [jnp.where(strict, x[:CHUNK], 0.0) for x in arb]
    a_rb = [jnp.where(incl, x[CHUNK:], 0.0).astype(BF16) for x in arb]
    a_akrk = [jnp.concatenate([jnp.where(strict, x[:CHUNK], 0.0), jnp.where(incl, x[CHUNK:], 0.0)],
                              axis=0).astype(BF16) for x in ark]
    yield
    base = 8
    d = [jnp.where((rc // base) == (cc // base), a, 0.0) for a in a_ab]
    d2 = [_pair_mm(t, t, left) for t in d]
    akrkv = [_dg(x, y, _NN) for x, y in zip(a_akrk, bd_v)]
    akv = [x[:CHUNK] for x in akrkv]
    rkv = [x[CHUNK:] for x in akrkv]
    x = [jnp.where(diag, 1.0, 0.0) + t for t in d]
    yield
    x = [xi + _pair_mm(t2, xi, left) for xi, t2 in zip(x, d2)]
    d4 = [_pair_mm(t2, t2, left) for t2 in d2]
    yield
    x = [xi + _pair_mm(t4, xi, left) for xi, t4 in zip(x, d4)]
    yield
    size = base
    while size < CHUNK:
        off = jnp.logical_and((rc // (2 * size)) == (cc // (2 * size)),
                              (rc // size) != (cc // size))
        o = [jnp.where(off, a, 0.0) for a in a_ab]
        ox = [_pair_mm(oi, xi, left) for oi, xi in zip(o, x)]
        yield
        x = [xi + _pair_mm(xi, oxi, left) for xi, oxi in zip(x, ox)]
        yield
        size *= 2
    t = [xi.astype(BF16) for xi in x]
    wu = [_dg(ti, jnp.concatenate([_pair_diag(y, left), _pair_diag(z, left)], axis=1), _NN)
          for ti, y, z in zip(t, ab_, akv)]
    kv = [_dg(xi.astype(BF16), y.astype(BF16), _TN) for xi, y in zip(kt_, v_)]
    yield
    ry = [_dg(xi, jnp.concatenate([_pair_diag(y[:, :PAIR], left), _pair_diag(y[:, PAIR:], left)], axis=1), _NN)
          for xi, y in zip(a_rb, wu)]
    mn = [_dg(xi.astype(BF16), y.astype(BF16), _TN) for xi, y in zip(bt_, wu)]
    yield

    def head_blocks(z):
        return jnp.where(left, z[:HEAD, :], z[HEAD:, :])

    for u in range(len(ab_)):
        store(u,
              rb_[u] + ry[u][:, :PAIR],
              ry[u][:, PAIR:] + rkv[u],
              jnp.where(diag, pe_[u], 0.0) + head_blocks(mn[u][:, :PAIR]),
              head_blocks(mn[u][:, PAIR:]) + head_blocks(kv[u]))


def _rwkv_a_body(r_ref, k_ref, v_ref, l_ref, rh_ref, kh_ref, vh_ref, lh_ref,
                 mur_ref, muk_ref, muv_ref, mul_ref, w0_ref, a0_ref, kkw_ref, kaw_ref, rkw_ref,
                 w2_ref, a2_ref, g2_ref, ones_ref, tri_ref,
                 u_ref, gate_ref, halo_ref, cw_ref, cb_ref, wa_ref, wx_ref, ba_ref, bx_ref, lam_ref,
                 *rest):
    ncast = (len(rest) - 8) // 2
    cast_in = rest[:ncast]
    rp_ref, yp_ref, m_ref, n_ref, bonus_ref, g_ref, ya_ref = rest[ncast:ncast + 7]
    cast_out = rest[ncast + 7:-1]
    carry_ref = rest[-1]
    first = pl.program_id(1) == 0

    @pl.when(first)
    def _():
        carry_ref[...] = jnp.zeros_like(carry_ref)
    cl = CHUNK
    rows = CHUNKS_PER_STEP * cl
    width = HEADS_PER_STEP * HEAD
    gw = ones_ref.shape[0]
    row_g = lax.broadcasted_iota(jnp.int32, (rows, gw), 0)
    row_l = lax.broadcasted_iota(jnp.int32, (rows, l_ref.shape[1]), 0)
    ones_h = ones_ref[...]

    lo = _token_shift(l_ref[...], lh_ref[...], mul_ref[...], first, row_l)
    act_w = _split(jnp.tanh(lo[:, 0:LANE]))
    act_a = _split(lo[:, LANE:2 * LANE])
    act_g = _split(jax.nn.sigmoid(lo[:, 2 * LANE:]))

    def lora(act, w_ref, cs, keep_low):
        (ah, al_), wb = act, w_ref[:, cs]
        out = _dg(ah, wb, _NN)
        return out + _dg(al_, wb, _NN) if keep_low else out

    def prologue(c0, out):
        cs = slice(c0, c0 + gw)
        r = _token_shift(r_ref[:, cs], rh_ref[:, cs], mur_ref[:, cs], first, row_g)
        k = _token_shift(k_ref[:, cs], kh_ref[:, cs], muk_ref[:, cs], first, row_g)
        v = _token_shift(v_ref[:, cs], vh_ref[:, cs], muv_ref[:, cs], first, row_g)
        w_lin = w0_ref[:, cs] + lora(act_w, w2_ref, cs, True)
        a_lin = a0_ref[:, cs] + lora(act_a, a2_ref, cs, False)
        g_ref[:, cs] = lora(act_g, g2_ref, cs, False).astype(BF16)
        kk = k * kkw_ref[:, cs]
        kk_ss = _head_sums(kk * kk, ones_h)
        yield
        lw = DECAY_SCALE * jax.nn.sigmoid(w_lin)
        a = jax.nn.sigmoid(a_lin)
        kk = kk * lax.rsqrt(jnp.maximum(kk_ss, L2_EPS * L2_EPS))
        kp = k * (1.0 + (a - 1.0) * kaw_ref[:, cs])
        bonus_ref[:, cs] = (_head_sums(r * kp * rkw_ref[:, cs], ones_h) * v).astype(BF16)
        lc = _mm2_exact_lhs(tri_ref[...], lw)
        yield
        p_incl = jnp.exp(lc)
        p_excl = jnp.exp(lc - lw)
        p_inv = 1.0 / p_incl
        p_end = jnp.concatenate(
            [jnp.broadcast_to(p_incl[(j + 1) * cl - 1:(j + 1) * cl, :], (cl, gw))
             for j in range(CHUNKS_PER_STEP)], axis=0)
        abar = -(kk * p_excl)
        bbar = kk * a * p_inv
        kbar = kp * p_inv
        rbar = r * p_incl
        btil = bbar * p_end
        ktil = kbar * p_end
        units = [(j, q) for j in range(CHUNKS_PER_STEP) for q in range(gw // PAIR)]
        out.extend([x[j * cl:(j + 1) * cl, q * PAIR:(q + 1) * PAIR] for j, q in units]
                   for x in (abar, bbar, kbar, rbar, v, btil, ktil, p_end))
        yield

    def make_store(c0):
        units = [(j, q) for j in range(CHUNKS_PER_STEP) for q in range(gw // PAIR)]

        def store(u, rp, yp, mm, nn):
            j, q = units[u]
            rs = slice(j * cl, (j + 1) * cl)
            qs = slice(c0 + q * PAIR, c0 + (q + 1) * PAIR)
            rp_ref[rs, qs] = rp.astype(BF16)
            yp_ref[rs, qs] = yp.astype(BF16)
            m_ref[rs, qs] = mm.astype(BF16)
            n_ref[rs, qs] = nn.astype(BF16)
        return store

    lru = (None for h in range(LRU_HEADS)
           for _ in _lru_head(h, first, u_ref, gate_ref, halo_ref, cw_ref, cb_ref, wa_ref, wx_ref,
                              ba_ref, bx_ref, lam_ref, ya_ref, carry_ref))
    chains = []
    for c0 in range(0, width, gw):
        ops = []
        for _ in prologue(c0, ops):
            for ch in chains:
                next(ch, None)
        chains.append(_chunk_chain(ops, make_store(c0)))
    live = list(chains)
    while live:
        live = [ch for ch in live if next(ch, StopIteration) is not StopIteration]
        next(lru, None)
    for _ in lru:
        pass

    for src, dst in zip(cast_in, cast_out):
        dst[...] = src[...].astype(BF16)


def _rwkv_a(p, p_lora, mu_rkv, mu_lora, w0, a0, k_k, k_a, r_k, w2p, a2p, g2p,
            conv_w, conv_b, wa, wx, ba, bx, lam, bsz, seq, rkv_col0, cast_ws=()):
    dl = conv_w.shape[1]
    lvec = lambda t: t.reshape(1, dl)
    lrow = pl.BlockSpec((1, dl), lambda b, i, q: (0, 0))

    def lru_tile(cb):
        return pl.BlockSpec((CHUNKS_PER_STEP * CHUNK, dl), lambda b, i, q: (b * nc + i, cb))
    cl = CHUNKS_PER_STEP * CHUNK
    width = HEADS_PER_STEP * HEAD
    dr = w0.shape[1]
    ngroups = dr // width
    assert ngroups == 1, "the LRU ride-along expects one grid step per row tile"
    nc = seq // cl
    lw_ = mu_lora.shape[1]
    cb0 = rkv_col0 // width
    rows8 = cl // SUBLANE
    rt, ct = _iota2((cl, cl))
    tri = jnp.where(jnp.logical_and(rt >= ct, (rt // CHUNK) == (ct // CHUNK)), 1.0, 0.0).astype(BF16)
    ones_h = _head_ones(ONES_WIDTH)
    const = lambda arr: pl.BlockSpec(arr.shape, lambda b, i, q: (0, 0))
    lora_w = [wgt.astype(BF16) for wgt in (w2p, a2p, g2p)]

    def tile(cb_off):
        return pl.BlockSpec((cl, width), lambda b, i, q: (b * nc + i, cb0 + cb_off + q))

    def halo(cb_off):
        return pl.BlockSpec(
            (SUBLANE, width),
            lambda b, i, q: (jnp.maximum((b * nc + i) * rows8 - 1, 0), cb0 + cb_off + q))

    def prow(off=0):
        return pl.BlockSpec((1, width), lambda b, i, q: (0, off + q))

    out_tile = pl.BlockSpec((cl, width), lambda b, i, q: (b * nc + i, q))
    out_mat = pl.BlockSpec((CHUNKS_PER_STEP * HEAD, width), lambda b, i, q: (b * nc + i, q))
    act = jax.ShapeDtypeStruct((bsz * seq, dr), BF16)
    mat = jax.ShapeDtypeStruct((bsz * (seq // CHUNK) * HEAD, dr), BF16)

    nsteps = bsz * nc * ngroups
    cast_specs = []
    for wgt in cast_ws:
        hold = 1
        while (wgt.shape[0] * hold) % (nsteps * BF16_SUBLANE) != 0:
            hold *= 2
        blk = (wgt.shape[0] * hold // nsteps, wgt.shape[1])
        cast_specs.append(pl.BlockSpec(
            blk, lambda b, i, q, hold=hold: (((b * nc + i) * ngroups + q) // hold, 0)))
    cast_shapes = [jax.ShapeDtypeStruct(wgt.shape, BF16) for wgt in cast_ws]

    return pl.pallas_call(
        _rwkv_a_body,
        grid=(bsz, nc, ngroups),
        in_specs=[tile(0), tile(ngroups), tile(2 * ngroups),
                  pl.BlockSpec((cl, lw_), lambda b, i, q: (b * nc + i, 0)),
                  halo(0), halo(ngroups), halo(2 * ngroups),
                  pl.BlockSpec((SUBLANE, lw_),
                               lambda b, i, q: (jnp.maximum((b * nc + i) * rows8 - 1, 0), 0)),
                  prow(0), prow(ngroups), prow(2 * ngroups),
                  pl.BlockSpec((1, lw_), lambda b, i, q: (0, 0)),
                  prow(), prow(), prow(), prow(), prow()]
                 + [pl.BlockSpec((t.shape[0], width), lambda b, i, q: (0, q)) for t in lora_w]
                 + [const(ones_h), const(tri)]
                 + [lru_tile(0), lru_tile(1),
                    pl.BlockSpec((SUBLANE, dl),
                                 lambda b, i, q: (jnp.maximum((b * nc + i) * rows8 - 1, 0), 0)),
                    const(conv_w), lrow, pl.BlockSpec(wa.shape, lambda b, i, q: (0, 0, 0)),
                    pl.BlockSpec(wx.shape, lambda b, i, q: (0, 0, 0)), lrow, lrow, lrow]
                 + cast_specs,
        out_specs=[out_tile, out_tile, out_mat, out_mat, out_tile, out_tile, lru_tile(0)] + cast_specs,
        out_shape=[act, act, mat, mat, act, act, jax.ShapeDtypeStruct((bsz * seq, dl), BF16)]
                  + cast_shapes,
        scratch_shapes=[pltpu.VMEM((1, dl), F32)],
        compiler_params=_params("arbitrary", "arbitrary", "arbitrary"),
        name="rwkv_a",
    )(p, p, p, p_lora, p, p, p, p_lora, mu_rkv, mu_rkv, mu_rkv, mu_lora, w0, a0, k_k, k_a, r_k,
      *lora_w, ones_h, tri,
      p, p, p, conv_w, lvec(conv_b), wa, wx, lvec(ba), lvec(bx), lvec(lam), *cast_ws)


def _rwkv_b_body(rp_ref, yp_ref, m_ref, n_ref, bonus_ref, g_ref, lng_ref, lnb_ref, ones_ref,
                 o_ref, state_ref):
    @pl.when(pl.program_id(1) == 0)
    def _():
        state_ref[...] = jnp.zeros_like(state_ref)

    npairs = state_ref.shape[0]
    pairs = range(npairs)
    ps = [slice(q * PAIR, (q + 1) * PAIR) for q in pairs]
    left = lax.broadcasted_iota(jnp.int32, (HEAD, PAIR), 1) < HEAD
    ones_h = ones_ref[...]
    inv_n = 1.0 / HEAD
    state = [state_ref[q] for q in pairs]
    for j in range(rp_ref.shape[0] // CHUNK):
        rs = slice(j * CHUNK, (j + 1) * CHUNK)
        ks = slice(j * HEAD, (j + 1) * HEAD)
        g0 = [_pair_diag(state[q], left) for q in pairs]
        ys = [_dg(rp_ref[rs, ps[q]], g0[q], _NN) + yp_ref[rs, ps[q]] for q in pairs]
        state = [_dg(m_ref[ks, ps[q]], g0[q], _NN) + n_ref[ks, ps[q]] for q in pairs]
        y = jnp.concatenate(ys, axis=1)
        yc = y - _head_sums(y, ones_h) * inv_n
        var = _head_sums(yc * yc, ones_h) * inv_n
        yn = yc * lax.rsqrt(var + GN_EPS) * lng_ref[...] + lnb_ref[...]
        o_ref[rs, :] = ((yn + bonus_ref[rs, :]) * g_ref[rs, :]).astype(BF16)
    for q in pairs:
        state_ref[q] = state[q]


def _rwkv_b(rp, yp, mc, nm, bonus, g, ln_g, ln_b, bsz, seq):
    cl = RWKV_B_CHUNKS * CHUNK
    dr = rp.shape[1]
    nc = seq // cl
    tile = pl.BlockSpec((cl, dr), lambda b, i: (b * nc + i, 0))
    mat = pl.BlockSpec((RWKV_B_CHUNKS * HEAD, dr), lambda b, i: (b * nc + i, 0))
    prow = pl.BlockSpec((1, dr), lambda b, i: (0, 0))
    ones_h = _head_ones(ONES_WIDTH_B)
    return pl.pallas_call(
        _rwkv_b_body,
        grid=(bsz, nc),
        in_specs=[tile, tile, mat, mat, tile, tile, prow, prow,
                  pl.BlockSpec(ones_h.shape, lambda b, i: (0, 0))],
        out_specs=tile,
        out_shape=jax.ShapeDtypeStruct((bsz * seq, dr), BF16),
        scratch_shapes=[pltpu.VMEM((dr // PAIR, HEAD, PAIR), F32)],
        compiler_params=_params("parallel", "arbitrary"),
        name="rwkv_b",
    )(rp, yp, mc, nm, bonus, g, ln_g, ln_b, ones_h)


def _mm_out_body(ya_ref, yb_ref, x_ref, gm_ref, w_ref, g_ref, sh_ref, sc_ref, o_ref, h_ref, *, sub):
    da = ya_ref.shape[1]
    for r0 in range(0, x_ref.shape[0], sub):
        rs = slice(r0, r0 + sub)
        mix = (jnp.dot(ya_ref[rs, :], w_ref[:da, :], preferred_element_type=F32)
               + jnp.dot(yb_ref[rs, :], w_ref[da:, :], preferred_element_type=F32))
        x1 = x_ref[rs, :] + gm_ref[0] * mix
        o_ref[rs, :] = x1
        h_ref[rs, :] = _norm_mod(x1, g_ref[...], sh_ref[0], sc_ref[0]).astype(BF16)


def _mm_out(ya, yb, x2, gm, w, g, sh, sc, seq, tm=512, sub=256):
    m, d = x2.shape
    per_b = seq // tm
    brow = pl.BlockSpec((1, 1, d), lambda i: (i // per_b, 0, 0))
    tile = pl.BlockSpec((tm, d), lambda i: (i, 0))
    return pl.pallas_call(
        functools.partial(_mm_out_body, sub=sub),
        grid=(m // tm,),
        in_specs=[pl.BlockSpec((tm, ya.shape[1]), lambda i: (i, 0)),
                  pl.BlockSpec((tm, yb.shape[1]), lambda i: (i, 0)),
                  tile, brow,
                  pl.BlockSpec(w.shape, lambda i: (0, 0)),
                  pl.BlockSpec((1, d), lambda i: (0, 0)), brow, brow],
        out_specs=[tile, tile],
        out_shape=[jax.ShapeDtypeStruct((m, d), F32), jax.ShapeDtypeStruct((m, d), BF16)],
        compiler_params=_params("parallel"),
        name="mm_out",
    )(ya, yb, x2, gm, w, g, sh, sc)


def _ffn_body(x_ref, h_ref, gf_ref, wg_ref, wu_ref, wd_ref, fg_ref, o_ref, acc_ref):
    f = pl.program_id(1)

    @pl.when(f == 0)
    def _():
        acc_ref[...] = jnp.zeros_like(acc_ref)

    h = h_ref[...]
    gate = jnp.dot(h, wg_ref[...], preferred_element_type=F32)
    up = jnp.dot(h, wu_ref[...], preferred_element_type=F32)
    act = (gate * jax.nn.sigmoid(gate) * up).astype(BF16)
    acc_ref[...] += jnp.dot(act, wd_ref[...], preferred_element_type=F32)

    @pl.when(f == pl.num_programs(1) - 1)
    def _():
        y = x_ref[...] + gf_ref[0] * acc_ref[...]
        o_ref[...] = (y * lax.rsqrt(jnp.mean(y * y, axis=-1, keepdims=True) + RMS_EPS)
                      * fg_ref[...])


def _ffn(x1, h2, gf, w_gu, w_down, fg, seq, tm=512, tf=512):
    m, d = x1.shape
    dff = w_down.shape[0]
    nf = dff // tf
    assert seq % tm == 0 and dff % tf == 0, "row tiles must not straddle sequences"
    per_b = seq // tm
    tile = pl.BlockSpec((tm, d), lambda i, f: (i, 0))
    prow = pl.BlockSpec((1, d), lambda i, f: (0, 0))
    return pl.pallas_call(
        _ffn_body,
        grid=(m // tm, nf),
        in_specs=[tile, tile,
                  pl.BlockSpec((1, 1, d), lambda i, f: (i // per_b, 0, 0)),
                  pl.BlockSpec((d, tf), lambda i, f: (0, f)),
                  pl.BlockSpec((d, tf), lambda i, f: (0, nf + f)),
                  pl.BlockSpec((tf, d), lambda i, f: (f, 0)),
                  prow],
        out_specs=tile,
        out_shape=jax.ShapeDtypeStruct((m, d), F32),
        scratch_shapes=[pltpu.VMEM((tm, d), F32)],
        compiler_params=_params("parallel", "arbitrary"),
        name="ffn",
    )(x1, h2, gf, w_gu, w_gu, w_down, fg)


def _pad_cols(w, n):
    return jnp.pad(w, ((0, 0), (0, n - w.shape[1])))


def _pad_rows(w, n):
    return jnp.pad(w, ((0, n - w.shape[0]), (0, 0)))


def kernel(x, c, w_ada, b_ada, norm_mix_g, w_in, conv_w, conv_b, lru_wa, lru_ba, lru_wx, lru_bx, lru_lambda, rwkv_mu, rwkv_w0, rwkv_w2, rwkv_a0, rwkv_a2, rwkv_g2, rwkv_k_k, rwkv_k_a, rwkv_r_k, rwkv_ln_g, rwkv_ln_b, w_out, norm_ffn_g, w_gu, w_down, final_norm_g):
    bsz, seq, d = x.shape
    depth = w_ada.shape[0]
    assert depth == 1, "the closing RMSNorm is fused into the (single) layer's ffn kernel"
    dl = conv_w.shape[2]
    dr = rwkv_w0.shape[1]
    w_lora, a_lora, g_lora = rwkv_w2.shape[1], rwkv_a2.shape[1], rwkv_g2.shape[1]
    wpad, apad = LANE, LANE
    gpad = -(-g_lora // LANE) * LANE
    rkv_col0 = 2 * dl
    lora0 = rkv_col0 + 3 * dr

    x2 = x.reshape(bsz * seq, d)
    for l in range(depth):
        mod = _mod(c, w_ada[l], b_ada[l].reshape(1, -1))
        sh_m, sc_m, g_m, sh_f, sc_f, g_f = [t.reshape(bsz, 1, d) for t in jnp.split(mod, 6, axis=-1)]

        wi = jnp.swapaxes(w_in[l], 0, 1)
        o1, o2 = lora0 + w_lora, lora0 + w_lora + a_lora
        w_lora_p = jnp.concatenate(
            [_pad_rows(wi[lora0:o1], wpad), _pad_rows(wi[o1:o2], apad),
             _pad_rows(wi[o2:], gpad)], axis=0)
        mu = rwkv_mu[l].reshape(1, -1)
        mu_rkv = mu[:, :3 * dr]
        mu_lora = jnp.concatenate(
            [_pad_cols(mu[:, 3 * dr:3 * dr + w_lora], wpad),
             _pad_cols(mu[:, 3 * dr + w_lora:3 * dr + w_lora + a_lora], apad),
             _pad_cols(mu[:, 3 * dr + w_lora + a_lora:], gpad)], axis=1)
        w2p = _pad_rows(rwkv_w2[l], wpad)
        a2p = _pad_rows(rwkv_a2[l], apad)
        g2p = _pad_rows(rwkv_g2[l], gpad)

        h, p_lora = _norm(x2, norm_mix_g[l].reshape(1, d), sh_m, sc_m, w_lora_p, seq)
        p = _mm_in(h, wi, lora0, gelu_tile=1, tn=dl)

        rowv = lambda t: t.reshape(1, dr)
        rp, yp, mc, nm, bonus, gg, y_a, w_out_b, w_gu_b, w_down_b = _rwkv_a(
            p, p_lora, mu_rkv, mu_lora, rowv(rwkv_w0[l]), rowv(rwkv_a0[l]), rowv(rwkv_k_k[l]),
            rowv(rwkv_k_a[l]), rowv(rwkv_r_k[l]), w2p, a2p, g2p,
            conv_w[l], conv_b[l], lru_wa[l].astype(BF16), lru_wx[l].astype(BF16),
            lru_ba[l], lru_bx[l], lru_lambda[l], bsz, seq, rkv_col0,
            cast_ws=(w_out[l], w_gu[l], w_down[l]))
        y_b = _rwkv_b(rp, yp, mc, nm, bonus, gg, rowv(rwkv_ln_g[l]), rowv(rwkv_ln_b[l]), bsz, seq)

        x2, h2 = _mm_out(y_a, y_b, x2, g_m, w_out_b, norm_ffn_g[l].reshape(1, d), sh_f, sc_f, seq)
        x2 = _ffn(x2, h2, g_f, w_gu_b, w_down_b, final_norm_g.reshape(1, d), seq)
    return x2.reshape(bsz, seq, d)
```

```python
import functools
import math

import jax
import jax.numpy as jnp
from jax import lax
from jax.experimental import pallas as pl
from jax.experimental.pallas import tpu as pltpu

F32 = jnp.float32
BF16 = jnp.bfloat16

LRU_HEADS = 4
CONV_WIDTH = 4
LRU_C = 8.0
HEAD = 64
CHUNK = 64
PAIR = 2 * HEAD
HEADS_PER_STEP = 16
ONES_WIDTH = 256
ONES_WIDTH_B = 128
CHUNKS_PER_STEP = 4
RWKV_B_CHUNKS = 8
MOD_DMA_BANDS = 4
RMS_EPS = 1e-6
GN_EPS = 64e-5
L2_EPS = 1e-12
DECAY_SCALE = -math.exp(-0.5)
LANE = 128
SUBLANE = 8
BF16_SUBLANE = 16
VMEM_LIMIT = 56 * 1024 * 1024


def _params(*sem):
    return pltpu.CompilerParams(dimension_semantics=sem, vmem_limit_bytes=VMEM_LIMIT)


_NN = (((1,), (0,)), ((), ()))
_NT = (((1,), (1,)), ((), ()))
_TN = (((0,), (0,)), ((), ()))


def _dg(a, b, dims):
    return lax.dot_general(a, b, dims, preferred_element_type=F32)


def _split(x):
    hi = x.astype(BF16)
    lo = (x - hi.astype(F32)).astype(BF16)
    return hi, lo


def _mm3(a, b, dims=_NN):
    ah, al = _split(a)
    bh, bl = _split(b)
    return _dg(ah, bh, dims) + (_dg(ah, bl, dims) + _dg(al, bh, dims))


def _head_sums(x, ones_h):
    n = ones_h.shape[0]
    xb = x.astype(BF16)
    return jnp.concatenate([_dg(xb[:, c:c + n], ones_h, _NN) for c in range(0, x.shape[1], n)],
                           axis=1)


def _mm2_exact_lhs(a_bf16, b):
    bh, bl = _split(b)
    return _dg(a_bf16, bh, _NN) + _dg(a_bf16, bl, _NN)


def _softplus(x):
    return jnp.maximum(x, 0.0) + jnp.log1p(jnp.exp(-jnp.abs(x)))


def _iota2(shape):
    return (lax.broadcasted_iota(jnp.int32, shape, 0),
            lax.broadcasted_iota(jnp.int32, shape, 1))


def _head_ones(n):
    r, c = _iota2((n, n))
    return jnp.where((r // HEAD) == (c // HEAD), 1.0, 0.0).astype(BF16)


def _mod_body(c_ref, *refs):
    w_refs, b_ref, o_ref = refs[:-2], refs[-2], refs[-1]
    c = c_ref[...]
    ca = c * jax.nn.sigmoid(c)
    kb = w_refs[0].shape[0]
    acc = b_ref[...]
    for s, w_ref in enumerate(w_refs):
        acc = acc + _mm3(ca[:, s * kb:(s + 1) * kb], w_ref[...])
    o_ref[...] = acc


def _mod(c, w, b, tn=1024, bands=MOD_DMA_BANDS):
    bsz, d = c.shape
    n = w.shape[1]
    kb = d // bands
    return pl.pallas_call(
        _mod_body,
        grid=(n // tn,),
        in_specs=[pl.BlockSpec((bsz, d), lambda j: (0, 0))]
                 + [pl.BlockSpec((kb, tn), lambda j, s=s: (s, j)) for s in range(bands)]
                 + [pl.BlockSpec((1, tn), lambda j: (0, j))],
        out_specs=pl.BlockSpec((bsz, tn), lambda j: (0, j)),
        out_shape=jax.ShapeDtypeStruct((bsz, n), F32),
        compiler_params=_params("parallel"),
        name="mod",
    )(c, *([w] * bands), b)


def _norm_mod(x, g, sh, sc):
    y = x * lax.rsqrt(jnp.mean(x * x, axis=-1, keepdims=True) + RMS_EPS) * g
    return y * (1.0 + sc) + sh


def _norm_body(x_ref, g_ref, sh_ref, sc_ref, w_ref, o_ref, pl_ref, wb_ref):
    @pl.when(pl.program_id(0) == 0)
    def _():
        wb_ref[...] = w_ref[...].astype(BF16)

    h = _norm_mod(x_ref[...], g_ref[...], sh_ref[0], sc_ref[0]).astype(BF16)
    o_ref[...] = h
    pl_ref[...] = _dg(h, wb_ref[...], _NT)


def _norm(x2, g, sh, sc, w_lora_t, seq, tm=512):
    m, d = x2.shape
    nl = w_lora_t.shape[0]
    per_b = seq // tm
    return pl.pallas_call(
        _norm_body,
        grid=(m // tm,),
        in_specs=[pl.BlockSpec((tm, d), lambda i: (i, 0)),
                  pl.BlockSpec((1, d), lambda i: (0, 0)),
                  pl.BlockSpec((1, 1, d), lambda i: (i // per_b, 0, 0)),
                  pl.BlockSpec((1, 1, d), lambda i: (i // per_b, 0, 0)),
                  pl.BlockSpec((nl, d), lambda i: (0, 0))],
        out_specs=[pl.BlockSpec((tm, d), lambda i: (i, 0)), pl.BlockSpec((tm, nl), lambda i: (i, 0))],
        out_shape=[jax.ShapeDtypeStruct((m, d), BF16), jax.ShapeDtypeStruct((m, nl), F32)],
        scratch_shapes=[pltpu.VMEM((nl, d), BF16)],
        compiler_params=_params("arbitrary"),
        name="norm_mix",
    )(x2, g, sh, sc, w_lora_t)


def _mm_in_body(h_ref, w_ref, o_ref, wb_ref, *, gelu_tile):
    @pl.when(pl.program_id(1) == 0)
    def _():
        wb_ref[...] = w_ref[...].astype(BF16)

    @pl.when(pl.program_id(0) == gelu_tile)
    def _():
        o_ref[...] = jax.nn.gelu(_dg(h_ref[...], wb_ref[...], _NT))

    @pl.when(pl.program_id(0) != gelu_tile)
    def _():
        o_ref[...] = _dg(h_ref[...], wb_ref[...], _NT)


def _mm_in(h, wt, ncols, gelu_tile, tm=1024, tn=1024):
    m, d = h.shape
    return pl.pallas_call(
        functools.partial(_mm_in_body, gelu_tile=gelu_tile),
        grid=(ncols // tn, m // tm),
        in_specs=[pl.BlockSpec((tm, d), lambda j, i: (i, 0)),
                  pl.BlockSpec((tn, d), lambda j, i: (j, 0))],
        out_specs=pl.BlockSpec((tm, tn), lambda j, i: (i, j)),
        out_shape=jax.ShapeDtypeStruct((m, ncols), F32),
        scratch_shapes=[pltpu.VMEM((tn, d), BF16)],
        compiler_params=_params("parallel", "arbitrary"),
        name="mm_in",
    )(h, wt)


def _lru_head(h, first, u_ref, gate_ref, halo_ref, cw_ref, cb_ref, wa_ref, wx_ref, ba_ref, bx_ref,
              lam_ref, o_ref, carry_ref):
    tt = u_ref.shape[0]
    hd = u_ref.shape[1] // LRU_HEADS
    cs = slice(h * hd, (h + 1) * hd)
    p = u_ref[:, cs]
    halo = jnp.where(first, 0.0, halo_ref[:, cs])
    ext = jnp.concatenate([halo, p], axis=0)
    cw = cw_ref[:, cs]
    u = cb_ref[:, cs] + p * cw[CONV_WIDTH - 1:CONV_WIDTH, :]
    for j in range(1, CONV_WIDTH):
        shifted = pltpu.roll(ext, j, 0)[SUBLANE:, :]
        u = u + shifted * cw[CONV_WIDTH - 1 - j:CONV_WIDTH - j, :]
    ub = u.astype(BF16)
    ra = jnp.dot(ub, wa_ref[h], preferred_element_type=F32)
    rx = jnp.dot(ub, wx_ref[h], preferred_element_type=F32)
    yield
    r = jax.nn.sigmoid(ra + ba_ref[:, cs])
    ig = jax.nn.sigmoid(rx + bx_ref[:, cs])
    a = jnp.exp(r * ((-LRU_C) * _softplus(-lam_ref[:, cs])))
    mult = jnp.sqrt(1.0 - a * a)
    row = lax.broadcasted_iota(jnp.int32, (tt, hd), 0)
    mult = jnp.where(jnp.logical_and(first, row == 0), 1.0, mult)
    b = mult * (ig * u)

    groups = tt // SUBLANE
    a3 = a.reshape(groups, SUBLANE, hd)
    b3 = b.reshape(groups, SUBLANE, hd)
    sub = lax.broadcasted_iota(jnp.int32, (groups, SUBLANE, hd), 1)
    s = 1
    while s < SUBLANE:
        keep = sub >= s
        a_s = jnp.where(keep, pltpu.roll(a3, s, 1), 1.0)
        b_s = jnp.where(keep, pltpu.roll(b3, s, 1), 0.0)
        b3 = a3 * b_s + b3
        a3 = a3 * a_s
        s *= 2
    yield
    gate = gate_ref[:, cs]
    carry = carry_ref[:, cs]
    outs = []
    for g in range(groups):
        hh = b3[g] + a3[g] * carry
        carry = hh[SUBLANE - 1:SUBLANE, :]
        outs.append(hh * gate[g * SUBLANE:(g + 1) * SUBLANE, :])
    per = BF16_SUBLANE // SUBLANE
    for t0 in range(0, groups, per):
        o_ref[t0 * SUBLANE:(t0 + per) * SUBLANE, cs] = jnp.concatenate(
            outs[t0:t0 + per], axis=0).astype(BF16)
    carry_ref[:, cs] = carry
    yield


def _token_shift(x, halo, mu, first, row):
    prev = jnp.where(first, 0.0, halo[SUBLANE - 1:SUBLANE, :])
    xs = jnp.where(row == 0, prev, pltpu.roll(x, 1, 0))
    return x + (xs - x) * mu


def _pair_diag(y, left):
    return jnp.concatenate([jnp.where(left, y, 0.0), jnp.where(left, 0.0, y)], axis=0).astype(BF16)


def _pair_mm(x, y, left):
    return _dg(x.astype(BF16), _pair_diag(y, left), _NN)


def _chunk_chain(ops, store):
    ab_, bb_, kb_, rb_, v_, bt_, kt_, pe_ = ops
    rc, lane = _iota2((CHUNK, PAIR))
    cc = lane % HEAD
    left = lane < HEAD
    strict = rc > cc
    incl = rc >= cc
    diag = rc == cc
    ar16 = [jnp.concatenate([x, y], axis=0).astype(BF16) for x, y in zip(ab_, rb_)]
    bd_b = [_pair_diag(x, left) for x in bb_]
    bd_k = [_pair_diag(x, left) for x in kb_]
    bd_v = [_pair_diag(x, left) for x in v_]
    arb = [_dg(x, y, _NT) for x, y in zip(ar16, bd_b)]
    ark = [_dg(x, y, _NT) for x, y in zip(ar16, bd_k)]
    a_ab = [jnp.where(strict, x[:CHUNK], 0.0) for x in arb]
    a_rb = [jnp.where(incl, x[CHUNK:], 0.0).astype(BF16) for x in arb]
    a_akrk = [jnp.concatenate([jnp.where(strict, x[:CHUNK], 0.0), jnp.where(incl, x[CHUNK:], 0.0)],
                              axis=0).astype(BF16) for x in ark]
    yield
    base = 8
    d = [jnp.where((rc // base) == (cc // base), a, 0.0) for a in a_ab]
    d2 = [_pair_mm(t, t, left) for t in d]
    akrkv = [_dg(x, y, _NN) for x, y in zip(a_akrk, bd_v)]
    akv = [x[:CHUNK] for x in akrkv]
    rkv = [x[CHUNK:] for x in akrkv]
    x = [jnp.where(diag, 1.0, 0.0) + t for t in d]
    yield
    x = [xi + _pair_mm(t2, xi, left) for xi, t2 in zip(x, d2)]
    d4 = [_pair_mm(t2, t2, left) for t2 in d2]
    yield
    x = [xi + _pair_mm(t4, xi, left) for xi, t4 in zip(x, d4)]
    yield
    size = base
    while size < CHUNK:
        off = jnp.logical_and((rc // (2 * size)) == (cc // (2 * size)),
                              (rc // size) != (cc // size))
        o = [jnp.where(off, a, 0.0) for a in a_ab]
        ox = [_pair_mm(oi, xi, left) for oi, xi in zip(o, x)]
        yield
        x = [xi + _pair_mm(xi, oxi, left) for xi, oxi in zip(x, ox)]
        yield
        size *= 2
    t = [xi.astype(BF16) for xi in x]
    wu = [_dg(ti, jnp.concatenate([_pair_diag(y, left), _pair_diag(z, left)], axis=1), _NN)
          for ti, y, z in zip(t, ab_, akv)]
    kv = [_dg(xi.astype(BF16), y.astype(BF16), _TN) for xi, y in zip(kt_, v_)]
    yield
    ry = [_dg(xi, jnp.concatenate([_pair_diag(y[:, :PAIR], left), _pair_diag(y[:, PAIR:], left)], axis=1), _NN)
          for xi, y in zip(a_rb, wu)]
    mn = [_dg(xi.astype(BF16), y.astype(BF16), _TN) for xi, y in zip(bt_, wu)]
    yield

    def head_blocks(z):
        return jnp.where(left, z[:HEAD, :], z[HEAD:, :])

    for u in range(len(ab_)):
        store(u,
              rb_[u] + ry[u][:, :PAIR],
              ry[u][:, PAIR:] + rkv[u],
              jnp.where(diag, pe_[u], 0.0) + head_blocks(mn[u][:, :PAIR]),
              head_blocks(mn[u][:, PAIR:]) + head_blocks(kv[u]))


def _rwkv_a_body(r_ref, k_ref, v_ref, l_ref, rh_ref, kh_ref, vh_ref, lh_ref,
                 mur_ref, muk_ref, muv_ref, mul_ref, w0_ref, a0_ref, kkw_ref, kaw_ref, rkw_ref,
                 w2_ref, a2_ref, g2_ref, ones_ref, tri_ref,
                 u_ref, gate_ref, halo_ref, cw_ref, cb_ref, wa_ref, wx_ref, ba_ref, bx_ref, lam_ref,
                 *rest):
    ncast = (len(rest) - 8) // 2
    cast_in = rest[:ncast]
    rp_ref, yp_ref, m_ref, n_ref, bonus_ref, g_ref, ya_ref = rest[ncast:ncast + 7]
    cast_out = rest[ncast + 7:-1]
    carry_ref = rest[-1]
    first = pl.program_id(1) == 0

    @pl.when(first)
    def _():
        carry_ref[...] = jnp.zeros_like(carry_ref)
    cl = CHUNK
    rows = CHUNKS_PER_STEP * cl
    width = HEADS_PER_STEP * HEAD
    gw = ones_ref.shape[0]
    row_g = lax.broadcasted_iota(jnp.int32, (rows, gw), 0)
    row_l = lax.broadcasted_iota(jnp.int32, (rows, l_ref.shape[1]), 0)
    ones_h = ones_ref[...]

    lo = _token_shift(l_ref[...], lh_ref[...], mul_ref[...], first, row_l)
    act_w = _split(jnp.tanh(lo[:, 0:LANE]))
    act_a = _split(lo[:, LANE:2 * LANE])
    act_g = _split(jax.nn.sigmoid(lo[:, 2 * LANE:]))

    def lora(act, w_ref, cs, keep_low):
        (ah, al_), wb = act, w_ref[:, cs]
        out = _dg(ah, wb, _NN)
        return out + _dg(al_, wb, _NN) if keep_low else out

    def prologue(c0, out):
        cs = slice(c0, c0 + gw)
        r = _token_shift(r_ref[:, cs], rh_ref[:, cs], mur_ref[:, cs], first, row_g)
        k = _token_shift(k_ref[:, cs], kh_ref[:, cs], muk_ref[:, cs], first, row_g)
        v = _token_shift(v_ref[:, cs], vh_ref[:, cs], muv_ref[:, cs], first, row_g)
        w_lin = w0_ref[:, cs] + lora(act_w, w2_ref, cs, True)
        a_lin = a0_ref[:, cs] + lora(act_a, a2_ref, cs, False)
        g_ref[:, cs] = lora(act_g, g2_ref, cs, False).astype(BF16)
        kk = k * kkw_ref[:, cs]
        kk_ss = _head_sums(kk * kk, ones_h)
        yield
        lw = DECAY_SCALE * jax.nn.sigmoid(w_lin)
        a = jax.nn.sigmoid(a_lin)
        kk = kk * lax.rsqrt(jnp.maximum(kk_ss, L2_EPS * L2_EPS))
        kp = k * (1.0 + (a - 1.0) * kaw_ref[:, cs])
        bonus_ref[:, cs] = (_head_sums(r * kp * rkw_ref[:, cs], ones_h) * v).astype(BF16)
        lc = _mm2_exact_lhs(tri_ref[...], lw)
        yield
        p_incl = jnp.exp(lc)
        p_excl = jnp.exp(lc - lw)
        p_inv = 1.0 / p_incl
        p_end = jnp.concatenate(
            [jnp.broadcast_to(p_incl[(j + 1) * cl - 1:(j + 1) * cl, :], (cl, gw))
             for j in range(CHUNKS_PER_STEP)], axis=0)
        abar = -(kk * p_excl)
        bbar = kk * a * p_inv
        kbar = kp * p_inv
        rbar = r * p_incl
        btil = bbar * p_end
        ktil = kbar * p_end
        units = [(j, q) for j in range(CHUNKS_PER_STEP) for q in range(gw // PAIR)]
        out.extend([x[j * cl:(j + 1) * cl, q * PAIR:(q + 1) * PAIR] for j, q in units]
                   for x in (abar, bbar, kbar, rbar, v, btil, ktil, p_end))
        yield

    def make_store(c0):
        units = [(j, q) for j in range(CHUNKS_PER_STEP) for q in range(gw // PAIR)]

        def store(u, rp, yp, mm, nn):
            j, q = units[u]
            rs = slice(j * cl, (j + 1) * cl)
            qs = slice(c0 + q * PAIR, c0 + (q + 1) * PAIR)
            rp_ref[rs, qs] = rp.astype(BF16)
            yp_ref[rs, qs] = yp.astype(BF16)
            m_ref[rs, qs] = mm.astype(BF16)
            n_ref[rs, qs] = nn.astype(BF16)
        return store

    lru = (None for h in range(LRU_HEADS)
           for _ in _lru_head(h, first, u_ref, gate_ref, halo_ref, cw_ref, cb_ref, wa_ref, wx_ref,
                              ba_ref, bx_ref, lam_ref, ya_ref, carry_ref))
    chains = []
    for c0 in range(0, width, gw):
        ops = []
        for _ in prologue(c0, ops):
            for ch in chains:
                next(ch, None)
        chains.append(_chunk_chain(ops, make_store(c0)))
    live = list(chains)
    while live:
        live = [ch for ch in live if next(ch, StopIteration) is not StopIteration]
        next(lru, None)
    for _ in lru:
        pass

    for src, dst in zip(cast_in, cast_out):
        dst[...] = src[...].astype(BF16)


def _rwkv_a(p, p_lora, mu_rkv, mu_lora, w0, a0, k_k, k_a, r_k, w2p, a2p, g2p,
            conv_w, conv_b, wa, wx, ba, bx, lam, bsz, seq, rkv_col0, cast_ws=()):
    dl = conv_w.shape[1]
    lvec = lambda t: t.reshape(1, dl)
    lrow = pl.BlockSpec((1, dl), lambda b, i, q: (0, 0))

    def lru_tile(cb):
        return pl.BlockSpec((CHUNKS_PER_STEP * CHUNK, dl), lambda b, i, q: (b * nc + i, cb))
    cl = CHUNKS_PER_STEP * CHUNK
    width = HEADS_PER_STEP * HEAD
    dr = w0.shape[1]
    ngroups = dr // width
    assert ngroups == 1, "the LRU ride-along expects one grid step per row tile"
    nc = seq // cl
    lw_ = mu_lora.shape[1]
    cb0 = rkv_col0 // width
    rows8 = cl // SUBLANE
    rt, ct = _iota2((cl, cl))
    tri = jnp.where(jnp.logical_and(rt >= ct, (rt // CHUNK) == (ct // CHUNK)), 1.0, 0.0).astype(BF16)
    ones_h = _head_ones(ONES_WIDTH)
    const = lambda arr: pl.BlockSpec(arr.shape, lambda b, i, q: (0, 0))
    lora_w = [wgt.astype(BF16) for wgt in (w2p, a2p, g2p)]

    def tile(cb_off):
        return pl.BlockSpec((cl, width), lambda b, i, q: (b * nc + i, cb0 + cb_off + q))

    def halo(cb_off):
        return pl.BlockSpec(
            (SUBLANE, width),
            lambda b, i, q: (jnp.maximum((b * nc + i) * rows8 - 1, 0), cb0 + cb_off + q))

    def prow(off=0):
        return pl.BlockSpec((1, width), lambda b, i, q: (0, off + q))

    out_tile = pl.BlockSpec((cl, width), lambda b, i, q: (b * nc + i, q))
    out_mat = pl.BlockSpec((CHUNKS_PER_STEP * HEAD, width), lambda b, i, q: (b * nc + i, q))
    act = jax.ShapeDtypeStruct((bsz * seq, dr), BF16)
    mat = jax.ShapeDtypeStruct((bsz * (seq // CHUNK) * HEAD, dr), BF16)

    nsteps = bsz * nc * ngroups
    cast_specs = []
    for wgt in cast_ws:
        hold = 1
        while (wgt.shape[0] * hold) % (nsteps * BF16_SUBLANE) != 0:
            hold *= 2
        blk = (wgt.shape[0] * hold // nsteps, wgt.shape[1])
        cast_specs.append(pl.BlockSpec(
            blk, lambda b, i, q, hold=hold: (((b * nc + i) * ngroups + q) // hold, 0)))
    cast_shapes = [jax.ShapeDtypeStruct(wgt.shape, BF16) for wgt in cast_ws]

    return pl.pallas_call(
        _rwkv_a_body,
        grid=(bsz, nc, ngroups),
        in_specs=[tile(0), tile(ngroups), tile(2 * ngroups),
                  pl.BlockSpec((cl, lw_), lambda b, i, q: (b * nc + i, 0)),
                  halo(0), halo(ngroups), halo(2 * ngroups),
                  pl.BlockSpec((SUBLANE, lw_),
                               lambda b, i, q: (jnp.maximum((b * nc + i) * rows8 - 1, 0), 0)),
                  prow(0), prow(ngroups), prow(2 * ngroups),
                  pl.BlockSpec((1, lw_), lambda b, i, q: (0, 0)),
                  prow(), prow(), prow(), prow(), prow()]
                 + [pl.BlockSpec((t.shape[0], width), lambda b, i, q: (0, q)) for t in lora_w]
                 + [const(ones_h), const(tri)]
                 + [lru_tile(0), lru_tile(1),
                    pl.BlockSpec((SUBLANE, dl),
                                 lambda b, i, q: (jnp.maximum((b * nc + i) * rows8 - 1, 0), 0)),
                    const(conv_w), lrow, pl.BlockSpec(wa.shape, lambda b, i, q: (0, 0, 0)),
                    pl.BlockSpec(wx.shape, lambda b, i, q: (0, 0, 0)), lrow, lrow, lrow]
                 + cast_specs,
        out_specs=[out_tile, out_tile, out_mat, out_mat, out_tile, out_tile, lru_tile(0)] + cast_specs,
        out_shape=[act, act, mat, mat, act, act, jax.ShapeDtypeStruct((bsz * seq, dl), BF16)]
                  + cast_shapes,
        scratch_shapes=[pltpu.VMEM((1, dl), F32)],
        compiler_params=_params("arbitrary", "arbitrary", "arbitrary"),
        name="rwkv_a",
    )(p, p, p, p_lora, p, p, p, p_lora, mu_rkv, mu_rkv, mu_rkv, mu_lora, w0, a0, k_k, k_a, r_k,
      *lora_w, ones_h, tri,
      p, p, p, conv_w, lvec(conv_b), wa, wx, lvec(ba), lvec(bx), lvec(lam), *cast_ws)


def _rwkv_b_body(rp_ref, yp_ref, m_ref, n_ref, bonus_ref, g_ref, lng_ref, lnb_ref, ones_ref,
                 o_ref, state_ref):
    @pl.when(pl.program_id(1) == 0)
    def _():
        state_ref[...] = jnp.zeros_like(state_ref)

    npairs = state_ref.shape[0]
    pairs = range(npairs)
    ps = [slice(q * PAIR, (q + 1) * PAIR) for q in pairs]
    left = lax.broadcasted_iota(jnp.int32, (HEAD, PAIR), 1) < HEAD
    ones_h = ones_ref[...]
    inv_n = 1.0 / HEAD
    state = [state_ref[q] for q in pairs]
    for j in range(rp_ref.shape[0] // CHUNK):
        rs = slice(j * CHUNK, (j + 1) * CHUNK)
        ks = slice(j * HEAD, (j + 1) * HEAD)
        g0 = [_pair_diag(state[q], left) for q in pairs]
        ys = [_dg(rp_ref[rs, ps[q]], g0[q], _NN) + yp_ref[rs, ps[q]] for q in pairs]
        state = [_dg(m_ref[ks, ps[q]], g0[q], _NN) + n_ref[ks, ps[q]] for q in pairs]
        y = jnp.concatenate(ys, axis=1)
        yc = y - _head_sums(y, ones_h) * inv_n
        var = _head_sums(yc * yc, ones_h) * inv_n
        yn = yc * lax.rsqrt(var + GN_EPS) * lng_ref[...] + lnb_ref[...]
        o_ref[rs, :] = ((yn + bonus_ref[rs, :]) * g_ref[rs, :]).astype(BF16)
    for q in pairs:
        state_ref[q] = state[q]


def _rwkv_b(rp, yp, mc, nm, bonus, g, ln_g, ln_b, bsz, seq):
    cl = RWKV_B_CHUNKS * CHUNK
    dr = rp.shape[1]
    nc = seq // cl
    tile = pl.BlockSpec((cl, dr), lambda b, i: (b * nc + i, 0))
    mat = pl.BlockSpec((RWKV_B_CHUNKS * HEAD, dr), lambda b, i: (b * nc + i, 0))
    prow = pl.BlockSpec((1, dr), lambda b, i: (0, 0))
    ones_h = _head_ones(ONES_WIDTH_B)
    return pl.pallas_call(
        _rwkv_b_body,
        grid=(bsz, nc),
        in_specs=[tile, tile, mat, mat, tile, tile, prow, prow,
                  pl.BlockSpec(ones_h.shape, lambda b, i: (0, 0))],
        out_specs=tile,
        out_shape=jax.ShapeDtypeStruct((bsz * seq, dr), BF16),
        scratch_shapes=[pltpu.VMEM((dr // PAIR, HEAD, PAIR), F32)],
        compiler_params=_params("parallel", "arbitrary"),
        name="rwkv_b",
    )(rp, yp, mc, nm, bonus, g, ln_g, ln_b, ones_h)


def _mm_out_body(ya_ref, yb_ref, x_ref, gm_ref, w_ref, g_ref, sh_ref, sc_ref, o_ref, h_ref, *, sub):
    da = ya_ref.shape[1]
    for r0 in range(0, x_ref.shape[0], sub):
        rs = slice(r0, r0 + sub)
        mix = (jnp.dot(ya_ref[rs, :], w_ref[:da, :], preferred_element_type=F32)
               + jnp.dot(yb_ref[rs, :], w_ref[da:, :], preferred_element_type=F32))
        x1 = x_ref[rs, :] + gm_ref[0] * mix
        o_ref[rs, :] = x1
        h_ref[rs, :] = _norm_mod(x1, g_ref[...], sh_ref[0], sc_ref[0]).astype(BF16)


def _mm_out(ya, yb, x2, gm, w, g, sh, sc, seq, tm=512, sub=256):
    m, d = x2.shape
    per_b = seq // tm
    brow = pl.BlockSpec((1, 1, d), lambda i: (i // per_b, 0, 0))
    tile = pl.BlockSpec((tm, d), lambda i: (i, 0))
    return pl.pallas_call(
        functools.partial(_mm_out_body, sub=sub),
        grid=(m // tm,),
        in_specs=[pl.BlockSpec((tm, ya.shape[1]), lambda i: (i, 0)),
                  pl.BlockSpec((tm, yb.shape[1]), lambda i: (i, 0)),
                  tile, brow,
                  pl.BlockSpec(w.shape, lambda i: (0, 0)),
                  pl.BlockSpec((1, d), lambda i: (0, 0)), brow, brow],
        out_specs=[tile, tile],
        out_shape=[jax.ShapeDtypeStruct((m, d), F32), jax.ShapeDtypeStruct((m, d), BF16)],
        compiler_params=_params("parallel"),
        name="mm_out",
    )(ya, yb, x2, gm, w, g, sh, sc)


def _ffn_body(x_ref, h_ref, gf_ref, wg_ref, wu_ref, wd_ref, fg_ref, o_ref, acc_ref, a_ref, *, sub):
    f = pl.program_id(1)
    last = pl.num_programs(1) - 1

    @pl.when(f == 0)
    def _():
        acc_ref[...] = jnp.zeros_like(acc_ref)

    h = h_ref[...]
    gate = jnp.dot(h, wg_ref[...], preferred_element_type=F32)
    up = jnp.dot(h, wu_ref[...], preferred_element_type=F32)
    a_ref[...] = (gate * jax.nn.sigmoid(gate) * up).astype(BF16)

    @pl.when(f < last)
    def _():
        acc_ref[...] += jnp.dot(a_ref[...], wd_ref[...], preferred_element_type=F32)

    @pl.when(f == last)
    def _():
        for r0 in range(0, x_ref.shape[0], sub):
            rs = slice(r0, r0 + sub)
            tot = acc_ref[rs, :] + jnp.dot(a_ref[rs, :], wd_ref[...], preferred_element_type=F32)
            y = x_ref[rs, :] + gf_ref[0] * tot
            o_ref[rs, :] = (y * lax.rsqrt(jnp.mean(y * y, axis=-1, keepdims=True) + RMS_EPS)
                            * fg_ref[...])


def _ffn(x1, h2, gf, w_gu, w_down, fg, seq, tm=512, tf=512, sub=256):
    m, d = x1.shape
    dff = w_down.shape[0]
    nf = dff // tf
    assert seq % tm == 0 and dff % tf == 0, "row tiles must not straddle sequences"
    per_b = seq // tm
    tile = pl.BlockSpec((tm, d), lambda i, f: (i, 0))
    prow = pl.BlockSpec((1, d), lambda i, f: (0, 0))
    return pl.pallas_call(
        functools.partial(_ffn_body, sub=sub),
        grid=(m // tm, nf),
        in_specs=[tile, tile,
                  pl.BlockSpec((1, 1, d), lambda i, f: (i // per_b, 0, 0)),
                  pl.BlockSpec((d, tf), lambda i, f: (0, f)),
                  pl.BlockSpec((d, tf), lambda i, f: (0, nf + f)),
                  pl.BlockSpec((tf, d), lambda i, f: (f, 0)),
                  prow],
        out_specs=tile,
        out_shape=jax.ShapeDtypeStruct((m, d), F32),
        scratch_shapes=[pltpu.VMEM((tm, d), F32), pltpu.VMEM((tm, tf), BF16)],
        compiler_params=_params("parallel", "arbitrary"),
        name="ffn",
    )(x1, h2, gf, w_gu, w_gu, w_down, fg)


def _pad_cols(w, n):
    return jnp.pad(w, ((0, 0), (0, n - w.shape[1])))


def _pad_rows(w, n):
    return jnp.pad(w, ((0, n - w.shape[0]), (0, 0)))


def kernel(x, c, w_ada, b_ada, norm_mix_g, w_in, conv_w, conv_b, lru_wa, lru_ba, lru_wx, lru_bx, lru_lambda, rwkv_mu, rwkv_w0, rwkv_w2, rwkv_a0, rwkv_a2, rwkv_g2, rwkv_k_k, rwkv_k_a, rwkv_r_k, rwkv_ln_g, rwkv_ln_b, w_out, norm_ffn_g, w_gu, w_down, final_norm_g):
    bsz, seq, d = x.shape
    depth = w_ada.shape[0]
    assert depth == 1, "the closing RMSNorm is fused into the (single) layer's ffn kernel"
    dl = conv_w.shape[2]
    dr = rwkv_w0.shape[1]
    w_lora, a_lora, g_lora = rwkv_w2.shape[1], rwkv_a2.shape[1], rwkv_g2.shape[1]
    wpad, apad = LANE, LANE
    gpad = -(-g_lora // LANE) * LANE
    rkv_col0 = 2 * dl
    lora0 = rkv_col0 + 3 * dr

    x2 = x.reshape(bsz * seq, d)
    for l in range(depth):
        mod = _mod(c, w_ada[l], b_ada[l].reshape(1, -1))
        sh_m, sc_m, g_m, sh_f, sc_f, g_f = [t.reshape(bsz, 1, d) for t in jnp.split(mod, 6, axis=-1)]

        wi = jnp.swapaxes(w_in[l], 0, 1)
        o1, o2 = lora0 + w_lora, lora0 + w_lora + a_lora
        w_lora_p = jnp.concatenate(
            [_pad_rows(wi[lora0:o1], wpad), _pad_rows(wi[o1:o2], apad),
             _pad_rows(wi[o2:], gpad)], axis=0)
        mu = rwkv_mu[l].reshape(1, -1)
        mu_rkv = mu[:, :3 * dr]
        mu_lora = jnp.concatenate(
            [_pad_cols(mu[:, 3 * dr:3 * dr + w_lora], wpad),
             _pad_cols(mu[:, 3 * dr + w_lora:3 * dr + w_lora + a_lora], apad),
             _pad_cols(mu[:, 3 * dr + w_lora + a_lora:], gpad)], axis=1)
        w2p = _pad_rows(rwkv_w2[l], wpad)
        a2p = _pad_rows(rwkv_a2[l], apad)
        g2p = _pad_rows(rwkv_g2[l], gpad)

        h, p_lora = _norm(x2, norm_mix_g[l].reshape(1, d), sh_m, sc_m, w_lora_p, seq)
        p = _mm_in(h, wi, lora0, gelu_tile=1, tn=dl)

        rowv = lambda t: t.reshape(1, dr)
        rp, yp, mc, nm, bonus, gg, y_a, w_out_b, w_gu_b, w_down_b = _rwkv_a(
            p, p_lora, mu_rkv, mu_lora, rowv(rwkv_w0[l]), rowv(rwkv_a0[l]), rowv(rwkv_k_k[l]),
            rowv(rwkv_k_a[l]), rowv(rwkv_r_k[l]), w2p, a2p, g2p,
            conv_w[l], conv_b[l], lru_wa[l].astype(BF16), lru_wx[l].astype(BF16),
            lru_ba[l], lru_bx[l], lru_lambda[l], bsz, seq, rkv_col0,
            cast_ws=(w_out[l], w_gu[l], w_down[l]))
        y_b = _rwkv_b(rp, yp, mc, nm, bonus, gg, rowv(rwkv_ln_g[l]), rowv(rwkv_ln_b[l]), bsz, seq)

        x2, h2 = _mm_out(y_a, y_b, x2, g_m, w_out_b, norm_ffn_g[l].reshape(1, d), sh_f, sc_f, seq)
        x2 = _ffn(x2, h2, g_f, w_gu_b, w_down_b, final_norm_g.reshape(1, d), seq)
    return x2.reshape(bsz, seq, d)
```

```python
import functools
import math

import jax
import jax.numpy as jnp
from jax import lax
from jax.experimental import pallas as pl
from jax.experimental.pallas import tpu as pltpu

F32 = jnp.float32
BF16 = jnp.bfloat16

LRU_HEADS = 4
CONV_WIDTH = 4
LRU_C = 8.0
HEAD = 64
CHUNK = 64
PAIR = 2 * HEAD
HEADS_PER_STEP = 16
ONES_WIDTH = 256
ONES_WIDTH_B = 128
CHUNKS_PER_STEP = 4
RWKV_B_CHUNKS = 4
MOD_DMA_BANDS = 4
RMS_EPS = 1e-6
GN_EPS = 64e-5
L2_EPS = 1e-12
DECAY_SCALE = -math.exp(-0.5)
LANE = 128
SUBLANE = 8
BF16_SUBLANE = 16
VMEM_LIMIT = 56 * 1024 * 1024


def _params(*sem):
    return pltpu.CompilerParams(dimension_semantics=sem, vmem_limit_bytes=VMEM_LIMIT)


_NN = (((1,), (0,)), ((), ()))
_NT = (((1,), (1,)), ((), ()))
_TN = (((0,), (0,)), ((), ()))


def _dg(a, b, dims):
    return lax.dot_general(a, b, dims, preferred_element_type=F32)


def _split(x):
    hi = x.astype(BF16)
    lo = (x - hi.astype(F32)).astype(BF16)
    return hi, lo


def _mm3(a, b, dims=_NN):
    ah, al = _split(a)
    bh, bl = _split(b)
    return _dg(ah, bh, dims) + (_dg(ah, bl, dims) + _dg(al, bh, dims))


def _head_sums(x, ones_h):
    n = ones_h.shape[0]
    xb = x.astype(BF16)
    return jnp.concatenate([_dg(xb[:, c:c + n], ones_h, _NN) for c in range(0, x.shape[1], n)],
                           axis=1)


def _mm2_exact_lhs(a_bf16, b):
    bh, bl = _split(b)
    return _dg(a_bf16, bh, _NN) + _dg(a_bf16, bl, _NN)


def _softplus(x):
    return jnp.maximum(x, 0.0) + jnp.log1p(jnp.exp(-jnp.abs(x)))


def _iota2(shape):
    return (lax.broadcasted_iota(jnp.int32, shape, 0),
            lax.broadcasted_iota(jnp.int32, shape, 1))


def _head_ones(n):
    r, c = _iota2((n, n))
    return jnp.where((r // HEAD) == (c // HEAD), 1.0, 0.0).astype(BF16)


def _mod_body(c_ref, *refs):
    w_refs, b_ref, o_ref = refs[:-2], refs[-2], refs[-1]
    c = c_ref[...]
    ca = c * jax.nn.sigmoid(c)
    kb = w_refs[0].shape[0]
    acc = b_ref[...]
    for s, w_ref in enumerate(w_refs):
        acc = acc + _mm3(ca[:, s * kb:(s + 1) * kb], w_ref[...])
    o_ref[...] = acc


def _mod(c, w, b, tn=1024, bands=MOD_DMA_BANDS):
    bsz, d = c.shape
    n = w.shape[1]
    kb = d // bands
    return pl.pallas_call(
        _mod_body,
        grid=(n // tn,),
        in_specs=[pl.BlockSpec((bsz, d), lambda j: (0, 0))]
                 + [pl.BlockSpec((kb, tn), lambda j, s=s: (s, j)) for s in range(bands)]
                 + [pl.BlockSpec((1, tn), lambda j: (0, j))],
        out_specs=pl.BlockSpec((bsz, tn), lambda j: (0, j)),
        out_shape=jax.ShapeDtypeStruct((bsz, n), F32),
        compiler_params=_params("parallel"),
        name="mod",
    )(c, *([w] * bands), b)


def _norm_mod(x, g, sh, sc):
    y = x * lax.rsqrt(jnp.mean(x * x, axis=-1, keepdims=True) + RMS_EPS) * g
    return y * (1.0 + sc) + sh


def _norm_body(x_ref, g_ref, sh_ref, sc_ref, w_ref, o_ref, pl_ref, wb_ref):
    @pl.when(pl.program_id(0) == 0)
    def _():
        wb_ref[...] = w_ref[...].astype(BF16)

    h = _norm_mod(x_ref[...], g_ref[...], sh_ref[0], sc_ref[0]).astype(BF16)
    o_ref[...] = h
    pl_ref[...] = _dg(h, wb_ref[...], _NT)


def _norm(x2, g, sh, sc, w_lora_t, seq, tm=512):
    m, d = x2.shape
    nl = w_lora_t.shape[0]
    per_b = seq // tm
    return pl.pallas_call(
        _norm_body,
        grid=(m // tm,),
        in_specs=[pl.BlockSpec((tm, d), lambda i: (i, 0)),
                  pl.BlockSpec((1, d), lambda i: (0, 0)),
                  pl.BlockSpec((1, 1, d), lambda i: (i // per_b, 0, 0)),
                  pl.BlockSpec((1, 1, d), lambda i: (i // per_b, 0, 0)),
                  pl.BlockSpec((nl, d), lambda i: (0, 0))],
        out_specs=[pl.BlockSpec((tm, d), lambda i: (i, 0)), pl.BlockSpec((tm, nl), lambda i: (i, 0))],
        out_shape=[jax.ShapeDtypeStruct((m, d), BF16), jax.ShapeDtypeStruct((m, nl), F32)],
        scratch_shapes=[pltpu.VMEM((nl, d), BF16)],
        compiler_params=_params("arbitrary"),
        name="norm_mix",
    )(x2, g, sh, sc, w_lora_t)


def _mm_in_body(h_ref, w_ref, o_ref, wb_ref, *, gelu_tile):
    @pl.when(pl.program_id(1) == 0)
    def _():
        wb_ref[...] = w_ref[...].astype(BF16)

    @pl.when(pl.program_id(0) == gelu_tile)
    def _():
        o_ref[...] = jax.nn.gelu(_dg(h_ref[...], wb_ref[...], _NT))

    @pl.when(pl.program_id(0) != gelu_tile)
    def _():
        o_ref[...] = _dg(h_ref[...], wb_ref[...], _NT)


def _mm_in(h, wt, ncols, gelu_tile, tm=1024, tn=1024):
    m, d = h.shape
    return pl.pallas_call(
        functools.partial(_mm_in_body, gelu_tile=gelu_tile),
        grid=(ncols // tn, m // tm),
        in_specs=[pl.BlockSpec((tm, d), lambda j, i: (i, 0)),
                  pl.BlockSpec((tn, d), lambda j, i: (j, 0))],
        out_specs=pl.BlockSpec((tm, tn), lambda j, i: (i, j)),
        out_shape=jax.ShapeDtypeStruct((m, ncols), F32),
        scratch_shapes=[pltpu.VMEM((tn, d), BF16)],
        compiler_params=_params("parallel", "arbitrary"),
        name="mm_in",
    )(h, wt)


def _lru_head(h, first, u_ref, gate_ref, halo_ref, cw_ref, cb_ref, wa_ref, wx_ref, ba_ref, bx_ref,
              lam_ref, o_ref, carry_ref):
    tt = u_ref.shape[0]
    hd = u_ref.shape[1] // LRU_HEADS
    cs = slice(h * hd, (h + 1) * hd)
    p = u_ref[:, cs]
    halo = jnp.where(first, 0.0, halo_ref[:, cs])
    ext = jnp.concatenate([halo, p], axis=0)
    cw = cw_ref[:, cs]
    u = cb_ref[:, cs] + p * cw[CONV_WIDTH - 1:CONV_WIDTH, :]
    for j in range(1, CONV_WIDTH):
        shifted = pltpu.roll(ext, j, 0)[SUBLANE:, :]
        u = u + shifted * cw[CONV_WIDTH - 1 - j:CONV_WIDTH - j, :]
    ub = u.astype(BF16)
    ra = jnp.dot(ub, wa_ref[h], preferred_element_type=F32)
    rx = jnp.dot(ub, wx_ref[h], preferred_element_type=F32)
    yield
    r = jax.nn.sigmoid(ra + ba_ref[:, cs])
    ig = jax.nn.sigmoid(rx + bx_ref[:, cs])
    a = jnp.exp(r * ((-LRU_C) * _softplus(-lam_ref[:, cs])))
    mult = jnp.sqrt(1.0 - a * a)
    row = lax.broadcasted_iota(jnp.int32, (tt, hd), 0)
    mult = jnp.where(jnp.logical_and(first, row == 0), 1.0, mult)
    b = mult * (ig * u)

    groups = tt // SUBLANE
    a3 = a.reshape(groups, SUBLANE, hd)
    b3 = b.reshape(groups, SUBLANE, hd)
    sub = lax.broadcasted_iota(jnp.int32, (groups, SUBLANE, hd), 1)
    s = 1
    while s < SUBLANE:
        keep = sub >= s
        a_s = jnp.where(keep, pltpu.roll(a3, s, 1), 1.0)
        b_s = jnp.where(keep, pltpu.roll(b3, s, 1), 0.0)
        b3 = a3 * b_s + b3
        a3 = a3 * a_s
        s *= 2
    yield
    gate = gate_ref[:, cs]
    carry = carry_ref[:, cs]
    outs = []
    for g in range(groups):
        hh = b3[g] + a3[g] * carry
        carry = hh[SUBLANE - 1:SUBLANE, :]
        outs.append(hh * gate[g * SUBLANE:(g + 1) * SUBLANE, :])
    per = BF16_SUBLANE // SUBLANE
    for t0 in range(0, groups, per):
        o_ref[t0 * SUBLANE:(t0 + per) * SUBLANE, cs] = jnp.concatenate(
            outs[t0:t0 + per], axis=0).astype(BF16)
    carry_ref[:, cs] = carry
    yield


def _token_shift(x, halo, mu, first, row):
    prev = jnp.where(first, 0.0, halo[SUBLANE - 1:SUBLANE, :])
    xs = jnp.where(row == 0, prev, pltpu.roll(x, 1, 0))
    return x + (xs - x) * mu


def _pair_diag(y, left):
    return jnp.concatenate([jnp.where(left, y, 0.0), jnp.where(left, 0.0, y)], axis=0).astype(BF16)


def _pair_mm(x, y, left):
    return _dg(x.astype(BF16), _pair_diag(y, left), _NN)


def _chunk_chain(ops, store):
    ab_, bb_, kb_, rb_, v_, bt_, kt_, pe_ = ops
    rc, lane = _iota2((CHUNK, PAIR))
    cc = lane % HEAD
    left = lane < HEAD
    strict = rc > cc
    incl = rc >= cc
    diag = rc == cc
    ar16 = [jnp.concatenate([x, y], axis=0).astype(BF16) for x, y in zip(ab_, rb_)]
    bd_b = [_pair_diag(x, left) for x in bb_]
    bd_k = [_pair_diag(x, left) for x in kb_]
    bd_v = [_pair_diag(x, left) for x in v_]
    arb = [_dg(x, y, _NT) for x, y in zip(ar16, bd_b)]
    ark = [_dg(x, y, _NT) for x, y in zip(ar16, bd_k)]
    a_ab = [jnp.where(strict, x[:CHUNK], 0.0) for x in arb]
    a_rb = [jnp.where(incl, x[CHUNK:], 0.0).astype(BF16) for x in arb]
    a_akrk = [jnp.concatenate([jnp.where(strict, x[:CHUNK], 0.0), jnp.where(incl, x[CHUNK:], 0.0)],
                              axis=0).astype(BF16) for x in ark]
    yield
    base = 8
    d = [jnp.where((rc // base) == (cc // base), a, 0.0) for a in a_ab]
    d2 = [_pair_mm(t, t, left) for t in d]
    akrkv = [_dg(x, y, _NN) for x, y in zip(a_akrk, bd_v)]
    akv = [x[:CHUNK] for x in akrkv]
    rkv = [x[CHUNK:] for x in akrkv]
    x = [jnp.where(diag, 1.0, 0.0) + t for t in d]
    yield
    x = [xi + _pair_mm(t2, xi, left) for xi, t2 in zip(x, d2)]
    d4 = [_pair_mm(t2, t2, left) for t2 in d2]
    yield
    x = [xi + _pair_mm(t4, xi, left) for xi, t4 in zip(x, d4)]
    yield
    size = base
    while size < CHUNK:
        off = jnp.logical_and((rc // (2 * size)) == (cc // (2 * size)),
                              (rc // size) != (cc // size))
        o = [jnp.where(off, a, 0.0) for a in a_ab]
        ox = [_pair_mm(oi, xi, left) for oi, xi in zip(o, x)]
        yield
        x = [xi + _pair_mm(xi, oxi, left) for xi, oxi in zip(x, ox)]
        yield
        size *= 2
    t = [xi.astype(BF16) for xi in x]
    wu = [_dg(ti, jnp.concatenate([_pair_diag(y, left), _pair_diag(z, left)], axis=1), _NN)
          for ti, y, z in zip(t, ab_, akv)]
    kv = [_dg(xi.astype(BF16), y.astype(BF16), _TN) for xi, y in zip(kt_, v_)]
    yield
    ry = [_dg(xi, jnp.concatenate([_pair_diag(y[:, :PAIR], left), _pair_diag(y[:, PAIR:], left)], axis=1), _NN)
          for xi, y in zip(a_rb, wu)]
    mn = [_dg(xi.astype(BF16), y.astype(BF16), _TN) for xi, y in zip(bt_, wu)]
    yield

    def head_blocks(z):
        return jnp.where(left, z[:HEAD, :], z[HEAD:, :])

    for u in range(len(ab_)):
        store(u,
              rb_[u] + ry[u][:, :PAIR],
              ry[u][:, PAIR:] + rkv[u],
              jnp.where(diag, pe_[u], 0.0) + head_blocks(mn[u][:, :PAIR]),
              head_blocks(mn[u][:, PAIR:]) + head_blocks(kv[u]))


def _rwkv_a_body(r_ref, k_ref, v_ref, l_ref, rh_ref, kh_ref, vh_ref, lh_ref,
                 mur_ref, muk_ref, muv_ref, mul_ref, w0_ref, a0_ref, kkw_ref, kaw_ref, rkw_ref,
                 w2_ref, a2_ref, g2_ref, ones_ref, tri_ref,
                 u_ref, gate_ref, halo_ref, cw_ref, cb_ref, wa_ref, wx_ref, ba_ref, bx_ref, lam_ref,
                 *rest):
    ncast = (len(rest) - 8) // 2
    cast_in = rest[:ncast]
    rp_ref, yp_ref, m_ref, n_ref, bonus_ref, g_ref, ya_ref = rest[ncast:ncast + 7]
    cast_out = rest[ncast + 7:-1]
    carry_ref = rest[-1]
    first = pl.program_id(1) == 0

    @pl.when(first)
    def _():
        carry_ref[...] = jnp.zeros_like(carry_ref)
    cl = CHUNK
    rows = CHUNKS_PER_STEP * cl
    width = HEADS_PER_STEP * HEAD
    gw = ones_ref.shape[0]
    row_g = lax.broadcasted_iota(jnp.int32, (rows, gw), 0)
    row_l = lax.broadcasted_iota(jnp.int32, (rows, l_ref.shape[1]), 0)
    ones_h = ones_ref[...]

    lo = _token_shift(l_ref[...], lh_ref[...], mul_ref[...], first, row_l)
    act_w = _split(jnp.tanh(lo[:, 0:LANE]))
    act_a = _split(lo[:, LANE:2 * LANE])
    act_g = _split(jax.nn.sigmoid(lo[:, 2 * LANE:]))

    def lora(act, w_ref, cs, keep_low):
        (ah, al_), wb = act, w_ref[:, cs]
        out = _dg(ah, wb, _NN)
        return out + _dg(al_, wb, _NN) if keep_low else out

    def prologue(c0, out):
        cs = slice(c0, c0 + gw)
        r = _token_shift(r_ref[:, cs], rh_ref[:, cs], mur_ref[:, cs], first, row_g)
        k = _token_shift(k_ref[:, cs], kh_ref[:, cs], muk_ref[:, cs], first, row_g)
        v = _token_shift(v_ref[:, cs], vh_ref[:, cs], muv_ref[:, cs], first, row_g)
        w_lin = w0_ref[:, cs] + lora(act_w, w2_ref, cs, True)
        a_lin = a0_ref[:, cs] + lora(act_a, a2_ref, cs, False)
        g_ref[:, cs] = lora(act_g, g2_ref, cs, False).astype(BF16)
        kk = k * kkw_ref[:, cs]
        kk_ss = _head_sums(kk * kk, ones_h)
        yield
        lw = DECAY_SCALE * jax.nn.sigmoid(w_lin)
        a = jax.nn.sigmoid(a_lin)
        kk = kk * lax.rsqrt(jnp.maximum(kk_ss, L2_EPS * L2_EPS))
        kp = k * (1.0 + (a - 1.0) * kaw_ref[:, cs])
        bonus_ref[:, cs] = (_head_sums(r * kp * rkw_ref[:, cs], ones_h) * v).astype(BF16)
        lc = _mm2_exact_lhs(tri_ref[...], lw)
        yield
        p_incl = jnp.exp(lc)
        p_excl = jnp.exp(lc - lw)
        p_inv = 1.0 / p_incl
        p_end = jnp.concatenate(
            [jnp.broadcast_to(p_incl[(j + 1) * cl - 1:(j + 1) * cl, :], (cl, gw))
             for j in range(CHUNKS_PER_STEP)], axis=0)
        abar = -(kk * p_excl)
        bbar = kk * a * p_inv
        kbar = kp * p_inv
        rbar = r * p_incl
        btil = bbar * p_end
        ktil = kbar * p_end
        units = [(j, q) for j in range(CHUNKS_PER_STEP) for q in range(gw // PAIR)]
        out.extend([x[j * cl:(j + 1) * cl, q * PAIR:(q + 1) * PAIR] for j, q in units]
                   for x in (abar, bbar, kbar, rbar, v, btil, ktil, p_end))
        yield

    def make_store(c0):
        units = [(j, q) for j in range(CHUNKS_PER_STEP) for q in range(gw // PAIR)]

        def store(u, rp, yp, mm, nn):
            j, q = units[u]
            rs = slice(j * cl, (j + 1) * cl)
            qs = slice(c0 + q * PAIR, c0 + (q + 1) * PAIR)
            rp_ref[rs, qs] = rp.astype(BF16)
            yp_ref[rs, qs] = yp.astype(BF16)
            m_ref[rs, qs] = mm.astype(BF16)
            n_ref[rs, qs] = nn.astype(BF16)
        return store

    lru = (None for h in range(LRU_HEADS)
           for _ in _lru_head(h, first, u_ref, gate_ref, halo_ref, cw_ref, cb_ref, wa_ref, wx_ref,
                              ba_ref, bx_ref, lam_ref, ya_ref, carry_ref))
    chains = []
    for c0 in range(0, width, gw):
        ops = []
        for _ in prologue(c0, ops):
            for ch in chains:
                next(ch, None)
        chains.append(_chunk_chain(ops, make_store(c0)))
    live = list(chains)
    while live:
        live = [ch for ch in live if next(ch, StopIteration) is not StopIteration]
        next(lru, None)
    for _ in lru:
        pass

    for src, dst in zip(cast_in, cast_out):
        dst[...] = src[...].astype(BF16)


def _rwkv_a(p, p_lora, mu_rkv, mu_lora, w0, a0, k_k, k_a, r_k, w2p, a2p, g2p,
            conv_w, conv_b, wa, wx, ba, bx, lam, bsz, seq, rkv_col0, cast_ws=()):
    dl = conv_w.shape[1]
    lvec = lambda t: t.reshape(1, dl)
    lrow = pl.BlockSpec((1, dl), lambda b, i, q: (0, 0))

    def lru_tile(cb):
        return pl.BlockSpec((CHUNKS_PER_STEP * CHUNK, dl), lambda b, i, q: (b * nc + i, cb))
    cl = CHUNKS_PER_STEP * CHUNK
    width = HEADS_PER_STEP * HEAD
    dr = w0.shape[1]
    ngroups = dr // width
    assert ngroups == 1, "the LRU ride-along expects one grid step per row tile"
    nc = seq // cl
    lw_ = mu_lora.shape[1]
    cb0 = rkv_col0 // width
    rows8 = cl // SUBLANE
    rt, ct = _iota2((cl, cl))
    tri = jnp.where(jnp.logical_and(rt >= ct, (rt // CHUNK) == (ct // CHUNK)), 1.0, 0.0).astype(BF16)
    ones_h = _head_ones(ONES_WIDTH)
    const = lambda arr: pl.BlockSpec(arr.shape, lambda b, i, q: (0, 0))
    lora_w = [wgt.astype(BF16) for wgt in (w2p, a2p, g2p)]

    def tile(cb_off):
        return pl.BlockSpec((cl, width), lambda b, i, q: (b * nc + i, cb0 + cb_off + q))

    def halo(cb_off):
        return pl.BlockSpec(
            (SUBLANE, width),
            lambda b, i, q: (jnp.maximum((b * nc + i) * rows8 - 1, 0), cb0 + cb_off + q))

    def prow(off=0):
        return pl.BlockSpec((1, width), lambda b, i, q: (0, off + q))

    out_tile = pl.BlockSpec((cl, width), lambda b, i, q: (b * nc + i, q))
    out_mat = pl.BlockSpec((CHUNKS_PER_STEP * HEAD, width), lambda b, i, q: (b * nc + i, q))
    act = jax.ShapeDtypeStruct((bsz * seq, dr), BF16)
    mat = jax.ShapeDtypeStruct((bsz * (seq // CHUNK) * HEAD, dr), BF16)

    nsteps = bsz * nc * ngroups
    cast_specs = []
    for wgt in cast_ws:
        hold = 1
        while (wgt.shape[0] * hold) % (nsteps * BF16_SUBLANE) != 0:
            hold *= 2
        blk = (wgt.shape[0] * hold // nsteps, wgt.shape[1])
        cast_specs.append(pl.BlockSpec(
            blk, lambda b, i, q, hold=hold: (((b * nc + i) * ngroups + q) // hold, 0)))
    cast_shapes = [jax.ShapeDtypeStruct(wgt.shape, BF16) for wgt in cast_ws]

    return pl.pallas_call(
        _rwkv_a_body,
        grid=(bsz, nc, ngroups),
        in_specs=[tile(0), tile(ngroups), tile(2 * ngroups),
                  pl.BlockSpec((cl, lw_), lambda b, i, q: (b * nc + i, 0)),
                  halo(0), halo(ngroups), halo(2 * ngroups),
                  pl.BlockSpec((SUBLANE, lw_),
                               lambda b, i, q: (jnp.maximum((b * nc + i) * rows8 - 1, 0), 0)),
                  prow(0), prow(ngroups), prow(2 * ngroups),
                  pl.BlockSpec((1, lw_), lambda b, i, q: (0, 0)),
                  prow(), prow(), prow(), prow(), prow()]
                 + [pl.BlockSpec((t.shape[0], width), lambda b, i, q: (0, q)) for t in lora_w]
                 + [const(ones_h), const(tri)]
                 + [lru_tile(0), lru_tile(1),
                    pl.BlockSpec((SUBLANE, dl),
                                 lambda b, i, q: (jnp.maximum((b * nc + i) * rows8 - 1, 0), 0)),
                    const(conv_w), lrow, pl.BlockSpec(wa.shape, lambda b, i, q: (0, 0, 0)),
                    pl.BlockSpec(wx.shape, lambda b, i, q: (0, 0, 0)), lrow, lrow, lrow]
                 + cast_specs,
        out_specs=[out_tile, out_tile, out_mat, out_mat, out_tile, out_tile, lru_tile(0)] + cast_specs,
        out_shape=[act, act, mat, mat, act, act, jax.ShapeDtypeStruct((bsz * seq, dl), BF16)]
                  + cast_shapes,
        scratch_shapes=[pltpu.VMEM((1, dl), F32)],
        compiler_params=_params("arbitrary", "arbitrary", "arbitrary"),
        name="rwkv_a",
    )(p, p, p, p_lora, p, p, p, p_lora, mu_rkv, mu_rkv, mu_rkv, mu_lora, w0, a0, k_k, k_a, r_k,
      *lora_w, ones_h, tri,
      p, p, p, conv_w, lvec(conv_b), wa, wx, lvec(ba), lvec(bx), lvec(lam), *cast_ws)


def _rwkv_b_body(rp_ref, yp_ref, m_ref, n_ref, bonus_ref, g_ref, lng_ref, lnb_ref, ones_ref,
                 o_ref, state_ref):
    @pl.when(pl.program_id(0) == 0)
    def _():
        state_ref[...] = jnp.zeros_like(state_ref)

    nb = rp_ref.shape[0]
    npairs = state_ref.shape[0] // nb
    units = [(b, q) for b in range(nb) for q in range(npairs)]
    ps = [slice(q * PAIR, (q + 1) * PAIR) for q in range(npairs)]
    left = lax.broadcasted_iota(jnp.int32, (HEAD, PAIR), 1) < HEAD
    ones_h = ones_ref[...]
    inv_n = 1.0 / HEAD
    state = [state_ref[u] for u in range(len(units))]
    for j in range(rp_ref.shape[1] // CHUNK):
        rs = slice(j * CHUNK, (j + 1) * CHUNK)
        ks = slice(j * HEAD, (j + 1) * HEAD)
        g0 = [_pair_diag(s, left) for s in state]
        ys = [_dg(rp_ref[b, rs, ps[q]], g0[u], _NN) + yp_ref[b, rs, ps[q]]
              for u, (b, q) in enumerate(units)]
        state = [_dg(m_ref[b, ks, ps[q]], g0[u], _NN) + n_ref[b, ks, ps[q]]
                 for u, (b, q) in enumerate(units)]
        y = [jnp.concatenate(ys[b * npairs:(b + 1) * npairs], axis=1) for b in range(nb)]
        yc = [t - _head_sums(t, ones_h) * inv_n for t in y]
        var = [_head_sums(t * t, ones_h) * inv_n for t in yc]
        for b in range(nb):
            yn = yc[b] * lax.rsqrt(var[b] + GN_EPS) * lng_ref[...] + lnb_ref[...]
            o_ref[b, rs, :] = ((yn + bonus_ref[b, rs, :]) * g_ref[b, rs, :]).astype(BF16)
    for u in range(len(units)):
        state_ref[u] = state[u]


def _rwkv_b(rp, yp, mc, nm, bonus, g, ln_g, ln_b, bsz, seq):
    cl = RWKV_B_CHUNKS * CHUNK
    dr = rp.shape[1]
    nc = seq // cl
    by_seq = lambda t: t.reshape(bsz, -1, dr)
    tile = pl.BlockSpec((bsz, cl, dr), lambda i: (0, i, 0))
    mat = pl.BlockSpec((bsz, RWKV_B_CHUNKS * HEAD, dr), lambda i: (0, i, 0))
    prow = pl.BlockSpec((1, dr), lambda i: (0, 0))
    ones_h = _head_ones(ONES_WIDTH_B)
    out = pl.pallas_call(
        _rwkv_b_body,
        grid=(nc,),
        in_specs=[tile, tile, mat, mat, tile, tile, prow, prow,
                  pl.BlockSpec(ones_h.shape, lambda i: (0, 0))],
        out_specs=tile,
        out_shape=jax.ShapeDtypeStruct((bsz, seq, dr), BF16),
        scratch_shapes=[pltpu.VMEM((bsz * (dr // PAIR), HEAD, PAIR), F32)],
        compiler_params=_params("arbitrary"),
        name="rwkv_b",
    )(by_seq(rp), by_seq(yp), by_seq(mc), by_seq(nm), by_seq(bonus), by_seq(g), ln_g, ln_b, ones_h)
    return out.reshape(bsz * seq, dr)


def _mm_out_body(ya_ref, yb_ref, x_ref, gm_ref, w_ref, g_ref, sh_ref, sc_ref, o_ref, h_ref, *, sub):
    da = ya_ref.shape[1]
    for r0 in range(0, x_ref.shape[0], sub):
        rs = slice(r0, r0 + sub)
        mix = (jnp.dot(ya_ref[rs, :], w_ref[:da, :], preferred_element_type=F32)
               + jnp.dot(yb_ref[rs, :], w_ref[da:, :], preferred_element_type=F32))
        x1 = x_ref[rs, :] + gm_ref[0] * mix
        o_ref[rs, :] = x1
        h_ref[rs, :] = _norm_mod(x1, g_ref[...], sh_ref[0], sc_ref[0]).astype(BF16)


def _mm_out(ya, yb, x2, gm, w, g, sh, sc, seq, tm=512, sub=256):
    m, d = x2.shape
    per_b = seq // tm
    brow = pl.BlockSpec((1, 1, d), lambda i: (i // per_b, 0, 0))
    tile = pl.BlockSpec((tm, d), lambda i: (i, 0))
    return pl.pallas_call(
        functools.partial(_mm_out_body, sub=sub),
        grid=(m // tm,),
        in_specs=[pl.BlockSpec((tm, ya.shape[1]), lambda i: (i, 0)),
                  pl.BlockSpec((tm, yb.shape[1]), lambda i: (i, 0)),
                  tile, brow,
                  pl.BlockSpec(w.shape, lambda i: (0, 0)),
                  pl.BlockSpec((1, d), lambda i: (0, 0)), brow, brow],
        out_specs=[tile, tile],
        out_shape=[jax.ShapeDtypeStruct((m, d), F32), jax.ShapeDtypeStruct((m, d), BF16)],
        compiler_params=_params("parallel"),
        name="mm_out",
    )(ya, yb, x2, gm, w, g, sh, sc)


def _ffn_body(x_ref, h_ref, gf_ref, wg_ref, wu_ref, wd_ref, fg_ref, o_ref, acc_ref):
    f = pl.program_id(1)

    @pl.when(f == 0)
    def _():
        acc_ref[...] = jnp.zeros_like(acc_ref)

    h = h_ref[...]
    gate = jnp.dot(h, wg_ref[...], preferred_element_type=F32)
    up = jnp.dot(h, wu_ref[...], preferred_element_type=F32)
    act = (gate * jax.nn.sigmoid(gate) * up).astype(BF16)
    acc_ref[...] += jnp.dot(act, wd_ref[...], preferred_element_type=F32)

    @pl.when(f == pl.num_programs(1) - 1)
    def _():
        y = x_ref[...] + gf_ref[0] * acc_ref[...]
        o_ref[...] = (y * lax.rsqrt(jnp.mean(y * y, axis=-1, keepdims=True) + RMS_EPS)
                      * fg_ref[...])


def _ffn(x1, h2, gf, w_gu, w_down, fg, seq, tm=512, tf=512):
    m, d = x1.shape
    dff = w_down.shape[0]
    nf = dff // tf
    assert seq % tm == 0 and dff % tf == 0, "row tiles must not straddle sequences"
    per_b = seq // tm
    tile = pl.BlockSpec((tm, d), lambda i, f: (i, 0))
    prow = pl.BlockSpec((1, d), lambda i, f: (0, 0))
    return pl.pallas_call(
        _ffn_body,
        grid=(m // tm, nf),
        in_specs=[tile, tile,
                  pl.BlockSpec((1, 1, d), lambda i, f: (i // per_b, 0, 0)),
                  pl.BlockSpec((d, tf), lambda i, f: (0, f)),
                  pl.BlockSpec((d, tf), lambda i, f: (0, nf + f)),
                  pl.BlockSpec((tf, d), lambda i, f: (f, 0)),
                  prow],
        out_specs=tile,
        out_shape=jax.ShapeDtypeStruct((m, d), F32),
        scratch_shapes=[pltpu.VMEM((tm, d), F32)],
        compiler_params=_params("parallel", "arbitrary"),
        name="ffn",
    )(x1, h2, gf, w_gu, w_gu, w_down, fg)


def _pad_cols(w, n):
    return jnp.pad(w, ((0, 0), (0, n - w.shape[1])))


def _pad_rows(w, n):
    return jnp.pad(w, ((0, n - w.shape[0]), (0, 0)))


def kernel(x, c, w_ada, b_ada, norm_mix_g, w_in, conv_w, conv_b, lru_wa, lru_ba, lru_wx, lru_bx, lru_lambda, rwkv_mu, rwkv_w0, rwkv_w2, rwkv_a0, rwkv_a2, rwkv_g2, rwkv_k_k, rwkv_k_a, rwkv_r_k, rwkv_ln_g, rwkv_ln_b, w_out, norm_ffn_g, w_gu, w_down, final_norm_g):
    bsz, seq, d = x.shape
    depth = w_ada.shape[0]
    assert depth == 1, "the closing RMSNorm is fused into the (single) layer's ffn kernel"
    dl = conv_w.shape[2]
    dr = rwkv_w0.shape[1]
    w_lora, a_lora, g_lora = rwkv_w2.shape[1], rwkv_a2.shape[1], rwkv_g2.shape[1]
    wpad, apad = LANE, LANE
    gpad = -(-g_lora // LANE) * LANE
    rkv_col0 = 2 * dl
    lora0 = rkv_col0 + 3 * dr

    x2 = x.reshape(bsz * seq, d)
    for l in range(depth):
        mod = _mod(c, w_ada[l], b_ada[l].reshape(1, -1))
        sh_m, sc_m, g_m, sh_f, sc_f, g_f = [t.reshape(bsz, 1, d) for t in jnp.split(mod, 6, axis=-1)]

        wi = jnp.swapaxes(w_in[l], 0, 1)
        o1, o2 = lora0 + w_lora, lora0 + w_lora + a_lora
        w_lora_p = jnp.concatenate(
            [_pad_rows(wi[lora0:o1], wpad), _pad_rows(wi[o1:o2], apad),
             _pad_rows(wi[o2:], gpad)], axis=0)
        mu = rwkv_mu[l].reshape(1, -1)
        mu_rkv = mu[:, :3 * dr]
        mu_lora = jnp.concatenate(
            [_pad_cols(mu[:, 3 * dr:3 * dr + w_lora], wpad),
             _pad_cols(mu[:, 3 * dr + w_lora:3 * dr + w_lora + a_lora], apad),
             _pad_cols(mu[:, 3 * dr + w_lora + a_lora:], gpad)], axis=1)
        w2p = _pad_rows(rwkv_w2[l], wpad)
        a2p = _pad_rows(rwkv_a2[l], apad)
        g2p = _pad_rows(rwkv_g2[l], gpad)

        h, p_lora = _norm(x2, norm_mix_g[l].reshape(1, d), sh_m, sc_m, w_lora_p, seq)
        p = _mm_in(h, wi, lora0, gelu_tile=1, tn=dl)

        rowv = lambda t: t.reshape(1, dr)
        rp, yp, mc, nm, bonus, gg, y_a, w_out_b, w_gu_b, w_down_b = _rwkv_a(
            p, p_lora, mu_rkv, mu_lora, rowv(rwkv_w0[l]), rowv(rwkv_a0[l]), rowv(rwkv_k_k[l]),
            rowv(rwkv_k_a[l]), rowv(rwkv_r_k[l]), w2p, a2p, g2p,
            conv_w[l], conv_b[l], lru_wa[l].astype(BF16), lru_wx[l].astype(BF16),
            lru_ba[l], lru_bx[l], lru_lambda[l], bsz, seq, rkv_col0,
            cast_ws=(w_out[l], w_gu[l], w_down[l]))
        y_b = _rwkv_b(rp, yp, mc, nm, bonus, gg, rowv(rwkv_ln_g[l]), rowv(rwkv_ln_b[l]), bsz, seq)

        x2, h2 = _mm_out(y_a, y_b, x2, g_m, w_out_b, norm_ffn_g[l].reshape(1, d), sh_f, sc_f, seq)
        x2 = _ffn(x2, h2, g_f, w_gu_b, w_down_b, final_norm_g.reshape(1, d), seq)
    return x2.reshape(bsz, seq, d)
```

```python
import functools
import math

import jax
import jax.numpy as jnp
from jax import lax
from jax.experimental import pallas as pl
from jax.experimental.pallas import tpu as pltpu

F32 = jnp.float32
BF16 = jnp.bfloat16

LRU_HEADS = 4
CONV_WIDTH = 4
LRU_C = 8.0
HEAD = 64
CHUNK = 64
PAIR = 2 * HEAD
HEADS_PER_STEP = 16
ONES_WIDTH = 256
ONES_WIDTH_B = 128
CHUNKS_PER_STEP = 4
RWKV_B_CHUNKS = 4
MOD_DMA_BANDS = 4
RMS_EPS = 1e-6
GN_EPS = 64e-5
L2_EPS = 1e-12
DECAY_SCALE = -math.exp(-0.5)
LANE = 128
SUBLANE = 8
BF16_SUBLANE = 16
VMEM_LIMIT = 56 * 1024 * 1024


def _params(*sem):
    return pltpu.CompilerParams(dimension_semantics=sem, vmem_limit_bytes=VMEM_LIMIT)


_NN = (((1,), (0,)), ((), ()))
_NT = (((1,), (1,)), ((), ()))
_TN = (((0,), (0,)), ((), ()))


def _dg(a, b, dims):
    return lax.dot_general(a, b, dims, preferred_element_type=F32)


def _split(x):
    hi = x.astype(BF16)
    lo = (x - hi.astype(F32)).astype(BF16)
    return hi, lo


def _mm3(a, b, dims=_NN):
    ah, al = _split(a)
    bh, bl = _split(b)
    return _dg(ah, bh, dims) + (_dg(ah, bl, dims) + _dg(al, bh, dims))


def _head_sums(x, ones_h):
    n = ones_h.shape[0]
    xb = x.astype(BF16)
    return jnp.concatenate([_dg(xb[:, c:c + n], ones_h, _NN) for c in range(0, x.shape[1], n)],
                           axis=1)


def _mm2_exact_lhs(a_bf16, b):
    bh, bl = _split(b)
    return _dg(a_bf16, bh, _NN) + _dg(a_bf16, bl, _NN)


def _softplus(x):
    return jnp.maximum(x, 0.0) + jnp.log1p(jnp.exp(-jnp.abs(x)))


def _iota2(shape):
    return (lax.broadcasted_iota(jnp.int32, shape, 0),
            lax.broadcasted_iota(jnp.int32, shape, 1))


def _head_ones(n):
    r, c = _iota2((n, n))
    return jnp.where((r // HEAD) == (c // HEAD), 1.0, 0.0).astype(BF16)


def _mod_body(c_ref, *refs):
    w_refs, b_ref, o_ref = refs[:-2], refs[-2], refs[-1]
    c = c_ref[...]
    ca = c * jax.nn.sigmoid(c)
    kb = w_refs[0].shape[0]
    acc = b_ref[...]
    for s, w_ref in enumerate(w_refs):
        acc = acc + _mm3(ca[:, s * kb:(s + 1) * kb], w_ref[...])
    o_ref[...] = acc


def _mod(c, w, b, tn=1024, bands=MOD_DMA_BANDS):
    bsz, d = c.shape
    n = w.shape[1]
    kb = d // bands
    return pl.pallas_call(
        _mod_body,
        grid=(n // tn,),
        in_specs=[pl.BlockSpec((bsz, d), lambda j: (0, 0))]
                 + [pl.BlockSpec((kb, tn), lambda j, s=s: (s, j)) for s in range(bands)]
                 + [pl.BlockSpec((1, tn), lambda j: (0, j))],
        out_specs=pl.BlockSpec((bsz, tn), lambda j: (0, j)),
        out_shape=jax.ShapeDtypeStruct((bsz, n), F32),
        compiler_params=_params("parallel"),
        name="mod",
    )(c, *([w] * bands), b)


def _norm_mod(x, g, sh, sc):
    y = x * lax.rsqrt(jnp.mean(x * x, axis=-1, keepdims=True) + RMS_EPS) * g
    return y * (1.0 + sc) + sh


def _norm_body(x_ref, g_ref, sh_ref, sc_ref, w_ref, o_ref, pl_ref, wb_ref):
    @pl.when(pl.program_id(0) == 0)
    def _():
        wb_ref[...] = w_ref[...].astype(BF16)

    h = _norm_mod(x_ref[...], g_ref[...], sh_ref[0], sc_ref[0]).astype(BF16)
    o_ref[...] = h
    pl_ref[...] = _dg(h, wb_ref[...], _NT)


def _norm(x2, g, sh, sc, w_lora_t, seq, tm=512):
    m, d = x2.shape
    nl = w_lora_t.shape[0]
    per_b = seq // tm
    return pl.pallas_call(
        _norm_body,
        grid=(m // tm,),
        in_specs=[pl.BlockSpec((tm, d), lambda i: (i, 0)),
                  pl.BlockSpec((1, d), lambda i: (0, 0)),
                  pl.BlockSpec((1, 1, d), lambda i: (i // per_b, 0, 0)),
                  pl.BlockSpec((1, 1, d), lambda i: (i // per_b, 0, 0)),
                  pl.BlockSpec((nl, d), lambda i: (0, 0))],
        out_specs=[pl.BlockSpec((tm, d), lambda i: (i, 0)), pl.BlockSpec((tm, nl), lambda i: (i, 0))],
        out_shape=[jax.ShapeDtypeStruct((m, d), BF16), jax.ShapeDtypeStruct((m, nl), F32)],
        scratch_shapes=[pltpu.VMEM((nl, d), BF16)],
        compiler_params=_params("arbitrary"),
        name="norm_mix",
    )(x2, g, sh, sc, w_lora_t)


def _mm_in_body(h_ref, w_ref, o_ref, wb_ref, *, gelu_tile):
    @pl.when(pl.program_id(1) == 0)
    def _():
        wb_ref[...] = w_ref[...].astype(BF16)

    @pl.when(pl.program_id(0) == gelu_tile)
    def _():
        o_ref[...] = jax.nn.gelu(_dg(h_ref[...], wb_ref[...], _NT))

    @pl.when(pl.program_id(0) != gelu_tile)
    def _():
        o_ref[...] = _dg(h_ref[...], wb_ref[...], _NT)


def _mm_in(h, wt, ncols, gelu_tile, tm=1024, tn=1024):
    m, d = h.shape
    return pl.pallas_call(
        functools.partial(_mm_in_body, gelu_tile=gelu_tile),
        grid=(ncols // tn, m // tm),
        in_specs=[pl.BlockSpec((tm, d), lambda j, i: (i, 0)),
                  pl.BlockSpec((tn, d), lambda j, i: (j, 0))],
        out_specs=pl.BlockSpec((tm, tn), lambda j, i: (i, j)),
        out_shape=jax.ShapeDtypeStruct((m, ncols), F32),
        scratch_shapes=[pltpu.VMEM((tn, d), BF16)],
        compiler_params=_params("parallel", "arbitrary"),
        name="mm_in",
    )(h, wt)


def _lru_head(h, first, u_ref, gate_ref, halo_ref, cw_ref, cb_ref, wa_ref, wx_ref, ba_ref, bx_ref,
              lam_ref, o_ref, carry_ref):
    tt = u_ref.shape[0]
    hd = u_ref.shape[1] // LRU_HEADS
    cs = slice(h * hd, (h + 1) * hd)
    p = u_ref[:, cs]
    halo = jnp.where(first, 0.0, halo_ref[:, cs])
    ext = jnp.concatenate([halo, p], axis=0)
    cw = cw_ref[:, cs]
    u = cb_ref[:, cs] + p * cw[CONV_WIDTH - 1:CONV_WIDTH, :]
    for j in range(1, CONV_WIDTH):
        shifted = pltpu.roll(ext, j, 0)[SUBLANE:, :]
        u = u + shifted * cw[CONV_WIDTH - 1 - j:CONV_WIDTH - j, :]
    ub = u.astype(BF16)
    ra = jnp.dot(ub, wa_ref[h], preferred_element_type=F32)
    rx = jnp.dot(ub, wx_ref[h], preferred_element_type=F32)
    yield
    r = jax.nn.sigmoid(ra + ba_ref[:, cs])
    ig = jax.nn.sigmoid(rx + bx_ref[:, cs])
    a = jnp.exp(r * ((-LRU_C) * _softplus(-lam_ref[:, cs])))
    mult = jnp.sqrt(1.0 - a * a)
    row = lax.broadcasted_iota(jnp.int32, (tt, hd), 0)
    mult = jnp.where(jnp.logical_and(first, row == 0), 1.0, mult)
    b = mult * (ig * u)

    groups = tt // SUBLANE
    a3 = a.reshape(groups, SUBLANE, hd)
    b3 = b.reshape(groups, SUBLANE, hd)
    sub = lax.broadcasted_iota(jnp.int32, (groups, SUBLANE, hd), 1)
    s = 1
    while s < SUBLANE:
        keep = sub >= s
        a_s = jnp.where(keep, pltpu.roll(a3, s, 1), 1.0)
        b_s = jnp.where(keep, pltpu.roll(b3, s, 1), 0.0)
        b3 = a3 * b_s + b3
        a3 = a3 * a_s
        s *= 2
    yield
    gate = gate_ref[:, cs]
    carry = carry_ref[:, cs]
    outs = []
    for g in range(groups):
        hh = b3[g] + a3[g] * carry
        carry = hh[SUBLANE - 1:SUBLANE, :]
        outs.append(hh * gate[g * SUBLANE:(g + 1) * SUBLANE, :])
    per = BF16_SUBLANE // SUBLANE
    for t0 in range(0, groups, per):
        o_ref[t0 * SUBLANE:(t0 + per) * SUBLANE, cs] = jnp.concatenate(
            outs[t0:t0 + per], axis=0).astype(BF16)
    carry_ref[:, cs] = carry
    yield


def _token_shift(x, halo, mu, first, row):
    prev = jnp.where(first, 0.0, halo[SUBLANE - 1:SUBLANE, :])
    xs = jnp.where(row == 0, prev, pltpu.roll(x, 1, 0))
    return x + (xs - x) * mu


def _pair_diag(y, left):
    return jnp.concatenate([jnp.where(left, y, 0.0), jnp.where(left, 0.0, y)], axis=0).astype(BF16)


def _pair_mm(x, y, left):
    return _dg(x.astype(BF16), _pair_diag(y, left), _NN)


def _chunk_chain(ops, store):
    ab_, bb_, kb_, rb_, v_, bt_, kt_, pe_ = ops
    rc, lane = _iota2((CHUNK, PAIR))
    cc = lane % HEAD
    left = lane < HEAD
    strict = rc > cc
    incl = rc >= cc
    diag = rc == cc
    ar16 = [jnp.concatenate([x, y], axis=0).astype(BF16) for x, y in zip(ab_, rb_)]
    bd_b = [_pair_diag(x, left) for x in bb_]
    bd_k = [_pair_diag(x, left) for x in kb_]
    bd_v = [_pair_diag(x, left) for x in v_]
    arb = [_dg(x, y, _NT) for x, y in zip(ar16, bd_b)]
    ark = [_dg(x, y, _NT) for x, y in zip(ar16, bd_k)]
    a_ab = [jnp.where(strict, x[:CHUNK], 0.0) for x in arb]
    a_rb = [jnp.where(incl, x[CHUNK:], 0.0).astype(BF16) for x in arb]
    a_akrk = [jnp.concatenate([jnp.where(strict, x[:CHUNK], 0.0), jnp.where(incl, x[CHUNK:], 0.0)],
                              axis=0).astype(BF16) for x in ark]
    yield
    base = 8
    d = [jnp.where((rc // base) == (cc // base), a, 0.0) for a in a_ab]
    d2 = [_pair_mm(t, t, left) for t in d]
    akrkv = [_dg(x, y, _NN) for x, y in zip(a_akrk, bd_v)]
    akv = [x[:CHUNK] for x in akrkv]
    rkv = [x[CHUNK:] for x in akrkv]
    x = [jnp.where(diag, 1.0, 0.0) + t for t in d]
    yield
    x = [xi + _pair_mm(t2, xi, left) for xi, t2 in zip(x, d2)]
    d4 = [_pair_mm(t2, t2, left) for t2 in d2]
    yield
    x = [xi + _pair_mm(t4, xi, left) for xi, t4 in zip(x, d4)]
    yield
    size = base
    while size < CHUNK:
        off = jnp.logical_and((rc // (2 * size)) == (cc // (2 * size)),
                              (rc // size) != (cc // size))
        o = [jnp.where(off, a, 0.0) for a in a_ab]
        ox = [_pair_mm(oi, xi, left) for oi, xi in zip(o, x)]
        yield
        x = [xi + _pair_mm(xi, oxi, left) for xi, oxi in zip(x, ox)]
        yield
        size *= 2
    t = [xi.astype(BF16) for xi in x]
    wu = [_dg(ti, jnp.concatenate([_pair_diag(y, left), _pair_diag(z, left)], axis=1), _NN)
          for ti, y, z in zip(t, ab_, akv)]
    kv = [_dg(xi.astype(BF16), y.astype(BF16), _TN) for xi, y in zip(kt_, v_)]
    yield
    ry = [_dg(xi, jnp.concatenate([_pair_diag(y[:, :PAIR], left), _pair_diag(y[:, PAIR:], left)], axis=1), _NN)
          for xi, y in zip(a_rb, wu)]
    mn = [_dg(xi.astype(BF16), y.astype(BF16), _TN) for xi, y in zip(bt_, wu)]
    yield

    def head_blocks(z):
        return jnp.where(left, z[:HEAD, :], z[HEAD:, :])

    for u in range(len(ab_)):
        store(u,
              rb_[u] + ry[u][:, :PAIR],
              ry[u][:, PAIR:] + rkv[u],
              jnp.where(diag, pe_[u], 0.0) + head_blocks(mn[u][:, :PAIR]),
              head_blocks(mn[u][:, PAIR:]) + head_blocks(kv[u]))


def _rwkv_a_body(r_ref, k_ref, v_ref, l_ref, rh_ref, kh_ref, vh_ref, lh_ref,
                 mur_ref, muk_ref, muv_ref, mul_ref, w0_ref, a0_ref, kkw_ref, kaw_ref, rkw_ref,
                 w2_ref, a2_ref, g2_ref, ones_ref, tri_ref,
                 u_ref, gate_ref, halo_ref, cw_ref, cb_ref, wa_ref, wx_ref, ba_ref, bx_ref, lam_ref,
                 *rest):
    ncast = (len(rest) - 8) // 2
    cast_in = rest[:ncast]
    rp_ref, yp_ref, m_ref, n_ref, bonus_ref, g_ref, ya_ref = rest[ncast:ncast + 7]
    cast_out = rest[ncast + 7:-1]
    carry_ref = rest[-1]
    first = pl.program_id(1) == 0

    @pl.when(first)
    def _():
        carry_ref[...] = jnp.zeros_like(carry_ref)
    cl = CHUNK
    rows = CHUNKS_PER_STEP * cl
    width = HEADS_PER_STEP * HEAD
    gw = ones_ref.shape[0]
    row_g = lax.broadcasted_iota(jnp.int32, (rows, gw), 0)
    row_l = lax.broadcasted_iota(jnp.int32, (rows, l_ref.shape[1]), 0)
    ones_h = ones_ref[...]

    lo = _token_shift(l_ref[...], lh_ref[...], mul_ref[...], first, row_l)
    act_w = _split(jnp.tanh(lo[:, 0:LANE]))
    act_a = _split(lo[:, LANE:2 * LANE])
    act_g = _split(jax.nn.sigmoid(lo[:, 2 * LANE:]))

    def lora(act, w_ref, cs, keep_low):
        (ah, al_), wb = act, w_ref[:, cs]
        out = _dg(ah, wb, _NN)
        return out + _dg(al_, wb, _NN) if keep_low else out

    def prologue(c0, out):
        cs = slice(c0, c0 + gw)
        r = _token_shift(r_ref[:, cs], rh_ref[:, cs], mur_ref[:, cs], first, row_g)
        k = _token_shift(k_ref[:, cs], kh_ref[:, cs], muk_ref[:, cs], first, row_g)
        v = _token_shift(v_ref[:, cs], vh_ref[:, cs], muv_ref[:, cs], first, row_g)
        w_lin = w0_ref[:, cs] + lora(act_w, w2_ref, cs, True)
        a_lin = a0_ref[:, cs] + lora(act_a, a2_ref, cs, False)
        g_ref[:, cs] = lora(act_g, g2_ref, cs, False).astype(BF16)
        kk = k * kkw_ref[:, cs]
        kk_ss = _head_sums(kk * kk, ones_h)
        yield
        lw = DECAY_SCALE * jax.nn.sigmoid(w_lin)
        a = jax.nn.sigmoid(a_lin)
        kk = kk * lax.rsqrt(jnp.maximum(kk_ss, L2_EPS * L2_EPS))
        kp = k * (1.0 + (a - 1.0) * kaw_ref[:, cs])
        bonus_ref[:, cs] = (_head_sums(r * kp * rkw_ref[:, cs], ones_h) * v).astype(BF16)
        lc = _mm2_exact_lhs(tri_ref[...], lw)
        yield
        p_incl = jnp.exp(lc)
        p_excl = jnp.exp(lc - lw)
        p_inv = 1.0 / p_incl
        p_end = jnp.concatenate(
            [jnp.broadcast_to(p_incl[(j + 1) * cl - 1:(j + 1) * cl, :], (cl, gw))
             for j in range(CHUNKS_PER_STEP)], axis=0)
        abar = -(kk * p_excl)
        bbar = kk * a * p_inv
        kbar = kp * p_inv
        rbar = r * p_incl
        btil = bbar * p_end
        ktil = kbar * p_end
        units = [(j, q) for j in range(CHUNKS_PER_STEP) for q in range(gw // PAIR)]
        out.extend([x[j * cl:(j + 1) * cl, q * PAIR:(q + 1) * PAIR] for j, q in units]
                   for x in (abar, bbar, kbar, rbar, v, btil, ktil, p_end))
        yield

    def make_store(c0):
        units = [(j, q) for j in range(CHUNKS_PER_STEP) for q in range(gw // PAIR)]

        def store(u, rp, yp, mm, nn):
            j, q = units[u]
            rs = slice(j * cl, (j + 1) * cl)
            qs = slice(c0 + q * PAIR, c0 + (q + 1) * PAIR)
            rp_ref[rs, qs] = rp.astype(BF16)
            yp_ref[rs, qs] = yp.astype(BF16)
            m_ref[rs, qs] = mm.astype(BF16)
            n_ref[rs, qs] = nn.astype(BF16)
        return store

    lru = (None for h in range(LRU_HEADS)
           for _ in _lru_head(h, first, u_ref, gate_ref, halo_ref, cw_ref, cb_ref, wa_ref, wx_ref,
                              ba_ref, bx_ref, lam_ref, ya_ref, carry_ref))
    chains = []
    for c0 in range(0, width, gw):
        ops = []
        for _ in prologue(c0, ops):
            for ch in chains:
                next(ch, None)
        chains.append(_chunk_chain(ops, make_store(c0)))
    live = list(chains)
    while live:
        live = [ch for ch in live if next(ch, StopIteration) is not StopIteration]
        next(lru, None)
    for _ in lru:
        pass

    for src, dst in zip(cast_in, cast_out):
        dst[...] = src[...].astype(BF16)


def _rwkv_a(p, p_lora, mu_rkv, mu_lora, w0, a0, k_k, k_a, r_k, w2p, a2p, g2p,
            conv_w, conv_b, wa, wx, ba, bx, lam, bsz, seq, rkv_col0, cast_ws=()):
    dl = conv_w.shape[1]
    lvec = lambda t: t.reshape(1, dl)
    lrow = pl.BlockSpec((1, dl), lambda b, i, q: (0, 0))

    def lru_tile(cb):
        return pl.BlockSpec((CHUNKS_PER_STEP * CHUNK, dl), lambda b, i, q: (b * nc + i, cb))
    cl = CHUNKS_PER_STEP * CHUNK
    width = HEADS_PER_STEP * HEAD
    dr = w0.shape[1]
    ngroups = dr // width
    assert ngroups == 1, "the LRU ride-along expects one grid step per row tile"
    nc = seq // cl
    lw_ = mu_lora.shape[1]
    cb0 = rkv_col0 // width
    rows8 = cl // SUBLANE
    rt, ct = _iota2((cl, cl))
    tri = jnp.where(jnp.logical_and(rt >= ct, (rt // CHUNK) == (ct // CHUNK)), 1.0, 0.0).astype(BF16)
    ones_h = _head_ones(ONES_WIDTH)
    const = lambda arr: pl.BlockSpec(arr.shape, lambda b, i, q: (0, 0))
    lora_w = [wgt.astype(BF16) for wgt in (w2p, a2p, g2p)]

    def tile(cb_off):
        return pl.BlockSpec((cl, width), lambda b, i, q: (b * nc + i, cb0 + cb_off + q))

    def halo(cb_off):
        return pl.BlockSpec(
            (SUBLANE, width),
            lambda b, i, q: (jnp.maximum((b * nc + i) * rows8 - 1, 0), cb0 + cb_off + q))

    def prow(off=0):
        return pl.BlockSpec((1, width), lambda b, i, q: (0, off + q))

    out_tile = pl.BlockSpec((cl, width), lambda b, i, q: (b * nc + i, q))
    out_mat = pl.BlockSpec((CHUNKS_PER_STEP * HEAD, width), lambda b, i, q: (b * nc + i, q))
    act = jax.ShapeDtypeStruct((bsz * seq, dr), BF16)
    mat = jax.ShapeDtypeStruct((bsz * (seq // CHUNK) * HEAD, dr), BF16)

    nsteps = bsz * nc * ngroups
    cast_specs = []
    for wgt in cast_ws:
        hold = 1
        while (wgt.shape[0] * hold) % (nsteps * BF16_SUBLANE) != 0:
            hold *= 2
        blk = (wgt.shape[0] * hold // nsteps, wgt.shape[1])
        cast_specs.append(pl.BlockSpec(
            blk, lambda b, i, q, hold=hold: (((b * nc + i) * ngroups + q) // hold, 0)))
    cast_shapes = [jax.ShapeDtypeStruct(wgt.shape, BF16) for wgt in cast_ws]

    return pl.pallas_call(
        _rwkv_a_body,
        grid=(bsz, nc, ngroups),
        in_specs=[tile(0), tile(ngroups), tile(2 * ngroups),
                  pl.BlockSpec((cl, lw_), lambda b, i, q: (b * nc + i, 0)),
                  halo(0), halo(ngroups), halo(2 * ngroups),
                  pl.BlockSpec((SUBLANE, lw_),
                               lambda b, i, q: (jnp.maximum((b * nc + i) * rows8 - 1, 0), 0)),
                  prow(0), prow(ngroups), prow(2 * ngroups),
                  pl.BlockSpec((1, lw_), lambda b, i, q: (0, 0)),
                  prow(), prow(), prow(), prow(), prow()]
                 + [pl.BlockSpec((t.shape[0], width), lambda b, i, q: (0, q)) for t in lora_w]
                 + [const(ones_h), const(tri)]
                 + [lru_tile(0), lru_tile(1),
                    pl.BlockSpec((SUBLANE, dl),
                                 lambda b, i, q: (jnp.maximum((b * nc + i) * rows8 - 1, 0), 0)),
                    const(conv_w), lrow, pl.BlockSpec(wa.shape, lambda b, i, q: (0, 0, 0)),
                    pl.BlockSpec(wx.shape, lambda b, i, q: (0, 0, 0)), lrow, lrow, lrow]
                 + cast_specs,
        out_specs=[out_tile, out_tile, out_mat, out_mat, out_tile, out_tile, lru_tile(0)] + cast_specs,
        out_shape=[act, act, mat, mat, act, act, jax.ShapeDtypeStruct((bsz * seq, dl), BF16)]
                  + cast_shapes,
        scratch_shapes=[pltpu.VMEM((1, dl), F32)],
        compiler_params=_params("arbitrary", "arbitrary", "arbitrary"),
        name="rwkv_a",
    )(p, p, p, p_lora, p, p, p, p_lora, mu_rkv, mu_rkv, mu_rkv, mu_lora, w0, a0, k_k, k_a, r_k,
      *lora_w, ones_h, tri,
      p, p, p, conv_w, lvec(conv_b), wa, wx, lvec(ba), lvec(bx), lvec(lam), *cast_ws)


def _rwkv_b_body(rp_ref, yp_ref, m_ref, n_ref, bonus_ref, g_ref, lng_ref, lnb_ref, ones_ref,
                 o_ref, state_ref):
    @pl.when(pl.program_id(0) == 0)
    def _():
        state_ref[...] = jnp.zeros_like(state_ref)

    nb = rp_ref.shape[0]
    npairs = state_ref.shape[0] // nb
    units = [(b, q) for b in range(nb) for q in range(npairs)]
    ps = [slice(q * PAIR, (q + 1) * PAIR) for q in range(npairs)]
    left = lax.broadcasted_iota(jnp.int32, (HEAD, PAIR), 1) < HEAD
    ones_h = ones_ref[...]
    inv_n = 1.0 / HEAD
    state = [state_ref[u] for u in range(len(units))]
    for j in range(rp_ref.shape[1] // CHUNK):
        rs = slice(j * CHUNK, (j + 1) * CHUNK)
        ks = slice(j * HEAD, (j + 1) * HEAD)
        g0 = [_pair_diag(s, left) for s in state]
        ys = [_dg(rp_ref[b, rs, ps[q]], g0[u], _NN) + yp_ref[b, rs, ps[q]]
              for u, (b, q) in enumerate(units)]
        state = [_dg(m_ref[b, ks, ps[q]], g0[u], _NN) + n_ref[b, ks, ps[q]]
                 for u, (b, q) in enumerate(units)]
        y = jnp.concatenate([jnp.concatenate(ys[b * npairs:(b + 1) * npairs], axis=1)
                             for b in range(nb)], axis=0)
        yc = y - _head_sums(y, ones_h) * inv_n
        var = _head_sums(yc * yc, ones_h) * inv_n
        yn = yc * lax.rsqrt(var + GN_EPS) * lng_ref[...] + lnb_ref[...]
        for b in range(nb):
            bs = slice(b * CHUNK, (b + 1) * CHUNK)
            o_ref[b, rs, :] = ((yn[bs] + bonus_ref[b, rs, :]) * g_ref[b, rs, :]).astype(BF16)
    for u in range(len(units)):
        state_ref[u] = state[u]


def _rwkv_b(rp, yp, mc, nm, bonus, g, ln_g, ln_b, bsz, seq):
    cl = RWKV_B_CHUNKS * CHUNK
    dr = rp.shape[1]
    nc = seq // cl
    by_seq = lambda t: t.reshape(bsz, -1, dr)
    tile = pl.BlockSpec((bsz, cl, dr), lambda i: (0, i, 0))
    mat = pl.BlockSpec((bsz, RWKV_B_CHUNKS * HEAD, dr), lambda i: (0, i, 0))
    prow = pl.BlockSpec((1, dr), lambda i: (0, 0))
    ones_h = _head_ones(ONES_WIDTH_B)
    out = pl.pallas_call(
        _rwkv_b_body,
        grid=(nc,),
        in_specs=[tile, tile, mat, mat, tile, tile, prow, prow,
                  pl.BlockSpec(ones_h.shape, lambda i: (0, 0))],
        out_specs=tile,
        out_shape=jax.ShapeDtypeStruct((bsz, seq, dr), BF16),
        scratch_shapes=[pltpu.VMEM((bsz * (dr // PAIR), HEAD, PAIR), F32)],
        compiler_params=_params("arbitrary"),
        name="rwkv_b",
    )(by_seq(rp), by_seq(yp), by_seq(mc), by_seq(nm), by_seq(bonus), by_seq(g), ln_g, ln_b, ones_h)
    return out.reshape(bsz * seq, dr)


def _mm_out_body(ya_ref, yb_ref, x_ref, gm_ref, w_ref, g_ref, sh_ref, sc_ref, o_ref, h_ref, *, sub):
    da = ya_ref.shape[1]
    for r0 in range(0, x_ref.shape[0], sub):
        rs = slice(r0, r0 + sub)
        mix = (jnp.dot(ya_ref[rs, :], w_ref[:da, :], preferred_element_type=F32)
               + jnp.dot(yb_ref[rs, :], w_ref[da:, :], preferred_element_type=F32))
        x1 = x_ref[rs, :] + gm_ref[0] * mix
        o_ref[rs, :] = x1
        h_ref[rs, :] = _norm_mod(x1, g_ref[...], sh_ref[0], sc_ref[0]).astype(BF16)


def _mm_out(ya, yb, x2, gm, w, g, sh, sc, seq, tm=512, sub=256):
    m, d = x2.shape
    per_b = seq // tm
    brow = pl.BlockSpec((1, 1, d), lambda i: (i // per_b, 0, 0))
    tile = pl.BlockSpec((tm, d), lambda i: (i, 0))
    return pl.pallas_call(
        functools.partial(_mm_out_body, sub=sub),
        grid=(m // tm,),
        in_specs=[pl.BlockSpec((tm, ya.shape[1]), lambda i: (i, 0)),
                  pl.BlockSpec((tm, yb.shape[1]), lambda i: (i, 0)),
                  tile, brow,
                  pl.BlockSpec(w.shape, lambda i: (0, 0)),
                  pl.BlockSpec((1, d), lambda i: (0, 0)), brow, brow],
        out_specs=[tile, tile],
        out_shape=[jax.ShapeDtypeStruct((m, d), F32), jax.ShapeDtypeStruct((m, d), BF16)],
        compiler_params=_params("parallel"),
        name="mm_out",
    )(ya, yb, x2, gm, w, g, sh, sc)


def _ffn_body(x_ref, h_ref, gf_ref, wg_ref, wu_ref, wd_ref, fg_ref, o_ref, acc_ref):
    f = pl.program_id(1)

    @pl.when(f == 0)
    def _():
        acc_ref[...] = jnp.zeros_like(acc_ref)

    h = h_ref[...]
    gate = jnp.dot(h, wg_ref[...], preferred_element_type=F32)
    up = jnp.dot(h, wu_ref[...], preferred_element_type=F32)
    act = (gate * jax.nn.sigmoid(gate) * up).astype(BF16)
    acc_ref[...] += jnp.dot(act, wd_ref[...], preferred_element_type=F32)

    @pl.when(f == pl.num_programs(1) - 1)
    def _():
        y = x_ref[...] + gf_ref[0] * acc_ref[...]
        o_ref[...] = (y * lax.rsqrt(jnp.mean(y * y, axis=-1, keepdims=True) + RMS_EPS)
                      * fg_ref[...])


def _ffn(x1, h2, gf, w_gu, w_down, fg, seq, tm=512, tf=512):
    m, d = x1.shape
    dff = w_down.shape[0]
    nf = dff // tf
    assert seq % tm == 0 and dff % tf == 0, "row tiles must not straddle sequences"
    per_b = seq // tm
    tile = pl.BlockSpec((tm, d), lambda i, f: (i, 0))
    prow = pl.BlockSpec((1, d), lambda i, f: (0, 0))
    return pl.pallas_call(
        _ffn_body,
        grid=(m // tm, nf),
        in_specs=[tile, tile,
                  pl.BlockSpec((1, 1, d), lambda i, f: (i // per_b, 0, 0)),
                  pl.BlockSpec((d, tf), lambda i, f: (0, f)),
                  pl.BlockSpec((d, tf), lambda i, f: (0, nf + f)),
                  pl.BlockSpec((tf, d), lambda i, f: (f, 0)),
                  prow],
        out_specs=tile,
        out_shape=jax.ShapeDtypeStruct((m, d), F32),
        scratch_shapes=[pltpu.VMEM((tm, d), F32)],
        compiler_params=_params("parallel", "arbitrary"),
        name="ffn",
    )(x1, h2, gf, w_gu, w_gu, w_down, fg)


def _pad_cols(w, n):
    return jnp.pad(w, ((0, 0), (0, n - w.shape[1])))


def _pad_rows(w, n):
    return jnp.pad(w, ((0, n - w.shape[0]), (0, 0)))


def kernel(x, c, w_ada, b_ada, norm_mix_g, w_in, conv_w, conv_b, lru_wa, lru_ba, lru_wx, lru_bx, lru_lambda, rwkv_mu, rwkv_w0, rwkv_w2, rwkv_a0, rwkv_a2, rwkv_g2, rwkv_k_k, rwkv_k_a, rwkv_r_k, rwkv_ln_g, rwkv_ln_b, w_out, norm_ffn_g, w_gu, w_down, final_norm_g):
    bsz, seq, d = x.shape
    depth = w_ada.shape[0]
    assert depth == 1, "the closing RMSNorm is fused into the (single) layer's ffn kernel"
    dl = conv_w.shape[2]
    dr = rwkv_w0.shape[1]
    w_lora, a_lora, g_lora = rwkv_w2.shape[1], rwkv_a2.shape[1], rwkv_g2.shape[1]
    wpad, apad = LANE, LANE
    gpad = -(-g_lora // LANE) * LANE
    rkv_col0 = 2 * dl
    lora0 = rkv_col0 + 3 * dr

    x2 = x.reshape(bsz * seq, d)
    for l in range(depth):
        mod = _mod(c, w_ada[l], b_ada[l].reshape(1, -1))
        sh_m, sc_m, g_m, sh_f, sc_f, g_f = [t.reshape(bsz, 1, d) for t in jnp.split(mod, 6, axis=-1)]

        wi = jnp.swapaxes(w_in[l], 0, 1)
        o1, o2 = lora0 + w_lora, lora0 + w_lora + a_lora
        w_lora_p = jnp.concatenate(
            [_pad_rows(wi[lora0:o1], wpad), _pad_rows(wi[o1:o2], apad),
             _pad_rows(wi[o2:], gpad)], axis=0)
        mu = rwkv_mu[l].reshape(1, -1)
        mu_rkv = mu[:, :3 * dr]
        mu_lora = jnp.concatenate(
            [_pad_cols(mu[:, 3 * dr:3 * dr + w_lora], wpad),
             _pad_cols(mu[:, 3 * dr + w_lora:3 * dr + w_lora + a_lora], apad),
             _pad_cols(mu[:, 3 * dr + w_lora + a_lora:], gpad)], axis=1)
        w2p = _pad_rows(rwkv_w2[l], wpad)
        a2p = _pad_rows(rwkv_a2[l], apad)
        g2p = _pad_rows(rwkv_g2[l], gpad)

        h, p_lora = _norm(x2, norm_mix_g[l].reshape(1, d), sh_m, sc_m, w_lora_p, seq)
        p = _mm_in(h, wi, lora0, gelu_tile=1, tn=dl)

        rowv = lambda t: t.reshape(1, dr)
        rp, yp, mc, nm, bonus, gg, y_a, w_out_b, w_gu_b, w_down_b = _rwkv_a(
            p, p_lora, mu_rkv, mu_lora, rowv(rwkv_w0[l]), rowv(rwkv_a0[l]), rowv(rwkv_k_k[l]),
            rowv(rwkv_k_a[l]), rowv(rwkv_r_k[l]), w2p, a2p, g2p,
            conv_w[l], conv_b[l], lru_wa[l].astype(BF16), lru_wx[l].astype(BF16),
            lru_ba[l], lru_bx[l], lru_lambda[l], bsz, seq, rkv_col0,
            cast_ws=(w_out[l], w_gu[l], w_down[l]))
        y_b = _rwkv_b(rp, yp, mc, nm, bonus, gg, rowv(rwkv_ln_g[l]), rowv(rwkv_ln_b[l]), bsz, seq)

        x2, h2 = _mm_out(y_a, y_b, x2, g_m, w_out_b, norm_ffn_g[l].reshape(1, d), sh_f, sc_f, seq)
        x2 = _ffn(x2, h2, g_f, w_gu_b, w_down_b, final_norm_g.reshape(1, d), seq)
    return x2.reshape(bsz, seq, d)
```

```python
import functools
import math

import jax
import jax.numpy as jnp
from jax import lax
from jax.experimental import pallas as pl
from jax.experimental.pallas import tpu as pltpu

F32 = jnp.float32
BF16 = jnp.bfloat16

LRU_HEADS = 4
CONV_WIDTH = 4
LRU_C = 8.0
HEAD = 64
CHUNK = 64
PAIR = 2 * HEAD
HEADS_PER_STEP = 16
ONES_WIDTH = 256
ONES_WIDTH_B = 128
CHUNKS_PER_STEP = 4
RWKV_B_CHUNKS = 4
MOD_DMA_BANDS = 4
RMS_EPS = 1e-6
GN_EPS = 64e-5
L2_EPS = 1e-12
DECAY_SCALE = -math.exp(-0.5)
LANE = 128
SUBLANE = 8
BF16_SUBLANE = 16
VMEM_LIMIT = 56 * 1024 * 1024


def _params(*sem):
    return pltpu.CompilerParams(dimension_semantics=sem, vmem_limit_bytes=VMEM_LIMIT)


_NN = (((1,), (0,)), ((), ()))
_NT = (((1,), (1,)), ((), ()))
_TN = (((0,), (0,)), ((), ()))


def _dg(a, b, dims):
    return lax.dot_general(a, b, dims, preferred_element_type=F32)


def _split(x):
    hi = x.astype(BF16)
    lo = (x - hi.astype(F32)).astype(BF16)
    return hi, lo


def _mm3(a, b, dims=_NN):
    ah, al = _split(a)
    bh, bl = _split(b)
    return _dg(ah, bh, dims) + (_dg(ah, bl, dims) + _dg(al, bh, dims))


def _head_sums(x, ones_h):
    n = ones_h.shape[0]
    xb = x.astype(BF16)
    return jnp.concatenate([_dg(xb[:, c:c + n], ones_h, _NN) for c in range(0, x.shape[1], n)],
                           axis=1)


def _mm2_exact_lhs(a_bf16, b):
    bh, bl = _split(b)
    return _dg(a_bf16, bh, _NN) + _dg(a_bf16, bl, _NN)


def _softplus(x):
    return jnp.maximum(x, 0.0) + jnp.log1p(jnp.exp(-jnp.abs(x)))


def _iota2(shape):
    return (lax.broadcasted_iota(jnp.int32, shape, 0),
            lax.broadcasted_iota(jnp.int32, shape, 1))


def _head_ones(n):
    r, c = _iota2((n, n))
    return jnp.where((r // HEAD) == (c // HEAD), 1.0, 0.0).astype(BF16)


def _mod_body(c_ref, *refs):
    w_refs, b_ref, o_ref = refs[:-2], refs[-2], refs[-1]
    c = c_ref[...]
    ca = c * jax.nn.sigmoid(c)
    kb = w_refs[0].shape[0]
    acc = b_ref[...]
    for s, w_ref in enumerate(w_refs):
        acc = acc + _mm3(ca[:, s * kb:(s + 1) * kb], w_ref[...])
    o_ref[...] = acc


def _mod(c, w, b, tn=1024, bands=MOD_DMA_BANDS):
    bsz, d = c.shape
    n = w.shape[1]
    kb = d // bands
    return pl.pallas_call(
        _mod_body,
        grid=(n // tn,),
        in_specs=[pl.BlockSpec((bsz, d), lambda j: (0, 0))]
                 + [pl.BlockSpec((kb, tn), lambda j, s=s: (s, j)) for s in range(bands)]
                 + [pl.BlockSpec((1, tn), lambda j: (0, j))],
        out_specs=pl.BlockSpec((bsz, tn), lambda j: (0, j)),
        out_shape=jax.ShapeDtypeStruct((bsz, n), F32),
        compiler_params=_params("parallel"),
        name="mod",
    )(c, *([w] * bands), b)


def _norm_mod(x, g, sh, sc):
    y = x * lax.rsqrt(jnp.mean(x * x, axis=-1, keepdims=True) + RMS_EPS) * g
    return y * (1.0 + sc) + sh


def _norm_body(x_ref, g_ref, sh_ref, sc_ref, w_ref, o_ref, pl_ref, wb_ref):
    @pl.when(pl.program_id(0) == 0)
    def _():
        wb_ref[...] = w_ref[...].astype(BF16)

    h = _norm_mod(x_ref[...], g_ref[...], sh_ref[0], sc_ref[0]).astype(BF16)
    o_ref[...] = h
    pl_ref[...] = _dg(h, wb_ref[...], _NT)


def _norm(x2, g, sh, sc, w_lora_t, seq, tm=512):
    m, d = x2.shape
    nl = w_lora_t.shape[0]
    per_b = seq // tm
    return pl.pallas_call(
        _norm_body,
        grid=(m // tm,),
        in_specs=[pl.BlockSpec((tm, d), lambda i: (i, 0)),
                  pl.BlockSpec((1, d), lambda i: (0, 0)),
                  pl.BlockSpec((1, 1, d), lambda i: (i // per_b, 0, 0)),
                  pl.BlockSpec((1, 1, d), lambda i: (i // per_b, 0, 0)),
                  pl.BlockSpec((nl, d), lambda i: (0, 0))],
        out_specs=[pl.BlockSpec((tm, d), lambda i: (i, 0)), pl.BlockSpec((tm, nl), lambda i: (i, 0))],
        out_shape=[jax.ShapeDtypeStruct((m, d), BF16), jax.ShapeDtypeStruct((m, nl), F32)],
        scratch_shapes=[pltpu.VMEM((nl, d), BF16)],
        compiler_params=_params("arbitrary"),
        name="norm_mix",
    )(x2, g, sh, sc, w_lora_t)


def _mm_in_body(h_ref, w_ref, o_ref, wb_ref, *, gelu_tile):
    @pl.when(pl.program_id(1) == 0)
    def _():
        wb_ref[...] = w_ref[...].astype(BF16)

    @pl.when(pl.program_id(0) == gelu_tile)
    def _():
        o_ref[...] = jax.nn.gelu(_dg(h_ref[...], wb_ref[...], _NT))

    @pl.when(pl.program_id(0) != gelu_tile)
    def _():
        o_ref[...] = _dg(h_ref[...], wb_ref[...], _NT)


def _mm_in(h, wt, ncols, gelu_tile, tm=1024, tn=1024):
    m, d = h.shape
    return pl.pallas_call(
        functools.partial(_mm_in_body, gelu_tile=gelu_tile),
        grid=(ncols // tn, m // tm),
        in_specs=[pl.BlockSpec((tm, d), lambda j, i: (i, 0)),
                  pl.BlockSpec((tn, d), lambda j, i: (j, 0))],
        out_specs=pl.BlockSpec((tm, tn), lambda j, i: (i, j)),
        out_shape=jax.ShapeDtypeStruct((m, ncols), F32),
        scratch_shapes=[pltpu.VMEM((tn, d), BF16)],
        compiler_params=_params("parallel", "arbitrary"),
        name="mm_in",
    )(h, wt)


def _lru_head(h, first, u_ref, gate_ref, halo_ref, cw_ref, cb_ref, wa_ref, wx_ref, ba_ref, bx_ref,
              lam_ref, o_ref, carry_ref):
    tt = u_ref.shape[0]
    hd = u_ref.shape[1] // LRU_HEADS
    cs = slice(h * hd, (h + 1) * hd)
    p = u_ref[:, cs]
    halo = jnp.where(first, 0.0, halo_ref[:, cs])
    ext = jnp.concatenate([halo, p], axis=0)
    cw = cw_ref[:, cs]
    u = cb_ref[:, cs] + p * cw[CONV_WIDTH - 1:CONV_WIDTH, :]
    for j in range(1, CONV_WIDTH):
        shifted = pltpu.roll(ext, j, 0)[SUBLANE:, :]
        u = u + shifted * cw[CONV_WIDTH - 1 - j:CONV_WIDTH - j, :]
    ub = u.astype(BF16)
    ra = jnp.dot(ub, wa_ref[h], preferred_element_type=F32)
    rx = jnp.dot(ub, wx_ref[h], preferred_element_type=F32)
    yield
    r = jax.nn.sigmoid(ra + ba_ref[:, cs])
    ig = jax.nn.sigmoid(rx + bx_ref[:, cs])
    a = jnp.exp(r * ((-LRU_C) * _softplus(-lam_ref[:, cs])))
    mult = jnp.sqrt(1.0 - a * a)
    row = lax.broadcasted_iota(jnp.int32, (tt, hd), 0)
    mult = jnp.where(jnp.logical_and(first, row == 0), 1.0, mult)
    b = mult * (ig * u)

    groups = tt // SUBLANE
    a3 = a.reshape(groups, SUBLANE, hd)
    b3 = b.reshape(groups, SUBLANE, hd)
    sub = lax.broadcasted_iota(jnp.int32, (groups, SUBLANE, hd), 1)
    s = 1
    while s < SUBLANE:
        keep = sub >= s
        a_s = jnp.where(keep, pltpu.roll(a3, s, 1), 1.0)
        b_s = jnp.where(keep, pltpu.roll(b3, s, 1), 0.0)
        b3 = a3 * b_s + b3
        a3 = a3 * a_s
        s *= 2
    yield
    gate = gate_ref[:, cs]
    carry = carry_ref[:, cs]
    outs = []
    for g in range(groups):
        hh = b3[g] + a3[g] * carry
        carry = hh[SUBLANE - 1:SUBLANE, :]
        outs.append(hh * gate[g * SUBLANE:(g + 1) * SUBLANE, :])
    per = BF16_SUBLANE // SUBLANE
    for t0 in range(0, groups, per):
        o_ref[t0 * SUBLANE:(t0 + per) * SUBLANE, cs] = jnp.concatenate(
            outs[t0:t0 + per], axis=0).astype(BF16)
    carry_ref[:, cs] = carry
    yield


def _token_shift(x, halo, mu, first, row):
    prev = jnp.where(first, 0.0, halo[SUBLANE - 1:SUBLANE, :])
    xs = jnp.where(row == 0, prev, pltpu.roll(x, 1, 0))
    return x + (xs - x) * mu


def _pair_diag(y, left):
    return jnp.concatenate([jnp.where(left, y, 0.0), jnp.where(left, 0.0, y)], axis=0).astype(BF16)


def _pair_mm(x, y, left):
    return _dg(x.astype(BF16), _pair_diag(y, left), _NN)


def _chunk_chain(ops, store):
    ab_, bb_, kb_, rb_, v_, bt_, kt_, pe_ = ops
    rc, lane = _iota2((CHUNK, PAIR))
    cc = lane % HEAD
    left = lane < HEAD
    strict = rc > cc
    incl = rc >= cc
    diag = rc == cc
    ar16 = [jnp.concatenate([x, y], axis=0).astype(BF16) for x, y in zip(ab_, rb_)]
    bd_b = [_pair_diag(x, left) for x in bb_]
    bd_k = [_pair_diag(x, left) for x in kb_]
    bd_v = [_pair_diag(x, left) for x in v_]
    arbk = [_dg(x, jnp.concatenate([y, z], axis=0), _NT) for x, y, z in zip(ar16, bd_b, bd_k)]
    arb = [x[:, :PAIR] for x in arbk]
    ark = [x[:, PAIR:] for x in arbk]
    a_ab = [jnp.where(strict, x[:CHUNK], 0.0) for x in arb]
    a_rb = [jnp.where(incl, x[CHUNK:], 0.0).astype(BF16) for x in arb]
    a_akrk = [jnp.concatenate([jnp.where(strict, x[:CHUNK], 0.0), jnp.where(incl, x[CHUNK:], 0.0)],
                              axis=0).astype(BF16) for x in ark]
    yield
    base = 8
    d = [jnp.where((rc // base) == (cc // base), a, 0.0) for a in a_ab]
    d2 = [_pair_mm(t, t, left) for t in d]
    akrkv = [_dg(x, y, _NN) for x, y in zip(a_akrk, bd_v)]
    akv = [x[:CHUNK] for x in akrkv]
    rkv = [x[CHUNK:] for x in akrkv]
    x = [jnp.where(diag, 1.0, 0.0) + t for t in d]
    yield
    xd = [_dg(t2.astype(BF16),
              jnp.concatenate([_pair_diag(xi, left), _pair_diag(t2, left)], axis=1), _NN)
          for xi, t2 in zip(x, d2)]
    x = [xi + y[:, :PAIR] for xi, y in zip(x, xd)]
    d4 = [y[:, PAIR:] for y in xd]
    yield
    x = [xi + _pair_mm(t4, xi, left) for xi, t4 in zip(x, d4)]
    yield
    size = base
    while size < CHUNK:
        off = jnp.logical_and((rc // (2 * size)) == (cc // (2 * size)),
                              (rc // size) != (cc // size))
        o = [jnp.where(off, a, 0.0) for a in a_ab]
        ox = [_pair_mm(oi, xi, left) for oi, xi in zip(o, x)]
        yield
        x = [xi + _pair_mm(xi, oxi, left) for xi, oxi in zip(x, ox)]
        yield
        size *= 2
    t = [xi.astype(BF16) for xi in x]
    wu = [_dg(ti, jnp.concatenate([_pair_diag(y, left), _pair_diag(z, left)], axis=1), _NN)
          for ti, y, z in zip(t, ab_, akv)]
    kv = [_dg(xi.astype(BF16), y.astype(BF16), _TN) for xi, y in zip(kt_, v_)]
    yield
    ry = [_dg(xi, jnp.concatenate([_pair_diag(y[:, :PAIR], left), _pair_diag(y[:, PAIR:], left)], axis=1), _NN)
          for xi, y in zip(a_rb, wu)]
    mn = [_dg(xi.astype(BF16), y.astype(BF16), _TN) for xi, y in zip(bt_, wu)]
    yield

    def head_blocks(z):
        return jnp.where(left, z[:HEAD, :], z[HEAD:, :])

    for u in range(len(ab_)):
        store(u,
              rb_[u] + ry[u][:, :PAIR],
              ry[u][:, PAIR:] + rkv[u],
              jnp.where(diag, pe_[u], 0.0) + head_blocks(mn[u][:, :PAIR]),
              head_blocks(mn[u][:, PAIR:]) + head_blocks(kv[u]))


def _rwkv_a_body(r_ref, k_ref, v_ref, l_ref, rh_ref, kh_ref, vh_ref, lh_ref,
                 mur_ref, muk_ref, muv_ref, mul_ref, w0_ref, a0_ref, kkw_ref, kaw_ref, rkw_ref,
                 w2_ref, a2_ref, g2_ref, ones_ref, tri_ref,
                 u_ref, gate_ref, halo_ref, cw_ref, cb_ref, wa_ref, wx_ref, ba_ref, bx_ref, lam_ref,
                 *rest):
    ncast = (len(rest) - 8) // 2
    cast_in = rest[:ncast]
    rp_ref, yp_ref, m_ref, n_ref, bonus_ref, g_ref, ya_ref = rest[ncast:ncast + 7]
    cast_out = rest[ncast + 7:-1]
    carry_ref = rest[-1]
    first = pl.program_id(1) == 0

    @pl.when(first)
    def _():
        carry_ref[...] = jnp.zeros_like(carry_ref)
    cl = CHUNK
    rows = CHUNKS_PER_STEP * cl
    width = HEADS_PER_STEP * HEAD
    gw = ones_ref.shape[0]
    row_g = lax.broadcasted_iota(jnp.int32, (rows, gw), 0)
    row_l = lax.broadcasted_iota(jnp.int32, (rows, l_ref.shape[1]), 0)
    ones_h = ones_ref[...]

    lo = _token_shift(l_ref[...], lh_ref[...], mul_ref[...], first, row_l)
    act_w = _split(jnp.tanh(lo[:, 0:LANE]))
    act_a = _split(lo[:, LANE:2 * LANE])
    act_g = _split(jax.nn.sigmoid(lo[:, 2 * LANE:]))

    def lora(act, w_ref, cs, keep_low):
        (ah, al_), wb = act, w_ref[:, cs]
        out = _dg(ah, wb, _NN)
        return out + _dg(al_, wb, _NN) if keep_low else out

    def prologue(c0, out):
        cs = slice(c0, c0 + gw)
        r = _token_shift(r_ref[:, cs], rh_ref[:, cs], mur_ref[:, cs], first, row_g)
        k = _token_shift(k_ref[:, cs], kh_ref[:, cs], muk_ref[:, cs], first, row_g)
        v = _token_shift(v_ref[:, cs], vh_ref[:, cs], muv_ref[:, cs], first, row_g)
        w_lin = w0_ref[:, cs] + lora(act_w, w2_ref, cs, True)
        a_lin = a0_ref[:, cs] + lora(act_a, a2_ref, cs, False)
        g_ref[:, cs] = lora(act_g, g2_ref, cs, False).astype(BF16)
        kk = k * kkw_ref[:, cs]
        kk_ss = _head_sums(kk * kk, ones_h)
        yield
        lw = DECAY_SCALE * jax.nn.sigmoid(w_lin)
        a = jax.nn.sigmoid(a_lin)
        kk = kk * lax.rsqrt(jnp.maximum(kk_ss, L2_EPS * L2_EPS))
        kp = k * (1.0 + (a - 1.0) * kaw_ref[:, cs])
        bonus_ref[:, cs] = (_head_sums(r * kp * rkw_ref[:, cs], ones_h) * v).astype(BF16)
        lc = _mm2_exact_lhs(tri_ref[...], lw)
        yield
        p_incl = jnp.exp(lc)
        p_excl = jnp.exp(lc - lw)
        p_inv = 1.0 / p_incl
        p_end = jnp.concatenate(
            [jnp.broadcast_to(p_incl[(j + 1) * cl - 1:(j + 1) * cl, :], (cl, gw))
             for j in range(CHUNKS_PER_STEP)], axis=0)
        abar = -(kk * p_excl)
        bbar = kk * a * p_inv
        kbar = kp * p_inv
        rbar = r * p_incl
        btil = bbar * p_end
        ktil = kbar * p_end
        units = [(j, q) for j in range(CHUNKS_PER_STEP) for q in range(gw // PAIR)]
        out.extend([x[j * cl:(j + 1) * cl, q * PAIR:(q + 1) * PAIR] for j, q in units]
                   for x in (abar, bbar, kbar, rbar, v, btil, ktil, p_end))
        yield

    def make_store(c0):
        units = [(j, q) for j in range(CHUNKS_PER_STEP) for q in range(gw // PAIR)]

        def store(u, rp, yp, mm, nn):
            j, q = units[u]
            rs = slice(j * cl, (j + 1) * cl)
            qs = slice(c0 + q * PAIR, c0 + (q + 1) * PAIR)
            rp_ref[rs, qs] = rp.astype(BF16)
            yp_ref[rs, qs] = yp.astype(BF16)
            m_ref[rs, qs] = mm.astype(BF16)
            n_ref[rs, qs] = nn.astype(BF16)
        return store

    lru = (None for h in range(LRU_HEADS)
           for _ in _lru_head(h, first, u_ref, gate_ref, halo_ref, cw_ref, cb_ref, wa_ref, wx_ref,
                              ba_ref, bx_ref, lam_ref, ya_ref, carry_ref))
    chains = []
    for c0 in range(0, width, gw):
        ops = []
        for _ in prologue(c0, ops):
            for ch in chains:
                next(ch, None)
        chains.append(_chunk_chain(ops, make_store(c0)))
    live = list(chains)
    while live:
        live = [ch for ch in live if next(ch, StopIteration) is not StopIteration]
        next(lru, None)
    for _ in lru:
        pass

    for src, dst in zip(cast_in, cast_out):
        dst[...] = src[...].astype(BF16)


def _rwkv_a(p, p_lora, mu_rkv, mu_lora, w0, a0, k_k, k_a, r_k, w2p, a2p, g2p,
            conv_w, conv_b, wa, wx, ba, bx, lam, bsz, seq, rkv_col0, cast_ws=()):
    dl = conv_w.shape[1]
    lvec = lambda t: t.reshape(1, dl)
    lrow = pl.BlockSpec((1, dl), lambda b, i, q: (0, 0))

    def lru_tile(cb):
        return pl.BlockSpec((CHUNKS_PER_STEP * CHUNK, dl), lambda b, i, q: (b * nc + i, cb))
    cl = CHUNKS_PER_STEP * CHUNK
    width = HEADS_PER_STEP * HEAD
    dr = w0.shape[1]
    ngroups = dr // width
    assert ngroups == 1, "the LRU ride-along expects one grid step per row tile"
    nc = seq // cl
    lw_ = mu_lora.shape[1]
    cb0 = rkv_col0 // width
    rows8 = cl // SUBLANE
    rt, ct = _iota2((cl, cl))
    tri = jnp.where(jnp.logical_and(rt >= ct, (rt // CHUNK) == (ct // CHUNK)), 1.0, 0.0).astype(BF16)
    ones_h = _head_ones(ONES_WIDTH)
    const = lambda arr: pl.BlockSpec(arr.shape, lambda b, i, q: (0, 0))
    lora_w = [wgt.astype(BF16) for wgt in (w2p, a2p, g2p)]

    def tile(cb_off):
        return pl.BlockSpec((cl, width), lambda b, i, q: (b * nc + i, cb0 + cb_off + q))

    def halo(cb_off):
        return pl.BlockSpec(
            (SUBLANE, width),
            lambda b, i, q: (jnp.maximum((b * nc + i) * rows8 - 1, 0), cb0 + cb_off + q))

    def prow(off=0):
        return pl.BlockSpec((1, width), lambda b, i, q: (0, off + q))

    out_tile = pl.BlockSpec((cl, width), lambda b, i, q: (b * nc + i, q))
    out_mat = pl.BlockSpec((CHUNKS_PER_STEP * HEAD, width), lambda b, i, q: (b * nc + i, q))
    act = jax.ShapeDtypeStruct((bsz * seq, dr), BF16)
    mat = jax.ShapeDtypeStruct((bsz * (seq // CHUNK) * HEAD, dr), BF16)

    nsteps = bsz * nc * ngroups
    cast_specs = []
    for wgt in cast_ws:
        hold = 1
        while (wgt.shape[0] * hold) % (nsteps * BF16_SUBLANE) != 0:
            hold *= 2
        blk = (wgt.shape[0] * hold // nsteps, wgt.shape[1])
        cast_specs.append(pl.BlockSpec(
            blk, lambda b, i, q, hold=hold: (((b * nc + i) * ngroups + q) // hold, 0)))
    cast_shapes = [jax.ShapeDtypeStruct(wgt.shape, BF16) for wgt in cast_ws]

    return pl.pallas_call(
        _rwkv_a_body,
        grid=(bsz, nc, ngroups),
        in_specs=[tile(0), tile(ngroups), tile(2 * ngroups),
                  pl.BlockSpec((cl, lw_), lambda b, i, q: (b * nc + i, 0)),
                  halo(0), halo(ngroups), halo(2 * ngroups),
                  pl.BlockSpec((SUBLANE, lw_),
                               lambda b, i, q: (jnp.maximum((b * nc + i) * rows8 - 1, 0), 0)),
                  prow(0), prow(ngroups), prow(2 * ngroups),
                  pl.BlockSpec((1, lw_), lambda b, i, q: (0, 0)),
                  prow(), prow(), prow(), prow(), prow()]
                 + [pl.BlockSpec((t.shape[0], width), lambda b, i, q: (0, q)) for t in lora_w]
                 + [const(ones_h), const(tri)]
                 + [lru_tile(0), lru_tile(1),
                    pl.BlockSpec((SUBLANE, dl),
                                 lambda b, i, q: (jnp.maximum((b * nc + i) * rows8 - 1, 0), 0)),
                    const(conv_w), lrow, pl.BlockSpec(wa.shape, lambda b, i, q: (0, 0, 0)),
                    pl.BlockSpec(wx.shape, lambda b, i, q: (0, 0, 0)), lrow, lrow, lrow]
                 + cast_specs,
        out_specs=[out_tile, out_tile, out_mat, out_mat, out_tile, out_tile, lru_tile(0)] + cast_specs,
        out_shape=[act, act, mat, mat, act, act, jax.ShapeDtypeStruct((bsz * seq, dl), BF16)]
                  + cast_shapes,
        scratch_shapes=[pltpu.VMEM((1, dl), F32)],
        compiler_params=_params("arbitrary", "arbitrary", "arbitrary"),
        name="rwkv_a",
    )(p, p, p, p_lora, p, p, p, p_lora, mu_rkv, mu_rkv, mu_rkv, mu_lora, w0, a0, k_k, k_a, r_k,
      *lora_w, ones_h, tri,
      p, p, p, conv_w, lvec(conv_b), wa, wx, lvec(ba), lvec(bx), lvec(lam), *cast_ws)


def _rwkv_b_body(rp_ref, yp_ref, m_ref, n_ref, bonus_ref, g_ref, lng_ref, lnb_ref, ones_ref,
                 o_ref, state_ref):
    @pl.when(pl.program_id(0) == 0)
    def _():
        state_ref[...] = jnp.zeros_like(state_ref)

    nb = rp_ref.shape[0]
    npairs = state_ref.shape[0] // nb
    units = [(b, q) for b in range(nb) for q in range(npairs)]
    ps = [slice(q * PAIR, (q + 1) * PAIR) for q in range(npairs)]
    left = lax.broadcasted_iota(jnp.int32, (HEAD, PAIR), 1) < HEAD
    ones_h = ones_ref[...]
    inv_n = 1.0 / HEAD
    state = [state_ref[u] for u in range(len(units))]
    for j in range(rp_ref.shape[1] // CHUNK):
        rs = slice(j * CHUNK, (j + 1) * CHUNK)
        ks = slice(j * HEAD, (j + 1) * HEAD)
        g0 = [_pair_diag(s, left) for s in state]
        ys = [_dg(rp_ref[b, rs, ps[q]], g0[u], _NN) + yp_ref[b, rs, ps[q]]
              for u, (b, q) in enumerate(units)]
        state = [_dg(m_ref[b, ks, ps[q]], g0[u], _NN) + n_ref[b, ks, ps[q]]
                 for u, (b, q) in enumerate(units)]
        y = jnp.concatenate([jnp.concatenate(ys[b * npairs:(b + 1) * npairs], axis=1)
                             for b in range(nb)], axis=0)
        yc = y - _head_sums(y, ones_h) * inv_n
        var = _head_sums(yc * yc, ones_h) * inv_n
        yn = yc * lax.rsqrt(var + GN_EPS) * lng_ref[...] + lnb_ref[...]
        for b in range(nb):
            bs = slice(b * CHUNK, (b + 1) * CHUNK)
            o_ref[b, rs, :] = ((yn[bs] + bonus_ref[b, rs, :]) * g_ref[b, rs, :]).astype(BF16)
    for u in range(len(units)):
        state_ref[u] = state[u]


def _rwkv_b(rp, yp, mc, nm, bonus, g, ln_g, ln_b, bsz, seq):
    cl = RWKV_B_CHUNKS * CHUNK
    dr = rp.shape[1]
    nc = seq // cl
    by_seq = lambda t: t.reshape(bsz, -1, dr)
    tile = pl.BlockSpec((bsz, cl, dr), lambda i: (0, i, 0))
    mat = pl.BlockSpec((bsz, RWKV_B_CHUNKS * HEAD, dr), lambda i: (0, i, 0))
    prow = pl.BlockSpec((1, dr), lambda i: (0, 0))
    ones_h = _head_ones(ONES_WIDTH_B)
    out = pl.pallas_call(
        _rwkv_b_body,
        grid=(nc,),
        in_specs=[tile, tile, mat, mat, tile, tile, prow, prow,
                  pl.BlockSpec(ones_h.shape, lambda i: (0, 0))],
        out_specs=tile,
        out_shape=jax.ShapeDtypeStruct((bsz, seq, dr), BF16),
        scratch_shapes=[pltpu.VMEM((bsz * (dr // PAIR), HEAD, PAIR), F32)],
        compiler_params=_params("arbitrary"),
        name="rwkv_b",
    )(by_seq(rp), by_seq(yp), by_seq(mc), by_seq(nm), by_seq(bonus), by_seq(g), ln_g, ln_b, ones_h)
    return out.reshape(bsz * seq, dr)


def _mm_out_body(ya_ref, yb_ref, x_ref, gm_ref, w_ref, g_ref, sh_ref, sc_ref, o_ref, h_ref, *, sub):
    da = ya_ref.shape[1]
    for r0 in range(0, x_ref.shape[0], sub):
        rs = slice(r0, r0 + sub)
        mix = (jnp.dot(ya_ref[rs, :], w_ref[:da, :], preferred_element_type=F32)
               + jnp.dot(yb_ref[rs, :], w_ref[da:, :], preferred_element_type=F32))
        x1 = x_ref[rs, :] + gm_ref[0] * mix
        o_ref[rs, :] = x1
        h_ref[rs, :] = _norm_mod(x1, g_ref[...], sh_ref[0], sc_ref[0]).astype(BF16)


def _mm_out(ya, yb, x2, gm, w, g, sh, sc, seq, tm=512, sub=256):
    m, d = x2.shape
    per_b = seq // tm
    brow = pl.BlockSpec((1, 1, d), lambda i: (i // per_b, 0, 0))
    tile = pl.BlockSpec((tm, d), lambda i: (i, 0))
    return pl.pallas_call(
        functools.partial(_mm_out_body, sub=sub),
        grid=(m // tm,),
        in_specs=[pl.BlockSpec((tm, ya.shape[1]), lambda i: (i, 0)),
                  pl.BlockSpec((tm, yb.shape[1]), lambda i: (i, 0)),
                  tile, brow,
                  pl.BlockSpec(w.shape, lambda i: (0, 0)),
                  pl.BlockSpec((1, d), lambda i: (0, 0)), brow, brow],
        out_specs=[tile, tile],
        out_shape=[jax.ShapeDtypeStruct((m, d), F32), jax.ShapeDtypeStruct((m, d), BF16)],
        compiler_params=_params("parallel"),
        name="mm_out",
    )(ya, yb, x2, gm, w, g, sh, sc)


def _ffn_body(x_ref, h_ref, gf_ref, wg_ref, wu_ref, wd_ref, fg_ref, o_ref, acc_ref):
    f = pl.program_id(1)

    @pl.when(f == 0)
    def _():
        acc_ref[...] = jnp.zeros_like(acc_ref)

    h = h_ref[...]
    gate = jnp.dot(h, wg_ref[...], preferred_element_type=F32)
    up = jnp.dot(h, wu_ref[...], preferred_element_type=F32)
    act = (gate * jax.nn.sigmoid(gate) * up).astype(BF16)
    acc_ref[...] += jnp.dot(act, wd_ref[...], preferred_element_type=F32)

    @pl.when(f == pl.num_programs(1) - 1)
    def _():
        y = x_ref[...] + gf_ref[0] * acc_ref[...]
        o_ref[...] = (y * lax.rsqrt(jnp.mean(y * y, axis=-1, keepdims=True) + RMS_EPS)
                      * fg_ref[...])


def _ffn(x1, h2, gf, w_gu, w_down, fg, seq, tm=512, tf=512):
    m, d = x1.shape
    dff = w_down.shape[0]
    nf = dff // tf
    assert seq % tm == 0 and dff % tf == 0, "row tiles must not straddle sequences"
    per_b = seq // tm
    tile = pl.BlockSpec((tm, d), lambda i, f: (i, 0))
    prow = pl.BlockSpec((1, d), lambda i, f: (0, 0))
    return pl.pallas_call(
        _ffn_body,
        grid=(m // tm, nf),
        in_specs=[tile, tile,
                  pl.BlockSpec((1, 1, d), lambda i, f: (i // per_b, 0, 0)),
                  pl.BlockSpec((d, tf), lambda i, f: (0, f)),
                  pl.BlockSpec((d, tf), lambda i, f: (0, nf + f)),
                  pl.BlockSpec((tf, d), lambda i, f: (f, 0)),
                  prow],
        out_specs=tile,
        out_shape=jax.ShapeDtypeStruct((m, d), F32),
        scratch_shapes=[pltpu.VMEM((tm, d), F32)],
        compiler_params=_params("parallel", "arbitrary"),
        name="ffn",
    )(x1, h2, gf, w_gu, w_gu, w_down, fg)


def _pad_cols(w, n):
    return jnp.pad(w, ((0, 0), (0, n - w.shape[1])))


def _pad_rows(w, n):
    return jnp.pad(w, ((0, n - w.shape[0]), (0, 0)))


def kernel(x, c, w_ada, b_ada, norm_mix_g, w_in, conv_w, conv_b, lru_wa, lru_ba, lru_wx, lru_bx, lru_lambda, rwkv_mu, rwkv_w0, rwkv_w2, rwkv_a0, rwkv_a2, rwkv_g2, rwkv_k_k, rwkv_k_a, rwkv_r_k, rwkv_ln_g, rwkv_ln_b, w_out, norm_ffn_g, w_gu, w_down, final_norm_g):
    bsz, seq, d = x.shape
    depth = w_ada.shape[0]
    assert depth == 1, "the closing RMSNorm is fused into the (single) layer's ffn kernel"
    dl = conv_w.shape[2]
    dr = rwkv_w0.shape[1]
    w_lora, a_lora, g_lora = rwkv_w2.shape[1], rwkv_a2.shape[1], rwkv_g2.shape[1]
    wpad, apad = LANE, LANE
    gpad = -(-g_lora // LANE) * LANE
    rkv_col0 = 2 * dl
    lora0 = rkv_col0 + 3 * dr

    x2 = x.reshape(bsz * seq, d)
    for l in range(depth):
        mod = _mod(c, w_ada[l], b_ada[l].reshape(1, -1))
        sh_m, sc_m, g_m, sh_f, sc_f, g_f = [t.reshape(bsz, 1, d) for t in jnp.split(mod, 6, axis=-1)]

        wi = jnp.swapaxes(w_in[l], 0, 1)
        o1, o2 = lora0 + w_lora, lora0 + w_lora + a_lora
        w_lora_p = jnp.concatenate(
            [_pad_rows(wi[lora0:o1], wpad), _pad_rows(wi[o1:o2], apad),
             _pad_rows(wi[o2:], gpad)], axis=0)
        mu = rwkv_mu[l].reshape(1, -1)
        mu_rkv = mu[:, :3 * dr]
        mu_lora = jnp.concatenate(
            [_pad_cols(mu[:, 3 * dr:3 * dr + w_lora], wpad),
             _pad_cols(mu[:, 3 * dr + w_lora:3 * dr + w_lora + a_lora], apad),
             _pad_cols(mu[:, 3 * dr + w_lora + a_lora:], gpad)], axis=1)
        w2p = _pad_rows(rwkv_w2[l], wpad)
        a2p = _pad_rows(rwkv_a2[l], apad)
        g2p = _pad_rows(rwkv_g2[l], gpad)

        h, p_lora = _norm(x2, norm_mix_g[l].reshape(1, d), sh_m, sc_m, w_lora_p, seq)
        p = _mm_in(h, wi, lora0, gelu_tile=1, tn=dl)

        rowv = lambda t: t.reshape(1, dr)
        rp, yp, mc, nm, bonus, gg, y_a, w_out_b, w_gu_b, w_down_b = _rwkv_a(
            p, p_lora, mu_rkv, mu_lora, rowv(rwkv_w0[l]), rowv(rwkv_a0[l]), rowv(rwkv_k_k[l]),
            rowv(rwkv_k_a[l]), rowv(rwkv_r_k[l]), w2p, a2p, g2p,
            conv_w[l], conv_b[l], lru_wa[l].astype(BF16), lru_wx[l].astype(BF16),
            lru_ba[l], lru_bx[l], lru_lambda[l], bsz, seq, rkv_col0,
            cast_ws=(w_out[l], w_gu[l], w_down[l]))
        y_b = _rwkv_b(rp, yp, mc, nm, bonus, gg, rowv(rwkv_ln_g[l]), rowv(rwkv_ln_b[l]), bsz, seq)

        x2, h2 = _mm_out(y_a, y_b, x2, g_m, w_out_b, norm_ffn_g[l].reshape(1, d), sh_f, sc_f, seq)
        x2 = _ffn(x2, h2, g_f, w_gu_b, w_down_b, final_norm_g.reshape(1, d), seq)
    return x2.reshape(bsz, seq, d)
```

```python
import functools
import math

import jax
import jax.numpy as jnp
from jax import lax
from jax.experimental import pallas as pl
from jax.experimental.pallas import tpu as pltpu

F32 = jnp.float32
BF16 = jnp.bfloat16

LRU_HEADS = 4
CONV_WIDTH = 4
LRU_C = 8.0
HEAD = 64
CHUNK = 64
PAIR = 2 * HEAD
HEADS_PER_STEP = 16
ONES_WIDTH = 256
ONES_WIDTH_B = 128
CHUNKS_PER_STEP = 4
RWKV_B_CHUNKS = 4
MOD_DMA_BANDS = 4
RMS_EPS = 1e-6
GN_EPS = 64e-5
L2_EPS = 1e-12
DECAY_SCALE = -math.exp(-0.5)
LANE = 128
SUBLANE = 8
BF16_SUBLANE = 16
VMEM_LIMIT = 56 * 1024 * 1024


def _params(*sem):
    return pltpu.CompilerParams(dimension_semantics=sem, vmem_limit_bytes=VMEM_LIMIT)


_NN = (((1,), (0,)), ((), ()))
_NT = (((1,), (1,)), ((), ()))
_TN = (((0,), (0,)), ((), ()))


def _dg(a, b, dims):
    return lax.dot_general(a, b, dims, preferred_element_type=F32)


def _split(x):
    hi = x.astype(BF16)
    lo = (x - hi.astype(F32)).astype(BF16)
    return hi, lo


def _mm3(a, b, dims=_NN):
    ah, al = _split(a)
    bh, bl = _split(b)
    return _dg(ah, bh, dims) + (_dg(ah, bl, dims) + _dg(al, bh, dims))


def _head_sums(x, ones_h):
    n = ones_h.shape[0]
    xb = x.astype(BF16)
    return jnp.concatenate([_dg(xb[:, c:c + n], ones_h, _NN) for c in range(0, x.shape[1], n)],
                           axis=1)


def _mm2_exact_lhs(a_bf16, b):
    bh, bl = _split(b)
    return _dg(a_bf16, bh, _NN) + _dg(a_bf16, bl, _NN)


def _softplus(x):
    return jnp.maximum(x, 0.0) + jnp.log1p(jnp.exp(-jnp.abs(x)))


def _iota2(shape):
    return (lax.broadcasted_iota(jnp.int32, shape, 0),
            lax.broadcasted_iota(jnp.int32, shape, 1))


def _head_ones(n):
    r, c = _iota2((n, n))
    return jnp.where((r // HEAD) == (c // HEAD), 1.0, 0.0).astype(BF16)


def _mod_body(c_ref, *refs):
    w_refs, b_ref, o_ref = refs[:-2], refs[-2], refs[-1]
    c = c_ref[...]
    ca = c * jax.nn.sigmoid(c)
    kb = w_refs[0].shape[0]
    acc = b_ref[...]
    for s, w_ref in enumerate(w_refs):
        acc = acc + _mm3(ca[:, s * kb:(s + 1) * kb], w_ref[...])
    o_ref[...] = acc


def _mod(c, w, b, tn=1024, bands=MOD_DMA_BANDS):
    bsz, d = c.shape
    n = w.shape[1]
    kb = d // bands
    return pl.pallas_call(
        _mod_body,
        grid=(n // tn,),
        in_specs=[pl.BlockSpec((bsz, d), lambda j: (0, 0))]
                 + [pl.BlockSpec((kb, tn), lambda j, s=s: (s, j)) for s in range(bands)]
                 + [pl.BlockSpec((1, tn), lambda j: (0, j))],
        out_specs=pl.BlockSpec((bsz, tn), lambda j: (0, j)),
        out_shape=jax.ShapeDtypeStruct((bsz, n), F32),
        compiler_params=_params("parallel"),
        name="mod",
    )(c, *([w] * bands), b)


def _norm_mod(x, g, sh, sc):
    y = x * lax.rsqrt(jnp.mean(x * x, axis=-1, keepdims=True) + RMS_EPS) * g
    return y * (1.0 + sc) + sh


def _norm_body(x_ref, g_ref, sh_ref, sc_ref, w_ref, o_ref, pl_ref, wb_ref):
    @pl.when(pl.program_id(0) == 0)
    def _():
        wb_ref[...] = w_ref[...].astype(BF16)

    h = _norm_mod(x_ref[...], g_ref[...], sh_ref[0], sc_ref[0]).astype(BF16)
    o_ref[...] = h
    pl_ref[...] = _dg(h, wb_ref[...], _NT)


def _norm(x2, g, sh, sc, w_lora_t, seq, tm=512):
    m, d = x2.shape
    nl = w_lora_t.shape[0]
    per_b = seq // tm
    return pl.pallas_call(
        _norm_body,
        grid=(m // tm,),
        in_specs=[pl.BlockSpec((tm, d), lambda i: (i, 0)),
                  pl.BlockSpec((1, d), lambda i: (0, 0)),
                  pl.BlockSpec((1, 1, d), lambda i: (i // per_b, 0, 0)),
                  pl.BlockSpec((1, 1, d), lambda i: (i // per_b, 0, 0)),
                  pl.BlockSpec((nl, d), lambda i: (0, 0))],
        out_specs=[pl.BlockSpec((tm, d), lambda i: (i, 0)), pl.BlockSpec((tm, nl), lambda i: (i, 0))],
        out_shape=[jax.ShapeDtypeStruct((m, d), BF16), jax.ShapeDtypeStruct((m, nl), F32)],
        scratch_shapes=[pltpu.VMEM((nl, d), BF16)],
        compiler_params=_params("arbitrary"),
        name="norm_mix",
    )(x2, g, sh, sc, w_lora_t)


def _mm_in_body(h_ref, w_ref, o_ref, wb_ref, *, gelu_tile):
    @pl.when(pl.program_id(1) == 0)
    def _():
        wb_ref[...] = w_ref[...].astype(BF16)

    @pl.when(pl.program_id(0) == gelu_tile)
    def _():
        o_ref[...] = jax.nn.gelu(_dg(h_ref[...], wb_ref[...], _NT))

    @pl.when(pl.program_id(0) != gelu_tile)
    def _():
        o_ref[...] = _dg(h_ref[...], wb_ref[...], _NT)


def _mm_in(h, wt, ncols, gelu_tile, tm=1024, tn=1024):
    m, d = h.shape
    return pl.pallas_call(
        functools.partial(_mm_in_body, gelu_tile=gelu_tile),
        grid=(ncols // tn, m // tm),
        in_specs=[pl.BlockSpec((tm, d), lambda j, i: (i, 0)),
                  pl.BlockSpec((tn, d), lambda j, i: (j, 0))],
        out_specs=pl.BlockSpec((tm, tn), lambda j, i: (i, j)),
        out_shape=jax.ShapeDtypeStruct((m, ncols), F32),
        scratch_shapes=[pltpu.VMEM((tn, d), BF16)],
        compiler_params=_params("parallel", "arbitrary"),
        name="mm_in",
    )(h, wt)


def _lru_head(h, first, u_ref, gate_ref, halo_ref, cw_ref, cb_ref, wa_ref, wx_ref, ba_ref, bx_ref,
              lam_ref, o_ref, carry_ref):
    tt = u_ref.shape[0]
    hd = u_ref.shape[1] // LRU_HEADS
    cs = slice(h * hd, (h + 1) * hd)
    p = u_ref[:, cs]
    halo = jnp.where(first, 0.0, halo_ref[:, cs])
    ext = jnp.concatenate([halo, p], axis=0)
    cw = cw_ref[:, cs]
    u = cb_ref[:, cs] + p * cw[CONV_WIDTH - 1:CONV_WIDTH, :]
    for j in range(1, CONV_WIDTH):
        shifted = pltpu.roll(ext, j, 0)[SUBLANE:, :]
        u = u + shifted * cw[CONV_WIDTH - 1 - j:CONV_WIDTH - j, :]
    ub = u.astype(BF16)
    ra = jnp.dot(ub, wa_ref[h], preferred_element_type=F32)
    rx = jnp.dot(ub, wx_ref[h], preferred_element_type=F32)
    yield
    r = jax.nn.sigmoid(ra + ba_ref[:, cs])
    ig = jax.nn.sigmoid(rx + bx_ref[:, cs])
    a = jnp.exp(r * ((-LRU_C) * _softplus(-lam_ref[:, cs])))
    mult = jnp.sqrt(1.0 - a * a)
    row = lax.broadcasted_iota(jnp.int32, (tt, hd), 0)
    mult = jnp.where(jnp.logical_and(first, row == 0), 1.0, mult)
    b = mult * (ig * u)

    groups = tt // SUBLANE
    a3 = a.reshape(groups, SUBLANE, hd)
    b3 = b.reshape(groups, SUBLANE, hd)
    sub = lax.broadcasted_iota(jnp.int32, (groups, SUBLANE, hd), 1)
    s = 1
    while s < SUBLANE:
        keep = sub >= s
        a_s = jnp.where(keep, pltpu.roll(a3, s, 1), 1.0)
        b_s = jnp.where(keep, pltpu.roll(b3, s, 1), 0.0)
        b3 = a3 * b_s + b3
        a3 = a3 * a_s
        s *= 2
    yield
    gate = gate_ref[:, cs]
    carry = carry_ref[:, cs]
    outs = []
    for g in range(groups):
        hh = b3[g] + a3[g] * carry
        carry = hh[SUBLANE - 1:SUBLANE, :]
        outs.append(hh * gate[g * SUBLANE:(g + 1) * SUBLANE, :])
    per = BF16_SUBLANE // SUBLANE
    for t0 in range(0, groups, per):
        o_ref[t0 * SUBLANE:(t0 + per) * SUBLANE, cs] = jnp.concatenate(
            outs[t0:t0 + per], axis=0).astype(BF16)
    carry_ref[:, cs] = carry
    yield


def _token_shift(x, halo, mu, first, row):
    prev = jnp.where(first, 0.0, halo[SUBLANE - 1:SUBLANE, :])
    xs = jnp.where(row == 0, prev, pltpu.roll(x, 1, 0))
    return x + (xs - x) * mu


def _pair_diag(y, left):
    return jnp.concatenate([jnp.where(left, y, 0.0), jnp.where(left, 0.0, y)], axis=0).astype(BF16)


def _pair_mm(x, y, left):
    return _dg(x.astype(BF16), _pair_diag(y, left), _NN)


def _chunk_chain(ops, store):
    ab_, bb_, kb_, rb_, v_, bt_, kt_, pe_ = ops
    rc, lane = _iota2((CHUNK, PAIR))
    cc = lane % HEAD
    left = lane < HEAD
    strict = rc > cc
    incl = rc >= cc
    diag = rc == cc
    ar16 = [jnp.concatenate([x, y], axis=0).astype(BF16) for x, y in zip(ab_, rb_)]
    bd_b = [_pair_diag(x, left) for x in bb_]
    bd_k = [_pair_diag(x, left) for x in kb_]
    bd_v = [_pair_diag(x, left) for x in v_]
    arbk = [_dg(x, jnp.concatenate([y, z], axis=0), _NT) for x, y, z in zip(ar16, bd_b, bd_k)]
    arb = [x[:, :PAIR] for x in arbk]
    ark = [x[:, PAIR:] for x in arbk]
    a_ab = [jnp.where(strict, x[:CHUNK], 0.0) for x in arb]
    a_rb = [jnp.where(incl, x[CHUNK:], 0.0).astype(BF16) for x in arb]
    a_akrk = [jnp.concatenate([jnp.where(strict, x[:CHUNK], 0.0), jnp.where(incl, x[CHUNK:], 0.0)],
                              axis=0).astype(BF16) for x in ark]
    yield
    base = 8
    d = [jnp.where((rc // base) == (cc // base), a, 0.0) for a in a_ab]
    d2 = [_pair_mm(t, t, left) for t in d]
    akrkv = [_dg(x, y, _NN) for x, y in zip(a_akrk, bd_v)]
    akv = [x[:CHUNK] for x in akrkv]
    rkv = [x[CHUNK:] for x in akrkv]
    x = [jnp.where(diag, 1.0, 0.0) + t for t in d]
    yield
    xd = [_dg(t2.astype(BF16),
              jnp.concatenate([_pair_diag(xi, left), _pair_diag(t2, left)], axis=1), _NN)
          for xi, t2 in zip(x, d2)]
    x = [xi + y[:, :PAIR] for xi, y in zip(x, xd)]
    d4 = [y[:, PAIR:] for y in xd]
    yield
    x = [xi + _pair_mm(t4, xi, left) for xi, t4 in zip(x, d4)]
    yield
    size = base
    while size < CHUNK:
        off = jnp.logical_and((rc // (2 * size)) == (cc // (2 * size)),
                              (rc // size) != (cc // size))
        o = [jnp.where(off, a, 0.0) for a in a_ab]
        ox = [_pair_mm(oi, xi, left) for oi, xi in zip(o, x)]
        yield
        x = [xi + _pair_mm(xi, oxi, left) for xi, oxi in zip(x, ox)]
        yield
        size *= 2
    t = [xi.astype(BF16) for xi in x]
    wu = [_dg(ti, jnp.concatenate([_pair_diag(y, left), _pair_diag(z, left)], axis=1), _NN)
          for ti, y, z in zip(t, ab_, akv)]
    kv = [_dg(xi.astype(BF16), y.astype(BF16), _TN) for xi, y in zip(kt_, v_)]
    yield
    ry = [_dg(xi, jnp.concatenate([_pair_diag(y[:, :PAIR], left), _pair_diag(y[:, PAIR:], left)], axis=1), _NN)
          for xi, y in zip(a_rb, wu)]
    mn = [_dg(xi.astype(BF16), y.astype(BF16), _TN) for xi, y in zip(bt_, wu)]
    yield

    def head_blocks(z):
        return jnp.where(left, z[:HEAD, :], z[HEAD:, :])

    for u in range(len(ab_)):
        store(u,
              rb_[u] + ry[u][:, :PAIR],
              ry[u][:, PAIR:] + rkv[u],
              jnp.where(diag, pe_[u], 0.0) + head_blocks(mn[u][:, :PAIR]),
              head_blocks(mn[u][:, PAIR:]) + head_blocks(kv[u]))


def _rwkv_a_body(r_ref, k_ref, v_ref, l_ref, rh_ref, kh_ref, vh_ref, lh_ref,
                 mur_ref, muk_ref, muv_ref, mul_ref, w0_ref, a0_ref, kkw_ref, kaw_ref, rkw_ref,
                 w2_ref, a2_ref, g2_ref, ones_ref, tri_ref,
                 u_ref, gate_ref, halo_ref, cw_ref, cb_ref, wa_ref, wx_ref, ba_ref, bx_ref, lam_ref,
                 *rest):
    ncast = (len(rest) - 8) // 2
    cast_in = rest[:ncast]
    rp_ref, yp_ref, m_ref, n_ref, bonus_ref, g_ref, ya_ref = rest[ncast:ncast + 7]
    cast_out = rest[ncast + 7:-1]
    carry_ref = rest[-1]
    first = pl.program_id(1) == 0

    @pl.when(first)
    def _():
        carry_ref[...] = jnp.zeros_like(carry_ref)
    cl = CHUNK
    rows = CHUNKS_PER_STEP * cl
    width = HEADS_PER_STEP * HEAD
    gw = ones_ref.shape[0]
    row_g = lax.broadcasted_iota(jnp.int32, (rows, gw), 0)
    row_l = lax.broadcasted_iota(jnp.int32, (rows, l_ref.shape[1]), 0)
    ones_h = ones_ref[...]

    lo = _token_shift(l_ref[...], lh_ref[...], mul_ref[...], first, row_l)
    act_w = _split(jnp.tanh(lo[:, 0:LANE]))
    act_a = _split(lo[:, LANE:2 * LANE])
    act_g = _split(jax.nn.sigmoid(lo[:, 2 * LANE:]))

    def lora(act, w_ref, cs, keep_low):
        (ah, al_), wb = act, w_ref[:, cs]
        out = _dg(ah, wb, _NN)
        return out + _dg(al_, wb, _NN) if keep_low else out

    def prologue(c0, out):
        cs = slice(c0, c0 + gw)
        r = _token_shift(r_ref[:, cs], rh_ref[:, cs], mur_ref[:, cs], first, row_g)
        k = _token_shift(k_ref[:, cs], kh_ref[:, cs], muk_ref[:, cs], first, row_g)
        v = _token_shift(v_ref[:, cs], vh_ref[:, cs], muv_ref[:, cs], first, row_g)
        w_lin = w0_ref[:, cs] + lora(act_w, w2_ref, cs, True)
        a_lin = a0_ref[:, cs] + lora(act_a, a2_ref, cs, False)
        g_ref[:, cs] = lora(act_g, g2_ref, cs, False).astype(BF16)
        kk = k * kkw_ref[:, cs]
        kk_ss = _head_sums(kk * kk, ones_h)
        yield
        lw = DECAY_SCALE * jax.nn.sigmoid(w_lin)
        a = jax.nn.sigmoid(a_lin)
        kk = kk * lax.rsqrt(jnp.maximum(kk_ss, L2_EPS * L2_EPS))
        kp = k * (1.0 + (a - 1.0) * kaw_ref[:, cs])
        bonus_ref[:, cs] = (_head_sums(r * kp * rkw_ref[:, cs], ones_h) * v).astype(BF16)
        lc = _mm2_exact_lhs(tri_ref[...], lw)
        yield
        p_incl = jnp.exp(lc)
        p_excl = jnp.exp(lc - lw)
        p_inv = 1.0 / p_incl
        p_end = jnp.concatenate(
            [jnp.broadcast_to(p_incl[(j + 1) * cl - 1:(j + 1) * cl, :], (cl, gw))
             for j in range(CHUNKS_PER_STEP)], axis=0)
        abar = -(kk * p_excl)
        bbar = kk * a * p_inv
        kbar = kp * p_inv
        rbar = r * p_incl
        btil = bbar * p_end
        ktil = kbar * p_end
        units = [(j, q) for j in range(CHUNKS_PER_STEP) for q in range(gw // PAIR)]
        out.extend([x[j * cl:(j + 1) * cl, q * PAIR:(q + 1) * PAIR] for j, q in units]
                   for x in (abar, bbar, kbar, rbar, v, btil, ktil, p_end))
        yield

    def make_store(c0):
        units = [(j, q) for j in range(CHUNKS_PER_STEP) for q in range(gw // PAIR)]

        def store(u, rp, yp, mm, nn):
            j, q = units[u]
            rs = slice(j * cl, (j + 1) * cl)
            qs = slice(c0 + q * PAIR, c0 + (q + 1) * PAIR)
            rp_ref[rs, qs] = rp.astype(BF16)
            yp_ref[rs, qs] = yp.astype(BF16)
            m_ref[rs, qs] = mm.astype(BF16)
            n_ref[rs, qs] = nn.astype(BF16)
        return store

    lru = (None for h in range(LRU_HEADS)
           for _ in _lru_head(h, first, u_ref, gate_ref, halo_ref, cw_ref, cb_ref, wa_ref, wx_ref,
                              ba_ref, bx_ref, lam_ref, ya_ref, carry_ref))
    chains = []
    for c0 in range(0, width, gw):
        ops = []
        for _ in prologue(c0, ops):
            for ch in chains:
                next(ch, None)
        chains.append(_chunk_chain(ops, make_store(c0)))
    live = list(chains)
    while live:
        live = [ch for ch in live if next(ch, StopIteration) is not StopIteration]
        next(lru, None)
    for _ in lru:
        pass

    for src, dst in zip(cast_in, cast_out):
        dst[...] = src[...].astype(BF16)


def _rwkv_a(p, p_lora, mu_rkv, mu_lora, w0, a0, k_k, k_a, r_k, w2p, a2p, g2p,
            conv_w, conv_b, wa, wx, ba, bx, lam, bsz, seq, rkv_col0, cast_ws=()):
    dl = conv_w.shape[1]
    lvec = lambda t: t.reshape(1, dl)
    lrow = pl.BlockSpec((1, dl), lambda b, i, q: (0, 0))

    def lru_tile(cb):
        return pl.BlockSpec((CHUNKS_PER_STEP * CHUNK, dl), lambda b, i, q: (b * nc + i, cb))
    cl = CHUNKS_PER_STEP * CHUNK
    width = HEADS_PER_STEP * HEAD
    dr = w0.shape[1]
    ngroups = dr // width
    assert ngroups == 1, "the LRU ride-along expects one grid step per row tile"
    nc = seq // cl
    lw_ = mu_lora.shape[1]
    cb0 = rkv_col0 // width
    rows8 = cl // SUBLANE
    rt, ct = _iota2((cl, cl))
    tri = jnp.where(jnp.logical_and(rt >= ct, (rt // CHUNK) == (ct // CHUNK)), 1.0, 0.0).astype(BF16)
    ones_h = _head_ones(ONES_WIDTH)
    const = lambda arr: pl.BlockSpec(arr.shape, lambda b, i, q: (0, 0))
    lora_w = [wgt.astype(BF16) for wgt in (w2p, a2p, g2p)]

    def tile(cb_off):
        return pl.BlockSpec((cl, width), lambda b, i, q: (b * nc + i, cb0 + cb_off + q))

    def halo(cb_off):
        return pl.BlockSpec(
            (SUBLANE, width),
            lambda b, i, q: (jnp.maximum((b * nc + i) * rows8 - 1, 0), cb0 + cb_off + q))

    def prow(off=0):
        return pl.BlockSpec((1, width), lambda b, i, q: (0, off + q))

    out_tile = pl.BlockSpec((cl, width), lambda b, i, q: (b * nc + i, q))
    out_mat = pl.BlockSpec((CHUNKS_PER_STEP * HEAD, width), lambda b, i, q: (b * nc + i, q))
    act = jax.ShapeDtypeStruct((bsz * seq, dr), BF16)
    mat = jax.ShapeDtypeStruct((bsz * (seq // CHUNK) * HEAD, dr), BF16)

    nsteps = bsz * nc * ngroups
    cast_specs = []
    for wgt in cast_ws:
        hold = 1
        while (wgt.shape[0] * hold) % (nsteps * BF16_SUBLANE) != 0:
            hold *= 2
        blk = (wgt.shape[0] * hold // nsteps, wgt.shape[1])
        cast_specs.append(pl.BlockSpec(
            blk, lambda b, i, q, hold=hold: (((b * nc + i) * ngroups + q) // hold, 0)))
    cast_shapes = [jax.ShapeDtypeStruct(wgt.shape, BF16) for wgt in cast_ws]

    return pl.pallas_call(
        _rwkv_a_body,
        grid=(bsz, nc, ngroups),
        in_specs=[tile(0), tile(ngroups), tile(2 * ngroups),
                  pl.BlockSpec((cl, lw_), lambda b, i, q: (b * nc + i, 0)),
                  halo(0), halo(ngroups), halo(2 * ngroups),
                  pl.BlockSpec((SUBLANE, lw_),
                               lambda b, i, q: (jnp.maximum((b * nc + i) * rows8 - 1, 0), 0)),
                  prow(0), prow(ngroups), prow(2 * ngroups),
                  pl.BlockSpec((1, lw_), lambda b, i, q: (0, 0)),
                  prow(), prow(), prow(), prow(), prow()]
                 + [pl.BlockSpec((t.shape[0], width), lambda b, i, q: (0, q)) for t in lora_w]
                 + [const(ones_h), const(tri)]
                 + [lru_tile(0), lru_tile(1),
                    pl.BlockSpec((SUBLANE, dl),
                                 lambda b, i, q: (jnp.maximum((b * nc + i) * rows8 - 1, 0), 0)),
                    const(conv_w), lrow, pl.BlockSpec(wa.shape, lambda b, i, q: (0, 0, 0)),
                    pl.BlockSpec(wx.shape, lambda b, i, q: (0, 0, 0)), lrow, lrow, lrow]
                 + cast_specs,
        out_specs=[out_tile, out_tile, out_mat, out_mat, out_tile, out_tile, lru_tile(0)] + cast_specs,
        out_shape=[act, act, mat, mat, act, act, jax.ShapeDtypeStruct((bsz * seq, dl), BF16)]
                  + cast_shapes,
        scratch_shapes=[pltpu.VMEM((1, dl), F32)],
        compiler_params=_params("arbitrary", "arbitrary", "arbitrary"),
        name="rwkv_a",
    )(p, p, p, p_lora, p, p, p, p_lora, mu_rkv, mu_rkv, mu_rkv, mu_lora, w0, a0, k_k, k_a, r_k,
      *lora_w, ones_h, tri,
      p, p, p, conv_w, lvec(conv_b), wa, wx, lvec(ba), lvec(bx), lvec(lam), *cast_ws)


def _rwkv_b_body(rp_ref, yp_ref, m_ref, n_ref, bonus_ref, g_ref, lng_ref, lnb_ref, ones_ref,
                 o_ref, state_ref):
    @pl.when(pl.program_id(0) == 0)
    def _():
        state_ref[...] = jnp.zeros_like(state_ref)

    nb = rp_ref.shape[0]
    npairs = state_ref.shape[0] // nb
    units = [(b, q) for b in range(nb) for q in range(npairs)]
    ps = [slice(q * PAIR, (q + 1) * PAIR) for q in range(npairs)]
    left = lax.broadcasted_iota(jnp.int32, (HEAD, PAIR), 1) < HEAD
    ones_h = ones_ref[...]
    inv_n = 1.0 / HEAD
    state = [state_ref[u] for u in range(len(units))]
    for j in range(rp_ref.shape[1] // CHUNK):
        rs = slice(j * CHUNK, (j + 1) * CHUNK)
        ks = slice(j * HEAD, (j + 1) * HEAD)
        g0 = [_pair_diag(s, left) for s in state]
        rm = [_dg(jnp.concatenate([rp_ref[b, rs, ps[q]], m_ref[b, ks, ps[q]]], axis=0), g0[u], _NN)
              for u, (b, q) in enumerate(units)]
        ys = [rm[u][:CHUNK] + yp_ref[b, rs, ps[q]] for u, (b, q) in enumerate(units)]
        state = [rm[u][CHUNK:] + n_ref[b, ks, ps[q]] for u, (b, q) in enumerate(units)]
        y = jnp.concatenate([jnp.concatenate(ys[b * npairs:(b + 1) * npairs], axis=1)
                             for b in range(nb)], axis=0)
        yc = y - _head_sums(y, ones_h) * inv_n
        var = _head_sums(yc * yc, ones_h) * inv_n
        yn = yc * lax.rsqrt(var + GN_EPS) * lng_ref[...] + lnb_ref[...]
        for b in range(nb):
            bs = slice(b * CHUNK, (b + 1) * CHUNK)
            o_ref[b, rs, :] = ((yn[bs] + bonus_ref[b, rs, :]) * g_ref[b, rs, :]).astype(BF16)
    for u in range(len(units)):
        state_ref[u] = state[u]


def _rwkv_b(rp, yp, mc, nm, bonus, g, ln_g, ln_b, bsz, seq):
    cl = RWKV_B_CHUNKS * CHUNK
    dr = rp.shape[1]
    nc = seq // cl
    by_seq = lambda t: t.reshape(bsz, -1, dr)
    tile = pl.BlockSpec((bsz, cl, dr), lambda i: (0, i, 0))
    mat = pl.BlockSpec((bsz, RWKV_B_CHUNKS * HEAD, dr), lambda i: (0, i, 0))
    prow = pl.BlockSpec((1, dr), lambda i: (0, 0))
    ones_h = _head_ones(ONES_WIDTH_B)
    out = pl.pallas_call(
        _rwkv_b_body,
        grid=(nc,),
        in_specs=[tile, tile, mat, mat, tile, tile, prow, prow,
                  pl.BlockSpec(ones_h.shape, lambda i: (0, 0))],
        out_specs=tile,
        out_shape=jax.ShapeDtypeStruct((bsz, seq, dr), BF16),
        scratch_shapes=[pltpu.VMEM((bsz * (dr // PAIR), HEAD, PAIR), F32)],
        compiler_params=_params("arbitrary"),
        name="rwkv_b",
    )(by_seq(rp), by_seq(yp), by_seq(mc), by_seq(nm), by_seq(bonus), by_seq(g), ln_g, ln_b, ones_h)
    return out.reshape(bsz * seq, dr)


def _mm_out_body(ya_ref, yb_ref, x_ref, gm_ref, w_ref, g_ref, sh_ref, sc_ref, o_ref, h_ref, *, sub):
    da = ya_ref.shape[1]
    for r0 in range(0, x_ref.shape[0], sub):
        rs = slice(r0, r0 + sub)
        mix = (jnp.dot(ya_ref[rs, :], w_ref[:da, :], preferred_element_type=F32)
               + jnp.dot(yb_ref[rs, :], w_ref[da:, :], preferred_element_type=F32))
        x1 = x_ref[rs, :] + gm_ref[0] * mix
        o_ref[rs, :] = x1
        h_ref[rs, :] = _norm_mod(x1, g_ref[...], sh_ref[0], sc_ref[0]).astype(BF16)


def _mm_out(ya, yb, x2, gm, w, g, sh, sc, seq, tm=512, sub=256):
    m, d = x2.shape
    per_b = seq // tm
    brow = pl.BlockSpec((1, 1, d), lambda i: (i // per_b, 0, 0))
    tile = pl.BlockSpec((tm, d), lambda i: (i, 0))
    return pl.pallas_call(
        functools.partial(_mm_out_body, sub=sub),
        grid=(m // tm,),
        in_specs=[pl.BlockSpec((tm, ya.shape[1]), lambda i: (i, 0)),
                  pl.BlockSpec((tm, yb.shape[1]), lambda i: (i, 0)),
                  tile, brow,
                  pl.BlockSpec(w.shape, lambda i: (0, 0)),
                  pl.BlockSpec((1, d), lambda i: (0, 0)), brow, brow],
        out_specs=[tile, tile],
        out_shape=[jax.ShapeDtypeStruct((m, d), F32), jax.ShapeDtypeStruct((m, d), BF16)],
        compiler_params=_params("parallel"),
        name="mm_out",
    )(ya, yb, x2, gm, w, g, sh, sc)


def _ffn_body(x_ref, h_ref, gf_ref, wg_ref, wu_ref, wd_ref, fg_ref, o_ref, acc_ref):
    f = pl.program_id(1)

    @pl.when(f == 0)
    def _():
        acc_ref[...] = jnp.zeros_like(acc_ref)

    h = h_ref[...]
    gate = jnp.dot(h, wg_ref[...], preferred_element_type=F32)
    up = jnp.dot(h, wu_ref[...], preferred_element_type=F32)
    act = (gate * jax.nn.sigmoid(gate) * up).astype(BF16)
    acc_ref[...] += jnp.dot(act, wd_ref[...], preferred_element_type=F32)

    @pl.when(f == pl.num_programs(1) - 1)
    def _():
        y = x_ref[...] + gf_ref[0] * acc_ref[...]
        o_ref[...] = (y * lax.rsqrt(jnp.mean(y * y, axis=-1, keepdims=True) + RMS_EPS)
                      * fg_ref[...])


def _ffn(x1, h2, gf, w_gu, w_down, fg, seq, tm=512, tf=512):
    m, d = x1.shape
    dff = w_down.shape[0]
    nf = dff // tf
    assert seq % tm == 0 and dff % tf == 0, "row tiles must not straddle sequences"
    per_b = seq // tm
    tile = pl.BlockSpec((tm, d), lambda i, f: (i, 0))
    prow = pl.BlockSpec((1, d), lambda i, f: (0, 0))
    return pl.pallas_call(
        _ffn_body,
        grid=(m // tm, nf),
        in_specs=[tile, tile,
                  pl.BlockSpec((1, 1, d), lambda i, f: (i // per_b, 0, 0)),
                  pl.BlockSpec((d, tf), lambda i, f: (0, f)),
                  pl.BlockSpec((d, tf), lambda i, f: (0, nf + f)),
                  pl.BlockSpec((tf, d), lambda i, f: (f, 0)),
                  prow],
        out_specs=tile,
        out_shape=jax.ShapeDtypeStruct((m, d), F32),
        scratch_shapes=[pltpu.VMEM((tm, d), F32)],
        compiler_params=_params("parallel", "arbitrary"),
        name="ffn",
    )(x1, h2, gf, w_gu, w_gu, w_down, fg)


def _pad_cols(w, n):
    return jnp.pad(w, ((0, 0), (0, n - w.shape[1])))


def _pad_rows(w, n):
    return jnp.pad(w, ((0, n - w.shape[0]), (0, 0)))


def kernel(x, c, w_ada, b_ada, norm_mix_g, w_in, conv_w, conv_b, lru_wa, lru_ba, lru_wx, lru_bx, lru_lambda, rwkv_mu, rwkv_w0, rwkv_w2, rwkv_a0, rwkv_a2, rwkv_g2, rwkv_k_k, rwkv_k_a, rwkv_r_k, rwkv_ln_g, rwkv_ln_b, w_out, norm_ffn_g, w_gu, w_down, final_norm_g):
    bsz, seq, d = x.shape
    depth = w_ada.shape[0]
    assert depth == 1, "the closing RMSNorm is fused into the (single) layer's ffn kernel"
    dl = conv_w.shape[2]
    dr = rwkv_w0.shape[1]
    w_lora, a_lora, g_lora = rwkv_w2.shape[1], rwkv_a2.shape[1], rwkv_g2.shape[1]
    wpad, apad = LANE, LANE
    gpad = -(-g_lora // LANE) * LANE
    rkv_col0 = 2 * dl
    lora0 = rkv_col0 + 3 * dr

    x2 = x.reshape(bsz * seq, d)
    for l in range(depth):
        mod = _mod(c, w_ada[l], b_ada[l].reshape(1, -1))
        sh_m, sc_m, g_m, sh_f, sc_f, g_f = [t.reshape(bsz, 1, d) for t in jnp.split(mod, 6, axis=-1)]

        wi = jnp.swapaxes(w_in[l], 0, 1)
        o1, o2 = lora0 + w_lora, lora0 + w_lora + a_lora
        w_lora_p = jnp.concatenate(
            [_pad_rows(wi[lora0:o1], wpad), _pad_rows(wi[o1:o2], apad),
             _pad_rows(wi[o2:], gpad)], axis=0)
        mu = rwkv_mu[l].reshape(1, -1)
        mu_rkv = mu[:, :3 * dr]
        mu_lora = jnp.concatenate(
            [_pad_cols(mu[:, 3 * dr:3 * dr + w_lora], wpad),
             _pad_cols(mu[:, 3 * dr + w_lora:3 * dr + w_lora + a_lora], apad),
             _pad_cols(mu[:, 3 * dr + w_lora + a_lora:], gpad)], axis=1)
        w2p = _pad_rows(rwkv_w2[l], wpad)
        a2p = _pad_rows(rwkv_a2[l], apad)
        g2p = _pad_rows(rwkv_g2[l], gpad)

        h, p_lora = _norm(x2, norm_mix_g[l].reshape(1, d), sh_m, sc_m, w_lora_p, seq)
        p = _mm_in(h, wi, lora0, gelu_tile=1, tn=dl)

        rowv = lambda t: t.reshape(1, dr)
        rp, yp, mc, nm, bonus, gg, y_a, w_out_b, w_gu_b, w_down_b = _rwkv_a(
            p, p_lora, mu_rkv, mu_lora, rowv(rwkv_w0[l]), rowv(rwkv_a0[l]), rowv(rwkv_k_k[l]),
            rowv(rwkv_k_a[l]), rowv(rwkv_r_k[l]), w2p, a2p, g2p,
            conv_w[l], conv_b[l], lru_wa[l].astype(BF16), lru_wx[l].astype(BF16),
            lru_ba[l], lru_bx[l], lru_lambda[l], bsz, seq, rkv_col0,
            cast_ws=(w_out[l], w_gu[l], w_down[l]))
        y_b = _rwkv_b(rp, yp, mc, nm, bonus, gg, rowv(rwkv_ln_g[l]), rowv(rwkv_ln_b[l]), bsz, seq)

        x2, h2 = _mm_out(y_a, y_b, x2, g_m, w_out_b, norm_ffn_g[l].reshape(1, d), sh_f, sc_f, seq)
        x2 = _ffn(x2, h2, g_f, w_gu_b, w_down_b, final_norm_g.reshape(1, d), seq)
    return x2.reshape(bsz, seq, d)
```

```python
import functools
import math

import jax
import jax.numpy as jnp
from jax import lax
from jax.experimental import pallas as pl
from jax.experimental.pallas import tpu as pltpu

F32 = jnp.float32
BF16 = jnp.bfloat16

LRU_HEADS = 4
CONV_WIDTH = 4
LRU_C = 8.0
HEAD = 64
CHUNK = 64
PAIR = 2 * HEAD
HEADS_PER_STEP = 16
ONES_WIDTH = 256
ONES_WIDTH_B = 128
CHUNKS_PER_STEP = 4
RWKV_B_CHUNKS = 4
MOD_DMA_BANDS = 4
RMS_EPS = 1e-6
GN_EPS = 64e-5
L2_EPS = 1e-12
DECAY_SCALE = -math.exp(-0.5)
LANE = 128
SUBLANE = 8
BF16_SUBLANE = 16
VMEM_LIMIT = 56 * 1024 * 1024


def _params(*sem):
    return pltpu.CompilerParams(dimension_semantics=sem, vmem_limit_bytes=VMEM_LIMIT)


_NN = (((1,), (0,)), ((), ()))
_NT = (((1,), (1,)), ((), ()))
_TN = (((0,), (0,)), ((), ()))


def _dg(a, b, dims):
    return lax.dot_general(a, b, dims, preferred_element_type=F32)


def _split(x):
    hi = x.astype(BF16)
    lo = (x - hi.astype(F32)).astype(BF16)
    return hi, lo


def _mm3(a, b, dims=_NN):
    ah, al = _split(a)
    bh, bl = _split(b)
    return _dg(ah, bh, dims) + (_dg(ah, bl, dims) + _dg(al, bh, dims))


def _head_sums(x, ones_h):
    n = ones_h.shape[0]
    xb = x.astype(BF16)
    return jnp.concatenate([_dg(xb[:, c:c + n], ones_h, _NN) for c in range(0, x.shape[1], n)],
                           axis=1)


def _mm2_exact_lhs(a_bf16, b):
    bh, bl = _split(b)
    return _dg(a_bf16, bh, _NN) + _dg(a_bf16, bl, _NN)


def _softplus(x):
    return jnp.maximum(x, 0.0) + jnp.log1p(jnp.exp(-jnp.abs(x)))


def _iota2(shape):
    return (lax.broadcasted_iota(jnp.int32, shape, 0),
            lax.broadcasted_iota(jnp.int32, shape, 1))


def _head_ones(n):
    r, c = _iota2((n, n))
    return jnp.where((r // HEAD) == (c // HEAD), 1.0, 0.0).astype(BF16)


def _mod_body(c_ref, *refs):
    w_refs, b_ref, o_ref = refs[:-2], refs[-2], refs[-1]
    c = c_ref[...]
    ca = c * jax.nn.sigmoid(c)
    kb = w_refs[0].shape[0]
    acc = b_ref[...]
    for s, w_ref in enumerate(w_refs):
        acc = acc + _mm3(ca[:, s * kb:(s + 1) * kb], w_ref[...])
    o_ref[...] = acc


def _mod(c, w, b, tn=1024, bands=MOD_DMA_BANDS):
    bsz, d = c.shape
    n = w.shape[1]
    kb = d // bands
    return pl.pallas_call(
        _mod_body,
        grid=(n // tn,),
        in_specs=[pl.BlockSpec((bsz, d), lambda j: (0, 0))]
                 + [pl.BlockSpec((kb, tn), lambda j, s=s: (s, j)) for s in range(bands)]
                 + [pl.BlockSpec((1, tn), lambda j: (0, j))],
        out_specs=pl.BlockSpec((bsz, tn), lambda j: (0, j)),
        out_shape=jax.ShapeDtypeStruct((bsz, n), F32),
        compiler_params=_params("parallel"),
        name="mod",
    )(c, *([w] * bands), b)


def _norm_mod(x, g, sh, sc):
    y = x * lax.rsqrt(jnp.mean(x * x, axis=-1, keepdims=True) + RMS_EPS) * g
    return y * (1.0 + sc) + sh


def _norm_body(x_ref, g_ref, sh_ref, sc_ref, w_ref, o_ref, pl_ref, wb_ref):
    @pl.when(pl.program_id(0) == 0)
    def _():
        wb_ref[...] = w_ref[...].astype(BF16)

    h = _norm_mod(x_ref[...], g_ref[...], sh_ref[0], sc_ref[0]).astype(BF16)
    o_ref[...] = h
    pl_ref[...] = _dg(h, wb_ref[...], _NT)


def _norm(x2, g, sh, sc, w_lora_t, seq, tm=512):
    m, d = x2.shape
    nl = w_lora_t.shape[0]
    per_b = seq // tm
    return pl.pallas_call(
        _norm_body,
        grid=(m // tm,),
        in_specs=[pl.BlockSpec((tm, d), lambda i: (i, 0)),
                  pl.BlockSpec((1, d), lambda i: (0, 0)),
                  pl.BlockSpec((1, 1, d), lambda i: (i // per_b, 0, 0)),
                  pl.BlockSpec((1, 1, d), lambda i: (i // per_b, 0, 0)),
                  pl.BlockSpec((nl, d), lambda i: (0, 0))],
        out_specs=[pl.BlockSpec((tm, d), lambda i: (i, 0)), pl.BlockSpec((tm, nl), lambda i: (i, 0))],
        out_shape=[jax.ShapeDtypeStruct((m, d), BF16), jax.ShapeDtypeStruct((m, nl), F32)],
        scratch_shapes=[pltpu.VMEM((nl, d), BF16)],
        compiler_params=_params("arbitrary"),
        name="norm_mix",
    )(x2, g, sh, sc, w_lora_t)


def _mm_in_body(h_ref, w_ref, o_ref, wb_ref, *, gelu_tile):
    @pl.when(pl.program_id(1) == 0)
    def _():
        wb_ref[...] = w_ref[...].astype(BF16)

    @pl.when(pl.program_id(0) == gelu_tile)
    def _():
        o_ref[...] = jax.nn.gelu(_dg(h_ref[...], wb_ref[...], _NT))

    @pl.when(pl.program_id(0) != gelu_tile)
    def _():
        o_ref[...] = _dg(h_ref[...], wb_ref[...], _NT)


def _mm_in(h, wt, ncols, gelu_tile, tm=1024, tn=1024):
    m, d = h.shape
    return pl.pallas_call(
        functools.partial(_mm_in_body, gelu_tile=gelu_tile),
        grid=(ncols // tn, m // tm),
        in_specs=[pl.BlockSpec((tm, d), lambda j, i: (i, 0)),
                  pl.BlockSpec((tn, d), lambda j, i: (j, 0))],
        out_specs=pl.BlockSpec((tm, tn), lambda j, i: (i, j)),
        out_shape=jax.ShapeDtypeStruct((m, ncols), F32),
        scratch_shapes=[pltpu.VMEM((tn, d), BF16)],
        compiler_params=_params("parallel", "arbitrary"),
        name="mm_in",
    )(h, wt)


def _lru_head(h, first, u_ref, gate_ref, halo_ref, cw_ref, cb_ref, wa_ref, wx_ref, ba_ref, bx_ref,
              lam_ref, o_ref, carry_ref):
    tt = u_ref.shape[0]
    hd = u_ref.shape[1] // LRU_HEADS
    cs = slice(h * hd, (h + 1) * hd)
    p = u_ref[:, cs]
    halo = jnp.where(first, 0.0, halo_ref[:, cs])
    ext = jnp.concatenate([halo, p], axis=0)
    cw = cw_ref[:, cs]
    u = cb_ref[:, cs] + p * cw[CONV_WIDTH - 1:CONV_WIDTH, :]
    for j in range(1, CONV_WIDTH):
        shifted = pltpu.roll(ext, j, 0)[SUBLANE:, :]
        u = u + shifted * cw[CONV_WIDTH - 1 - j:CONV_WIDTH - j, :]
    ub = u.astype(BF16)
    ra = jnp.dot(ub, wa_ref[h], preferred_element_type=F32)
    rx = jnp.dot(ub, wx_ref[h], preferred_element_type=F32)
    yield
    r = jax.nn.sigmoid(ra + ba_ref[:, cs])
    ig = jax.nn.sigmoid(rx + bx_ref[:, cs])
    a = jnp.exp(r * ((-LRU_C) * _softplus(-lam_ref[:, cs])))
    mult = jnp.sqrt(1.0 - a * a)
    row = lax.broadcasted_iota(jnp.int32, (tt, hd), 0)
    mult = jnp.where(jnp.logical_and(first, row == 0), 1.0, mult)
    b = mult * (ig * u)

    groups = tt // SUBLANE
    a3 = a.reshape(groups, SUBLANE, hd)
    b3 = b.reshape(groups, SUBLANE, hd)
    sub = lax.broadcasted_iota(jnp.int32, (groups, SUBLANE, hd), 1)
    s = 1
    while s < SUBLANE:
        keep = sub >= s
        a_s = jnp.where(keep, pltpu.roll(a3, s, 1), 1.0)
        b_s = jnp.where(keep, pltpu.roll(b3, s, 1), 0.0)
        b3 = a3 * b_s + b3
        a3 = a3 * a_s
        s *= 2
    yield
    gate = gate_ref[:, cs]
    carry = carry_ref[:, cs]
    outs = []
    for g in range(groups):
        hh = b3[g] + a3[g] * carry
        carry = hh[SUBLANE - 1:SUBLANE, :]
        outs.append(hh * gate[g * SUBLANE:(g + 1) * SUBLANE, :])
    per = BF16_SUBLANE // SUBLANE
    for t0 in range(0, groups, per):
        o_ref[t0 * SUBLANE:(t0 + per) * SUBLANE, cs] = jnp.concatenate(
            outs[t0:t0 + per], axis=0).astype(BF16)
    carry_ref[:, cs] = carry
    yield


def _token_shift(x, halo, mu, first, row):
    prev = jnp.where(first, 0.0, halo[SUBLANE - 1:SUBLANE, :])
    xs = jnp.where(row == 0, prev, pltpu.roll(x, 1, 0))
    return x + (xs - x) * mu


def _pair_diag(y, left):
    return jnp.concatenate([jnp.where(left, y, 0.0), jnp.where(left, 0.0, y)], axis=0).astype(BF16)


def _pair_mm(x, y, left):
    return _dg(x.astype(BF16), _pair_diag(y, left), _NN)


def _chunk_chain(ops, store):
    ab_, bb_, kb_, rb_, v_, bt_, kt_, pe_ = ops
    rc, lane = _iota2((CHUNK, PAIR))
    cc = lane % HEAD
    left = lane < HEAD
    strict = rc > cc
    incl = rc >= cc
    diag = rc == cc
    ar16 = [jnp.concatenate([x, y], axis=0).astype(BF16) for x, y in zip(ab_, rb_)]
    bd_b = [_pair_diag(x, left) for x in bb_]
    bd_k = [_pair_diag(x, left) for x in kb_]
    bd_v = [_pair_diag(x, left) for x in v_]
    arbk = [_dg(x, jnp.concatenate([y, z], axis=0), _NT) for x, y, z in zip(ar16, bd_b, bd_k)]
    arb = [x[:, :PAIR] for x in arbk]
    ark = [x[:, PAIR:] for x in arbk]
    a_ab = [jnp.where(strict, x[:CHUNK], 0.0) for x in arb]
    a_rb = [jnp.where(incl, x[CHUNK:], 0.0).astype(BF16) for x in arb]
    a_akrk = [jnp.concatenate([jnp.where(strict, x[:CHUNK], 0.0), jnp.where(incl, x[CHUNK:], 0.0)],
                              axis=0).astype(BF16) for x in ark]
    yield
    base = 8
    d = [jnp.where((rc // base) == (cc // base), a, 0.0) for a in a_ab]
    d2 = [_pair_mm(t, t, left) for t in d]
    akrkv = [_dg(x, y, _NN) for x, y in zip(a_akrk, bd_v)]
    akv = [x[:CHUNK] for x in akrkv]
    rkv = [x[CHUNK:] for x in akrkv]
    x = [jnp.where(diag, 1.0, 0.0) + t for t in d]
    yield
    xd = [_dg(t2.astype(BF16),
              jnp.concatenate([_pair_diag(xi, left), _pair_diag(t2, left)], axis=1), _NN)
          for xi, t2 in zip(x, d2)]
    x = [xi + y[:, :PAIR] for xi, y in zip(x, xd)]
    d4 = [y[:, PAIR:] for y in xd]
    yield
    x = [xi + _pair_mm(t4, xi, left) for xi, t4 in zip(x, d4)]
    yield
    size = base
    while size < CHUNK:
        off = jnp.logical_and((rc // (2 * size)) == (cc // (2 * size)),
                              (rc // size) != (cc // size))
        o = [jnp.where(off, a, 0.0) for a in a_ab]
        ox = [_pair_mm(oi, xi, left) for oi, xi in zip(o, x)]
        yield
        x = [xi + _pair_mm(xi, oxi, left) for xi, oxi in zip(x, ox)]
        yield
        size *= 2
    t = [xi.astype(BF16) for xi in x]
    wu = [_dg(ti, jnp.concatenate([_pair_diag(y, left), _pair_diag(z, left)], axis=1), _NN)
          for ti, y, z in zip(t, ab_, akv)]
    kv = [_dg(xi.astype(BF16), y.astype(BF16), _TN) for xi, y in zip(kt_, v_)]
    yield
    ry = [_dg(xi, jnp.concatenate([_pair_diag(y[:, :PAIR], left), _pair_diag(y[:, PAIR:], left)], axis=1), _NN)
          for xi, y in zip(a_rb, wu)]
    mn = [_dg(xi.astype(BF16), y.astype(BF16), _TN) for xi, y in zip(bt_, wu)]
    yield

    def head_blocks(z):
        return jnp.where(left, z[:HEAD, :], z[HEAD:, :])

    for u in range(len(ab_)):
        store(u,
              rb_[u] + ry[u][:, :PAIR],
              ry[u][:, PAIR:] + rkv[u],
              jnp.where(diag, pe_[u], 0.0) + head_blocks(mn[u][:, :PAIR]),
              head_blocks(mn[u][:, PAIR:]) + head_blocks(kv[u]))


def _rwkv_a_body(r_ref, k_ref, v_ref, l_ref, rh_ref, kh_ref, vh_ref, lh_ref,
                 mur_ref, muk_ref, muv_ref, mul_ref, w0_ref, a0_ref, kkw_ref, kaw_ref, rkw_ref,
                 w2_ref, a2_ref, g2_ref, ones_ref, tri_ref,
                 u_ref, gate_ref, halo_ref, cw_ref, cb_ref, wa_ref, wx_ref, ba_ref, bx_ref, lam_ref,
                 *rest):
    ncast = (len(rest) - 8) // 2
    cast_in = rest[:ncast]
    rp_ref, yp_ref, m_ref, n_ref, bonus_ref, g_ref, ya_ref = rest[ncast:ncast + 7]
    cast_out = rest[ncast + 7:-1]
    carry_ref = rest[-1]
    first = pl.program_id(1) == 0

    @pl.when(first)
    def _():
        carry_ref[...] = jnp.zeros_like(carry_ref)
    cl = CHUNK
    rows = CHUNKS_PER_STEP * cl
    width = HEADS_PER_STEP * HEAD
    gw = ones_ref.shape[0]
    row_g = lax.broadcasted_iota(jnp.int32, (rows, gw), 0)
    row_l = lax.broadcasted_iota(jnp.int32, (rows, l_ref.shape[1]), 0)
    ones_h = ones_ref[...]

    lo = _token_shift(l_ref[...], lh_ref[...], mul_ref[...], first, row_l)
    act_w = _split(jnp.tanh(lo[:, 0:LANE]))
    act_a = _split(lo[:, LANE:2 * LANE])
    act_g = _split(jax.nn.sigmoid(lo[:, 2 * LANE:]))

    def lora(act, w_ref, cs, keep_low):
        (ah, al_), wb = act, w_ref[:, cs]
        out = _dg(ah, wb, _NN)
        return out + _dg(al_, wb, _NN) if keep_low else out

    def prologue(c0, out):
        cs = slice(c0, c0 + gw)
        r = _token_shift(r_ref[:, cs], rh_ref[:, cs], mur_ref[:, cs], first, row_g)
        k = _token_shift(k_ref[:, cs], kh_ref[:, cs], muk_ref[:, cs], first, row_g)
        v = _token_shift(v_ref[:, cs], vh_ref[:, cs], muv_ref[:, cs], first, row_g)
        w_lin = w0_ref[:, cs] + lora(act_w, w2_ref, cs, True)
        a_lin = a0_ref[:, cs] + lora(act_a, a2_ref, cs, False)
        g_ref[:, cs] = lora(act_g, g2_ref, cs, False).astype(BF16)
        kk = k * kkw_ref[:, cs]
        kk_ss = _head_sums(kk * kk, ones_h)
        yield
        lw = DECAY_SCALE * jax.nn.sigmoid(w_lin)
        a = jax.nn.sigmoid(a_lin)
        kk = kk * lax.rsqrt(jnp.maximum(kk_ss, L2_EPS * L2_EPS))
        kp = k * (1.0 + (a - 1.0) * kaw_ref[:, cs])
        bonus_ref[:, cs] = (_head_sums(r * kp * rkw_ref[:, cs], ones_h) * v).astype(BF16)
        lc = _mm2_exact_lhs(tri_ref[...], lw)
        yield
        p_incl = jnp.exp(lc)
        p_excl = jnp.exp(lc - lw)
        p_inv = 1.0 / p_incl
        p_end = jnp.concatenate(
            [jnp.broadcast_to(p_incl[(j + 1) * cl - 1:(j + 1) * cl, :], (cl, gw))
             for j in range(CHUNKS_PER_STEP)], axis=0)
        abar = -(kk * p_excl)
        bbar = kk * a * p_inv
        kbar = kp * p_inv
        rbar = r * p_incl
        btil = bbar * p_end
        ktil = kbar * p_end
        units = [(j, q) for j in range(CHUNKS_PER_STEP) for q in range(gw // PAIR)]
        out.extend([x[j * cl:(j + 1) * cl, q * PAIR:(q + 1) * PAIR] for j, q in units]
                   for x in (abar, bbar, kbar, rbar, v, btil, ktil, p_end))
        yield

    def make_store(c0):
        units = [(j, q) for j in range(CHUNKS_PER_STEP) for q in range(gw // PAIR)]

        def store(u, rp, yp, mm, nn):
            j, q = units[u]
            rs = slice(j * cl, (j + 1) * cl)
            qs = slice(c0 + q * PAIR, c0 + (q + 1) * PAIR)
            rp_ref[rs, qs] = rp.astype(BF16)
            yp_ref[rs, qs] = yp.astype(BF16)
            m_ref[rs, qs] = mm.astype(BF16)
            n_ref[rs, qs] = nn.astype(BF16)
        return store

    lru = (None for h in range(LRU_HEADS)
           for _ in _lru_head(h, first, u_ref, gate_ref, halo_ref, cw_ref, cb_ref, wa_ref, wx_ref,
                              ba_ref, bx_ref, lam_ref, ya_ref, carry_ref))
    chains = []
    for c0 in range(0, width, gw):
        ops = []
        for _ in prologue(c0, ops):
            for ch in chains:
                next(ch, None)
        chains.append(_chunk_chain(ops, make_store(c0)))
    live = list(chains)
    while live:
        live = [ch for ch in live if next(ch, StopIteration) is not StopIteration]
        next(lru, None)
    for _ in lru:
        pass

    for src, dst in zip(cast_in, cast_out):
        dst[...] = src[...].astype(BF16)


def _rwkv_a(p, p_lora, mu_rkv, mu_lora, w0, a0, k_k, k_a, r_k, w2p, a2p, g2p,
            conv_w, conv_b, wa, wx, ba, bx, lam, bsz, seq, rkv_col0, cast_ws=()):
    dl = conv_w.shape[1]
    lvec = lambda t: t.reshape(1, dl)
    lrow = pl.BlockSpec((1, dl), lambda b, i, q: (0, 0))

    def lru_tile(cb):
        return pl.BlockSpec((CHUNKS_PER_STEP * CHUNK, dl), lambda b, i, q: (b * nc + i, cb))
    cl = CHUNKS_PER_STEP * CHUNK
    width = HEADS_PER_STEP * HEAD
    dr = w0.shape[1]
    ngroups = dr // width
    assert ngroups == 1, "the LRU ride-along expects one grid step per row tile"
    nc = seq // cl
    lw_ = mu_lora.shape[1]
    cb0 = rkv_col0 // width
    rows8 = cl // SUBLANE
    rt, ct = _iota2((cl, cl))
    tri = jnp.where(jnp.logical_and(rt >= ct, (rt // CHUNK) == (ct // CHUNK)), 1.0, 0.0).astype(BF16)
    ones_h = _head_ones(ONES_WIDTH)
    const = lambda arr: pl.BlockSpec(arr.shape, lambda b, i, q: (0, 0))
    lora_w = [wgt.astype(BF16) for wgt in (w2p, a2p, g2p)]

    def tile(cb_off):
        return pl.BlockSpec((cl, width), lambda b, i, q: (b * nc + i, cb0 + cb_off + q))

    def halo(cb_off):
        return pl.BlockSpec(
            (SUBLANE, width),
            lambda b, i, q: (jnp.maximum((b * nc + i) * rows8 - 1, 0), cb0 + cb_off + q))

    def prow(off=0):
        return pl.BlockSpec((1, width), lambda b, i, q: (0, off + q))

    out_tile = pl.BlockSpec((cl, width), lambda b, i, q: (b * nc + i, q))
    out_mat = pl.BlockSpec((CHUNKS_PER_STEP * HEAD, width), lambda b, i, q: (b * nc + i, q))
    act = jax.ShapeDtypeStruct((bsz * seq, dr), BF16)
    mat = jax.ShapeDtypeStruct((bsz * (seq // CHUNK) * HEAD, dr), BF16)

    nsteps = bsz * nc * ngroups
    cast_specs = []
    for wgt in cast_ws:
        hold = 1
        while (wgt.shape[0] * hold) % (nsteps * BF16_SUBLANE) != 0:
            hold *= 2
        blk = (wgt.shape[0] * hold // nsteps, wgt.shape[1])
        cast_specs.append(pl.BlockSpec(
            blk, lambda b, i, q, hold=hold: (((b * nc + i) * ngroups + q) // hold, 0)))
    cast_shapes = [jax.ShapeDtypeStruct(wgt.shape, BF16) for wgt in cast_ws]

    return pl.pallas_call(
        _rwkv_a_body,
        grid=(bsz, nc, ngroups),
        in_specs=[tile(0), tile(ngroups), tile(2 * ngroups),
                  pl.BlockSpec((cl, lw_), lambda b, i, q: (b * nc + i, 0)),
                  halo(0), halo(ngroups), halo(2 * ngroups),
                  pl.BlockSpec((SUBLANE, lw_),
                               lambda b, i, q: (jnp.maximum((b * nc + i) * rows8 - 1, 0), 0)),
                  prow(0), prow(ngroups), prow(2 * ngroups),
                  pl.BlockSpec((1, lw_), lambda b, i, q: (0, 0)),
                  prow(), prow(), prow(), prow(), prow()]
                 + [pl.BlockSpec((t.shape[0], width), lambda b, i, q: (0, q)) for t in lora_w]
                 + [const(ones_h), const(tri)]
                 + [lru_tile(0), lru_tile(1),
                    pl.BlockSpec((SUBLANE, dl),
                                 lambda b, i, q: (jnp.maximum((b * nc + i) * rows8 - 1, 0), 0)),
                    const(conv_w), lrow, pl.BlockSpec(wa.shape, lambda b, i, q: (0, 0, 0)),
                    pl.BlockSpec(wx.shape, lambda b, i, q: (0, 0, 0)), lrow, lrow, lrow]
                 + cast_specs,
        out_specs=[out_tile, out_tile, out_mat, out_mat, out_tile, out_tile, lru_tile(0)] + cast_specs,
        out_shape=[act, act, mat, mat, act, act, jax.ShapeDtypeStruct((bsz * seq, dl), BF16)]
                  + cast_shapes,
        scratch_shapes=[pltpu.VMEM((1, dl), F32)],
        compiler_params=_params("arbitrary", "arbitrary", "arbitrary"),
        name="rwkv_a",
    )(p, p, p, p_lora, p, p, p, p_lora, mu_rkv, mu_rkv, mu_rkv, mu_lora, w0, a0, k_k, k_a, r_k,
      *lora_w, ones_h, tri,
      p, p, p, conv_w, lvec(conv_b), wa, wx, lvec(ba), lvec(bx), lvec(lam), *cast_ws)


def _rwkv_b_body(rp_ref, yp_ref, m_ref, n_ref, bonus_ref, g_ref, lng_ref, lnb_ref, ones_ref,
                 o_ref, state_ref):
    @pl.when(pl.program_id(0) == 0)
    def _():
        state_ref[...] = jnp.zeros_like(state_ref)

    nb = rp_ref.shape[0]
    npairs = state_ref.shape[0] // nb
    units = [(b, q) for b in range(nb) for q in range(npairs)]
    ps = [slice(q * PAIR, (q + 1) * PAIR) for q in range(npairs)]
    left = lax.broadcasted_iota(jnp.int32, (HEAD, PAIR), 1) < HEAD
    ones_h = ones_ref[...]
    inv_n = 1.0 / HEAD
    state = [state_ref[u] for u in range(len(units))]
    for j in range(rp_ref.shape[1] // CHUNK):
        rs = slice(j * CHUNK, (j + 1) * CHUNK)
        ks = slice(j * HEAD, (j + 1) * HEAD)
        g0 = [_pair_diag(s, left) for s in state]
        ys = [_dg(rp_ref[b, rs, ps[q]], g0[u], _NN) + yp_ref[b, rs, ps[q]]
              for u, (b, q) in enumerate(units)]
        state = [_dg(m_ref[b, ks, ps[q]], g0[u], _NN) + n_ref[b, ks, ps[q]]
                 for u, (b, q) in enumerate(units)]
        y = jnp.concatenate([jnp.concatenate(ys[b * npairs:(b + 1) * npairs], axis=1)
                             for b in range(nb)], axis=0)
        yc = y - _head_sums(y, ones_h) * inv_n
        var = _head_sums(yc * yc, ones_h) * inv_n
        yn = yc * lax.rsqrt(var + GN_EPS) * lng_ref[...] + lnb_ref[...]
        for b in range(nb):
            bs = slice(b * CHUNK, (b + 1) * CHUNK)
            o_ref[b, rs, :] = ((yn[bs] + bonus_ref[b, rs, :]) * g_ref[b, rs, :]).astype(BF16)
    for u in range(len(units)):
        state_ref[u] = state[u]


def _rwkv_b(rp, yp, mc, nm, bonus, g, ln_g, ln_b, bsz, seq):
    cl = RWKV_B_CHUNKS * CHUNK
    dr = rp.shape[1]
    nc = seq // cl
    by_seq = lambda t: t.reshape(bsz, -1, dr)
    tile = pl.BlockSpec((bsz, cl, dr), lambda i: (0, i, 0))
    mat = pl.BlockSpec((bsz, RWKV_B_CHUNKS * HEAD, dr), lambda i: (0, i, 0))
    prow = pl.BlockSpec((1, dr), lambda i: (0, 0))
    ones_h = _head_ones(ONES_WIDTH_B)
    out = pl.pallas_call(
        _rwkv_b_body,
        grid=(nc,),
        in_specs=[tile, tile, mat, mat, tile, tile, prow, prow,
                  pl.BlockSpec(ones_h.shape, lambda i: (0, 0))],
        out_specs=tile,
        out_shape=jax.ShapeDtypeStruct((bsz, seq, dr), BF16),
        scratch_shapes=[pltpu.VMEM((bsz * (dr // PAIR), HEAD, PAIR), F32)],
        compiler_params=_params("arbitrary"),
        name="rwkv_b",
    )(by_seq(rp), by_seq(yp), by_seq(mc), by_seq(nm), by_seq(bonus), by_seq(g), ln_g, ln_b, ones_h)
    return out.reshape(bsz * seq, dr)


def _mm_out_body(ya_ref, yb_ref, x_ref, gm_ref, w_ref, g_ref, sh_ref, sc_ref, o_ref, h_ref, *, sub):
    da = ya_ref.shape[1]
    for r0 in range(0, x_ref.shape[0], sub):
        rs = slice(r0, r0 + sub)
        mix = (jnp.dot(ya_ref[rs, :], w_ref[:da, :], preferred_element_type=F32)
               + jnp.dot(yb_ref[rs, :], w_ref[da:, :], preferred_element_type=F32))
        x1 = x_ref[rs, :] + gm_ref[0] * mix
        o_ref[rs, :] = x1
        h_ref[rs, :] = _norm_mod(x1, g_ref[...], sh_ref[0], sc_ref[0]).astype(BF16)


def _mm_out(ya, yb, x2, gm, w, g, sh, sc, seq, tm=512, sub=256):
    m, d = x2.shape
    per_b = seq // tm
    brow = pl.BlockSpec((1, 1, d), lambda i: (i // per_b, 0, 0))
    tile = pl.BlockSpec((tm, d), lambda i: (i, 0))
    return pl.pallas_call(
        functools.partial(_mm_out_body, sub=sub),
        grid=(m // tm,),
        in_specs=[pl.BlockSpec((tm, ya.shape[1]), lambda i: (i, 0)),
                  pl.BlockSpec((tm, yb.shape[1]), lambda i: (i, 0)),
                  tile, brow,
                  pl.BlockSpec(w.shape, lambda i: (0, 0)),
                  pl.BlockSpec((1, d), lambda i: (0, 0)), brow, brow],
        out_specs=[tile, tile],
        out_shape=[jax.ShapeDtypeStruct((m, d), F32), jax.ShapeDtypeStruct((m, d), BF16)],
        compiler_params=_params("parallel"),
        name="mm_out",
    )(ya, yb, x2, gm, w, g, sh, sc)


def _ffn_body(x_ref, h_ref, gf_ref, wg_ref, wu_ref, wd_ref, fg_ref, o_ref, acc_ref):
    f = pl.program_id(1)

    @pl.when(f == 0)
    def _():
        acc_ref[...] = jnp.zeros_like(acc_ref)

    h = h_ref[...]
    gate = jnp.dot(h, wg_ref[...], preferred_element_type=F32)
    up = jnp.dot(h, wu_ref[...], preferred_element_type=F32)
    act = (gate * jax.nn.sigmoid(gate) * up).astype(BF16)
    acc_ref[...] += jnp.dot(act, wd_ref[...], preferred_element_type=F32)

    @pl.when(f == pl.num_programs(1) - 1)
    def _():
        y = x_ref[...] + gf_ref[0] * acc_ref[...]
        o_ref[...] = (y * lax.rsqrt(jnp.mean(y * y, axis=-1, keepdims=True) + RMS_EPS)
                      * fg_ref[...])


def _ffn(x1, h2, gf, w_gu, w_down, fg, seq, tm=512, tf=512):
    m, d = x1.shape
    dff = w_down.shape[0]
    nf = dff // tf
    assert seq % tm == 0 and dff % tf == 0, "row tiles must not straddle sequences"
    per_b = seq // tm
    tile = pl.BlockSpec((tm, d), lambda i, f: (i, 0))
    prow = pl.BlockSpec((1, d), lambda i, f: (0, 0))
    return pl.pallas_call(
        _ffn_body,
        grid=(m // tm, nf),
        in_specs=[tile, tile,
                  pl.BlockSpec((1, 1, d), lambda i, f: (i // per_b, 0, 0)),
                  pl.BlockSpec((d, tf), lambda i, f: (0, f)),
                  pl.BlockSpec((d, tf), lambda i, f: (0, nf + f)),
                  pl.BlockSpec((tf, d), lambda i, f: (f, 0)),
                  prow],
        out_specs=tile,
        out_shape=jax.ShapeDtypeStruct((m, d), F32),
        scratch_shapes=[pltpu.VMEM((tm, d), F32)],
        compiler_params=_params("parallel", "arbitrary"),
        name="ffn",
    )(x1, h2, gf, w_gu, w_gu, w_down, fg)


def _ffn_step(x_ref, h_ref, wg_ref, wu_ref, wd_ref, o_ref, *, gf_ref, fg_ref, acc_ref, step_ref,
              nf, per_b):
    step = step_ref[0]
    step_ref[0] = step + 1
    i, f = step // nf, step % nf

    @pl.when(f == 0)
    def _():
        acc_ref[...] = jnp.zeros_like(acc_ref)

    h = h_ref[...]
    gate = jnp.dot(h, wg_ref[...], preferred_element_type=F32)
    up = jnp.dot(h, wu_ref[...], preferred_element_type=F32)
    act = (gate * jax.nn.sigmoid(gate) * up).astype(BF16)
    acc_ref[...] += jnp.dot(act, wd_ref[...], preferred_element_type=F32)

    @pl.when(f == nf - 1)
    def _():
        y = x_ref[...] + gf_ref[i // per_b] * acc_ref[...]
        o_ref[...] = (y * lax.rsqrt(jnp.mean(y * y, axis=-1, keepdims=True) + RMS_EPS)
                      * fg_ref[...])


def _ffn_nested_body(x_hbm, h_hbm, gf_ref, wgu_hbm, wd_hbm, fg_ref, o_hbm, acc_ref, step_ref, *,
                     tm, tf, nf, per_b, wbuf):
    nt = x_hbm.shape[0] // tm
    d = x_hbm.shape[1]
    row = pl.BlockSpec((tm, d), lambda i, f: (i, 0))
    deep = pl.Buffered(wbuf)
    step_ref[0] = 0
    pltpu.emit_pipeline(
        functools.partial(_ffn_step, gf_ref=gf_ref, fg_ref=fg_ref, acc_ref=acc_ref,
                          step_ref=step_ref, nf=nf, per_b=per_b),
        grid=(nt, nf),
        in_specs=[row, row,
                  pl.BlockSpec((d, tf), lambda i, f: (0, f), pipeline_mode=deep),
                  pl.BlockSpec((d, tf), lambda i, f: (0, nf + f), pipeline_mode=deep),
                  pl.BlockSpec((tf, d), lambda i, f: (f, 0), pipeline_mode=deep)],
        out_specs=[row],
    )(x_hbm, h_hbm, wgu_hbm, wgu_hbm, wd_hbm, o_hbm)


def _ffn_nested(x1, h2, gf, w_gu, w_down, fg, seq, tm=512, tf=512, wbuf=3):
    m, d = x1.shape
    dff = w_down.shape[0]
    assert seq % tm == 0 and dff % tf == 0, "row tiles must not straddle sequences"
    hbm = pl.BlockSpec(memory_space=pl.ANY)
    vmem = pl.BlockSpec(memory_space=pltpu.VMEM)
    return pl.pallas_call(
        functools.partial(_ffn_nested_body, tm=tm, tf=tf, nf=dff // tf, per_b=seq // tm, wbuf=wbuf),
        in_specs=[hbm, hbm, vmem, hbm, hbm, vmem],
        out_specs=hbm,
        out_shape=jax.ShapeDtypeStruct((m, d), F32),
        scratch_shapes=[pltpu.VMEM((tm, d), F32), pltpu.SMEM((1,), jnp.int32)],
        compiler_params=pltpu.CompilerParams(vmem_limit_bytes=VMEM_LIMIT),
        name="ffn",
    )(x1, h2, gf, w_gu, w_down, fg)


def _pad_cols(w, n):
    return jnp.pad(w, ((0, 0), (0, n - w.shape[1])))


def _pad_rows(w, n):
    return jnp.pad(w, ((0, n - w.shape[0]), (0, 0)))


def kernel(x, c, w_ada, b_ada, norm_mix_g, w_in, conv_w, conv_b, lru_wa, lru_ba, lru_wx, lru_bx, lru_lambda, rwkv_mu, rwkv_w0, rwkv_w2, rwkv_a0, rwkv_a2, rwkv_g2, rwkv_k_k, rwkv_k_a, rwkv_r_k, rwkv_ln_g, rwkv_ln_b, w_out, norm_ffn_g, w_gu, w_down, final_norm_g):
    bsz, seq, d = x.shape
    depth = w_ada.shape[0]
    assert depth == 1, "the closing RMSNorm is fused into the (single) layer's ffn kernel"
    dl = conv_w.shape[2]
    dr = rwkv_w0.shape[1]
    w_lora, a_lora, g_lora = rwkv_w2.shape[1], rwkv_a2.shape[1], rwkv_g2.shape[1]
    wpad, apad = LANE, LANE
    gpad = -(-g_lora // LANE) * LANE
    rkv_col0 = 2 * dl
    lora0 = rkv_col0 + 3 * dr

    x2 = x.reshape(bsz * seq, d)
    for l in range(depth):
        mod = _mod(c, w_ada[l], b_ada[l].reshape(1, -1))
        sh_m, sc_m, g_m, sh_f, sc_f, g_f = [t.reshape(bsz, 1, d) for t in jnp.split(mod, 6, axis=-1)]

        wi = jnp.swapaxes(w_in[l], 0, 1)
        o1, o2 = lora0 + w_lora, lora0 + w_lora + a_lora
        w_lora_p = jnp.concatenate(
            [_pad_rows(wi[lora0:o1], wpad), _pad_rows(wi[o1:o2], apad),
             _pad_rows(wi[o2:], gpad)], axis=0)
        mu = rwkv_mu[l].reshape(1, -1)
        mu_rkv = mu[:, :3 * dr]
        mu_lora = jnp.concatenate(
            [_pad_cols(mu[:, 3 * dr:3 * dr + w_lora], wpad),
             _pad_cols(mu[:, 3 * dr + w_lora:3 * dr + w_lora + a_lora], apad),
             _pad_cols(mu[:, 3 * dr + w_lora + a_lora:], gpad)], axis=1)
        w2p = _pad_rows(rwkv_w2[l], wpad)
        a2p = _pad_rows(rwkv_a2[l], apad)
        g2p = _pad_rows(rwkv_g2[l], gpad)

        h, p_lora = _norm(x2, norm_mix_g[l].reshape(1, d), sh_m, sc_m, w_lora_p, seq)
        p = _mm_in(h, wi, lora0, gelu_tile=1, tn=dl)

        rowv = lambda t: t.reshape(1, dr)
        rp, yp, mc, nm, bonus, gg, y_a, w_out_b, w_gu_b, w_down_b = _rwkv_a(
            p, p_lora, mu_rkv, mu_lora, rowv(rwkv_w0[l]), rowv(rwkv_a0[l]), rowv(rwkv_k_k[l]),
            rowv(rwkv_k_a[l]), rowv(rwkv_r_k[l]), w2p, a2p, g2p,
            conv_w[l], conv_b[l], lru_wa[l].astype(BF16), lru_wx[l].astype(BF16),
            lru_ba[l], lru_bx[l], lru_lambda[l], bsz, seq, rkv_col0,
            cast_ws=(w_out[l], w_gu[l], w_down[l]))
        y_b = _rwkv_b(rp, yp, mc, nm, bonus, gg, rowv(rwkv_ln_g[l]), rowv(rwkv_ln_b[l]), bsz, seq)

        x2, h2 = _mm_out(y_a, y_b, x2, g_m, w_out_b, norm_ffn_g[l].reshape(1, d), sh_f, sc_f, seq)
        x2 = _ffn_nested(x2, h2, g_f, w_gu_b, w_down_b, final_norm_g.reshape(1, d), seq)
    return x2.reshape(bsz, seq, d)
```

```python
import functools
import math

import jax
import jax.numpy as jnp
from jax import lax
from jax.experimental import pallas as pl
from jax.experimental.pallas import tpu as pltpu

F32 = jnp.float32
BF16 = jnp.bfloat16

LRU_HEADS = 4
CONV_WIDTH = 4
LRU_C = 8.0
HEAD = 64
CHUNK = 64
PAIR = 2 * HEAD
HEADS_PER_STEP = 16
ONES_WIDTH = 256
ONES_WIDTH_B = 128
CHUNKS_PER_STEP = 4
RWKV_B_CHUNKS = 4
MOD_DMA_BANDS = 4
RMS_EPS = 1e-6
GN_EPS = 64e-5
L2_EPS = 1e-12
DECAY_SCALE = -math.exp(-0.5)
LANE = 128
SUBLANE = 8
BF16_SUBLANE = 16
VMEM_LIMIT = 56 * 1024 * 1024


def _params(*sem):
    return pltpu.CompilerParams(dimension_semantics=sem, vmem_limit_bytes=VMEM_LIMIT)


_NN = (((1,), (0,)), ((), ()))
_NT = (((1,), (1,)), ((), ()))
_TN = (((0,), (0,)), ((), ()))


def _dg(a, b, dims):
    return lax.dot_general(a, b, dims, preferred_element_type=F32)


def _split(x):
    hi = x.astype(BF16)
    lo = (x - hi.astype(F32)).astype(BF16)
    return hi, lo


def _mm3(a, b, dims=_NN):
    ah, al = _split(a)
    bh, bl = _split(b)
    return _dg(ah, bh, dims) + (_dg(ah, bl, dims) + _dg(al, bh, dims))


def _head_sums(x, ones_h):
    n = ones_h.shape[0]
    xb = x.astype(BF16)
    return jnp.concatenate([_dg(xb[:, c:c + n], ones_h, _NN) for c in range(0, x.shape[1], n)],
                           axis=1)


def _mm2_exact_lhs(a_bf16, b):
    bh, bl = _split(b)
    return _dg(a_bf16, bh, _NN) + _dg(a_bf16, bl, _NN)


def _softplus(x):
    return jnp.maximum(x, 0.0) + jnp.log1p(jnp.exp(-jnp.abs(x)))


def _iota2(shape):
    return (lax.broadcasted_iota(jnp.int32, shape, 0),
            lax.broadcasted_iota(jnp.int32, shape, 1))


def _head_ones(n):
    r, c = _iota2((n, n))
    return jnp.where((r // HEAD) == (c // HEAD), 1.0, 0.0).astype(BF16)


def _mod_body(c_ref, *refs):
    w_refs, b_ref, o_ref = refs[:-2], refs[-2], refs[-1]
    c = c_ref[...]
    ca = c * jax.nn.sigmoid(c)
    kb = w_refs[0].shape[0]
    acc = b_ref[...]
    for s, w_ref in enumerate(w_refs):
        acc = acc + _mm3(ca[:, s * kb:(s + 1) * kb], w_ref[...])
    o_ref[...] = acc


def _mod(c, w, b, tn=1024, bands=MOD_DMA_BANDS):
    bsz, d = c.shape
    n = w.shape[1]
    kb = d // bands
    return pl.pallas_call(
        _mod_body,
        grid=(n // tn,),
        in_specs=[pl.BlockSpec((bsz, d), lambda j: (0, 0))]
                 + [pl.BlockSpec((kb, tn), lambda j, s=s: (s, j)) for s in range(bands)]
                 + [pl.BlockSpec((1, tn), lambda j: (0, j))],
        out_specs=pl.BlockSpec((bsz, tn), lambda j: (0, j)),
        out_shape=jax.ShapeDtypeStruct((bsz, n), F32),
        compiler_params=_params("parallel"),
        name="mod",
    )(c, *([w] * bands), b)


def _norm_mod(x, g, sh, sc):
    y = x * lax.rsqrt(jnp.mean(x * x, axis=-1, keepdims=True) + RMS_EPS) * g
    return y * (1.0 + sc) + sh


def _norm_body(x_ref, g_ref, sh_ref, sc_ref, w_ref, o_ref, pl_ref, wb_ref):
    @pl.when(pl.program_id(0) == 0)
    def _():
        wb_ref[...] = w_ref[...].astype(BF16)

    h = _norm_mod(x_ref[...], g_ref[...], sh_ref[0], sc_ref[0]).astype(BF16)
    o_ref[...] = h
    pl_ref[...] = _dg(h, wb_ref[...], _NT)


def _norm(x2, g, sh, sc, w_lora_t, seq, tm=512):
    m, d = x2.shape
    nl = w_lora_t.shape[0]
    per_b = seq // tm
    return pl.pallas_call(
        _norm_body,
        grid=(m // tm,),
        in_specs=[pl.BlockSpec((tm, d), lambda i: (i, 0)),
                  pl.BlockSpec((1, d), lambda i: (0, 0)),
                  pl.BlockSpec((1, 1, d), lambda i: (i // per_b, 0, 0)),
                  pl.BlockSpec((1, 1, d), lambda i: (i // per_b, 0, 0)),
                  pl.BlockSpec((nl, d), lambda i: (0, 0))],
        out_specs=[pl.BlockSpec((tm, d), lambda i: (i, 0)), pl.BlockSpec((tm, nl), lambda i: (i, 0))],
        out_shape=[jax.ShapeDtypeStruct((m, d), BF16), jax.ShapeDtypeStruct((m, nl), F32)],
        scratch_shapes=[pltpu.VMEM((nl, d), BF16)],
        compiler_params=_params("arbitrary"),
        name="norm_mix",
    )(x2, g, sh, sc, w_lora_t)


def _mm_in_body(h_ref, w_ref, o_ref, wb_ref, *, gelu_tile):
    @pl.when(pl.program_id(1) == 0)
    def _():
        wb_ref[...] = w_ref[...].astype(BF16)

    @pl.when(pl.program_id(0) == gelu_tile)
    def _():
        o_ref[...] = jax.nn.gelu(_dg(h_ref[...], wb_ref[...], _NT))

    @pl.when(pl.program_id(0) != gelu_tile)
    def _():
        o_ref[...] = _dg(h_ref[...], wb_ref[...], _NT)


def _mm_in(h, wt, ncols, gelu_tile, tm=1024, tn=1024):
    m, d = h.shape
    return pl.pallas_call(
        functools.partial(_mm_in_body, gelu_tile=gelu_tile),
        grid=(ncols // tn, m // tm),
        in_specs=[pl.BlockSpec((tm, d), lambda j, i: (i, 0)),
                  pl.BlockSpec((tn, d), lambda j, i: (j, 0))],
        out_specs=pl.BlockSpec((tm, tn), lambda j, i: (i, j)),
        out_shape=jax.ShapeDtypeStruct((m, ncols), F32),
        scratch_shapes=[pltpu.VMEM((tn, d), BF16)],
        compiler_params=_params("parallel", "arbitrary"),
        name="mm_in",
    )(h, wt)


def _lru_head(h, first, u_ref, gate_ref, halo_ref, cw_ref, cb_ref, wa_ref, wx_ref, ba_ref, bx_ref,
              lam_ref, o_ref, carry_ref):
    tt = u_ref.shape[0]
    hd = u_ref.shape[1] // LRU_HEADS
    cs = slice(h * hd, (h + 1) * hd)
    p = u_ref[:, cs]
    halo = jnp.where(first, 0.0, halo_ref[:, cs])
    ext = jnp.concatenate([halo, p], axis=0)
    cw = cw_ref[:, cs]
    u = cb_ref[:, cs] + p * cw[CONV_WIDTH - 1:CONV_WIDTH, :]
    for j in range(1, CONV_WIDTH):
        shifted = pltpu.roll(ext, j, 0)[SUBLANE:, :]
        u = u + shifted * cw[CONV_WIDTH - 1 - j:CONV_WIDTH - j, :]
    ub = u.astype(BF16)
    ra = jnp.dot(ub, wa_ref[h], preferred_element_type=F32)
    rx = jnp.dot(ub, wx_ref[h], preferred_element_type=F32)
    yield
    r = jax.nn.sigmoid(ra + ba_ref[:, cs])
    ig = jax.nn.sigmoid(rx + bx_ref[:, cs])
    a = jnp.exp(r * ((-LRU_C) * _softplus(-lam_ref[:, cs])))
    mult = jnp.sqrt(1.0 - a * a)
    row = lax.broadcasted_iota(jnp.int32, (tt, hd), 0)
    mult = jnp.where(jnp.logical_and(first, row == 0), 1.0, mult)
    b = mult * (ig * u)

    groups = tt // SUBLANE
    a3 = a.reshape(groups, SUBLANE, hd)
    b3 = b.reshape(groups, SUBLANE, hd)
    sub = lax.broadcasted_iota(jnp.int32, (groups, SUBLANE, hd), 1)
    s = 1
    while s < SUBLANE:
        keep = sub >= s
        a_s = jnp.where(keep, pltpu.roll(a3, s, 1), 1.0)
        b_s = jnp.where(keep, pltpu.roll(b3, s, 1), 0.0)
        b3 = a3 * b_s + b3
        a3 = a3 * a_s
        s *= 2
    yield
    gate = gate_ref[:, cs]
    carry = carry_ref[:, cs]
    outs = []
    for g in range(groups):
        hh = b3[g] + a3[g] * carry
        carry = hh[SUBLANE - 1:SUBLANE, :]
        outs.append(hh * gate[g * SUBLANE:(g + 1) * SUBLANE, :])
    per = BF16_SUBLANE // SUBLANE
    for t0 in range(0, groups, per):
        o_ref[t0 * SUBLANE:(t0 + per) * SUBLANE, cs] = jnp.concatenate(
            outs[t0:t0 + per], axis=0).astype(BF16)
    carry_ref[:, cs] = carry
    yield


def _token_shift(x, halo, mu, first, row):
    prev = jnp.where(first, 0.0, halo[SUBLANE - 1:SUBLANE, :])
    xs = jnp.where(row == 0, prev, pltpu.roll(x, 1, 0))
    return x + (xs - x) * mu


def _pair_diag(y, left):
    return jnp.concatenate([jnp.where(left, y, 0.0), jnp.where(left, 0.0, y)], axis=0).astype(BF16)


def _pair_mm(x, y, left):
    return _dg(x.astype(BF16), _pair_diag(y, left), _NN)


def _chunk_chain(ops, store):
    ab_, bb_, kb_, rb_, v_, bt_, kt_, pe_ = ops
    rc, lane = _iota2((CHUNK, PAIR))
    cc = lane % HEAD
    left = lane < HEAD
    strict = rc > cc
    incl = rc >= cc
    diag = rc == cc
    ar16 = [jnp.concatenate([x, y], axis=0).astype(BF16) for x, y in zip(ab_, rb_)]
    bd_b = [_pair_diag(x, left) for x in bb_]
    bd_k = [_pair_diag(x, left) for x in kb_]
    bd_v = [_pair_diag(x, left) for x in v_]
    arbk = [_dg(x, jnp.concatenate([y, z], axis=0), _NT) for x, y, z in zip(ar16, bd_b, bd_k)]
    arb = [x[:, :PAIR] for x in arbk]
    ark = [x[:, PAIR:] for x in arbk]
    a_ab = [jnp.where(strict, x[:CHUNK], 0.0) for x in arb]
    a_rb = [jnp.where(incl, x[CHUNK:], 0.0).astype(BF16) for x in arb]
    a_akrk = [jnp.concatenate([jnp.where(strict, x[:CHUNK], 0.0), jnp.where(incl, x[CHUNK:], 0.0)],
                              axis=0).astype(BF16) for x in ark]
    yield
    base = 8
    d = [jnp.where((rc // base) == (cc // base), a, 0.0) for a in a_ab]
    d2 = [_pair_mm(t, t, left) for t in d]
    akrkv = [_dg(x, y, _NN) for x, y in zip(a_akrk, bd_v)]
    akv = [x[:CHUNK] for x in akrkv]
    rkv = [x[CHUNK:] for x in akrkv]
    x = [jnp.where(diag, 1.0, 0.0) + t for t in d]
    yield
    xd = [_dg(t2.astype(BF16),
              jnp.concatenate([_pair_diag(xi, left), _pair_diag(t2, left)], axis=1), _NN)
          for xi, t2 in zip(x, d2)]
    x = [xi + y[:, :PAIR] for xi, y in zip(x, xd)]
    d4 = [y[:, PAIR:] for y in xd]
    yield
    x = [xi + _pair_mm(t4, xi, left) for xi, t4 in zip(x, d4)]
    yield
    size = base
    while size < CHUNK:
        off = jnp.logical_and((rc // (2 * size)) == (cc // (2 * size)),
                              (rc // size) != (cc // size))
        o = [jnp.where(off, a, 0.0) for a in a_ab]
        ox = [_pair_mm(oi, xi, left) for oi, xi in zip(o, x)]
        yield
        x = [xi + _pair_mm(xi, oxi, left) for xi, oxi in zip(x, ox)]
        yield
        size *= 2
    t = [xi.astype(BF16) for xi in x]
    wu = [_dg(ti, jnp.concatenate([_pair_diag(y, left), _pair_diag(z, left)], axis=1), _NN)
          for ti, y, z in zip(t, ab_, akv)]
    kv = [_dg(xi.astype(BF16), y.astype(BF16), _TN) for xi, y in zip(kt_, v_)]
    yield
    ry = [_dg(xi, jnp.concatenate([_pair_diag(y[:, :PAIR], left), _pair_diag(y[:, PAIR:], left)], axis=1), _NN)
          for xi, y in zip(a_rb, wu)]
    mn = [_dg(xi.astype(BF16), y.astype(BF16), _TN) for xi, y in zip(bt_, wu)]
    yield

    def head_blocks(z):
        return jnp.where(left, z[:HEAD, :], z[HEAD:, :])

    for u in range(len(ab_)):
        store(u,
              rb_[u] + ry[u][:, :PAIR],
              ry[u][:, PAIR:] + rkv[u],
              jnp.where(diag, pe_[u], 0.0) + head_blocks(mn[u][:, :PAIR]),
              head_blocks(mn[u][:, PAIR:]) + head_blocks(kv[u]))


def _rwkv_a_body(r_ref, k_ref, v_ref, l_ref, rh_ref, kh_ref, vh_ref, lh_ref,
                 mur_ref, muk_ref, muv_ref, mul_ref, w0_ref, a0_ref, kkw_ref, kaw_ref, rkw_ref,
                 w2_ref, a2_ref, g2_ref, ones_ref, tri_ref,
                 u_ref, gate_ref, halo_ref, cw_ref, cb_ref, wa_ref, wx_ref, ba_ref, bx_ref, lam_ref,
                 *rest):
    ncast = (len(rest) - 8) // 2
    cast_in = rest[:ncast]
    rp_ref, yp_ref, m_ref, n_ref, bonus_ref, g_ref, ya_ref = rest[ncast:ncast + 7]
    cast_out = rest[ncast + 7:-1]
    carry_ref = rest[-1]
    first = pl.program_id(1) == 0

    @pl.when(first)
    def _():
        carry_ref[...] = jnp.zeros_like(carry_ref)
    cl = CHUNK
    rows = CHUNKS_PER_STEP * cl
    width = HEADS_PER_STEP * HEAD
    gw = ones_ref.shape[0]
    row_g = lax.broadcasted_iota(jnp.int32, (rows, gw), 0)
    row_l = lax.broadcasted_iota(jnp.int32, (rows, l_ref.shape[1]), 0)
    ones_h = ones_ref[...]

    lo = _token_shift(l_ref[...], lh_ref[...], mul_ref[...], first, row_l)
    act_w = _split(jnp.tanh(lo[:, 0:LANE]))
    act_a = _split(lo[:, LANE:2 * LANE])
    act_g = _split(jax.nn.sigmoid(lo[:, 2 * LANE:]))

    def lora(act, w_ref, cs, keep_low):
        (ah, al_), wb = act, w_ref[:, cs]
        out = _dg(ah, wb, _NN)
        return out + _dg(al_, wb, _NN) if keep_low else out

    def prologue(c0, out):
        cs = slice(c0, c0 + gw)
        r = _token_shift(r_ref[:, cs], rh_ref[:, cs], mur_ref[:, cs], first, row_g)
        k = _token_shift(k_ref[:, cs], kh_ref[:, cs], muk_ref[:, cs], first, row_g)
        v = _token_shift(v_ref[:, cs], vh_ref[:, cs], muv_ref[:, cs], first, row_g)
        w_lin = w0_ref[:, cs] + lora(act_w, w2_ref, cs, True)
        a_lin = a0_ref[:, cs] + lora(act_a, a2_ref, cs, False)
        g_ref[:, cs] = lora(act_g, g2_ref, cs, False).astype(BF16)
        kk = k * kkw_ref[:, cs]
        kk_ss = _head_sums(kk * kk, ones_h)
        yield
        lw = DECAY_SCALE * jax.nn.sigmoid(w_lin)
        a = jax.nn.sigmoid(a_lin)
        kk = kk * lax.rsqrt(jnp.maximum(kk_ss, L2_EPS * L2_EPS))
        kp = k * (1.0 + (a - 1.0) * kaw_ref[:, cs])
        bonus_ref[:, cs] = (_head_sums(r * kp * rkw_ref[:, cs], ones_h) * v).astype(BF16)
        lc = _mm2_exact_lhs(tri_ref[...], lw)
        yield
        p_incl = jnp.exp(lc)
        p_excl = jnp.exp(lc - lw)
        p_inv = 1.0 / p_incl
        p_end = jnp.concatenate(
            [jnp.broadcast_to(p_incl[(j + 1) * cl - 1:(j + 1) * cl, :], (cl, gw))
             for j in range(CHUNKS_PER_STEP)], axis=0)
        abar = -(kk * p_excl)
        bbar = kk * a * p_inv
        kbar = kp * p_inv
        rbar = r * p_incl
        btil = bbar * p_end
        ktil = kbar * p_end
        units = [(j, q) for j in range(CHUNKS_PER_STEP) for q in range(gw // PAIR)]
        out.extend([x[j * cl:(j + 1) * cl, q * PAIR:(q + 1) * PAIR] for j, q in units]
                   for x in (abar, bbar, kbar, rbar, v, btil, ktil, p_end))
        yield

    def make_store(c0):
        units = [(j, q) for j in range(CHUNKS_PER_STEP) for q in range(gw // PAIR)]

        def store(u, rp, yp, mm, nn):
            j, q = units[u]
            rs = slice(j * cl, (j + 1) * cl)
            qs = slice(c0 + q * PAIR, c0 + (q + 1) * PAIR)
            rp_ref[rs, qs] = rp.astype(BF16)
            yp_ref[rs, qs] = yp.astype(BF16)
            m_ref[rs, qs] = mm.astype(BF16)
            n_ref[rs, qs] = nn.astype(BF16)
        return store

    lru = (None for h in range(LRU_HEADS)
           for _ in _lru_head(h, first, u_ref, gate_ref, halo_ref, cw_ref, cb_ref, wa_ref, wx_ref,
                              ba_ref, bx_ref, lam_ref, ya_ref, carry_ref))
    chains = []
    for c0 in range(0, width, gw):
        ops = []
        for _ in prologue(c0, ops):
            for ch in chains:
                next(ch, None)
        chains.append(_chunk_chain(ops, make_store(c0)))
    live = list(chains)
    while live:
        live = [ch for ch in live if next(ch, StopIteration) is not StopIteration]
        next(lru, None)
    for _ in lru:
        pass

    for src, dst in zip(cast_in, cast_out):
        dst[...] = src[...].astype(BF16)


def _rwkv_a(p, p_lora, mu_rkv, mu_lora, w0, a0, k_k, k_a, r_k, w2p, a2p, g2p,
            conv_w, conv_b, wa, wx, ba, bx, lam, bsz, seq, rkv_col0, cast_ws=()):
    dl = conv_w.shape[1]
    lvec = lambda t: t.reshape(1, dl)
    lrow = pl.BlockSpec((1, dl), lambda b, i, q: (0, 0))

    def lru_tile(cb):
        return pl.BlockSpec((CHUNKS_PER_STEP * CHUNK, dl), lambda b, i, q: (b * nc + i, cb))
    cl = CHUNKS_PER_STEP * CHUNK
    width = HEADS_PER_STEP * HEAD
    dr = w0.shape[1]
    ngroups = dr // width
    assert ngroups == 1, "the LRU ride-along expects one grid step per row tile"
    nc = seq // cl
    lw_ = mu_lora.shape[1]
    cb0 = rkv_col0 // width
    rows8 = cl // SUBLANE
    rt, ct = _iota2((cl, cl))
    tri = jnp.where(jnp.logical_and(rt >= ct, (rt // CHUNK) == (ct // CHUNK)), 1.0, 0.0).astype(BF16)
    ones_h = _head_ones(ONES_WIDTH)
    const = lambda arr: pl.BlockSpec(arr.shape, lambda b, i, q: (0, 0))
    lora_w = [wgt.astype(BF16) for wgt in (w2p, a2p, g2p)]

    def tile(cb_off):
        return pl.BlockSpec((cl, width), lambda b, i, q: (b * nc + i, cb0 + cb_off + q))

    def halo(cb_off):
        return pl.BlockSpec(
            (SUBLANE, width),
            lambda b, i, q: (jnp.maximum((b * nc + i) * rows8 - 1, 0), cb0 + cb_off + q))

    def prow(off=0):
        return pl.BlockSpec((1, width), lambda b, i, q: (0, off + q))

    out_tile = pl.BlockSpec((cl, width), lambda b, i, q: (b * nc + i, q))
    out_mat = pl.BlockSpec((CHUNKS_PER_STEP * HEAD, width), lambda b, i, q: (b * nc + i, q))
    act = jax.ShapeDtypeStruct((bsz * seq, dr), BF16)
    mat = jax.ShapeDtypeStruct((bsz * (seq // CHUNK) * HEAD, dr), BF16)

    nsteps = bsz * nc * ngroups
    cast_specs = []
    for wgt in cast_ws:
        hold = 1
        while (wgt.shape[0] * hold) % (nsteps * BF16_SUBLANE) != 0:
            hold *= 2
        blk = (wgt.shape[0] * hold // nsteps, wgt.shape[1])
        cast_specs.append(pl.BlockSpec(
            blk, lambda b, i, q, hold=hold: (((b * nc + i) * ngroups + q) // hold, 0)))
    cast_shapes = [jax.ShapeDtypeStruct(wgt.shape, BF16) for wgt in cast_ws]

    return pl.pallas_call(
        _rwkv_a_body,
        grid=(bsz, nc, ngroups),
        in_specs=[tile(0), tile(ngroups), tile(2 * ngroups),
                  pl.BlockSpec((cl, lw_), lambda b, i, q: (b * nc + i, 0)),
                  halo(0), halo(ngroups), halo(2 * ngroups),
                  pl.BlockSpec((SUBLANE, lw_),
                               lambda b, i, q: (jnp.maximum((b * nc + i) * rows8 - 1, 0), 0)),
                  prow(0), prow(ngroups), prow(2 * ngroups),
                  pl.BlockSpec((1, lw_), lambda b, i, q: (0, 0)),
                  prow(), prow(), prow(), prow(), prow()]
                 + [pl.BlockSpec((t.shape[0], width), lambda b, i, q: (0, q)) for t in lora_w]
                 + [const(ones_h), const(tri)]
                 + [lru_tile(0), lru_tile(1),
                    pl.BlockSpec((SUBLANE, dl),
                                 lambda b, i, q: (jnp.maximum((b * nc + i) * rows8 - 1, 0), 0)),
                    const(conv_w), lrow, pl.BlockSpec(wa.shape, lambda b, i, q: (0, 0, 0)),
                    pl.BlockSpec(wx.shape, lambda b, i, q: (0, 0, 0)), lrow, lrow, lrow]
                 + cast_specs,
        out_specs=[out_tile, out_tile, out_mat, out_mat, out_tile, out_tile, lru_tile(0)] + cast_specs,
        out_shape=[act, act, mat, mat, act, act, jax.ShapeDtypeStruct((bsz * seq, dl), BF16)]
                  + cast_shapes,
        scratch_shapes=[pltpu.VMEM((1, dl), F32)],
        compiler_params=_params("arbitrary", "arbitrary", "arbitrary"),
        name="rwkv_a",
    )(p, p, p, p_lora, p, p, p, p_lora, mu_rkv, mu_rkv, mu_rkv, mu_lora, w0, a0, k_k, k_a, r_k,
      *lora_w, ones_h, tri,
      p, p, p, conv_w, lvec(conv_b), wa, wx, lvec(ba), lvec(bx), lvec(lam), *cast_ws)


def _rwkv_b_body(rp_ref, yp_ref, m_ref, n_ref, bonus_ref, g_ref, lng_ref, lnb_ref, ones_ref,
                 o_ref, state_ref):
    @pl.when(pl.program_id(0) == 0)
    def _():
        state_ref[...] = jnp.zeros_like(state_ref)

    nb = rp_ref.shape[0]
    npairs = state_ref.shape[0] // nb
    units = [(b, q) for b in range(nb) for q in range(npairs)]
    ps = [slice(q * PAIR, (q + 1) * PAIR) for q in range(npairs)]
    left = lax.broadcasted_iota(jnp.int32, (HEAD, PAIR), 1) < HEAD
    ones_h = ones_ref[...]
    inv_n = 1.0 / HEAD
    state = [state_ref[u] for u in range(len(units))]
    for j in range(rp_ref.shape[1] // CHUNK):
        rs = slice(j * CHUNK, (j + 1) * CHUNK)
        ks = slice(j * HEAD, (j + 1) * HEAD)
        g0 = [_pair_diag(s, left) for s in state]
        ys = [_dg(rp_ref[b, rs, ps[q]], g0[u], _NN) + yp_ref[b, rs, ps[q]]
              for u, (b, q) in enumerate(units)]
        state = [_dg(m_ref[b, ks, ps[q]], g0[u], _NN) + n_ref[b, ks, ps[q]]
                 for u, (b, q) in enumerate(units)]
        y = jnp.concatenate([jnp.concatenate(ys[b * npairs:(b + 1) * npairs], axis=1)
                             for b in range(nb)], axis=0)
        yc = y - _head_sums(y, ones_h) * inv_n
        var = _head_sums(yc * yc, ones_h) * inv_n
        yn = yc * lax.rsqrt(var + GN_EPS) * lng_ref[...] + lnb_ref[...]
        for b in range(nb):
            bs = slice(b * CHUNK, (b + 1) * CHUNK)
            o_ref[b, rs, :] = ((yn[bs] + bonus_ref[b, rs, :]) * g_ref[b, rs, :]).astype(BF16)
    for u in range(len(units)):
        state_ref[u] = state[u]


def _rwkv_b(rp, yp, mc, nm, bonus, g, ln_g, ln_b, bsz, seq):
    cl = RWKV_B_CHUNKS * CHUNK
    dr = rp.shape[1]
    nc = seq // cl
    by_seq = lambda t: t.reshape(bsz, -1, dr)
    tile = pl.BlockSpec((bsz, cl, dr), lambda i: (0, i, 0))
    mat = pl.BlockSpec((bsz, RWKV_B_CHUNKS * HEAD, dr), lambda i: (0, i, 0))
    prow = pl.BlockSpec((1, dr), lambda i: (0, 0))
    ones_h = _head_ones(ONES_WIDTH_B)
    out = pl.pallas_call(
        _rwkv_b_body,
        grid=(nc,),
        in_specs=[tile, tile, mat, mat, tile, tile, prow, prow,
                  pl.BlockSpec(ones_h.shape, lambda i: (0, 0))],
        out_specs=tile,
        out_shape=jax.ShapeDtypeStruct((bsz, seq, dr), BF16),
        scratch_shapes=[pltpu.VMEM((bsz * (dr // PAIR), HEAD, PAIR), F32)],
        compiler_params=_params("arbitrary"),
        name="rwkv_b",
    )(by_seq(rp), by_seq(yp), by_seq(mc), by_seq(nm), by_seq(bonus), by_seq(g), ln_g, ln_b, ones_h)
    return out.reshape(bsz * seq, dr)


def _mm_out_body(ya_ref, yb_ref, x_ref, gm_ref, w_ref, g_ref, sh_ref, sc_ref, o_ref, h_ref, *, sub):
    da = ya_ref.shape[1]
    for r0 in range(0, x_ref.shape[0], sub):
        rs = slice(r0, r0 + sub)
        mix = (jnp.dot(ya_ref[rs, :], w_ref[:da, :], preferred_element_type=F32)
               + jnp.dot(yb_ref[rs, :], w_ref[da:, :], preferred_element_type=F32))
        x1 = x_ref[rs, :] + gm_ref[0] * mix
        o_ref[rs, :] = x1
        h_ref[rs, :] = _norm_mod(x1, g_ref[...], sh_ref[0], sc_ref[0]).astype(BF16)


def _mm_out(ya, yb, x2, gm, w, g, sh, sc, seq, tm=512, sub=256):
    m, d = x2.shape
    per_b = seq // tm
    brow = pl.BlockSpec((1, 1, d), lambda i: (i // per_b, 0, 0))
    tile = pl.BlockSpec((tm, d), lambda i: (i, 0))
    return pl.pallas_call(
        functools.partial(_mm_out_body, sub=sub),
        grid=(m // tm,),
        in_specs=[pl.BlockSpec((tm, ya.shape[1]), lambda i: (i, 0)),
                  pl.BlockSpec((tm, yb.shape[1]), lambda i: (i, 0)),
                  tile, brow,
                  pl.BlockSpec(w.shape, lambda i: (0, 0)),
                  pl.BlockSpec((1, d), lambda i: (0, 0)), brow, brow],
        out_specs=[tile, tile],
        out_shape=[jax.ShapeDtypeStruct((m, d), F32), jax.ShapeDtypeStruct((m, d), BF16)],
        compiler_params=_params("parallel"),
        name="mm_out",
    )(ya, yb, x2, gm, w, g, sh, sc)


def _ffn_body(x_ref, h_ref, gf_ref, wg_ref, wu_ref, wd_ref, fg_ref, o_ref, acc_ref):
    f = pl.program_id(1)

    @pl.when(f == 0)
    def _():
        acc_ref[...] = jnp.zeros_like(acc_ref)

    h = h_ref[...]
    gate = jnp.dot(h, wg_ref[...], preferred_element_type=F32)
    up = jnp.dot(h, wu_ref[...], preferred_element_type=F32)
    act = (gate * jax.nn.sigmoid(gate) * up).astype(BF16)
    acc_ref[...] += jnp.dot(act, wd_ref[...], preferred_element_type=F32)

    @pl.when(f == pl.num_programs(1) - 1)
    def _():
        y = x_ref[...] + gf_ref[0] * acc_ref[...]
        o_ref[...] = (y * lax.rsqrt(jnp.mean(y * y, axis=-1, keepdims=True) + RMS_EPS)
                      * fg_ref[...])


def _ffn(x1, h2, gf, w_gu, w_down, fg, seq, tm=512, tf=512):
    m, d = x1.shape
    dff = w_down.shape[0]
    nf = dff // tf
    assert seq % tm == 0 and dff % tf == 0, "row tiles must not straddle sequences"
    per_b = seq // tm
    tile = pl.BlockSpec((tm, d), lambda i, f: (i, 0))
    prow = pl.BlockSpec((1, d), lambda i, f: (0, 0))
    return pl.pallas_call(
        _ffn_body,
        grid=(m // tm, nf),
        in_specs=[tile, tile,
                  pl.BlockSpec((1, 1, d), lambda i, f: (i // per_b, 0, 0)),
                  pl.BlockSpec((d, tf), lambda i, f: (0, f)),
                  pl.BlockSpec((d, tf), lambda i, f: (0, nf + f)),
                  pl.BlockSpec((tf, d), lambda i, f: (f, 0)),
                  prow],
        out_specs=tile,
        out_shape=jax.ShapeDtypeStruct((m, d), F32),
        scratch_shapes=[pltpu.VMEM((tm, d), F32)],
        compiler_params=_params("parallel", "arbitrary"),
        name="ffn",
    )(x1, h2, gf, w_gu, w_gu, w_down, fg)


def _ffn_step(x_ref, h_ref, wg_ref, wu_ref, wd_ref, o_ref, *, gf_ref, fg_ref, acc_ref, step_ref,
              nf, per_b):
    step = step_ref[0]
    step_ref[0] = step + 1
    i, f = step // nf, step % nf

    @pl.when(f == 0)
    def _():
        acc_ref[...] = jnp.zeros_like(acc_ref)

    h = h_ref[...]
    gate = jnp.dot(h, wg_ref[...], preferred_element_type=F32)
    up = jnp.dot(h, wu_ref[...], preferred_element_type=F32)
    act = (gate * jax.nn.sigmoid(gate) * up).astype(BF16)
    acc_ref[...] += jnp.dot(act, wd_ref[...], preferred_element_type=F32)

    @pl.when(f == nf - 1)
    def _():
        y = x_ref[...] + gf_ref[i // per_b] * acc_ref[...]
        o_ref[...] = (y * lax.rsqrt(jnp.mean(y * y, axis=-1, keepdims=True) + RMS_EPS)
                      * fg_ref[...])


def _ffn_nested_body(x_hbm, h_hbm, gf_ref, wgu_hbm, wd_hbm, fg_ref, o_hbm, acc_ref, step_ref, *,
                     tm, tf, nf, per_b):
    nt = x_hbm.shape[0] // tm
    d = x_hbm.shape[1]
    row_out = pl.BlockSpec((tm, d), lambda i, f: (i, 0))
    row_in = pl.BlockSpec((tm, d), lambda i, f: (i, 0),
                          pipeline_mode=pl.Buffered(2, use_lookahead=True))
    step_ref[0] = 0
    pltpu.emit_pipeline(
        functools.partial(_ffn_step, gf_ref=gf_ref, fg_ref=fg_ref, acc_ref=acc_ref,
                          step_ref=step_ref, nf=nf, per_b=per_b),
        grid=(nt, nf),
        in_specs=[row_in, row_in,
                  pl.BlockSpec((d, tf), lambda i, f: (0, f)),
                  pl.BlockSpec((d, tf), lambda i, f: (0, nf + f)),
                  pl.BlockSpec((tf, d), lambda i, f: (f, 0))],
        out_specs=[row_out],
    )(x_hbm, h_hbm, wgu_hbm, wgu_hbm, wd_hbm, o_hbm)


def _ffn_nested(x1, h2, gf, w_gu, w_down, fg, seq, tm=512, tf=512):
    m, d = x1.shape
    dff = w_down.shape[0]
    assert seq % tm == 0 and dff % tf == 0, "row tiles must not straddle sequences"
    hbm = pl.BlockSpec(memory_space=pl.ANY)
    vmem = pl.BlockSpec(memory_space=pltpu.VMEM)
    return pl.pallas_call(
        functools.partial(_ffn_nested_body, tm=tm, tf=tf, nf=dff // tf, per_b=seq // tm),
        in_specs=[hbm, hbm, vmem, hbm, hbm, vmem],
        out_specs=hbm,
        out_shape=jax.ShapeDtypeStruct((m, d), F32),
        scratch_shapes=[pltpu.VMEM((tm, d), F32), pltpu.SMEM((1,), jnp.int32)],
        compiler_params=pltpu.CompilerParams(vmem_limit_bytes=VMEM_LIMIT),
        name="ffn",
    )(x1, h2, gf, w_gu, w_down, fg)


def _pad_cols(w, n):
    return jnp.pad(w, ((0, 0), (0, n - w.shape[1])))


def _pad_rows(w, n):
    return jnp.pad(w, ((0, n - w.shape[0]), (0, 0)))


def kernel(x, c, w_ada, b_ada, norm_mix_g, w_in, conv_w, conv_b, lru_wa, lru_ba, lru_wx, lru_bx, lru_lambda, rwkv_mu, rwkv_w0, rwkv_w2, rwkv_a0, rwkv_a2, rwkv_g2, rwkv_k_k, rwkv_k_a, rwkv_r_k, rwkv_ln_g, rwkv_ln_b, w_out, norm_ffn_g, w_gu, w_down, final_norm_g):
    bsz, seq, d = x.shape
    depth = w_ada.shape[0]
    assert depth == 1, "the closing RMSNorm is fused into the (single) layer's ffn kernel"
    dl = conv_w.shape[2]
    dr = rwkv_w0.shape[1]
    w_lora, a_lora, g_lora = rwkv_w2.shape[1], rwkv_a2.shape[1], rwkv_g2.shape[1]
    wpad, apad = LANE, LANE
    gpad = -(-g_lora // LANE) * LANE
    rkv_col0 = 2 * dl
    lora0 = rkv_col0 + 3 * dr

    x2 = x.reshape(bsz * seq, d)
    for l in range(depth):
        mod = _mod(c, w_ada[l], b_ada[l].reshape(1, -1))
        sh_m, sc_m, g_m, sh_f, sc_f, g_f = [t.reshape(bsz, 1, d) for t in jnp.split(mod, 6, axis=-1)]

        wi = jnp.swapaxes(w_in[l], 0, 1)
        o1, o2 = lora0 + w_lora, lora0 + w_lora + a_lora
        w_lora_p = jnp.concatenate(
            [_pad_rows(wi[lora0:o1], wpad), _pad_rows(wi[o1:o2], apad),
             _pad_rows(wi[o2:], gpad)], axis=0)
        mu = rwkv_mu[l].reshape(1, -1)
        mu_rkv = mu[:, :3 * dr]
        mu_lora = jnp.concatenate(
            [_pad_cols(mu[:, 3 * dr:3 * dr + w_lora], wpad),
             _pad_cols(mu[:, 3 * dr + w_lora:3 * dr + w_lora + a_lora], apad),
             _pad_cols(mu[:, 3 * dr + w_lora + a_lora:], gpad)], axis=1)
        w2p = _pad_rows(rwkv_w2[l], wpad)
        a2p = _pad_rows(rwkv_a2[l], apad)
        g2p = _pad_rows(rwkv_g2[l], gpad)

        h, p_lora = _norm(x2, norm_mix_g[l].reshape(1, d), sh_m, sc_m, w_lora_p, seq)
        p = _mm_in(h, wi, lora0, gelu_tile=1, tn=dl)

        rowv = lambda t: t.reshape(1, dr)
        rp, yp, mc, nm, bonus, gg, y_a, w_out_b, w_gu_b, w_down_b = _rwkv_a(
            p, p_lora, mu_rkv, mu_lora, rowv(rwkv_w0[l]), rowv(rwkv_a0[l]), rowv(rwkv_k_k[l]),
            rowv(rwkv_k_a[l]), rowv(rwkv_r_k[l]), w2p, a2p, g2p,
            conv_w[l], conv_b[l], lru_wa[l].astype(BF16), lru_wx[l].astype(BF16),
            lru_ba[l], lru_bx[l], lru_lambda[l], bsz, seq, rkv_col0,
            cast_ws=(w_out[l], w_gu[l], w_down[l]))
        y_b = _rwkv_b(rp, yp, mc, nm, bonus, gg, rowv(rwkv_ln_g[l]), rowv(rwkv_ln_b[l]), bsz, seq)

        x2, h2 = _mm_out(y_a, y_b, x2, g_m, w_out_b, norm_ffn_g[l].reshape(1, d), sh_f, sc_f, seq)
        x2 = _ffn_nested(x2, h2, g_f, w_gu_b, w_down_b, final_norm_g.reshape(1, d), seq)
    return x2.reshape(bsz, seq, d)
```
